```python
import math
import numpy as np
import jax
import jax.numpy as jnp
from jax import lax

D_MODEL = 1024
BATCH = 32
SEQ = 256
DEPTH = 4
DEC_BATCH = 2
DEC_SEQ = 4096
PAST_LEN = 512

GRID_W = 64
N_EVEN = (DEPTH + 1) // 2
N_ODD = DEPTH // 2
HA = 4
DKA = 128
DVA = 128
CHUNK_A = 64
HB = 4
DKB = 128
DVB = 128
CONV_K = 3
CHUNK_B = 64
HC = 16
KVH = 4
HD = 64
WINDOW = 128
QBLOCK = 128
ROPE_THETA = 10000.0
N_GROUPS = 4
EXPERTS_PER_GROUP = 4
N_EXPERTS = N_GROUPS * EXPERTS_PER_GROUP
TOP_K_IN_GROUP = 2
EXPERT_FF = 256
DN_ALPHA = (2 * DEPTH) ** 0.25
DN_BETA = (8 * DEPTH) ** -0.25
LN_EPS = 1e-5
EVEN_SPLIT_WIDTHS = (HA * DKA, HA * DKA, HA * DVA, HA * DVA, 4 * HA,
                     2 * HB * DKB + HB * DVB, HB * DVB, 2 * HB, 2 * HB)
EVEN_IN = sum(EVEN_SPLIT_WIDTHS)
EVEN_OUT = HA * DVA + HB * DVB

kernel_name = 'bidir_mlstm_deltanet_window_gqa_hmoe_step'


def _split(x, widths):
    return jnp.split(x, np.cumsum(widths)[:-1].tolist(), axis=-1)


def _heads(t, nh):
    b, n, _ = t.shape
    return t.reshape(b, n, nh, -1).transpose(0, 2, 1, 3)


def _merge(t):
    b, h, n, d = t.shape
    return t.transpose(0, 2, 1, 3).reshape(b, n, h * d)


def _layer_norm(x, g, b):
    xf = x.astype(jnp.float32)
    mu = xf.mean(-1, keepdims=True)
    var = jnp.square(xf - mu).mean(-1, keepdims=True)
    return ((xf - mu) * lax.rsqrt(var + LN_EPS) * g + b).astype(x.dtype)


def _head_layer_norm(x, g):
    mu = x.mean(-1, keepdims=True)
    var = jnp.square(x - mu).mean(-1, keepdims=True)
    return (x - mu) * lax.rsqrt(var + LN_EPS) * g


def _head_rms_norm(x, g):
    return x * lax.rsqrt(jnp.square(x).mean(-1, keepdims=True) + LN_EPS) * g


def _l2norm(x):
    return x * lax.rsqrt(jnp.sum(jnp.square(x), -1, keepdims=True) + 1e-6)


def _modulation(cvec, w, b):
    m = jax.nn.silu(cvec) @ w + b
    return tuple(t[:, None, :] for t in jnp.split(m, 6, axis=-1))


def _to_chunks(t, L):
    b, h, n = t.shape[:3]
    return jnp.moveaxis(t.reshape(b, h, n // L, L, *t.shape[3:]), 2, 0)


def _from_chunks(t):
    nc, b, h, l, d = t.shape
    return jnp.moveaxis(t, 0, 2).reshape(b, h, nc * l, d)


def _centered_conv(x, w):
    p = w.shape[0] // 2
    n = x.shape[1]
    xp = jnp.pad(x, ((0, 0), (p, p), (0, 0)))
    return sum(xp[:, j:j + n] * w[j] for j in range(w.shape[0]))


def _rope_1d(x, pos):
    half = x.shape[-1] // 2
    inv = ROPE_THETA ** (-jnp.arange(half, dtype=jnp.float32) / half)
    ang = pos.astype(jnp.float32)[:, None] * inv[None, :]
    cos, sin = jnp.cos(ang), jnp.sin(ang)
    x1, x2 = x[..., :half], x[..., half:]
    return jnp.concatenate([x1 * cos - x2 * sin, x1 * sin + x2 * cos], axis=-1)


def _axial_rope(x, row, col):
    h = x.shape[-1] // 2
    return jnp.concatenate([_rope_1d(x[..., :h], row), _rope_1d(x[..., h:], col)], axis=-1)


def _mlstm_scan(q, k, v, log_i, log_f, c0, n0, m0):
    L = CHUNK_A
    tri = jnp.tril(jnp.ones((L, L), bool))

    def step(carry, xs):
        c, nrm, m = carry
        qj, kj, vj, ij, fj = xs
        bcum = jnp.cumsum(fj, axis=-1)
        d = jnp.where(tri, bcum[..., :, None] - bcum[..., None, :] + ij[..., None, :], -jnp.inf)
        inter = bcum + m[..., None]
        m_t = jnp.maximum(inter, d.max(-1))
        s = jnp.einsum('bhtk,bhsk->bhts', qj, kj) * jnp.exp(d - m_t[..., None])
        w_inter = jnp.exp(inter - m_t)
        num = jnp.einsum('bhts,bhsv->bhtv', s, vj) + w_inter[..., None] * jnp.einsum('bhtk,bhkv->bhtv', qj, c)
        den = s.sum(-1) + w_inter * jnp.einsum('bhtk,bhk->bht', qj, nrm)
        hj = num / jnp.maximum(jnp.abs(den), jnp.exp(-m_t))[..., None]
        g_end = bcum[..., -1:] - bcum + ij
        m_new = jnp.maximum(bcum[..., -1] + m, g_end.max(-1))
        w_end = jnp.exp(g_end - m_new[..., None])
        carry_decay = jnp.exp(bcum[..., -1] + m - m_new)
        c_new = carry_decay[..., None, None] * c + jnp.einsum('bhs,bhsk,bhsv->bhkv', w_end, kj, vj)
        n_new = carry_decay[..., None] * nrm + jnp.einsum('bhs,bhsk->bhk', w_end, kj)
        return (c_new, n_new, m_new), hj

    f32 = jnp.float32
    carry0 = (c0.astype(f32), n0.astype(f32), m0.astype(f32))
    (c, nrm, m), h = lax.scan(step, carry0, tuple(_to_chunks(t, L) for t in (q, k, v, log_i, log_f)))
    return _from_chunks(h), c, nrm, m


def _delta_scan(q, k, v, log_a, beta, s0):
    L = CHUNK_B
    dv = v.shape[-1]
    tri = jnp.tril(jnp.ones((L, L), bool))
    strict = jnp.tril(jnp.ones((L, L), bool), -1)
    eye = jnp.eye(L, dtype=jnp.float32)

    def step(s, xs):
        qj, kj, vj, gj, bj = xs
        G = jnp.cumsum(gj, axis=-1)
        decay = jnp.exp(jnp.where(tri, G[..., :, None] - G[..., None, :], -jnp.inf))
        kb = kj * bj[..., None]
        a_mat = jnp.where(strict, jnp.einsum('bhik,bhjk->bhij', kb, kj) * decay, 0.0)
        rhs = jnp.concatenate([vj * bj[..., None], kb * jnp.exp(G)[..., None]], axis=-1)
        sol = lax.linalg.triangular_solve(eye + a_mat, rhs, left_side=True, lower=True)
        v_new = sol[..., :dv] - jnp.einsum('bhlk,bhkv->bhlv', sol[..., dv:], s)
        o = (jnp.einsum('bhlk,bhkv->bhlv', qj * jnp.exp(G)[..., None], s)
             + jnp.einsum('bhij,bhjv->bhiv', jnp.einsum('bhik,bhjk->bhij', qj, kj) * decay, v_new))
        g_end = G[..., -1:]
        s_new = (jnp.exp(g_end)[..., None] * s
                 + jnp.einsum('bhlk,bhlv->bhkv', kj * jnp.exp(g_end - G)[..., None], v_new))
        return s_new, o

    s, o = lax.scan(step, s0.astype(jnp.float32), tuple(_to_chunks(t, L) for t in (q, k, v, log_a, beta)))
    return _from_chunks(o), s


def _even_mixer(h, w_in, gate_b, mnorm_g, conv_w, a_log, dt_bias, dnorm_g, w_out, mc0, mn0, mm0, ds0):
    f32 = jnp.float32
    b, n, _ = h.shape
    qa, ka, va, oa, ga, qkv_b, zb, bb, ab = _split(h @ w_in, EVEN_SPLIT_WIDTHS)
    flip = lambda t: jnp.flip(t, axis=2)
    q = _heads(qa, HA).astype(f32)
    k = _heads(ka, HA).astype(f32) * DKA ** -0.5
    v = _heads(va, HA).astype(f32)
    g = (ga.astype(f32).reshape(b, n, 2, 2, HA) + gate_b).transpose(2, 3, 0, 4, 1)
    log_i, log_f = g[:, 0], jax.nn.log_sigmoid(g[:, 1])
    hf, cf, nf, mf = _mlstm_scan(q, k, v, log_i[0], log_f[0], mc0[:, 0], mn0[:, 0], mm0[:, 0])
    hb, cb, nb, mb = _mlstm_scan(flip(q), flip(k), flip(v), flip(log_i[1]), flip(log_f[1]),
                                 mc0[:, 1], mn0[:, 1], mm0[:, 1])
    ya = jax.nn.sigmoid(_heads(oa, HA).astype(f32)) * _head_layer_norm(hf + flip(hb), mnorm_g.reshape(HA, 1, DVA))
    qkv = jax.nn.silu(_centered_conv(qkv_b, conv_w).astype(f32))
    qd, kd, vd = _split(qkv, (HB * DKB, HB * DKB, HB * DVB))
    qd = _l2norm(_heads(qd, HB)) * DKB ** -0.5
    kd = _l2norm(_heads(kd, HB))
    vd = _heads(vd, HB)
    beta = jax.nn.sigmoid(bb.astype(f32).reshape(b, n, 2, HB)).transpose(2, 0, 3, 1)
    a_pre = ab.astype(f32).reshape(b, n, 2, HB).transpose(2, 0, 3, 1)
    log_a = -jnp.exp(a_log.astype(f32))[:, None, :, None] * jax.nn.softplus(a_pre + dt_bias.astype(f32)[:, None, :, None])
    of, sf = _delta_scan(qd, kd, vd, log_a[0], beta[0], ds0[:, 0])
    ob, sb = _delta_scan(flip(qd), flip(kd), flip(vd), flip(log_a[1]), flip(beta[1]), ds0[:, 1])
    yb = _head_rms_norm(of + flip(ob), dnorm_g.reshape(HB, 1, DVB)) * jax.nn.silu(_heads(zb, HB).astype(f32))
    y = jnp.concatenate([_merge(ya), _merge(yb)], axis=-1).astype(h.dtype) @ w_out
    states = (jnp.stack([cf, cb], 1), jnp.stack([nf, nb], 1), jnp.stack([mf, mb], 1), jnp.stack([sf, sb], 1))
    return y, states


def _odd_qkv(h, w_qkv):
    q, k, v = _split((h @ w_qkv).astype(jnp.float32), (HC * HD, KVH * HD, KVH * HD))
    return _heads(q, HC) * HD ** -0.5, _heads(k, KVH), _heads(v, KVH)


def _attn_context(h, w_qkv, sink, w_out):
    b, n, _ = h.shape
    grp = HC // KVH
    q, k, v = _odd_qkv(h, w_qkv)
    qg = q.reshape(b, KVH, grp, n, HD)
    s = jnp.einsum('bkgqd,bkcd->bkgqc', qg, k)
    s_sink = jnp.broadcast_to(sink.astype(jnp.float32).reshape(1, KVH, grp, 1, 1), s.shape[:-1] + (1,))
    p = jax.nn.softmax(jnp.concatenate([s, s_sink], axis=-1), axis=-1)[..., :n]
    o = jnp.einsum('bkgqc,bkcd->bkgqd', p, v).reshape(b, HC, n, HD)
    return _merge(o).astype(h.dtype) @ w_out, k, v


def _attn_latent(h, w_qkv, sink, w_out, ck, cv, row, col):
    b, n, _ = h.shape
    nb = n // QBLOCK
    grp = HC // KVH
    f32 = jnp.float32
    q, k, v = _odd_qkv(h, w_qkv)
    q = _axial_rope(q, row, col)
    k = _axial_rope(k, row, col)
    qb = q.reshape(b, KVH, grp, nb, QBLOCK, HD)
    pad = ((0, 0), (0, 0), (QBLOCK, QBLOCK), (0, 0))
    kp = jnp.pad(k, pad).reshape(b, KVH, nb + 2, QBLOCK, HD)
    vp = jnp.pad(v, pad).reshape(b, KVH, nb + 2, QBLOCK, HD)
    band = lambda t: jnp.concatenate([t[:, :, :-2], t[:, :, 1:-1], t[:, :, 2:]], axis=3)
    kw, vw = band(kp), band(vp)
    qpos = jnp.arange(n).reshape(nb, QBLOCK)
    kpos = (jnp.arange(nb) * QBLOCK - QBLOCK)[:, None] + jnp.arange(3 * QBLOCK)[None, :]
    valid = ((jnp.abs(qpos[:, :, None] - kpos[:, None, :]) <= WINDOW)
             & (kpos[:, None, :] >= 0) & (kpos[:, None, :] < n))
    s_loc = jnp.where(valid, jnp.einsum('bkgnqd,bknjd->bkgnqj', qb, kw), -jnp.inf)
    ckf, cvf = ck.astype(f32), cv.astype(f32)
    s_ctx = jnp.einsum('bkgnqd,bkcd->bkgnqc', qb, ckf)
    s_sink = jnp.broadcast_to(sink.astype(f32).reshape(1, KVH, grp, 1, 1, 1), s_ctx.shape[:-1] + (1,))
    p = jax.nn.softmax(jnp.concatenate([s_ctx, s_loc, s_sink], axis=-1), axis=-1)
    lc = ck.shape[2]
    o = (jnp.einsum('bkgnqc,bkcd->bkgnqd', p[..., :lc], cvf)
         + jnp.einsum('bkgnqj,bknjd->bkgnqd', p[..., lc:lc + 3 * QBLOCK], vw))
    o = o.reshape(b, HC, n, HD)
    return _merge(o).astype(h.dtype) @ w_out


def _hier_moe(h, w_grp, b_grp, w_er, b_er, w_gate, w_up, w_down):
    b, n, d = h.shape
    f32 = jnp.float32
    t = h.reshape(b * n, d)
    lg = (t @ w_grp + b_grp).astype(f32)
    g_idx = jnp.argmax(lg, axis=-1)
    g_w = jnp.max(jax.nn.softmax(lg, axis=-1), axis=-1)
    g_hot = jax.nn.one_hot(g_idx, N_GROUPS, dtype=f32)
    le = (jnp.einsum('td,gde->tge', t, w_er) + b_er).astype(f32)
    le = jnp.einsum('tge,tg->te', le, g_hot)
    top_v, top_i = lax.top_k(le, TOP_K_IN_GROUP)
    w_sel = jax.nn.softmax(top_v, axis=-1) * g_w[:, None]
    e_idx = g_idx[:, None] * EXPERTS_PER_GROUP + top_i
    gates = jnp.einsum('tk,tke->te', w_sel, jax.nn.one_hot(e_idx, N_EXPERTS, dtype=f32))
    a = jnp.einsum('td,edf->tef', t, w_gate)
    u = jnp.einsum('td,edf->tef', t, w_up)
    hid = ((jax.nn.silu(a) * u).astype(f32) * gates[:, :, None]).astype(t.dtype)
    y = jnp.einsum('tef,efd->td', hid, w_down)
    return y.reshape(b, n, d)


def setup_inputs(seed: int = 0) -> dict:
    key = jax.random.key(seed)
    ks = iter(jax.random.split(key, 48))
    f32 = jnp.float32
    D = D_MODEL

    def nrm(shape, scale):
        return jax.random.normal(next(ks), shape, f32) * scale

    x_prompt = nrm((BATCH, SEQ, D), 1.0)
    x_sample = nrm((DEC_BATCH, DEC_SEQ, D), 1.0)
    c = nrm((DEC_BATCH, D), 1.0)
    c_ctx = nrm((D,), 1.0)
    state_mlstm_c = nrm((DEC_BATCH, N_EVEN, 2, HA, DKA, DVA), 0.1)
    state_mlstm_n = nrm((DEC_BATCH, N_EVEN, 2, HA, DKA), 0.1)
    state_mlstm_m = nrm((DEC_BATCH, N_EVEN, 2, HA), 0.5)
    state_delta = nrm((DEC_BATCH, N_EVEN, 2, HB, DKB, DVB), 0.1)
    cache_k = nrm((DEC_BATCH, N_ODD, KVH, PAST_LEN, HD), 1.0)
    cache_v = nrm((DEC_BATCH, N_ODD, KVH, PAST_LEN, HD), 1.0)
    w_mod = nrm((DEPTH, D, 6 * D), D ** -0.5)
    b_mod = nrm((DEPTH, 6 * D), 0.02)
    ln_g = 1.0 + nrm((DEPTH, 2, D), 0.02)
    ln_b = nrm((DEPTH, 2, D), 0.02)
    w_in_even = nrm((N_EVEN, D, EVEN_IN), D ** -0.5)
    mlstm_gate_b = jnp.stack([nrm((N_EVEN, 2, HA), 0.1), 3.0 + nrm((N_EVEN, 2, HA), 0.5)], axis=2)
    mlstm_norm_g = 1.0 + nrm((N_EVEN, HA * DVA), 0.02)
    delta_conv_w = nrm((N_EVEN, CONV_K, 2 * HB * DKB + HB * DVB), CONV_K ** -0.5)
    delta_a_log = jnp.log(jax.random.uniform(next(ks), (N_EVEN, 2, HB), f32, 1.0, 16.0))
    dt = jnp.exp(jax.random.uniform(next(ks), (N_EVEN, 2, HB), f32, math.log(1e-3), math.log(1e-1)))
    delta_dt_bias = dt + jnp.log(-jnp.expm1(-dt))
    delta_norm_g = 1.0 + nrm((N_EVEN, HB * DVB), 0.02)
    w_out_even = nrm((N_EVEN, EVEN_OUT, D), EVEN_OUT ** -0.5 * DN_BETA)
    w_qkv_odd = nrm((N_ODD, D, (HC + 2 * KVH) * HD), D ** -0.5)
    attn_sink = nrm((N_ODD, HC), 0.5)
    w_out_odd = nrm((N_ODD, HC * HD, D), (HC * HD) ** -0.5 * DN_BETA)
    w_grp = nrm((DEPTH, D, N_GROUPS), D ** -0.5)
    b_grp = nrm((DEPTH, N_GROUPS), 0.01)
    w_erouter = nrm((DEPTH, N_GROUPS, D, EXPERTS_PER_GROUP), D ** -0.5)
    b_erouter = nrm((DEPTH, N_GROUPS, EXPERTS_PER_GROUP), 0.01)
    w_gate = nrm((DEPTH, N_EXPERTS, D, EXPERT_FF), D ** -0.5)
    w_up = nrm((DEPTH, N_EXPERTS, D, EXPERT_FF), D ** -0.5)
    w_down = nrm((DEPTH, N_EXPERTS, EXPERT_FF, D), EXPERT_FF ** -0.5 * DN_BETA)
    return {'x_prompt': x_prompt, 'x_sample': x_sample, 'c': c, 'c_ctx': c_ctx,
            'state_mlstm_c': state_mlstm_c, 'state_mlstm_n': state_mlstm_n, 'state_mlstm_m': state_mlstm_m,
            'state_delta': state_delta, 'cache_k': cache_k, 'cache_v': cache_v,
            'w_mod': w_mod, 'b_mod': b_mod, 'ln_g': ln_g, 'ln_b': ln_b,
            'w_in_even': w_in_even, 'mlstm_gate_b': mlstm_gate_b, 'mlstm_norm_g': mlstm_norm_g,
            'delta_conv_w': delta_conv_w, 'delta_a_log': delta_a_log, 'delta_dt_bias': delta_dt_bias,
            'delta_norm_g': delta_norm_g, 'w_out_even': w_out_even,
            'w_qkv_odd': w_qkv_odd, 'attn_sink': attn_sink, 'w_out_odd': w_out_odd,
            'w_grp': w_grp, 'b_grp': b_grp, 'w_erouter': w_erouter, 'b_erouter': b_erouter,
            'w_gate': w_gate, 'w_up': w_up, 'w_down': w_down}


def reference(x_prompt, x_sample, c, c_ctx, state_mlstm_c, state_mlstm_n, state_mlstm_m, state_delta,
              cache_k, cache_v, w_mod, b_mod, ln_g, ln_b, w_in_even, mlstm_gate_b, mlstm_norm_g,
              delta_conv_w, delta_a_log, delta_dt_bias, delta_norm_g, w_out_even, w_qkv_odd, attn_sink,
              w_out_odd, w_grp, b_grp, w_erouter, b_erouter, w_gate, w_up, w_down):
    f32 = jnp.float32
    bp = x_prompt.shape[0]
    n_lat = x_sample.shape[1]
    rows = n_lat // GRID_W
    row = jnp.repeat(jnp.arange(rows), GRID_W)
    col = jnp.tile(jnp.arange(GRID_W), rows)
    zero_ctx = (jnp.zeros((bp, 2, HA, DKA, DVA), f32), jnp.zeros((bp, 2, HA, DKA), f32),
                jnp.zeros((bp, 2, HA), f32), jnp.zeros((bp, 2, HB, DKB, DVB), f32))
    xp, xs = x_prompt, x_sample
    out_mc, out_mn, out_mm, out_ds, out_k, out_v = [], [], [], [], [], []
    for l in range(DEPTH):
        mp = _modulation(c_ctx[None, :], w_mod[l], b_mod[l])
        ms = _modulation(c, w_mod[l], b_mod[l])
        hp = xp * (1 + mp[1]) + mp[0]
        hs = xs * (1 + ms[1]) + ms[0]
        if l % 2 == 0:
            e = l // 2
            ew = (w_in_even[e], mlstm_gate_b[e], mlstm_norm_g[e], delta_conv_w[e], delta_a_log[e],
                  delta_dt_bias[e], delta_norm_g[e], w_out_even[e])
            yp, (mc, mn, mm, ds) = _even_mixer(hp, *ew, *zero_ctx)
            ys, _ = _even_mixer(hs, *ew, state_mlstm_c[:, e], state_mlstm_n[:, e], state_mlstm_m[:, e], state_delta[:, e])
            out_mc.append(mc)
            out_mn.append(mn)
            out_mm.append(mm)
            out_ds.append(ds)
        else:
            o = l // 2
            yp, kc, vc = _attn_context(hp, w_qkv_odd[o], attn_sink[o], w_out_odd[o])
            ys = _attn_latent(hs, w_qkv_odd[o], attn_sink[o], w_out_odd[o], cache_k[:, o], cache_v[:, o], row, col)
            out_k.append(kc)
            out_v.append(vc)
        xp = _layer_norm(DN_ALPHA * xp + mp[2] * yp, ln_g[l, 0], ln_b[l, 0])
        xs = _layer_norm(DN_ALPHA * xs + ms[2] * ys, ln_g[l, 0], ln_b[l, 0])
        moe_w = (w_grp[l], b_grp[l], w_erouter[l], b_erouter[l], w_gate[l], w_up[l], w_down[l])
        xp = _layer_norm(DN_ALPHA * xp + mp[5] * _hier_moe(xp * (1 + mp[4]) + mp[3], *moe_w), ln_g[l, 1], ln_b[l, 1])
        xs = _layer_norm(DN_ALPHA * xs + ms[5] * _hier_moe(xs * (1 + ms[4]) + ms[3], *moe_w), ln_g[l, 1], ln_b[l, 1])
    dt = x_prompt.dtype
    return (xp, xs,
            jnp.stack(out_mc, 1).astype(dt), jnp.stack(out_mn, 1).astype(dt), jnp.stack(out_mm, 1).astype(dt),
            jnp.stack(out_ds, 1).astype(dt), jnp.stack(out_k, 1).astype(dt), jnp.stack(out_v, 1).astype(dt))
```

```python
import functools

import jax
import jax.numpy as jnp
from jax import lax
from jax.experimental import pallas as pl
from jax.experimental.pallas import tpu as pltpu

F32 = jnp.float32
BF16 = jnp.bfloat16
HI = lax.Precision.HIGHEST

D = 1024
BATCH = 32
SEQ = 256
DEPTH = 4
DEC_BATCH = 2
DEC_SEQ = 4096
PAST_LEN = 512
GRID_W = 64
N_EVEN = 2
N_ODD = 2
HA = 4
HB = 4
DH = 128
CHUNK = 64
HC = 16
KVH = 4
HD = 64
WINDOW = 128
QBLOCK = 128
ROPE_THETA = 10000.0
N_GROUPS = 4
EPG = 4
N_EXPERTS = 16
EXPERT_FF = 256
DN_ALPHA = (2 * DEPTH) ** 0.25
LN_EPS = 1e-5

NP_ROWS = BATCH * SEQ
NS_ROWS = DEC_BATCH * DEC_SEQ
T = NP_ROWS + NS_ROWS
N_MOD_ROWS = 8
EVEN_W = 4224
QKV_W = (HC + 2 * KVH) * HD

P_CHUNKS = SEQ // CHUNK
S_CHUNKS = DEC_SEQ // CHUNK
P_STEPS = BATCH * P_CHUNKS
S_STEPS = DEC_BATCH * S_CHUNKS
P_BLOCKS = NP_ROWS // CHUNK

VMEM_LIMIT = 48 * 1024 * 1024


def _params(n_axes):
    return pltpu.CompilerParams(dimension_semantics=("arbitrary",) * n_axes,
                                vmem_limit_bytes=VMEM_LIMIT)


def _mm(a, b, prec=None):
    return lax.dot_general(a, b, (((1,), (0,)), ((), ())), precision=prec, preferred_element_type=F32)


def _mm_nt(a, b, prec=None):
    return lax.dot_general(a, b, (((1,), (1,)), ((), ())), precision=prec, preferred_element_type=F32)


def _mm_tn(a, b, prec=None):
    return lax.dot_general(a, b, (((0,), (0,)), ((), ())), precision=prec, preferred_element_type=F32)


def _bf(x):
    return x.astype(BF16)


def _sigmoid(x):
    return 1.0 / (1.0 + jnp.exp(-x))


def _softplus(x):
    return jnp.maximum(x, 0.0) + jnp.log1p(jnp.exp(-jnp.abs(x)))


def _log_sigmoid(x):
    return jnp.minimum(x, 0.0) - jnp.log1p(jnp.exp(-jnp.abs(x)))


def _mod_row(tile, tm):
    npt = NP_ROWS // tm
    per = DEC_SEQ // tm
    return jnp.where(tile < npt, 0, 1 + (tile - npt) // per)


def _mod_spec(layer, chunk, tm):
    def imap(i, *_):
        return ((layer * N_MOD_ROWS + _mod_row(i, tm)) * 6 + chunk, 0, 0)
    return pl.BlockSpec((1, 1, D), imap)


def _modulation_kernel(c_ref, w_ref, b_ref, o_ref):
    x = c_ref[...]
    s = x * _sigmoid(x)
    o_ref[0] = _mm(s, w_ref[0], HI) + b_ref[0]


def _modulation(cvecs, w_mod, b_mod):
    out = pl.pallas_call(
        _modulation_kernel,
        out_shape=jax.ShapeDtypeStruct((DEPTH, N_MOD_ROWS, 6 * D), F32),
        grid=(DEPTH, 6),
        in_specs=[pl.BlockSpec((N_MOD_ROWS, D), lambda l, j: (0, 0)),
                  pl.BlockSpec((1, D, D), lambda l, j: (l, 0, j)),
                  pl.BlockSpec((1, 1, D), lambda l, j: (l * 6 + j, 0, 0))],
        out_specs=pl.BlockSpec((1, N_MOD_ROWS, D), lambda l, j: (l, 0, j)),
        compiler_params=_params(2),
        name="modulation",
    )(cvecs, w_mod, b_mod.reshape(DEPTH * 6, 1, D))
    return out.reshape(DEPTH * N_MOD_ROWS * 6, 1, D)


def _proj_kernel(x_ref, sh_ref, sc_ref, w_ref, o_ref):
    xm = x_ref[...] * (1.0 + sc_ref[0]) + sh_ref[0]
    o_ref[...] = _mm(_bf(xm), w_ref[...])


def _even_proj(x, mods, layer, w):
    tm = 256
    return pl.pallas_call(
        _proj_kernel,
        out_shape=jax.ShapeDtypeStruct((T, EVEN_W), F32),
        grid=(T // tm,),
        in_specs=[pl.BlockSpec((tm, D), lambda i: (i, 0)),
                  _mod_spec(layer, 0, tm), _mod_spec(layer, 1, tm),
                  pl.BlockSpec((D, EVEN_W), lambda i: (0, 0))],
        out_specs=pl.BlockSpec((tm, EVEN_W), lambda i: (i, 0)),
        compiler_params=_params(1),
        name="even_proj",
    )(x, mods, mods, w)


def _qkv_kernel(x_ref, sh_ref, sc_ref, w_ref, cos_ref, sa_ref, sb_ref, o_ref, *, tm):
    i = pl.program_id(0)
    xm = x_ref[...] * (1.0 + sc_ref[0]) + sh_ref[0]
    acc = _mm(_bf(xm), w_ref[...])
    n_q = HC * HD // 128
    n_k = KVH * HD // 128
    is_latent = i >= NP_ROWS // tm
    cos = jnp.where(is_latent, cos_ref[...], 1.0)
    sa = jnp.where(is_latent, sa_ref[...], 0.0)
    sb = jnp.where(is_latent, sb_ref[...], 0.0)
    for g in range(n_q + n_k):
        blk = acc[:, g * 128:(g + 1) * 128]
        if g < n_q:
            blk = blk * (HD ** -0.5)
        rot = blk * cos + pltpu.roll(blk, 112, 1) * sa + pltpu.roll(blk, 16, 1) * sb
        o_ref[:, g * 128:(g + 1) * 128] = rot
    o_ref[:, (n_q + n_k) * 128:] = acc[:, (n_q + n_k) * 128:]


def _rope_tables():
    half = HD // 4
    inv = ROPE_THETA ** (-jnp.arange(half, dtype=F32) / half)
    pos = jnp.arange(DEC_SEQ)
    row = (pos // GRID_W).astype(F32)[:, None] * inv[None, :]
    col = (pos % GRID_W).astype(F32)[:, None] * inv[None, :]
    cos = jnp.concatenate([jnp.cos(row), jnp.cos(row), jnp.cos(col), jnp.cos(col)], axis=-1)
    sin = jnp.concatenate([jnp.sin(row), jnp.sin(row), jnp.sin(col), jnp.sin(col)], axis=-1)
    first = (jnp.arange(HD) % 32) < 16
    sa = jnp.where(first, -sin, 0.0)
    sb = jnp.where(first, 0.0, sin)
    tile2 = lambda t: jnp.concatenate([t, t], axis=-1)
    return tile2(cos), tile2(sa), tile2(sb)


def _odd_proj(x, mods, layer, w, tables):
    tm = 256
    npt = NP_ROWS // tm
    per = DEC_SEQ // tm
    tab_spec = pl.BlockSpec((tm, 128), lambda i: (jnp.where(i < npt, 0, (i - npt) % per), 0))
    return pl.pallas_call(
        functools.partial(_qkv_kernel, tm=tm),
        out_shape=jax.ShapeDtypeStruct((T, QKV_W), F32),
        grid=(T // tm,),
        in_specs=[pl.BlockSpec((tm, D), lambda i: (i, 0)),
                  _mod_spec(layer, 0, tm), _mod_spec(layer, 1, tm),
                  pl.BlockSpec((D, QKV_W), lambda i: (0, 0)),
                  tab_spec, tab_spec, tab_spec],
        out_specs=pl.BlockSpec((tm, QKV_W), lambda i: (i, 0)),
        compiler_params=_params(1),
        name="odd_qkv_proj",
    )(x, mods, mods, w, *tables)


PREP_ROWS = 256


def _delta_prep_kernel(x_ref, prev_ref, next_ref, w_ref, o_ref):
    i = pl.program_id(0)
    part = pl.program_id(1)
    npb = NP_ROWS // PREP_ROWS
    per = DEC_SEQ // PREP_ROWS
    is_latent = i >= npb
    pos = (i - npb) % per
    has_prev = jnp.logical_and(is_latent, pos > 0)
    has_next = jnp.logical_and(is_latent, pos < per - 1)
    x = x_ref[...]
    w = w_ref[...]
    rows = lax.broadcasted_iota(jnp.int32, x.shape, 0)
    prev_row = jnp.where(has_prev, prev_ref[7:8, :], 0.0)
    next_row = jnp.where(has_next, next_ref[0:1, :], 0.0)
    xm1 = jnp.where(rows == 0, prev_row, pltpu.roll(x, 1, 0))
    xp1 = jnp.where(rows == PREP_ROWS - 1, next_row, pltpu.roll(x, PREP_ROWS - 1, 0))
    y = xm1 * w[0:1, :] + x * w[1:2, :] + xp1 * w[2:3, :]
    y = y * _sigmoid(y)
    q_scale = jnp.where(part == 0, DH ** -0.5, 1.0)
    for h in range(4):
        yh = y[:, h * DH:(h + 1) * DH]
        inv = lax.rsqrt(jnp.sum(yh * yh, axis=-1, keepdims=True) + 1e-6)
        scale = jnp.where(part == 2, 1.0, inv * q_scale)
        o_ref[:, h * DH:(h + 1) * DH] = yh * scale


def _delta_prep(proj, conv_w):
    nblk = T // PREP_ROWS
    sub = PREP_ROWS // 8
    last8 = T // 8 - 1
    return pl.pallas_call(
        _delta_prep_kernel,
        out_shape=jax.ShapeDtypeStruct((T, 3 * 512), F32),
        grid=(nblk, 3),
        in_specs=[pl.BlockSpec((PREP_ROWS, 512), lambda i, p: (i, 4 + p)),
                  pl.BlockSpec((8, 512), lambda i, p: (jnp.maximum(i * sub - 1, 0), 4 + p)),
                  pl.BlockSpec((8, 512), lambda i, p: (jnp.minimum((i + 1) * sub, last8), 4 + p)),
                  pl.BlockSpec((3, 512), lambda i, p: (0, p))],
        out_specs=pl.BlockSpec((PREP_ROWS, 512), lambda i, p: (i, p)),
        compiler_params=_params(2),
        name="delta_prep",
    )(proj, proj, proj, conv_w)


def _scan_decode(s):
    is_s = s >= P_STEPS
    sp = s - P_STEPS
    b = jnp.where(is_s, sp // S_CHUNKS, s // P_CHUNKS)
    j = jnp.where(is_s, sp % S_CHUNKS, s % P_CHUNKS)
    nc = jnp.where(is_s, S_CHUNKS, P_CHUNKS)
    base = jnp.where(is_s, P_BLOCKS + b * S_CHUNKS, b * P_CHUNKS)
    return is_s, b, j, nc, base


def _fwd_blk(s):
    _, _, j, _, base = _scan_decode(s)
    return base + j


def _bwd_blk(s):
    _, _, j, nc, base = _scan_decode(s)
    return base + nc - 1 - j


def _state_in_idx(s):
    is_s, b, _, _, _ = _scan_decode(s)
    return jnp.where(is_s, b, 0)


def _state_out_idx(s):
    is_s, b, _, _, _ = _scan_decode(s)
    return jnp.where(is_s, BATCH - 1, b)


def _tri_masks(direction):
    row = lax.broadcasted_iota(jnp.int32, (CHUNK, CHUNK), 0)
    col = lax.broadcasted_iota(jnp.int32, (CHUNK, CHUNK), 1)
    if direction == 0:
        return col <= row, col < row
    return col >= row, col > row


def _mlstm_kernel(qf_ref, kf_ref, vf_ref, qb_ref, kb_ref, vb_ref, gcf_ref, gcb_ref, grf_ref, grb_ref,
                  brow_ref, bcol_ref, c0_ref, n0_ref, m0_ref,
                  hf_ref, hb_ref, c_out, n_out, m_out, c_s, n_s, m_s):
    s = pl.program_id(0)
    is_s, _, j, nc, _ = _scan_decode(s)

    @pl.when(jnp.logical_and(j == 0, jnp.logical_not(is_s)))
    def _():
        c_s[...] = jnp.zeros_like(c_s)
        n_s[...] = jnp.zeros_like(n_s)
        m_s[...] = jnp.zeros_like(m_s)

    @pl.when(jnp.logical_and(j == 0, is_s))
    def _():
        for d in range(2):
            for h in range(HA):
                c_s[d * HA + h] = c0_ref[0, 0, d, h]
            n_s[d * HA:(d + 1) * HA, :] = n0_ref[0, 0, d]
        m_s[...] = m0_ref[0, 0]

    ins = ((qf_ref, kf_ref, vf_ref, gcf_ref, grf_ref, hf_ref),
           (qb_ref, kb_ref, vb_ref, gcb_ref, grb_ref, hb_ref))
    for d in range(2):
        q_ref, k_ref, v_ref, gc_ref, gr_ref, h_ref = ins[d]
        incl, _ = _tri_masks(d)
        tri = incl.astype(F32)
        end = CHUNK - 1 if d == 0 else 0
        gc = gc_ref[...] + brow_ref[...]
        gr = gr_ref[0] + bcol_ref[...]
        cs_col = _mm(tri, _log_sigmoid(gc), HI)
        cs_row = _mm_nt(_log_sigmoid(gr), tri, HI)
        for h in range(HA):
            c = d * HA + h
            ci = d * 8 + h
            cf = d * 8 + 4 + h
            i_col = gc[:, ci:ci + 1]
            i_row = gr[ci:ci + 1, :]
            b_col = cs_col[:, cf:cf + 1]
            b_row = cs_row[cf:cf + 1, :]
            b_last = b_col[end:end + 1, :]
            m = m_s[c:c + 1, 0:1]
            dmat = jnp.where(incl, b_col - b_row + i_row, -jnp.inf)
            inter = b_col + m
            m_t = jnp.maximum(inter, jnp.max(dmat, axis=1, keepdims=True))
            q = q_ref[:, h * DH:(h + 1) * DH]
            k = k_ref[:, h * DH:(h + 1) * DH] * (DH ** -0.5)
            v = v_ref[:, h * DH:(h + 1) * DH]
            qb_, kb_, vb_ = _bf(q), _bf(k), _bf(v)
            sc = _mm_nt(qb_, kb_) * jnp.exp(dmat - m_t)
            w_inter = jnp.exp(inter - m_t)
            c_mat = c_s[c]
            nrm = n_s[c:c + 1, :]
            num = _mm(_bf(sc), vb_) + w_inter * _mm(qb_, _bf(c_mat))
            den = jnp.sum(sc, axis=1, keepdims=True) + w_inter * jnp.sum(q * nrm, axis=1, keepdims=True)
            h_ref[:, h * DH:(h + 1) * DH] = num / jnp.maximum(jnp.abs(den), jnp.exp(-m_t))
            g_end = b_last - b_col + i_col
            m_new = jnp.maximum(b_last + m, jnp.max(g_end, axis=0, keepdims=True))
            kw = k * jnp.exp(g_end - m_new)
            decay = jnp.exp(b_last + m - m_new)
            c_s[c] = decay * c_mat + _mm_tn(_bf(kw), vb_)
            n_s[c:c + 1, :] = decay * nrm + jnp.sum(kw, axis=0, keepdims=True)
            m_s[c:c + 1, :] = jnp.broadcast_to(m_new, (1, 128))

    @pl.when(jnp.logical_and(j == nc - 1, jnp.logical_not(is_s)))
    def _():
        for d in range(2):
            for h in range(HA):
                c_out[0, d, h] = c_s[d * HA + h]
            n_out[0, d] = n_s[d * HA:(d + 1) * HA, :]
        m_out[0] = m_s[...]


def _scan_specs(width_blocks):
    fwd = [pl.BlockSpec((CHUNK, 512), (lambda s, p=p: (_fwd_blk(s), p))) for p in width_blocks]
    bwd = [pl.BlockSpec((CHUNK, 512), (lambda s, p=p: (_bwd_blk(s), p))) for p in width_blocks]
    return fwd + bwd


def _gate_specs(col_block):
    return [pl.BlockSpec((CHUNK, 128), lambda s: (_fwd_blk(s), col_block)),
            pl.BlockSpec((CHUNK, 128), lambda s: (_bwd_blk(s), col_block)),
            pl.BlockSpec((1, 32, CHUNK), lambda s: (_fwd_blk(s), 0, 0)),
            pl.BlockSpec((1, 32, CHUNK), lambda s: (_bwd_blk(s), 0, 0))]


def _mlstm_scan(proj, gates_t, bias_row, bias_col, c0, n0, m0, e):
    const2 = lambda s: (0, 0)
    in_specs = (_scan_specs((0, 1, 2)) + _gate_specs(EVEN_W // 128 - 1)
                + [pl.BlockSpec((1, 128), const2), pl.BlockSpec((32, 1), const2),
                   pl.BlockSpec((1, 1, 2, HA, DH, DH), lambda s: (_state_in_idx(s), e, 0, 0, 0, 0)),
                   pl.BlockSpec((1, 1, 2, HA, DH), lambda s: (_state_in_idx(s), e, 0, 0, 0)),
                   pl.BlockSpec((1, 1, 2 * HA, 128), lambda s: (_state_in_idx(s), e, 0, 0))])
    out_shape = (jax.ShapeDtypeStruct((T, HA * DH), F32), jax.ShapeDtypeStruct((T, HA * DH), F32),
                 jax.ShapeDtypeStruct((BATCH, 2, HA, DH, DH), F32),
                 jax.ShapeDtypeStruct((BATCH, 2, HA, DH), F32),
                 jax.ShapeDtypeStruct((BATCH, 2 * HA, 128), F32))
    out_specs = (pl.BlockSpec((CHUNK, 512), lambda s: (_fwd_blk(s), 0)),
                 pl.BlockSpec((CHUNK, 512), lambda s: (_bwd_blk(s), 0)),
                 pl.BlockSpec((1, 2, HA, DH, DH), lambda s: (_state_out_idx(s), 0, 0, 0, 0)),
                 pl.BlockSpec((1, 2, HA, DH), lambda s: (_state_out_idx(s), 0, 0, 0)),
                 pl.BlockSpec((1, 2 * HA, 128), lambda s: (_state_out_idx(s), 0, 0)))
    return pl.pallas_call(
        _mlstm_kernel,
        out_shape=out_shape,
        grid=(P_STEPS + S_STEPS,),
        in_specs=in_specs,
        out_specs=out_specs,
        scratch_shapes=[pltpu.VMEM((2 * HA, DH, DH), F32), pltpu.VMEM((2 * HA, DH), F32),
                        pltpu.VMEM((2 * HA, 128), F32)],
        compiler_params=_params(1),
        name="mlstm_scan",
    )(proj, proj, proj, proj, proj, proj, proj, proj, gates_t, gates_t, bias_row, bias_col, c0, n0, m0)


def _delta_kernel(qf_ref, kf_ref, vf_ref, qb_ref, kb_ref, vb_ref, gcf_ref, gcb_ref, grf_ref, grb_ref,
                  dtrow_ref, dtcol_ref, narow_ref, nacol_ref, s0_ref,
                  of_ref, ob_ref, s_out, s_s):
    s = pl.program_id(0)
    is_s, _, j, nc, _ = _scan_decode(s)

    @pl.when(jnp.logical_and(j == 0, jnp.logical_not(is_s)))
    def _():
        s_s[...] = jnp.zeros_like(s_s)

    @pl.when(jnp.logical_and(j == 0, is_s))
    def _():
        for d in range(2):
            for h in range(HB):
                s_s[d * HB + h] = s0_ref[0, 0, d, h]

    row = lax.broadcasted_iota(jnp.int32, (CHUNK, CHUNK), 0)
    col = lax.broadcasted_iota(jnp.int32, (CHUNK, CHUNK), 1)
    eye = (row == col).astype(F32)
    ins = ((qf_ref, kf_ref, vf_ref, gcf_ref, grf_ref, of_ref),
           (qb_ref, kb_ref, vb_ref, gcb_ref, grb_ref, ob_ref))
    for d in range(2):
        q_ref, k_ref, v_ref, gc_ref, gr_ref, o_ref = ins[d]
        incl, strict = _tri_masks(d)
        tri = incl.astype(F32)
        end = CHUNK - 1 if d == 0 else 0
        xc = gc_ref[...]
        xr = gr_ref[0]
        beta_c = _sigmoid(xc)
        la_c = narow_ref[...] * _softplus(xc + dtrow_ref[...])
        la_r = nacol_ref[...] * _softplus(xr + dtcol_ref[...])
        g_c = _mm(tri, la_c, HI)
        g_r = _mm_nt(la_r, tri, HI)
        for h in range(HB):
            c = d * HB + h
            ib = 16 + d * HB + h
            ia = 24 + d * HB + h
            gcol = g_c[:, ia:ia + 1]
            grow = g_r[ia:ia + 1, :]
            bc = beta_c[:, ib:ib + 1]
            decay = jnp.exp(jnp.where(incl, gcol - grow, -jnp.inf))
            q = q_ref[:, h * DH:(h + 1) * DH]
            k = k_ref[:, h * DH:(h + 1) * DH]
            v = v_ref[:, h * DH:(h + 1) * DH]
            kb = k * bc
            kbf = _bf(k)
            a_mat = jnp.where(strict, _mm_nt(_bf(kb), kbf) * decay, 0.0)
            p = -a_mat
            t_inv = eye + p
            for _ in range(5):
                p = _mm(p, p, HI)
                t_inv = t_inv + _mm(t_inv, p, HI)
            eg = jnp.exp(gcol)
            rhs = jnp.concatenate([v * bc, kb * eg], axis=1)
            sol = _mm(t_inv, rhs, HI)
            s_mat = s_s[c]
            sbf = _bf(s_mat)
            v_new = sol[:, :DH] - _mm(_bf(sol[:, DH:]), sbf)
            vnb = _bf(v_new)
            o_ref[:, h * DH:(h + 1) * DH] = (_mm(_bf(q * eg), sbf)
                                             + _mm(_bf(_mm_nt(_bf(q), kbf) * decay), vnb))
            g_last = gcol[end:end + 1, :]
            s_s[c] = jnp.exp(g_last) * s_mat + _mm_tn(_bf(k * jnp.exp(g_last - gcol)), vnb)

    @pl.when(jnp.logical_and(j == nc - 1, jnp.logical_not(is_s)))
    def _():
        for d in range(2):
            for h in range(HB):
                s_out[0, d, h] = s_s[d * HB + h]


def _delta_scan(qkv, proj, gates_t, dt_row, dt_col, na_row, na_col, s0, e):
    const2 = lambda s: (0, 0)
    in_specs = (_scan_specs((0, 1, 2)) + _gate_specs(EVEN_W // 128 - 1)
                + [pl.BlockSpec((1, 128), const2), pl.BlockSpec((32, 1), const2),
                   pl.BlockSpec((1, 128), const2), pl.BlockSpec((32, 1), const2),
                   pl.BlockSpec((1, 1, 2, HB, DH, DH), lambda s: (_state_in_idx(s), e, 0, 0, 0, 0))])
    out_shape = (jax.ShapeDtypeStruct((T, HB * DH), F32), jax.ShapeDtypeStruct((T, HB * DH), F32),
                 jax.ShapeDtypeStruct((BATCH, 2, HB, DH, DH), F32))
    out_specs = (pl.BlockSpec((CHUNK, 512), lambda s: (_fwd_blk(s), 0)),
                 pl.BlockSpec((CHUNK, 512), lambda s: (_bwd_blk(s), 0)),
                 pl.BlockSpec((1, 2, HB, DH, DH), lambda s: (_state_out_idx(s), 0, 0, 0, 0)))
    return pl.pallas_call(
        _delta_kernel,
        out_shape=out_shape,
        grid=(P_STEPS + S_STEPS,),
        in_specs=in_specs,
        out_specs=out_specs,
        scratch_shapes=[pltpu.VMEM((2 * HB, DH, DH), F32)],
        compiler_params=_params(1),
        name="delta_scan",
    )(qkv, qkv, qkv, qkv, qkv, qkv, proj, proj, gates_t, gates_t, dt_row, dt_col, na_row, na_col, s0)


def _residual_ln(x, gate, y, g, b):
    r = DN_ALPHA * x + gate * y
    mu = jnp.mean(r, axis=-1, keepdims=True)
    var = jnp.mean(jnp.square(r - mu), axis=-1, keepdims=True)
    return (r - mu) * lax.rsqrt(var + LN_EPS) * g + b


def _even_out_kernel(hf_ref, hb_ref, oa_ref, of_ref, ob_ref, zb_ref, mg_ref, dg_ref,
                     w_ref, x_ref, gate_ref, g_ref, b_ref, o_ref):
    parts = []
    for h in range(HA):
        sl = slice(h * DH, (h + 1) * DH)
        hh = hf_ref[:, sl] + hb_ref[:, sl]
        mu = jnp.mean(hh, axis=-1, keepdims=True)
        var = jnp.mean(jnp.square(hh - mu), axis=-1, keepdims=True)
        parts.append(_sigmoid(oa_ref[:, sl]) * ((hh - mu) * lax.rsqrt(var + LN_EPS) * mg_ref[:, sl]))
    for h in range(HB):
        sl = slice(h * DH, (h + 1) * DH)
        oo = of_ref[:, sl] + ob_ref[:, sl]
        z = zb_ref[:, sl]
        nrm = oo * lax.rsqrt(jnp.mean(jnp.square(oo), axis=-1, keepdims=True) + LN_EPS) * dg_ref[:, sl]
        parts.append(nrm * (z * _sigmoid(z)))
    a = jnp.concatenate(parts, axis=1)
    y = _mm(_bf(a), w_ref[...])
    o_ref[...] = _residual_ln(x_ref[...], gate_ref[0], y, g_ref[...], b_ref[...])


def _even_out(hf, hb, of, ob, proj, mg, dg, w, x, mods, layer, ln_g, ln_b):
    tm = 256
    row512 = lambda i: (i, 0)
    const2 = lambda i: (0, 0)
    return pl.pallas_call(
        _even_out_kernel,
        out_shape=jax.ShapeDtypeStruct((T, D), F32),
        grid=(T // tm,),
        in_specs=[pl.BlockSpec((tm, 512), row512), pl.BlockSpec((tm, 512), row512),
                  pl.BlockSpec((tm, 512), lambda i: (i, 3)),
                  pl.BlockSpec((tm, 512), row512), pl.BlockSpec((tm, 512), row512),
                  pl.BlockSpec((tm, 512), lambda i: (i, 7)),
                  pl.BlockSpec((1, 512), const2), pl.BlockSpec((1, 512), const2),
                  pl.BlockSpec((D, D), const2),
                  pl.BlockSpec((tm, D), row512),
                  _mod_spec(layer, 2, tm),
                  pl.BlockSpec((1, D), const2), pl.BlockSpec((1, D), const2)],
        out_specs=pl.BlockSpec((tm, D), row512),
        compiler_params=_params(1),
        name="even_out_ln",
    )(hf, hb, proj, of, ob, proj, mg, dg, w, x, mods, ln_g, ln_b)


def _odd_out_kernel(a_ref, w_ref, x_ref, gate_ref, g_ref, b_ref, o_ref):
    y = _mm(_bf(a_ref[...]), w_ref[...])
    o_ref[...] = _residual_ln(x_ref[...], gate_ref[0], y, g_ref[...], b_ref[...])


def _odd_out(a, w, x, mods, layer, ln_g, ln_b):
    tm = 256
    row = lambda i: (i, 0)
    const2 = lambda i: (0, 0)
    return pl.pallas_call(
        _odd_out_kernel,
        out_shape=jax.ShapeDtypeStruct((T, D), F32),
        grid=(T // tm,),
        in_specs=[pl.BlockSpec((tm, D), row), pl.BlockSpec((D, D), const2), pl.BlockSpec((tm, D), row),
                  _mod_spec(layer, 2, tm), pl.BlockSpec((1, D), const2), pl.BlockSpec((1, D), const2)],
        out_specs=pl.BlockSpec((tm, D), row),
        compiler_params=_params(1),
        name="odd_out_ln",
    )(a, w, x, mods, ln_g, ln_b)


def _attn_ctx_kernel(qkv_ref, sink_ref, o_ref):
    for h in range(HC):
        kv = h // (HC // KVH)
        q = _bf(qkv_ref[:, h * HD:(h + 1) * HD])
        k = _bf(qkv_ref[:, HC * HD + kv * HD:HC * HD + (kv + 1) * HD])
        v = _bf(qkv_ref[:, (HC + KVH) * HD + kv * HD:(HC + KVH) * HD + (kv + 1) * HD])
        sink = sink_ref[h]
        sc = _mm_nt(q, k)
        m = jnp.maximum(jnp.max(sc, axis=1, keepdims=True), sink)
        p = jnp.exp(sc - m)
        den = jnp.sum(p, axis=1, keepdims=True) + jnp.exp(sink - m)
        o_ref[:, h * HD:(h + 1) * HD] = _mm(_bf(p), v) / den


def _attn_context(qkv, sink):
    return pl.pallas_call(
        _attn_ctx_kernel,
        out_shape=jax.ShapeDtypeStruct((T, HC * HD), F32),
        grid=(BATCH,),
        in_specs=[pl.BlockSpec((SEQ, QKV_W), lambda b: (b, 0)),
                  pl.BlockSpec(memory_space=pltpu.SMEM)],
        out_specs=pl.BlockSpec((SEQ, HC * HD), lambda b: (b, 0)),
        compiler_params=_params(1),
        name="attn_context",
    )(qkv, sink)


def _attn_lat_kernel(prev_out_ref, q_ref, kp_ref, kc_ref, kn_ref, vp_ref, vc_ref, vn_ref, ck_ref, cv_ref,
                     sink_ref, o_ref):
    del prev_out_ref
    j = pl.program_id(1)
    nb = DEC_SEQ // QBLOCK
    r = lax.broadcasted_iota(jnp.int32, (QBLOCK, QBLOCK), 0)
    c = lax.broadcasted_iota(jnp.int32, (QBLOCK, QBLOCK), 1)
    ok_prev = jnp.logical_and(c >= r, j >= 1)
    ok_next = jnp.logical_and(c <= r, j <= nb - 2)
    for h in range(HC):
        kv = h // (HC // KVH)
        ks = slice(kv * HD, (kv + 1) * HD)
        q = _bf(q_ref[:, h * HD:(h + 1) * HD])
        ck = _bf(ck_ref[0, 0, kv])
        cv = _bf(cv_ref[0, 0, kv])
        sink = sink_ref[h]
        s_ctx = _mm_nt(q, ck)
        s_p = jnp.where(ok_prev, _mm_nt(q, _bf(kp_ref[:, ks])), -jnp.inf)
        s_c = _mm_nt(q, _bf(kc_ref[:, ks]))
        s_n = jnp.where(ok_next, _mm_nt(q, _bf(kn_ref[:, ks])), -jnp.inf)
        m = jnp.maximum(jnp.maximum(jnp.max(s_ctx, axis=1, keepdims=True), jnp.max(s_c, axis=1, keepdims=True)),
                        jnp.maximum(jnp.max(s_p, axis=1, keepdims=True), jnp.max(s_n, axis=1, keepdims=True)))
        m = jnp.maximum(m, sink)
        p_ctx = jnp.exp(s_ctx - m)
        p_p = jnp.exp(s_p - m)
        p_c = jnp.exp(s_c - m)
        p_n = jnp.exp(s_n - m)
        den = (jnp.sum(p_ctx, axis=1, keepdims=True) + jnp.sum(p_p, axis=1, keepdims=True)
               + jnp.sum(p_c, axis=1, keepdims=True) + jnp.sum(p_n, axis=1, keepdims=True) + jnp.exp(sink - m))
        num = (_mm(_bf(p_ctx), cv) + _mm(_bf(p_p), _bf(vp_ref[:, ks]))
               + _mm(_bf(p_c), _bf(vc_ref[:, ks])) + _mm(_bf(p_n), _bf(vn_ref[:, ks])))
        o_ref[:, h * HD:(h + 1) * HD] = num / den


def _attn_latent(ctx_out, qkv, cache_k, cache_v, sink, o):
    nb = DEC_SEQ // QBLOCK
    base = NP_ROWS // QBLOCK
    blk = lambda b, j: base + b * nb + j
    prev = lambda b, j: base + b * nb + jnp.maximum(j - 1, 0)
    nxt = lambda b, j: base + b * nb + jnp.minimum(j + 1, nb - 1)
    kcol, vcol = HC * HD // 256, HC * HD // 256 + 1
    cache_spec = pl.BlockSpec((1, 1, KVH, PAST_LEN, HD), lambda b, j: (b, o, 0, 0, 0))
    return pl.pallas_call(
        _attn_lat_kernel,
        out_shape=jax.ShapeDtypeStruct((T, HC * HD), F32),
        grid=(DEC_BATCH, nb),
        in_specs=[pl.BlockSpec(memory_space=pl.ANY),
                  pl.BlockSpec((QBLOCK, HC * HD), lambda b, j: (blk(b, j), 0)),
                  pl.BlockSpec((QBLOCK, 256), lambda b, j: (prev(b, j), kcol)),
                  pl.BlockSpec((QBLOCK, 256), lambda b, j: (blk(b, j), kcol)),
                  pl.BlockSpec((QBLOCK, 256), lambda b, j: (nxt(b, j), kcol)),
                  pl.BlockSpec((QBLOCK, 256), lambda b, j: (prev(b, j), vcol)),
                  pl.BlockSpec((QBLOCK, 256), lambda b, j: (blk(b, j), vcol)),
                  pl.BlockSpec((QBLOCK, 256), lambda b, j: (nxt(b, j), vcol)),
                  cache_spec, cache_spec,
                  pl.BlockSpec(memory_space=pltpu.SMEM)],
        out_specs=pl.BlockSpec((QBLOCK, HC * HD), lambda b, j: (blk(b, j), 0)),
        input_output_aliases={0: 0},
        compiler_params=_params(2),
        name="attn_latent",
    )(ctx_out, qkv, qkv, qkv, qkv, qkv, qkv, qkv, cache_k, cache_v, sink)


def _router_kernel(x_ref, sh_ref, sc_ref, w_ref, b_ref, o_ref):
    xm = x_ref[...] * (1.0 + sc_ref[0]) + sh_ref[0]
    lg = _mm(xm, w_ref[...], HI) + b_ref[...]
    lane = lax.broadcasted_iota(jnp.int32, lg.shape, 1)
    neg = -jnp.inf
    big = 1 << 20
    is_grp = jnp.logical_and(lane >= N_EXPERTS, lane < N_EXPERTS + N_GROUPS)
    mg = jnp.max(jnp.where(is_grp, lg, neg), axis=1, keepdims=True)
    g_lane = jnp.min(jnp.where(jnp.logical_and(is_grp, lg == mg), lane, big), axis=1, keepdims=True)
    g_w = 1.0 / jnp.sum(jnp.where(is_grp, jnp.exp(lg - mg), 0.0), axis=1, keepdims=True)
    g_idx = g_lane - N_EXPERTS
    in_grp = jnp.logical_and(lane >= g_idx * EPG, lane < (g_idx + 1) * EPG)
    v1 = jnp.max(jnp.where(in_grp, lg, neg), axis=1, keepdims=True)
    i1 = jnp.min(jnp.where(jnp.logical_and(in_grp, lg == v1), lane, big), axis=1, keepdims=True)
    rest = jnp.logical_and(in_grp, lane != i1)
    v2 = jnp.max(jnp.where(rest, lg, neg), axis=1, keepdims=True)
    i2 = jnp.min(jnp.where(jnp.logical_and(rest, lg == v2), lane, big), axis=1, keepdims=True)
    e2 = jnp.exp(v2 - v1)
    p1 = 1.0 / (1.0 + e2)
    p2 = e2 / (1.0 + e2)
    o_ref[...] = jnp.where(lane == i1, p1 * g_w, jnp.where(lane == i2, p2 * g_w, 0.0))


def _router(x, mods, layer, w, b):
    tm = 512
    return pl.pallas_call(
        _router_kernel,
        out_shape=jax.ShapeDtypeStruct((T, 128), F32),
        grid=(T // tm,),
        in_specs=[pl.BlockSpec((tm, D), lambda i: (i, 0)),
                  _mod_spec(layer, 3, tm), _mod_spec(layer, 4, tm),
                  pl.BlockSpec((D, 128), lambda i: (0, 0)), pl.BlockSpec((1, 128), lambda i: (0, 0))],
        out_specs=pl.BlockSpec((tm, 128), lambda i: (i, 0)),
        compiler_params=_params(1),
        name="moe_router",
    )(x, mods, mods, w, b)


def _moe_kernel(x_ref, sh_ref, sc_ref, gates_ref, wg_ref, wu_ref, wd_ref, gate_ref, g_ref, b_ref, o_ref,
                xm_s, acc_s):
    e = pl.program_id(1)

    @pl.when(e == 0)
    def _():
        xm_s[...] = _bf(x_ref[...] * (1.0 + sc_ref[0]) + sh_ref[0])
        acc_s[...] = jnp.zeros_like(acc_s)

    xm = xm_s[...]
    a = _mm(xm, wg_ref[0])
    u = _mm(xm, wu_ref[0])
    gates = gates_ref[...]
    lane = lax.broadcasted_iota(jnp.int32, gates.shape, 1)
    ge = jnp.sum(jnp.where(lane == e, gates, 0.0), axis=1, keepdims=True)
    hid = (a * _sigmoid(a)) * u * ge
    acc_s[...] += _mm(_bf(hid), wd_ref[0])

    @pl.when(e == N_EXPERTS - 1)
    def _():
        o_ref[...] = _residual_ln(x_ref[...], gate_ref[0], acc_s[...], g_ref[...], b_ref[...])


def _moe(x, mods, layer, gates, wg, wu, wd, ln_g, ln_b):
    tm = 512
    row = lambda i, e: (i, 0)
    const2 = lambda i, e: (0, 0)
    return pl.pallas_call(
        _moe_kernel,
        out_shape=jax.ShapeDtypeStruct((T, D), F32),
        grid=(T // tm, N_EXPERTS),
        in_specs=[pl.BlockSpec((tm, D), row), _mod_spec(layer, 3, tm), _mod_spec(layer, 4, tm),
                  pl.BlockSpec((tm, 128), row),
                  pl.BlockSpec((1, D, EXPERT_FF), lambda i, e: (e, 0, 0)),
                  pl.BlockSpec((1, D, EXPERT_FF), lambda i, e: (e, 0, 0)),
                  pl.BlockSpec((1, EXPERT_FF, D), lambda i, e: (e, 0, 0)),
                  _mod_spec(layer, 5, tm), pl.BlockSpec((1, D), const2), pl.BlockSpec((1, D), const2)],
        out_specs=pl.BlockSpec((tm, D), row),
        scratch_shapes=[pltpu.VMEM((tm, D), BF16), pltpu.VMEM((tm, D), F32)],
        compiler_params=_params(2),
        name="moe_experts",
    )(x, mods, mods, gates, wg, wu, wd, mods, ln_g, ln_b)


def _permute_even_w(w):
    a_end = 4 * HA * DH
    g_end = a_end + 4 * HA
    c_end = g_end + 3 * HB * DH
    z_end = c_end + HB * DH
    small = jnp.concatenate([w[:, a_end:g_end], w[:, z_end:]], axis=1)
    pad = jnp.zeros((w.shape[0], 128 - small.shape[1]), w.dtype)
    return jnp.concatenate([w[:, :a_end], w[:, g_end:c_end], w[:, c_end:z_end], small, pad], axis=1)


def _lane_row(vals, offset):
    return jnp.zeros((1, 128), F32).at[0, offset:offset + vals.shape[0]].set(vals.astype(F32))


def _sub_col(vals, offset, fill=0.0):
    return jnp.full((32, 1), fill, F32).at[offset:offset + vals.shape[0], 0].set(vals.astype(F32))


def kernel(x_prompt, x_sample, c, c_ctx, state_mlstm_c, state_mlstm_n, state_mlstm_m, state_delta, cache_k, cache_v, w_mod, b_mod, ln_g, ln_b, w_in_even, mlstm_gate_b, mlstm_norm_g, delta_conv_w, delta_a_log, delta_dt_bias, delta_norm_g, w_out_even, w_qkv_odd, attn_sink, w_out_odd, w_grp, b_grp, w_erouter, b_erouter, w_gate, w_up, w_down):
    x = jnp.concatenate([x_prompt.reshape(NP_ROWS, D), x_sample.reshape(NS_ROWS, D)], axis=0)
    cvecs = jnp.concatenate([c_ctx[None, :], c, jnp.zeros((N_MOD_ROWS - 1 - DEC_BATCH, D), F32)], axis=0)
    mods = _modulation(cvecs, w_mod, b_mod)
    tables = _rope_tables()
    m0_all = jnp.broadcast_to(state_mlstm_m.reshape(DEC_BATCH, N_EVEN, 2 * HA, 1), (DEC_BATCH, N_EVEN, 2 * HA, 128))

    out_mc, out_mn, out_mm, out_ds, out_k, out_v = [], [], [], [], [], []
    for l in range(DEPTH):
        if l % 2 == 0:
            e = l // 2
            proj = _even_proj(x, mods, l, _bf(_permute_even_w(w_in_even[e])))
            gates_t = proj[:, EVEN_W - 128:EVEN_W - 96].reshape(T // CHUNK, CHUNK, 32).transpose(0, 2, 1)
            gb = mlstm_gate_b[e].reshape(-1)
            hf, hb, mc, mn, mm = _mlstm_scan(proj, gates_t, _lane_row(gb, 0), _sub_col(gb, 0),
                                             state_mlstm_c, state_mlstm_n, m0_all, e)
            qkv = _delta_prep(proj, delta_conv_w[e])
            dtb = delta_dt_bias[e].reshape(-1)
            nea = -jnp.exp(delta_a_log[e].astype(F32)).reshape(-1)
            of, ob, ds = _delta_scan(qkv, proj, gates_t, _lane_row(dtb, 24), _sub_col(dtb, 24),
                                     _lane_row(nea, 24), _sub_col(nea, 24), state_delta, e)
            x = _even_out(hf, hb, of, ob, proj, mlstm_norm_g[e][None, :], delta_norm_g[e][None, :],
                          _bf(w_out_even[e]), x, mods, l, ln_g[l, 0][None, :], ln_b[l, 0][None, :])
            out_mc.append(mc)
            out_mn.append(mn)
            out_mm.append(mm[:, :, 0].reshape(BATCH, 2, HA))
            out_ds.append(ds)
        else:
            o = l // 2
            qkv = _odd_proj(x, mods, l, _bf(w_qkv_odd[o]), tables)
            a = _attn_context(qkv, attn_sink[o])
            a = _attn_latent(a, qkv, cache_k, cache_v, attn_sink[o], o)
            x = _odd_out(a, _bf(w_out_odd[o]), x, mods, l, ln_g[l, 0][None, :], ln_b[l, 0][None, :])
            kp = qkv[:NP_ROWS, HC * HD:(HC + KVH) * HD].reshape(BATCH, SEQ, KVH, HD).transpose(0, 2, 1, 3)
            vp = qkv[:NP_ROWS, (HC + KVH) * HD:].reshape(BATCH, SEQ, KVH, HD).transpose(0, 2, 1, 3)
            out_k.append(kp)
            out_v.append(vp)
        w_r = jnp.concatenate([w_erouter[l].transpose(1, 0, 2).reshape(D, N_EXPERTS), w_grp[l],
                               jnp.zeros((D, 128 - N_EXPERTS - N_GROUPS), F32)], axis=1)
        b_r = jnp.concatenate([b_erouter[l].reshape(-1), b_grp[l],
                               jnp.zeros((128 - N_EXPERTS - N_GROUPS,), F32)])[None, :]
        gates = _router(x, mods, l, w_r, b_r)
        x = _moe(x, mods, l, gates, _bf(w_gate[l]), _bf(w_up[l]), _bf(w_down[l]),
                 ln_g[l, 1][None, :], ln_b[l, 1][None, :])
    return (x[:NP_ROWS].reshape(BATCH, SEQ, D), x[NP_ROWS:].reshape(DEC_BATCH, DEC_SEQ, D),
            jnp.stack(out_mc, 1), jnp.stack(out_mn, 1), jnp.stack(out_mm, 1), jnp.stack(out_ds, 1),
            jnp.stack(out_k, 1), jnp.stack(out_v, 1))
```

```python
import functools

import jax
import jax.numpy as jnp
from jax import lax
from jax.experimental import pallas as pl
from jax.experimental.pallas import tpu as pltpu

F32 = jnp.float32
BF16 = jnp.bfloat16
HI = lax.Precision.HIGHEST

D = 1024
BATCH = 32
SEQ = 256
DEPTH = 4
DEC_BATCH = 2
DEC_SEQ = 4096
PAST_LEN = 512
GRID_W = 64
N_EVEN = 2
N_ODD = 2
HA = 4
HB = 4
DH = 128
CHUNK = 64
HC = 16
KVH = 4
HD = 64
WINDOW = 128
QBLOCK = 128
ROPE_THETA = 10000.0
N_GROUPS = 4
EPG = 4
N_EXPERTS = 16
EXPERT_FF = 256
DN_ALPHA = (2 * DEPTH) ** 0.25
LN_EPS = 1e-5

NP_ROWS = BATCH * SEQ
NS_ROWS = DEC_BATCH * DEC_SEQ
T = NP_ROWS + NS_ROWS
N_MOD_ROWS = 8
EVEN_W = 4224
QKV_W = (HC + 2 * KVH) * HD

P_CHUNKS = SEQ // CHUNK
S_CHUNKS = DEC_SEQ // CHUNK
P_STEPS = BATCH * P_CHUNKS
S_STEPS = DEC_BATCH * S_CHUNKS
P_BLOCKS = NP_ROWS // CHUNK

VMEM_LIMIT = 48 * 1024 * 1024


def _params(n_axes):
    return pltpu.CompilerParams(dimension_semantics=("arbitrary",) * n_axes,
                                vmem_limit_bytes=VMEM_LIMIT)


def _mm(a, b, prec=None):
    return lax.dot_general(a, b, (((1,), (0,)), ((), ())), precision=prec, preferred_element_type=F32)


def _mm_nt(a, b, prec=None):
    return lax.dot_general(a, b, (((1,), (1,)), ((), ())), precision=prec, preferred_element_type=F32)


def _mm_tn(a, b, prec=None):
    return lax.dot_general(a, b, (((0,), (0,)), ((), ())), precision=prec, preferred_element_type=F32)


def _bf(x):
    return x.astype(BF16)


def _sigmoid(x):
    return 1.0 / (1.0 + jnp.exp(-x))


def _softplus(x):
    return jnp.maximum(x, 0.0) + jnp.log1p(jnp.exp(-jnp.abs(x)))


def _log_sigmoid(x):
    return jnp.minimum(x, 0.0) - jnp.log1p(jnp.exp(-jnp.abs(x)))


def _mod_row(tile, tm):
    npt = NP_ROWS // tm
    per = DEC_SEQ // tm
    return jnp.where(tile < npt, 0, 1 + (tile - npt) // per)


def _mod_spec(layer, chunk, tm):
    def imap(i, *_):
        return ((layer * N_MOD_ROWS + _mod_row(i, tm)) * 6 + chunk, 0, 0)
    return pl.BlockSpec((1, 1, D), imap)


def _modulation_kernel(c_ref, w_ref, b_ref, o_ref):
    x = c_ref[...]
    s = x * _sigmoid(x)
    o_ref[0] = _mm(s, w_ref[0], HI) + b_ref[0]


def _modulation(cvecs, w_mod, b_mod):
    out = pl.pallas_call(
        _modulation_kernel,
        out_shape=jax.ShapeDtypeStruct((DEPTH, N_MOD_ROWS, 6 * D), F32),
        grid=(DEPTH, 6),
        in_specs=[pl.BlockSpec((N_MOD_ROWS, D), lambda l, j: (0, 0)),
                  pl.BlockSpec((1, D, D), lambda l, j: (l, 0, j)),
                  pl.BlockSpec((1, 1, D), lambda l, j: (l * 6 + j, 0, 0))],
        out_specs=pl.BlockSpec((1, N_MOD_ROWS, D), lambda l, j: (l, 0, j)),
        compiler_params=_params(2),
        name="modulation",
    )(cvecs, w_mod, b_mod.reshape(DEPTH * 6, 1, D))
    return out.reshape(DEPTH * N_MOD_ROWS * 6, 1, D)


def _proj_kernel(x_ref, sh_ref, sc_ref, w_ref, o_ref):
    xm = x_ref[...] * (1.0 + sc_ref[0]) + sh_ref[0]
    o_ref[...] = _mm(_bf(xm), w_ref[...])


def _even_proj(x, mods, layer, w):
    tm = 256
    return pl.pallas_call(
        _proj_kernel,
        out_shape=jax.ShapeDtypeStruct((T, EVEN_W), F32),
        grid=(T // tm,),
        in_specs=[pl.BlockSpec((tm, D), lambda i: (i, 0)),
                  _mod_spec(layer, 0, tm), _mod_spec(layer, 1, tm),
                  pl.BlockSpec((D, EVEN_W), lambda i: (0, 0))],
        out_specs=pl.BlockSpec((tm, EVEN_W), lambda i: (i, 0)),
        compiler_params=_params(1),
        name="even_proj",
    )(x, mods, mods, w)


def _qkv_kernel(x_ref, sh_ref, sc_ref, w_ref, cos_ref, sa_ref, sb_ref, o_ref, *, tm):
    i = pl.program_id(0)
    xm = x_ref[...] * (1.0 + sc_ref[0]) + sh_ref[0]
    acc = _mm(_bf(xm), w_ref[...])
    n_q = HC * HD // 128
    n_k = KVH * HD // 128
    is_latent = i >= NP_ROWS // tm
    cos = jnp.where(is_latent, cos_ref[...], 1.0)
    sa = jnp.where(is_latent, sa_ref[...], 0.0)
    sb = jnp.where(is_latent, sb_ref[...], 0.0)
    for g in range(n_q + n_k):
        blk = acc[:, g * 128:(g + 1) * 128]
        if g < n_q:
            blk = blk * (HD ** -0.5)
        rot = blk * cos + pltpu.roll(blk, 112, 1) * sa + pltpu.roll(blk, 16, 1) * sb
        o_ref[:, g * 128:(g + 1) * 128] = rot
    o_ref[:, (n_q + n_k) * 128:] = acc[:, (n_q + n_k) * 128:]


def _rope_tables():
    half = HD // 4
    inv = ROPE_THETA ** (-jnp.arange(half, dtype=F32) / half)
    pos = jnp.arange(DEC_SEQ)
    row = (pos // GRID_W).astype(F32)[:, None] * inv[None, :]
    col = (pos % GRID_W).astype(F32)[:, None] * inv[None, :]
    cos = jnp.concatenate([jnp.cos(row), jnp.cos(row), jnp.cos(col), jnp.cos(col)], axis=-1)
    sin = jnp.concatenate([jnp.sin(row), jnp.sin(row), jnp.sin(col), jnp.sin(col)], axis=-1)
    first = (jnp.arange(HD) % 32) < 16
    sa = jnp.where(first, -sin, 0.0)
    sb = jnp.where(first, 0.0, sin)
    tile2 = lambda t: jnp.concatenate([t, t], axis=-1)
    return tile2(cos), tile2(sa), tile2(sb)


def _odd_proj(x, mods, layer, w, tables):
    tm = 256
    npt = NP_ROWS // tm
    per = DEC_SEQ // tm
    tab_spec = pl.BlockSpec((tm, 128), lambda i: (jnp.where(i < npt, 0, (i - npt) % per), 0))
    return pl.pallas_call(
        functools.partial(_qkv_kernel, tm=tm),
        out_shape=jax.ShapeDtypeStruct((T, QKV_W), F32),
        grid=(T // tm,),
        in_specs=[pl.BlockSpec((tm, D), lambda i: (i, 0)),
                  _mod_spec(layer, 0, tm), _mod_spec(layer, 1, tm),
                  pl.BlockSpec((D, QKV_W), lambda i: (0, 0)),
                  tab_spec, tab_spec, tab_spec],
        out_specs=pl.BlockSpec((tm, QKV_W), lambda i: (i, 0)),
        compiler_params=_params(1),
        name="odd_qkv_proj",
    )(x, mods, mods, w, *tables)


PREP_ROWS = 256


def _delta_prep_kernel(x_ref, prev_ref, next_ref, w_ref, o_ref):
    i = pl.program_id(0)
    part = pl.program_id(1)
    npb = NP_ROWS // PREP_ROWS
    per = DEC_SEQ // PREP_ROWS
    is_latent = i >= npb
    pos = (i - npb) % per
    has_prev = jnp.logical_and(is_latent, pos > 0)
    has_next = jnp.logical_and(is_latent, pos < per - 1)
    x = x_ref[...]
    w = w_ref[...]
    rows = lax.broadcasted_iota(jnp.int32, x.shape, 0)
    prev_row = jnp.where(has_prev, prev_ref[7:8, :], 0.0)
    next_row = jnp.where(has_next, next_ref[0:1, :], 0.0)
    xm1 = jnp.where(rows == 0, prev_row, pltpu.roll(x, 1, 0))
    xp1 = jnp.where(rows == PREP_ROWS - 1, next_row, pltpu.roll(x, PREP_ROWS - 1, 0))
    y = xm1 * w[0:1, :] + x * w[1:2, :] + xp1 * w[2:3, :]
    y = y * _sigmoid(y)
    q_scale = jnp.where(part == 0, DH ** -0.5, 1.0)
    for h in range(4):
        yh = y[:, h * DH:(h + 1) * DH]
        inv = lax.rsqrt(jnp.sum(yh * yh, axis=-1, keepdims=True) + 1e-6)
        scale = jnp.where(part == 2, 1.0, inv * q_scale)
        o_ref[:, h * DH:(h + 1) * DH] = yh * scale


def _delta_prep(proj, conv_w):
    nblk = T // PREP_ROWS
    sub = PREP_ROWS // 8
    last8 = T // 8 - 1
    return pl.pallas_call(
        _delta_prep_kernel,
        out_shape=jax.ShapeDtypeStruct((T, 3 * 512), F32),
        grid=(nblk, 3),
        in_specs=[pl.BlockSpec((PREP_ROWS, 512), lambda i, p: (i, 4 + p)),
                  pl.BlockSpec((8, 512), lambda i, p: (jnp.maximum(i * sub - 1, 0), 4 + p)),
                  pl.BlockSpec((8, 512), lambda i, p: (jnp.minimum((i + 1) * sub, last8), 4 + p)),
                  pl.BlockSpec((3, 512), lambda i, p: (0, p))],
        out_specs=pl.BlockSpec((PREP_ROWS, 512), lambda i, p: (i, p)),
        compiler_params=_params(2),
        name="delta_prep",
    )(proj, proj, proj, conv_w)


def _scan_decode(s):
    is_s = s >= P_STEPS
    sp = s - P_STEPS
    b = jnp.where(is_s, sp // S_CHUNKS, s // P_CHUNKS)
    j = jnp.where(is_s, sp % S_CHUNKS, s % P_CHUNKS)
    nc = jnp.where(is_s, S_CHUNKS, P_CHUNKS)
    base = jnp.where(is_s, P_BLOCKS + b * S_CHUNKS, b * P_CHUNKS)
    return is_s, b, j, nc, base


def _fwd_blk(s):
    _, _, j, _, base = _scan_decode(s)
    return base + j


def _bwd_blk(s):
    _, _, j, nc, base = _scan_decode(s)
    return base + nc - 1 - j


def _state_in_idx(s):
    is_s, b, _, _, _ = _scan_decode(s)
    return jnp.where(is_s, b, 0)


def _state_out_idx(s):
    is_s, b, _, _, _ = _scan_decode(s)
    return jnp.where(is_s, BATCH - 1, b)


def _tri_masks(direction):
    row = lax.broadcasted_iota(jnp.int32, (CHUNK, CHUNK), 0)
    col = lax.broadcasted_iota(jnp.int32, (CHUNK, CHUNK), 1)
    if direction == 0:
        return col <= row, col < row
    return col >= row, col > row


def _mlstm_kernel(qf_ref, kf_ref, vf_ref, qb_ref, kb_ref, vb_ref, gcf_ref, gcb_ref, grf_ref, grb_ref,
                  brow_ref, bcol_ref, c0_ref, n0_ref, m0_ref,
                  hf_ref, hb_ref, c_out, n_out, m_out, c_s, n_s, m_s):
    s = pl.program_id(0)
    is_s, _, j, nc, _ = _scan_decode(s)

    @pl.when(jnp.logical_and(j == 0, jnp.logical_not(is_s)))
    def _():
        c_s[...] = jnp.zeros_like(c_s)
        n_s[...] = jnp.zeros_like(n_s)
        m_s[...] = jnp.zeros_like(m_s)

    @pl.when(jnp.logical_and(j == 0, is_s))
    def _():
        for d in range(2):
            for h in range(HA):
                c_s[d * HA + h] = c0_ref[0, 0, d, h]
            n_s[d * HA:(d + 1) * HA, :] = n0_ref[0, 0, d]
        m_s[...] = m0_ref[0, 0]

    ins = ((qf_ref, kf_ref, vf_ref, gcf_ref, grf_ref, hf_ref),
           (qb_ref, kb_ref, vb_ref, gcb_ref, grb_ref, hb_ref))
    for d in range(2):
        q_ref, k_ref, v_ref, gc_ref, gr_ref, h_ref = ins[d]
        incl, _ = _tri_masks(d)
        incl_t, _ = _tri_masks(1 - d)
        end = CHUNK - 1 if d == 0 else 0
        gc = gc_ref[...] + brow_ref[...]
        gr = gr_ref[0] + bcol_ref[...]
        cs_col = _prefix_cols(incl.astype(BF16), _log_sigmoid(gc))
        cs_row = _prefix_rows(_log_sigmoid(gr), incl_t.astype(BF16))
        for h in range(HA):
            c = d * HA + h
            ci = d * 8 + h
            cf = d * 8 + 4 + h
            i_col = gc[:, ci:ci + 1]
            i_row = gr[ci:ci + 1, :]
            b_col = cs_col[:, cf:cf + 1]
            b_row = cs_row[cf:cf + 1, :]
            b_last = b_col[end:end + 1, :]
            m = m_s[c:c + 1, 0:1]
            dmat = jnp.where(incl, b_col - b_row + i_row, -jnp.inf)
            inter = b_col + m
            m_t = jnp.maximum(inter, jnp.max(dmat, axis=1, keepdims=True))
            q = q_ref[:, h * DH:(h + 1) * DH]
            k = k_ref[:, h * DH:(h + 1) * DH] * (DH ** -0.5)
            v = v_ref[:, h * DH:(h + 1) * DH]
            qb_, kb_, vb_ = _bf(q), _bf(k), _bf(v)
            sc = _mm_nt(qb_, kb_) * jnp.exp(dmat - m_t)
            w_inter = jnp.exp(inter - m_t)
            c_mat = c_s[c]
            nrm = n_s[c:c + 1, :]
            num = _mm(_bf(sc), vb_) + w_inter * _mm(qb_, _bf(c_mat))
            den = jnp.sum(sc, axis=1, keepdims=True) + w_inter * jnp.sum(q * nrm, axis=1, keepdims=True)
            h_ref[:, h * DH:(h + 1) * DH] = num / jnp.maximum(jnp.abs(den), jnp.exp(-m_t))
            g_end = b_last - b_col + i_col
            m_new = jnp.maximum(b_last + m, jnp.max(g_end, axis=0, keepdims=True))
            kw = k * jnp.exp(g_end - m_new)
            decay = jnp.exp(b_last + m - m_new)
            c_s[c] = decay * c_mat + _mm_tn(_bf(kw), vb_)
            n_s[c:c + 1, :] = decay * nrm + jnp.sum(kw, axis=0, keepdims=True)
            m_s[c:c + 1, :] = jnp.broadcast_to(m_new, (1, 128))

    @pl.when(jnp.logical_and(j == nc - 1, jnp.logical_not(is_s)))
    def _():
        for d in range(2):
            for h in range(HA):
                c_out[0, d, h] = c_s[d * HA + h]
            n_out[0, d] = n_s[d * HA:(d + 1) * HA, :]
        m_out[0] = m_s[...]


def _scan_specs(width_blocks):
    fwd = [pl.BlockSpec((CHUNK, 512), (lambda s, p=p: (_fwd_blk(s), p))) for p in width_blocks]
    bwd = [pl.BlockSpec((CHUNK, 512), (lambda s, p=p: (_bwd_blk(s), p))) for p in width_blocks]
    return fwd + bwd


def _gate_specs(col_block):
    return [pl.BlockSpec((CHUNK, 128), lambda s: (_fwd_blk(s), col_block)),
            pl.BlockSpec((CHUNK, 128), lambda s: (_bwd_blk(s), col_block)),
            pl.BlockSpec((1, 32, CHUNK), lambda s: (_fwd_blk(s), 0, 0)),
            pl.BlockSpec((1, 32, CHUNK), lambda s: (_bwd_blk(s), 0, 0))]


def _mlstm_scan(proj, gates_t, bias_row, bias_col, c0, n0, m0, e):
    const2 = lambda s: (0, 0)
    in_specs = (_scan_specs((0, 1, 2)) + _gate_specs(EVEN_W // 128 - 1)
                + [pl.BlockSpec((1, 128), const2), pl.BlockSpec((32, 1), const2),
                   pl.BlockSpec((1, 1, 2, HA, DH, DH), lambda s: (_state_in_idx(s), e, 0, 0, 0, 0)),
                   pl.BlockSpec((1, 1, 2, HA, DH), lambda s: (_state_in_idx(s), e, 0, 0, 0)),
                   pl.BlockSpec((1, 1, 2 * HA, 128), lambda s: (_state_in_idx(s), e, 0, 0))])
    out_shape = (jax.ShapeDtypeStruct((T, HA * DH), F32), jax.ShapeDtypeStruct((T, HA * DH), F32),
                 jax.ShapeDtypeStruct((BATCH, 2, HA, DH, DH), F32),
                 jax.ShapeDtypeStruct((BATCH, 2, HA, DH), F32),
                 jax.ShapeDtypeStruct((BATCH, 2 * HA, 128), F32))
    out_specs = (pl.BlockSpec((CHUNK, 512), lambda s: (_fwd_blk(s), 0)),
                 pl.BlockSpec((CHUNK, 512), lambda s: (_bwd_blk(s), 0)),
                 pl.BlockSpec((1, 2, HA, DH, DH), lambda s: (_state_out_idx(s), 0, 0, 0, 0)),
                 pl.BlockSpec((1, 2, HA, DH), lambda s: (_state_out_idx(s), 0, 0, 0)),
                 pl.BlockSpec((1, 2 * HA, 128), lambda s: (_state_out_idx(s), 0, 0)))
    return pl.pallas_call(
        _mlstm_kernel,
        out_shape=out_shape,
        grid=(P_STEPS + S_STEPS,),
        in_specs=in_specs,
        out_specs=out_specs,
        scratch_shapes=[pltpu.VMEM((2 * HA, DH, DH), F32), pltpu.VMEM((2 * HA, DH), F32),
                        pltpu.VMEM((2 * HA, 128), F32)],
        compiler_params=_params(1),
        name="mlstm_scan",
    )(proj, proj, proj, proj, proj, proj, proj, proj, gates_t, gates_t, bias_row, bias_col, c0, n0, m0)


def _split2(x):
    hi = _bf(x)
    return hi, _bf(x - hi.astype(F32))


def _split3(x):
    h1 = _bf(x)
    r1 = x - h1.astype(F32)
    h2 = _bf(r1)
    return h1, h2, _bf(r1 - h2.astype(F32))


def _lhs3(x):
    hi, lo = _split2(x)
    return jnp.concatenate([hi, lo, hi], axis=1)


def _rhs3(x):
    hi, lo = _split2(x)
    return jnp.concatenate([hi, hi, lo], axis=0)


def _prefix_cols(tri_bf, x):
    n = x.shape[1]
    r = _mm(tri_bf, jnp.concatenate(_split3(x), axis=1))
    return r[:, :n] + r[:, n:2 * n] + r[:, 2 * n:]


def _prefix_rows(x, tri_bf):
    m = x.shape[0]
    r = _mm(jnp.concatenate(_split3(x), axis=0), tri_bf)
    return r[:m] + r[m:2 * m] + r[2 * m:]


LEVELS = tuple(range(6))


def _block_diag2(a, b):
    z = jnp.zeros_like(a)
    return jnp.concatenate([jnp.concatenate([a, z], axis=1), jnp.concatenate([z, b], axis=1)], axis=0)


def _delta_kernel(qf_ref, kf_ref, vf_ref, qb_ref, kb_ref, vb_ref, gcf_ref, gcb_ref, grf_ref, grb_ref,
                  dtrow_ref, dt2_ref, narow_ref, na2_ref, s0_ref,
                  of_ref, ob_ref, s_out, s_s):
    s = pl.program_id(0)
    is_s, _, j, nc, _ = _scan_decode(s)

    @pl.when(jnp.logical_and(j == 0, jnp.logical_not(is_s)))
    def _():
        s_s[...] = jnp.zeros_like(s_s)

    @pl.when(jnp.logical_and(j == 0, is_s))
    def _():
        for d in range(2):
            for h in range(HB):
                s_s[d * HB + h] = s0_ref[0, 0, d, h]

    row = lax.broadcasted_iota(jnp.int32, (CHUNK, 128), 0)
    lane = lax.broadcasted_iota(jnp.int32, (CHUNK, 128), 1)
    left = lane < CHUNK
    lcol = jnp.where(left, lane, lane - CHUNK)
    eye_p = (lcol == row).astype(F32)
    r64 = lax.broadcasted_iota(jnp.int32, (CHUNK, CHUNK), 0)
    c64 = lax.broadcasted_iota(jnp.int32, (CHUNK, CHUNK), 1)
    r128 = lax.broadcasted_iota(jnp.int32, (128, 128), 0)
    c128 = lax.broadcasted_iota(jnp.int32, (128, 128), 1)
    same_half = (r128 < CHUNK) == (c128 < CHUNK)
    ins = ((qf_ref, kf_ref, vf_ref, gcf_ref, grf_ref, of_ref),
           (qb_ref, kb_ref, vb_ref, gcb_ref, grb_ref, ob_ref))

    gate = []
    for d in range(2):
        gc_ref, gr_ref = ins[d][3], ins[d][4]
        if d == 0:
            incl_p, strict_p = lcol <= row, lcol < row
            tri_c, tri_r = c64 <= r64, jnp.logical_and(same_half, r128 <= c128)
        else:
            incl_p, strict_p = lcol >= row, lcol > row
            tri_c, tri_r = c64 >= r64, jnp.logical_and(same_half, r128 >= c128)
        xc = gc_ref[...]
        xr = gr_ref[0]
        la_c = narow_ref[...] * _softplus(xc + dtrow_ref[...])
        la_r = na2_ref[...] * _softplus(xr + dt2_ref[...])
        g_c = _prefix_cols(tri_c.astype(BF16), la_c)
        g_r = _prefix_rows(la_r, tri_r.astype(BF16))
        gate.append((incl_p, strict_p, _sigmoid(xc), g_c, g_r))

    def bd_rhs(x):
        return _rhs3(jnp.concatenate([jnp.where(left, x, 0.0), jnp.where(left, 0.0, x)], axis=0))

    off = ([], [])
    for lv in LEVELS:
        same = jnp.right_shift(row, lv + 1) == jnp.right_shift(lcol, lv + 1)
        r_hi = jnp.bitwise_and(jnp.right_shift(row, lv), 1) == 1
        c_hi = jnp.bitwise_and(jnp.right_shift(lcol, lv), 1) == 1
        off[0].append(same & r_hi & jnp.logical_not(c_hi))
        off[1].append(same & jnp.logical_not(r_hi) & c_hi)

    pairs = [(d, hp) for d in range(2) for hp in range(HB // 2)]
    st = []
    for d, hp in pairs:
        q_ref, k_ref, v_ref = ins[d][:3]
        incl_p, strict_p, beta_c, g_c, g_r = gate[d]
        end = CHUNK - 1 if d == 0 else 0
        heads = []
        for h in (2 * hp, 2 * hp + 1):
            ib = 16 + d * HB + h
            ia = 24 + d * HB + h
            sl = slice(h * DH, (h + 1) * DH)
            gcol = g_c[:, ia:ia + 1]
            bc = beta_c[:, ib:ib + 1]
            q, k, v = q_ref[:, sl], k_ref[:, sl], v_ref[:, sl]
            heads.append(dict(h=h, gcol=gcol, bc=bc, q=q, k=k, v=v, kb=k * bc, eg=jnp.exp(gcol),
                              g_last=gcol[end:end + 1, :]))
        ha, hb_ = heads
        r = 12 + d * 2 + hp
        gcol_p = jnp.where(left, ha["gcol"], hb_["gcol"])
        decay = jnp.exp(jnp.where(incl_p, gcol_p - g_r[r:r + 1, :], -jnp.inf))
        k_bd = _bf(_block_diag2(ha["k"], hb_["k"]))
        kb_cat = _bf(jnp.concatenate([ha["kb"], hb_["kb"]], axis=1))
        q_cat = _bf(jnp.concatenate([ha["q"], hb_["q"]], axis=1))
        a_mat = jnp.where(strict_p, _mm_nt(kb_cat, k_bd) * decay, 0.0)
        qk = _mm_nt(q_cat, k_bd) * decay
        st.append(dict(d=d, heads=heads, t=eye_p - jnp.where(off[d][0], a_mat, 0.0), qk=qk,
                       am=[bd_rhs(jnp.where(m, a_mat, 0.0)) for m in off[d][1:]]))

    for li in range(len(LEVELS) - 1):
        for p in st:
            p["w"] = _mm(_lhs3(p["t"]), p["am"][li])
        for p in st:
            p["t"] = p["t"] - _mm(_lhs3(p["w"]), bd_rhs(p["t"]))

    for p in st:
        ha, hb_ = p["heads"]
        o_ref = ins[p["d"]][5]
        rhs_a = jnp.concatenate([ha["v"] * ha["bc"], ha["kb"] * ha["eg"]], axis=1)
        rhs_b = jnp.concatenate([hb_["v"] * hb_["bc"], hb_["kb"] * hb_["eg"]], axis=1)
        sol = _mm(_lhs3(p["t"]), _rhs3(_block_diag2(rhs_a, rhs_b)))
        vn = []
        for idx, hd in enumerate((ha, hb_)):
            c = p["d"] * HB + hd["h"]
            s_mat = s_s[c]
            sbf = _bf(s_mat)
            so = sol[:, idx * 2 * DH:(idx + 1) * 2 * DH]
            v_new = so[:, :DH] - _mm(_bf(so[:, DH:]), sbf)
            vn.append(v_new)
            hd["o1"] = _mm(_bf(hd["q"] * hd["eg"]), sbf)
            s_s[c] = (jnp.exp(hd["g_last"]) * s_mat
                      + _mm_tn(_bf(hd["k"] * jnp.exp(hd["g_last"] - hd["gcol"])), _bf(v_new)))
        o2 = _mm(_bf(p["qk"]), _bf(_block_diag2(vn[0], vn[1])))
        for idx, hd in enumerate((ha, hb_)):
            o_ref[:, hd["h"] * DH:(hd["h"] + 1) * DH] = hd["o1"] + o2[:, idx * DH:(idx + 1) * DH]

    @pl.when(jnp.logical_and(j == nc - 1, jnp.logical_not(is_s)))
    def _():
        for d in range(2):
            for h in range(HB):
                s_out[0, d, h] = s_s[d * HB + h]


def _delta_scan(qkv, proj, gates_t2, dt_row, dt2, na_row, na2, s0, e):
    const2 = lambda s: (0, 0)
    in_specs = (_scan_specs((0, 1, 2))
                + [pl.BlockSpec((CHUNK, 128), lambda s: (_fwd_blk(s), EVEN_W // 128 - 1)),
                   pl.BlockSpec((CHUNK, 128), lambda s: (_bwd_blk(s), EVEN_W // 128 - 1)),
                   pl.BlockSpec((1, 16, 128), lambda s: (_fwd_blk(s), 0, 0)),
                   pl.BlockSpec((1, 16, 128), lambda s: (_bwd_blk(s), 0, 0)),
                   pl.BlockSpec((1, 128), const2), pl.BlockSpec((16, 128), const2),
                   pl.BlockSpec((1, 128), const2), pl.BlockSpec((16, 128), const2),
                   pl.BlockSpec((1, 1, 2, HB, DH, DH), lambda s: (_state_in_idx(s), e, 0, 0, 0, 0))])
    out_shape = (jax.ShapeDtypeStruct((T, HB * DH), F32), jax.ShapeDtypeStruct((T, HB * DH), F32),
                 jax.ShapeDtypeStruct((BATCH, 2, HB, DH, DH), F32))
    out_specs = (pl.BlockSpec((CHUNK, 512), lambda s: (_fwd_blk(s), 0)),
                 pl.BlockSpec((CHUNK, 512), lambda s: (_bwd_blk(s), 0)),
                 pl.BlockSpec((1, 2, HB, DH, DH), lambda s: (_state_out_idx(s), 0, 0, 0, 0)))
    return pl.pallas_call(
        _delta_kernel,
        out_shape=out_shape,
        grid=(P_STEPS + S_STEPS,),
        in_specs=in_specs,
        out_specs=out_specs,
        scratch_shapes=[pltpu.VMEM((2 * HB, DH, DH), F32)],
        compiler_params=_params(1),
        name="delta_scan",
    )(qkv, qkv, qkv, qkv, qkv, qkv, proj, proj, gates_t2, gates_t2, dt_row, dt2, na_row, na2, s0)


def _residual_ln(x, gate, y, g, b):
    r = DN_ALPHA * x + gate * y
    mu = jnp.mean(r, axis=-1, keepdims=True)
    var = jnp.mean(jnp.square(r - mu), axis=-1, keepdims=True)
    return (r - mu) * lax.rsqrt(var + LN_EPS) * g + b


def _even_out_kernel(hf_ref, hb_ref, oa_ref, of_ref, ob_ref, zb_ref, mg_ref, dg_ref,
                     w_ref, x_ref, gate_ref, g_ref, b_ref, o_ref):
    parts = []
    for h in range(HA):
        sl = slice(h * DH, (h + 1) * DH)
        hh = hf_ref[:, sl] + hb_ref[:, sl]
        mu = jnp.mean(hh, axis=-1, keepdims=True)
        var = jnp.mean(jnp.square(hh - mu), axis=-1, keepdims=True)
        parts.append(_sigmoid(oa_ref[:, sl]) * ((hh - mu) * lax.rsqrt(var + LN_EPS) * mg_ref[:, sl]))
    for h in range(HB):
        sl = slice(h * DH, (h + 1) * DH)
        oo = of_ref[:, sl] + ob_ref[:, sl]
        z = zb_ref[:, sl]
        nrm = oo * lax.rsqrt(jnp.mean(jnp.square(oo), axis=-1, keepdims=True) + LN_EPS) * dg_ref[:, sl]
        parts.append(nrm * (z * _sigmoid(z)))
    a = jnp.concatenate(parts, axis=1)
    y = _mm(_bf(a), w_ref[...])
    o_ref[...] = _residual_ln(x_ref[...], gate_ref[0], y, g_ref[...], b_ref[...])


def _even_out(hf, hb, of, ob, proj, mg, dg, w, x, mods, layer, ln_g, ln_b):
    tm = 256
    row512 = lambda i: (i, 0)
    const2 = lambda i: (0, 0)
    return pl.pallas_call(
        _even_out_kernel,
        out_shape=jax.ShapeDtypeStruct((T, D), F32),
        grid=(T // tm,),
        in_specs=[pl.BlockSpec((tm, 512), row512), pl.BlockSpec((tm, 512), row512),
                  pl.BlockSpec((tm, 512), lambda i: (i, 3)),
                  pl.BlockSpec((tm, 512), row512), pl.BlockSpec((tm, 512), row512),
                  pl.BlockSpec((tm, 512), lambda i: (i, 7)),
                  pl.BlockSpec((1, 512), const2), pl.BlockSpec((1, 512), const2),
                  pl.BlockSpec((D, D), const2),
                  pl.BlockSpec((tm, D), row512),
                  _mod_spec(layer, 2, tm),
                  pl.BlockSpec((1, D), const2), pl.BlockSpec((1, D), const2)],
        out_specs=pl.BlockSpec((tm, D), row512),
        compiler_params=_params(1),
        name="even_out_ln",
    )(hf, hb, proj, of, ob, proj, mg, dg, w, x, mods, ln_g, ln_b)


def _odd_out_kernel(a_ref, w_ref, x_ref, gate_ref, g_ref, b_ref, o_ref):
    y = _mm(_bf(a_ref[...]), w_ref[...])
    o_ref[...] = _residual_ln(x_ref[...], gate_ref[0], y, g_ref[...], b_ref[...])


def _odd_out(a, w, x, mods, layer, ln_g, ln_b):
    tm = 256
    row = lambda i: (i, 0)
    const2 = lambda i: (0, 0)
    return pl.pallas_call(
        _odd_out_kernel,
        out_shape=jax.ShapeDtypeStruct((T, D), F32),
        grid=(T // tm,),
        in_specs=[pl.BlockSpec((tm, D), row), pl.BlockSpec((D, D), const2), pl.BlockSpec((tm, D), row),
                  _mod_spec(layer, 2, tm), pl.BlockSpec((1, D), const2), pl.BlockSpec((1, D), const2)],
        out_specs=pl.BlockSpec((tm, D), row),
        compiler_params=_params(1),
        name="odd_out_ln",
    )(a, w, x, mods, ln_g, ln_b)


def _attn_ctx_kernel(qkv_ref, sink_ref, o_ref):
    for h in range(HC):
        kv = h // (HC // KVH)
        q = _bf(qkv_ref[:, h * HD:(h + 1) * HD])
        k = _bf(qkv_ref[:, HC * HD + kv * HD:HC * HD + (kv + 1) * HD])
        v = _bf(qkv_ref[:, (HC + KVH) * HD + kv * HD:(HC + KVH) * HD + (kv + 1) * HD])
        sink = sink_ref[h]
        sc = _mm_nt(q, k)
        m = jnp.maximum(jnp.max(sc, axis=1, keepdims=True), sink)
        p = jnp.exp(sc - m)
        den = jnp.sum(p, axis=1, keepdims=True) + jnp.exp(sink - m)
        o_ref[:, h * HD:(h + 1) * HD] = _mm(_bf(p), v) / den


def _attn_context(qkv, sink):
    return pl.pallas_call(
        _attn_ctx_kernel,
        out_shape=jax.ShapeDtypeStruct((T, HC * HD), F32),
        grid=(BATCH,),
        in_specs=[pl.BlockSpec((SEQ, QKV_W), lambda b: (b, 0)),
                  pl.BlockSpec(memory_space=pltpu.SMEM)],
        out_specs=pl.BlockSpec((SEQ, HC * HD), lambda b: (b, 0)),
        compiler_params=_params(1),
        name="attn_context",
    )(qkv, sink)


def _attn_lat_kernel(prev_out_ref, q_ref, kp_ref, kc_ref, kn_ref, vp_ref, vc_ref, vn_ref, ck_ref, cv_ref,
                     sink_ref, o_ref):
    del prev_out_ref
    j = pl.program_id(1)
    nb = DEC_SEQ // QBLOCK
    r = lax.broadcasted_iota(jnp.int32, (QBLOCK, QBLOCK), 0)
    c = lax.broadcasted_iota(jnp.int32, (QBLOCK, QBLOCK), 1)
    ok_prev = jnp.logical_and(c >= r, j >= 1)
    ok_next = jnp.logical_and(c <= r, j <= nb - 2)
    for h in range(HC):
        kv = h // (HC // KVH)
        ks = slice(kv * HD, (kv + 1) * HD)
        q = _bf(q_ref[:, h * HD:(h + 1) * HD])
        ck = _bf(ck_ref[0, 0, kv])
        cv = _bf(cv_ref[0, 0, kv])
        sink = sink_ref[h]
        s_ctx = _mm_nt(q, ck)
        s_p = jnp.where(ok_prev, _mm_nt(q, _bf(kp_ref[:, ks])), -jnp.inf)
        s_c = _mm_nt(q, _bf(kc_ref[:, ks]))
        s_n = jnp.where(ok_next, _mm_nt(q, _bf(kn_ref[:, ks])), -jnp.inf)
        m = jnp.maximum(jnp.maximum(jnp.max(s_ctx, axis=1, keepdims=True), jnp.max(s_c, axis=1, keepdims=True)),
                        jnp.maximum(jnp.max(s_p, axis=1, keepdims=True), jnp.max(s_n, axis=1, keepdims=True)))
        m = jnp.maximum(m, sink)
        p_ctx = jnp.exp(s_ctx - m)
        p_p = jnp.exp(s_p - m)
        p_c = jnp.exp(s_c - m)
        p_n = jnp.exp(s_n - m)
        den = (jnp.sum(p_ctx, axis=1, keepdims=True) + jnp.sum(p_p, axis=1, keepdims=True)
               + jnp.sum(p_c, axis=1, keepdims=True) + jnp.sum(p_n, axis=1, keepdims=True) + jnp.exp(sink - m))
        num = (_mm(_bf(p_ctx), cv) + _mm(_bf(p_p), _bf(vp_ref[:, ks]))
               + _mm(_bf(p_c), _bf(vc_ref[:, ks])) + _mm(_bf(p_n), _bf(vn_ref[:, ks])))
        o_ref[:, h * HD:(h + 1) * HD] = num / den


def _attn_latent(ctx_out, qkv, cache_k, cache_v, sink, o):
    nb = DEC_SEQ // QBLOCK
    base = NP_ROWS // QBLOCK
    blk = lambda b, j: base + b * nb + j
    prev = lambda b, j: base + b * nb + jnp.maximum(j - 1, 0)
    nxt = lambda b, j: base + b * nb + jnp.minimum(j + 1, nb - 1)
    kcol, vcol = HC * HD // 256, HC * HD // 256 + 1
    cache_spec = pl.BlockSpec((1, 1, KVH, PAST_LEN, HD), lambda b, j: (b, o, 0, 0, 0))
    return pl.pallas_call(
        _attn_lat_kernel,
        out_shape=jax.ShapeDtypeStruct((T, HC * HD), F32),
        grid=(DEC_BATCH, nb),
        in_specs=[pl.BlockSpec(memory_space=pl.ANY),
                  pl.BlockSpec((QBLOCK, HC * HD), lambda b, j: (blk(b, j), 0)),
                  pl.BlockSpec((QBLOCK, 256), lambda b, j: (prev(b, j), kcol)),
                  pl.BlockSpec((QBLOCK, 256), lambda b, j: (blk(b, j), kcol)),
                  pl.BlockSpec((QBLOCK, 256), lambda b, j: (nxt(b, j), kcol)),
                  pl.BlockSpec((QBLOCK, 256), lambda b, j: (prev(b, j), vcol)),
                  pl.BlockSpec((QBLOCK, 256), lambda b, j: (blk(b, j), vcol)),
                  pl.BlockSpec((QBLOCK, 256), lambda b, j: (nxt(b, j), vcol)),
                  cache_spec, cache_spec,
                  pl.BlockSpec(memory_space=pltpu.SMEM)],
        out_specs=pl.BlockSpec((QBLOCK, HC * HD), lambda b, j: (blk(b, j), 0)),
        input_output_aliases={0: 0},
        compiler_params=_params(2),
        name="attn_latent",
    )(ctx_out, qkv, qkv, qkv, qkv, qkv, qkv, qkv, cache_k, cache_v, sink)


def _router_kernel(x_ref, sh_ref, sc_ref, w_ref, b_ref, o_ref):
    xm = x_ref[...] * (1.0 + sc_ref[0]) + sh_ref[0]
    lg = _mm(xm, w_ref[...], HI) + b_ref[...]
    lane = lax.broadcasted_iota(jnp.int32, lg.shape, 1)
    neg = -jnp.inf
    big = 1 << 20
    is_grp = jnp.logical_and(lane >= N_EXPERTS, lane < N_EXPERTS + N_GROUPS)
    mg = jnp.max(jnp.where(is_grp, lg, neg), axis=1, keepdims=True)
    g_lane = jnp.min(jnp.where(jnp.logical_and(is_grp, lg == mg), lane, big), axis=1, keepdims=True)
    g_w = 1.0 / jnp.sum(jnp.where(is_grp, jnp.exp(lg - mg), 0.0), axis=1, keepdims=True)
    g_idx = g_lane - N_EXPERTS
    in_grp = jnp.logical_and(lane >= g_idx * EPG, lane < (g_idx + 1) * EPG)
    v1 = jnp.max(jnp.where(in_grp, lg, neg), axis=1, keepdims=True)
    i1 = jnp.min(jnp.where(jnp.logical_and(in_grp, lg == v1), lane, big), axis=1, keepdims=True)
    rest = jnp.logical_and(in_grp, lane != i1)
    v2 = jnp.max(jnp.where(rest, lg, neg), axis=1, keepdims=True)
    i2 = jnp.min(jnp.where(jnp.logical_and(rest, lg == v2), lane, big), axis=1, keepdims=True)
    e2 = jnp.exp(v2 - v1)
    p1 = 1.0 / (1.0 + e2)
    p2 = e2 / (1.0 + e2)
    o_ref[...] = jnp.where(lane == i1, p1 * g_w, jnp.where(lane == i2, p2 * g_w, 0.0))


def _router(x, mods, layer, w, b):
    tm = 512
    return pl.pallas_call(
        _router_kernel,
        out_shape=jax.ShapeDtypeStruct((T, 128), F32),
        grid=(T // tm,),
        in_specs=[pl.BlockSpec((tm, D), lambda i: (i, 0)),
                  _mod_spec(layer, 3, tm), _mod_spec(layer, 4, tm),
                  pl.BlockSpec((D, 128), lambda i: (0, 0)), pl.BlockSpec((1, 128), lambda i: (0, 0))],
        out_specs=pl.BlockSpec((tm, 128), lambda i: (i, 0)),
        compiler_params=_params(1),
        name="moe_router",
    )(x, mods, mods, w, b)


def _moe_kernel(x_ref, sh_ref, sc_ref, gates_ref, wg_ref, wu_ref, wd_ref, gate_ref, g_ref, b_ref, o_ref,
                xm_s, acc_s):
    e = pl.program_id(1)

    @pl.when(e == 0)
    def _():
        xm_s[...] = _bf(x_ref[...] * (1.0 + sc_ref[0]) + sh_ref[0])
        acc_s[...] = jnp.zeros_like(acc_s)

    xm = xm_s[...]
    a = _mm(xm, wg_ref[0])
    u = _mm(xm, wu_ref[0])
    gates = gates_ref[...]
    lane = lax.broadcasted_iota(jnp.int32, gates.shape, 1)
    ge = jnp.sum(jnp.where(lane == e, gates, 0.0), axis=1, keepdims=True)
    hid = (a * _sigmoid(a)) * u * ge
    acc_s[...] += _mm(_bf(hid), wd_ref[0])

    @pl.when(e == N_EXPERTS - 1)
    def _():
        o_ref[...] = _residual_ln(x_ref[...], gate_ref[0], acc_s[...], g_ref[...], b_ref[...])


def _moe(x, mods, layer, gates, wg, wu, wd, ln_g, ln_b):
    tm = 512
    row = lambda i, e: (i, 0)
    const2 = lambda i, e: (0, 0)
    return pl.pallas_call(
        _moe_kernel,
        out_shape=jax.ShapeDtypeStruct((T, D), F32),
        grid=(T // tm, N_EXPERTS),
        in_specs=[pl.BlockSpec((tm, D), row), _mod_spec(layer, 3, tm), _mod_spec(layer, 4, tm),
                  pl.BlockSpec((tm, 128), row),
                  pl.BlockSpec((1, D, EXPERT_FF), lambda i, e: (e, 0, 0)),
                  pl.BlockSpec((1, D, EXPERT_FF), lambda i, e: (e, 0, 0)),
                  pl.BlockSpec((1, EXPERT_FF, D), lambda i, e: (e, 0, 0)),
                  _mod_spec(layer, 5, tm), pl.BlockSpec((1, D), const2), pl.BlockSpec((1, D), const2)],
        out_specs=pl.BlockSpec((tm, D), row),
        scratch_shapes=[pltpu.VMEM((tm, D), BF16), pltpu.VMEM((tm, D), F32)],
        compiler_params=_params(2),
        name="moe_experts",
    )(x, mods, mods, gates, wg, wu, wd, mods, ln_g, ln_b)


def _permute_even_w(w):
    a_end = 4 * HA * DH
    g_end = a_end + 4 * HA
    c_end = g_end + 3 * HB * DH
    z_end = c_end + HB * DH
    small = jnp.concatenate([w[:, a_end:g_end], w[:, z_end:]], axis=1)
    pad = jnp.zeros((w.shape[0], 128 - small.shape[1]), w.dtype)
    return jnp.concatenate([w[:, :a_end], w[:, g_end:c_end], w[:, c_end:z_end], small, pad], axis=1)


def _lane_row(vals, offset):
    return jnp.zeros((1, 128), F32).at[0, offset:offset + vals.shape[0]].set(vals.astype(F32))


def _pair_rows(vals, offset):
    v32 = jnp.zeros((32,), F32).at[offset:offset + vals.shape[0]].set(vals.astype(F32))
    return jnp.repeat(v32.reshape(16, 2), CHUNK, axis=1)


def _sub_col(vals, offset, fill=0.0):
    return jnp.full((32, 1), fill, F32).at[offset:offset + vals.shape[0], 0].set(vals.astype(F32))


def kernel(x_prompt, x_sample, c, c_ctx, state_mlstm_c, state_mlstm_n, state_mlstm_m, state_delta, cache_k, cache_v, w_mod, b_mod, ln_g, ln_b, w_in_even, mlstm_gate_b, mlstm_norm_g, delta_conv_w, delta_a_log, delta_dt_bias, delta_norm_g, w_out_even, w_qkv_odd, attn_sink, w_out_odd, w_grp, b_grp, w_erouter, b_erouter, w_gate, w_up, w_down):
    x = jnp.concatenate([x_prompt.reshape(NP_ROWS, D), x_sample.reshape(NS_ROWS, D)], axis=0)
    cvecs = jnp.concatenate([c_ctx[None, :], c, jnp.zeros((N_MOD_ROWS - 1 - DEC_BATCH, D), F32)], axis=0)
    mods = _modulation(cvecs, w_mod, b_mod)
    tables = _rope_tables()
    m0_all = jnp.broadcast_to(state_mlstm_m.reshape(DEC_BATCH, N_EVEN, 2 * HA, 1), (DEC_BATCH, N_EVEN, 2 * HA, 128))

    out_mc, out_mn, out_mm, out_ds, out_k, out_v = [], [], [], [], [], []
    for l in range(DEPTH):
        if l % 2 == 0:
            e = l // 2
            proj = _even_proj(x, mods, l, _bf(_permute_even_w(w_in_even[e])))
            gates_t = proj[:, EVEN_W - 128:EVEN_W - 96].reshape(T // CHUNK, CHUNK, 32).transpose(0, 2, 1)
            gb = mlstm_gate_b[e].reshape(-1)
            hf, hb, mc, mn, mm = _mlstm_scan(proj, gates_t, _lane_row(gb, 0), _sub_col(gb, 0),
                                             state_mlstm_c, state_mlstm_n, m0_all, e)
            qkv = _delta_prep(proj, delta_conv_w[e])
            dtb = delta_dt_bias[e].reshape(-1)
            nea = -jnp.exp(delta_a_log[e].astype(F32)).reshape(-1)
            of, ob, ds = _delta_scan(qkv, proj, gates_t.reshape(T // CHUNK, 16, 128),
                                     _lane_row(dtb, 24), _pair_rows(dtb, 24),
                                     _lane_row(nea, 24), _pair_rows(nea, 24), state_delta, e)
            x = _even_out(hf, hb, of, ob, proj, mlstm_norm_g[e][None, :], delta_norm_g[e][None, :],
                          _bf(w_out_even[e]), x, mods, l, ln_g[l, 0][None, :], ln_b[l, 0][None, :])
            out_mc.append(mc)
            out_mn.append(mn)
            out_mm.append(mm[:, :, 0].reshape(BATCH, 2, HA))
            out_ds.append(ds)
        else:
            o = l // 2
            qkv = _odd_proj(x, mods, l, _bf(w_qkv_odd[o]), tables)
            a = _attn_context(qkv, attn_sink[o])
            a = _attn_latent(a, qkv, cache_k, cache_v, attn_sink[o], o)
            x = _odd_out(a, _bf(w_out_odd[o]), x, mods, l, ln_g[l, 0][None, :], ln_b[l, 0][None, :])
            kp = qkv[:NP_ROWS, HC * HD:(HC + KVH) * HD].reshape(BATCH, SEQ, KVH, HD).transpose(0, 2, 1, 3)
            vp = qkv[:NP_ROWS, (HC + KVH) * HD:].reshape(BATCH, SEQ, KVH, HD).transpose(0, 2, 1, 3)
            out_k.append(kp)
            out_v.append(vp)
        w_r = jnp.concatenate([w_erouter[l].transpose(1, 0, 2).reshape(D, N_EXPERTS), w_grp[l],
                               jnp.zeros((D, 128 - N_EXPERTS - N_GROUPS), F32)], axis=1)
        b_r = jnp.concatenate([b_erouter[l].reshape(-1), b_grp[l],
                               jnp.zeros((128 - N_EXPERTS - N_GROUPS,), F32)])[None, :]
        gates = _router(x, mods, l, w_r, b_r)
        x = _moe(x, mods, l, gates, _bf(w_gate[l]), _bf(w_up[l]), _bf(w_down[l]),
                 ln_g[l, 1][None, :], ln_b[l, 1][None, :])
    return (x[:NP_ROWS].reshape(BATCH, SEQ, D), x[NP_ROWS:].reshape(DEC_BATCH, DEC_SEQ, D),
            jnp.stack(out_mc, 1), jnp.stack(out_mn, 1), jnp.stack(out_mm, 1), jnp.stack(out_ds, 1),
            jnp.stack(out_k, 1), jnp.stack(out_v, 1))
```

```python
import functools

import jax
import jax.numpy as jnp
from jax import lax
from jax.experimental import pallas as pl
from jax.experimental.pallas import tpu as pltpu

F32 = jnp.float32
BF16 = jnp.bfloat16
HI = lax.Precision.HIGHEST

D = 1024
BATCH = 32
SEQ = 256
DEPTH = 4
DEC_BATCH = 2
DEC_SEQ = 4096
PAST_LEN = 512
GRID_W = 64
N_EVEN = 2
N_ODD = 2
HA = 4
HB = 4
DH = 128
CHUNK = 64
HC = 16
KVH = 4
HD = 64
WINDOW = 128
QBLOCK = 128
ROPE_THETA = 10000.0
N_GROUPS = 4
EPG = 4
N_EXPERTS = 16
EXPERT_FF = 256
DN_ALPHA = (2 * DEPTH) ** 0.25
LN_EPS = 1e-5

NP_ROWS = BATCH * SEQ
NS_ROWS = DEC_BATCH * DEC_SEQ
T = NP_ROWS + NS_ROWS
N_MOD_ROWS = 8
EVEN_W = 4224
QKV_W = (HC + 2 * KVH) * HD

P_CHUNKS = SEQ // CHUNK
S_CHUNKS = DEC_SEQ // CHUNK
P_STEPS = BATCH * P_CHUNKS
S_STEPS = DEC_BATCH * S_CHUNKS
P_BLOCKS = NP_ROWS // CHUNK

VMEM_LIMIT = 48 * 1024 * 1024


def _params(n_axes):
    return pltpu.CompilerParams(dimension_semantics=("arbitrary",) * n_axes,
                                vmem_limit_bytes=VMEM_LIMIT)


def _mm(a, b, prec=None):
    return lax.dot_general(a, b, (((1,), (0,)), ((), ())), precision=prec, preferred_element_type=F32)


def _mm_nt(a, b, prec=None):
    return lax.dot_general(a, b, (((1,), (1,)), ((), ())), precision=prec, preferred_element_type=F32)


def _mm_tn(a, b, prec=None):
    return lax.dot_general(a, b, (((0,), (0,)), ((), ())), precision=prec, preferred_element_type=F32)


def _bf(x):
    return x.astype(BF16)


def _sigmoid(x):
    return 1.0 / (1.0 + jnp.exp(-x))


def _softplus(x):
    return jnp.maximum(x, 0.0) + jnp.log1p(jnp.exp(-jnp.abs(x)))


def _log_sigmoid(x):
    return jnp.minimum(x, 0.0) - jnp.log1p(jnp.exp(-jnp.abs(x)))


def _mod_row(tile, tm):
    npt = NP_ROWS // tm
    per = DEC_SEQ // tm
    return jnp.where(tile < npt, 0, 1 + (tile - npt) // per)


def _mod_spec(layer, chunk, tm):
    def imap(i, *_):
        return ((layer * N_MOD_ROWS + _mod_row(i, tm)) * 6 + chunk, 0, 0)
    return pl.BlockSpec((1, 1, D), imap)


def _modulation_kernel(c_ref, w_ref, b_ref, o_ref):
    x = c_ref[...]
    s = x * _sigmoid(x)
    o_ref[0] = _mm(s, w_ref[0], HI) + b_ref[0]


def _modulation(cvecs, w_mod, b_mod):
    out = pl.pallas_call(
        _modulation_kernel,
        out_shape=jax.ShapeDtypeStruct((DEPTH, N_MOD_ROWS, 6 * D), F32),
        grid=(DEPTH, 6),
        in_specs=[pl.BlockSpec((N_MOD_ROWS, D), lambda l, j: (0, 0)),
                  pl.BlockSpec((1, D, D), lambda l, j: (l, 0, j)),
                  pl.BlockSpec((1, 1, D), lambda l, j: (l * 6 + j, 0, 0))],
        out_specs=pl.BlockSpec((1, N_MOD_ROWS, D), lambda l, j: (l, 0, j)),
        compiler_params=_params(2),
        name="modulation",
    )(cvecs, w_mod, b_mod.reshape(DEPTH * 6, 1, D))
    return out.reshape(DEPTH * N_MOD_ROWS * 6, 1, D)


def _proj_kernel(x_ref, sh_ref, sc_ref, w_ref, o_ref):
    xm = x_ref[...] * (1.0 + sc_ref[0]) + sh_ref[0]
    o_ref[...] = _mm(_bf(xm), w_ref[...])


def _even_proj(x, mods, layer, w):
    tm = 256
    return pl.pallas_call(
        _proj_kernel,
        out_shape=jax.ShapeDtypeStruct((T, EVEN_W), F32),
        grid=(T // tm,),
        in_specs=[pl.BlockSpec((tm, D), lambda i: (i, 0)),
                  _mod_spec(layer, 0, tm), _mod_spec(layer, 1, tm),
                  pl.BlockSpec((D, EVEN_W), lambda i: (0, 0))],
        out_specs=pl.BlockSpec((tm, EVEN_W), lambda i: (i, 0)),
        compiler_params=_params(1),
        name="even_proj",
    )(x, mods, mods, w)


def _qkv_kernel(x_ref, sh_ref, sc_ref, w_ref, cos_ref, sa_ref, sb_ref, o_ref, *, tm):
    i = pl.program_id(0)
    xm = x_ref[...] * (1.0 + sc_ref[0]) + sh_ref[0]
    acc = _mm(_bf(xm), w_ref[...])
    n_q = HC * HD // 128
    n_k = KVH * HD // 128
    is_latent = i >= NP_ROWS // tm
    cos = jnp.where(is_latent, cos_ref[...], 1.0)
    sa = jnp.where(is_latent, sa_ref[...], 0.0)
    sb = jnp.where(is_latent, sb_ref[...], 0.0)
    for g in range(n_q + n_k):
        blk = acc[:, g * 128:(g + 1) * 128]
        if g < n_q:
            blk = blk * (HD ** -0.5)
        rot = blk * cos + pltpu.roll(blk, 112, 1) * sa + pltpu.roll(blk, 16, 1) * sb
        o_ref[:, g * 128:(g + 1) * 128] = rot
    o_ref[:, (n_q + n_k) * 128:] = acc[:, (n_q + n_k) * 128:]


def _rope_tables():
    half = HD // 4
    inv = ROPE_THETA ** (-jnp.arange(half, dtype=F32) / half)
    pos = jnp.arange(DEC_SEQ)
    row = (pos // GRID_W).astype(F32)[:, None] * inv[None, :]
    col = (pos % GRID_W).astype(F32)[:, None] * inv[None, :]
    cos = jnp.concatenate([jnp.cos(row), jnp.cos(row), jnp.cos(col), jnp.cos(col)], axis=-1)
    sin = jnp.concatenate([jnp.sin(row), jnp.sin(row), jnp.sin(col), jnp.sin(col)], axis=-1)
    first = (jnp.arange(HD) % 32) < 16
    sa = jnp.where(first, -sin, 0.0)
    sb = jnp.where(first, 0.0, sin)
    tile2 = lambda t: jnp.concatenate([t, t], axis=-1)
    return tile2(cos), tile2(sa), tile2(sb)


def _odd_proj(x, mods, layer, w, tables):
    tm = 256
    npt = NP_ROWS // tm
    per = DEC_SEQ // tm
    tab_spec = pl.BlockSpec((tm, 128), lambda i: (jnp.where(i < npt, 0, (i - npt) % per), 0))
    return pl.pallas_call(
        functools.partial(_qkv_kernel, tm=tm),
        out_shape=jax.ShapeDtypeStruct((T, QKV_W), F32),
        grid=(T // tm,),
        in_specs=[pl.BlockSpec((tm, D), lambda i: (i, 0)),
                  _mod_spec(layer, 0, tm), _mod_spec(layer, 1, tm),
                  pl.BlockSpec((D, QKV_W), lambda i: (0, 0)),
                  tab_spec, tab_spec, tab_spec],
        out_specs=pl.BlockSpec((tm, QKV_W), lambda i: (i, 0)),
        compiler_params=_params(1),
        name="odd_qkv_proj",
    )(x, mods, mods, w, *tables)


PREP_ROWS = 256


def _delta_prep_kernel(x_ref, prev_ref, next_ref, w_ref, o_ref):
    i = pl.program_id(0)
    part = pl.program_id(1)
    npb = NP_ROWS // PREP_ROWS
    per = DEC_SEQ // PREP_ROWS
    is_latent = i >= npb
    pos = (i - npb) % per
    has_prev = jnp.logical_and(is_latent, pos > 0)
    has_next = jnp.logical_and(is_latent, pos < per - 1)
    x = x_ref[...]
    w = w_ref[...]
    rows = lax.broadcasted_iota(jnp.int32, x.shape, 0)
    prev_row = jnp.where(has_prev, prev_ref[7:8, :], 0.0)
    next_row = jnp.where(has_next, next_ref[0:1, :], 0.0)
    xm1 = jnp.where(rows == 0, prev_row, pltpu.roll(x, 1, 0))
    xp1 = jnp.where(rows == PREP_ROWS - 1, next_row, pltpu.roll(x, PREP_ROWS - 1, 0))
    y = xm1 * w[0:1, :] + x * w[1:2, :] + xp1 * w[2:3, :]
    y = y * _sigmoid(y)
    q_scale = jnp.where(part == 0, DH ** -0.5, 1.0)
    for h in range(4):
        yh = y[:, h * DH:(h + 1) * DH]
        inv = lax.rsqrt(jnp.sum(yh * yh, axis=-1, keepdims=True) + 1e-6)
        scale = jnp.where(part == 2, 1.0, inv * q_scale)
        o_ref[:, h * DH:(h + 1) * DH] = yh * scale


def _delta_prep(proj, conv_w):
    nblk = T // PREP_ROWS
    sub = PREP_ROWS // 8
    last8 = T // 8 - 1
    return pl.pallas_call(
        _delta_prep_kernel,
        out_shape=jax.ShapeDtypeStruct((T, 3 * 512), F32),
        grid=(nblk, 3),
        in_specs=[pl.BlockSpec((PREP_ROWS, 512), lambda i, p: (i, 4 + p)),
                  pl.BlockSpec((8, 512), lambda i, p: (jnp.maximum(i * sub - 1, 0), 4 + p)),
                  pl.BlockSpec((8, 512), lambda i, p: (jnp.minimum((i + 1) * sub, last8), 4 + p)),
                  pl.BlockSpec((3, 512), lambda i, p: (0, p))],
        out_specs=pl.BlockSpec((PREP_ROWS, 512), lambda i, p: (i, p)),
        compiler_params=_params(2),
        name="delta_prep",
    )(proj, proj, proj, conv_w)


def _scan_decode(s):
    is_s = s >= P_STEPS
    sp = s - P_STEPS
    b = jnp.where(is_s, sp // S_CHUNKS, s // P_CHUNKS)
    j = jnp.where(is_s, sp % S_CHUNKS, s % P_CHUNKS)
    nc = jnp.where(is_s, S_CHUNKS, P_CHUNKS)
    base = jnp.where(is_s, P_BLOCKS + b * S_CHUNKS, b * P_CHUNKS)
    return is_s, b, j, nc, base


def _fwd_blk(s):
    _, _, j, _, base = _scan_decode(s)
    return base + j


def _bwd_blk(s):
    _, _, j, nc, base = _scan_decode(s)
    return base + nc - 1 - j


def _state_in_idx(s):
    is_s, b, _, _, _ = _scan_decode(s)
    return jnp.where(is_s, b, 0)


def _state_out_idx(s):
    is_s, b, _, _, _ = _scan_decode(s)
    return jnp.where(is_s, BATCH - 1, b)


def _mlstm_kernel(xf_ref, xb_ref, gcf_ref, gcb_ref, grf_ref, grb_ref,
                  brow_ref, b2_ref, c0_ref, n0_ref, m0_ref,
                  hf_ref, hb_ref, c_out, n_out, m_out, c_s, n_s, m_s):
    s = pl.program_id(0)
    is_s, _, j, nc, _ = _scan_decode(s)

    @pl.when(jnp.logical_and(j == 0, jnp.logical_not(is_s)))
    def _():
        c_s[...] = jnp.zeros_like(c_s)
        n_s[...] = jnp.zeros_like(n_s)
        m_s[...] = jnp.zeros_like(m_s)

    @pl.when(jnp.logical_and(j == 0, is_s))
    def _():
        for d in range(2):
            for h in range(HA):
                c_s[d * HA + h] = c0_ref[0, 0, d, h]
            n_s[d * HA:(d + 1) * HA, :] = n0_ref[0, 0, d]
        m_s[...] = m0_ref[0, 0]

    row = lax.broadcasted_iota(jnp.int32, (CHUNK, 128), 0)
    lane = lax.broadcasted_iota(jnp.int32, (CHUNK, 128), 1)
    left = lane < CHUNK
    left1 = left[0:1, :]
    lcol = jnp.where(left, lane, lane - CHUNK)
    r64 = lax.broadcasted_iota(jnp.int32, (CHUNK, CHUNK), 0)
    c64 = lax.broadcasted_iota(jnp.int32, (CHUNK, CHUNK), 1)
    r128 = lax.broadcasted_iota(jnp.int32, (128, 128), 0)
    c128 = lax.broadcasted_iota(jnp.int32, (128, 128), 1)
    same_half = (r128 < CHUNK) == (c128 < CHUNK)
    neg = -jnp.inf
    ins = ((xf_ref, gcf_ref, grf_ref, hf_ref), (xb_ref, gcb_ref, grb_ref, hb_ref))

    gate = []
    for d in range(2):
        gc_ref, gr_ref = ins[d][1], ins[d][2]
        if d == 0:
            incl_p, tri_c, tri_r = lcol <= row, c64 <= r64, jnp.logical_and(same_half, r128 <= c128)
        else:
            incl_p, tri_c, tri_r = lcol >= row, c64 >= r64, jnp.logical_and(same_half, r128 >= c128)
        gc = gc_ref[...] + brow_ref[...]
        gr = gr_ref[0] + b2_ref[...]
        cs_col = _prefix_cols(tri_c.astype(BF16), _log_sigmoid(gc))
        cs_row = _prefix_rows(_log_sigmoid(gr), tri_r.astype(BF16))
        gate.append((incl_p, gc, gr, cs_col, cs_row))

    st = []
    for d in range(2):
        for hp in range(HA // 2):
            x_ref = ins[d][0]
            incl_p, gc, gr, cs_col, cs_row = gate[d]
            end = CHUNK - 1 if d == 0 else 0
            heads = []
            for h in (2 * hp, 2 * hp + 1):
                c = d * HA + h
                ci = d * 8 + h
                cf = d * 8 + 4 + h
                b_col = cs_col[:, cf:cf + 1]
                heads.append(dict(c=c, h=h, q=x_ref[:, h * DH:(h + 1) * DH],
                                  k=x_ref[:, 512 + h * DH:512 + (h + 1) * DH] * (DH ** -0.5),
                                  v=x_ref[:, 1024 + h * DH:1024 + (h + 1) * DH],
                                  i_col=gc[:, ci:ci + 1], b_col=b_col, b_last=b_col[end:end + 1, :],
                                  m=m_s[c:c + 1, 0:1]))
            ha, hb_ = heads
            i_row = gr[d * 4 + hp:d * 4 + hp + 1, :]
            b_row = cs_row[d * 4 + 2 + hp:d * 4 + 3 + hp, :]
            b_col_p = jnp.where(left, ha["b_col"], hb_["b_col"])
            m_p = jnp.where(left1, m_s[ha["c"]:ha["c"] + 1, :], m_s[hb_["c"]:hb_["c"] + 1, :])
            dmat = jnp.where(incl_p, b_col_p - b_row + i_row, neg)
            inter = b_col_p + m_p
            dmax = jnp.where(left, jnp.max(jnp.where(left, dmat, neg), axis=1, keepdims=True),
                             jnp.max(jnp.where(left, neg, dmat), axis=1, keepdims=True))
            m_t = jnp.maximum(inter, dmax)
            q_cat = _bf(jnp.concatenate([ha["q"], hb_["q"]], axis=1))
            k_bd = _bf(_block_diag2(ha["k"], hb_["k"]))
            sc = _mm_nt(q_cat, k_bd) * jnp.exp(dmat - m_t)
            st.append(dict(d=d, heads=heads, sc=sc, w_inter=jnp.exp(inter - m_t), m_t=m_t))

    for p in st:
        ha, hb_ = p["heads"]
        h_ref = ins[p["d"]][3]
        sc = p["sc"]
        sv = _mm(_bf(sc), _bf(_block_diag2(ha["v"], hb_["v"])))
        dens = (jnp.sum(jnp.where(left, sc, 0.0), axis=1, keepdims=True),
                jnp.sum(jnp.where(left, 0.0, sc), axis=1, keepdims=True))
        for idx, hd in enumerate((ha, hb_)):
            c, h, q, k = hd["c"], hd["h"], hd["q"], hd["k"]
            w_inter = p["w_inter"][:, idx * CHUNK:idx * CHUNK + 1]
            m_t = p["m_t"][:, idx * CHUNK:idx * CHUNK + 1]
            c_mat = c_s[c]
            nrm = n_s[c:c + 1, :]
            num = sv[:, idx * DH:(idx + 1) * DH] + w_inter * _mm(_bf(q), _bf(c_mat))
            den = dens[idx] + w_inter * jnp.sum(q * nrm, axis=1, keepdims=True)
            h_ref[:, h * DH:(h + 1) * DH] = num / jnp.maximum(jnp.abs(den), jnp.exp(-m_t))
            g_end = hd["b_last"] - hd["b_col"] + hd["i_col"]
            m_new = jnp.maximum(hd["b_last"] + hd["m"], jnp.max(g_end, axis=0, keepdims=True))
            kw = k * jnp.exp(g_end - m_new)
            decay = jnp.exp(hd["b_last"] + hd["m"] - m_new)
            c_s[c] = decay * c_mat + _mm_tn(_bf(kw), _bf(hd["v"]))
            n_s[c:c + 1, :] = decay * nrm + jnp.sum(kw, axis=0, keepdims=True)
            m_s[c:c + 1, :] = jnp.broadcast_to(m_new, (1, 128))

    @pl.when(jnp.logical_and(j == nc - 1, jnp.logical_not(is_s)))
    def _():
        for d in range(2):
            for h in range(HA):
                c_out[0, d, h] = c_s[d * HA + h]
            n_out[0, d] = n_s[d * HA:(d + 1) * HA, :]
        m_out[0] = m_s[...]


def _scan_specs():
    gcol = EVEN_W // 128 - 1
    return [pl.BlockSpec((CHUNK, 3 * 512), lambda s: (_fwd_blk(s), 0)),
            pl.BlockSpec((CHUNK, 3 * 512), lambda s: (_bwd_blk(s), 0)),
            pl.BlockSpec((CHUNK, 128), lambda s: (_fwd_blk(s), gcol)),
            pl.BlockSpec((CHUNK, 128), lambda s: (_bwd_blk(s), gcol)),
            pl.BlockSpec((1, 16, 128), lambda s: (_fwd_blk(s), 0, 0)),
            pl.BlockSpec((1, 16, 128), lambda s: (_bwd_blk(s), 0, 0))]


def _mlstm_scan(proj, gates_t2, bias_row, bias2, c0, n0, m0, e):
    const2 = lambda s: (0, 0)
    in_specs = (_scan_specs()
                + [pl.BlockSpec((1, 128), const2), pl.BlockSpec((16, 128), const2),
                   pl.BlockSpec((1, 1, 2, HA, DH, DH), lambda s: (_state_in_idx(s), e, 0, 0, 0, 0)),
                   pl.BlockSpec((1, 1, 2, HA, DH), lambda s: (_state_in_idx(s), e, 0, 0, 0)),
                   pl.BlockSpec((1, 1, 2 * HA, 128), lambda s: (_state_in_idx(s), e, 0, 0))])
    out_shape = (jax.ShapeDtypeStruct((T, HA * DH), F32), jax.ShapeDtypeStruct((T, HA * DH), F32),
                 jax.ShapeDtypeStruct((BATCH, 2, HA, DH, DH), F32),
                 jax.ShapeDtypeStruct((BATCH, 2, HA, DH), F32),
                 jax.ShapeDtypeStruct((BATCH, 2 * HA, 128), F32))
    out_specs = (pl.BlockSpec((CHUNK, 512), lambda s: (_fwd_blk(s), 0)),
                 pl.BlockSpec((CHUNK, 512), lambda s: (_bwd_blk(s), 0)),
                 pl.BlockSpec((1, 2, HA, DH, DH), lambda s: (_state_out_idx(s), 0, 0, 0, 0)),
                 pl.BlockSpec((1, 2, HA, DH), lambda s: (_state_out_idx(s), 0, 0, 0)),
                 pl.BlockSpec((1, 2 * HA, 128), lambda s: (_state_out_idx(s), 0, 0)))
    return pl.pallas_call(
        _mlstm_kernel,
        out_shape=out_shape,
        grid=(P_STEPS + S_STEPS,),
        in_specs=in_specs,
        out_specs=out_specs,
        scratch_shapes=[pltpu.VMEM((2 * HA, DH, DH), F32), pltpu.VMEM((2 * HA, DH), F32),
                        pltpu.VMEM((2 * HA, 128), F32)],
        compiler_params=_params(1),
        name="mlstm_scan",
    )(proj, proj, proj, proj, gates_t2, gates_t2, bias_row, bias2, c0, n0, m0)


def _split2(x):
    hi = _bf(x)
    return hi, _bf(x - hi.astype(F32))


def _split3(x):
    h1 = _bf(x)
    r1 = x - h1.astype(F32)
    h2 = _bf(r1)
    return h1, h2, _bf(r1 - h2.astype(F32))


def _lhs3(x):
    hi, lo = _split2(x)
    return jnp.concatenate([hi, lo, hi], axis=1)


def _rhs3(x):
    hi, lo = _split2(x)
    return jnp.concatenate([hi, hi, lo], axis=0)


def _prefix_cols(tri_bf, x):
    n = x.shape[1]
    r = _mm(tri_bf, jnp.concatenate(_split3(x), axis=1))
    return r[:, :n] + r[:, n:2 * n] + r[:, 2 * n:]


def _prefix_rows(x, tri_bf):
    m = x.shape[0]
    r = _mm(jnp.concatenate(_split3(x), axis=0), tri_bf)
    return r[:m] + r[m:2 * m] + r[2 * m:]


LEVELS = tuple(range(6))


def _block_diag2(a, b):
    z = jnp.zeros_like(a)
    return jnp.concatenate([jnp.concatenate([a, z], axis=1), jnp.concatenate([z, b], axis=1)], axis=0)


def _delta_kernel(xf_ref, xb_ref, gcf_ref, gcb_ref, grf_ref, grb_ref,
                  dtrow_ref, dt2_ref, narow_ref, na2_ref, s0_ref,
                  of_ref, ob_ref, s_out, s_s):
    s = pl.program_id(0)
    is_s, _, j, nc, _ = _scan_decode(s)

    @pl.when(jnp.logical_and(j == 0, jnp.logical_not(is_s)))
    def _():
        s_s[...] = jnp.zeros_like(s_s)

    @pl.when(jnp.logical_and(j == 0, is_s))
    def _():
        for d in range(2):
            for h in range(HB):
                s_s[d * HB + h] = s0_ref[0, 0, d, h]

    row = lax.broadcasted_iota(jnp.int32, (CHUNK, 128), 0)
    lane = lax.broadcasted_iota(jnp.int32, (CHUNK, 128), 1)
    left = lane < CHUNK
    lcol = jnp.where(left, lane, lane - CHUNK)
    eye_p = (lcol == row).astype(F32)
    r64 = lax.broadcasted_iota(jnp.int32, (CHUNK, CHUNK), 0)
    c64 = lax.broadcasted_iota(jnp.int32, (CHUNK, CHUNK), 1)
    r128 = lax.broadcasted_iota(jnp.int32, (128, 128), 0)
    c128 = lax.broadcasted_iota(jnp.int32, (128, 128), 1)
    same_half = (r128 < CHUNK) == (c128 < CHUNK)
    ins = ((xf_ref, gcf_ref, grf_ref, of_ref), (xb_ref, gcb_ref, grb_ref, ob_ref))

    gate = []
    for d in range(2):
        gc_ref, gr_ref = ins[d][1], ins[d][2]
        if d == 0:
            incl_p, strict_p = lcol <= row, lcol < row
            tri_c, tri_r = c64 <= r64, jnp.logical_and(same_half, r128 <= c128)
        else:
            incl_p, strict_p = lcol >= row, lcol > row
            tri_c, tri_r = c64 >= r64, jnp.logical_and(same_half, r128 >= c128)
        xc = gc_ref[...]
        xr = gr_ref[0]
        la_c = narow_ref[...] * _softplus(xc + dtrow_ref[...])
        la_r = na2_ref[...] * _softplus(xr + dt2_ref[...])
        g_c = _prefix_cols(tri_c.astype(BF16), la_c)
        g_r = _prefix_rows(la_r, tri_r.astype(BF16))
        gate.append((incl_p, strict_p, _sigmoid(xc), g_c, g_r))

    def bd_rhs(x):
        return _rhs3(jnp.concatenate([jnp.where(left, x, 0.0), jnp.where(left, 0.0, x)], axis=0))

    off = ([], [])
    for lv in LEVELS:
        same = jnp.right_shift(row, lv + 1) == jnp.right_shift(lcol, lv + 1)
        r_hi = jnp.bitwise_and(jnp.right_shift(row, lv), 1) == 1
        c_hi = jnp.bitwise_and(jnp.right_shift(lcol, lv), 1) == 1
        off[0].append(same & r_hi & jnp.logical_not(c_hi))
        off[1].append(same & jnp.logical_not(r_hi) & c_hi)

    pairs = [(d, hp) for d in range(2) for hp in range(HB // 2)]
    st = []
    for d, hp in pairs:
        x_ref = ins[d][0]
        incl_p, strict_p, beta_c, g_c, g_r = gate[d]
        end = CHUNK - 1 if d == 0 else 0
        heads = []
        for h in (2 * hp, 2 * hp + 1):
            ib = 16 + d * HB + h
            ia = 24 + d * HB + h
            gcol = g_c[:, ia:ia + 1]
            bc = beta_c[:, ib:ib + 1]
            q = x_ref[:, h * DH:(h + 1) * DH]
            k = x_ref[:, 512 + h * DH:512 + (h + 1) * DH]
            v = x_ref[:, 1024 + h * DH:1024 + (h + 1) * DH]
            heads.append(dict(h=h, gcol=gcol, bc=bc, q=q, k=k, v=v, kb=k * bc, eg=jnp.exp(gcol),
                              g_last=gcol[end:end + 1, :]))
        ha, hb_ = heads
        r = 12 + d * 2 + hp
        gcol_p = jnp.where(left, ha["gcol"], hb_["gcol"])
        decay = jnp.exp(jnp.where(incl_p, gcol_p - g_r[r:r + 1, :], -jnp.inf))
        k_bd = _bf(_block_diag2(ha["k"], hb_["k"]))
        kb_cat = _bf(jnp.concatenate([ha["kb"], hb_["kb"]], axis=1))
        q_cat = _bf(jnp.concatenate([ha["q"], hb_["q"]], axis=1))
        a_mat = jnp.where(strict_p, _mm_nt(kb_cat, k_bd) * decay, 0.0)
        qk = _mm_nt(q_cat, k_bd) * decay
        st.append(dict(d=d, heads=heads, t=eye_p - jnp.where(off[d][0], a_mat, 0.0), qk=qk,
                       am=[bd_rhs(jnp.where(m, a_mat, 0.0)) for m in off[d][1:]]))

    for li in range(len(LEVELS) - 1):
        for p in st:
            p["w"] = _mm(_lhs3(p["t"]), p["am"][li])
        for p in st:
            p["t"] = p["t"] - _mm(_lhs3(p["w"]), bd_rhs(p["t"]))

    for p in st:
        ha, hb_ = p["heads"]
        o_ref = ins[p["d"]][3]
        rhs_a = jnp.concatenate([ha["v"] * ha["bc"], ha["kb"] * ha["eg"]], axis=1)
        rhs_b = jnp.concatenate([hb_["v"] * hb_["bc"], hb_["kb"] * hb_["eg"]], axis=1)
        sol = _mm(_lhs3(p["t"]), _rhs3(_block_diag2(rhs_a, rhs_b)))
        vn = []
        for idx, hd in enumerate((ha, hb_)):
            c = p["d"] * HB + hd["h"]
            s_mat = s_s[c]
            sbf = _bf(s_mat)
            so = sol[:, idx * 2 * DH:(idx + 1) * 2 * DH]
            v_new = so[:, :DH] - _mm(_bf(so[:, DH:]), sbf)
            vn.append(v_new)
            hd["o1"] = _mm(_bf(hd["q"] * hd["eg"]), sbf)
            s_s[c] = (jnp.exp(hd["g_last"]) * s_mat
                      + _mm_tn(_bf(hd["k"] * jnp.exp(hd["g_last"] - hd["gcol"])), _bf(v_new)))
        o2 = _mm(_bf(p["qk"]), _bf(_block_diag2(vn[0], vn[1])))
        for idx, hd in enumerate((ha, hb_)):
            o_ref[:, hd["h"] * DH:(hd["h"] + 1) * DH] = hd["o1"] + o2[:, idx * DH:(idx + 1) * DH]

    @pl.when(jnp.logical_and(j == nc - 1, jnp.logical_not(is_s)))
    def _():
        for d in range(2):
            for h in range(HB):
                s_out[0, d, h] = s_s[d * HB + h]


def _delta_scan(qkv, proj, gates_t2, dt_row, dt2, na_row, na2, s0, e):
    const2 = lambda s: (0, 0)
    in_specs = (_scan_specs()
                + [pl.BlockSpec((1, 128), const2), pl.BlockSpec((16, 128), const2),
                   pl.BlockSpec((1, 128), const2), pl.BlockSpec((16, 128), const2),
                   pl.BlockSpec((1, 1, 2, HB, DH, DH), lambda s: (_state_in_idx(s), e, 0, 0, 0, 0))])
    out_shape = (jax.ShapeDtypeStruct((T, HB * DH), F32), jax.ShapeDtypeStruct((T, HB * DH), F32),
                 jax.ShapeDtypeStruct((BATCH, 2, HB, DH, DH), F32))
    out_specs = (pl.BlockSpec((CHUNK, 512), lambda s: (_fwd_blk(s), 0)),
                 pl.BlockSpec((CHUNK, 512), lambda s: (_bwd_blk(s), 0)),
                 pl.BlockSpec((1, 2, HB, DH, DH), lambda s: (_state_out_idx(s), 0, 0, 0, 0)))
    return pl.pallas_call(
        _delta_kernel,
        out_shape=out_shape,
        grid=(P_STEPS + S_STEPS,),
        in_specs=in_specs,
        out_specs=out_specs,
        scratch_shapes=[pltpu.VMEM((2 * HB, DH, DH), F32)],
        compiler_params=_params(1),
        name="delta_scan",
    )(qkv, qkv, proj, proj, gates_t2, gates_t2, dt_row, dt2, na_row, na2, s0)


def _residual_ln(x, gate, y, g, b):
    r = DN_ALPHA * x + gate * y
    mu = jnp.mean(r, axis=-1, keepdims=True)
    var = jnp.mean(jnp.square(r - mu), axis=-1, keepdims=True)
    return (r - mu) * lax.rsqrt(var + LN_EPS) * g + b


def _even_out_kernel(hf_ref, hb_ref, oa_ref, of_ref, ob_ref, zb_ref, mg_ref, dg_ref,
                     w_ref, x_ref, gate_ref, g_ref, b_ref, o_ref):
    parts = []
    for h in range(HA):
        sl = slice(h * DH, (h + 1) * DH)
        hh = hf_ref[:, sl] + hb_ref[:, sl]
        mu = jnp.mean(hh, axis=-1, keepdims=True)
        var = jnp.mean(jnp.square(hh - mu), axis=-1, keepdims=True)
        parts.append(_sigmoid(oa_ref[:, sl]) * ((hh - mu) * lax.rsqrt(var + LN_EPS) * mg_ref[:, sl]))
    for h in range(HB):
        sl = slice(h * DH, (h + 1) * DH)
        oo = of_ref[:, sl] + ob_ref[:, sl]
        z = zb_ref[:, sl]
        nrm = oo * lax.rsqrt(jnp.mean(jnp.square(oo), axis=-1, keepdims=True) + LN_EPS) * dg_ref[:, sl]
        parts.append(nrm * (z * _sigmoid(z)))
    a = jnp.concatenate(parts, axis=1)
    y = _mm(_bf(a), w_ref[...])
    o_ref[...] = _residual_ln(x_ref[...], gate_ref[0], y, g_ref[...], b_ref[...])


def _even_out(hf, hb, of, ob, proj, mg, dg, w, x, mods, layer, ln_g, ln_b):
    tm = 256
    row512 = lambda i: (i, 0)
    const2 = lambda i: (0, 0)
    return pl.pallas_call(
        _even_out_kernel,
        out_shape=jax.ShapeDtypeStruct((T, D), F32),
        grid=(T // tm,),
        in_specs=[pl.BlockSpec((tm, 512), row512), pl.BlockSpec((tm, 512), row512),
                  pl.BlockSpec((tm, 512), lambda i: (i, 3)),
                  pl.BlockSpec((tm, 512), row512), pl.BlockSpec((tm, 512), row512),
                  pl.BlockSpec((tm, 512), lambda i: (i, 7)),
                  pl.BlockSpec((1, 512), const2), pl.BlockSpec((1, 512), const2),
                  pl.BlockSpec((D, D), const2),
                  pl.BlockSpec((tm, D), row512),
                  _mod_spec(layer, 2, tm),
                  pl.BlockSpec((1, D), const2), pl.BlockSpec((1, D), const2)],
        out_specs=pl.BlockSpec((tm, D), row512),
        compiler_params=_params(1),
        name="even_out_ln",
    )(hf, hb, proj, of, ob, proj, mg, dg, w, x, mods, ln_g, ln_b)


def _odd_out_kernel(ap_ref, as_ref, w_ref, x_ref, gate_ref, g_ref, b_ref, o_ref, *, tm):
    a = jnp.where(pl.program_id(0) < NP_ROWS // tm, ap_ref[...], as_ref[...])
    y = _mm(_bf(a), w_ref[...])
    o_ref[...] = _residual_ln(x_ref[...], gate_ref[0], y, g_ref[...], b_ref[...])


def _odd_out(a_prompt, a_latent, w, x, mods, layer, ln_g, ln_b):
    tm = 256
    npt = NP_ROWS // tm
    row = lambda i: (i, 0)
    const2 = lambda i: (0, 0)
    return pl.pallas_call(
        functools.partial(_odd_out_kernel, tm=tm),
        out_shape=jax.ShapeDtypeStruct((T, D), F32),
        grid=(T // tm,),
        in_specs=[pl.BlockSpec((tm, D), lambda i: (jnp.minimum(i, npt - 1), 0)),
                  pl.BlockSpec((tm, D), lambda i: (jnp.maximum(i - npt, 0), 0)),
                  pl.BlockSpec((D, D), const2), pl.BlockSpec((tm, D), row),
                  _mod_spec(layer, 2, tm), pl.BlockSpec((1, D), const2), pl.BlockSpec((1, D), const2)],
        out_specs=pl.BlockSpec((tm, D), row),
        compiler_params=_params(1),
        name="odd_out_ln",
    )(a_prompt, a_latent, w, x, mods, ln_g, ln_b)


def _attn_ctx_kernel(qkv_ref, sink_ref, o_ref):
    for h in range(HC):
        kv = h // (HC // KVH)
        q = _bf(qkv_ref[:, h * HD:(h + 1) * HD])
        k = _bf(qkv_ref[:, HC * HD + kv * HD:HC * HD + (kv + 1) * HD])
        v = _bf(qkv_ref[:, (HC + KVH) * HD + kv * HD:(HC + KVH) * HD + (kv + 1) * HD])
        sink = sink_ref[h]
        sc = _mm_nt(q, k)
        m = jnp.maximum(jnp.max(sc, axis=1, keepdims=True), sink)
        p = jnp.exp(sc - m)
        den = jnp.sum(p, axis=1, keepdims=True) + jnp.exp(sink - m)
        o_ref[:, h * HD:(h + 1) * HD] = _mm(_bf(p), v) / den


def _attn_context(qkv, sink):
    return pl.pallas_call(
        _attn_ctx_kernel,
        out_shape=jax.ShapeDtypeStruct((NP_ROWS, HC * HD), F32),
        grid=(BATCH,),
        in_specs=[pl.BlockSpec((SEQ, QKV_W), lambda b: (b, 0)),
                  pl.BlockSpec(memory_space=pltpu.SMEM)],
        out_specs=pl.BlockSpec((SEQ, HC * HD), lambda b: (b, 0)),
        compiler_params=_params(1),
        name="attn_context",
    )(qkv, sink)


def _attn_lat_kernel(q_ref, kp_ref, kc_ref, kn_ref, vp_ref, vc_ref, vn_ref, ck_ref, cv_ref,
                     sink_ref, o_ref, bias_s):
    j = pl.program_id(1)
    nb = DEC_SEQ // QBLOCK
    r = lax.broadcasted_iota(jnp.int32, (QBLOCK, 3 * QBLOCK), 0)
    cc = lax.broadcasted_iota(jnp.int32, (QBLOCK, 3 * QBLOCK), 1)
    lo = jnp.where(j >= 1, 0, QBLOCK)
    hi = jnp.where(j <= nb - 2, 3 * QBLOCK, 2 * QBLOCK)
    ok = (jnp.abs(QBLOCK + r - cc) <= WINDOW) & (cc >= lo) & (cc < hi)
    bias_s[...] = jnp.where(ok, 0.0, -jnp.inf)
    for kv in range(KVH):
        ks = slice(kv * HD, (kv + 1) * HD)
        k_all = _bf(jnp.concatenate([ck_ref[0, 0, kv], kp_ref[:, ks], kc_ref[:, ks], kn_ref[:, ks]], axis=0))
        v_all = _bf(jnp.concatenate([cv_ref[0, 0, kv], vp_ref[:, ks], vc_ref[:, ks], vn_ref[:, ks]], axis=0))
        for h in range(kv * (HC // KVH), (kv + 1) * (HC // KVH)):
            sink = sink_ref[h]
            sc = _mm_nt(_bf(q_ref[:, h * HD:(h + 1) * HD]), k_all)
            s_ctx = sc[:, :PAST_LEN]
            s_loc = sc[:, PAST_LEN:] + bias_s[...]
            m = jnp.maximum(jnp.maximum(jnp.max(s_ctx, axis=1, keepdims=True),
                                        jnp.max(s_loc, axis=1, keepdims=True)), sink)
            p_ctx = jnp.exp(s_ctx - m)
            p_loc = jnp.exp(s_loc - m)
            den = (jnp.sum(p_ctx, axis=1, keepdims=True) + jnp.sum(p_loc, axis=1, keepdims=True)
                   + jnp.exp(sink - m))
            num = _mm(_bf(jnp.concatenate([p_ctx, p_loc], axis=1)), v_all)
            o_ref[:, h * HD:(h + 1) * HD] = num / den


def _attn_latent(qkv, cache_k, cache_v, sink, o):
    nb = DEC_SEQ // QBLOCK
    base = NP_ROWS // QBLOCK
    blk = lambda b, j: base + b * nb + j
    prev = lambda b, j: base + b * nb + jnp.maximum(j - 1, 0)
    nxt = lambda b, j: base + b * nb + jnp.minimum(j + 1, nb - 1)
    kcol, vcol = HC * HD // 256, HC * HD // 256 + 1
    cache_spec = pl.BlockSpec((1, 1, KVH, PAST_LEN, HD), lambda b, j: (b, o, 0, 0, 0))
    return pl.pallas_call(
        _attn_lat_kernel,
        out_shape=jax.ShapeDtypeStruct((NS_ROWS, HC * HD), F32),
        grid=(DEC_BATCH, nb),
        in_specs=[pl.BlockSpec((QBLOCK, HC * HD), lambda b, j: (blk(b, j), 0)),
                  pl.BlockSpec((QBLOCK, 256), lambda b, j: (prev(b, j), kcol)),
                  pl.BlockSpec((QBLOCK, 256), lambda b, j: (blk(b, j), kcol)),
                  pl.BlockSpec((QBLOCK, 256), lambda b, j: (nxt(b, j), kcol)),
                  pl.BlockSpec((QBLOCK, 256), lambda b, j: (prev(b, j), vcol)),
                  pl.BlockSpec((QBLOCK, 256), lambda b, j: (blk(b, j), vcol)),
                  pl.BlockSpec((QBLOCK, 256), lambda b, j: (nxt(b, j), vcol)),
                  cache_spec, cache_spec,
                  pl.BlockSpec(memory_space=pltpu.SMEM)],
        out_specs=pl.BlockSpec((QBLOCK, HC * HD), lambda b, j: (b * nb + j, 0)),
        scratch_shapes=[pltpu.VMEM((QBLOCK, 3 * QBLOCK), F32)],
        compiler_params=_params(2),
        name="attn_latent",
    )(qkv, qkv, qkv, qkv, qkv, qkv, qkv, cache_k, cache_v, sink)


def _router_kernel(x_ref, sh_ref, sc_ref, w_ref, b_ref, o_ref):
    xm = x_ref[...] * (1.0 + sc_ref[0]) + sh_ref[0]
    lg = _mm(xm, w_ref[...], HI) + b_ref[...]
    lane = lax.broadcasted_iota(jnp.int32, lg.shape, 1)
    neg = -jnp.inf
    big = 1 << 20
    is_grp = jnp.logical_and(lane >= N_EXPERTS, lane < N_EXPERTS + N_GROUPS)
    mg = jnp.max(jnp.where(is_grp, lg, neg), axis=1, keepdims=True)
    g_lane = jnp.min(jnp.where(jnp.logical_and(is_grp, lg == mg), lane, big), axis=1, keepdims=True)
    g_w = 1.0 / jnp.sum(jnp.where(is_grp, jnp.exp(lg - mg), 0.0), axis=1, keepdims=True)
    g_idx = g_lane - N_EXPERTS
    in_grp = jnp.logical_and(lane >= g_idx * EPG, lane < (g_idx + 1) * EPG)
    v1 = jnp.max(jnp.where(in_grp, lg, neg), axis=1, keepdims=True)
    i1 = jnp.min(jnp.where(jnp.logical_and(in_grp, lg == v1), lane, big), axis=1, keepdims=True)
    rest = jnp.logical_and(in_grp, lane != i1)
    v2 = jnp.max(jnp.where(rest, lg, neg), axis=1, keepdims=True)
    i2 = jnp.min(jnp.where(jnp.logical_and(rest, lg == v2), lane, big), axis=1, keepdims=True)
    e2 = jnp.exp(v2 - v1)
    p1 = 1.0 / (1.0 + e2)
    p2 = e2 / (1.0 + e2)
    o_ref[...] = jnp.where(lane == 0, i1.astype(F32),
                           jnp.where(lane == 1, i2.astype(F32),
                                     jnp.where(lane == 2, p1 * g_w, jnp.where(lane == 3, p2 * g_w, 0.0))))


def _router(x, mods, layer, w, b):
    tm = 512
    return pl.pallas_call(
        _router_kernel,
        out_shape=jax.ShapeDtypeStruct((T, 128), F32),
        grid=(T // tm,),
        in_specs=[pl.BlockSpec((tm, D), lambda i: (i, 0)),
                  _mod_spec(layer, 3, tm), _mod_spec(layer, 4, tm),
                  pl.BlockSpec((D, 128), lambda i: (0, 0)), pl.BlockSpec((1, 128), lambda i: (0, 0))],
        out_specs=pl.BlockSpec((tm, 128), lambda i: (i, 0)),
        compiler_params=_params(1),
        name="moe_router",
    )(x, mods, mods, w, b)


MOE_R = 256
MOE_TILES = 2 * T // MOE_R + N_EXPERTS
MOE_ROWS = MOE_TILES * MOE_R
MOE_TM = 256
N_USED_LANE = 255


def _chosen(meta, lane):
    i1 = meta[:, 0:1].astype(jnp.int32)
    i2 = meta[:, 1:2].astype(jnp.int32)
    return lane == i1, lane == i2


def _tiles_per_expert(cnt):
    return jnp.floor((cnt + (MOE_R - 1.0)) * (1.0 / MOE_R))


def _plan_rank_kernel(meta_ref, rank_ref, cnt_ref, carry_s):
    @pl.when(pl.program_id(0) == 0)
    def _():
        carry_s[...] = jnp.zeros_like(carry_s)

    meta = meta_ref[...]
    tm = meta.shape[0]
    lane = lax.broadcasted_iota(jnp.int32, meta.shape, 1)
    s1, s2 = _chosen(meta, lane)
    sel = jnp.logical_or(s1, s2)
    r = lax.broadcasted_iota(jnp.int32, (tm, tm), 0)
    c = lax.broadcasted_iota(jnp.int32, (tm, tm), 1)
    rank_ref[...] = _mm((c < r).astype(BF16), sel.astype(BF16)) + carry_s[0:1, :]
    carry_s[...] = carry_s[...] + jnp.sum(sel.astype(F32), axis=0, keepdims=True)
    cnt_ref[...] = carry_s[...]


def _plan_rank(meta):
    tm = 512
    return pl.pallas_call(
        _plan_rank_kernel,
        out_shape=(jax.ShapeDtypeStruct((T, 128), F32), jax.ShapeDtypeStruct((8, 128), F32)),
        grid=(T // tm,),
        in_specs=[pl.BlockSpec((tm, 128), lambda i: (i, 0))],
        out_specs=(pl.BlockSpec((tm, 128), lambda i: (i, 0)), pl.BlockSpec((8, 128), lambda i: (0, 0))),
        scratch_shapes=[pltpu.VMEM((8, 128), F32)],
        compiler_params=_params(1),
        name="moe_plan_rank",
    )(meta)


def _plan_dest_kernel(meta_ref, rank_ref, cnt_ref, dest_ref):
    nt = _tiles_per_expert(cnt_ref[...])
    r = lax.broadcasted_iota(jnp.int32, (128, 128), 0)
    c = lax.broadcasted_iota(jnp.int32, (128, 128), 1)
    start = _mm(_bf(nt), (r < c).astype(BF16))
    base = start[0:1, :] * MOE_R + rank_ref[...]
    meta = meta_ref[...]
    lane = lax.broadcasted_iota(jnp.int32, meta.shape, 1)
    s1, s2 = _chosen(meta, lane)
    d1 = jnp.sum(jnp.where(s1, base, 0.0), axis=1, keepdims=True)
    d2 = jnp.sum(jnp.where(s2, base, 0.0), axis=1, keepdims=True)
    dest_ref[...] = jnp.where(lane == 0, d1, jnp.where(lane == 1, d2, 0.0)).astype(jnp.int32)


def _plan_dest(meta, rank, cnt):
    tm = 512
    row = lambda i: (i, 0)
    return pl.pallas_call(
        _plan_dest_kernel,
        out_shape=jax.ShapeDtypeStruct((T, 128), jnp.int32),
        grid=(T // tm,),
        in_specs=[pl.BlockSpec((tm, 128), row), pl.BlockSpec((tm, 128), row),
                  pl.BlockSpec((8, 128), lambda i: (0, 0))],
        out_specs=pl.BlockSpec((tm, 128), row),
        compiler_params=_params(1),
        name="moe_plan_dest",
    )(meta, rank, cnt)


def _plan_tiles_kernel(cnt_ref, te_ref):
    nt = _tiles_per_expert(cnt_ref[...])
    ntb = jnp.broadcast_to(nt[0:1, :], (128, 128))
    r = lax.broadcasted_iota(jnp.int32, (128, 128), 0)
    c = lax.broadcasted_iota(jnp.int32, (128, 128), 1)
    end_col = jnp.sum(jnp.where(c <= r, ntb, 0.0), axis=1, keepdims=True)
    ti = lax.broadcasted_iota(jnp.int32, (128, 256), 1)
    rr = lax.broadcasted_iota(jnp.int32, (128, 256), 0)
    done = jnp.logical_and(end_col <= ti.astype(F32), rr < N_EXPERTS)
    te = jnp.minimum(jnp.sum(done.astype(F32), axis=0, keepdims=True), N_EXPERTS - 1.0)
    n_used = end_col[N_EXPERTS - 1:N_EXPERTS, :]
    lane = lax.broadcasted_iota(jnp.int32, (8, 256), 1)
    te_ref[...] = jnp.where(lane == N_USED_LANE, n_used, jnp.broadcast_to(te, (8, 256))).astype(jnp.int32)


def _plan_tiles(cnt):
    return pl.pallas_call(
        _plan_tiles_kernel,
        out_shape=jax.ShapeDtypeStruct((8, 256), jnp.int32),
        name="moe_plan_tiles",
    )(cnt)


def _dispatch_kernel(dest_hbm, x_ref, sh_ref, sc_ref, xs_in, xs_out, dsm, xw, sem_d, sem):
    del xs_in
    i = pl.program_id(0)
    dcp = pltpu.make_async_copy(dest_hbm.at[i], dsm, sem_d)
    dcp.start()
    xm = x_ref[...] * (1.0 + sc_ref[0]) + sh_ref[0]
    hi = lax.bitcast_convert_type(_bf(xm[:, :D // 2]).astype(F32), jnp.uint32)
    lo = lax.bitcast_convert_type(_bf(xm[:, D // 2:]).astype(F32), jnp.uint32)
    xw[...] = hi | (lo >> 16)
    dcp.wait()

    def row_copy(r, d):
        return pltpu.make_async_copy(xw.at[pl.ds(r, 1)], xs_out.at[pl.ds(d, 1)], sem)

    def issue(r, carry):
        row_copy(r, dsm[0, r]).start()
        row_copy(r, dsm[1, r]).start()
        return carry

    def drain(r, carry):
        row_copy(0, 0).wait()
        row_copy(0, 0).wait()
        return carry

    lax.fori_loop(0, MOE_TM, issue, 0, unroll=8)
    lax.fori_loop(0, MOE_TM, drain, 0, unroll=8)


def _dispatch(dest_t, x, mods, layer):
    tm = MOE_TM
    return pl.pallas_call(
        _dispatch_kernel,
        out_shape=jax.ShapeDtypeStruct((MOE_ROWS, D // 2), jnp.uint32),
        grid=(T // tm,),
        in_specs=[pl.BlockSpec(memory_space=pl.ANY),
                  pl.BlockSpec((tm, D), lambda i: (i, 0)),
                  _mod_spec(layer, 3, tm), _mod_spec(layer, 4, tm),
                  pl.BlockSpec(memory_space=pl.ANY)],
        out_specs=pl.BlockSpec(memory_space=pl.ANY),
        input_output_aliases={4: 0},
        scratch_shapes=[pltpu.SMEM((2, tm), jnp.int32), pltpu.VMEM((tm, D // 2), jnp.uint32),
                        pltpu.SemaphoreType.DMA(()), pltpu.SemaphoreType.DMA(())],
        compiler_params=_params(1),
        name="moe_dispatch",
    )(dest_t, x, mods, mods, jnp.zeros((MOE_ROWS, D // 2), jnp.uint32))


def _expert_kernel(tiles_ref, xs_ref, wg_ref, wu_ref, wd_ref, y_ref):
    i = pl.program_id(0)
    used = i < tiles_ref[N_USED_LANE]

    @pl.when(used)
    def _():
        w = xs_ref[...]
        left = lax.bitcast_convert_type(w & jnp.uint32(0xFFFF0000), F32).astype(BF16)
        right = lax.bitcast_convert_type(w << 16, F32).astype(BF16)
        xm = jnp.concatenate([left, right], axis=1)
        a = _mm(xm, wg_ref[0])
        u = _mm(xm, wu_ref[0])
        y_ref[...] = _mm(_bf((a * _sigmoid(a)) * u), wd_ref[0])

    @pl.when(jnp.logical_not(used))
    def _():
        y_ref[...] = jnp.zeros_like(y_ref)


def _experts(tiles, xs, wg, wu, wd):
    grid_spec = pltpu.PrefetchScalarGridSpec(
        num_scalar_prefetch=1,
        grid=(MOE_TILES,),
        in_specs=[pl.BlockSpec((MOE_R, D // 2), lambda i, t: (i, 0)),
                  pl.BlockSpec((1, D, EXPERT_FF), lambda i, t: (t[i], 0, 0)),
                  pl.BlockSpec((1, D, EXPERT_FF), lambda i, t: (t[i], 0, 0)),
                  pl.BlockSpec((1, EXPERT_FF, D), lambda i, t: (t[i], 0, 0))],
        out_specs=pl.BlockSpec((MOE_R, D), lambda i, t: (i, 0)))
    return pl.pallas_call(
        _expert_kernel,
        out_shape=jax.ShapeDtypeStruct((MOE_ROWS, D), F32),
        grid_spec=grid_spec,
        compiler_params=_params(1),
        name="moe_experts",
    )(tiles, xs, wg, wu, wd)


def _combine_kernel(dest_hbm, y_hbm, meta_ref, x_ref, gate_ref, g_ref, b_ref, o_ref, dsm, y1, y2, sem_d, sem):
    i = pl.program_id(0)
    dcp = pltpu.make_async_copy(dest_hbm.at[i], dsm, sem_d)
    dcp.start()
    dcp.wait()

    def row_copy(d, buf, r):
        return pltpu.make_async_copy(y_hbm.at[pl.ds(d, 1)], buf.at[pl.ds(r, 1)], sem)

    def issue(r, carry):
        row_copy(dsm[0, r], y1, r).start()
        row_copy(dsm[1, r], y2, r).start()
        return carry

    def drain(r, carry):
        row_copy(0, y1, 0).wait()
        row_copy(0, y2, 0).wait()
        return carry

    lax.fori_loop(0, MOE_TM, issue, 0, unroll=8)
    lax.fori_loop(0, MOE_TM, drain, 0, unroll=8)
    meta = meta_ref[...]
    y = meta[:, 2:3] * y1[...] + meta[:, 3:4] * y2[...]
    o_ref[...] = _residual_ln(x_ref[...], gate_ref[0], y, g_ref[...], b_ref[...])


def _combine(dest_t, ys, meta, x, mods, layer, ln_g, ln_b):
    tm = MOE_TM
    row = lambda i: (i, 0)
    const2 = lambda i: (0, 0)
    return pl.pallas_call(
        _combine_kernel,
        out_shape=jax.ShapeDtypeStruct((T, D), F32),
        grid=(T // tm,),
        in_specs=[pl.BlockSpec(memory_space=pl.ANY), pl.BlockSpec(memory_space=pl.ANY),
                  pl.BlockSpec((tm, 128), row), pl.BlockSpec((tm, D), row),
                  _mod_spec(layer, 5, tm), pl.BlockSpec((1, D), const2), pl.BlockSpec((1, D), const2)],
        out_specs=pl.BlockSpec((tm, D), row),
        scratch_shapes=[pltpu.SMEM((2, tm), jnp.int32), pltpu.VMEM((tm, D), F32), pltpu.VMEM((tm, D), F32),
                        pltpu.SemaphoreType.DMA(()), pltpu.SemaphoreType.DMA(())],
        compiler_params=_params(1),
        name="moe_combine",
    )(dest_t, ys, meta, x, mods, ln_g, ln_b)


def _moe(x, mods, layer, w_r, b_r, wg, wu, wd, ln_g, ln_b):
    meta = _router(x, mods, layer, w_r, b_r)
    rank, cnt = _plan_rank(meta)
    dest = _plan_dest(meta, rank, cnt)
    tiles = _plan_tiles(cnt)[0]
    dest_t = dest[:, :2].T.reshape(2, T // MOE_TM, MOE_TM).transpose(1, 0, 2)
    xs = _dispatch(dest_t, x, mods, layer)
    ys = _experts(tiles, xs, wg, wu, wd)
    return _combine(dest_t, ys, meta, x, mods, layer, ln_g, ln_b)


def _moe_dense_kernel(x_ref, sh_ref, sc_ref, meta_ref, wg_ref, wu_ref, wd_ref, gate_ref, g_ref, b_ref, o_ref,
                      xm_s, acc_s):
    grp = pl.program_id(1)

    @pl.when(grp == 0)
    def _():
        xm_s[...] = _bf(x_ref[...] * (1.0 + sc_ref[0]) + sh_ref[0])
        acc_s[...] = jnp.zeros_like(acc_s)

    xm = xm_s[...]
    meta = meta_ref[...]
    i1 = meta[:, 0:1].astype(jnp.int32)
    i2 = meta[:, 1:2].astype(jnp.int32)
    w1 = meta[:, 2:3]
    w2 = meta[:, 3:4]
    hid = []
    for e in range(EPG):
        eid = grp * EPG + e
        gate = jnp.where(i1 == eid, w1, 0.0) + jnp.where(i2 == eid, w2, 0.0)
        a = _mm(xm, wg_ref[0, e])
        u = _mm(xm, wu_ref[0, e])
        hid.append(_bf((a * _sigmoid(a)) * u * gate))
    acc_s[...] += _mm(jnp.concatenate(hid, axis=1), wd_ref[0])

    @pl.when(grp == N_GROUPS - 1)
    def _():
        o_ref[...] = _residual_ln(x_ref[...], gate_ref[0], acc_s[...], g_ref[...], b_ref[...])


def _moe_dense(x, mods, layer, meta, wg, wu, wd, ln_g, ln_b):
    tm = 512
    row = lambda i, g: (i, 0)
    const2 = lambda i, g: (0, 0)
    return pl.pallas_call(
        _moe_dense_kernel,
        out_shape=jax.ShapeDtypeStruct((T, D), F32),
        grid=(T // tm, N_GROUPS),
        in_specs=[pl.BlockSpec((tm, D), row), _mod_spec(layer, 3, tm), _mod_spec(layer, 4, tm),
                  pl.BlockSpec((tm, 128), row),
                  pl.BlockSpec((1, EPG, D, EXPERT_FF), lambda i, g: (g, 0, 0, 0)),
                  pl.BlockSpec((1, EPG, D, EXPERT_FF), lambda i, g: (g, 0, 0, 0)),
                  pl.BlockSpec((1, EPG * EXPERT_FF, D), lambda i, g: (g, 0, 0)),
                  _mod_spec(layer, 5, tm), pl.BlockSpec((1, D), const2), pl.BlockSpec((1, D), const2)],
        out_specs=pl.BlockSpec((tm, D), row),
        scratch_shapes=[pltpu.VMEM((tm, D), BF16), pltpu.VMEM((tm, D), F32)],
        compiler_params=_params(2),
        name="moe_dense",
    )(x, mods, mods, meta, wg, wu, wd, mods, ln_g, ln_b)


def _permute_even_w(w):
    a_end = 4 * HA * DH
    g_end = a_end + 4 * HA
    c_end = g_end + 3 * HB * DH
    z_end = c_end + HB * DH
    small = jnp.concatenate([w[:, a_end:g_end], w[:, z_end:]], axis=1)
    pad = jnp.zeros((w.shape[0], 128 - small.shape[1]), w.dtype)
    return jnp.concatenate([w[:, :a_end], w[:, g_end:c_end], w[:, c_end:z_end], small, pad], axis=1)


def _lane_row(vals, offset):
    return jnp.zeros((1, 128), F32).at[0, offset:offset + vals.shape[0]].set(vals.astype(F32))


def _pair_rows(vals, offset):
    v32 = jnp.zeros((32,), F32).at[offset:offset + vals.shape[0]].set(vals.astype(F32))
    return jnp.repeat(v32.reshape(16, 2), CHUNK, axis=1)


def kernel(x_prompt, x_sample, c, c_ctx, state_mlstm_c, state_mlstm_n, state_mlstm_m, state_delta, cache_k, cache_v, w_mod, b_mod, ln_g, ln_b, w_in_even, mlstm_gate_b, mlstm_norm_g, delta_conv_w, delta_a_log, delta_dt_bias, delta_norm_g, w_out_even, w_qkv_odd, attn_sink, w_out_odd, w_grp, b_grp, w_erouter, b_erouter, w_gate, w_up, w_down):
    x = jnp.concatenate([x_prompt.reshape(NP_ROWS, D), x_sample.reshape(NS_ROWS, D)], axis=0)
    cvecs = jnp.concatenate([c_ctx[None, :], c, jnp.zeros((N_MOD_ROWS - 1 - DEC_BATCH, D), F32)], axis=0)
    mods = _modulation(cvecs, w_mod, b_mod)
    tables = _rope_tables()
    m0_all = jnp.broadcast_to(state_mlstm_m.reshape(DEC_BATCH, N_EVEN, 2 * HA, 1), (DEC_BATCH, N_EVEN, 2 * HA, 128))

    out_mc, out_mn, out_mm, out_ds, out_k, out_v = [], [], [], [], [], []
    for l in range(DEPTH):
        if l % 2 == 0:
            e = l // 2
            proj = _even_proj(x, mods, l, _bf(_permute_even_w(w_in_even[e])))
            gates_t2 = (proj[:, EVEN_W - 128:EVEN_W - 96].reshape(T // CHUNK, CHUNK, 32).transpose(0, 2, 1)
                        .reshape(T // CHUNK, 16, 128))
            gb = mlstm_gate_b[e].reshape(-1)
            hf, hb, mc, mn, mm = _mlstm_scan(proj, gates_t2, _lane_row(gb, 0), _pair_rows(gb, 0),
                                             state_mlstm_c, state_mlstm_n, m0_all, e)
            qkv = _delta_prep(proj, delta_conv_w[e])
            dtb = delta_dt_bias[e].reshape(-1)
            nea = -jnp.exp(delta_a_log[e].astype(F32)).reshape(-1)
            of, ob, ds = _delta_scan(qkv, proj, gates_t2, _lane_row(dtb, 24), _pair_rows(dtb, 24),
                                     _lane_row(nea, 24), _pair_rows(nea, 24), state_delta, e)
            x = _even_out(hf, hb, of, ob, proj, mlstm_norm_g[e][None, :], delta_norm_g[e][None, :],
                          _bf(w_out_even[e]), x, mods, l, ln_g[l, 0][None, :], ln_b[l, 0][None, :])
            out_mc.append(mc)
            out_mn.append(mn)
            out_mm.append(mm[:, :, 0].reshape(BATCH, 2, HA))
            out_ds.append(ds)
        else:
            o = l // 2
            qkv = _odd_proj(x, mods, l, _bf(w_qkv_odd[o]), tables)
            a_p = _attn_context(qkv, attn_sink[o])
            a_s = _attn_latent(qkv, cache_k, cache_v, attn_sink[o], o)
            x = _odd_out(a_p, a_s, _bf(w_out_odd[o]), x, mods, l, ln_g[l, 0][None, :], ln_b[l, 0][None, :])
            kp = qkv[:NP_ROWS, HC * HD:(HC + KVH) * HD].reshape(BATCH, SEQ, KVH, HD).transpose(0, 2, 1, 3)
            vp = qkv[:NP_ROWS, (HC + KVH) * HD:].reshape(BATCH, SEQ, KVH, HD).transpose(0, 2, 1, 3)
            out_k.append(kp)
            out_v.append(vp)
        w_r = jnp.concatenate([w_erouter[l].transpose(1, 0, 2).reshape(D, N_EXPERTS), w_grp[l],
                               jnp.zeros((D, 128 - N_EXPERTS - N_GROUPS), F32)], axis=1)
        b_r = jnp.concatenate([b_erouter[l].reshape(-1), b_grp[l],
                               jnp.zeros((128 - N_EXPERTS - N_GROUPS,), F32)])[None, :]
        meta = _router(x, mods, l, w_r, b_r)
        x = _moe_dense(x, mods, l, meta,
                       _bf(w_gate[l]).reshape(N_GROUPS, EPG, D, EXPERT_FF),
                       _bf(w_up[l]).reshape(N_GROUPS, EPG, D, EXPERT_FF),
                       _bf(w_down[l]).reshape(N_GROUPS, EPG * EXPERT_FF, D),
                       ln_g[l, 1][None, :], ln_b[l, 1][None, :])
    return (x[:NP_ROWS].reshape(BATCH, SEQ, D), x[NP_ROWS:].reshape(DEC_BATCH, DEC_SEQ, D),
            jnp.stack(out_mc, 1), jnp.stack(out_mn, 1), jnp.stack(out_mm, 1), jnp.stack(out_ds, 1),
            jnp.stack(out_k, 1), jnp.stack(out_v, 1))
```

```python
import functools

import jax
import jax.numpy as jnp
from jax import lax
from jax.experimental import pallas as pl
from jax.experimental.pallas import tpu as pltpu

F32 = jnp.float32
BF16 = jnp.bfloat16
HI = lax.Precision.HIGHEST

D = 1024
BATCH = 32
SEQ = 256
DEPTH = 4
DEC_BATCH = 2
DEC_SEQ = 4096
PAST_LEN = 512
GRID_W = 64
N_EVEN = 2
N_ODD = 2
HA = 4
HB = 4
DH = 128
CHUNK = 64
HC = 16
KVH = 4
HD = 64
WINDOW = 128
QBLOCK = 128
ROPE_THETA = 10000.0
N_GROUPS = 4
EPG = 4
N_EXPERTS = 16
EXPERT_FF = 256
DN_ALPHA = (2 * DEPTH) ** 0.25
LN_EPS = 1e-5

NP_ROWS = BATCH * SEQ
NS_ROWS = DEC_BATCH * DEC_SEQ
T = NP_ROWS + NS_ROWS
N_MOD_ROWS = 8
EVEN_W = 4224
QKV_W = (HC + 2 * KVH) * HD

P_CHUNKS = SEQ // CHUNK
S_CHUNKS = DEC_SEQ // CHUNK
P_STEPS = BATCH * P_CHUNKS
S_STEPS = DEC_BATCH * S_CHUNKS
P_BLOCKS = NP_ROWS // CHUNK

VMEM_LIMIT = 48 * 1024 * 1024


def _params(n_axes):
    return pltpu.CompilerParams(dimension_semantics=("arbitrary",) * n_axes,
                                vmem_limit_bytes=VMEM_LIMIT)


def _mm(a, b, prec=None):
    return lax.dot_general(a, b, (((1,), (0,)), ((), ())), precision=prec, preferred_element_type=F32)


def _mm_nt(a, b, prec=None):
    return lax.dot_general(a, b, (((1,), (1,)), ((), ())), precision=prec, preferred_element_type=F32)


def _mm_tn(a, b, prec=None):
    return lax.dot_general(a, b, (((0,), (0,)), ((), ())), precision=prec, preferred_element_type=F32)


def _bf(x):
    return x.astype(BF16)


def _sigmoid(x):
    return 1.0 / (1.0 + jnp.exp(-x))


def _softplus(x):
    return jnp.maximum(x, 0.0) + jnp.log1p(jnp.exp(-jnp.abs(x)))


def _log_sigmoid(x):
    return jnp.minimum(x, 0.0) - jnp.log1p(jnp.exp(-jnp.abs(x)))


def _mod_row(tile, tm):
    npt = NP_ROWS // tm
    per = DEC_SEQ // tm
    return jnp.where(tile < npt, 0, 1 + (tile - npt) // per)


def _mod_spec(layer, chunk, tm):
    def imap(i, *_):
        return ((layer * N_MOD_ROWS + _mod_row(i, tm)) * 6 + chunk, 0, 0)
    return pl.BlockSpec((1, 1, D), imap)


def _modulation_kernel(c_ref, w_ref, b_ref, o_ref):
    x = c_ref[...]
    s = x * _sigmoid(x)
    o_ref[0] = _mm(s, w_ref[0], HI) + b_ref[0]


def _modulation(cvecs, w_mod, b_mod):
    out = pl.pallas_call(
        _modulation_kernel,
        out_shape=jax.ShapeDtypeStruct((DEPTH, N_MOD_ROWS, 6 * D), F32),
        grid=(DEPTH, 6),
        in_specs=[pl.BlockSpec((N_MOD_ROWS, D), lambda l, j: (0, 0)),
                  pl.BlockSpec((1, D, D), lambda l, j: (l, 0, j)),
                  pl.BlockSpec((1, 1, D), lambda l, j: (l * 6 + j, 0, 0))],
        out_specs=pl.BlockSpec((1, N_MOD_ROWS, D), lambda l, j: (l, 0, j)),
        compiler_params=_params(2),
        name="modulation",
    )(cvecs, w_mod, b_mod.reshape(DEPTH * 6, 1, D))
    return out.reshape(DEPTH * N_MOD_ROWS * 6, 1, D)


def _proj_kernel(x_ref, sh_ref, sc_ref, w_ref, o_ref):
    xm = x_ref[...] * (1.0 + sc_ref[0]) + sh_ref[0]
    o_ref[...] = _mm(_bf(xm), w_ref[...])


def _even_proj(x, mods, layer, w):
    tm = 256
    return pl.pallas_call(
        _proj_kernel,
        out_shape=jax.ShapeDtypeStruct((T, EVEN_W), F32),
        grid=(T // tm,),
        in_specs=[pl.BlockSpec((tm, D), lambda i: (i, 0)),
                  _mod_spec(layer, 0, tm), _mod_spec(layer, 1, tm),
                  pl.BlockSpec((D, EVEN_W), lambda i: (0, 0))],
        out_specs=pl.BlockSpec((tm, EVEN_W), lambda i: (i, 0)),
        compiler_params=_params(1),
        name="even_proj",
    )(x, mods, mods, w)


def _qkv_kernel(x_ref, sh_ref, sc_ref, w_ref, cos_ref, sa_ref, sb_ref, o_ref, *, tm):
    i = pl.program_id(0)
    xm = x_ref[...] * (1.0 + sc_ref[0]) + sh_ref[0]
    acc = _mm(_bf(xm), w_ref[...])
    n_q = HC * HD // 128
    n_k = KVH * HD // 128
    is_latent = i >= NP_ROWS // tm
    cos = jnp.where(is_latent, cos_ref[...], 1.0)
    sa = jnp.where(is_latent, sa_ref[...], 0.0)
    sb = jnp.where(is_latent, sb_ref[...], 0.0)
    for g in range(n_q + n_k):
        blk = acc[:, g * 128:(g + 1) * 128]
        if g < n_q:
            blk = blk * (HD ** -0.5)
        rot = blk * cos + pltpu.roll(blk, 112, 1) * sa + pltpu.roll(blk, 16, 1) * sb
        o_ref[:, g * 128:(g + 1) * 128] = rot
    o_ref[:, (n_q + n_k) * 128:] = acc[:, (n_q + n_k) * 128:]


def _rope_tables():
    half = HD // 4
    inv = ROPE_THETA ** (-jnp.arange(half, dtype=F32) / half)
    pos = jnp.arange(DEC_SEQ)
    row = (pos // GRID_W).astype(F32)[:, None] * inv[None, :]
    col = (pos % GRID_W).astype(F32)[:, None] * inv[None, :]
    cos = jnp.concatenate([jnp.cos(row), jnp.cos(row), jnp.cos(col), jnp.cos(col)], axis=-1)
    sin = jnp.concatenate([jnp.sin(row), jnp.sin(row), jnp.sin(col), jnp.sin(col)], axis=-1)
    first = (jnp.arange(HD) % 32) < 16
    sa = jnp.where(first, -sin, 0.0)
    sb = jnp.where(first, 0.0, sin)
    tile2 = lambda t: jnp.concatenate([t, t], axis=-1)
    return tile2(cos), tile2(sa), tile2(sb)


def _odd_proj(x, mods, layer, w, tables):
    tm = 256
    npt = NP_ROWS // tm
    per = DEC_SEQ // tm
    tab_spec = pl.BlockSpec((tm, 128), lambda i: (jnp.where(i < npt, 0, (i - npt) % per), 0))
    return pl.pallas_call(
        functools.partial(_qkv_kernel, tm=tm),
        out_shape=jax.ShapeDtypeStruct((T, QKV_W), F32),
        grid=(T // tm,),
        in_specs=[pl.BlockSpec((tm, D), lambda i: (i, 0)),
                  _mod_spec(layer, 0, tm), _mod_spec(layer, 1, tm),
                  pl.BlockSpec((D, QKV_W), lambda i: (0, 0)),
                  tab_spec, tab_spec, tab_spec],
        out_specs=pl.BlockSpec((tm, QKV_W), lambda i: (i, 0)),
        compiler_params=_params(1),
        name="odd_qkv_proj",
    )(x, mods, mods, w, *tables)


PREP_ROWS = 256


def _delta_prep_kernel(x_ref, prev_ref, next_ref, w_ref, o_ref):
    i = pl.program_id(0)
    part = pl.program_id(1)
    npb = NP_ROWS // PREP_ROWS
    per = DEC_SEQ // PREP_ROWS
    is_latent = i >= npb
    pos = (i - npb) % per
    has_prev = jnp.logical_and(is_latent, pos > 0)
    has_next = jnp.logical_and(is_latent, pos < per - 1)
    x = x_ref[...]
    w = w_ref[...]
    rows = lax.broadcasted_iota(jnp.int32, x.shape, 0)
    prev_row = jnp.where(has_prev, prev_ref[7:8, :], 0.0)
    next_row = jnp.where(has_next, next_ref[0:1, :], 0.0)
    xm1 = jnp.where(rows == 0, prev_row, pltpu.roll(x, 1, 0))
    xp1 = jnp.where(rows == PREP_ROWS - 1, next_row, pltpu.roll(x, PREP_ROWS - 1, 0))
    y = xm1 * w[0:1, :] + x * w[1:2, :] + xp1 * w[2:3, :]
    y = y * _sigmoid(y)
    q_scale = jnp.where(part == 0, DH ** -0.5, 1.0)
    for h in range(4):
        yh = y[:, h * DH:(h + 1) * DH]
        inv = lax.rsqrt(jnp.sum(yh * yh, axis=-1, keepdims=True) + 1e-6)
        scale = jnp.where(part == 2, 1.0, inv * q_scale)
        o_ref[:, h * DH:(h + 1) * DH] = yh * scale


def _delta_prep(proj, conv_w):
    nblk = T // PREP_ROWS
    sub = PREP_ROWS // 8
    last8 = T // 8 - 1
    return pl.pallas_call(
        _delta_prep_kernel,
        out_shape=jax.ShapeDtypeStruct((T, 3 * 512), F32),
        grid=(nblk, 3),
        in_specs=[pl.BlockSpec((PREP_ROWS, 512), lambda i, p: (i, 4 + p)),
                  pl.BlockSpec((8, 512), lambda i, p: (jnp.maximum(i * sub - 1, 0), 4 + p)),
                  pl.BlockSpec((8, 512), lambda i, p: (jnp.minimum((i + 1) * sub, last8), 4 + p)),
                  pl.BlockSpec((3, 512), lambda i, p: (0, p))],
        out_specs=pl.BlockSpec((PREP_ROWS, 512), lambda i, p: (i, p)),
        compiler_params=_params(2),
        name="delta_prep",
    )(proj, proj, proj, conv_w)


assert P_STEPS == S_STEPS
SCAN_STEPS = P_STEPS


def _bwd_local(s, nc):
    return (s // nc) * nc + nc - 1 - s % nc


def _scan_blocks():
    return (lambda s: s, lambda s: _bwd_local(s, P_CHUNKS),
            lambda s: P_BLOCKS + s, lambda s: P_BLOCKS + _bwd_local(s, S_CHUNKS))


def _scan_specs():
    gcol = EVEN_W // 128 - 1
    blocks = _scan_blocks()
    return ([pl.BlockSpec((CHUNK, 3 * 512), lambda s, f=f: (f(s), 0)) for f in blocks]
            + [pl.BlockSpec((CHUNK, 128), lambda s, f=f: (f(s), gcol)) for f in blocks]
            + [pl.BlockSpec((1, 16, 128), lambda s, f=f: (f(s), 0, 0)) for f in blocks])


def _scan_out_specs():
    local = (lambda s: s, lambda s: _bwd_local(s, P_CHUNKS), lambda s: s, lambda s: _bwd_local(s, S_CHUNKS))
    return [pl.BlockSpec((CHUNK, 512), lambda s, f=f: (f(s), 0)) for f in local]


def _mlstm_kernel(xpf_ref, xpb_ref, xsf_ref, xsb_ref, gpf_ref, gpb_ref, gsf_ref, gsb_ref,
                  rpf_ref, rpb_ref, rsf_ref, rsb_ref,
                  brow_ref, b2_ref, c0_ref, n0_ref, m0_ref,
                  hpf_ref, hpb_ref, hsf_ref, hsb_ref, c_out, n_out, m_out,
                  cp_s, np_s, mp_s, cs_s, ns_s, ms_s):
    s = pl.program_id(0)
    jp = s % P_CHUNKS
    js = s % S_CHUNKS

    @pl.when(jp == 0)
    def _():
        cp_s[...] = jnp.zeros_like(cp_s)
        np_s[...] = jnp.zeros_like(np_s)
        mp_s[...] = jnp.zeros_like(mp_s)

    @pl.when(js == 0)
    def _():
        for d in range(2):
            for h in range(HA):
                cs_s[d * HA + h] = c0_ref[0, 0, d, h]
            ns_s[d * HA:(d + 1) * HA, :] = n0_ref[0, 0, d]
        ms_s[...] = m0_ref[0, 0]

    states = ((cp_s, np_s, mp_s), (cs_s, ns_s, ms_s))

    row = lax.broadcasted_iota(jnp.int32, (CHUNK, 128), 0)
    lane = lax.broadcasted_iota(jnp.int32, (CHUNK, 128), 1)
    left = lane < CHUNK
    left1 = left[0:1, :]
    lcol = jnp.where(left, lane, lane - CHUNK)
    r64 = lax.broadcasted_iota(jnp.int32, (CHUNK, CHUNK), 0)
    c64 = lax.broadcasted_iota(jnp.int32, (CHUNK, CHUNK), 1)
    r128 = lax.broadcasted_iota(jnp.int32, (128, 128), 0)
    c128 = lax.broadcasted_iota(jnp.int32, (128, 128), 1)
    same_half = (r128 < CHUNK) == (c128 < CHUNK)
    neg = -jnp.inf
    ins = ((xpf_ref, gpf_ref, rpf_ref, hpf_ref), (xpb_ref, gpb_ref, rpb_ref, hpb_ref),
           (xsf_ref, gsf_ref, rsf_ref, hsf_ref), (xsb_ref, gsb_ref, rsb_ref, hsb_ref))

    gate = []
    for u in range(4):
        d = u % 2
        gc_ref, gr_ref = ins[u][1], ins[u][2]
        if d == 0:
            incl_p, tri_c, tri_r = lcol <= row, c64 <= r64, jnp.logical_and(same_half, r128 <= c128)
        else:
            incl_p, tri_c, tri_r = lcol >= row, c64 >= r64, jnp.logical_and(same_half, r128 >= c128)
        gc = gc_ref[...] + brow_ref[...]
        gr = gr_ref[0] + b2_ref[...]
        cs_col = _prefix_cols(tri_c.astype(BF16), _log_sigmoid(gc))
        cs_row = _prefix_rows(_log_sigmoid(gr), tri_r.astype(BF16))
        gate.append((incl_p, gc, gr, cs_col, cs_row))

    st = []
    for u in range(4):
        d = u % 2
        c_s, n_s, m_s = states[u // 2]
        for hp in range(HA // 2):
            x_ref = ins[u][0]
            incl_p, gc, gr, cs_col, cs_row = gate[u]
            end = CHUNK - 1 if d == 0 else 0
            heads = []
            for h in (2 * hp, 2 * hp + 1):
                c = d * HA + h
                ci = d * 8 + h
                cf = d * 8 + 4 + h
                b_col = cs_col[:, cf:cf + 1]
                heads.append(dict(c=c, h=h, q=x_ref[:, h * DH:(h + 1) * DH],
                                  k=x_ref[:, 512 + h * DH:512 + (h + 1) * DH] * (DH ** -0.5),
                                  v=x_ref[:, 1024 + h * DH:1024 + (h + 1) * DH],
                                  i_col=gc[:, ci:ci + 1], b_col=b_col, b_last=b_col[end:end + 1, :],
                                  m=m_s[c:c + 1, 0:1]))
            ha, hb_ = heads
            i_row = gr[d * 4 + hp:d * 4 + hp + 1, :]
            b_row = cs_row[d * 4 + 2 + hp:d * 4 + 3 + hp, :]
            b_col_p = jnp.where(left, ha["b_col"], hb_["b_col"])
            m_p = jnp.where(left1, m_s[ha["c"]:ha["c"] + 1, :], m_s[hb_["c"]:hb_["c"] + 1, :])
            dmat = jnp.where(incl_p, b_col_p - b_row + i_row, neg)
            inter = b_col_p + m_p
            dmax = jnp.where(left, jnp.max(jnp.where(left, dmat, neg), axis=1, keepdims=True),
                             jnp.max(jnp.where(left, neg, dmat), axis=1, keepdims=True))
            m_t = jnp.maximum(inter, dmax)
            q_cat = _bf(jnp.concatenate([ha["q"], hb_["q"]], axis=1))
            k_bd = _bf(_block_diag2(ha["k"], hb_["k"]))
            sc = _mm_nt(q_cat, k_bd) * jnp.exp(dmat - m_t)
            st.append(dict(u=u, heads=heads, sc=sc, w_inter=jnp.exp(inter - m_t), m_t=m_t))

    for p in st:
        ha, hb_ = p["heads"]
        h_ref = ins[p["u"]][3]
        c_s, n_s, m_s = states[p["u"] // 2]
        sc = p["sc"]
        sv = _mm(_bf(sc), _bf(_block_diag2(ha["v"], hb_["v"])))
        dens = (jnp.sum(jnp.where(left, sc, 0.0), axis=1, keepdims=True),
                jnp.sum(jnp.where(left, 0.0, sc), axis=1, keepdims=True))
        for idx, hd in enumerate((ha, hb_)):
            c, h, q, k = hd["c"], hd["h"], hd["q"], hd["k"]
            w_inter = p["w_inter"][:, idx * CHUNK:idx * CHUNK + 1]
            m_t = p["m_t"][:, idx * CHUNK:idx * CHUNK + 1]
            c_mat = c_s[c]
            nrm = n_s[c:c + 1, :]
            num = sv[:, idx * DH:(idx + 1) * DH] + w_inter * _mm(_bf(q), _bf(c_mat))
            den = dens[idx] + w_inter * jnp.sum(q * nrm, axis=1, keepdims=True)
            h_ref[:, h * DH:(h + 1) * DH] = num / jnp.maximum(jnp.abs(den), jnp.exp(-m_t))
            g_end = hd["b_last"] - hd["b_col"] + hd["i_col"]
            m_new = jnp.maximum(hd["b_last"] + hd["m"], jnp.max(g_end, axis=0, keepdims=True))
            kw = k * jnp.exp(g_end - m_new)
            decay = jnp.exp(hd["b_last"] + hd["m"] - m_new)
            c_s[c] = decay * c_mat + _mm_tn(_bf(kw), _bf(hd["v"]))
            n_s[c:c + 1, :] = decay * nrm + jnp.sum(kw, axis=0, keepdims=True)
            m_s[c:c + 1, :] = jnp.broadcast_to(m_new, (1, 128))

    @pl.when(jp == P_CHUNKS - 1)
    def _():
        for d in range(2):
            for h in range(HA):
                c_out[0, d, h] = cp_s[d * HA + h]
            n_out[0, d] = np_s[d * HA:(d + 1) * HA, :]
        m_out[0] = mp_s[...]


def _mlstm_scan(proj, gates_t2, bias_row, bias2, c0, n0, m0, e):
    const2 = lambda s: (0, 0)
    in_specs = (_scan_specs()
                + [pl.BlockSpec((1, 128), const2), pl.BlockSpec((16, 128), const2),
                   pl.BlockSpec((1, 1, 2, HA, DH, DH), lambda s: (s // S_CHUNKS, e, 0, 0, 0, 0)),
                   pl.BlockSpec((1, 1, 2, HA, DH), lambda s: (s // S_CHUNKS, e, 0, 0, 0)),
                   pl.BlockSpec((1, 1, 2 * HA, 128), lambda s: (s // S_CHUNKS, e, 0, 0))])
    half = jax.ShapeDtypeStruct((NP_ROWS, HA * DH), F32)
    out_shape = (half, half, half, half,
                 jax.ShapeDtypeStruct((BATCH, 2, HA, DH, DH), F32),
                 jax.ShapeDtypeStruct((BATCH, 2, HA, DH), F32),
                 jax.ShapeDtypeStruct((BATCH, 2 * HA, 128), F32))
    out_specs = _scan_out_specs() + [
        pl.BlockSpec((1, 2, HA, DH, DH), lambda s: (s // P_CHUNKS, 0, 0, 0, 0)),
        pl.BlockSpec((1, 2, HA, DH), lambda s: (s // P_CHUNKS, 0, 0, 0)),
        pl.BlockSpec((1, 2 * HA, 128), lambda s: (s // P_CHUNKS, 0, 0))]
    state = [pltpu.VMEM((2 * HA, DH, DH), F32), pltpu.VMEM((2 * HA, DH), F32), pltpu.VMEM((2 * HA, 128), F32)]
    return pl.pallas_call(
        _mlstm_kernel,
        out_shape=out_shape,
        grid=(SCAN_STEPS,),
        in_specs=in_specs,
        out_specs=out_specs,
        scratch_shapes=state + state,
        compiler_params=_params(1),
        name="mlstm_scan",
    )(proj, proj, proj, proj, proj, proj, proj, proj, gates_t2, gates_t2, gates_t2, gates_t2,
      bias_row, bias2, c0, n0, m0)


def _split2(x):
    hi = _bf(x)
    return hi, _bf(x - hi.astype(F32))


def _split3(x):
    h1 = _bf(x)
    r1 = x - h1.astype(F32)
    h2 = _bf(r1)
    return h1, h2, _bf(r1 - h2.astype(F32))


def _lhs3(x):
    hi, lo = _split2(x)
    return jnp.concatenate([hi, lo, hi], axis=1)


def _rhs3(x):
    hi, lo = _split2(x)
    return jnp.concatenate([hi, hi, lo], axis=0)


def _prefix_cols(tri_bf, x):
    n = x.shape[1]
    r = _mm(tri_bf, jnp.concatenate(_split3(x), axis=1))
    return r[:, :n] + r[:, n:2 * n] + r[:, 2 * n:]


def _prefix_rows(x, tri_bf):
    m = x.shape[0]
    r = _mm(jnp.concatenate(_split3(x), axis=0), tri_bf)
    return r[:m] + r[m:2 * m] + r[2 * m:]


LEVELS = tuple(range(6))


def _block_diag2(a, b):
    z = jnp.zeros_like(a)
    return jnp.concatenate([jnp.concatenate([a, z], axis=1), jnp.concatenate([z, b], axis=1)], axis=0)


def _delta_kernel(xpf_ref, xpb_ref, xsf_ref, xsb_ref, gpf_ref, gpb_ref, gsf_ref, gsb_ref,
                  rpf_ref, rpb_ref, rsf_ref, rsb_ref,
                  dtrow_ref, dt2_ref, narow_ref, na2_ref, s0_ref,
                  opf_ref, opb_ref, osf_ref, osb_ref, s_out, sp_s, ss_s):
    s = pl.program_id(0)
    jp = s % P_CHUNKS
    js = s % S_CHUNKS

    @pl.when(jp == 0)
    def _():
        sp_s[...] = jnp.zeros_like(sp_s)

    @pl.when(js == 0)
    def _():
        for d in range(2):
            for h in range(HB):
                ss_s[d * HB + h] = s0_ref[0, 0, d, h]

    states = (sp_s, ss_s)

    row = lax.broadcasted_iota(jnp.int32, (CHUNK, 128), 0)
    lane = lax.broadcasted_iota(jnp.int32, (CHUNK, 128), 1)
    left = lane < CHUNK
    lcol = jnp.where(left, lane, lane - CHUNK)
    eye_p = (lcol == row).astype(F32)
    r64 = lax.broadcasted_iota(jnp.int32, (CHUNK, CHUNK), 0)
    c64 = lax.broadcasted_iota(jnp.int32, (CHUNK, CHUNK), 1)
    r128 = lax.broadcasted_iota(jnp.int32, (128, 128), 0)
    c128 = lax.broadcasted_iota(jnp.int32, (128, 128), 1)
    same_half = (r128 < CHUNK) == (c128 < CHUNK)
    ins = ((xpf_ref, gpf_ref, rpf_ref, opf_ref), (xpb_ref, gpb_ref, rpb_ref, opb_ref),
           (xsf_ref, gsf_ref, rsf_ref, osf_ref), (xsb_ref, gsb_ref, rsb_ref, osb_ref))

    gate = []
    for u in range(4):
        d = u % 2
        gc_ref, gr_ref = ins[u][1], ins[u][2]
        if d == 0:
            incl_p, strict_p = lcol <= row, lcol < row
            tri_c, tri_r = c64 <= r64, jnp.logical_and(same_half, r128 <= c128)
        else:
            incl_p, strict_p = lcol >= row, lcol > row
            tri_c, tri_r = c64 >= r64, jnp.logical_and(same_half, r128 >= c128)
        xc = gc_ref[...]
        xr = gr_ref[0]
        la_c = narow_ref[...] * _softplus(xc + dtrow_ref[...])
        la_r = na2_ref[...] * _softplus(xr + dt2_ref[...])
        g_c = _prefix_cols(tri_c.astype(BF16), la_c)
        g_r = _prefix_rows(la_r, tri_r.astype(BF16))
        gate.append((incl_p, strict_p, _sigmoid(xc), g_c, g_r))

    def bd_rhs(x):
        return _rhs3(jnp.concatenate([jnp.where(left, x, 0.0), jnp.where(left, 0.0, x)], axis=0))

    off = ([], [])
    for lv in LEVELS:
        same = jnp.right_shift(row, lv + 1) == jnp.right_shift(lcol, lv + 1)
        r_hi = jnp.bitwise_and(jnp.right_shift(row, lv), 1) == 1
        c_hi = jnp.bitwise_and(jnp.right_shift(lcol, lv), 1) == 1
        off[0].append(same & r_hi & jnp.logical_not(c_hi))
        off[1].append(same & jnp.logical_not(r_hi) & c_hi)

    pairs = [(u, hp) for u in range(4) for hp in range(HB // 2)]
    st = []
    for u, hp in pairs:
        d = u % 2
        x_ref = ins[u][0]
        incl_p, strict_p, beta_c, g_c, g_r = gate[u]
        end = CHUNK - 1 if d == 0 else 0
        heads = []
        for h in (2 * hp, 2 * hp + 1):
            ib = 16 + d * HB + h
            ia = 24 + d * HB + h
            gcol = g_c[:, ia:ia + 1]
            bc = beta_c[:, ib:ib + 1]
            q = x_ref[:, h * DH:(h + 1) * DH]
            k = x_ref[:, 512 + h * DH:512 + (h + 1) * DH]
            v = x_ref[:, 1024 + h * DH:1024 + (h + 1) * DH]
            heads.append(dict(h=h, gcol=gcol, bc=bc, q=q, k=k, v=v, kb=k * bc, eg=jnp.exp(gcol),
                              g_last=gcol[end:end + 1, :]))
        ha, hb_ = heads
        r = 12 + d * 2 + hp
        gcol_p = jnp.where(left, ha["gcol"], hb_["gcol"])
        decay = jnp.exp(jnp.where(incl_p, gcol_p - g_r[r:r + 1, :], -jnp.inf))
        k_bd = _bf(_block_diag2(ha["k"], hb_["k"]))
        kb_cat = _bf(jnp.concatenate([ha["kb"], hb_["kb"]], axis=1))
        q_cat = _bf(jnp.concatenate([ha["q"], hb_["q"]], axis=1))
        a_mat = jnp.where(strict_p, _mm_nt(kb_cat, k_bd) * decay, 0.0)
        qk = _mm_nt(q_cat, k_bd) * decay
        st.append(dict(u=u, d=d, heads=heads, t=eye_p - jnp.where(off[d][0], a_mat, 0.0), qk=qk,
                       am=[bd_rhs(jnp.where(m, a_mat, 0.0)) for m in off[d][1:]]))

    for li in range(len(LEVELS) - 1):
        for p in st:
            p["w"] = _mm(_lhs3(p["t"]), p["am"][li])
        for p in st:
            p["t"] = p["t"] - _mm(_lhs3(p["w"]), bd_rhs(p["t"]))

    for p in st:
        ha, hb_ = p["heads"]
        o_ref = ins[p["u"]][3]
        s_s = states[p["u"] // 2]
        rhs_a = jnp.concatenate([ha["v"] * ha["bc"], ha["kb"] * ha["eg"]], axis=1)
        rhs_b = jnp.concatenate([hb_["v"] * hb_["bc"], hb_["kb"] * hb_["eg"]], axis=1)
        sol = _mm(_lhs3(p["t"]), _rhs3(_block_diag2(rhs_a, rhs_b)))
        vn = []
        for idx, hd in enumerate((ha, hb_)):
            c = p["d"] * HB + hd["h"]
            s_mat = s_s[c]
            sbf = _bf(s_mat)
            so = sol[:, idx * 2 * DH:(idx + 1) * 2 * DH]
            v_new = so[:, :DH] - _mm(_bf(so[:, DH:]), sbf)
            vn.append(v_new)
            hd["o1"] = _mm(_bf(hd["q"] * hd["eg"]), sbf)
            s_s[c] = (jnp.exp(hd["g_last"]) * s_mat
                      + _mm_tn(_bf(hd["k"] * jnp.exp(hd["g_last"] - hd["gcol"])), _bf(v_new)))
        o2 = _mm(_bf(p["qk"]), _bf(_block_diag2(vn[0], vn[1])))
        for idx, hd in enumerate((ha, hb_)):
            o_ref[:, hd["h"] * DH:(hd["h"] + 1) * DH] = hd["o1"] + o2[:, idx * DH:(idx + 1) * DH]

    @pl.when(jp == P_CHUNKS - 1)
    def _():
        for d in range(2):
            for h in range(HB):
                s_out[0, d, h] = sp_s[d * HB + h]


def _delta_scan(qkv, proj, gates_t2, dt_row, dt2, na_row, na2, s0, e):
    const2 = lambda s: (0, 0)
    in_specs = (_scan_specs()
                + [pl.BlockSpec((1, 128), const2), pl.BlockSpec((16, 128), const2),
                   pl.BlockSpec((1, 128), const2), pl.BlockSpec((16, 128), const2),
                   pl.BlockSpec((1, 1, 2, HB, DH, DH), lambda s: (s // S_CHUNKS, e, 0, 0, 0, 0))])
    half = jax.ShapeDtypeStruct((NP_ROWS, HB * DH), F32)
    out_shape = (half, half, half, half, jax.ShapeDtypeStruct((BATCH, 2, HB, DH, DH), F32))
    out_specs = _scan_out_specs() + [pl.BlockSpec((1, 2, HB, DH, DH), lambda s: (s // P_CHUNKS, 0, 0, 0, 0))]
    return pl.pallas_call(
        _delta_kernel,
        out_shape=out_shape,
        grid=(SCAN_STEPS,),
        in_specs=in_specs,
        out_specs=out_specs,
        scratch_shapes=[pltpu.VMEM((2 * HB, DH, DH), F32), pltpu.VMEM((2 * HB, DH, DH), F32)],
        compiler_params=_params(1),
        name="delta_scan",
    )(qkv, qkv, qkv, qkv, proj, proj, proj, proj, gates_t2, gates_t2, gates_t2, gates_t2,
      dt_row, dt2, na_row, na2, s0)


def _residual_ln(x, gate, y, g, b):
    r = DN_ALPHA * x + gate * y
    mu = jnp.mean(r, axis=-1, keepdims=True)
    var = jnp.mean(jnp.square(r - mu), axis=-1, keepdims=True)
    return (r - mu) * lax.rsqrt(var + LN_EPS) * g + b


def _even_out_kernel(hpf_ref, hpb_ref, hsf_ref, hsb_ref, opf_ref, opb_ref, osf_ref, osb_ref,
                     oa_ref, zb_ref, mg_ref, dg_ref, w_ref, x_ref, gate_ref, g_ref, b_ref, o_ref, *, tm):
    is_p = pl.program_id(0) < NP_ROWS // tm
    hf_ref, hb_ref, of_ref, ob_ref = (
        lambda sl, p=p, q=q: jnp.where(is_p, p[:, sl], q[:, sl])
        for p, q in ((hpf_ref, hsf_ref), (hpb_ref, hsb_ref), (opf_ref, osf_ref), (opb_ref, osb_ref)))
    parts = []
    for h in range(HA):
        sl = slice(h * DH, (h + 1) * DH)
        hh = hf_ref(sl) + hb_ref(sl)
        mu = jnp.mean(hh, axis=-1, keepdims=True)
        var = jnp.mean(jnp.square(hh - mu), axis=-1, keepdims=True)
        parts.append(_sigmoid(oa_ref[:, sl]) * ((hh - mu) * lax.rsqrt(var + LN_EPS) * mg_ref[:, sl]))
    for h in range(HB):
        sl = slice(h * DH, (h + 1) * DH)
        oo = of_ref(sl) + ob_ref(sl)
        z = zb_ref[:, sl]
        nrm = oo * lax.rsqrt(jnp.mean(jnp.square(oo), axis=-1, keepdims=True) + LN_EPS) * dg_ref[:, sl]
        parts.append(nrm * (z * _sigmoid(z)))
    a = jnp.concatenate(parts, axis=1)
    y = _mm(_bf(a), w_ref[...])
    o_ref[...] = _residual_ln(x_ref[...], gate_ref[0], y, g_ref[...], b_ref[...])


def _even_out(h_scan, o_scan, proj, mg, dg, w, x, mods, layer, ln_g, ln_b):
    tm = 256
    npt = NP_ROWS // tm
    row512 = lambda i: (i, 0)
    const2 = lambda i: (0, 0)
    p_spec = pl.BlockSpec((tm, 512), lambda i: (jnp.minimum(i, npt - 1), 0))
    s_spec = pl.BlockSpec((tm, 512), lambda i: (jnp.maximum(i - npt, 0), 0))
    return pl.pallas_call(
        functools.partial(_even_out_kernel, tm=tm),
        out_shape=jax.ShapeDtypeStruct((T, D), F32),
        grid=(T // tm,),
        in_specs=[p_spec, p_spec, s_spec, s_spec, p_spec, p_spec, s_spec, s_spec,
                  pl.BlockSpec((tm, 512), lambda i: (i, 3)),
                  pl.BlockSpec((tm, 512), lambda i: (i, 7)),
                  pl.BlockSpec((1, 512), const2), pl.BlockSpec((1, 512), const2),
                  pl.BlockSpec((D, D), const2),
                  pl.BlockSpec((tm, D), row512),
                  _mod_spec(layer, 2, tm),
                  pl.BlockSpec((1, D), const2), pl.BlockSpec((1, D), const2)],
        out_specs=pl.BlockSpec((tm, D), row512),
        compiler_params=_params(1),
        name="even_out_ln",
    )(*h_scan, *o_scan, proj, proj, mg, dg, w, x, mods, ln_g, ln_b)


def _odd_out_kernel(ap_ref, as_ref, w_ref, x_ref, gate_ref, g_ref, b_ref, o_ref, *, tm):
    a = jnp.where(pl.program_id(0) < NP_ROWS // tm, ap_ref[...], as_ref[...])
    y = _mm(_bf(a), w_ref[...])
    o_ref[...] = _residual_ln(x_ref[...], gate_ref[0], y, g_ref[...], b_ref[...])


def _odd_out(a_prompt, a_latent, w, x, mods, layer, ln_g, ln_b):
    tm = 256
    npt = NP_ROWS // tm
    row = lambda i: (i, 0)
    const2 = lambda i: (0, 0)
    return pl.pallas_call(
        functools.partial(_odd_out_kernel, tm=tm),
        out_shape=jax.ShapeDtypeStruct((T, D), F32),
        grid=(T // tm,),
        in_specs=[pl.BlockSpec((tm, D), lambda i: (jnp.minimum(i, npt - 1), 0)),
                  pl.BlockSpec((tm, D), lambda i: (jnp.maximum(i - npt, 0), 0)),
                  pl.BlockSpec((D, D), const2), pl.BlockSpec((tm, D), row),
                  _mod_spec(layer, 2, tm), pl.BlockSpec((1, D), const2), pl.BlockSpec((1, D), const2)],
        out_specs=pl.BlockSpec((tm, D), row),
        compiler_params=_params(1),
        name="odd_out_ln",
    )(a_prompt, a_latent, w, x, mods, ln_g, ln_b)


def _attn_ctx_kernel(qkv_ref, sink_ref, o_ref):
    for h in range(HC):
        kv = h // (HC // KVH)
        q = _bf(qkv_ref[:, h * HD:(h + 1) * HD])
        k = _bf(qkv_ref[:, HC * HD + kv * HD:HC * HD + (kv + 1) * HD])
        v = _bf(qkv_ref[:, (HC + KVH) * HD + kv * HD:(HC + KVH) * HD + (kv + 1) * HD])
        sink = sink_ref[h]
        sc = _mm_nt(q, k)
        m = jnp.maximum(jnp.max(sc, axis=1, keepdims=True), sink)
        p = jnp.exp(sc - m)
        den = jnp.sum(p, axis=1, keepdims=True) + jnp.exp(sink - m)
        o_ref[:, h * HD:(h + 1) * HD] = _mm(_bf(p), v) / den


def _attn_context(qkv, sink):
    return pl.pallas_call(
        _attn_ctx_kernel,
        out_shape=jax.ShapeDtypeStruct((NP_ROWS, HC * HD), F32),
        grid=(BATCH,),
        in_specs=[pl.BlockSpec((SEQ, QKV_W), lambda b: (b, 0)),
                  pl.BlockSpec(memory_space=pltpu.SMEM)],
        out_specs=pl.BlockSpec((SEQ, HC * HD), lambda b: (b, 0)),
        compiler_params=_params(1),
        name="attn_context",
    )(qkv, sink)


def _attn_lat_kernel(q_ref, kp_ref, kc_ref, kn_ref, vp_ref, vc_ref, vn_ref, ck_ref, cv_ref,
                     sink_ref, o_ref, bias_s):
    j = pl.program_id(1)
    nb = DEC_SEQ // QBLOCK
    r = lax.broadcasted_iota(jnp.int32, (QBLOCK, 3 * QBLOCK), 0)
    cc = lax.broadcasted_iota(jnp.int32, (QBLOCK, 3 * QBLOCK), 1)
    lo = jnp.where(j >= 1, 0, QBLOCK)
    hi = jnp.where(j <= nb - 2, 3 * QBLOCK, 2 * QBLOCK)
    ok = (jnp.abs(QBLOCK + r - cc) <= WINDOW) & (cc >= lo) & (cc < hi)
    bias_s[...] = jnp.where(ok, 0.0, -jnp.inf)
    for kv in range(KVH):
        ks = slice(kv * HD, (kv + 1) * HD)
        k_all = _bf(jnp.concatenate([ck_ref[0, 0, kv], kp_ref[:, ks], kc_ref[:, ks], kn_ref[:, ks]], axis=0))
        v_all = _bf(jnp.concatenate([cv_ref[0, 0, kv], vp_ref[:, ks], vc_ref[:, ks], vn_ref[:, ks]], axis=0))
        for h in range(kv * (HC // KVH), (kv + 1) * (HC // KVH)):
            sink = sink_ref[h]
            sc = _mm_nt(_bf(q_ref[:, h * HD:(h + 1) * HD]), k_all)
            s_ctx = sc[:, :PAST_LEN]
            s_loc = sc[:, PAST_LEN:] + bias_s[...]
            m = jnp.maximum(jnp.maximum(jnp.max(s_ctx, axis=1, keepdims=True),
                                        jnp.max(s_loc, axis=1, keepdims=True)), sink)
            p_ctx = jnp.exp(s_ctx - m)
            p_loc = jnp.exp(s_loc - m)
            den = (jnp.sum(p_ctx, axis=1, keepdims=True) + jnp.sum(p_loc, axis=1, keepdims=True)
                   + jnp.exp(sink - m))
            num = _mm(_bf(jnp.concatenate([p_ctx, p_loc], axis=1)), v_all)
            o_ref[:, h * HD:(h + 1) * HD] = num / den


def _attn_latent(qkv, cache_k, cache_v, sink, o):
    nb = DEC_SEQ // QBLOCK
    base = NP_ROWS // QBLOCK
    blk = lambda b, j: base + b * nb + j
    prev = lambda b, j: base + b * nb + jnp.maximum(j - 1, 0)
    nxt = lambda b, j: base + b * nb + jnp.minimum(j + 1, nb - 1)
    kcol, vcol = HC * HD // 256, HC * HD // 256 + 1
    cache_spec = pl.BlockSpec((1, 1, KVH, PAST_LEN, HD), lambda b, j: (b, o, 0, 0, 0))
    return pl.pallas_call(
        _attn_lat_kernel,
        out_shape=jax.ShapeDtypeStruct((NS_ROWS, HC * HD), F32),
        grid=(DEC_BATCH, nb),
        in_specs=[pl.BlockSpec((QBLOCK, HC * HD), lambda b, j: (blk(b, j), 0)),
                  pl.BlockSpec((QBLOCK, 256), lambda b, j: (prev(b, j), kcol)),
                  pl.BlockSpec((QBLOCK, 256), lambda b, j: (blk(b, j), kcol)),
                  pl.BlockSpec((QBLOCK, 256), lambda b, j: (nxt(b, j), kcol)),
                  pl.BlockSpec((QBLOCK, 256), lambda b, j: (prev(b, j), vcol)),
                  pl.BlockSpec((QBLOCK, 256), lambda b, j: (blk(b, j), vcol)),
                  pl.BlockSpec((QBLOCK, 256), lambda b, j: (nxt(b, j), vcol)),
                  cache_spec, cache_spec,
                  pl.BlockSpec(memory_space=pltpu.SMEM)],
        out_specs=pl.BlockSpec((QBLOCK, HC * HD), lambda b, j: (b * nb + j, 0)),
        scratch_shapes=[pltpu.VMEM((QBLOCK, 3 * QBLOCK), F32)],
        compiler_params=_params(2),
        name="attn_latent",
    )(qkv, qkv, qkv, qkv, qkv, qkv, qkv, cache_k, cache_v, sink)


def _router_kernel(x_ref, sh_ref, sc_ref, w_ref, b_ref, o_ref):
    xm = x_ref[...] * (1.0 + sc_ref[0]) + sh_ref[0]
    lg = _mm(xm, w_ref[...], HI) + b_ref[...]
    lane = lax.broadcasted_iota(jnp.int32, lg.shape, 1)
    neg = -jnp.inf
    big = 1 << 20
    is_grp = jnp.logical_and(lane >= N_EXPERTS, lane < N_EXPERTS + N_GROUPS)
    mg = jnp.max(jnp.where(is_grp, lg, neg), axis=1, keepdims=True)
    g_lane = jnp.min(jnp.where(jnp.logical_and(is_grp, lg == mg), lane, big), axis=1, keepdims=True)
    g_w = 1.0 / jnp.sum(jnp.where(is_grp, jnp.exp(lg - mg), 0.0), axis=1, keepdims=True)
    g_idx = g_lane - N_EXPERTS
    in_grp = jnp.logical_and(lane >= g_idx * EPG, lane < (g_idx + 1) * EPG)
    v1 = jnp.max(jnp.where(in_grp, lg, neg), axis=1, keepdims=True)
    i1 = jnp.min(jnp.where(jnp.logical_and(in_grp, lg == v1), lane, big), axis=1, keepdims=True)
    rest = jnp.logical_and(in_grp, lane != i1)
    v2 = jnp.max(jnp.where(rest, lg, neg), axis=1, keepdims=True)
    i2 = jnp.min(jnp.where(jnp.logical_and(rest, lg == v2), lane, big), axis=1, keepdims=True)
    e2 = jnp.exp(v2 - v1)
    p1 = 1.0 / (1.0 + e2)
    p2 = e2 / (1.0 + e2)
    o_ref[...] = jnp.where(lane == 0, i1.astype(F32),
                           jnp.where(lane == 1, i2.astype(F32),
                                     jnp.where(lane == 2, p1 * g_w, jnp.where(lane == 3, p2 * g_w, 0.0))))


def _router(x, mods, layer, w, b):
    tm = 512
    return pl.pallas_call(
        _router_kernel,
        out_shape=jax.ShapeDtypeStruct((T, 128), F32),
        grid=(T // tm,),
        in_specs=[pl.BlockSpec((tm, D), lambda i: (i, 0)),
                  _mod_spec(layer, 3, tm), _mod_spec(layer, 4, tm),
                  pl.BlockSpec((D, 128), lambda i: (0, 0)), pl.BlockSpec((1, 128), lambda i: (0, 0))],
        out_specs=pl.BlockSpec((tm, 128), lambda i: (i, 0)),
        compiler_params=_params(1),
        name="moe_router",
    )(x, mods, mods, w, b)


MOE_R = 256
MOE_TILES = 2 * T // MOE_R + N_EXPERTS
MOE_ROWS = MOE_TILES * MOE_R
MOE_TM = 256
N_USED_LANE = 255


def _chosen(meta, lane):
    i1 = meta[:, 0:1].astype(jnp.int32)
    i2 = meta[:, 1:2].astype(jnp.int32)
    return lane == i1, lane == i2


def _tiles_per_expert(cnt):
    return jnp.floor((cnt + (MOE_R - 1.0)) * (1.0 / MOE_R))


def _plan_rank_kernel(meta_ref, rank_ref, cnt_ref, carry_s):
    @pl.when(pl.program_id(0) == 0)
    def _():
        carry_s[...] = jnp.zeros_like(carry_s)

    meta = meta_ref[...]
    tm = meta.shape[0]
    lane = lax.broadcasted_iota(jnp.int32, meta.shape, 1)
    s1, s2 = _chosen(meta, lane)
    sel = jnp.logical_or(s1, s2)
    r = lax.broadcasted_iota(jnp.int32, (tm, tm), 0)
    c = lax.broadcasted_iota(jnp.int32, (tm, tm), 1)
    rank_ref[...] = _mm((c < r).astype(BF16), sel.astype(BF16)) + carry_s[0:1, :]
    carry_s[...] = carry_s[...] + jnp.sum(sel.astype(F32), axis=0, keepdims=True)
    cnt_ref[...] = carry_s[...]


def _plan_rank(meta):
    tm = 512
    return pl.pallas_call(
        _plan_rank_kernel,
        out_shape=(jax.ShapeDtypeStruct((T, 128), F32), jax.ShapeDtypeStruct((8, 128), F32)),
        grid=(T // tm,),
        in_specs=[pl.BlockSpec((tm, 128), lambda i: (i, 0))],
        out_specs=(pl.BlockSpec((tm, 128), lambda i: (i, 0)), pl.BlockSpec((8, 128), lambda i: (0, 0))),
        scratch_shapes=[pltpu.VMEM((8, 128), F32)],
        compiler_params=_params(1),
        name="moe_plan_rank",
    )(meta)


def _plan_dest_kernel(meta_ref, rank_ref, cnt_ref, dest_ref):
    nt = _tiles_per_expert(cnt_ref[...])
    r = lax.broadcasted_iota(jnp.int32, (128, 128), 0)
    c = lax.broadcasted_iota(jnp.int32, (128, 128), 1)
    start = _mm(_bf(nt), (r < c).astype(BF16))
    base = start[0:1, :] * MOE_R + rank_ref[...]
    meta = meta_ref[...]
    lane = lax.broadcasted_iota(jnp.int32, meta.shape, 1)
    s1, s2 = _chosen(meta, lane)
    d1 = jnp.sum(jnp.where(s1, base, 0.0), axis=1, keepdims=True)
    d2 = jnp.sum(jnp.where(s2, base, 0.0), axis=1, keepdims=True)
    dest_ref[...] = jnp.where(lane == 0, d1, jnp.where(lane == 1, d2, 0.0)).astype(jnp.int32)


def _plan_dest(meta, rank, cnt):
    tm = 512
    row = lambda i: (i, 0)
    return pl.pallas_call(
        _plan_dest_kernel,
        out_shape=jax.ShapeDtypeStruct((T, 128), jnp.int32),
        grid=(T // tm,),
        in_specs=[pl.BlockSpec((tm, 128), row), pl.BlockSpec((tm, 128), row),
                  pl.BlockSpec((8, 128), lambda i: (0, 0))],
        out_specs=pl.BlockSpec((tm, 128), row),
        compiler_params=_params(1),
        name="moe_plan_dest",
    )(meta, rank, cnt)


def _plan_tiles_kernel(cnt_ref, te_ref):
    nt = _tiles_per_expert(cnt_ref[...])
    ntb = jnp.broadcast_to(nt[0:1, :], (128, 128))
    r = lax.broadcasted_iota(jnp.int32, (128, 128), 0)
    c = lax.broadcasted_iota(jnp.int32, (128, 128), 1)
    end_col = jnp.sum(jnp.where(c <= r, ntb, 0.0), axis=1, keepdims=True)
    ti = lax.broadcasted_iota(jnp.int32, (128, 256), 1)
    rr = lax.broadcasted_iota(jnp.int32, (128, 256), 0)
    done = jnp.logical_and(end_col <= ti.astype(F32), rr < N_EXPERTS)
    te = jnp.minimum(jnp.sum(done.astype(F32), axis=0, keepdims=True), N_EXPERTS - 1.0)
    n_used = end_col[N_EXPERTS - 1:N_EXPERTS, :]
    lane = lax.broadcasted_iota(jnp.int32, (8, 256), 1)
    te_ref[...] = jnp.where(lane == N_USED_LANE, n_used, jnp.broadcast_to(te, (8, 256))).astype(jnp.int32)


def _plan_tiles(cnt):
    return pl.pallas_call(
        _plan_tiles_kernel,
        out_shape=jax.ShapeDtypeStruct((8, 256), jnp.int32),
        name="moe_plan_tiles",
    )(cnt)


def _dispatch_kernel(dest_hbm, x_ref, sh_ref, sc_ref, xs_in, xs_out, dsm, xw, sem_d, sem):
    del xs_in
    i = pl.program_id(0)
    dcp = pltpu.make_async_copy(dest_hbm.at[i], dsm, sem_d)
    dcp.start()
    xm = x_ref[...] * (1.0 + sc_ref[0]) + sh_ref[0]
    hi = lax.bitcast_convert_type(_bf(xm[:, :D // 2]).astype(F32), jnp.uint32)
    lo = lax.bitcast_convert_type(_bf(xm[:, D // 2:]).astype(F32), jnp.uint32)
    xw[...] = hi | (lo >> 16)
    dcp.wait()

    def row_copy(r, d):
        return pltpu.make_async_copy(xw.at[pl.ds(r, 1)], xs_out.at[pl.ds(d, 1)], sem)

    def issue(r, carry):
        row_copy(r, dsm[0, r]).start()
        row_copy(r, dsm[1, r]).start()
        return carry

    def drain(r, carry):
        row_copy(0, 0).wait()
        row_copy(0, 0).wait()
        return carry

    lax.fori_loop(0, MOE_TM, issue, 0, unroll=8)
    lax.fori_loop(0, MOE_TM, drain, 0, unroll=8)


def _dispatch(dest_t, x, mods, layer):
    tm = MOE_TM
    return pl.pallas_call(
        _dispatch_kernel,
        out_shape=jax.ShapeDtypeStruct((MOE_ROWS, D // 2), jnp.uint32),
        grid=(T // tm,),
        in_specs=[pl.BlockSpec(memory_space=pl.ANY),
                  pl.BlockSpec((tm, D), lambda i: (i, 0)),
                  _mod_spec(layer, 3, tm), _mod_spec(layer, 4, tm),
                  pl.BlockSpec(memory_space=pl.ANY)],
        out_specs=pl.BlockSpec(memory_space=pl.ANY),
        input_output_aliases={4: 0},
        scratch_shapes=[pltpu.SMEM((2, tm), jnp.int32), pltpu.VMEM((tm, D // 2), jnp.uint32),
                        pltpu.SemaphoreType.DMA(()), pltpu.SemaphoreType.DMA(())],
        compiler_params=_params(1),
        name="moe_dispatch",
    )(dest_t, x, mods, mods, jnp.zeros((MOE_ROWS, D // 2), jnp.uint32))


def _expert_kernel(tiles_ref, xs_ref, wg_ref, wu_ref, wd_ref, y_ref):
    i = pl.program_id(0)
    used = i < tiles_ref[N_USED_LANE]

    @pl.when(used)
    def _():
        w = xs_ref[...]
        left = lax.bitcast_convert_type(w & jnp.uint32(0xFFFF0000), F32).astype(BF16)
        right = lax.bitcast_convert_type(w << 16, F32).astype(BF16)
        xm = jnp.concatenate([left, right], axis=1)
        a = _mm(xm, wg_ref[0])
        u = _mm(xm, wu_ref[0])
        y_ref[...] = _mm(_bf((a * _sigmoid(a)) * u), wd_ref[0])

    @pl.when(jnp.logical_not(used))
    def _():
        y_ref[...] = jnp.zeros_like(y_ref)


def _experts(tiles, xs, wg, wu, wd):
    grid_spec = pltpu.PrefetchScalarGridSpec(
        num_scalar_prefetch=1,
        grid=(MOE_TILES,),
        in_specs=[pl.BlockSpec((MOE_R, D // 2), lambda i, t: (i, 0)),
                  pl.BlockSpec((1, D, EXPERT_FF), lambda i, t: (t[i], 0, 0)),
                  pl.BlockSpec((1, D, EXPERT_FF), lambda i, t: (t[i], 0, 0)),
                  pl.BlockSpec((1, EXPERT_FF, D), lambda i, t: (t[i], 0, 0))],
        out_specs=pl.BlockSpec((MOE_R, D), lambda i, t: (i, 0)))
    return pl.pallas_call(
        _expert_kernel,
        out_shape=jax.ShapeDtypeStruct((MOE_ROWS, D), F32),
        grid_spec=grid_spec,
        compiler_params=_params(1),
        name="moe_experts",
    )(tiles, xs, wg, wu, wd)


def _combine_kernel(dest_hbm, y_hbm, meta_ref, x_ref, gate_ref, g_ref, b_ref, o_ref, dsm, y1, y2, sem_d, sem):
    i = pl.program_id(0)
    dcp = pltpu.make_async_copy(dest_hbm.at[i], dsm, sem_d)
    dcp.start()
    dcp.wait()

    def row_copy(d, buf, r):
        return pltpu.make_async_copy(y_hbm.at[pl.ds(d, 1)], buf.at[pl.ds(r, 1)], sem)

    def issue(r, carry):
        row_copy(dsm[0, r], y1, r).start()
        row_copy(dsm[1, r], y2, r).start()
        return carry

    def drain(r, carry):
        row_copy(0, y1, 0).wait()
        row_copy(0, y2, 0).wait()
        return carry

    lax.fori_loop(0, MOE_TM, issue, 0, unroll=8)
    lax.fori_loop(0, MOE_TM, drain, 0, unroll=8)
    meta = meta_ref[...]
    y = meta[:, 2:3] * y1[...] + meta[:, 3:4] * y2[...]
    o_ref[...] = _residual_ln(x_ref[...], gate_ref[0], y, g_ref[...], b_ref[...])


def _combine(dest_t, ys, meta, x, mods, layer, ln_g, ln_b):
    tm = MOE_TM
    row = lambda i: (i, 0)
    const2 = lambda i: (0, 0)
    return pl.pallas_call(
        _combine_kernel,
        out_shape=jax.ShapeDtypeStruct((T, D), F32),
        grid=(T // tm,),
        in_specs=[pl.BlockSpec(memory_space=pl.ANY), pl.BlockSpec(memory_space=pl.ANY),
                  pl.BlockSpec((tm, 128), row), pl.BlockSpec((tm, D), row),
                  _mod_spec(layer, 5, tm), pl.BlockSpec((1, D), const2), pl.BlockSpec((1, D), const2)],
        out_specs=pl.BlockSpec((tm, D), row),
        scratch_shapes=[pltpu.SMEM((2, tm), jnp.int32), pltpu.VMEM((tm, D), F32), pltpu.VMEM((tm, D), F32),
                        pltpu.SemaphoreType.DMA(()), pltpu.SemaphoreType.DMA(())],
        compiler_params=_params(1),
        name="moe_combine",
    )(dest_t, ys, meta, x, mods, ln_g, ln_b)


def _moe(x, mods, layer, w_r, b_r, wg, wu, wd, ln_g, ln_b):
    meta = _router(x, mods, layer, w_r, b_r)
    rank, cnt = _plan_rank(meta)
    dest = _plan_dest(meta, rank, cnt)
    tiles = _plan_tiles(cnt)[0]
    dest_t = dest[:, :2].T.reshape(2, T // MOE_TM, MOE_TM).transpose(1, 0, 2)
    xs = _dispatch(dest_t, x, mods, layer)
    ys = _experts(tiles, xs, wg, wu, wd)
    return _combine(dest_t, ys, meta, x, mods, layer, ln_g, ln_b)


def _moe_dense_kernel(x_ref, sh_ref, sc_ref, meta_ref, wg_ref, wu_ref, wd_ref, gate_ref, g_ref, b_ref, o_ref,
                      xm_s, acc_s):
    grp = pl.program_id(1)

    @pl.when(grp == 0)
    def _():
        xm_s[...] = _bf(x_ref[...] * (1.0 + sc_ref[0]) + sh_ref[0])
        acc_s[...] = jnp.zeros_like(acc_s)

    xm = xm_s[...]
    meta = meta_ref[...]
    i1 = meta[:, 0:1].astype(jnp.int32)
    i2 = meta[:, 1:2].astype(jnp.int32)
    w1 = meta[:, 2:3]
    w2 = meta[:, 3:4]
    hid = []
    for e in range(EPG):
        eid = grp * EPG + e
        gate = jnp.where(i1 == eid, w1, 0.0) + jnp.where(i2 == eid, w2, 0.0)
        a = _mm(xm, wg_ref[0, e])
        u = _mm(xm, wu_ref[0, e])
        hid.append(_bf((a * _sigmoid(a)) * u * gate))
    acc_s[...] += _mm(jnp.concatenate(hid, axis=1), wd_ref[0])

    @pl.when(grp == N_GROUPS - 1)
    def _():
        o_ref[...] = _residual_ln(x_ref[...], gate_ref[0], acc_s[...], g_ref[...], b_ref[...])


def _moe_dense(x, mods, layer, meta, wg, wu, wd, ln_g, ln_b):
    tm = 512
    row = lambda i, g: (i, 0)
    const2 = lambda i, g: (0, 0)
    return pl.pallas_call(
        _moe_dense_kernel,
        out_shape=jax.ShapeDtypeStruct((T, D), F32),
        grid=(T // tm, N_GROUPS),
        in_specs=[pl.BlockSpec((tm, D), row), _mod_spec(layer, 3, tm), _mod_spec(layer, 4, tm),
                  pl.BlockSpec((tm, 128), row),
                  pl.BlockSpec((1, EPG, D, EXPERT_FF), lambda i, g: (g, 0, 0, 0)),
                  pl.BlockSpec((1, EPG, D, EXPERT_FF), lambda i, g: (g, 0, 0, 0)),
                  pl.BlockSpec((1, EPG * EXPERT_FF, D), lambda i, g: (g, 0, 0)),
                  _mod_spec(layer, 5, tm), pl.BlockSpec((1, D), const2), pl.BlockSpec((1, D), const2)],
        out_specs=pl.BlockSpec((tm, D), row),
        scratch_shapes=[pltpu.VMEM((tm, D), BF16), pltpu.VMEM((tm, D), F32)],
        compiler_params=_params(2),
        name="moe_dense",
    )(x, mods, mods, meta, wg, wu, wd, mods, ln_g, ln_b)


def _permute_even_w(w):
    a_end = 4 * HA * DH
    g_end = a_end + 4 * HA
    c_end = g_end + 3 * HB * DH
    z_end = c_end + HB * DH
    small = jnp.concatenate([w[:, a_end:g_end], w[:, z_end:]], axis=1)
    pad = jnp.zeros((w.shape[0], 128 - small.shape[1]), w.dtype)
    return jnp.concatenate([w[:, :a_end], w[:, g_end:c_end], w[:, c_end:z_end], small, pad], axis=1)


def _lane_row(vals, offset):
    return jnp.zeros((1, 128), F32).at[0, offset:offset + vals.shape[0]].set(vals.astype(F32))


def _pair_rows(vals, offset):
    v32 = jnp.zeros((32,), F32).at[offset:offset + vals.shape[0]].set(vals.astype(F32))
    return jnp.repeat(v32.reshape(16, 2), CHUNK, axis=1)


def kernel(x_prompt, x_sample, c, c_ctx, state_mlstm_c, state_mlstm_n, state_mlstm_m, state_delta, cache_k, cache_v, w_mod, b_mod, ln_g, ln_b, w_in_even, mlstm_gate_b, mlstm_norm_g, delta_conv_w, delta_a_log, delta_dt_bias, delta_norm_g, w_out_even, w_qkv_odd, attn_sink, w_out_odd, w_grp, b_grp, w_erouter, b_erouter, w_gate, w_up, w_down):
    x = jnp.concatenate([x_prompt.reshape(NP_ROWS, D), x_sample.reshape(NS_ROWS, D)], axis=0)
    cvecs = jnp.concatenate([c_ctx[None, :], c, jnp.zeros((N_MOD_ROWS - 1 - DEC_BATCH, D), F32)], axis=0)
    mods = _modulation(cvecs, w_mod, b_mod)
    tables = _rope_tables()
    m0_all = jnp.broadcast_to(state_mlstm_m.reshape(DEC_BATCH, N_EVEN, 2 * HA, 1), (DEC_BATCH, N_EVEN, 2 * HA, 128))

    out_mc, out_mn, out_mm, out_ds, out_k, out_v = [], [], [], [], [], []
    for l in range(DEPTH):
        if l % 2 == 0:
            e = l // 2
            proj = _even_proj(x, mods, l, _bf(_permute_even_w(w_in_even[e])))
            gates_t2 = (proj[:, EVEN_W - 128:EVEN_W - 96].reshape(T // CHUNK, CHUNK, 32).transpose(0, 2, 1)
                        .reshape(T // CHUNK, 16, 128))
            gb = mlstm_gate_b[e].reshape(-1)
            *h_scan, mc, mn, mm = _mlstm_scan(proj, gates_t2, _lane_row(gb, 0), _pair_rows(gb, 0),
                                              state_mlstm_c, state_mlstm_n, m0_all, e)
            qkv = _delta_prep(proj, delta_conv_w[e])
            dtb = delta_dt_bias[e].reshape(-1)
            nea = -jnp.exp(delta_a_log[e].astype(F32)).reshape(-1)
            *o_scan, ds = _delta_scan(qkv, proj, gates_t2, _lane_row(dtb, 24), _pair_rows(dtb, 24),
                                      _lane_row(nea, 24), _pair_rows(nea, 24), state_delta, e)
            x = _even_out(h_scan, o_scan, proj, mlstm_norm_g[e][None, :], delta_norm_g[e][None, :],
                          _bf(w_out_even[e]), x, mods, l, ln_g[l, 0][None, :], ln_b[l, 0][None, :])
            out_mc.append(mc)
            out_mn.append(mn)
            out_mm.append(mm[:, :, 0].reshape(BATCH, 2, HA))
            out_ds.append(ds)
        else:
            o = l // 2
            qkv = _odd_proj(x, mods, l, _bf(w_qkv_odd[o]), tables)
            a_p = _attn_context(qkv, attn_sink[o])
            a_s = _attn_latent(qkv, cache_k, cache_v, attn_sink[o], o)
            x = _odd_out(a_p, a_s, _bf(w_out_odd[o]), x, mods, l, ln_g[l, 0][None, :], ln_b[l, 0][None, :])
            kp = qkv[:NP_ROWS, HC * HD:(HC + KVH) * HD].reshape(BATCH, SEQ, KVH, HD).transpose(0, 2, 1, 3)
            vp = qkv[:NP_ROWS, (HC + KVH) * HD:].reshape(BATCH, SEQ, KVH, HD).transpose(0, 2, 1, 3)
            out_k.append(kp)
            out_v.append(vp)
        w_r = jnp.concatenate([w_erouter[l].transpose(1, 0, 2).reshape(D, N_EXPERTS), w_grp[l],
                               jnp.zeros((D, 128 - N_EXPERTS - N_GROUPS), F32)], axis=1)
        b_r = jnp.concatenate([b_erouter[l].reshape(-1), b_grp[l],
                               jnp.zeros((128 - N_EXPERTS - N_GROUPS,), F32)])[None, :]
        meta = _router(x, mods, l, w_r, b_r)
        x = _moe_dense(x, mods, l, meta,
                       _bf(w_gate[l]).reshape(N_GROUPS, EPG, D, EXPERT_FF),
                       _bf(w_up[l]).reshape(N_GROUPS, EPG, D, EXPERT_FF),
                       _bf(w_down[l]).reshape(N_GROUPS, EPG * EXPERT_FF, D),
                       ln_g[l, 1][None, :], ln_b[l, 1][None, :])
    return (x[:NP_ROWS].reshape(BATCH, SEQ, D), x[NP_ROWS:].reshape(DEC_BATCH, DEC_SEQ, D),
            jnp.stack(out_mc, 1), jnp.stack(out_mn, 1), jnp.stack(out_mm, 1), jnp.stack(out_ds, 1),
            jnp.stack(out_k, 1), jnp.stack(out_v, 1))
```

```python
import functools

import jax
import jax.numpy as jnp
from jax import lax
from jax.experimental import pallas as pl
from jax.experimental.pallas import tpu as pltpu

F32 = jnp.float32
BF16 = jnp.bfloat16
HI = lax.Precision.HIGHEST

D = 1024
BATCH = 32
SEQ = 256
DEPTH = 4
DEC_BATCH = 2
DEC_SEQ = 4096
PAST_LEN = 512
GRID_W = 64
N_EVEN = 2
N_ODD = 2
HA = 4
HB = 4
DH = 128
CHUNK = 64
HC = 16
KVH = 4
HD = 64
WINDOW = 128
QBLOCK = 128
ROPE_THETA = 10000.0
N_GROUPS = 4
EPG = 4
N_EXPERTS = 16
EXPERT_FF = 256
DN_ALPHA = (2 * DEPTH) ** 0.25
LN_EPS = 1e-5

NP_ROWS = BATCH * SEQ
NS_ROWS = DEC_BATCH * DEC_SEQ
T = NP_ROWS + NS_ROWS
N_MOD_ROWS = 8
EVEN_W = 4224
QKV_W = (HC + 2 * KVH) * HD

P_CHUNKS = SEQ // CHUNK
S_CHUNKS = DEC_SEQ // CHUNK
P_STEPS = BATCH * P_CHUNKS
S_STEPS = DEC_BATCH * S_CHUNKS
P_BLOCKS = NP_ROWS // CHUNK

VMEM_LIMIT = 48 * 1024 * 1024


def _params(n_axes):
    return pltpu.CompilerParams(dimension_semantics=("arbitrary",) * n_axes,
                                vmem_limit_bytes=VMEM_LIMIT)


def _mm(a, b, prec=None):
    return lax.dot_general(a, b, (((1,), (0,)), ((), ())), precision=prec, preferred_element_type=F32)


def _mm_nt(a, b, prec=None):
    return lax.dot_general(a, b, (((1,), (1,)), ((), ())), precision=prec, preferred_element_type=F32)


def _mm_tn(a, b, prec=None):
    return lax.dot_general(a, b, (((0,), (0,)), ((), ())), precision=prec, preferred_element_type=F32)


def _bf(x):
    return x.astype(BF16)


def _sigmoid(x):
    return 1.0 / (1.0 + jnp.exp(-x))


def _softplus(x):
    return jnp.maximum(x, 0.0) + jnp.log1p(jnp.exp(-jnp.abs(x)))


def _log_sigmoid(x):
    return jnp.minimum(x, 0.0) - jnp.log1p(jnp.exp(-jnp.abs(x)))


def _mod_row(tile, tm):
    npt = NP_ROWS // tm
    per = DEC_SEQ // tm
    return jnp.where(tile < npt, 0, 1 + (tile - npt) // per)


def _mod_spec(layer, chunk, tm):
    def imap(i, *_):
        return ((layer * N_MOD_ROWS + _mod_row(i, tm)) * 6 + chunk, 0, 0)
    return pl.BlockSpec((1, 1, D), imap)


def _modulation_kernel(c_ref, w_ref, b_ref, o_ref):
    x = c_ref[...]
    s = x * _sigmoid(x)
    o_ref[0] = _mm(s, w_ref[0], HI) + b_ref[0]


def _modulation(cvecs, w_mod, b_mod):
    out = pl.pallas_call(
        _modulation_kernel,
        out_shape=jax.ShapeDtypeStruct((DEPTH, N_MOD_ROWS, 6 * D), F32),
        grid=(DEPTH, 6),
        in_specs=[pl.BlockSpec((N_MOD_ROWS, D), lambda l, j: (0, 0)),
                  pl.BlockSpec((1, D, D), lambda l, j: (l, 0, j)),
                  pl.BlockSpec((1, 1, D), lambda l, j: (l * 6 + j, 0, 0))],
        out_specs=pl.BlockSpec((1, N_MOD_ROWS, D), lambda l, j: (l, 0, j)),
        compiler_params=_params(2),
        name="modulation",
    )(cvecs, w_mod, b_mod.reshape(DEPTH * 6, 1, D))
    return out.reshape(DEPTH * N_MOD_ROWS * 6, 1, D)


def _proj_kernel(x_ref, sh_ref, sc_ref, w_ref, o_ref):
    xm = x_ref[...] * (1.0 + sc_ref[0]) + sh_ref[0]
    o_ref[...] = _mm(_bf(xm), w_ref[...])


def _even_proj(x, mods, layer, w):
    tm = 256
    return pl.pallas_call(
        _proj_kernel,
        out_shape=jax.ShapeDtypeStruct((T, EVEN_W), F32),
        grid=(T // tm,),
        in_specs=[pl.BlockSpec((tm, D), lambda i: (i, 0)),
                  _mod_spec(layer, 0, tm), _mod_spec(layer, 1, tm),
                  pl.BlockSpec((D, EVEN_W), lambda i: (0, 0))],
        out_specs=pl.BlockSpec((tm, EVEN_W), lambda i: (i, 0)),
        compiler_params=_params(1),
        name="even_proj",
    )(x, mods, mods, w)


def _qkv_kernel(x_ref, sh_ref, sc_ref, w_ref, cos_ref, sa_ref, sb_ref, o_ref, *, tm):
    i = pl.program_id(0)
    xm = x_ref[...] * (1.0 + sc_ref[0]) + sh_ref[0]
    acc = _mm(_bf(xm), w_ref[...])
    n_q = HC * HD // 128
    n_k = KVH * HD // 128
    is_latent = i >= NP_ROWS // tm
    cos = jnp.where(is_latent, cos_ref[...], 1.0)
    sa = jnp.where(is_latent, sa_ref[...], 0.0)
    sb = jnp.where(is_latent, sb_ref[...], 0.0)
    for g in range(n_q + n_k):
        blk = acc[:, g * 128:(g + 1) * 128]
        if g < n_q:
            blk = blk * (HD ** -0.5)
        rot = blk * cos + pltpu.roll(blk, 112, 1) * sa + pltpu.roll(blk, 16, 1) * sb
        o_ref[:, g * 128:(g + 1) * 128] = rot
    o_ref[:, (n_q + n_k) * 128:] = acc[:, (n_q + n_k) * 128:]


def _rope_tables():
    half = HD // 4
    inv = ROPE_THETA ** (-jnp.arange(half, dtype=F32) / half)
    pos = jnp.arange(DEC_SEQ)
    row = (pos // GRID_W).astype(F32)[:, None] * inv[None, :]
    col = (pos % GRID_W).astype(F32)[:, None] * inv[None, :]
    cos = jnp.concatenate([jnp.cos(row), jnp.cos(row), jnp.cos(col), jnp.cos(col)], axis=-1)
    sin = jnp.concatenate([jnp.sin(row), jnp.sin(row), jnp.sin(col), jnp.sin(col)], axis=-1)
    first = (jnp.arange(HD) % 32) < 16
    sa = jnp.where(first, -sin, 0.0)
    sb = jnp.where(first, 0.0, sin)
    tile2 = lambda t: jnp.concatenate([t, t], axis=-1)
    return tile2(cos), tile2(sa), tile2(sb)


def _odd_proj(x, mods, layer, w, tables):
    tm = 256
    npt = NP_ROWS // tm
    per = DEC_SEQ // tm
    tab_spec = pl.BlockSpec((tm, 128), lambda i: (jnp.where(i < npt, 0, (i - npt) % per), 0))
    return pl.pallas_call(
        functools.partial(_qkv_kernel, tm=tm),
        out_shape=jax.ShapeDtypeStruct((T, QKV_W), F32),
        grid=(T // tm,),
        in_specs=[pl.BlockSpec((tm, D), lambda i: (i, 0)),
                  _mod_spec(layer, 0, tm), _mod_spec(layer, 1, tm),
                  pl.BlockSpec((D, QKV_W), lambda i: (0, 0)),
                  tab_spec, tab_spec, tab_spec],
        out_specs=pl.BlockSpec((tm, QKV_W), lambda i: (i, 0)),
        compiler_params=_params(1),
        name="odd_qkv_proj",
    )(x, mods, mods, w, *tables)


PREP_ROWS = 256


def _delta_prep_kernel(x_ref, prev_ref, next_ref, w_ref, o_ref):
    i = pl.program_id(0)
    part = pl.program_id(1)
    npb = NP_ROWS // PREP_ROWS
    per = DEC_SEQ // PREP_ROWS
    is_latent = i >= npb
    pos = (i - npb) % per
    has_prev = jnp.logical_and(is_latent, pos > 0)
    has_next = jnp.logical_and(is_latent, pos < per - 1)
    x = x_ref[...]
    w = w_ref[...]
    rows = lax.broadcasted_iota(jnp.int32, x.shape, 0)
    prev_row = jnp.where(has_prev, prev_ref[7:8, :], 0.0)
    next_row = jnp.where(has_next, next_ref[0:1, :], 0.0)
    xm1 = jnp.where(rows == 0, prev_row, pltpu.roll(x, 1, 0))
    xp1 = jnp.where(rows == PREP_ROWS - 1, next_row, pltpu.roll(x, PREP_ROWS - 1, 0))
    y = xm1 * w[0:1, :] + x * w[1:2, :] + xp1 * w[2:3, :]
    y = y * _sigmoid(y)
    q_scale = jnp.where(part == 0, DH ** -0.5, 1.0)
    for h in range(4):
        yh = y[:, h * DH:(h + 1) * DH]
        inv = lax.rsqrt(jnp.sum(yh * yh, axis=-1, keepdims=True) + 1e-6)
        scale = jnp.where(part == 2, 1.0, inv * q_scale)
        o_ref[:, h * DH:(h + 1) * DH] = yh * scale


def _delta_prep(proj, conv_w):
    nblk = T // PREP_ROWS
    sub = PREP_ROWS // 8
    last8 = T // 8 - 1
    return pl.pallas_call(
        _delta_prep_kernel,
        out_shape=jax.ShapeDtypeStruct((T, 3 * 512), F32),
        grid=(nblk, 3),
        in_specs=[pl.BlockSpec((PREP_ROWS, 512), lambda i, p: (i, 4 + p)),
                  pl.BlockSpec((8, 512), lambda i, p: (jnp.maximum(i * sub - 1, 0), 4 + p)),
                  pl.BlockSpec((8, 512), lambda i, p: (jnp.minimum((i + 1) * sub, last8), 4 + p)),
                  pl.BlockSpec((3, 512), lambda i, p: (0, p))],
        out_specs=pl.BlockSpec((PREP_ROWS, 512), lambda i, p: (i, p)),
        compiler_params=_params(2),
        name="delta_prep",
    )(proj, proj, proj, conv_w)


assert P_STEPS == S_STEPS
SCAN_STEPS = P_STEPS


def _bwd_local(s, nc):
    return (s // nc) * nc + nc - 1 - s % nc


def _scan_blocks():
    return (lambda s: s, lambda s: _bwd_local(s, P_CHUNKS),
            lambda s: P_BLOCKS + s, lambda s: P_BLOCKS + _bwd_local(s, S_CHUNKS))


def _scan_specs():
    gcol = EVEN_W // 128 - 1
    blocks = _scan_blocks()
    return ([pl.BlockSpec((CHUNK, 3 * 512), lambda s, f=f: (f(s), 0)) for f in blocks]
            + [pl.BlockSpec((CHUNK, 128), lambda s, f=f: (f(s), gcol)) for f in blocks]
            + [pl.BlockSpec((1, 16, 128), lambda s, f=f: (f(s), 0, 0)) for f in blocks])


def _scan_out_specs():
    local = (lambda s: s, lambda s: _bwd_local(s, P_CHUNKS), lambda s: s, lambda s: _bwd_local(s, S_CHUNKS))
    return [pl.BlockSpec((CHUNK, 512), lambda s, f=f: (f(s), 0)) for f in local]


def _mlstm_kernel(xpf_ref, xpb_ref, xsf_ref, xsb_ref, gpf_ref, gpb_ref, gsf_ref, gsb_ref,
                  rpf_ref, rpb_ref, rsf_ref, rsb_ref,
                  brow_ref, b2_ref, c0_ref, n0_ref, m0_ref,
                  hpf_ref, hpb_ref, hsf_ref, hsb_ref, c_out, n_out, m_out,
                  cp_s, np_s, mp_s, cs_s, ns_s, ms_s):
    s = pl.program_id(0)
    jp = s % P_CHUNKS
    js = s % S_CHUNKS

    @pl.when(jp == 0)
    def _():
        cp_s[...] = jnp.zeros_like(cp_s)
        np_s[...] = jnp.zeros_like(np_s)
        mp_s[...] = jnp.zeros_like(mp_s)

    @pl.when(js == 0)
    def _():
        for d in range(2):
            for h in range(HA):
                cs_s[d * HA + h] = c0_ref[0, 0, d, h]
            ns_s[d * HA:(d + 1) * HA, :] = n0_ref[0, 0, d]
        ms_s[...] = m0_ref[0, 0]

    states = ((cp_s, np_s, mp_s), (cs_s, ns_s, ms_s))

    row = lax.broadcasted_iota(jnp.int32, (CHUNK, 128), 0)
    lane = lax.broadcasted_iota(jnp.int32, (CHUNK, 128), 1)
    left = lane < CHUNK
    left1 = left[0:1, :]
    lcol = jnp.where(left, lane, lane - CHUNK)
    r64 = lax.broadcasted_iota(jnp.int32, (CHUNK, CHUNK), 0)
    c64 = lax.broadcasted_iota(jnp.int32, (CHUNK, CHUNK), 1)
    r128 = lax.broadcasted_iota(jnp.int32, (128, 128), 0)
    c128 = lax.broadcasted_iota(jnp.int32, (128, 128), 1)
    same_half = (r128 < CHUNK) == (c128 < CHUNK)
    neg = -jnp.inf
    ins = ((xpf_ref, gpf_ref, rpf_ref, hpf_ref), (xpb_ref, gpb_ref, rpb_ref, hpb_ref),
           (xsf_ref, gsf_ref, rsf_ref, hsf_ref), (xsb_ref, gsb_ref, rsb_ref, hsb_ref))

    gate = []
    for u in range(4):
        d = u % 2
        gc_ref, gr_ref = ins[u][1], ins[u][2]
        if d == 0:
            incl_p, tri_c, tri_r = lcol <= row, c64 <= r64, jnp.logical_and(same_half, r128 <= c128)
        else:
            incl_p, tri_c, tri_r = lcol >= row, c64 >= r64, jnp.logical_and(same_half, r128 >= c128)
        gc = gc_ref[...] + brow_ref[...]
        gr = gr_ref[0] + b2_ref[...]
        cs_col = _prefix_cols(tri_c.astype(BF16), _log_sigmoid(gc))
        cs_row = _prefix_rows(_log_sigmoid(gr), tri_r.astype(BF16))
        gate.append((incl_p, gc, gr, cs_col, cs_row))

    st = []
    for u in range(4):
        d = u % 2
        c_s, n_s, m_s = states[u // 2]
        for hp in range(HA // 2):
            x_ref = ins[u][0]
            incl_p, gc, gr, cs_col, cs_row = gate[u]
            end = CHUNK - 1 if d == 0 else 0
            heads = []
            for h in (2 * hp, 2 * hp + 1):
                c = d * HA + h
                ci = d * 8 + h
                cf = d * 8 + 4 + h
                b_col = cs_col[:, cf:cf + 1]
                heads.append(dict(c=c, h=h, q=x_ref[:, h * DH:(h + 1) * DH],
                                  k=x_ref[:, 512 + h * DH:512 + (h + 1) * DH] * (DH ** -0.5),
                                  v=x_ref[:, 1024 + h * DH:1024 + (h + 1) * DH],
                                  i_col=gc[:, ci:ci + 1], b_col=b_col, b_last=b_col[end:end + 1, :],
                                  m=m_s[c:c + 1, 0:1]))
            ha, hb_ = heads
            i_row = gr[d * 4 + hp:d * 4 + hp + 1, :]
            b_row = cs_row[d * 4 + 2 + hp:d * 4 + 3 + hp, :]
            b_col_p = jnp.where(left, ha["b_col"], hb_["b_col"])
            m_p = jnp.where(left1, m_s[ha["c"]:ha["c"] + 1, :], m_s[hb_["c"]:hb_["c"] + 1, :])
            dmat = jnp.where(incl_p, b_col_p - b_row + i_row, neg)
            inter = b_col_p + m_p
            dmax = jnp.where(left, jnp.max(jnp.where(left, dmat, neg), axis=1, keepdims=True),
                             jnp.max(jnp.where(left, neg, dmat), axis=1, keepdims=True))
            m_t = jnp.maximum(inter, dmax)
            q_cat = _bf(jnp.concatenate([ha["q"], hb_["q"]], axis=1))
            k_bd = _bf(_block_diag2(ha["k"], hb_["k"]))
            sc = _mm_nt(q_cat, k_bd) * jnp.exp(dmat - m_t)
            st.append(dict(u=u, heads=heads, sc=sc, w_inter=jnp.exp(inter - m_t), m_t=m_t))

    for p in st:
        ha, hb_ = p["heads"]
        h_ref = ins[p["u"]][3]
        c_s, n_s, m_s = states[p["u"] // 2]
        sc = p["sc"]
        sv = _mm(_bf(sc), _bf(_block_diag2(ha["v"], hb_["v"])))
        dens = (jnp.sum(jnp.where(left, sc, 0.0), axis=1, keepdims=True),
                jnp.sum(jnp.where(left, 0.0, sc), axis=1, keepdims=True))
        for idx, hd in enumerate((ha, hb_)):
            c, h, q, k = hd["c"], hd["h"], hd["q"], hd["k"]
            w_inter = p["w_inter"][:, idx * CHUNK:idx * CHUNK + 1]
            m_t = p["m_t"][:, idx * CHUNK:idx * CHUNK + 1]
            c_mat = c_s[c]
            nrm = n_s[c:c + 1, :]
            num = sv[:, idx * DH:(idx + 1) * DH] + w_inter * _mm(_bf(q), _bf(c_mat))
            den = dens[idx] + w_inter * jnp.sum(q * nrm, axis=1, keepdims=True)
            h_ref[:, h * DH:(h + 1) * DH] = num / jnp.maximum(jnp.abs(den), jnp.exp(-m_t))
            g_end = hd["b_last"] - hd["b_col"] + hd["i_col"]
            m_new = jnp.maximum(hd["b_last"] + hd["m"], jnp.max(g_end, axis=0, keepdims=True))
            kw = k * jnp.exp(g_end - m_new)
            decay = jnp.exp(hd["b_last"] + hd["m"] - m_new)
            c_s[c] = decay * c_mat + _mm_tn(_bf(kw), _bf(hd["v"]))
            n_s[c:c + 1, :] = decay * nrm + jnp.sum(kw, axis=0, keepdims=True)
            m_s[c:c + 1, :] = jnp.broadcast_to(m_new, (1, 128))

    @pl.when(jp == P_CHUNKS - 1)
    def _():
        for d in range(2):
            for h in range(HA):
                c_out[0, d, h] = cp_s[d * HA + h]
            n_out[0, d] = np_s[d * HA:(d + 1) * HA, :]
        m_out[0] = mp_s[...]


def _mlstm_scan(proj, gates_t2, bias_row, bias2, c0, n0, m0, e):
    const2 = lambda s: (0, 0)
    in_specs = (_scan_specs()
                + [pl.BlockSpec((1, 128), const2), pl.BlockSpec((16, 128), const2),
                   pl.BlockSpec((1, 1, 2, HA, DH, DH), lambda s: (s // S_CHUNKS, e, 0, 0, 0, 0)),
                   pl.BlockSpec((1, 1, 2, HA, DH), lambda s: (s // S_CHUNKS, e, 0, 0, 0)),
                   pl.BlockSpec((1, 1, 2 * HA, 128), lambda s: (s // S_CHUNKS, e, 0, 0))])
    half = jax.ShapeDtypeStruct((NP_ROWS, HA * DH), F32)
    out_shape = (half, half, half, half,
                 jax.ShapeDtypeStruct((BATCH, 2, HA, DH, DH), F32),
                 jax.ShapeDtypeStruct((BATCH, 2, HA, DH), F32),
                 jax.ShapeDtypeStruct((BATCH, 2 * HA, 128), F32))
    out_specs = _scan_out_specs() + [
        pl.BlockSpec((1, 2, HA, DH, DH), lambda s: (s // P_CHUNKS, 0, 0, 0, 0)),
        pl.BlockSpec((1, 2, HA, DH), lambda s: (s // P_CHUNKS, 0, 0, 0)),
        pl.BlockSpec((1, 2 * HA, 128), lambda s: (s // P_CHUNKS, 0, 0))]
    state = [pltpu.VMEM((2 * HA, DH, DH), F32), pltpu.VMEM((2 * HA, DH), F32), pltpu.VMEM((2 * HA, 128), F32)]
    return pl.pallas_call(
        _mlstm_kernel,
        out_shape=out_shape,
        grid=(SCAN_STEPS,),
        in_specs=in_specs,
        out_specs=out_specs,
        scratch_shapes=state + state,
        compiler_params=_params(1),
        name="mlstm_scan",
    )(proj, proj, proj, proj, proj, proj, proj, proj, gates_t2, gates_t2, gates_t2, gates_t2,
      bias_row, bias2, c0, n0, m0)


def _split2(x):
    hi = _bf(x)
    return hi, _bf(x - hi.astype(F32))


def _split3(x):
    h1 = _bf(x)
    r1 = x - h1.astype(F32)
    h2 = _bf(r1)
    return h1, h2, _bf(r1 - h2.astype(F32))


def _lhs3(x):
    hi, lo = _split2(x)
    return jnp.concatenate([hi, lo, hi], axis=1)


def _rhs3(x):
    hi, lo = _split2(x)
    return jnp.concatenate([hi, hi, lo], axis=0)


def _prefix_cols(tri_bf, x):
    n = x.shape[1]
    r = _mm(tri_bf, jnp.concatenate(_split3(x), axis=1))
    return r[:, :n] + r[:, n:2 * n] + r[:, 2 * n:]


def _prefix_rows(x, tri_bf):
    m = x.shape[0]
    r = _mm(jnp.concatenate(_split3(x), axis=0), tri_bf)
    return r[:m] + r[m:2 * m] + r[2 * m:]


LEVELS = tuple(range(6))


def _block_diag2(a, b):
    z = jnp.zeros_like(a)
    return jnp.concatenate([jnp.concatenate([a, z], axis=1), jnp.concatenate([z, b], axis=1)], axis=0)


def _delta_kernel(xpf_ref, xpb_ref, xsf_ref, xsb_ref, gpf_ref, gpb_ref, gsf_ref, gsb_ref,
                  rpf_ref, rpb_ref, rsf_ref, rsb_ref,
                  dtrow_ref, dt2_ref, narow_ref, na2_ref, s0_ref,
                  opf_ref, opb_ref, osf_ref, osb_ref, s_out, sp_s, ss_s):
    s = pl.program_id(0)
    jp = s % P_CHUNKS
    js = s % S_CHUNKS

    @pl.when(jp == 0)
    def _():
        sp_s[...] = jnp.zeros_like(sp_s)

    @pl.when(js == 0)
    def _():
        for d in range(2):
            for h in range(HB):
                ss_s[d * HB + h] = s0_ref[0, 0, d, h]

    states = (sp_s, ss_s)

    row = lax.broadcasted_iota(jnp.int32, (CHUNK, 128), 0)
    lane = lax.broadcasted_iota(jnp.int32, (CHUNK, 128), 1)
    left = lane < CHUNK
    lcol = jnp.where(left, lane, lane - CHUNK)
    eye_p = (lcol == row).astype(F32)
    r64 = lax.broadcasted_iota(jnp.int32, (CHUNK, CHUNK), 0)
    c64 = lax.broadcasted_iota(jnp.int32, (CHUNK, CHUNK), 1)
    r128 = lax.broadcasted_iota(jnp.int32, (128, 128), 0)
    c128 = lax.broadcasted_iota(jnp.int32, (128, 128), 1)
    same_half = (r128 < CHUNK) == (c128 < CHUNK)
    ins = ((xpf_ref, gpf_ref, rpf_ref, opf_ref), (xpb_ref, gpb_ref, rpb_ref, opb_ref),
           (xsf_ref, gsf_ref, rsf_ref, osf_ref), (xsb_ref, gsb_ref, rsb_ref, osb_ref))

    gate = []
    for u in range(4):
        d = u % 2
        gc_ref, gr_ref = ins[u][1], ins[u][2]
        if d == 0:
            incl_p, strict_p = lcol <= row, lcol < row
            tri_c, tri_r = c64 <= r64, jnp.logical_and(same_half, r128 <= c128)
        else:
            incl_p, strict_p = lcol >= row, lcol > row
            tri_c, tri_r = c64 >= r64, jnp.logical_and(same_half, r128 >= c128)
        xc = gc_ref[...]
        xr = gr_ref[0]
        la_c = narow_ref[...] * _softplus(xc + dtrow_ref[...])
        la_r = na2_ref[...] * _softplus(xr + dt2_ref[...])
        g_c = _prefix_cols(tri_c.astype(BF16), la_c)
        g_r = _prefix_rows(la_r, tri_r.astype(BF16))
        gate.append((incl_p, strict_p, _sigmoid(xc), g_c, g_r))

    def bd_rhs(x):
        return _rhs3(jnp.concatenate([jnp.where(left, x, 0.0), jnp.where(left, 0.0, x)], axis=0))

    off = ([], [])
    for lv in LEVELS:
        same = jnp.right_shift(row, lv + 1) == jnp.right_shift(lcol, lv + 1)
        r_hi = jnp.bitwise_and(jnp.right_shift(row, lv), 1) == 1
        c_hi = jnp.bitwise_and(jnp.right_shift(lcol, lv), 1) == 1
        off[0].append(same & r_hi & jnp.logical_not(c_hi))
        off[1].append(same & jnp.logical_not(r_hi) & c_hi)

    pairs = [(u, hp) for u in range(4) for hp in range(HB // 2)]
    st = []
    for u, hp in pairs:
        d = u % 2
        x_ref = ins[u][0]
        incl_p, strict_p, beta_c, g_c, g_r = gate[u]
        end = CHUNK - 1 if d == 0 else 0
        heads = []
        for h in (2 * hp, 2 * hp + 1):
            ib = 16 + d * HB + h
            ia = 24 + d * HB + h
            gcol = g_c[:, ia:ia + 1]
            bc = beta_c[:, ib:ib + 1]
            q = x_ref[:, h * DH:(h + 1) * DH]
            k = x_ref[:, 512 + h * DH:512 + (h + 1) * DH]
            v = x_ref[:, 1024 + h * DH:1024 + (h + 1) * DH]
            heads.append(dict(h=h, gcol=gcol, bc=bc, q=q, k=k, v=v, kb=k * bc, eg=jnp.exp(gcol),
                              g_last=gcol[end:end + 1, :]))
        ha, hb_ = heads
        r = 12 + d * 2 + hp
        gcol_p = jnp.where(left, ha["gcol"], hb_["gcol"])
        decay = jnp.exp(jnp.where(incl_p, gcol_p - g_r[r:r + 1, :], -jnp.inf))
        k_bd = _bf(_block_diag2(ha["k"], hb_["k"]))
        kb_cat = _bf(jnp.concatenate([ha["kb"], hb_["kb"]], axis=1))
        q_cat = _bf(jnp.concatenate([ha["q"], hb_["q"]], axis=1))
        a_mat = jnp.where(strict_p, _mm_nt(kb_cat, k_bd) * decay, 0.0)
        qk = _mm_nt(q_cat, k_bd) * decay
        st.append(dict(u=u, d=d, heads=heads, t=eye_p - jnp.where(off[d][0], a_mat, 0.0), qk=qk,
                       am=[bd_rhs(jnp.where(m, a_mat, 0.0)) for m in off[d][1:]]))

    for li in range(len(LEVELS) - 1):
        for p in st:
            p["w"] = _mm(_lhs3(p["t"]), p["am"][li])
        for p in st:
            p["t"] = p["t"] - _mm(_lhs3(p["w"]), bd_rhs(p["t"]))

    for p in st:
        ha, hb_ = p["heads"]
        o_ref = ins[p["u"]][3]
        s_s = states[p["u"] // 2]
        rhs_a = jnp.concatenate([ha["v"] * ha["bc"], ha["kb"] * ha["eg"]], axis=1)
        rhs_b = jnp.concatenate([hb_["v"] * hb_["bc"], hb_["kb"] * hb_["eg"]], axis=1)
        sol = _mm(_lhs3(p["t"]), _rhs3(_block_diag2(rhs_a, rhs_b)))
        vn = []
        for idx, hd in enumerate((ha, hb_)):
            c = p["d"] * HB + hd["h"]
            s_mat = s_s[c]
            sbf = _bf(s_mat)
            so = sol[:, idx * 2 * DH:(idx + 1) * 2 * DH]
            v_new = so[:, :DH] - _mm(_bf(so[:, DH:]), sbf)
            vn.append(v_new)
            hd["o1"] = _mm(_bf(hd["q"] * hd["eg"]), sbf)
            s_s[c] = (jnp.exp(hd["g_last"]) * s_mat
                      + _mm_tn(_bf(hd["k"] * jnp.exp(hd["g_last"] - hd["gcol"])), _bf(v_new)))
        o2 = _mm(_bf(p["qk"]), _bf(_block_diag2(vn[0], vn[1])))
        for idx, hd in enumerate((ha, hb_)):
            o_ref[:, hd["h"] * DH:(hd["h"] + 1) * DH] = hd["o1"] + o2[:, idx * DH:(idx + 1) * DH]

    @pl.when(jp == P_CHUNKS - 1)
    def _():
        for d in range(2):
            for h in range(HB):
                s_out[0, d, h] = sp_s[d * HB + h]


def _delta_scan(qkv, proj, gates_t2, dt_row, dt2, na_row, na2, s0, e):
    const2 = lambda s: (0, 0)
    in_specs = (_scan_specs()
                + [pl.BlockSpec((1, 128), const2), pl.BlockSpec((16, 128), const2),
                   pl.BlockSpec((1, 128), const2), pl.BlockSpec((16, 128), const2),
                   pl.BlockSpec((1, 1, 2, HB, DH, DH), lambda s: (s // S_CHUNKS, e, 0, 0, 0, 0))])
    half = jax.ShapeDtypeStruct((NP_ROWS, HB * DH), F32)
    out_shape = (half, half, half, half, jax.ShapeDtypeStruct((BATCH, 2, HB, DH, DH), F32))
    out_specs = _scan_out_specs() + [pl.BlockSpec((1, 2, HB, DH, DH), lambda s: (s // P_CHUNKS, 0, 0, 0, 0))]
    return pl.pallas_call(
        _delta_kernel,
        out_shape=out_shape,
        grid=(SCAN_STEPS,),
        in_specs=in_specs,
        out_specs=out_specs,
        scratch_shapes=[pltpu.VMEM((2 * HB, DH, DH), F32), pltpu.VMEM((2 * HB, DH, DH), F32)],
        compiler_params=_params(1),
        name="delta_scan",
    )(qkv, qkv, qkv, qkv, proj, proj, proj, proj, gates_t2, gates_t2, gates_t2, gates_t2,
      dt_row, dt2, na_row, na2, s0)


def _residual_ln(x, gate, y, g, b):
    r = DN_ALPHA * x + gate * y
    mu = jnp.mean(r, axis=-1, keepdims=True)
    var = jnp.mean(jnp.square(r - mu), axis=-1, keepdims=True)
    return (r - mu) * lax.rsqrt(var + LN_EPS) * g + b


def _even_out_kernel(hpf_ref, hpb_ref, hsf_ref, hsb_ref, opf_ref, opb_ref, osf_ref, osb_ref,
                     oa_ref, zb_ref, mg_ref, dg_ref, w_ref, x_ref, gate_ref, g_ref, b_ref, o_ref, *, tm):
    is_p = pl.program_id(0) < NP_ROWS // tm
    hf_ref, hb_ref, of_ref, ob_ref = (
        lambda sl, p=p, q=q: jnp.where(is_p, p[:, sl], q[:, sl])
        for p, q in ((hpf_ref, hsf_ref), (hpb_ref, hsb_ref), (opf_ref, osf_ref), (opb_ref, osb_ref)))
    parts = []
    for h in range(HA):
        sl = slice(h * DH, (h + 1) * DH)
        hh = hf_ref(sl) + hb_ref(sl)
        mu = jnp.mean(hh, axis=-1, keepdims=True)
        var = jnp.mean(jnp.square(hh - mu), axis=-1, keepdims=True)
        parts.append(_sigmoid(oa_ref[:, sl]) * ((hh - mu) * lax.rsqrt(var + LN_EPS) * mg_ref[:, sl]))
    for h in range(HB):
        sl = slice(h * DH, (h + 1) * DH)
        oo = of_ref(sl) + ob_ref(sl)
        z = zb_ref[:, sl]
        nrm = oo * lax.rsqrt(jnp.mean(jnp.square(oo), axis=-1, keepdims=True) + LN_EPS) * dg_ref[:, sl]
        parts.append(nrm * (z * _sigmoid(z)))
    a = jnp.concatenate(parts, axis=1)
    y = _mm(_bf(a), w_ref[...])
    o_ref[...] = _residual_ln(x_ref[...], gate_ref[0], y, g_ref[...], b_ref[...])


def _even_out(h_scan, o_scan, proj, mg, dg, w, x, mods, layer, ln_g, ln_b):
    tm = 256
    npt = NP_ROWS // tm
    row512 = lambda i: (i, 0)
    const2 = lambda i: (0, 0)
    p_spec = pl.BlockSpec((tm, 512), lambda i: (jnp.minimum(i, npt - 1), 0))
    s_spec = pl.BlockSpec((tm, 512), lambda i: (jnp.maximum(i - npt, 0), 0))
    return pl.pallas_call(
        functools.partial(_even_out_kernel, tm=tm),
        out_shape=jax.ShapeDtypeStruct((T, D), F32),
        grid=(T // tm,),
        in_specs=[p_spec, p_spec, s_spec, s_spec, p_spec, p_spec, s_spec, s_spec,
                  pl.BlockSpec((tm, 512), lambda i: (i, 3)),
                  pl.BlockSpec((tm, 512), lambda i: (i, 7)),
                  pl.BlockSpec((1, 512), const2), pl.BlockSpec((1, 512), const2),
                  pl.BlockSpec((D, D), const2),
                  pl.BlockSpec((tm, D), row512),
                  _mod_spec(layer, 2, tm),
                  pl.BlockSpec((1, D), const2), pl.BlockSpec((1, D), const2)],
        out_specs=pl.BlockSpec((tm, D), row512),
        compiler_params=_params(1),
        name="even_out_ln",
    )(*h_scan, *o_scan, proj, proj, mg, dg, w, x, mods, ln_g, ln_b)


def _odd_out_kernel(ap_ref, as_ref, w_ref, x_ref, gate_ref, g_ref, b_ref, o_ref, *, tm):
    a = jnp.where(pl.program_id(0) < NP_ROWS // tm, ap_ref[...], as_ref[...])
    y = _mm(_bf(a), w_ref[...])
    o_ref[...] = _residual_ln(x_ref[...], gate_ref[0], y, g_ref[...], b_ref[...])


def _odd_out(a_prompt, a_latent, w, x, mods, layer, ln_g, ln_b):
    tm = 256
    npt = NP_ROWS // tm
    row = lambda i: (i, 0)
    const2 = lambda i: (0, 0)
    return pl.pallas_call(
        functools.partial(_odd_out_kernel, tm=tm),
        out_shape=jax.ShapeDtypeStruct((T, D), F32),
        grid=(T // tm,),
        in_specs=[pl.BlockSpec((tm, D), lambda i: (jnp.minimum(i, npt - 1), 0)),
                  pl.BlockSpec((tm, D), lambda i: (jnp.maximum(i - npt, 0), 0)),
                  pl.BlockSpec((D, D), const2), pl.BlockSpec((tm, D), row),
                  _mod_spec(layer, 2, tm), pl.BlockSpec((1, D), const2), pl.BlockSpec((1, D), const2)],
        out_specs=pl.BlockSpec((tm, D), row),
        compiler_params=_params(1),
        name="odd_out_ln",
    )(a_prompt, a_latent, w, x, mods, ln_g, ln_b)


def _attn_ctx_kernel(qkv_ref, sink_ref, o_ref):
    ones = jnp.ones((SEQ, HD), BF16)
    for kv in range(KVH):
        k = _bf(qkv_ref[:, HC * HD + kv * HD:HC * HD + (kv + 1) * HD])
        v = _bf(qkv_ref[:, (HC + KVH) * HD + kv * HD:(HC + KVH) * HD + (kv + 1) * HD])
        v1 = jnp.concatenate([v, ones], axis=1)
        heads = range(kv * (HC // KVH), (kv + 1) * (HC // KVH))
        sts = [_mm_nt(k, _bf(qkv_ref[:, h * HD:(h + 1) * HD])) for h in heads]
        ms = [jnp.maximum(jnp.max(st, axis=0, keepdims=True), sink_ref[h]) for st, h in zip(sts, heads)]
        ovs = [_mm_tn(v1, _bf(jnp.exp(st - m))) for st, m in zip(sts, ms)]
        outs = [ov[:HD] / (ov[HD:HD + 1] + jnp.exp(sink_ref[h] - m)) for ov, m, h in zip(ovs, ms, heads)]
        for pr in range(HC // KVH // 2):
            lane0 = (kv * (HC // KVH) + 2 * pr) * HD
            o_ref[:, lane0:lane0 + 2 * HD] = jnp.concatenate(outs[2 * pr:2 * pr + 2], axis=0).T


def _attn_context(qkv, sink):
    return pl.pallas_call(
        _attn_ctx_kernel,
        out_shape=jax.ShapeDtypeStruct((NP_ROWS, HC * HD), F32),
        grid=(BATCH,),
        in_specs=[pl.BlockSpec((SEQ, QKV_W), lambda b: (b, 0)),
                  pl.BlockSpec(memory_space=pltpu.SMEM)],
        out_specs=pl.BlockSpec((SEQ, HC * HD), lambda b: (b, 0)),
        compiler_params=_params(1),
        name="attn_context",
    )(qkv, sink)


def _attn_lat_kernel(q_ref, kp_ref, kc_ref, kn_ref, vp_ref, vc_ref, vn_ref, ck_ref, cv_ref,
                     sink_ref, o_ref, bias_s):
    j = pl.program_id(1)
    nb = DEC_SEQ // QBLOCK
    cc = lax.broadcasted_iota(jnp.int32, (3 * QBLOCK, QBLOCK), 0)
    r = lax.broadcasted_iota(jnp.int32, (3 * QBLOCK, QBLOCK), 1)
    lo = jnp.where(j >= 1, 0, QBLOCK)
    hi = jnp.where(j <= nb - 2, 3 * QBLOCK, 2 * QBLOCK)
    ok = (jnp.abs(QBLOCK + r - cc) <= WINDOW) & (cc >= lo) & (cc < hi)
    bias_s[...] = jnp.where(ok, 0.0, -jnp.inf)
    for kv in range(KVH):
        ks = slice(kv * HD, (kv + 1) * HD)
        k_all = _bf(jnp.concatenate([ck_ref[0, 0, kv], kp_ref[:, ks], kc_ref[:, ks], kn_ref[:, ks]], axis=0))
        v_all = _bf(jnp.concatenate([cv_ref[0, 0, kv], vp_ref[:, ks], vc_ref[:, ks], vn_ref[:, ks]], axis=0))
        v1 = jnp.concatenate([v_all, jnp.ones_like(v_all)], axis=1)
        outs = []
        for h in range(kv * (HC // KVH), (kv + 1) * (HC // KVH)):
            sink = sink_ref[h]
            st = _mm_nt(k_all, _bf(q_ref[:, h * HD:(h + 1) * HD]))
            s_ctx = st[:PAST_LEN]
            s_loc = st[PAST_LEN:] + bias_s[...]
            m = jnp.maximum(jnp.maximum(jnp.max(s_ctx, axis=0, keepdims=True),
                                        jnp.max(s_loc, axis=0, keepdims=True)), sink)
            p = _bf(jnp.concatenate([jnp.exp(s_ctx - m), jnp.exp(s_loc - m)], axis=0))
            ov = _mm_tn(v1, p)
            outs.append(ov[:HD] / (ov[HD:HD + 1] + jnp.exp(sink - m)))
        for pr in range(HC // KVH // 2):
            lane0 = (kv * (HC // KVH) + 2 * pr) * HD
            o_ref[:, lane0:lane0 + 2 * HD] = jnp.concatenate(outs[2 * pr:2 * pr + 2], axis=0).T


def _attn_latent(qkv, cache_k, cache_v, sink, o):
    nb = DEC_SEQ // QBLOCK
    base = NP_ROWS // QBLOCK
    blk = lambda b, j: base + b * nb + j
    prev = lambda b, j: base + b * nb + jnp.maximum(j - 1, 0)
    nxt = lambda b, j: base + b * nb + jnp.minimum(j + 1, nb - 1)
    kcol, vcol = HC * HD // 256, HC * HD // 256 + 1
    cache_spec = pl.BlockSpec((1, 1, KVH, PAST_LEN, HD), lambda b, j: (b, o, 0, 0, 0))
    return pl.pallas_call(
        _attn_lat_kernel,
        out_shape=jax.ShapeDtypeStruct((NS_ROWS, HC * HD), F32),
        grid=(DEC_BATCH, nb),
        in_specs=[pl.BlockSpec((QBLOCK, HC * HD), lambda b, j: (blk(b, j), 0)),
                  pl.BlockSpec((QBLOCK, 256), lambda b, j: (prev(b, j), kcol)),
                  pl.BlockSpec((QBLOCK, 256), lambda b, j: (blk(b, j), kcol)),
                  pl.BlockSpec((QBLOCK, 256), lambda b, j: (nxt(b, j), kcol)),
                  pl.BlockSpec((QBLOCK, 256), lambda b, j: (prev(b, j), vcol)),
                  pl.BlockSpec((QBLOCK, 256), lambda b, j: (blk(b, j), vcol)),
                  pl.BlockSpec((QBLOCK, 256), lambda b, j: (nxt(b, j), vcol)),
                  cache_spec, cache_spec,
                  pl.BlockSpec(memory_space=pltpu.SMEM)],
        out_specs=pl.BlockSpec((QBLOCK, HC * HD), lambda b, j: (b * nb + j, 0)),
        scratch_shapes=[pltpu.VMEM((3 * QBLOCK, QBLOCK), F32)],
        compiler_params=_params(2),
        name="attn_latent",
    )(qkv, qkv, qkv, qkv, qkv, qkv, qkv, cache_k, cache_v, sink)


def _route(lg):
    lane = lax.broadcasted_iota(jnp.int32, lg.shape, 1)
    neg = -jnp.inf
    big = 1 << 20
    is_grp = jnp.logical_and(lane >= N_EXPERTS, lane < N_EXPERTS + N_GROUPS)
    mg = jnp.max(jnp.where(is_grp, lg, neg), axis=1, keepdims=True)
    g_lane = jnp.min(jnp.where(jnp.logical_and(is_grp, lg == mg), lane, big), axis=1, keepdims=True)
    g_w = 1.0 / jnp.sum(jnp.where(is_grp, jnp.exp(lg - mg), 0.0), axis=1, keepdims=True)
    g_idx = g_lane - N_EXPERTS
    in_grp = jnp.logical_and(lane >= g_idx * EPG, lane < (g_idx + 1) * EPG)
    v1 = jnp.max(jnp.where(in_grp, lg, neg), axis=1, keepdims=True)
    i1 = jnp.min(jnp.where(jnp.logical_and(in_grp, lg == v1), lane, big), axis=1, keepdims=True)
    rest = jnp.logical_and(in_grp, lane != i1)
    v2 = jnp.max(jnp.where(rest, lg, neg), axis=1, keepdims=True)
    i2 = jnp.min(jnp.where(jnp.logical_and(rest, lg == v2), lane, big), axis=1, keepdims=True)
    e2 = jnp.exp(v2 - v1)
    p1 = 1.0 / (1.0 + e2)
    p2 = e2 / (1.0 + e2)
    return jnp.where(lane == 0, i1.astype(F32),
                     jnp.where(lane == 1, i2.astype(F32),
                               jnp.where(lane == 2, p1 * g_w, jnp.where(lane == 3, p2 * g_w, 0.0))))


def _moe_dense_kernel(x_ref, sh_ref, sc_ref, wr_ref, br_ref, wg_ref, wu_ref, wd_ref, gate_ref, g_ref, b_ref, o_ref,
                      xm_s, meta_s, acc_s):
    grp = pl.program_id(1)

    @pl.when(grp == 0)
    def _():
        xm = x_ref[...] * (1.0 + sc_ref[0]) + sh_ref[0]
        meta_s[...] = _route(_mm(_lhs3(xm), _rhs3(wr_ref[...])) + br_ref[...])
        xm_s[...] = _bf(xm)
        acc_s[...] = jnp.zeros_like(acc_s)

    xm = xm_s[...]
    meta = meta_s[...]
    i1 = meta[:, 0:1].astype(jnp.int32)
    i2 = meta[:, 1:2].astype(jnp.int32)
    w1 = meta[:, 2:3]
    w2 = meta[:, 3:4]
    hid = []
    for e in range(EPG):
        eid = grp * EPG + e
        gate = jnp.where(i1 == eid, w1, 0.0) + jnp.where(i2 == eid, w2, 0.0)
        a = _mm(xm, wg_ref[0, e])
        u = _mm(xm, wu_ref[0, e])
        hid.append(_bf((a * _sigmoid(a)) * u * gate))
    acc_s[...] += _mm(jnp.concatenate(hid, axis=1), wd_ref[0])

    @pl.when(grp == N_GROUPS - 1)
    def _():
        o_ref[...] = _residual_ln(x_ref[...], gate_ref[0], acc_s[...], g_ref[...], b_ref[...])


def _moe_dense(x, mods, layer, w_r, b_r, wg, wu, wd, ln_g, ln_b):
    tm = 512
    row = lambda i, g: (i, 0)
    const2 = lambda i, g: (0, 0)
    return pl.pallas_call(
        _moe_dense_kernel,
        out_shape=jax.ShapeDtypeStruct((T, D), F32),
        grid=(T // tm, N_GROUPS),
        in_specs=[pl.BlockSpec((tm, D), row), _mod_spec(layer, 3, tm), _mod_spec(layer, 4, tm),
                  pl.BlockSpec((D, 128), const2), pl.BlockSpec((1, 128), const2),
                  pl.BlockSpec((1, EPG, D, EXPERT_FF), lambda i, g: (g, 0, 0, 0)),
                  pl.BlockSpec((1, EPG, D, EXPERT_FF), lambda i, g: (g, 0, 0, 0)),
                  pl.BlockSpec((1, EPG * EXPERT_FF, D), lambda i, g: (g, 0, 0)),
                  _mod_spec(layer, 5, tm), pl.BlockSpec((1, D), const2), pl.BlockSpec((1, D), const2)],
        out_specs=pl.BlockSpec((tm, D), row),
        scratch_shapes=[pltpu.VMEM((tm, D), BF16), pltpu.VMEM((tm, 128), F32), pltpu.VMEM((tm, D), F32)],
        compiler_params=_params(2),
        name="moe_dense",
    )(x, mods, mods, w_r, b_r, wg, wu, wd, mods, ln_g, ln_b)


def _permute_even_w(w):
    a_end = 4 * HA * DH
    g_end = a_end + 4 * HA
    c_end = g_end + 3 * HB * DH
    z_end = c_end + HB * DH
    small = jnp.concatenate([w[:, a_end:g_end], w[:, z_end:]], axis=1)
    pad = jnp.zeros((w.shape[0], 128 - small.shape[1]), w.dtype)
    return jnp.concatenate([w[:, :a_end], w[:, g_end:c_end], w[:, c_end:z_end], small, pad], axis=1)


def _lane_row(vals, offset):
    return jnp.zeros((1, 128), F32).at[0, offset:offset + vals.shape[0]].set(vals.astype(F32))


def _pair_rows(vals, offset):
    v32 = jnp.zeros((32,), F32).at[offset:offset + vals.shape[0]].set(vals.astype(F32))
    return jnp.repeat(v32.reshape(16, 2), CHUNK, axis=1)


def kernel(x_prompt, x_sample, c, c_ctx, state_mlstm_c, state_mlstm_n, state_mlstm_m, state_delta, cache_k, cache_v, w_mod, b_mod, ln_g, ln_b, w_in_even, mlstm_gate_b, mlstm_norm_g, delta_conv_w, delta_a_log, delta_dt_bias, delta_norm_g, w_out_even, w_qkv_odd, attn_sink, w_out_odd, w_grp, b_grp, w_erouter, b_erouter, w_gate, w_up, w_down):
    x = jnp.concatenate([x_prompt.reshape(NP_ROWS, D), x_sample.reshape(NS_ROWS, D)], axis=0)
    cvecs = jnp.concatenate([c_ctx[None, :], c, jnp.zeros((N_MOD_ROWS - 1 - DEC_BATCH, D), F32)], axis=0)
    mods = _modulation(cvecs, w_mod, b_mod)
    tables = _rope_tables()
    m0_all = jnp.broadcast_to(state_mlstm_m.reshape(DEC_BATCH, N_EVEN, 2 * HA, 1), (DEC_BATCH, N_EVEN, 2 * HA, 128))

    out_mc, out_mn, out_mm, out_ds, out_k, out_v = [], [], [], [], [], []
    for l in range(DEPTH):
        if l % 2 == 0:
            e = l // 2
            proj = _even_proj(x, mods, l, _bf(_permute_even_w(w_in_even[e])))
            gates_t2 = (proj[:, EVEN_W - 128:EVEN_W - 96].reshape(T // CHUNK, CHUNK, 32).transpose(0, 2, 1)
                        .reshape(T // CHUNK, 16, 128))
            gb = mlstm_gate_b[e].reshape(-1)
            *h_scan, mc, mn, mm = _mlstm_scan(proj, gates_t2, _lane_row(gb, 0), _pair_rows(gb, 0),
                                              state_mlstm_c, state_mlstm_n, m0_all, e)
            qkv = _delta_prep(proj, delta_conv_w[e])
            dtb = delta_dt_bias[e].reshape(-1)
            nea = -jnp.exp(delta_a_log[e].astype(F32)).reshape(-1)
            *o_scan, ds = _delta_scan(qkv, proj, gates_t2, _lane_row(dtb, 24), _pair_rows(dtb, 24),
                                      _lane_row(nea, 24), _pair_rows(nea, 24), state_delta, e)
            x = _even_out(h_scan, o_scan, proj, mlstm_norm_g[e][None, :], delta_norm_g[e][None, :],
                          _bf(w_out_even[e]), x, mods, l, ln_g[l, 0][None, :], ln_b[l, 0][None, :])
            out_mc.append(mc)
            out_mn.append(mn)
            out_mm.append(mm[:, :, 0].reshape(BATCH, 2, HA))
            out_ds.append(ds)
        else:
            o = l // 2
            qkv = _odd_proj(x, mods, l, _bf(w_qkv_odd[o]), tables)
            a_p = _attn_context(qkv, attn_sink[o])
            a_s = _attn_latent(qkv, cache_k, cache_v, attn_sink[o], o)
            x = _odd_out(a_p, a_s, _bf(w_out_odd[o]), x, mods, l, ln_g[l, 0][None, :], ln_b[l, 0][None, :])
            kp = qkv[:NP_ROWS, HC * HD:(HC + KVH) * HD].reshape(BATCH, SEQ, KVH, HD).transpose(0, 2, 1, 3)
            vp = qkv[:NP_ROWS, (HC + KVH) * HD:].reshape(BATCH, SEQ, KVH, HD).transpose(0, 2, 1, 3)
            out_k.append(kp)
            out_v.append(vp)
        w_r = jnp.concatenate([w_erouter[l].transpose(1, 0, 2).reshape(D, N_EXPERTS), w_grp[l],
                               jnp.zeros((D, 128 - N_EXPERTS - N_GROUPS), F32)], axis=1)
        b_r = jnp.concatenate([b_erouter[l].reshape(-1), b_grp[l],
                               jnp.zeros((128 - N_EXPERTS - N_GROUPS,), F32)])[None, :]
        x = _moe_dense(x, mods, l, w_r, b_r,
                       _bf(w_gate[l]).reshape(N_GROUPS, EPG, D, EXPERT_FF),
                       _bf(w_up[l]).reshape(N_GROUPS, EPG, D, EXPERT_FF),
                       _bf(w_down[l]).reshape(N_GROUPS, EPG * EXPERT_FF, D),
                       ln_g[l, 1][None, :], ln_b[l, 1][None, :])
    return (x[:NP_ROWS].reshape(BATCH, SEQ, D), x[NP_ROWS:].reshape(DEC_BATCH, DEC_SEQ, D),
            jnp.stack(out_mc, 1), jnp.stack(out_mn, 1), jnp.stack(out_mm, 1), jnp.stack(out_ds, 1),
            jnp.stack(out_k, 1), jnp.stack(out_v, 1))
```

```python
import functools

import jax
import jax.numpy as jnp
from jax import lax
from jax.experimental import pallas as pl
from jax.experimental.pallas import tpu as pltpu

F32 = jnp.float32
BF16 = jnp.bfloat16
HI = lax.Precision.HIGHEST

D = 1024
BATCH = 32
SEQ = 256
DEPTH = 4
DEC_BATCH = 2
DEC_SEQ = 4096
PAST_LEN = 512
GRID_W = 64
N_EVEN = 2
N_ODD = 2
HA = 4
HB = 4
DH = 128
CHUNK = 64
HC = 16
KVH = 4
HD = 64
WINDOW = 128
QBLOCK = 128
ROPE_THETA = 10000.0
N_GROUPS = 4
EPG = 4
N_EXPERTS = 16
EXPERT_FF = 256
DN_ALPHA = (2 * DEPTH) ** 0.25
LN_EPS = 1e-5

NP_ROWS = BATCH * SEQ
NS_ROWS = DEC_BATCH * DEC_SEQ
T = NP_ROWS + NS_ROWS
N_MOD_ROWS = 8
EVEN_W = 4224
QKV_W = (HC + 2 * KVH) * HD

P_CHUNKS = SEQ // CHUNK
S_CHUNKS = DEC_SEQ // CHUNK
P_STEPS = BATCH * P_CHUNKS
S_STEPS = DEC_BATCH * S_CHUNKS
P_BLOCKS = NP_ROWS // CHUNK

VMEM_LIMIT = 48 * 1024 * 1024


def _params(n_axes):
    return pltpu.CompilerParams(dimension_semantics=("arbitrary",) * n_axes,
                                vmem_limit_bytes=VMEM_LIMIT)


def _mm(a, b, prec=None):
    return lax.dot_general(a, b, (((1,), (0,)), ((), ())), precision=prec, preferred_element_type=F32)


def _mm_nt(a, b, prec=None):
    return lax.dot_general(a, b, (((1,), (1,)), ((), ())), precision=prec, preferred_element_type=F32)


def _mm_tn(a, b, prec=None):
    return lax.dot_general(a, b, (((0,), (0,)), ((), ())), precision=prec, preferred_element_type=F32)


def _bf(x):
    return x.astype(BF16)


def _sigmoid(x):
    return 1.0 / (1.0 + jnp.exp(-x))


def _softplus(x):
    return jnp.maximum(x, 0.0) + jnp.log1p(jnp.exp(-jnp.abs(x)))


def _log_sigmoid(x):
    return jnp.minimum(x, 0.0) - jnp.log1p(jnp.exp(-jnp.abs(x)))


def _mod_row(tile, tm):
    npt = NP_ROWS // tm
    per = DEC_SEQ // tm
    return jnp.where(tile < npt, 0, 1 + (tile - npt) // per)


def _mod_spec(layer, chunk, tm):
    def imap(i, *_):
        return ((layer * N_MOD_ROWS + _mod_row(i, tm)) * 6 + chunk, 0, 0)
    return pl.BlockSpec((1, 1, D), imap)


def _modulation_kernel(c_ref, w_ref, b_ref, o_ref):
    x = c_ref[...]
    s = x * _sigmoid(x)
    o_ref[0] = _mm(s, w_ref[0], HI) + b_ref[0]


def _modulation(cvecs, w_mod, b_mod):
    out = pl.pallas_call(
        _modulation_kernel,
        out_shape=jax.ShapeDtypeStruct((DEPTH, N_MOD_ROWS, 6 * D), F32),
        grid=(DEPTH, 6),
        in_specs=[pl.BlockSpec((N_MOD_ROWS, D), lambda l, j: (0, 0)),
                  pl.BlockSpec((1, D, D), lambda l, j: (l, 0, j)),
                  pl.BlockSpec((1, 1, D), lambda l, j: (l * 6 + j, 0, 0))],
        out_specs=pl.BlockSpec((1, N_MOD_ROWS, D), lambda l, j: (l, 0, j)),
        compiler_params=_params(2),
        name="modulation",
    )(cvecs, w_mod, b_mod.reshape(DEPTH * 6, 1, D))
    return out.reshape(DEPTH * N_MOD_ROWS * 6, 1, D)


def _proj_kernel(x_ref, sh_ref, sc_ref, w_ref, o_ref):
    xm = x_ref[...] * (1.0 + sc_ref[0]) + sh_ref[0]
    o_ref[...] = _mm(_bf(xm), w_ref[...])


def _even_proj(x, mods, layer, w):
    tm = 256
    return pl.pallas_call(
        _proj_kernel,
        out_shape=jax.ShapeDtypeStruct((T, EVEN_W), F32),
        grid=(T // tm,),
        in_specs=[pl.BlockSpec((tm, D), lambda i: (i, 0)),
                  _mod_spec(layer, 0, tm), _mod_spec(layer, 1, tm),
                  pl.BlockSpec((D, EVEN_W), lambda i: (0, 0))],
        out_specs=pl.BlockSpec((tm, EVEN_W), lambda i: (i, 0)),
        compiler_params=_params(1),
        name="even_proj",
    )(x, mods, mods, w)


def _qkv_kernel(x_ref, sh_ref, sc_ref, w_ref, cos_ref, sa_ref, sb_ref, o_ref, *, tm):
    i = pl.program_id(0)
    xm = x_ref[...] * (1.0 + sc_ref[0]) + sh_ref[0]
    acc = _mm(_bf(xm), w_ref[...])
    n_q = HC * HD // 128
    n_k = KVH * HD // 128
    is_latent = i >= NP_ROWS // tm
    cos = jnp.where(is_latent, cos_ref[...], 1.0)
    sa = jnp.where(is_latent, sa_ref[...], 0.0)
    sb = jnp.where(is_latent, sb_ref[...], 0.0)
    for g in range(n_q + n_k):
        blk = acc[:, g * 128:(g + 1) * 128]
        if g < n_q:
            blk = blk * (HD ** -0.5)
        rot = blk * cos + pltpu.roll(blk, 112, 1) * sa + pltpu.roll(blk, 16, 1) * sb
        o_ref[:, g * 128:(g + 1) * 128] = rot
    o_ref[:, (n_q + n_k) * 128:] = acc[:, (n_q + n_k) * 128:]


def _rope_tables():
    half = HD // 4
    inv = ROPE_THETA ** (-jnp.arange(half, dtype=F32) / half)
    pos = jnp.arange(DEC_SEQ)
    row = (pos // GRID_W).astype(F32)[:, None] * inv[None, :]
    col = (pos % GRID_W).astype(F32)[:, None] * inv[None, :]
    cos = jnp.concatenate([jnp.cos(row), jnp.cos(row), jnp.cos(col), jnp.cos(col)], axis=-1)
    sin = jnp.concatenate([jnp.sin(row), jnp.sin(row), jnp.sin(col), jnp.sin(col)], axis=-1)
    first = (jnp.arange(HD) % 32) < 16
    sa = jnp.where(first, -sin, 0.0)
    sb = jnp.where(first, 0.0, sin)
    tile2 = lambda t: jnp.concatenate([t, t], axis=-1)
    return tile2(cos), tile2(sa), tile2(sb)


def _odd_proj(x, mods, layer, w, tables):
    tm = 256
    npt = NP_ROWS // tm
    per = DEC_SEQ // tm
    tab_spec = pl.BlockSpec((tm, 128), lambda i: (jnp.where(i < npt, 0, (i - npt) % per), 0))
    return pl.pallas_call(
        functools.partial(_qkv_kernel, tm=tm),
        out_shape=jax.ShapeDtypeStruct((T, QKV_W), F32),
        grid=(T // tm,),
        in_specs=[pl.BlockSpec((tm, D), lambda i: (i, 0)),
                  _mod_spec(layer, 0, tm), _mod_spec(layer, 1, tm),
                  pl.BlockSpec((D, QKV_W), lambda i: (0, 0)),
                  tab_spec, tab_spec, tab_spec],
        out_specs=pl.BlockSpec((tm, QKV_W), lambda i: (i, 0)),
        compiler_params=_params(1),
        name="odd_qkv_proj",
    )(x, mods, mods, w, *tables)


PREP_ROWS = 256


def _delta_prep_kernel(x_ref, prev_ref, next_ref, w_ref, o_ref):
    i = pl.program_id(0)
    npb = NP_ROWS // PREP_ROWS
    per = DEC_SEQ // PREP_ROWS
    is_latent = i >= npb
    pos = (i - npb) % per
    has_prev = jnp.logical_and(is_latent, pos > 0)
    has_next = jnp.logical_and(is_latent, pos < per - 1)
    x = x_ref[...]
    w = w_ref[...]
    rows = lax.broadcasted_iota(jnp.int32, x.shape, 0)
    prev_row = jnp.where(has_prev, prev_ref[7:8, :], 0.0)
    next_row = jnp.where(has_next, next_ref[0:1, :], 0.0)
    xm1 = jnp.where(rows == 0, prev_row, pltpu.roll(x, 1, 0))
    xp1 = jnp.where(rows == PREP_ROWS - 1, next_row, pltpu.roll(x, PREP_ROWS - 1, 0))
    y = xm1 * w[0:1, :] + x * w[1:2, :] + xp1 * w[2:3, :]
    y = y * _sigmoid(y)
    for h in range(3 * HB):
        yh = y[:, h * DH:(h + 1) * DH]
        if h < 2 * HB:
            inv = lax.rsqrt(jnp.sum(yh * yh, axis=-1, keepdims=True) + 1e-6)
            yh = yh * (inv * (DH ** -0.5) if h < HB else inv)
        o_ref[:, h * DH:(h + 1) * DH] = yh


def _delta_prep(proj, conv_w):
    nblk = T // PREP_ROWS
    sub = PREP_ROWS // 8
    last8 = T // 8 - 1
    return pl.pallas_call(
        _delta_prep_kernel,
        out_shape=jax.ShapeDtypeStruct((T, 3 * 512), F32),
        grid=(nblk,),
        in_specs=[pl.BlockSpec((PREP_ROWS, 3 * 512), lambda i: (i, 0)),
                  pl.BlockSpec((8, 3 * 512), lambda i: (jnp.maximum(i * sub - 1, 0), 0)),
                  pl.BlockSpec((8, 3 * 512), lambda i: (jnp.minimum((i + 1) * sub, last8), 0)),
                  pl.BlockSpec((3, 3 * 512), lambda i: (0, 0))],
        out_specs=pl.BlockSpec((PREP_ROWS, 3 * 512), lambda i: (i, 0)),
        compiler_params=_params(1),
        name="delta_prep",
    )(proj, proj, proj, conv_w)


assert P_STEPS == S_STEPS
SCAN_STEPS = P_STEPS


def _bwd_local(s, nc):
    return (s // nc) * nc + nc - 1 - s % nc


def _scan_blocks():
    return (lambda s: s, lambda s: _bwd_local(s, P_CHUNKS),
            lambda s: P_BLOCKS + s, lambda s: P_BLOCKS + _bwd_local(s, S_CHUNKS))


def _scan_specs(xcol):
    gcol = EVEN_W // 128 - 1
    blocks = _scan_blocks()
    return ([pl.BlockSpec((CHUNK, 3 * 512), lambda s, f=f: (f(s), xcol)) for f in blocks]
            + [pl.BlockSpec((CHUNK, 128), lambda s, f=f: (f(s), gcol)) for f in blocks]
            + [pl.BlockSpec((1, 16, 128), lambda s, f=f: (f(s), 0, 0)) for f in blocks])


def _scan_out_specs():
    local = (lambda s: s, lambda s: _bwd_local(s, P_CHUNKS), lambda s: s, lambda s: _bwd_local(s, S_CHUNKS))
    return [pl.BlockSpec((CHUNK, 512), lambda s, f=f: (f(s), 0)) for f in local]


def _mlstm_kernel(xpf_ref, xpb_ref, xsf_ref, xsb_ref, gpf_ref, gpb_ref, gsf_ref, gsb_ref,
                  rpf_ref, rpb_ref, rsf_ref, rsb_ref,
                  brow_ref, b2_ref, c0_ref, n0_ref, m0_ref,
                  hpf_ref, hpb_ref, hsf_ref, hsb_ref, c_out, n_out, m_out,
                  cp_s, np_s, mp_s, cs_s, ns_s, ms_s):
    s = pl.program_id(0)
    jp = s % P_CHUNKS
    js = s % S_CHUNKS

    @pl.when(jp == 0)
    def _():
        cp_s[...] = jnp.zeros_like(cp_s)
        np_s[...] = jnp.zeros_like(np_s)
        mp_s[...] = jnp.zeros_like(mp_s)

    @pl.when(js == 0)
    def _():
        for d in range(2):
            for h in range(HA):
                cs_s[d * HA + h] = c0_ref[0, 0, d, h]
            ns_s[d * HA:(d + 1) * HA, :] = n0_ref[0, 0, d]
        ms_s[...] = m0_ref[0, 0]

    states = ((cp_s, np_s, mp_s), (cs_s, ns_s, ms_s))

    row = lax.broadcasted_iota(jnp.int32, (CHUNK, 128), 0)
    lane = lax.broadcasted_iota(jnp.int32, (CHUNK, 128), 1)
    left = lane < CHUNK
    left1 = left[0:1, :]
    lcol = jnp.where(left, lane, lane - CHUNK)
    r64 = lax.broadcasted_iota(jnp.int32, (CHUNK, CHUNK), 0)
    c64 = lax.broadcasted_iota(jnp.int32, (CHUNK, CHUNK), 1)
    r128 = lax.broadcasted_iota(jnp.int32, (128, 128), 0)
    c128 = lax.broadcasted_iota(jnp.int32, (128, 128), 1)
    same_half = (r128 < CHUNK) == (c128 < CHUNK)
    neg = -jnp.inf
    ins = ((xpf_ref, gpf_ref, rpf_ref, hpf_ref), (xpb_ref, gpb_ref, rpb_ref, hpb_ref),
           (xsf_ref, gsf_ref, rsf_ref, hsf_ref), (xsb_ref, gsb_ref, rsb_ref, hsb_ref))

    gate = []
    for u in range(4):
        d = u % 2
        gc_ref, gr_ref = ins[u][1], ins[u][2]
        if d == 0:
            incl_p, tri_c, tri_r = lcol <= row, c64 <= r64, jnp.logical_and(same_half, r128 <= c128)
        else:
            incl_p, tri_c, tri_r = lcol >= row, c64 >= r64, jnp.logical_and(same_half, r128 >= c128)
        gc = gc_ref[...] + brow_ref[...]
        gr = gr_ref[0] + b2_ref[...]
        cs_col = _prefix_cols(tri_c.astype(BF16), _log_sigmoid(gc))
        cs_row = _prefix_rows(_log_sigmoid(gr), tri_r.astype(BF16))
        gate.append((incl_p, gc, gr, cs_col, cs_row))

    st = []
    for u in range(4):
        d = u % 2
        c_s, n_s, m_s = states[u // 2]
        for hp in range(HA // 2):
            x_ref = ins[u][0]
            incl_p, gc, gr, cs_col, cs_row = gate[u]
            end = CHUNK - 1 if d == 0 else 0
            heads = []
            for h in (2 * hp, 2 * hp + 1):
                c = d * HA + h
                ci = d * 8 + h
                cf = d * 8 + 4 + h
                b_col = cs_col[:, cf:cf + 1]
                heads.append(dict(c=c, h=h, q=x_ref[:, h * DH:(h + 1) * DH],
                                  k=x_ref[:, 512 + h * DH:512 + (h + 1) * DH] * (DH ** -0.5),
                                  v=x_ref[:, 1024 + h * DH:1024 + (h + 1) * DH],
                                  i_col=gc[:, ci:ci + 1], b_col=b_col, b_last=b_col[end:end + 1, :],
                                  m=m_s[c:c + 1, 0:1]))
            ha, hb_ = heads
            i_row = gr[d * 4 + hp:d * 4 + hp + 1, :]
            b_row = cs_row[d * 4 + 2 + hp:d * 4 + 3 + hp, :]
            b_col_p = jnp.where(left, ha["b_col"], hb_["b_col"])
            m_p = jnp.where(left1, m_s[ha["c"]:ha["c"] + 1, :], m_s[hb_["c"]:hb_["c"] + 1, :])
            dmat = jnp.where(incl_p, b_col_p - b_row + i_row, neg)
            inter = b_col_p + m_p
            dmax = jnp.where(left, jnp.max(jnp.where(left, dmat, neg), axis=1, keepdims=True),
                             jnp.max(jnp.where(left, neg, dmat), axis=1, keepdims=True))
            m_t = jnp.maximum(inter, dmax)
            q_cat = _bf(jnp.concatenate([ha["q"], hb_["q"]], axis=1))
            k_bd = _bf(_block_diag2(ha["k"], hb_["k"]))
            sc = _mm_nt(q_cat, k_bd) * jnp.exp(dmat - m_t)
            st.append(dict(u=u, heads=heads, sc=sc, w_inter=jnp.exp(inter - m_t), m_t=m_t))

    for p in st:
        ha, hb_ = p["heads"]
        h_ref = ins[p["u"]][3]
        c_s, n_s, m_s = states[p["u"] // 2]
        sc = p["sc"]
        sv = _mm(_bf(sc), _bf(_block_diag2(ha["v"], hb_["v"])))
        dens = (jnp.sum(jnp.where(left, sc, 0.0), axis=1, keepdims=True),
                jnp.sum(jnp.where(left, 0.0, sc), axis=1, keepdims=True))
        for idx, hd in enumerate((ha, hb_)):
            c, h, q, k = hd["c"], hd["h"], hd["q"], hd["k"]
            w_inter = p["w_inter"][:, idx * CHUNK:idx * CHUNK + 1]
            m_t = p["m_t"][:, idx * CHUNK:idx * CHUNK + 1]
            c_mat = c_s[c]
            nrm = n_s[c:c + 1, :]
            num = sv[:, idx * DH:(idx + 1) * DH] + w_inter * _mm(_bf(q), _bf(c_mat))
            den = dens[idx] + w_inter * jnp.sum(q * nrm, axis=1, keepdims=True)
            h_ref[:, h * DH:(h + 1) * DH] = num / jnp.maximum(jnp.abs(den), jnp.exp(-m_t))
            g_end = hd["b_last"] - hd["b_col"] + hd["i_col"]
            m_new = jnp.maximum(hd["b_last"] + hd["m"], jnp.max(g_end, axis=0, keepdims=True))
            kw = k * jnp.exp(g_end - m_new)
            decay = jnp.exp(hd["b_last"] + hd["m"] - m_new)
            c_s[c] = decay * c_mat + _mm_tn(_bf(kw), _bf(hd["v"]))
            n_s[c:c + 1, :] = decay * nrm + jnp.sum(kw, axis=0, keepdims=True)
            m_s[c:c + 1, :] = jnp.broadcast_to(m_new, (1, 128))

    @pl.when(jp == P_CHUNKS - 1)
    def _():
        for d in range(2):
            for h in range(HA):
                c_out[0, d, h] = cp_s[d * HA + h]
            n_out[0, d] = np_s[d * HA:(d + 1) * HA, :]
        m_out[0] = mp_s[...]


def _mlstm_scan(proj, gates_t2, bias_row, bias2, c0, n0, m0, e):
    const2 = lambda s: (0, 0)
    in_specs = (_scan_specs(1)
                + [pl.BlockSpec((1, 128), const2), pl.BlockSpec((16, 128), const2),
                   pl.BlockSpec((1, 1, 2, HA, DH, DH), lambda s: (s // S_CHUNKS, e, 0, 0, 0, 0)),
                   pl.BlockSpec((1, 1, 2, HA, DH), lambda s: (s // S_CHUNKS, e, 0, 0, 0)),
                   pl.BlockSpec((1, 1, 2 * HA, 128), lambda s: (s // S_CHUNKS, e, 0, 0))])
    half = jax.ShapeDtypeStruct((NP_ROWS, HA * DH), F32)
    out_shape = (half, half, half, half,
                 jax.ShapeDtypeStruct((BATCH, 2, HA, DH, DH), F32),
                 jax.ShapeDtypeStruct((BATCH, 2, HA, DH), F32),
                 jax.ShapeDtypeStruct((BATCH, 2 * HA, 128), F32))
    out_specs = _scan_out_specs() + [
        pl.BlockSpec((1, 2, HA, DH, DH), lambda s: (s // P_CHUNKS, 0, 0, 0, 0)),
        pl.BlockSpec((1, 2, HA, DH), lambda s: (s // P_CHUNKS, 0, 0, 0)),
        pl.BlockSpec((1, 2 * HA, 128), lambda s: (s // P_CHUNKS, 0, 0))]
    state = [pltpu.VMEM((2 * HA, DH, DH), F32), pltpu.VMEM((2 * HA, DH), F32), pltpu.VMEM((2 * HA, 128), F32)]
    return pl.pallas_call(
        _mlstm_kernel,
        out_shape=out_shape,
        grid=(SCAN_STEPS,),
        in_specs=in_specs,
        out_specs=out_specs,
        scratch_shapes=state + state,
        compiler_params=_params(1),
        name="mlstm_scan",
    )(proj, proj, proj, proj, proj, proj, proj, proj, gates_t2, gates_t2, gates_t2, gates_t2,
      bias_row, bias2, c0, n0, m0)


def _split2(x):
    hi = _bf(x)
    return hi, _bf(x - hi.astype(F32))


def _split3(x):
    h1 = _bf(x)
    r1 = x - h1.astype(F32)
    h2 = _bf(r1)
    return h1, h2, _bf(r1 - h2.astype(F32))


def _lhs3(x):
    hi, lo = _split2(x)
    return jnp.concatenate([hi, lo, hi], axis=1)


def _rhs3(x):
    hi, lo = _split2(x)
    return jnp.concatenate([hi, hi, lo], axis=0)


def _prefix_cols(tri_bf, x):
    n = x.shape[1]
    r = _mm(tri_bf, jnp.concatenate(_split3(x), axis=1))
    return r[:, :n] + r[:, n:2 * n] + r[:, 2 * n:]


def _prefix_rows(x, tri_bf):
    m = x.shape[0]
    r = _mm(jnp.concatenate(_split3(x), axis=0), tri_bf)
    return r[:m] + r[m:2 * m] + r[2 * m:]


LEVELS = tuple(range(6))


def _block_diag2(a, b):
    z = jnp.zeros_like(a)
    return jnp.concatenate([jnp.concatenate([a, z], axis=1), jnp.concatenate([z, b], axis=1)], axis=0)


def _delta_kernel(xpf_ref, xpb_ref, xsf_ref, xsb_ref, gpf_ref, gpb_ref, gsf_ref, gsb_ref,
                  rpf_ref, rpb_ref, rsf_ref, rsb_ref,
                  dtrow_ref, dt2_ref, narow_ref, na2_ref, s0_ref,
                  opf_ref, opb_ref, osf_ref, osb_ref, s_out, sp_s, ss_s):
    s = pl.program_id(0)
    jp = s % P_CHUNKS
    js = s % S_CHUNKS

    @pl.when(jp == 0)
    def _():
        sp_s[...] = jnp.zeros_like(sp_s)

    @pl.when(js == 0)
    def _():
        for d in range(2):
            for h in range(HB):
                ss_s[d * HB + h] = s0_ref[0, 0, d, h]

    states = (sp_s, ss_s)

    row = lax.broadcasted_iota(jnp.int32, (CHUNK, 128), 0)
    lane = lax.broadcasted_iota(jnp.int32, (CHUNK, 128), 1)
    left = lane < CHUNK
    lcol = jnp.where(left, lane, lane - CHUNK)
    eye_p = (lcol == row).astype(F32)
    r64 = lax.broadcasted_iota(jnp.int32, (CHUNK, CHUNK), 0)
    c64 = lax.broadcasted_iota(jnp.int32, (CHUNK, CHUNK), 1)
    r128 = lax.broadcasted_iota(jnp.int32, (128, 128), 0)
    c128 = lax.broadcasted_iota(jnp.int32, (128, 128), 1)
    same_half = (r128 < CHUNK) == (c128 < CHUNK)
    ins = ((xpf_ref, gpf_ref, rpf_ref, opf_ref), (xpb_ref, gpb_ref, rpb_ref, opb_ref),
           (xsf_ref, gsf_ref, rsf_ref, osf_ref), (xsb_ref, gsb_ref, rsb_ref, osb_ref))

    gate = []
    for u in range(4):
        d = u % 2
        gc_ref, gr_ref = ins[u][1], ins[u][2]
        if d == 0:
            incl_p, strict_p = lcol <= row, lcol < row
            tri_c, tri_r = c64 <= r64, jnp.logical_and(same_half, r128 <= c128)
        else:
            incl_p, strict_p = lcol >= row, lcol > row
            tri_c, tri_r = c64 >= r64, jnp.logical_and(same_half, r128 >= c128)
        xc = gc_ref[...]
        xr = gr_ref[0]
        la_c = narow_ref[...] * _softplus(xc + dtrow_ref[...])
        la_r = na2_ref[...] * _softplus(xr + dt2_ref[...])
        g_c = _prefix_cols(tri_c.astype(BF16), la_c)
        g_r = _prefix_rows(la_r, tri_r.astype(BF16))
        gate.append((incl_p, strict_p, _sigmoid(xc), g_c, g_r))

    def bd_rhs(x):
        return _rhs3(jnp.concatenate([jnp.where(left, x, 0.0), jnp.where(left, 0.0, x)], axis=0))

    off = ([], [])
    for lv in LEVELS:
        same = jnp.right_shift(row, lv + 1) == jnp.right_shift(lcol, lv + 1)
        r_hi = jnp.bitwise_and(jnp.right_shift(row, lv), 1) == 1
        c_hi = jnp.bitwise_and(jnp.right_shift(lcol, lv), 1) == 1
        off[0].append(same & r_hi & jnp.logical_not(c_hi))
        off[1].append(same & jnp.logical_not(r_hi) & c_hi)

    pairs = [(u, hp) for u in range(4) for hp in range(HB // 2)]
    st = []
    for u, hp in pairs:
        d = u % 2
        x_ref = ins[u][0]
        incl_p, strict_p, beta_c, g_c, g_r = gate[u]
        end = CHUNK - 1 if d == 0 else 0
        heads = []
        for h in (2 * hp, 2 * hp + 1):
            ib = 16 + d * HB + h
            ia = 24 + d * HB + h
            gcol = g_c[:, ia:ia + 1]
            bc = beta_c[:, ib:ib + 1]
            q = x_ref[:, h * DH:(h + 1) * DH]
            k = x_ref[:, 512 + h * DH:512 + (h + 1) * DH]
            v = x_ref[:, 1024 + h * DH:1024 + (h + 1) * DH]
            heads.append(dict(h=h, gcol=gcol, bc=bc, q=q, k=k, v=v, kb=k * bc, eg=jnp.exp(gcol),
                              g_last=gcol[end:end + 1, :]))
        ha, hb_ = heads
        r = 12 + d * 2 + hp
        gcol_p = jnp.where(left, ha["gcol"], hb_["gcol"])
        decay = jnp.exp(jnp.where(incl_p, gcol_p - g_r[r:r + 1, :], -jnp.inf))
        k_bd = _bf(_block_diag2(ha["k"], hb_["k"]))
        kb_cat = _bf(jnp.concatenate([ha["kb"], hb_["kb"]], axis=1))
        q_cat = _bf(jnp.concatenate([ha["q"], hb_["q"]], axis=1))
        a_mat = jnp.where(strict_p, _mm_nt(kb_cat, k_bd) * decay, 0.0)
        qk = _mm_nt(q_cat, k_bd) * decay
        st.append(dict(u=u, d=d, heads=heads, t=eye_p - jnp.where(off[d][0], a_mat, 0.0), qk=qk,
                       am=[bd_rhs(jnp.where(m, a_mat, 0.0)) for m in off[d][1:]]))

    for li in range(len(LEVELS) - 1):
        for p in st:
            p["w"] = _mm(_lhs3(p["t"]), p["am"][li])
        for p in st:
            p["t"] = p["t"] - _mm(_lhs3(p["w"]), bd_rhs(p["t"]))

    for p in st:
        ha, hb_ = p["heads"]
        o_ref = ins[p["u"]][3]
        s_s = states[p["u"] // 2]
        rhs_a = jnp.concatenate([ha["v"] * ha["bc"], ha["kb"] * ha["eg"]], axis=1)
        rhs_b = jnp.concatenate([hb_["v"] * hb_["bc"], hb_["kb"] * hb_["eg"]], axis=1)
        sol = _mm(_lhs3(p["t"]), _rhs3(_block_diag2(rhs_a, rhs_b)))
        vn = []
        for idx, hd in enumerate((ha, hb_)):
            c = p["d"] * HB + hd["h"]
            s_mat = s_s[c]
            sbf = _bf(s_mat)
            so = sol[:, idx * 2 * DH:(idx + 1) * 2 * DH]
            v_new = so[:, :DH] - _mm(_bf(so[:, DH:]), sbf)
            vn.append(v_new)
            hd["o1"] = _mm(_bf(hd["q"] * hd["eg"]), sbf)
            s_s[c] = (jnp.exp(hd["g_last"]) * s_mat
                      + _mm_tn(_bf(hd["k"] * jnp.exp(hd["g_last"] - hd["gcol"])), _bf(v_new)))
        o2 = _mm(_bf(p["qk"]), _bf(_block_diag2(vn[0], vn[1])))
        for idx, hd in enumerate((ha, hb_)):
            o_ref[:, hd["h"] * DH:(hd["h"] + 1) * DH] = hd["o1"] + o2[:, idx * DH:(idx + 1) * DH]

    @pl.when(jp == P_CHUNKS - 1)
    def _():
        for d in range(2):
            for h in range(HB):
                s_out[0, d, h] = sp_s[d * HB + h]


def _delta_scan(qkv, proj, gates_t2, dt_row, dt2, na_row, na2, s0, e):
    const2 = lambda s: (0, 0)
    in_specs = (_scan_specs(0)
                + [pl.BlockSpec((1, 128), const2), pl.BlockSpec((16, 128), const2),
                   pl.BlockSpec((1, 128), const2), pl.BlockSpec((16, 128), const2),
                   pl.BlockSpec((1, 1, 2, HB, DH, DH), lambda s: (s // S_CHUNKS, e, 0, 0, 0, 0))])
    half = jax.ShapeDtypeStruct((NP_ROWS, HB * DH), F32)
    out_shape = (half, half, half, half, jax.ShapeDtypeStruct((BATCH, 2, HB, DH, DH), F32))
    out_specs = _scan_out_specs() + [pl.BlockSpec((1, 2, HB, DH, DH), lambda s: (s // P_CHUNKS, 0, 0, 0, 0))]
    return pl.pallas_call(
        _delta_kernel,
        out_shape=out_shape,
        grid=(SCAN_STEPS,),
        in_specs=in_specs,
        out_specs=out_specs,
        scratch_shapes=[pltpu.VMEM((2 * HB, DH, DH), F32), pltpu.VMEM((2 * HB, DH, DH), F32)],
        compiler_params=_params(1),
        name="delta_scan",
    )(qkv, qkv, qkv, qkv, proj, proj, proj, proj, gates_t2, gates_t2, gates_t2, gates_t2,
      dt_row, dt2, na_row, na2, s0)


def _residual_ln(x, gate, y, g, b):
    r = DN_ALPHA * x + gate * y
    mu = jnp.mean(r, axis=-1, keepdims=True)
    var = jnp.mean(jnp.square(r - mu), axis=-1, keepdims=True)
    return (r - mu) * lax.rsqrt(var + LN_EPS) * g + b


def _even_out_kernel(hpf_ref, hpb_ref, hsf_ref, hsb_ref, opf_ref, opb_ref, osf_ref, osb_ref,
                     oa_ref, zb_ref, mg_ref, dg_ref, w_ref, x_ref, gate_ref, g_ref, b_ref, o_ref, *, tm):
    is_p = pl.program_id(0) < NP_ROWS // tm
    hf_ref, hb_ref, of_ref, ob_ref = (
        lambda sl, p=p, q=q: jnp.where(is_p, p[:, sl], q[:, sl])
        for p, q in ((hpf_ref, hsf_ref), (hpb_ref, hsb_ref), (opf_ref, osf_ref), (opb_ref, osb_ref)))
    parts = []
    for h in range(HA):
        sl = slice(h * DH, (h + 1) * DH)
        hh = hf_ref(sl) + hb_ref(sl)
        mu = jnp.mean(hh, axis=-1, keepdims=True)
        var = jnp.mean(jnp.square(hh - mu), axis=-1, keepdims=True)
        parts.append(_sigmoid(oa_ref[:, sl]) * ((hh - mu) * lax.rsqrt(var + LN_EPS) * mg_ref[:, sl]))
    for h in range(HB):
        sl = slice(h * DH, (h + 1) * DH)
        oo = of_ref(sl) + ob_ref(sl)
        z = zb_ref[:, sl]
        nrm = oo * lax.rsqrt(jnp.mean(jnp.square(oo), axis=-1, keepdims=True) + LN_EPS) * dg_ref[:, sl]
        parts.append(nrm * (z * _sigmoid(z)))
    a = jnp.concatenate(parts, axis=1)
    y = _mm(_bf(a), w_ref[...])
    o_ref[...] = _residual_ln(x_ref[...], gate_ref[0], y, g_ref[...], b_ref[...])


def _even_out(h_scan, o_scan, proj, mg, dg, w, x, mods, layer, ln_g, ln_b):
    tm = 256
    npt = NP_ROWS // tm
    row512 = lambda i: (i, 0)
    const2 = lambda i: (0, 0)
    p_spec = pl.BlockSpec((tm, 512), lambda i: (jnp.minimum(i, npt - 1), 0))
    s_spec = pl.BlockSpec((tm, 512), lambda i: (jnp.maximum(i - npt, 0), 0))
    return pl.pallas_call(
        functools.partial(_even_out_kernel, tm=tm),
        out_shape=jax.ShapeDtypeStruct((T, D), F32),
        grid=(T // tm,),
        in_specs=[p_spec, p_spec, s_spec, s_spec, p_spec, p_spec, s_spec, s_spec,
                  pl.BlockSpec((tm, 512), lambda i: (i, 6)),
                  pl.BlockSpec((tm, 512), lambda i: (i, 7)),
                  pl.BlockSpec((1, 512), const2), pl.BlockSpec((1, 512), const2),
                  pl.BlockSpec((D, D), const2),
                  pl.BlockSpec((tm, D), row512),
                  _mod_spec(layer, 2, tm),
                  pl.BlockSpec((1, D), const2), pl.BlockSpec((1, D), const2)],
        out_specs=pl.BlockSpec((tm, D), row512),
        compiler_params=_params(1),
        name="even_out_ln",
    )(*h_scan, *o_scan, proj, proj, mg, dg, w, x, mods, ln_g, ln_b)


def _odd_out_kernel(ap_ref, as_ref, w_ref, x_ref, gate_ref, g_ref, b_ref, o_ref, *, tm):
    a = jnp.where(pl.program_id(0) < NP_ROWS // tm, ap_ref[...], as_ref[...])
    y = _mm(_bf(a), w_ref[...])
    o_ref[...] = _residual_ln(x_ref[...], gate_ref[0], y, g_ref[...], b_ref[...])


def _odd_out(a_prompt, a_latent, w, x, mods, layer, ln_g, ln_b):
    tm = 256
    npt = NP_ROWS // tm
    row = lambda i: (i, 0)
    const2 = lambda i: (0, 0)
    return pl.pallas_call(
        functools.partial(_odd_out_kernel, tm=tm),
        out_shape=jax.ShapeDtypeStruct((T, D), F32),
        grid=(T // tm,),
        in_specs=[pl.BlockSpec((tm, D), lambda i: (jnp.minimum(i, npt - 1), 0)),
                  pl.BlockSpec((tm, D), lambda i: (jnp.maximum(i - npt, 0), 0)),
                  pl.BlockSpec((D, D), const2), pl.BlockSpec((tm, D), row),
                  _mod_spec(layer, 2, tm), pl.BlockSpec((1, D), const2), pl.BlockSpec((1, D), const2)],
        out_specs=pl.BlockSpec((tm, D), row),
        compiler_params=_params(1),
        name="odd_out_ln",
    )(a_prompt, a_latent, w, x, mods, ln_g, ln_b)


def _attn_ctx_kernel(qkv_ref, sink_ref, o_ref):
    ones = jnp.ones((SEQ, HD), BF16)
    for kv in range(KVH):
        k = _bf(qkv_ref[:, HC * HD + kv * HD:HC * HD + (kv + 1) * HD])
        v = _bf(qkv_ref[:, (HC + KVH) * HD + kv * HD:(HC + KVH) * HD + (kv + 1) * HD])
        v1 = jnp.concatenate([v, ones], axis=1)
        heads = range(kv * (HC // KVH), (kv + 1) * (HC // KVH))
        sts = [_mm_nt(k, _bf(qkv_ref[:, h * HD:(h + 1) * HD])) for h in heads]
        ms = [jnp.maximum(jnp.max(st, axis=0, keepdims=True), sink_ref[h]) for st, h in zip(sts, heads)]
        ovs = [_mm_tn(v1, _bf(jnp.exp(st - m))) for st, m in zip(sts, ms)]
        outs = [ov[:HD] / (ov[HD:HD + 1] + jnp.exp(sink_ref[h] - m)) for ov, m, h in zip(ovs, ms, heads)]
        for pr in range(HC // KVH // 2):
            lane0 = (kv * (HC // KVH) + 2 * pr) * HD
            o_ref[:, lane0:lane0 + 2 * HD] = jnp.concatenate(outs[2 * pr:2 * pr + 2], axis=0).T


def _attn_context(qkv, sink):
    return pl.pallas_call(
        _attn_ctx_kernel,
        out_shape=jax.ShapeDtypeStruct((NP_ROWS, HC * HD), F32),
        grid=(BATCH,),
        in_specs=[pl.BlockSpec((SEQ, QKV_W), lambda b: (b, 0)),
                  pl.BlockSpec(memory_space=pltpu.SMEM)],
        out_specs=pl.BlockSpec((SEQ, HC * HD), lambda b: (b, 0)),
        compiler_params=_params(1),
        name="attn_context",
    )(qkv, sink)


def _attn_lat_kernel(q_ref, kp_ref, kc_ref, kn_ref, vp_ref, vc_ref, vn_ref, ck_ref, cv_ref,
                     sink_ref, o_ref, bias_s):
    j = pl.program_id(1)
    nb = DEC_SEQ // QBLOCK
    cc = lax.broadcasted_iota(jnp.int32, (3 * QBLOCK, QBLOCK), 0)
    r = lax.broadcasted_iota(jnp.int32, (3 * QBLOCK, QBLOCK), 1)
    lo = jnp.where(j >= 1, 0, QBLOCK)
    hi = jnp.where(j <= nb - 2, 3 * QBLOCK, 2 * QBLOCK)
    ok = (jnp.abs(QBLOCK + r - cc) <= WINDOW) & (cc >= lo) & (cc < hi)
    bias_s[...] = jnp.where(ok, 0.0, -jnp.inf)
    for kv in range(KVH):
        ks = slice(kv * HD, (kv + 1) * HD)
        k_all = _bf(jnp.concatenate([ck_ref[0, 0, kv], kp_ref[:, ks], kc_ref[:, ks], kn_ref[:, ks]], axis=0))
        v_all = _bf(jnp.concatenate([cv_ref[0, 0, kv], vp_ref[:, ks], vc_ref[:, ks], vn_ref[:, ks]], axis=0))
        v1 = jnp.concatenate([v_all, jnp.ones_like(v_all)], axis=1)
        outs = []
        for h in range(kv * (HC // KVH), (kv + 1) * (HC // KVH)):
            sink = sink_ref[h]
            st = _mm_nt(k_all, _bf(q_ref[:, h * HD:(h + 1) * HD]))
            s_ctx = st[:PAST_LEN]
            s_loc = st[PAST_LEN:] + bias_s[...]
            m = jnp.maximum(jnp.maximum(jnp.max(s_ctx, axis=0, keepdims=True),
                                        jnp.max(s_loc, axis=0, keepdims=True)), sink)
            p = _bf(jnp.concatenate([jnp.exp(s_ctx - m), jnp.exp(s_loc - m)], axis=0))
            ov = _mm_tn(v1, p)
            outs.append(ov[:HD] / (ov[HD:HD + 1] + jnp.exp(sink - m)))
        for pr in range(HC // KVH // 2):
            lane0 = (kv * (HC // KVH) + 2 * pr) * HD
            o_ref[:, lane0:lane0 + 2 * HD] = jnp.concatenate(outs[2 * pr:2 * pr + 2], axis=0).T


def _attn_latent(qkv, cache_k, cache_v, sink, o):
    nb = DEC_SEQ // QBLOCK
    base = NP_ROWS // QBLOCK
    blk = lambda b, j: base + b * nb + j
    prev = lambda b, j: base + b * nb + jnp.maximum(j - 1, 0)
    nxt = lambda b, j: base + b * nb + jnp.minimum(j + 1, nb - 1)
    kcol, vcol = HC * HD // 256, HC * HD // 256 + 1
    cache_spec = pl.BlockSpec((1, 1, KVH, PAST_LEN, HD), lambda b, j: (b, o, 0, 0, 0))
    return pl.pallas_call(
        _attn_lat_kernel,
        out_shape=jax.ShapeDtypeStruct((NS_ROWS, HC * HD), F32),
        grid=(DEC_BATCH, nb),
        in_specs=[pl.BlockSpec((QBLOCK, HC * HD), lambda b, j: (blk(b, j), 0)),
                  pl.BlockSpec((QBLOCK, 256), lambda b, j: (prev(b, j), kcol)),
                  pl.BlockSpec((QBLOCK, 256), lambda b, j: (blk(b, j), kcol)),
                  pl.BlockSpec((QBLOCK, 256), lambda b, j: (nxt(b, j), kcol)),
                  pl.BlockSpec((QBLOCK, 256), lambda b, j: (prev(b, j), vcol)),
                  pl.BlockSpec((QBLOCK, 256), lambda b, j: (blk(b, j), vcol)),
                  pl.BlockSpec((QBLOCK, 256), lambda b, j: (nxt(b, j), vcol)),
                  cache_spec, cache_spec,
                  pl.BlockSpec(memory_space=pltpu.SMEM)],
        out_specs=pl.BlockSpec((QBLOCK, HC * HD), lambda b, j: (b * nb + j, 0)),
        scratch_shapes=[pltpu.VMEM((3 * QBLOCK, QBLOCK), F32)],
        compiler_params=_params(2),
        name="attn_latent",
    )(qkv, qkv, qkv, qkv, qkv, qkv, qkv, cache_k, cache_v, sink)


def _route(lg):
    lane = lax.broadcasted_iota(jnp.int32, lg.shape, 1)
    neg = -jnp.inf
    big = 1 << 20
    is_grp = jnp.logical_and(lane >= N_EXPERTS, lane < N_EXPERTS + N_GROUPS)
    mg = jnp.max(jnp.where(is_grp, lg, neg), axis=1, keepdims=True)
    g_lane = jnp.min(jnp.where(jnp.logical_and(is_grp, lg == mg), lane, big), axis=1, keepdims=True)
    g_w = 1.0 / jnp.sum(jnp.where(is_grp, jnp.exp(lg - mg), 0.0), axis=1, keepdims=True)
    g_idx = g_lane - N_EXPERTS
    in_grp = jnp.logical_and(lane >= g_idx * EPG, lane < (g_idx + 1) * EPG)
    v1 = jnp.max(jnp.where(in_grp, lg, neg), axis=1, keepdims=True)
    i1 = jnp.min(jnp.where(jnp.logical_and(in_grp, lg == v1), lane, big), axis=1, keepdims=True)
    rest = jnp.logical_and(in_grp, lane != i1)
    v2 = jnp.max(jnp.where(rest, lg, neg), axis=1, keepdims=True)
    i2 = jnp.min(jnp.where(jnp.logical_and(rest, lg == v2), lane, big), axis=1, keepdims=True)
    e2 = jnp.exp(v2 - v1)
    p1 = 1.0 / (1.0 + e2)
    p2 = e2 / (1.0 + e2)
    return jnp.where(lane == 0, i1.astype(F32),
                     jnp.where(lane == 1, i2.astype(F32),
                               jnp.where(lane == 2, p1 * g_w, jnp.where(lane == 3, p2 * g_w, 0.0))))


def _moe_dense_kernel(x_ref, sh_ref, sc_ref, wr_ref, br_ref, wg_ref, wu_ref, wd_ref, gate_ref, g_ref, b_ref, o_ref,
                      xm_s, meta_s, acc_s):
    grp = pl.program_id(1)

    @pl.when(grp == 0)
    def _():
        xm = x_ref[...] * (1.0 + sc_ref[0]) + sh_ref[0]
        meta_s[...] = _route(_mm(_lhs3(xm), _rhs3(wr_ref[...])) + br_ref[...])
        xm_s[...] = _bf(xm)
        acc_s[...] = jnp.zeros_like(acc_s)

    xm = xm_s[...]
    meta = meta_s[...]
    i1 = meta[:, 0:1].astype(jnp.int32)
    i2 = meta[:, 1:2].astype(jnp.int32)
    w1 = meta[:, 2:3]
    w2 = meta[:, 3:4]
    hid = []
    for e in range(EPG):
        eid = grp * EPG + e
        gate = jnp.where(i1 == eid, w1, 0.0) + jnp.where(i2 == eid, w2, 0.0)
        a = _mm(xm, wg_ref[0, e])
        u = _mm(xm, wu_ref[0, e])
        hid.append(_bf((a * _sigmoid(a)) * u * gate))
    acc_s[...] += _mm(jnp.concatenate(hid, axis=1), wd_ref[0])

    @pl.when(grp == N_GROUPS - 1)
    def _():
        o_ref[...] = _residual_ln(x_ref[...], gate_ref[0], acc_s[...], g_ref[...], b_ref[...])


def _moe_dense(x, mods, layer, w_r, b_r, wg, wu, wd, ln_g, ln_b):
    tm = 1024
    row = lambda i, g: (i, 0)
    const2 = lambda i, g: (0, 0)
    return pl.pallas_call(
        _moe_dense_kernel,
        out_shape=jax.ShapeDtypeStruct((T, D), F32),
        grid=(T // tm, N_GROUPS),
        in_specs=[pl.BlockSpec((tm, D), row), _mod_spec(layer, 3, tm), _mod_spec(layer, 4, tm),
                  pl.BlockSpec((D, 128), const2), pl.BlockSpec((1, 128), const2),
                  pl.BlockSpec((1, EPG, D, EXPERT_FF), lambda i, g: (g, 0, 0, 0)),
                  pl.BlockSpec((1, EPG, D, EXPERT_FF), lambda i, g: (g, 0, 0, 0)),
                  pl.BlockSpec((1, EPG * EXPERT_FF, D), lambda i, g: (g, 0, 0)),
                  _mod_spec(layer, 5, tm), pl.BlockSpec((1, D), const2), pl.BlockSpec((1, D), const2)],
        out_specs=pl.BlockSpec((tm, D), row),
        scratch_shapes=[pltpu.VMEM((tm, D), BF16), pltpu.VMEM((tm, 128), F32), pltpu.VMEM((tm, D), F32)],
        compiler_params=_params(2),
        name="moe_dense",
    )(x, mods, mods, w_r, b_r, wg, wu, wd, mods, ln_g, ln_b)


def _permute_even_w(w):
    a_end = 4 * HA * DH
    g_end = a_end + 4 * HA
    c_end = g_end + 3 * HB * DH
    z_end = c_end + HB * DH
    small = jnp.concatenate([w[:, a_end:g_end], w[:, z_end:]], axis=1)
    pad = jnp.zeros((w.shape[0], 128 - small.shape[1]), w.dtype)
    return jnp.concatenate([w[:, g_end:c_end], w[:, :a_end], w[:, c_end:z_end], small, pad], axis=1)


def _lane_row(vals, offset):
    return jnp.zeros((1, 128), F32).at[0, offset:offset + vals.shape[0]].set(vals.astype(F32))


def _pair_rows(vals, offset):
    v32 = jnp.zeros((32,), F32).at[offset:offset + vals.shape[0]].set(vals.astype(F32))
    return jnp.repeat(v32.reshape(16, 2), CHUNK, axis=1)


def kernel(x_prompt, x_sample, c, c_ctx, state_mlstm_c, state_mlstm_n, state_mlstm_m, state_delta, cache_k, cache_v, w_mod, b_mod, ln_g, ln_b, w_in_even, mlstm_gate_b, mlstm_norm_g, delta_conv_w, delta_a_log, delta_dt_bias, delta_norm_g, w_out_even, w_qkv_odd, attn_sink, w_out_odd, w_grp, b_grp, w_erouter, b_erouter, w_gate, w_up, w_down):
    x = jnp.concatenate([x_prompt.reshape(NP_ROWS, D), x_sample.reshape(NS_ROWS, D)], axis=0)
    cvecs = jnp.concatenate([c_ctx[None, :], c, jnp.zeros((N_MOD_ROWS - 1 - DEC_BATCH, D), F32)], axis=0)
    mods = _modulation(cvecs, w_mod, b_mod)
    tables = _rope_tables()
    m0_all = jnp.broadcast_to(state_mlstm_m.reshape(DEC_BATCH, N_EVEN, 2 * HA, 1), (DEC_BATCH, N_EVEN, 2 * HA, 128))

    out_mc, out_mn, out_mm, out_ds, out_k, out_v = [], [], [], [], [], []
    for l in range(DEPTH):
        if l % 2 == 0:
            e = l // 2
            proj = _even_proj(x, mods, l, _bf(_permute_even_w(w_in_even[e])))
            gates_t2 = (proj[:, EVEN_W - 128:EVEN_W - 96].reshape(T // CHUNK, CHUNK, 32).transpose(0, 2, 1)
                        .reshape(T // CHUNK, 16, 128))
            gb = mlstm_gate_b[e].reshape(-1)
            *h_scan, mc, mn, mm = _mlstm_scan(proj, gates_t2, _lane_row(gb, 0), _pair_rows(gb, 0),
                                              state_mlstm_c, state_mlstm_n, m0_all, e)
            qkv = _delta_prep(proj, delta_conv_w[e])
            dtb = delta_dt_bias[e].reshape(-1)
            nea = -jnp.exp(delta_a_log[e].astype(F32)).reshape(-1)
            *o_scan, ds = _delta_scan(qkv, proj, gates_t2, _lane_row(dtb, 24), _pair_rows(dtb, 24),
                                      _lane_row(nea, 24), _pair_rows(nea, 24), state_delta, e)
            x = _even_out(h_scan, o_scan, proj, mlstm_norm_g[e][None, :], delta_norm_g[e][None, :],
                          _bf(w_out_even[e]), x, mods, l, ln_g[l, 0][None, :], ln_b[l, 0][None, :])
            out_mc.append(mc)
            out_mn.append(mn)
            out_mm.append(mm[:, :, 0].reshape(BATCH, 2, HA))
            out_ds.append(ds)
        else:
            o = l // 2
            qkv = _odd_proj(x, mods, l, _bf(w_qkv_odd[o]), tables)
            a_p = _attn_context(qkv, attn_sink[o])
            a_s = _attn_latent(qkv, cache_k, cache_v, attn_sink[o], o)
            x = _odd_out(a_p, a_s, _bf(w_out_odd[o]), x, mods, l, ln_g[l, 0][None, :], ln_b[l, 0][None, :])
            kp = qkv[:NP_ROWS, HC * HD:(HC + KVH) * HD].reshape(BATCH, SEQ, KVH, HD).transpose(0, 2, 1, 3)
            vp = qkv[:NP_ROWS, (HC + KVH) * HD:].reshape(BATCH, SEQ, KVH, HD).transpose(0, 2, 1, 3)
            out_k.append(kp)
            out_v.append(vp)
        w_r = jnp.concatenate([w_erouter[l].transpose(1, 0, 2).reshape(D, N_EXPERTS), w_grp[l],
                               jnp.zeros((D, 128 - N_EXPERTS - N_GROUPS), F32)], axis=1)
        b_r = jnp.concatenate([b_erouter[l].reshape(-1), b_grp[l],
                               jnp.zeros((128 - N_EXPERTS - N_GROUPS,), F32)])[None, :]
        x = _moe_dense(x, mods, l, w_r, b_r,
                       _bf(w_gate[l]).reshape(N_GROUPS, EPG, D, EXPERT_FF),
                       _bf(w_up[l]).reshape(N_GROUPS, EPG, D, EXPERT_FF),
                       _bf(w_down[l]).reshape(N_GROUPS, EPG * EXPERT_FF, D),
                       ln_g[l, 1][None, :], ln_b[l, 1][None, :])
    return (x[:NP_ROWS].reshape(BATCH, SEQ, D), x[NP_ROWS:].reshape(DEC_BATCH, DEC_SEQ, D),
            jnp.stack(out_mc, 1), jnp.stack(out_mn, 1), jnp.stack(out_mm, 1), jnp.stack(out_ds, 1),
            jnp.stack(out_k, 1), jnp.stack(out_v, 1))
```

```python
import functools

import jax
import jax.numpy as jnp
from jax import lax
from jax.experimental import pallas as pl
from jax.experimental.pallas import tpu as pltpu

F32 = jnp.float32
BF16 = jnp.bfloat16
HI = lax.Precision.HIGHEST

D = 1024
BATCH = 32
SEQ = 256
DEPTH = 4
DEC_BATCH = 2
DEC_SEQ = 4096
PAST_LEN = 512
GRID_W = 64
N_EVEN = 2
N_ODD = 2
HA = 4
HB = 4
DH = 128
CHUNK = 64
HC = 16
KVH = 4
HD = 64
WINDOW = 128
QBLOCK = 128
ROPE_THETA = 10000.0
N_GROUPS = 4
EPG = 4
N_EXPERTS = 16
EXPERT_FF = 256
DN_ALPHA = (2 * DEPTH) ** 0.25
LN_EPS = 1e-5

NP_ROWS = BATCH * SEQ
NS_ROWS = DEC_BATCH * DEC_SEQ
T = NP_ROWS + NS_ROWS
N_MOD_ROWS = 8
EVEN_W = 4224
QKV_W = (HC + 2 * KVH) * HD

P_CHUNKS = SEQ // CHUNK
S_CHUNKS = DEC_SEQ // CHUNK
P_STEPS = BATCH * P_CHUNKS
S_STEPS = DEC_BATCH * S_CHUNKS
P_BLOCKS = NP_ROWS // CHUNK

VMEM_LIMIT = 48 * 1024 * 1024


def _params(n_axes):
    return pltpu.CompilerParams(dimension_semantics=("arbitrary",) * n_axes,
                                vmem_limit_bytes=VMEM_LIMIT)


def _mm(a, b, prec=None):
    return lax.dot_general(a, b, (((1,), (0,)), ((), ())), precision=prec, preferred_element_type=F32)


def _mm_nt(a, b, prec=None):
    return lax.dot_general(a, b, (((1,), (1,)), ((), ())), precision=prec, preferred_element_type=F32)


def _mm_tn(a, b, prec=None):
    return lax.dot_general(a, b, (((0,), (0,)), ((), ())), precision=prec, preferred_element_type=F32)


def _bf(x):
    return x.astype(BF16)


def _sigmoid(x):
    return 1.0 / (1.0 + jnp.exp(-x))


def _softplus(x):
    return jnp.maximum(x, 0.0) + jnp.log1p(jnp.exp(-jnp.abs(x)))


def _log_sigmoid(x):
    return jnp.minimum(x, 0.0) - jnp.log1p(jnp.exp(-jnp.abs(x)))


def _mod_row(tile, tm):
    npt = NP_ROWS // tm
    per = DEC_SEQ // tm
    return jnp.where(tile < npt, 0, 1 + (tile - npt) // per)


def _mod_spec(layer, chunk, tm):
    def imap(i, *_):
        return ((layer * N_MOD_ROWS + _mod_row(i, tm)) * 6 + chunk, 0, 0)
    return pl.BlockSpec((1, 1, D), imap)


def _modulation_kernel(c_ref, w_ref, b_ref, o_ref):
    x = c_ref[...]
    s = x * _sigmoid(x)
    o_ref[0] = _mm(s, w_ref[0], HI) + b_ref[0]


def _modulation(cvecs, w_mod, b_mod):
    out = pl.pallas_call(
        _modulation_kernel,
        out_shape=jax.ShapeDtypeStruct((DEPTH, N_MOD_ROWS, 6 * D), F32),
        grid=(DEPTH, 6),
        in_specs=[pl.BlockSpec((N_MOD_ROWS, D), lambda l, j: (0, 0)),
                  pl.BlockSpec((1, D, D), lambda l, j: (l, 0, j)),
                  pl.BlockSpec((1, 1, D), lambda l, j: (l * 6 + j, 0, 0))],
        out_specs=pl.BlockSpec((1, N_MOD_ROWS, D), lambda l, j: (l, 0, j)),
        compiler_params=_params(2),
        name="modulation",
    )(cvecs, w_mod, b_mod.reshape(DEPTH * 6, 1, D))
    return out.reshape(DEPTH * N_MOD_ROWS * 6, 1, D)


def _proj_kernel(x_ref, sh_ref, sc_ref, w_ref, o_ref):
    xm = x_ref[...] * (1.0 + sc_ref[0]) + sh_ref[0]
    o_ref[...] = _mm(_bf(xm), w_ref[...])


def _even_proj(x, mods, layer, w):
    tm = 256
    return pl.pallas_call(
        _proj_kernel,
        out_shape=jax.ShapeDtypeStruct((T, EVEN_W), F32),
        grid=(T // tm,),
        in_specs=[pl.BlockSpec((tm, D), lambda i: (i, 0)),
                  _mod_spec(layer, 0, tm), _mod_spec(layer, 1, tm),
                  pl.BlockSpec((D, EVEN_W), lambda i: (0, 0))],
        out_specs=pl.BlockSpec((tm, EVEN_W), lambda i: (i, 0)),
        compiler_params=_params(1),
        name="even_proj",
    )(x, mods, mods, w)


def _qkv_kernel(x_ref, sh_ref, sc_ref, w_ref, cos_ref, sa_ref, sb_ref, o_ref, *, tm):
    i = pl.program_id(0)
    xm = x_ref[...] * (1.0 + sc_ref[0]) + sh_ref[0]
    acc = _mm(_bf(xm), w_ref[...])
    n_q = HC * HD // 128
    n_k = KVH * HD // 128
    is_latent = i >= NP_ROWS // tm
    cos = jnp.where(is_latent, cos_ref[...], 1.0)
    sa = jnp.where(is_latent, sa_ref[...], 0.0)
    sb = jnp.where(is_latent, sb_ref[...], 0.0)
    for g in range(n_q + n_k):
        blk = acc[:, g * 128:(g + 1) * 128]
        if g < n_q:
            blk = blk * (HD ** -0.5)
        rot = blk * cos + pltpu.roll(blk, 112, 1) * sa + pltpu.roll(blk, 16, 1) * sb
        o_ref[:, g * 128:(g + 1) * 128] = rot
    o_ref[:, (n_q + n_k) * 128:] = acc[:, (n_q + n_k) * 128:]


def _rope_tables():
    half = HD // 4
    inv = ROPE_THETA ** (-jnp.arange(half, dtype=F32) / half)
    pos = jnp.arange(DEC_SEQ)
    row = (pos // GRID_W).astype(F32)[:, None] * inv[None, :]
    col = (pos % GRID_W).astype(F32)[:, None] * inv[None, :]
    cos = jnp.concatenate([jnp.cos(row), jnp.cos(row), jnp.cos(col), jnp.cos(col)], axis=-1)
    sin = jnp.concatenate([jnp.sin(row), jnp.sin(row), jnp.sin(col), jnp.sin(col)], axis=-1)
    first = (jnp.arange(HD) % 32) < 16
    sa = jnp.where(first, -sin, 0.0)
    sb = jnp.where(first, 0.0, sin)
    tile2 = lambda t: jnp.concatenate([t, t], axis=-1)
    return tile2(cos), tile2(sa), tile2(sb)


def _odd_proj(x, mods, layer, w, tables):
    tm = 512
    npt = NP_ROWS // tm
    per = DEC_SEQ // tm
    tab_spec = pl.BlockSpec((tm, 128), lambda i: (jnp.where(i < npt, 0, (i - npt) % per), 0))
    return pl.pallas_call(
        functools.partial(_qkv_kernel, tm=tm),
        out_shape=jax.ShapeDtypeStruct((T, QKV_W), F32),
        grid=(T // tm,),
        in_specs=[pl.BlockSpec((tm, D), lambda i: (i, 0)),
                  _mod_spec(layer, 0, tm), _mod_spec(layer, 1, tm),
                  pl.BlockSpec((D, QKV_W), lambda i: (0, 0)),
                  tab_spec, tab_spec, tab_spec],
        out_specs=pl.BlockSpec((tm, QKV_W), lambda i: (i, 0)),
        compiler_params=_params(1),
        name="odd_qkv_proj",
    )(x, mods, mods, w, *tables)


PREP_ROWS = 256


def _delta_prep_kernel(x_ref, prev_ref, next_ref, w_ref, o_ref):
    i = pl.program_id(0)
    npb = NP_ROWS // PREP_ROWS
    per = DEC_SEQ // PREP_ROWS
    is_latent = i >= npb
    pos = (i - npb) % per
    has_prev = jnp.logical_and(is_latent, pos > 0)
    has_next = jnp.logical_and(is_latent, pos < per - 1)
    x = x_ref[...]
    w = w_ref[...]
    rows = lax.broadcasted_iota(jnp.int32, x.shape, 0)
    prev_row = jnp.where(has_prev, prev_ref[7:8, :], 0.0)
    next_row = jnp.where(has_next, next_ref[0:1, :], 0.0)
    xm1 = jnp.where(rows == 0, prev_row, pltpu.roll(x, 1, 0))
    xp1 = jnp.where(rows == PREP_ROWS - 1, next_row, pltpu.roll(x, PREP_ROWS - 1, 0))
    y = xm1 * w[0:1, :] + x * w[1:2, :] + xp1 * w[2:3, :]
    y = y * _sigmoid(y)
    for h in range(3 * HB):
        yh = y[:, h * DH:(h + 1) * DH]
        if h < 2 * HB:
            inv = lax.rsqrt(jnp.sum(yh * yh, axis=-1, keepdims=True) + 1e-6)
            yh = yh * (inv * (DH ** -0.5) if h < HB else inv)
        o_ref[:, h * DH:(h + 1) * DH] = yh


def _delta_prep(proj, conv_w):
    nblk = T // PREP_ROWS
    sub = PREP_ROWS // 8
    last8 = T // 8 - 1
    return pl.pallas_call(
        _delta_prep_kernel,
        out_shape=jax.ShapeDtypeStruct((T, 3 * 512), F32),
        grid=(nblk,),
        in_specs=[pl.BlockSpec((PREP_ROWS, 3 * 512), lambda i: (i, 0)),
                  pl.BlockSpec((8, 3 * 512), lambda i: (jnp.maximum(i * sub - 1, 0), 0)),
                  pl.BlockSpec((8, 3 * 512), lambda i: (jnp.minimum((i + 1) * sub, last8), 0)),
                  pl.BlockSpec((3, 3 * 512), lambda i: (0, 0))],
        out_specs=pl.BlockSpec((PREP_ROWS, 3 * 512), lambda i: (i, 0)),
        compiler_params=_params(1),
        name="delta_prep",
    )(proj, proj, proj, conv_w)


assert P_STEPS == S_STEPS
SCAN_STEPS = P_STEPS


def _bwd_local(s, nc):
    return (s // nc) * nc + nc - 1 - s % nc


def _scan_blocks():
    return (lambda s: s, lambda s: _bwd_local(s, P_CHUNKS),
            lambda s: P_BLOCKS + s, lambda s: P_BLOCKS + _bwd_local(s, S_CHUNKS))


def _scan_specs(xcol):
    gcol = EVEN_W // 128 - 1
    blocks = _scan_blocks()
    return ([pl.BlockSpec((CHUNK, 3 * 512), lambda s, f=f: (f(s), xcol)) for f in blocks]
            + [pl.BlockSpec((CHUNK, 128), lambda s, f=f: (f(s), gcol)) for f in blocks]
            + [pl.BlockSpec((1, 16, 128), lambda s, f=f: (f(s), 0, 0)) for f in blocks])


def _scan_out_specs():
    local = (lambda s: s, lambda s: _bwd_local(s, P_CHUNKS), lambda s: s, lambda s: _bwd_local(s, S_CHUNKS))
    return [pl.BlockSpec((CHUNK, 512), lambda s, f=f: (f(s), 0)) for f in local]


def _mlstm_kernel(xpf_ref, xpb_ref, xsf_ref, xsb_ref, gpf_ref, gpb_ref, gsf_ref, gsb_ref,
                  rpf_ref, rpb_ref, rsf_ref, rsb_ref,
                  brow_ref, b2_ref, c0_ref, n0_ref, m0_ref,
                  hpf_ref, hpb_ref, hsf_ref, hsb_ref, c_out, n_out, m_out,
                  cp_s, np_s, mp_s, cs_s, ns_s, ms_s):
    s = pl.program_id(0)
    jp = s % P_CHUNKS
    js = s % S_CHUNKS

    @pl.when(jp == 0)
    def _():
        cp_s[...] = jnp.zeros_like(cp_s)
        np_s[...] = jnp.zeros_like(np_s)
        mp_s[...] = jnp.zeros_like(mp_s)

    @pl.when(js == 0)
    def _():
        for d in range(2):
            for h in range(HA):
                cs_s[d * HA + h] = c0_ref[0, 0, d, h]
                ns_s[d * HA + h] = jnp.broadcast_to(n0_ref[0, 0, d, h:h + 1, :], (DH, 128)).T
        ms_s[...] = m0_ref[0, 0]

    states = ((cp_s, np_s, mp_s), (cs_s, ns_s, ms_s))

    row = lax.broadcasted_iota(jnp.int32, (CHUNK, 128), 0)
    lane = lax.broadcasted_iota(jnp.int32, (CHUNK, 128), 1)
    left = lane < CHUNK
    lcol = jnp.where(left, lane, lane - CHUNK)
    r64 = lax.broadcasted_iota(jnp.int32, (CHUNK, CHUNK), 0)
    c64 = lax.broadcasted_iota(jnp.int32, (CHUNK, CHUNK), 1)
    r128 = lax.broadcasted_iota(jnp.int32, (128, 128), 0)
    c128 = lax.broadcasted_iota(jnp.int32, (128, 128), 1)
    same_half = (r128 < CHUNK) == (c128 < CHUNK)
    neg = -jnp.inf
    ins = ((xpf_ref, gpf_ref, rpf_ref, hpf_ref), (xpb_ref, gpb_ref, rpb_ref, hpb_ref),
           (xsf_ref, gsf_ref, rsf_ref, hsf_ref), (xsb_ref, gsb_ref, rsb_ref, hsb_ref))

    sel_r = lax.broadcasted_iota(jnp.int32, (128, HA * 128), 0)
    sel_h = lax.broadcasted_iota(jnp.int32, (128, HA * 128), 1) // 128
    ones_lr = lax.broadcasted_iota(jnp.int32, (256, 256), 0)
    ones_lc = lax.broadcasted_iota(jnp.int32, (256, 256), 1)
    half_sum = (((ones_lr % 128) < CHUNK) == (ones_lc < 128)).astype(BF16)
    ones_l = jnp.ones((CHUNK, 128), BF16)
    rows256 = lax.broadcasted_iota(jnp.int32, (CHUNK, 256), 0)

    def replicate(x, sel):
        return _mm(jnp.concatenate(_split3(x), axis=1), jnp.concatenate([sel, sel, sel], axis=0))

    def running_max(x, d):
        sh = 1
        while sh < CHUNK:
            if d == 0:
                x = jnp.maximum(x, jnp.where(rows256 >= sh, pltpu.roll(x, sh, 0), neg))
            else:
                x = jnp.maximum(x, jnp.where(rows256 < CHUNK - sh, pltpu.roll(x, CHUNK - sh, 0), neg))
            sh *= 2
        return x

    gate = []
    for u in range(4):
        d = u % 2
        gc_ref, gr_ref = ins[u][1], ins[u][2]
        if d == 0:
            incl_p, tri_c, tri_r = lcol <= row, c64 <= r64, jnp.logical_and(same_half, r128 <= c128)
        else:
            incl_p, tri_c, tri_r = lcol >= row, c64 >= r64, jnp.logical_and(same_half, r128 >= c128)
        gc = gc_ref[...] + brow_ref[...]
        gr = gr_ref[0] + b2_ref[...]
        i_rep = replicate(gc, (sel_r == d * 8 + sel_h).astype(BF16))
        f_rep = replicate(_log_sigmoid(gc), (sel_r == d * 8 + 4 + sel_h).astype(BF16))
        b_rep = _prefix_cols(tri_c.astype(BF16), f_rep)
        cs_row = _prefix_rows(_log_sigmoid(gr), tri_r.astype(BF16))
        gate.append((incl_p, gr, cs_row, i_rep, b_rep))

    st = []
    for u in range(4):
        d = u % 2
        c_s, n_s, m_s = states[u // 2]
        for hp in range(HA // 2):
            x_ref = ins[u][0]
            incl_p, gr, cs_row, i_rep, b_rep = gate[u]
            end = CHUNK - 1 if d == 0 else 0
            heads = []
            for h in (2 * hp, 2 * hp + 1):
                c = d * HA + h
                heads.append(dict(c=c, h=h, q=x_ref[:, h * DH:(h + 1) * DH],
                                  k=x_ref[:, 512 + h * DH:512 + (h + 1) * DH] * (DH ** -0.5),
                                  v=x_ref[:, 1024 + h * DH:1024 + (h + 1) * DH],
                                  i=i_rep[:, h * 128:(h + 1) * 128], b=b_rep[:, h * 128:(h + 1) * 128],
                                  m=m_s[c:c + 1, :]))
            ha, hb_ = heads
            i_row = gr[d * 4 + hp:d * 4 + hp + 1, :]
            b_row = cs_row[d * 4 + 2 + hp:d * 4 + 3 + hp, :]
            run = running_max(jnp.concatenate([ha["i"] - ha["b"], hb_["i"] - hb_["b"]], axis=1), d)
            for idx, hd in enumerate(heads):
                top = jnp.maximum(hd["m"], run[:, idx * 128:(idx + 1) * 128])
                hd["m_t"] = hd["b"] + top
                hd["w_inter"] = jnp.exp(hd["m"] - top)
            b_p = jnp.where(left, ha["b"], hb_["b"])
            m_t_p = jnp.where(left, ha["m_t"], hb_["m_t"])
            dmat = jnp.where(incl_p, b_p - b_row + i_row, neg)
            q_cat = _bf(jnp.concatenate([ha["q"], hb_["q"]], axis=1))
            k_bd = _bf(_block_diag2(ha["k"], hb_["k"]))
            st.append(dict(u=u, d=d, heads=heads, sc=_mm_nt(q_cat, k_bd) * jnp.exp(dmat - m_t_p)))

    for p in st:
        ha, hb_ = p["heads"]
        h_ref = ins[p["u"]][3]
        c_s, n_s, m_s = states[p["u"] // 2]
        end = CHUNK - 1 if p["d"] == 0 else 0
        sc = p["sc"]
        sv = _mm(_bf(sc), _bf(_block_diag2(ha["v"], hb_["v"])))
        dens = _mm(jnp.concatenate(_split2(sc), axis=1), half_sum)
        for idx, hd in enumerate((ha, hb_)):
            c, h, q, k = hd["c"], hd["h"], hd["q"], hd["k"]
            c_mat = c_s[c]
            n_mat = n_s[c]
            qb = _bf(q)
            num = sv[:, idx * DH:(idx + 1) * DH] + hd["w_inter"] * _mm(qb, _bf(c_mat))
            den = dens[:, idx * 128:(idx + 1) * 128] + hd["w_inter"] * _mm(qb, _bf(n_mat))
            h_ref[:, h * DH:(h + 1) * DH] = num / jnp.maximum(jnp.abs(den), jnp.exp(-hd["m_t"]))
            b_last = hd["b"][end:end + 1, :]
            g_end = b_last - hd["b"] + hd["i"]
            m_new = jnp.maximum(b_last + hd["m"], jnp.max(g_end, axis=0, keepdims=True))
            kwb = _bf(k * jnp.exp(g_end - m_new))
            decay = jnp.exp(b_last + hd["m"] - m_new)
            c_s[c] = decay * c_mat + _mm_tn(kwb, _bf(hd["v"]))
            n_s[c] = decay * n_mat + _mm_tn(kwb, ones_l)
            m_s[c:c + 1, :] = m_new

    @pl.when(jp == P_CHUNKS - 1)
    def _():
        for d in range(2):
            for h in range(HA):
                c_out[0, d, h] = cp_s[d * HA + h]
                n_out[0, d, h:h + 1, :] = np_s[d * HA + h].T[0:1, :]
        m_out[0] = mp_s[...]


def _mlstm_scan(proj, gates_t2, bias_row, bias2, c0, n0, m0, e):
    const2 = lambda s: (0, 0)
    in_specs = (_scan_specs(1)
                + [pl.BlockSpec((1, 128), const2), pl.BlockSpec((16, 128), const2),
                   pl.BlockSpec((1, 1, 2, HA, DH, DH), lambda s: (s // S_CHUNKS, e, 0, 0, 0, 0)),
                   pl.BlockSpec((1, 1, 2, HA, DH), lambda s: (s // S_CHUNKS, e, 0, 0, 0)),
                   pl.BlockSpec((1, 1, 2 * HA, 128), lambda s: (s // S_CHUNKS, e, 0, 0))])
    half = jax.ShapeDtypeStruct((NP_ROWS, HA * DH), F32)
    out_shape = (half, half, half, half,
                 jax.ShapeDtypeStruct((BATCH, 2, HA, DH, DH), F32),
                 jax.ShapeDtypeStruct((BATCH, 2, HA, DH), F32),
                 jax.ShapeDtypeStruct((BATCH, 2 * HA, 128), F32))
    out_specs = _scan_out_specs() + [
        pl.BlockSpec((1, 2, HA, DH, DH), lambda s: (s // P_CHUNKS, 0, 0, 0, 0)),
        pl.BlockSpec((1, 2, HA, DH), lambda s: (s // P_CHUNKS, 0, 0, 0)),
        pl.BlockSpec((1, 2 * HA, 128), lambda s: (s // P_CHUNKS, 0, 0))]
    state = [pltpu.VMEM((2 * HA, DH, DH), F32), pltpu.VMEM((2 * HA, DH, 128), F32), pltpu.VMEM((2 * HA, 128), F32)]
    return pl.pallas_call(
        _mlstm_kernel,
        out_shape=out_shape,
        grid=(SCAN_STEPS,),
        in_specs=in_specs,
        out_specs=out_specs,
        scratch_shapes=state + state,
        compiler_params=_params(1),
        name="mlstm_scan",
    )(proj, proj, proj, proj, proj, proj, proj, proj, gates_t2, gates_t2, gates_t2, gates_t2,
      bias_row, bias2, c0, n0, m0)


def _split2(x):
    hi = _bf(x)
    return hi, _bf(x - hi.astype(F32))


def _split3(x):
    h1 = _bf(x)
    r1 = x - h1.astype(F32)
    h2 = _bf(r1)
    return h1, h2, _bf(r1 - h2.astype(F32))


def _lhs3(x):
    hi, lo = _split2(x)
    return jnp.concatenate([hi, lo, hi], axis=1)


def _rhs3(x):
    hi, lo = _split2(x)
    return jnp.concatenate([hi, hi, lo], axis=0)


def _prefix_cols(tri_bf, x):
    n = x.shape[1]
    r = _mm(tri_bf, jnp.concatenate(_split3(x), axis=1))
    return r[:, :n] + r[:, n:2 * n] + r[:, 2 * n:]


def _prefix_rows(x, tri_bf):
    m = x.shape[0]
    r = _mm(jnp.concatenate(_split3(x), axis=0), tri_bf)
    return r[:m] + r[m:2 * m] + r[2 * m:]


LEVELS = tuple(range(6))


def _block_diag2(a, b):
    z = jnp.zeros_like(a)
    return jnp.concatenate([jnp.concatenate([a, z], axis=1), jnp.concatenate([z, b], axis=1)], axis=0)


def _delta_kernel(xpf_ref, xpb_ref, xsf_ref, xsb_ref, gpf_ref, gpb_ref, gsf_ref, gsb_ref,
                  rpf_ref, rpb_ref, rsf_ref, rsb_ref,
                  dtrow_ref, dt2_ref, narow_ref, na2_ref, s0_ref,
                  opf_ref, opb_ref, osf_ref, osb_ref, s_out, sp_s, ss_s):
    s = pl.program_id(0)
    jp = s % P_CHUNKS
    js = s % S_CHUNKS

    @pl.when(jp == 0)
    def _():
        sp_s[...] = jnp.zeros_like(sp_s)

    @pl.when(js == 0)
    def _():
        for d in range(2):
            for h in range(HB):
                ss_s[d * HB + h] = s0_ref[0, 0, d, h]

    states = (sp_s, ss_s)

    row = lax.broadcasted_iota(jnp.int32, (CHUNK, 128), 0)
    lane = lax.broadcasted_iota(jnp.int32, (CHUNK, 128), 1)
    left = lane < CHUNK
    lcol = jnp.where(left, lane, lane - CHUNK)
    eye_p = (lcol == row).astype(F32)
    r64 = lax.broadcasted_iota(jnp.int32, (CHUNK, CHUNK), 0)
    c64 = lax.broadcasted_iota(jnp.int32, (CHUNK, CHUNK), 1)
    r128 = lax.broadcasted_iota(jnp.int32, (128, 128), 0)
    c128 = lax.broadcasted_iota(jnp.int32, (128, 128), 1)
    same_half = (r128 < CHUNK) == (c128 < CHUNK)
    ins = ((xpf_ref, gpf_ref, rpf_ref, opf_ref), (xpb_ref, gpb_ref, rpb_ref, opb_ref),
           (xsf_ref, gsf_ref, rsf_ref, osf_ref), (xsb_ref, gsb_ref, rsb_ref, osb_ref))

    gate = []
    for u in range(4):
        d = u % 2
        gc_ref, gr_ref = ins[u][1], ins[u][2]
        if d == 0:
            incl_p, strict_p = lcol <= row, lcol < row
            tri_c, tri_r = c64 <= r64, jnp.logical_and(same_half, r128 <= c128)
        else:
            incl_p, strict_p = lcol >= row, lcol > row
            tri_c, tri_r = c64 >= r64, jnp.logical_and(same_half, r128 >= c128)
        xc = gc_ref[...]
        xr = gr_ref[0]
        la_c = narow_ref[...] * _softplus(xc + dtrow_ref[...])
        la_r = na2_ref[...] * _softplus(xr + dt2_ref[...])
        g_c = _prefix_cols(tri_c.astype(BF16), la_c)
        g_r = _prefix_rows(la_r, tri_r.astype(BF16))
        gate.append((incl_p, strict_p, _sigmoid(xc), g_c, g_r))

    def bd_rhs(x):
        return _rhs3(jnp.concatenate([jnp.where(left, x, 0.0), jnp.where(left, 0.0, x)], axis=0))

    off = ([], [])
    for lv in LEVELS:
        same = jnp.right_shift(row, lv + 1) == jnp.right_shift(lcol, lv + 1)
        r_hi = jnp.bitwise_and(jnp.right_shift(row, lv), 1) == 1
        c_hi = jnp.bitwise_and(jnp.right_shift(lcol, lv), 1) == 1
        off[0].append(same & r_hi & jnp.logical_not(c_hi))
        off[1].append(same & jnp.logical_not(r_hi) & c_hi)

    pairs = [(u, hp) for u in range(4) for hp in range(HB // 2)]
    st = []
    for u, hp in pairs:
        d = u % 2
        x_ref = ins[u][0]
        incl_p, strict_p, beta_c, g_c, g_r = gate[u]
        end = CHUNK - 1 if d == 0 else 0
        heads = []
        for h in (2 * hp, 2 * hp + 1):
            ib = 16 + d * HB + h
            ia = 24 + d * HB + h
            gcol = g_c[:, ia:ia + 1]
            bc = beta_c[:, ib:ib + 1]
            q = x_ref[:, h * DH:(h + 1) * DH]
            k = x_ref[:, 512 + h * DH:512 + (h + 1) * DH]
            v = x_ref[:, 1024 + h * DH:1024 + (h + 1) * DH]
            heads.append(dict(h=h, gcol=gcol, bc=bc, q=q, k=k, v=v, kb=k * bc, eg=jnp.exp(gcol),
                              g_last=gcol[end:end + 1, :]))
        ha, hb_ = heads
        r = 12 + d * 2 + hp
        gcol_p = jnp.where(left, ha["gcol"], hb_["gcol"])
        decay = jnp.exp(jnp.where(incl_p, gcol_p - g_r[r:r + 1, :], -jnp.inf))
        k_bd = _bf(_block_diag2(ha["k"], hb_["k"]))
        kb_cat = _bf(jnp.concatenate([ha["kb"], hb_["kb"]], axis=1))
        q_cat = _bf(jnp.concatenate([ha["q"], hb_["q"]], axis=1))
        a_mat = jnp.where(strict_p, _mm_nt(kb_cat, k_bd) * decay, 0.0)
        qk = _mm_nt(q_cat, k_bd) * decay
        st.append(dict(u=u, d=d, heads=heads, t=eye_p - jnp.where(off[d][0], a_mat, 0.0), qk=qk,
                       am=[bd_rhs(jnp.where(m, a_mat, 0.0)) for m in off[d][1:]]))

    for li in range(len(LEVELS) - 1):
        for p in st:
            p["w"] = _mm(_lhs3(p["t"]), p["am"][li])
        for p in st:
            p["t"] = p["t"] - _mm(_lhs3(p["w"]), bd_rhs(p["t"]))

    for p in st:
        ha, hb_ = p["heads"]
        o_ref = ins[p["u"]][3]
        s_s = states[p["u"] // 2]
        rhs_a = jnp.concatenate([ha["v"] * ha["bc"], ha["kb"] * ha["eg"]], axis=1)
        rhs_b = jnp.concatenate([hb_["v"] * hb_["bc"], hb_["kb"] * hb_["eg"]], axis=1)
        sol = _mm(_lhs3(p["t"]), _rhs3(_block_diag2(rhs_a, rhs_b)))
        vn = []
        for idx, hd in enumerate((ha, hb_)):
            c = p["d"] * HB + hd["h"]
            s_mat = s_s[c]
            sbf = _bf(s_mat)
            so = sol[:, idx * 2 * DH:(idx + 1) * 2 * DH]
            v_new = so[:, :DH] - _mm(_bf(so[:, DH:]), sbf)
            vn.append(v_new)
            hd["o1"] = _mm(_bf(hd["q"] * hd["eg"]), sbf)
            s_s[c] = (jnp.exp(hd["g_last"]) * s_mat
                      + _mm_tn(_bf(hd["k"] * jnp.exp(hd["g_last"] - hd["gcol"])), _bf(v_new)))
        o2 = _mm(_bf(p["qk"]), _bf(_block_diag2(vn[0], vn[1])))
        for idx, hd in enumerate((ha, hb_)):
            o_ref[:, hd["h"] * DH:(hd["h"] + 1) * DH] = hd["o1"] + o2[:, idx * DH:(idx + 1) * DH]

    @pl.when(jp == P_CHUNKS - 1)
    def _():
        for d in range(2):
            for h in range(HB):
                s_out[0, d, h] = sp_s[d * HB + h]


def _delta_scan(qkv, proj, gates_t2, dt_row, dt2, na_row, na2, s0, e):
    const2 = lambda s: (0, 0)
    in_specs = (_scan_specs(0)
                + [pl.BlockSpec((1, 128), const2), pl.BlockSpec((16, 128), const2),
                   pl.BlockSpec((1, 128), const2), pl.BlockSpec((16, 128), const2),
                   pl.BlockSpec((1, 1, 2, HB, DH, DH), lambda s: (s // S_CHUNKS, e, 0, 0, 0, 0))])
    half = jax.ShapeDtypeStruct((NP_ROWS, HB * DH), F32)
    out_shape = (half, half, half, half, jax.ShapeDtypeStruct((BATCH, 2, HB, DH, DH), F32))
    out_specs = _scan_out_specs() + [pl.BlockSpec((1, 2, HB, DH, DH), lambda s: (s // P_CHUNKS, 0, 0, 0, 0))]
    return pl.pallas_call(
        _delta_kernel,
        out_shape=out_shape,
        grid=(SCAN_STEPS,),
        in_specs=in_specs,
        out_specs=out_specs,
        scratch_shapes=[pltpu.VMEM((2 * HB, DH, DH), F32), pltpu.VMEM((2 * HB, DH, DH), F32)],
        compiler_params=_params(1),
        name="delta_scan",
    )(qkv, qkv, qkv, qkv, proj, proj, proj, proj, gates_t2, gates_t2, gates_t2, gates_t2,
      dt_row, dt2, na_row, na2, s0)


def _residual_ln(x, gate, y, g, b):
    r = DN_ALPHA * x + gate * y
    mu = jnp.mean(r, axis=-1, keepdims=True)
    var = jnp.mean(jnp.square(r - mu), axis=-1, keepdims=True)
    return (r - mu) * lax.rsqrt(var + LN_EPS) * g + b


def _even_out_kernel(hpf_ref, hpb_ref, hsf_ref, hsb_ref, opf_ref, opb_ref, osf_ref, osb_ref,
                     oa_ref, zb_ref, mg_ref, dg_ref, w_ref, x_ref, gate_ref, g_ref, b_ref, o_ref, *, tm):
    is_p = pl.program_id(0) < NP_ROWS // tm
    hf_ref, hb_ref, of_ref, ob_ref = (
        lambda sl, p=p, q=q: jnp.where(is_p, p[:, sl], q[:, sl])
        for p, q in ((hpf_ref, hsf_ref), (hpb_ref, hsb_ref), (opf_ref, osf_ref), (opb_ref, osb_ref)))
    parts = []
    for h in range(HA):
        sl = slice(h * DH, (h + 1) * DH)
        hh = hf_ref(sl) + hb_ref(sl)
        mu = jnp.mean(hh, axis=-1, keepdims=True)
        var = jnp.mean(jnp.square(hh - mu), axis=-1, keepdims=True)
        parts.append(_sigmoid(oa_ref[:, sl]) * ((hh - mu) * lax.rsqrt(var + LN_EPS) * mg_ref[:, sl]))
    for h in range(HB):
        sl = slice(h * DH, (h + 1) * DH)
        oo = of_ref(sl) + ob_ref(sl)
        z = zb_ref[:, sl]
        nrm = oo * lax.rsqrt(jnp.mean(jnp.square(oo), axis=-1, keepdims=True) + LN_EPS) * dg_ref[:, sl]
        parts.append(nrm * (z * _sigmoid(z)))
    a = jnp.concatenate(parts, axis=1)
    y = _mm(_bf(a), w_ref[...])
    o_ref[...] = _residual_ln(x_ref[...], gate_ref[0], y, g_ref[...], b_ref[...])


def _even_out(h_scan, o_scan, proj, mg, dg, w, x, mods, layer, ln_g, ln_b):
    tm = 512
    npt = NP_ROWS // tm
    row512 = lambda i: (i, 0)
    const2 = lambda i: (0, 0)
    p_spec = pl.BlockSpec((tm, 512), lambda i: (jnp.minimum(i, npt - 1), 0))
    s_spec = pl.BlockSpec((tm, 512), lambda i: (jnp.maximum(i - npt, 0), 0))
    return pl.pallas_call(
        functools.partial(_even_out_kernel, tm=tm),
        out_shape=jax.ShapeDtypeStruct((T, D), F32),
        grid=(T // tm,),
        in_specs=[p_spec, p_spec, s_spec, s_spec, p_spec, p_spec, s_spec, s_spec,
                  pl.BlockSpec((tm, 512), lambda i: (i, 6)),
                  pl.BlockSpec((tm, 512), lambda i: (i, 7)),
                  pl.BlockSpec((1, 512), const2), pl.BlockSpec((1, 512), const2),
                  pl.BlockSpec((D, D), const2),
                  pl.BlockSpec((tm, D), row512),
                  _mod_spec(layer, 2, tm),
                  pl.BlockSpec((1, D), const2), pl.BlockSpec((1, D), const2)],
        out_specs=pl.BlockSpec((tm, D), row512),
        compiler_params=_params(1),
        name="even_out_ln",
    )(*h_scan, *o_scan, proj, proj, mg, dg, w, x, mods, ln_g, ln_b)


def _odd_out_kernel(ap_ref, as_ref, w_ref, x_ref, gate_ref, g_ref, b_ref, o_ref, *, tm):
    a = jnp.where(pl.program_id(0) < NP_ROWS // tm, ap_ref[...], as_ref[...])
    y = _mm(_bf(a), w_ref[...])
    o_ref[...] = _residual_ln(x_ref[...], gate_ref[0], y, g_ref[...], b_ref[...])


def _odd_out(a_prompt, a_latent, w, x, mods, layer, ln_g, ln_b):
    tm = 512
    npt = NP_ROWS // tm
    row = lambda i: (i, 0)
    const2 = lambda i: (0, 0)
    return pl.pallas_call(
        functools.partial(_odd_out_kernel, tm=tm),
        out_shape=jax.ShapeDtypeStruct((T, D), F32),
        grid=(T // tm,),
        in_specs=[pl.BlockSpec((tm, D), lambda i: (jnp.minimum(i, npt - 1), 0)),
                  pl.BlockSpec((tm, D), lambda i: (jnp.maximum(i - npt, 0), 0)),
                  pl.BlockSpec((D, D), const2), pl.BlockSpec((tm, D), row),
                  _mod_spec(layer, 2, tm), pl.BlockSpec((1, D), const2), pl.BlockSpec((1, D), const2)],
        out_specs=pl.BlockSpec((tm, D), row),
        compiler_params=_params(1),
        name="odd_out_ln",
    )(a_prompt, a_latent, w, x, mods, ln_g, ln_b)


def _attn_ctx_kernel(qkv_ref, sink_ref, o_ref):
    ones = jnp.ones((SEQ, HD), BF16)
    for kv in range(KVH):
        k = _bf(qkv_ref[:, HC * HD + kv * HD:HC * HD + (kv + 1) * HD])
        v = _bf(qkv_ref[:, (HC + KVH) * HD + kv * HD:(HC + KVH) * HD + (kv + 1) * HD])
        v1 = jnp.concatenate([v, ones], axis=1)
        heads = range(kv * (HC // KVH), (kv + 1) * (HC // KVH))
        sts = [_mm_nt(k, _bf(qkv_ref[:, h * HD:(h + 1) * HD])) for h in heads]
        ms = [jnp.maximum(jnp.max(st, axis=0, keepdims=True), sink_ref[h]) for st, h in zip(sts, heads)]
        ovs = [_mm_tn(v1, _bf(jnp.exp(st - m))) for st, m in zip(sts, ms)]
        outs = [ov[:HD] / (ov[HD:HD + 1] + jnp.exp(sink_ref[h] - m)) for ov, m, h in zip(ovs, ms, heads)]
        for pr in range(HC // KVH // 2):
            lane0 = (kv * (HC // KVH) + 2 * pr) * HD
            o_ref[:, lane0:lane0 + 2 * HD] = jnp.concatenate(outs[2 * pr:2 * pr + 2], axis=0).T


def _attn_context(qkv, sink):
    return pl.pallas_call(
        _attn_ctx_kernel,
        out_shape=jax.ShapeDtypeStruct((NP_ROWS, HC * HD), F32),
        grid=(BATCH,),
        in_specs=[pl.BlockSpec((SEQ, QKV_W), lambda b: (b, 0)),
                  pl.BlockSpec(memory_space=pltpu.SMEM)],
        out_specs=pl.BlockSpec((SEQ, HC * HD), lambda b: (b, 0)),
        compiler_params=_params(1),
        name="attn_context",
    )(qkv, sink)


def _attn_lat_kernel(q_ref, kp_ref, kc_ref, kn_ref, vp_ref, vc_ref, vn_ref, ck_ref, cv_ref,
                     sink_ref, o_ref, bias_s):
    j = pl.program_id(1)
    nb = DEC_SEQ // QBLOCK
    cc = lax.broadcasted_iota(jnp.int32, (3 * QBLOCK, QBLOCK), 0)
    r = lax.broadcasted_iota(jnp.int32, (3 * QBLOCK, QBLOCK), 1)
    lo = jnp.where(j >= 1, 0, QBLOCK)
    hi = jnp.where(j <= nb - 2, 3 * QBLOCK, 2 * QBLOCK)
    ok = (jnp.abs(QBLOCK + r - cc) <= WINDOW) & (cc >= lo) & (cc < hi)
    bias_s[...] = jnp.where(ok, 0.0, -jnp.inf)
    for kv in range(KVH):
        ks = slice(kv * HD, (kv + 1) * HD)
        k_all = _bf(jnp.concatenate([ck_ref[0, 0, kv], kp_ref[:, ks], kc_ref[:, ks], kn_ref[:, ks]], axis=0))
        v_all = _bf(jnp.concatenate([cv_ref[0, 0, kv], vp_ref[:, ks], vc_ref[:, ks], vn_ref[:, ks]], axis=0))
        v1 = jnp.concatenate([v_all, jnp.ones_like(v_all)], axis=1)
        outs = []
        for h in range(kv * (HC // KVH), (kv + 1) * (HC // KVH)):
            sink = sink_ref[h]
            st = _mm_nt(k_all, _bf(q_ref[:, h * HD:(h + 1) * HD]))
            s_ctx = st[:PAST_LEN]
            s_loc = st[PAST_LEN:] + bias_s[...]
            m = jnp.maximum(jnp.maximum(jnp.max(s_ctx, axis=0, keepdims=True),
                                        jnp.max(s_loc, axis=0, keepdims=True)), sink)
            p = _bf(jnp.concatenate([jnp.exp(s_ctx - m), jnp.exp(s_loc - m)], axis=0))
            ov = _mm_tn(v1, p)
            outs.append(ov[:HD] / (ov[HD:HD + 1] + jnp.exp(sink - m)))
        for pr in range(HC // KVH // 2):
            lane0 = (kv * (HC // KVH) + 2 * pr) * HD
            o_ref[:, lane0:lane0 + 2 * HD] = jnp.concatenate(outs[2 * pr:2 * pr + 2], axis=0).T


def _attn_latent(qkv, cache_k, cache_v, sink, o):
    nb = DEC_SEQ // QBLOCK
    base = NP_ROWS // QBLOCK
    blk = lambda b, j: base + b * nb + j
    prev = lambda b, j: base + b * nb + jnp.maximum(j - 1, 0)
    nxt = lambda b, j: base + b * nb + jnp.minimum(j + 1, nb - 1)
    kcol, vcol = HC * HD // 256, HC * HD // 256 + 1
    cache_spec = pl.BlockSpec((1, 1, KVH, PAST_LEN, HD), lambda b, j: (b, o, 0, 0, 0))
    return pl.pallas_call(
        _attn_lat_kernel,
        out_shape=jax.ShapeDtypeStruct((NS_ROWS, HC * HD), F32),
        grid=(DEC_BATCH, nb),
        in_specs=[pl.BlockSpec((QBLOCK, HC * HD), lambda b, j: (blk(b, j), 0)),
                  pl.BlockSpec((QBLOCK, 256), lambda b, j: (prev(b, j), kcol)),
                  pl.BlockSpec((QBLOCK, 256), lambda b, j: (blk(b, j), kcol)),
                  pl.BlockSpec((QBLOCK, 256), lambda b, j: (nxt(b, j), kcol)),
                  pl.BlockSpec((QBLOCK, 256), lambda b, j: (prev(b, j), vcol)),
                  pl.BlockSpec((QBLOCK, 256), lambda b, j: (blk(b, j), vcol)),
                  pl.BlockSpec((QBLOCK, 256), lambda b, j: (nxt(b, j), vcol)),
                  cache_spec, cache_spec,
                  pl.BlockSpec(memory_space=pltpu.SMEM)],
        out_specs=pl.BlockSpec((QBLOCK, HC * HD), lambda b, j: (b * nb + j, 0)),
        scratch_shapes=[pltpu.VMEM((3 * QBLOCK, QBLOCK), F32)],
        compiler_params=_params(2),
        name="attn_latent",
    )(qkv, qkv, qkv, qkv, qkv, qkv, qkv, cache_k, cache_v, sink)


def _route(lg):
    lane = lax.broadcasted_iota(jnp.int32, lg.shape, 1)
    neg = -jnp.inf
    big = 1 << 20
    is_grp = jnp.logical_and(lane >= N_EXPERTS, lane < N_EXPERTS + N_GROUPS)
    mg = jnp.max(jnp.where(is_grp, lg, neg), axis=1, keepdims=True)
    g_lane = jnp.min(jnp.where(jnp.logical_and(is_grp, lg == mg), lane, big), axis=1, keepdims=True)
    g_w = 1.0 / jnp.sum(jnp.where(is_grp, jnp.exp(lg - mg), 0.0), axis=1, keepdims=True)
    g_idx = g_lane - N_EXPERTS
    in_grp = jnp.logical_and(lane >= g_idx * EPG, lane < (g_idx + 1) * EPG)
    v1 = jnp.max(jnp.where(in_grp, lg, neg), axis=1, keepdims=True)
    i1 = jnp.min(jnp.where(jnp.logical_and(in_grp, lg == v1), lane, big), axis=1, keepdims=True)
    rest = jnp.logical_and(in_grp, lane != i1)
    v2 = jnp.max(jnp.where(rest, lg, neg), axis=1, keepdims=True)
    i2 = jnp.min(jnp.where(jnp.logical_and(rest, lg == v2), lane, big), axis=1, keepdims=True)
    e2 = jnp.exp(v2 - v1)
    p1 = 1.0 / (1.0 + e2)
    p2 = e2 / (1.0 + e2)
    return jnp.where(lane == 0, i1.astype(F32),
                     jnp.where(lane == 1, i2.astype(F32),
                               jnp.where(lane == 2, p1 * g_w, jnp.where(lane == 3, p2 * g_w, 0.0))))


def _moe_dense_kernel(x_ref, sh_ref, sc_ref, wr_ref, br_ref, wg_ref, wu_ref, wd_ref, gate_ref, g_ref, b_ref, o_ref,
                      xm_s, meta_s, acc_s):
    grp = pl.program_id(1)

    @pl.when(grp == 0)
    def _():
        xm = x_ref[...] * (1.0 + sc_ref[0]) + sh_ref[0]
        meta_s[...] = _route(_mm(_lhs3(xm), _rhs3(wr_ref[...])) + br_ref[...])
        xm_s[...] = _bf(xm)
        acc_s[...] = jnp.zeros_like(acc_s)

    xm = xm_s[...]
    meta = meta_s[...]
    i1 = meta[:, 0:1].astype(jnp.int32)
    i2 = meta[:, 1:2].astype(jnp.int32)
    w1 = meta[:, 2:3]
    w2 = meta[:, 3:4]
    hid = []
    for e in range(EPG):
        eid = grp * EPG + e
        gate = jnp.where(i1 == eid, w1, 0.0) + jnp.where(i2 == eid, w2, 0.0)
        a = _mm(xm, wg_ref[0, e])
        u = _mm(xm, wu_ref[0, e])
        hid.append(_bf((a * _sigmoid(a)) * u * gate))
    acc_s[...] += _mm(jnp.concatenate(hid, axis=1), wd_ref[0])

    @pl.when(grp == N_GROUPS - 1)
    def _():
        o_ref[...] = _residual_ln(x_ref[...], gate_ref[0], acc_s[...], g_ref[...], b_ref[...])


def _moe_dense(x, mods, layer, w_r, b_r, wg, wu, wd, ln_g, ln_b):
    tm = 1024
    row = lambda i, g: (i, 0)
    const2 = lambda i, g: (0, 0)
    return pl.pallas_call(
        _moe_dense_kernel,
        out_shape=jax.ShapeDtypeStruct((T, D), F32),
        grid=(T // tm, N_GROUPS),
        in_specs=[pl.BlockSpec((tm, D), row), _mod_spec(layer, 3, tm), _mod_spec(layer, 4, tm),
                  pl.BlockSpec((D, 128), const2), pl.BlockSpec((1, 128), const2),
                  pl.BlockSpec((1, EPG, D, EXPERT_FF), lambda i, g: (g, 0, 0, 0)),
                  pl.BlockSpec((1, EPG, D, EXPERT_FF), lambda i, g: (g, 0, 0, 0)),
                  pl.BlockSpec((1, EPG * EXPERT_FF, D), lambda i, g: (g, 0, 0)),
                  _mod_spec(layer, 5, tm), pl.BlockSpec((1, D), const2), pl.BlockSpec((1, D), const2)],
        out_specs=pl.BlockSpec((tm, D), row),
        scratch_shapes=[pltpu.VMEM((tm, D), BF16), pltpu.VMEM((tm, 128), F32), pltpu.VMEM((tm, D), F32)],
        compiler_params=_params(2),
        name="moe_dense",
    )(x, mods, mods, w_r, b_r, wg, wu, wd, mods, ln_g, ln_b)


def _permute_even_w(w):
    a_end = 4 * HA * DH
    g_end = a_end + 4 * HA
    c_end = g_end + 3 * HB * DH
    z_end = c_end + HB * DH
    small = jnp.concatenate([w[:, a_end:g_end], w[:, z_end:]], axis=1)
    pad = jnp.zeros((w.shape[0], 128 - small.shape[1]), w.dtype)
    return jnp.concatenate([w[:, g_end:c_end], w[:, :a_end], w[:, c_end:z_end], small, pad], axis=1)


def _lane_row(vals, offset):
    return jnp.zeros((1, 128), F32).at[0, offset:offset + vals.shape[0]].set(vals.astype(F32))


def _pair_rows(vals, offset):
    v32 = jnp.zeros((32,), F32).at[offset:offset + vals.shape[0]].set(vals.astype(F32))
    return jnp.repeat(v32.reshape(16, 2), CHUNK, axis=1)


def kernel(x_prompt, x_sample, c, c_ctx, state_mlstm_c, state_mlstm_n, state_mlstm_m, state_delta, cache_k, cache_v, w_mod, b_mod, ln_g, ln_b, w_in_even, mlstm_gate_b, mlstm_norm_g, delta_conv_w, delta_a_log, delta_dt_bias, delta_norm_g, w_out_even, w_qkv_odd, attn_sink, w_out_odd, w_grp, b_grp, w_erouter, b_erouter, w_gate, w_up, w_down):
    x = jnp.concatenate([x_prompt.reshape(NP_ROWS, D), x_sample.reshape(NS_ROWS, D)], axis=0)
    cvecs = jnp.concatenate([c_ctx[None, :], c, jnp.zeros((N_MOD_ROWS - 1 - DEC_BATCH, D), F32)], axis=0)
    mods = _modulation(cvecs, w_mod, b_mod)
    tables = _rope_tables()
    m0_all = jnp.broadcast_to(state_mlstm_m.reshape(DEC_BATCH, N_EVEN, 2 * HA, 1), (DEC_BATCH, N_EVEN, 2 * HA, 128))

    out_mc, out_mn, out_mm, out_ds, out_k, out_v = [], [], [], [], [], []
    for l in range(DEPTH):
        if l % 2 == 0:
            e = l // 2
            proj = _even_proj(x, mods, l, _bf(_permute_even_w(w_in_even[e])))
            gates_t2 = (proj[:, EVEN_W - 128:EVEN_W - 96].reshape(T // CHUNK, CHUNK, 32).transpose(0, 2, 1)
                        .reshape(T // CHUNK, 16, 128))
            gb = mlstm_gate_b[e].reshape(-1)
            *h_scan, mc, mn, mm = _mlstm_scan(proj, gates_t2, _lane_row(gb, 0), _pair_rows(gb, 0),
                                              state_mlstm_c, state_mlstm_n, m0_all, e)
            qkv = _delta_prep(proj, delta_conv_w[e])
            dtb = delta_dt_bias[e].reshape(-1)
            nea = -jnp.exp(delta_a_log[e].astype(F32)).reshape(-1)
            *o_scan, ds = _delta_scan(qkv, proj, gates_t2, _lane_row(dtb, 24), _pair_rows(dtb, 24),
                                      _lane_row(nea, 24), _pair_rows(nea, 24), state_delta, e)
            x = _even_out(h_scan, o_scan, proj, mlstm_norm_g[e][None, :], delta_norm_g[e][None, :],
                          _bf(w_out_even[e]), x, mods, l, ln_g[l, 0][None, :], ln_b[l, 0][None, :])
            out_mc.append(mc)
            out_mn.append(mn)
            out_mm.append(mm[:, :, 0].reshape(BATCH, 2, HA))
            out_ds.append(ds)
        else:
            o = l // 2
            qkv = _odd_proj(x, mods, l, _bf(w_qkv_odd[o]), tables)
            a_p = _attn_context(qkv, attn_sink[o])
            a_s = _attn_latent(qkv, cache_k, cache_v, attn_sink[o], o)
            x = _odd_out(a_p, a_s, _bf(w_out_odd[o]), x, mods, l, ln_g[l, 0][None, :], ln_b[l, 0][None, :])
            kp = qkv[:NP_ROWS, HC * HD:(HC + KVH) * HD].reshape(BATCH, SEQ, KVH, HD).transpose(0, 2, 1, 3)
            vp = qkv[:NP_ROWS, (HC + KVH) * HD:].reshape(BATCH, SEQ, KVH, HD).transpose(0, 2, 1, 3)
            out_k.append(kp)
            out_v.append(vp)
        w_r = jnp.concatenate([w_erouter[l].transpose(1, 0, 2).reshape(D, N_EXPERTS), w_grp[l],
                               jnp.zeros((D, 128 - N_EXPERTS - N_GROUPS), F32)], axis=1)
        b_r = jnp.concatenate([b_erouter[l].reshape(-1), b_grp[l],
                               jnp.zeros((128 - N_EXPERTS - N_GROUPS,), F32)])[None, :]
        x = _moe_dense(x, mods, l, w_r, b_r,
                       _bf(w_gate[l]).reshape(N_GROUPS, EPG, D, EXPERT_FF),
                       _bf(w_up[l]).reshape(N_GROUPS, EPG, D, EXPERT_FF),
                       _bf(w_down[l]).reshape(N_GROUPS, EPG * EXPERT_FF, D),
                       ln_g[l, 1][None, :], ln_b[l, 1][None, :])
    return (x[:NP_ROWS].reshape(BATCH, SEQ, D), x[NP_ROWS:].reshape(DEC_BATCH, DEC_SEQ, D),
            jnp.stack(out_mc, 1), jnp.stack(out_mn, 1), jnp.stack(out_mm, 1), jnp.stack(out_ds, 1),
            jnp.stack(out_k, 1), jnp.stack(out_v, 1))
```

```python
import functools

import numpy as np
import jax
import jax.numpy as jnp
from jax import lax
from jax.experimental import pallas as pl
from jax.experimental.pallas import tpu as pltpu

F32 = jnp.float32
BF16 = jnp.bfloat16
HI = lax.Precision.HIGHEST

D = 1024
BATCH = 32
SEQ = 256
DEPTH = 4
DEC_BATCH = 2
DEC_SEQ = 4096
PAST_LEN = 512
GRID_W = 64
N_EVEN = 2
N_ODD = 2
HA = 4
HB = 4
DH = 128
CHUNK = 64
HC = 16
KVH = 4
HD = 64
WINDOW = 128
QBLOCK = 128
ROPE_THETA = 10000.0
N_GROUPS = 4
EPG = 4
N_EXPERTS = 16
EXPERT_FF = 256
DN_ALPHA = (2 * DEPTH) ** 0.25
LN_EPS = 1e-5

NP_ROWS = BATCH * SEQ
NS_ROWS = DEC_BATCH * DEC_SEQ
T = NP_ROWS + NS_ROWS
N_MOD_ROWS = 8
EVEN_W = 4224
QKV_W = (HC + 2 * KVH) * HD

P_CHUNKS = SEQ // CHUNK
S_CHUNKS = DEC_SEQ // CHUNK
P_STEPS = BATCH * P_CHUNKS
S_STEPS = DEC_BATCH * S_CHUNKS
P_BLOCKS = NP_ROWS // CHUNK

VMEM_LIMIT = 48 * 1024 * 1024


def _params(n_axes, vmem_limit=VMEM_LIMIT):
    return pltpu.CompilerParams(dimension_semantics=("arbitrary",) * n_axes,
                                vmem_limit_bytes=vmem_limit)


def _mm(a, b, prec=None):
    return lax.dot_general(a, b, (((1,), (0,)), ((), ())), precision=prec, preferred_element_type=F32)


def _mm_nt(a, b, prec=None):
    return lax.dot_general(a, b, (((1,), (1,)), ((), ())), precision=prec, preferred_element_type=F32)


def _mm_tn(a, b, prec=None):
    return lax.dot_general(a, b, (((0,), (0,)), ((), ())), precision=prec, preferred_element_type=F32)


def _bf(x):
    return x.astype(BF16)


def _sigmoid(x):
    return 1.0 / (1.0 + jnp.exp(-x))


def _softplus(x):
    return jnp.maximum(x, 0.0) + jnp.log1p(jnp.exp(-jnp.abs(x)))


def _log_sigmoid(x):
    return jnp.minimum(x, 0.0) - jnp.log1p(jnp.exp(-jnp.abs(x)))


def _mod_row(tile, tm):
    npt = NP_ROWS // tm
    per = DEC_SEQ // tm
    return jnp.where(tile < npt, 0, 1 + (tile - npt) // per)


def _mod_spec(layer, chunk, tm):
    def imap(i, *_):
        return ((layer * N_MOD_ROWS + _mod_row(i, tm)) * 6 + chunk, 0, 0)
    return pl.BlockSpec((1, 1, D), imap)


def _modulation_kernel(c_ref, w_ref, b_ref, o_ref):
    x = c_ref[...]
    s = x * _sigmoid(x)
    o_ref[0] = _mm(s, w_ref[0], HI) + b_ref[0]


def _modulation(cvecs, w_mod, b_mod):
    out = pl.pallas_call(
        _modulation_kernel,
        out_shape=jax.ShapeDtypeStruct((DEPTH, N_MOD_ROWS, 6 * D), F32),
        grid=(DEPTH, 6),
        in_specs=[pl.BlockSpec((N_MOD_ROWS, D), lambda l, j: (0, 0)),
                  pl.BlockSpec((1, D, D), lambda l, j: (l, 0, j)),
                  pl.BlockSpec((1, 1, D), lambda l, j: (l * 6 + j, 0, 0))],
        out_specs=pl.BlockSpec((1, N_MOD_ROWS, D), lambda l, j: (l, 0, j)),
        compiler_params=_params(2),
        name="modulation",
    )(cvecs, w_mod, b_mod.reshape(DEPTH * 6, 1, D))
    return out.reshape(DEPTH * N_MOD_ROWS * 6, 1, D)


def _x_operands(x, tm):
    if isinstance(x, tuple):
        npt = NP_ROWS // tm
        return ([pl.BlockSpec((tm, D), lambda i, *_: (jnp.minimum(i, npt - 1), 0)),
                 pl.BlockSpec((tm, D), lambda i, *_: (jnp.maximum(i - npt, 0), 0))], list(x))
    return [pl.BlockSpec((tm, D), lambda i, *_: (i, 0))], [x]


def _load_x(x_refs, tm):
    if len(x_refs) == 1:
        return x_refs[0][...]
    return jnp.where(pl.program_id(0) < NP_ROWS // tm, x_refs[0][...], x_refs[1][...])


def _proj_kernel(*refs, tm, n_x):
    x_refs, (sh_ref, sc_ref, w_ref, o_ref) = refs[:n_x], refs[n_x:]
    xm = _load_x(x_refs, tm) * (1.0 + sc_ref[0]) + sh_ref[0]
    o_ref[...] = _mm(_bf(xm), w_ref[...])


def _even_proj(x, mods, layer, w):
    tm = 256
    x_specs, x_arrays = _x_operands(x, tm)
    return pl.pallas_call(
        functools.partial(_proj_kernel, tm=tm, n_x=len(x_arrays)),
        out_shape=jax.ShapeDtypeStruct((T, EVEN_W), F32),
        grid=(T // tm,),
        in_specs=x_specs + [_mod_spec(layer, 0, tm), _mod_spec(layer, 1, tm),
                            pl.BlockSpec((D, EVEN_W), lambda i: (0, 0))],
        out_specs=pl.BlockSpec((tm, EVEN_W), lambda i: (i, 0)),
        compiler_params=_params(1),
        name="even_proj",
    )(*x_arrays, mods, mods, w)


def _qkv_kernel(x_ref, sh_ref, sc_ref, w_ref, cos_ref, sa_ref, sb_ref, o_ref, *, tm):
    i = pl.program_id(0)
    xm = x_ref[...] * (1.0 + sc_ref[0]) + sh_ref[0]
    acc = _mm(_bf(xm), w_ref[...])
    n_q = HC * HD // 128
    n_k = KVH * HD // 128
    is_latent = i >= NP_ROWS // tm
    cos = jnp.where(is_latent, cos_ref[...], 1.0)
    sa = jnp.where(is_latent, sa_ref[...], 0.0)
    sb = jnp.where(is_latent, sb_ref[...], 0.0)
    for g in range(n_q + n_k):
        blk = acc[:, g * 128:(g + 1) * 128]
        if g < n_q:
            blk = blk * (HD ** -0.5)
        rot = blk * cos + pltpu.roll(blk, 112, 1) * sa + pltpu.roll(blk, 16, 1) * sb
        o_ref[:, g * 128:(g + 1) * 128] = rot
    o_ref[:, (n_q + n_k) * 128:] = acc[:, (n_q + n_k) * 128:]


def _rope_tables():
    half = HD // 4
    inv = np.float32(ROPE_THETA) ** (-np.arange(half, dtype=np.float32) / np.float32(half))
    pos = np.arange(DEC_SEQ)
    row = (pos // GRID_W).astype(np.float32)[:, None] * inv[None, :]
    col = (pos % GRID_W).astype(np.float32)[:, None] * inv[None, :]
    cos = np.concatenate([np.cos(row), np.cos(row), np.cos(col), np.cos(col)], axis=-1)
    sin = np.concatenate([np.sin(row), np.sin(row), np.sin(col), np.sin(col)], axis=-1)
    first = (np.arange(HD) % 32) < 16
    sa = np.where(first, -sin, 0.0)
    sb = np.where(first, 0.0, sin)
    tile2 = lambda t: jnp.asarray(np.concatenate([t, t], axis=-1), F32)
    return tile2(cos), tile2(sa), tile2(sb)


def _odd_proj(x, mods, layer, w, tables):
    tm = 512
    npt = NP_ROWS // tm
    per = DEC_SEQ // tm
    tab_spec = pl.BlockSpec((tm, 128), lambda i: (jnp.where(i < npt, 0, (i - npt) % per), 0))
    return pl.pallas_call(
        functools.partial(_qkv_kernel, tm=tm),
        out_shape=jax.ShapeDtypeStruct((T, QKV_W), F32),
        grid=(T // tm,),
        in_specs=[pl.BlockSpec((tm, D), lambda i: (i, 0)),
                  _mod_spec(layer, 0, tm), _mod_spec(layer, 1, tm),
                  pl.BlockSpec((D, QKV_W), lambda i: (0, 0)),
                  tab_spec, tab_spec, tab_spec],
        out_specs=pl.BlockSpec((tm, QKV_W), lambda i: (i, 0)),
        compiler_params=_params(1),
        name="odd_qkv_proj",
    )(x, mods, mods, w, *tables)


PREP_ROWS = 256


def _delta_prep_kernel(x_ref, prev_ref, next_ref, w_ref, o_ref):
    i = pl.program_id(0)
    npb = NP_ROWS // PREP_ROWS
    per = DEC_SEQ // PREP_ROWS
    is_latent = i >= npb
    pos = (i - npb) % per
    has_prev = jnp.logical_and(is_latent, pos > 0)
    has_next = jnp.logical_and(is_latent, pos < per - 1)
    x = x_ref[...]
    w = w_ref[...]
    rows = lax.broadcasted_iota(jnp.int32, x.shape, 0)
    prev_row = jnp.where(has_prev, prev_ref[7:8, :], 0.0)
    next_row = jnp.where(has_next, next_ref[0:1, :], 0.0)
    xm1 = jnp.where(rows == 0, prev_row, pltpu.roll(x, 1, 0))
    xp1 = jnp.where(rows == PREP_ROWS - 1, next_row, pltpu.roll(x, PREP_ROWS - 1, 0))
    y = xm1 * w[0:1, :] + x * w[1:2, :] + xp1 * w[2:3, :]
    y = y * _sigmoid(y)
    for h in range(3 * HB):
        yh = y[:, h * DH:(h + 1) * DH]
        if h < 2 * HB:
            inv = lax.rsqrt(jnp.sum(yh * yh, axis=-1, keepdims=True) + 1e-6)
            yh = yh * (inv * (DH ** -0.5) if h < HB else inv)
        o_ref[:, h * DH:(h + 1) * DH] = yh


def _delta_prep(proj, conv_w):
    nblk = T // PREP_ROWS
    sub = PREP_ROWS // 8
    last8 = T // 8 - 1
    return pl.pallas_call(
        _delta_prep_kernel,
        out_shape=jax.ShapeDtypeStruct((T, 3 * 512), F32),
        grid=(nblk,),
        in_specs=[pl.BlockSpec((PREP_ROWS, 3 * 512), lambda i: (i, 0)),
                  pl.BlockSpec((8, 3 * 512), lambda i: (jnp.maximum(i * sub - 1, 0), 0)),
                  pl.BlockSpec((8, 3 * 512), lambda i: (jnp.minimum((i + 1) * sub, last8), 0)),
                  pl.BlockSpec((3, 3 * 512), lambda i: (0, 0))],
        out_specs=pl.BlockSpec((PREP_ROWS, 3 * 512), lambda i: (i, 0)),
        compiler_params=_params(1),
        name="delta_prep",
    )(proj, proj, proj, conv_w)


assert P_STEPS == S_STEPS
SCAN_STEPS = P_STEPS


def _bwd_local(s, nc):
    return (s // nc) * nc + nc - 1 - s % nc


def _scan_blocks():
    return (lambda s: s, lambda s: _bwd_local(s, P_CHUNKS),
            lambda s: P_BLOCKS + s, lambda s: P_BLOCKS + _bwd_local(s, S_CHUNKS))


def _scan_specs(xcol):
    gcol = EVEN_W // 128 - 1
    blocks = _scan_blocks()
    return ([pl.BlockSpec((CHUNK, 3 * 512), lambda s, f=f: (f(s), xcol)) for f in blocks]
            + [pl.BlockSpec((CHUNK, 128), lambda s, f=f: (f(s), gcol)) for f in blocks]
            + [pl.BlockSpec((1, 16, 128), lambda s, f=f: (f(s), 0, 0)) for f in blocks])


def _scan_out_specs():
    local = (lambda s: s, lambda s: _bwd_local(s, P_CHUNKS), lambda s: s, lambda s: _bwd_local(s, S_CHUNKS))
    return [pl.BlockSpec((CHUNK, 512), lambda s, f=f: (f(s), 0)) for f in local]


def _mlstm_kernel(xpf_ref, xpb_ref, xsf_ref, xsb_ref, gpf_ref, gpb_ref, gsf_ref, gsb_ref,
                  rpf_ref, rpb_ref, rsf_ref, rsb_ref,
                  brow_ref, b2_ref, c0_ref, n0_ref, m0_ref,
                  hpf_ref, hpb_ref, hsf_ref, hsb_ref, c_out, n_out, m_out,
                  cp_s, np_s, mp_s, cs_s, ns_s, ms_s):
    s = pl.program_id(0)
    jp = s % P_CHUNKS
    js = s % S_CHUNKS

    @pl.when(jp == 0)
    def _():
        cp_s[...] = jnp.zeros_like(cp_s)
        np_s[...] = jnp.zeros_like(np_s)
        mp_s[...] = jnp.zeros_like(mp_s)

    @pl.when(js == 0)
    def _():
        for d in range(2):
            for h in range(HA):
                cs_s[d * HA + h] = c0_ref[0, 0, d, h]
                ns_s[d * HA + h] = jnp.broadcast_to(n0_ref[0, 0, d, h:h + 1, :], (DH, 128)).T
        ms_s[...] = m0_ref[0, 0]

    states = ((cp_s, np_s, mp_s), (cs_s, ns_s, ms_s))

    row = lax.broadcasted_iota(jnp.int32, (CHUNK, 128), 0)
    lane = lax.broadcasted_iota(jnp.int32, (CHUNK, 128), 1)
    left = lane < CHUNK
    lcol = jnp.where(left, lane, lane - CHUNK)
    r64 = lax.broadcasted_iota(jnp.int32, (CHUNK, CHUNK), 0)
    c64 = lax.broadcasted_iota(jnp.int32, (CHUNK, CHUNK), 1)
    r128 = lax.broadcasted_iota(jnp.int32, (128, 128), 0)
    c128 = lax.broadcasted_iota(jnp.int32, (128, 128), 1)
    same_half = (r128 < CHUNK) == (c128 < CHUNK)
    neg = -jnp.inf
    ins = ((xpf_ref, gpf_ref, rpf_ref, hpf_ref), (xpb_ref, gpb_ref, rpb_ref, hpb_ref),
           (xsf_ref, gsf_ref, rsf_ref, hsf_ref), (xsb_ref, gsb_ref, rsb_ref, hsb_ref))

    sel_r = lax.broadcasted_iota(jnp.int32, (128, HA * 128), 0)
    sel_h = lax.broadcasted_iota(jnp.int32, (128, HA * 128), 1) // 128
    ones_lr = lax.broadcasted_iota(jnp.int32, (256, 256), 0)
    ones_lc = lax.broadcasted_iota(jnp.int32, (256, 256), 1)
    half_sum = (((ones_lr % 128) < CHUNK) == (ones_lc < 128)).astype(BF16)
    ones_l = jnp.ones((CHUNK, 128), BF16)
    rows256 = lax.broadcasted_iota(jnp.int32, (CHUNK, 256), 0)

    def running_max(x, d):
        sh = 1
        while sh < CHUNK:
            if d == 0:
                x = jnp.maximum(x, jnp.where(rows256 >= sh, pltpu.roll(x, sh, 0), neg))
            else:
                x = jnp.maximum(x, jnp.where(rows256 < CHUNK - sh, pltpu.roll(x, CHUNK - sh, 0), neg))
            sh *= 2
        return x

    gate = []
    for u in range(4):
        d = u % 2
        gc_ref, gr_ref = ins[u][1], ins[u][2]
        if d == 0:
            incl_p, tri_c, tri_r = lcol <= row, c64 <= r64, jnp.logical_and(same_half, r128 <= c128)
        else:
            incl_p, tri_c, tri_r = lcol >= row, c64 >= r64, jnp.logical_and(same_half, r128 >= c128)
        gc = gc_ref[...] + brow_ref[...]
        gr = gr_ref[0] + b2_ref[...]
        i_rep = _replicate(gc, (sel_r == d * 8 + sel_h).astype(BF16))
        f_rep = _replicate(_log_sigmoid(gc), (sel_r == d * 8 + 4 + sel_h).astype(BF16))
        b_rep = _prefix_cols(tri_c.astype(BF16), f_rep)
        cs_row = _prefix_rows(_log_sigmoid(gr), tri_r.astype(BF16))
        gate.append((incl_p, gr, cs_row, i_rep, b_rep))

    st = []
    for u in range(4):
        d = u % 2
        c_s, n_s, m_s = states[u // 2]
        for hp in range(HA // 2):
            x_ref = ins[u][0]
            incl_p, gr, cs_row, i_rep, b_rep = gate[u]
            end = CHUNK - 1 if d == 0 else 0
            heads = []
            for h in (2 * hp, 2 * hp + 1):
                c = d * HA + h
                heads.append(dict(c=c, h=h, q=x_ref[:, h * DH:(h + 1) * DH],
                                  k=x_ref[:, 512 + h * DH:512 + (h + 1) * DH] * (DH ** -0.5),
                                  v=x_ref[:, 1024 + h * DH:1024 + (h + 1) * DH],
                                  i=i_rep[:, h * 128:(h + 1) * 128], b=b_rep[:, h * 128:(h + 1) * 128],
                                  m=m_s[c:c + 1, :]))
            ha, hb_ = heads
            i_row = gr[d * 4 + hp:d * 4 + hp + 1, :]
            b_row = cs_row[d * 4 + 2 + hp:d * 4 + 3 + hp, :]
            run = running_max(jnp.concatenate([ha["i"] - ha["b"], hb_["i"] - hb_["b"]], axis=1), d)
            for idx, hd in enumerate(heads):
                top = jnp.maximum(hd["m"], run[:, idx * 128:(idx + 1) * 128])
                hd["m_t"] = hd["b"] + top
                hd["w_inter"] = jnp.exp(hd["m"] - top)
            b_p = jnp.where(left, ha["b"], hb_["b"])
            m_t_p = jnp.where(left, ha["m_t"], hb_["m_t"])
            dmat = jnp.where(incl_p, b_p - b_row + i_row, neg)
            q_cat = _bf(jnp.concatenate([ha["q"], hb_["q"]], axis=1))
            k_bd = _bf(_block_diag2(ha["k"], hb_["k"]))
            st.append(dict(u=u, d=d, heads=heads, sc=_mm_nt(q_cat, k_bd) * jnp.exp(dmat - m_t_p)))

    for p in st:
        ha, hb_ = p["heads"]
        h_ref = ins[p["u"]][3]
        c_s, n_s, m_s = states[p["u"] // 2]
        end = CHUNK - 1 if p["d"] == 0 else 0
        sc = p["sc"]
        sv = _mm(_bf(sc), _bf(_block_diag2(ha["v"], hb_["v"])))
        dens = _mm(jnp.concatenate(_split2(sc), axis=1), half_sum)
        for idx, hd in enumerate((ha, hb_)):
            c, h, q, k = hd["c"], hd["h"], hd["q"], hd["k"]
            c_mat = c_s[c]
            n_mat = n_s[c]
            qb = _bf(q)
            num = sv[:, idx * DH:(idx + 1) * DH] + hd["w_inter"] * _mm(qb, _bf(c_mat))
            den = dens[:, idx * 128:(idx + 1) * 128] + hd["w_inter"] * _mm(qb, _bf(n_mat))
            h_ref[:, h * DH:(h + 1) * DH] = num / jnp.maximum(jnp.abs(den), jnp.exp(-hd["m_t"]))
            b_last = hd["b"][end:end + 1, :]
            g_end = b_last - hd["b"] + hd["i"]
            m_new = jnp.maximum(b_last + hd["m"], jnp.max(g_end, axis=0, keepdims=True))
            kwb = _bf(k * jnp.exp(g_end - m_new))
            decay = jnp.exp(b_last + hd["m"] - m_new)
            c_s[c] = decay * c_mat + _mm_tn(kwb, _bf(hd["v"]))
            n_s[c] = decay * n_mat + _mm_tn(kwb, ones_l)
            m_s[c:c + 1, :] = m_new

    @pl.when(jp == P_CHUNKS - 1)
    def _():
        for d in range(2):
            for h in range(HA):
                c_out[0, d, h] = cp_s[d * HA + h]
                n_out[0, d, h:h + 1, :] = np_s[d * HA + h].T[0:1, :]
        m_out[0] = mp_s[...]


def _mlstm_scan(proj, gates_t2, bias_row, bias2, c0, n0, m0, e):
    const2 = lambda s: (0, 0)
    in_specs = (_scan_specs(1)
                + [pl.BlockSpec((1, 128), const2), pl.BlockSpec((16, 128), const2),
                   pl.BlockSpec((1, 1, 2, HA, DH, DH), lambda s: (s // S_CHUNKS, e, 0, 0, 0, 0)),
                   pl.BlockSpec((1, 1, 2, HA, DH), lambda s: (s // S_CHUNKS, e, 0, 0, 0)),
                   pl.BlockSpec((1, 1, 2 * HA, 128), lambda s: (s // S_CHUNKS, e, 0, 0))])
    half = jax.ShapeDtypeStruct((NP_ROWS, HA * DH), F32)
    out_shape = (half, half, half, half,
                 jax.ShapeDtypeStruct((BATCH, 2, HA, DH, DH), F32),
                 jax.ShapeDtypeStruct((BATCH, 2, HA, DH), F32),
                 jax.ShapeDtypeStruct((BATCH, 2 * HA, 128), F32))
    out_specs = _scan_out_specs() + [
        pl.BlockSpec((1, 2, HA, DH, DH), lambda s: (s // P_CHUNKS, 0, 0, 0, 0)),
        pl.BlockSpec((1, 2, HA, DH), lambda s: (s // P_CHUNKS, 0, 0, 0)),
        pl.BlockSpec((1, 2 * HA, 128), lambda s: (s // P_CHUNKS, 0, 0))]
    state = [pltpu.VMEM((2 * HA, DH, DH), F32), pltpu.VMEM((2 * HA, DH, 128), F32), pltpu.VMEM((2 * HA, 128), F32)]
    return pl.pallas_call(
        _mlstm_kernel,
        out_shape=out_shape,
        grid=(SCAN_STEPS,),
        in_specs=in_specs,
        out_specs=out_specs,
        scratch_shapes=state + state,
        compiler_params=_params(1),
        name="mlstm_scan",
    )(proj, proj, proj, proj, proj, proj, proj, proj, gates_t2, gates_t2, gates_t2, gates_t2,
      bias_row, bias2, c0, n0, m0)


def _split2(x):
    hi = _bf(x)
    return hi, _bf(x - hi.astype(F32))


def _split3(x):
    h1 = _bf(x)
    r1 = x - h1.astype(F32)
    h2 = _bf(r1)
    return h1, h2, _bf(r1 - h2.astype(F32))


def _lhs3(x):
    hi, lo = _split2(x)
    return jnp.concatenate([hi, lo, hi], axis=1)


def _rhs3(x):
    hi, lo = _split2(x)
    return jnp.concatenate([hi, hi, lo], axis=0)


def _prefix_cols(tri_bf, x):
    n = x.shape[1]
    r = _mm(tri_bf, jnp.concatenate(_split3(x), axis=1))
    return r[:, :n] + r[:, n:2 * n] + r[:, 2 * n:]


def _replicate(x, sel):
    return _mm(jnp.concatenate(_split3(x), axis=1), jnp.concatenate([sel, sel, sel], axis=0))


def _prefix_rows(x, tri_bf):
    m = x.shape[0]
    r = _mm(jnp.concatenate(_split3(x), axis=0), tri_bf)
    return r[:m] + r[m:2 * m] + r[2 * m:]


LEVELS = tuple(range(6))


def _block_diag2(a, b):
    z = jnp.zeros_like(a)
    return jnp.concatenate([jnp.concatenate([a, z], axis=1), jnp.concatenate([z, b], axis=1)], axis=0)


def _delta_kernel(xpf_ref, xpb_ref, xsf_ref, xsb_ref, gpf_ref, gpb_ref, gsf_ref, gsb_ref,
                  rpf_ref, rpb_ref, rsf_ref, rsb_ref,
                  dtrow_ref, dt2_ref, narow_ref, na2_ref, s0_ref,
                  opf_ref, opb_ref, osf_ref, osb_ref, s_out, sp_s, ss_s):
    s = pl.program_id(0)
    jp = s % P_CHUNKS
    js = s % S_CHUNKS

    @pl.when(jp == 0)
    def _():
        sp_s[...] = jnp.zeros_like(sp_s)

    @pl.when(js == 0)
    def _():
        for d in range(2):
            for h in range(HB):
                ss_s[d * HB + h] = s0_ref[0, 0, d, h]

    states = (sp_s, ss_s)

    row = lax.broadcasted_iota(jnp.int32, (CHUNK, 128), 0)
    lane = lax.broadcasted_iota(jnp.int32, (CHUNK, 128), 1)
    left = lane < CHUNK
    lcol = jnp.where(left, lane, lane - CHUNK)
    eye_p = (lcol == row).astype(F32)
    r64 = lax.broadcasted_iota(jnp.int32, (CHUNK, CHUNK), 0)
    c64 = lax.broadcasted_iota(jnp.int32, (CHUNK, CHUNK), 1)
    r128 = lax.broadcasted_iota(jnp.int32, (128, 128), 0)
    c128 = lax.broadcasted_iota(jnp.int32, (128, 128), 1)
    same_half = (r128 < CHUNK) == (c128 < CHUNK)
    ins = ((xpf_ref, gpf_ref, rpf_ref, opf_ref), (xpb_ref, gpb_ref, rpb_ref, opb_ref),
           (xsf_ref, gsf_ref, rsf_ref, osf_ref), (xsb_ref, gsb_ref, rsb_ref, osb_ref))

    gate = []
    for u in range(4):
        d = u % 2
        gc_ref, gr_ref = ins[u][1], ins[u][2]
        if d == 0:
            incl_p, strict_p = lcol <= row, lcol < row
            tri_c, tri_r = c64 <= r64, jnp.logical_and(same_half, r128 <= c128)
        else:
            incl_p, strict_p = lcol >= row, lcol > row
            tri_c, tri_r = c64 >= r64, jnp.logical_and(same_half, r128 >= c128)
        xc = gc_ref[...]
        xr = gr_ref[0]
        la_c = narow_ref[...] * _softplus(xc + dtrow_ref[...])
        la_r = na2_ref[...] * _softplus(xr + dt2_ref[...])
        g_c = _prefix_cols(tri_c.astype(BF16), la_c)
        g_r = _prefix_rows(la_r, tri_r.astype(BF16))
        gate.append((incl_p, strict_p, _sigmoid(xc), g_c, g_r))

    def bd_rhs(hi, lo, mask=None):
        top = left if mask is None else jnp.logical_and(left, mask)
        bot = jnp.logical_not(left) if mask is None else jnp.logical_and(jnp.logical_not(left), mask)
        zero = jnp.zeros_like(hi)
        blocks = [jnp.concatenate([jnp.where(top, x, zero), jnp.where(bot, x, zero)], axis=0) for x in (hi, lo)]
        return jnp.concatenate([blocks[0], blocks[0], blocks[1]], axis=0)

    off = ([], [])
    for lv in LEVELS:
        same = jnp.right_shift(row, lv + 1) == jnp.right_shift(lcol, lv + 1)
        r_hi = jnp.bitwise_and(jnp.right_shift(row, lv), 1) == 1
        c_hi = jnp.bitwise_and(jnp.right_shift(lcol, lv), 1) == 1
        off[0].append(same & r_hi & jnp.logical_not(c_hi))
        off[1].append(same & jnp.logical_not(r_hi) & c_hi)

    pairs = [(u, hp) for u in range(4) for hp in range(HB // 2)]
    st = []
    for u, hp in pairs:
        d = u % 2
        x_ref = ins[u][0]
        incl_p, strict_p, beta_c, g_c, g_r = gate[u]
        end = CHUNK - 1 if d == 0 else 0
        heads = []
        for h in (2 * hp, 2 * hp + 1):
            ib = 16 + d * HB + h
            ia = 24 + d * HB + h
            gcol = g_c[:, ia:ia + 1]
            bc = beta_c[:, ib:ib + 1]
            q = x_ref[:, h * DH:(h + 1) * DH]
            k = x_ref[:, 512 + h * DH:512 + (h + 1) * DH]
            v = x_ref[:, 1024 + h * DH:1024 + (h + 1) * DH]
            heads.append(dict(h=h, gcol=gcol, bc=bc, q=q, k=k, v=v, kb=k * bc, eg=jnp.exp(gcol),
                              g_last=gcol[end:end + 1, :]))
        ha, hb_ = heads
        r = 12 + d * 2 + hp
        gcol_p = jnp.where(left, ha["gcol"], hb_["gcol"])
        decay = jnp.exp(jnp.where(incl_p, gcol_p - g_r[r:r + 1, :], -jnp.inf))
        k_bd = _bf(_block_diag2(ha["k"], hb_["k"]))
        kb_cat = _bf(jnp.concatenate([ha["kb"], hb_["kb"]], axis=1))
        q_cat = _bf(jnp.concatenate([ha["q"], hb_["q"]], axis=1))
        a_mat = jnp.where(strict_p, _mm_nt(kb_cat, k_bd) * decay, 0.0)
        qk = _mm_nt(q_cat, k_bd) * decay
        a_hi, a_lo = _split2(a_mat)
        st.append(dict(u=u, d=d, heads=heads, t=eye_p - jnp.where(off[d][0], a_mat, 0.0), qk=qk,
                       am=[bd_rhs(a_hi, a_lo, m) for m in off[d][1:]]))

    for li in range(len(LEVELS) - 1):
        for p in st:
            p["t_parts"] = _split2(p["t"])
            t_hi, t_lo = p["t_parts"]
            p["w"] = _mm(jnp.concatenate([t_hi, t_lo, t_hi], axis=1), p["am"][li])
        for p in st:
            p["t"] = p["t"] - _mm(_lhs3(p["w"]), bd_rhs(*p["t_parts"]))

    for p in st:
        ha, hb_ = p["heads"]
        o_ref = ins[p["u"]][3]
        s_s = states[p["u"] // 2]
        rhs_a = jnp.concatenate([ha["v"] * ha["bc"], ha["kb"] * ha["eg"]], axis=1)
        rhs_b = jnp.concatenate([hb_["v"] * hb_["bc"], hb_["kb"] * hb_["eg"]], axis=1)
        (a_hi, a_lo), (b_hi, b_lo) = _split2(rhs_a), _split2(rhs_b)
        bd_hi, bd_lo = _block_diag2(a_hi, b_hi), _block_diag2(a_lo, b_lo)
        sol = _mm(_lhs3(p["t"]), jnp.concatenate([bd_hi, bd_hi, bd_lo], axis=0))
        vn = []
        for idx, hd in enumerate((ha, hb_)):
            c = p["d"] * HB + hd["h"]
            s_mat = s_s[c]
            sbf = _bf(s_mat)
            so = sol[:, idx * 2 * DH:(idx + 1) * 2 * DH]
            v_new = so[:, :DH] - _mm(_bf(so[:, DH:]), sbf)
            vn.append(v_new)
            hd["o1"] = _mm(_bf(hd["q"] * hd["eg"]), sbf)
            s_s[c] = (jnp.exp(hd["g_last"]) * s_mat
                      + _mm_tn(_bf(hd["k"] * jnp.exp(hd["g_last"] - hd["gcol"])), _bf(v_new)))
        o2 = _mm(_bf(p["qk"]), _bf(_block_diag2(vn[0], vn[1])))
        for idx, hd in enumerate((ha, hb_)):
            o_ref[:, hd["h"] * DH:(hd["h"] + 1) * DH] = hd["o1"] + o2[:, idx * DH:(idx + 1) * DH]

    @pl.when(jp == P_CHUNKS - 1)
    def _():
        for d in range(2):
            for h in range(HB):
                s_out[0, d, h] = sp_s[d * HB + h]


def _delta_scan(qkv, proj, gates_t2, dt_row, dt2, na_row, na2, s0, e):
    const2 = lambda s: (0, 0)
    in_specs = (_scan_specs(0)
                + [pl.BlockSpec((1, 128), const2), pl.BlockSpec((16, 128), const2),
                   pl.BlockSpec((1, 128), const2), pl.BlockSpec((16, 128), const2),
                   pl.BlockSpec((1, 1, 2, HB, DH, DH), lambda s: (s // S_CHUNKS, e, 0, 0, 0, 0))])
    half = jax.ShapeDtypeStruct((NP_ROWS, HB * DH), F32)
    out_shape = (half, half, half, half, jax.ShapeDtypeStruct((BATCH, 2, HB, DH, DH), F32))
    out_specs = _scan_out_specs() + [pl.BlockSpec((1, 2, HB, DH, DH), lambda s: (s // P_CHUNKS, 0, 0, 0, 0))]
    return pl.pallas_call(
        _delta_kernel,
        out_shape=out_shape,
        grid=(SCAN_STEPS,),
        in_specs=in_specs,
        out_specs=out_specs,
        scratch_shapes=[pltpu.VMEM((2 * HB, DH, DH), F32), pltpu.VMEM((2 * HB, DH, DH), F32)],
        compiler_params=_params(1),
        name="delta_scan",
    )(qkv, qkv, qkv, qkv, proj, proj, proj, proj, gates_t2, gates_t2, gates_t2, gates_t2,
      dt_row, dt2, na_row, na2, s0)


def _residual_ln(x, gate, y, g, b):
    r = DN_ALPHA * x + gate * y
    mu = jnp.mean(r, axis=-1, keepdims=True)
    var = jnp.mean(jnp.square(r - mu), axis=-1, keepdims=True)
    return (r - mu) * lax.rsqrt(var + LN_EPS) * g + b


def _even_out_kernel(*refs, tm, n_x):
    x_refs = refs[:n_x]
    (hpf_ref, hpb_ref, hsf_ref, hsb_ref, opf_ref, opb_ref, osf_ref, osb_ref,
     oa_ref, zb_ref, mg_ref, dg_ref, w_ref, gate_ref, g_ref, b_ref, o_ref) = refs[n_x:]
    is_p = pl.program_id(0) < NP_ROWS // tm
    hf_ref, hb_ref, of_ref, ob_ref = (
        lambda sl, p=p, q=q: jnp.where(is_p, p[:, sl], q[:, sl])
        for p, q in ((hpf_ref, hsf_ref), (hpb_ref, hsb_ref), (opf_ref, osf_ref), (opb_ref, osb_ref)))
    parts = []
    for h in range(HA):
        sl = slice(h * DH, (h + 1) * DH)
        hh = hf_ref(sl) + hb_ref(sl)
        mu = jnp.mean(hh, axis=-1, keepdims=True)
        var = jnp.mean(jnp.square(hh - mu), axis=-1, keepdims=True)
        parts.append(_sigmoid(oa_ref[:, sl]) * ((hh - mu) * lax.rsqrt(var + LN_EPS) * mg_ref[:, sl]))
    for h in range(HB):
        sl = slice(h * DH, (h + 1) * DH)
        oo = of_ref(sl) + ob_ref(sl)
        z = zb_ref[:, sl]
        nrm = oo * lax.rsqrt(jnp.mean(jnp.square(oo), axis=-1, keepdims=True) + LN_EPS) * dg_ref[:, sl]
        parts.append(nrm * (z * _sigmoid(z)))
    a = jnp.concatenate(parts, axis=1)
    y = _mm(_bf(a), w_ref[...])
    o_ref[...] = _residual_ln(_load_x(x_refs, tm), gate_ref[0], y, g_ref[...], b_ref[...])


def _even_out(h_scan, o_scan, proj, mg, dg, w, x, mods, layer, ln_g, ln_b):
    tm = 512
    npt = NP_ROWS // tm
    row512 = lambda i: (i, 0)
    const2 = lambda i: (0, 0)
    p_spec = pl.BlockSpec((tm, 512), lambda i: (jnp.minimum(i, npt - 1), 0))
    s_spec = pl.BlockSpec((tm, 512), lambda i: (jnp.maximum(i - npt, 0), 0))
    x_specs, x_arrays = _x_operands(x, tm)
    return pl.pallas_call(
        functools.partial(_even_out_kernel, tm=tm, n_x=len(x_arrays)),
        out_shape=jax.ShapeDtypeStruct((T, D), F32),
        grid=(T // tm,),
        in_specs=x_specs + [p_spec, p_spec, s_spec, s_spec, p_spec, p_spec, s_spec, s_spec,
                            pl.BlockSpec((tm, 512), lambda i: (i, 6)),
                            pl.BlockSpec((tm, 512), lambda i: (i, 7)),
                            pl.BlockSpec((1, 512), const2), pl.BlockSpec((1, 512), const2),
                            pl.BlockSpec((D, D), const2),
                            _mod_spec(layer, 2, tm),
                            pl.BlockSpec((1, D), const2), pl.BlockSpec((1, D), const2)],
        out_specs=pl.BlockSpec((tm, D), row512),
        compiler_params=_params(1),
        name="even_out_ln",
    )(*x_arrays, *h_scan, *o_scan, proj, proj, mg, dg, w, mods, ln_g, ln_b)


def _odd_out_kernel(ap_ref, as_ref, w_ref, x_ref, gate_ref, g_ref, b_ref, o_ref, *, tm):
    a = jnp.where(pl.program_id(0) < NP_ROWS // tm, ap_ref[...], as_ref[...])
    y = _mm(_bf(a), w_ref[...])
    o_ref[...] = _residual_ln(x_ref[...], gate_ref[0], y, g_ref[...], b_ref[...])


def _odd_out(a_prompt, a_latent, w, x, mods, layer, ln_g, ln_b):
    tm = 512
    npt = NP_ROWS // tm
    row = lambda i: (i, 0)
    const2 = lambda i: (0, 0)
    return pl.pallas_call(
        functools.partial(_odd_out_kernel, tm=tm),
        out_shape=jax.ShapeDtypeStruct((T, D), F32),
        grid=(T // tm,),
        in_specs=[pl.BlockSpec((tm, D), lambda i: (jnp.minimum(i, npt - 1), 0)),
                  pl.BlockSpec((tm, D), lambda i: (jnp.maximum(i - npt, 0), 0)),
                  pl.BlockSpec((D, D), const2), pl.BlockSpec((tm, D), row),
                  _mod_spec(layer, 2, tm), pl.BlockSpec((1, D), const2), pl.BlockSpec((1, D), const2)],
        out_specs=pl.BlockSpec((tm, D), row),
        compiler_params=_params(1),
        name="odd_out_ln",
    )(a_prompt, a_latent, w, x, mods, ln_g, ln_b)


def _attn_ctx_kernel(qkv_ref, sink_ref, o_ref):
    ones = jnp.ones((SEQ, HD), BF16)
    for kv in range(KVH):
        k = _bf(qkv_ref[:, HC * HD + kv * HD:HC * HD + (kv + 1) * HD])
        v = _bf(qkv_ref[:, (HC + KVH) * HD + kv * HD:(HC + KVH) * HD + (kv + 1) * HD])
        v1 = jnp.concatenate([v, ones], axis=1)
        heads = range(kv * (HC // KVH), (kv + 1) * (HC // KVH))
        sts = [_mm_nt(k, _bf(qkv_ref[:, h * HD:(h + 1) * HD])) for h in heads]
        ms = [jnp.maximum(jnp.max(st, axis=0, keepdims=True), sink_ref[h]) for st, h in zip(sts, heads)]
        ovs = [_mm_tn(v1, _bf(jnp.exp(st - m))) for st, m in zip(sts, ms)]
        outs = [ov[:HD] / (ov[HD:HD + 1] + jnp.exp(sink_ref[h] - m)) for ov, m, h in zip(ovs, ms, heads)]
        for pr in range(HC // KVH // 2):
            lane0 = (kv * (HC // KVH) + 2 * pr) * HD
            o_ref[:, lane0:lane0 + 2 * HD] = jnp.concatenate(outs[2 * pr:2 * pr + 2], axis=0).T


def _attn_context(qkv, sink):
    return pl.pallas_call(
        _attn_ctx_kernel,
        out_shape=jax.ShapeDtypeStruct((NP_ROWS, HC * HD), F32),
        grid=(BATCH,),
        in_specs=[pl.BlockSpec((SEQ, QKV_W), lambda b: (b, 0)),
                  pl.BlockSpec(memory_space=pltpu.SMEM)],
        out_specs=pl.BlockSpec((SEQ, HC * HD), lambda b: (b, 0)),
        compiler_params=_params(1),
        name="attn_context",
    )(qkv, sink)


def _attn_lat_kernel(q_ref, kp_ref, kc_ref, kn_ref, vp_ref, vc_ref, vn_ref, ck_ref, cv_ref,
                     sink_ref, o_ref, bias_s):
    j = pl.program_id(1)
    nb = DEC_SEQ // QBLOCK
    cc = lax.broadcasted_iota(jnp.int32, (3 * QBLOCK, QBLOCK), 0)
    r = lax.broadcasted_iota(jnp.int32, (3 * QBLOCK, QBLOCK), 1)
    lo = jnp.where(j >= 1, 0, QBLOCK)
    hi = jnp.where(j <= nb - 2, 3 * QBLOCK, 2 * QBLOCK)
    ok = (jnp.abs(QBLOCK + r - cc) <= WINDOW) & (cc >= lo) & (cc < hi)
    bias_s[...] = jnp.where(ok, 0.0, -jnp.inf)
    for kv in range(KVH):
        ks = slice(kv * HD, (kv + 1) * HD)
        k_all = _bf(jnp.concatenate([ck_ref[0, 0, kv], kp_ref[:, ks], kc_ref[:, ks], kn_ref[:, ks]], axis=0))
        v_all = _bf(jnp.concatenate([cv_ref[0, 0, kv], vp_ref[:, ks], vc_ref[:, ks], vn_ref[:, ks]], axis=0))
        v1 = jnp.concatenate([v_all, jnp.ones_like(v_all)], axis=1)
        outs = []
        for h in range(kv * (HC // KVH), (kv + 1) * (HC // KVH)):
            sink = sink_ref[h]
            st = _mm_nt(k_all, _bf(q_ref[:, h * HD:(h + 1) * HD]))
            s_ctx = st[:PAST_LEN]
            s_loc = st[PAST_LEN:] + bias_s[...]
            m = jnp.maximum(jnp.maximum(jnp.max(s_ctx, axis=0, keepdims=True),
                                        jnp.max(s_loc, axis=0, keepdims=True)), sink)
            p = _bf(jnp.concatenate([jnp.exp(s_ctx - m), jnp.exp(s_loc - m)], axis=0))
            ov = _mm_tn(v1, p)
            outs.append(ov[:HD] / (ov[HD:HD + 1] + jnp.exp(sink - m)))
        for pr in range(HC // KVH // 2):
            lane0 = (kv * (HC // KVH) + 2 * pr) * HD
            o_ref[:, lane0:lane0 + 2 * HD] = jnp.concatenate(outs[2 * pr:2 * pr + 2], axis=0).T


def _attn_latent(qkv, cache_k, cache_v, sink, o):
    nb = DEC_SEQ // QBLOCK
    base = NP_ROWS // QBLOCK
    blk = lambda b, j: base + b * nb + j
    prev = lambda b, j: base + b * nb + jnp.maximum(j - 1, 0)
    nxt = lambda b, j: base + b * nb + jnp.minimum(j + 1, nb - 1)
    kcol, vcol = HC * HD // 256, HC * HD // 256 + 1
    cache_spec = pl.BlockSpec((1, 1, KVH, PAST_LEN, HD), lambda b, j: (b, o, 0, 0, 0))
    return pl.pallas_call(
        _attn_lat_kernel,
        out_shape=jax.ShapeDtypeStruct((NS_ROWS, HC * HD), F32),
        grid=(DEC_BATCH, nb),
        in_specs=[pl.BlockSpec((QBLOCK, HC * HD), lambda b, j: (blk(b, j), 0)),
                  pl.BlockSpec((QBLOCK, 256), lambda b, j: (prev(b, j), kcol)),
                  pl.BlockSpec((QBLOCK, 256), lambda b, j: (blk(b, j), kcol)),
                  pl.BlockSpec((QBLOCK, 256), lambda b, j: (nxt(b, j), kcol)),
                  pl.BlockSpec((QBLOCK, 256), lambda b, j: (prev(b, j), vcol)),
                  pl.BlockSpec((QBLOCK, 256), lambda b, j: (blk(b, j), vcol)),
                  pl.BlockSpec((QBLOCK, 256), lambda b, j: (nxt(b, j), vcol)),
                  cache_spec, cache_spec,
                  pl.BlockSpec(memory_space=pltpu.SMEM)],
        out_specs=pl.BlockSpec((QBLOCK, HC * HD), lambda b, j: (b * nb + j, 0)),
        scratch_shapes=[pltpu.VMEM((3 * QBLOCK, QBLOCK), F32)],
        compiler_params=_params(2),
        name="attn_latent",
    )(qkv, qkv, qkv, qkv, qkv, qkv, qkv, cache_k, cache_v, sink)


def _route(lg):
    lane = lax.broadcasted_iota(jnp.int32, lg.shape, 1)
    neg = -jnp.inf
    big = 1 << 20
    is_grp = jnp.logical_and(lane >= N_EXPERTS, lane < N_EXPERTS + N_GROUPS)
    mg = jnp.max(jnp.where(is_grp, lg, neg), axis=1, keepdims=True)
    g_lane = jnp.min(jnp.where(jnp.logical_and(is_grp, lg == mg), lane, big), axis=1, keepdims=True)
    g_w = 1.0 / jnp.sum(jnp.where(is_grp, jnp.exp(lg - mg), 0.0), axis=1, keepdims=True)
    g_idx = g_lane - N_EXPERTS
    in_grp = jnp.logical_and(lane >= g_idx * EPG, lane < (g_idx + 1) * EPG)
    v1 = jnp.max(jnp.where(in_grp, lg, neg), axis=1, keepdims=True)
    i1 = jnp.min(jnp.where(jnp.logical_and(in_grp, lg == v1), lane, big), axis=1, keepdims=True)
    rest = jnp.logical_and(in_grp, lane != i1)
    v2 = jnp.max(jnp.where(rest, lg, neg), axis=1, keepdims=True)
    i2 = jnp.min(jnp.where(jnp.logical_and(rest, lg == v2), lane, big), axis=1, keepdims=True)
    e2 = jnp.exp(v2 - v1)
    p1 = 1.0 / (1.0 + e2)
    p2 = e2 / (1.0 + e2)
    return jnp.where(lane == 0, i1.astype(F32),
                     jnp.where(lane == 1, i2.astype(F32),
                               jnp.where(lane == 2, p1 * g_w, jnp.where(lane == 3, p2 * g_w, 0.0))))


def _moe_dense_kernel(x_ref, sh_ref, sc_ref, wr_ref, br_ref, wg_ref, wu_ref, wd_ref, gate_ref, g_ref, b_ref,
                      *out_and_scratch, tm):
    *o_refs, xm_s, meta_s, acc_s = out_and_scratch
    grp = pl.program_id(1)

    @pl.when(grp == 0)
    def _():
        xm = x_ref[...] * (1.0 + sc_ref[0]) + sh_ref[0]
        meta_s[...] = _route(_mm(_lhs3(xm), _rhs3(wr_ref[...])) + br_ref[...])
        xm_s[...] = _bf(xm)
        acc_s[...] = jnp.zeros_like(acc_s)

    xm = xm_s[...]
    meta = meta_s[...]
    i1 = meta[:, 0:1].astype(jnp.int32)
    i2 = meta[:, 1:2].astype(jnp.int32)
    w1 = meta[:, 2:3]
    w2 = meta[:, 3:4]
    hid = []
    for e in range(EPG):
        eid = grp * EPG + e
        gate = jnp.where(i1 == eid, w1, 0.0) + jnp.where(i2 == eid, w2, 0.0)
        a = _mm(xm, wg_ref[0, e])
        u = _mm(xm, wu_ref[0, e])
        hid.append(_bf((a * _sigmoid(a)) * u * gate))
    acc_s[...] += _mm(jnp.concatenate(hid, axis=1), wd_ref[0])

    def result():
        return _residual_ln(x_ref[...], gate_ref[0], acc_s[...], g_ref[...], b_ref[...])

    last = grp == N_GROUPS - 1
    if len(o_refs) == 1:
        @pl.when(last)
        def _():
            o_refs[0][...] = result()
    else:
        is_p = pl.program_id(0) < NP_ROWS // tm

        @pl.when(jnp.logical_and(last, is_p))
        def _():
            o_refs[0][...] = result()

        @pl.when(jnp.logical_and(last, jnp.logical_not(is_p)))
        def _():
            o_refs[1][...] = result()


def _moe_dense(x, mods, layer, w_r, b_r, wg, wu, wd, ln_g, ln_b, split_out=False):
    tm = 1024
    npt = NP_ROWS // tm
    row = lambda i, g: (i, 0)
    const2 = lambda i, g: (0, 0)
    if split_out:
        half = jax.ShapeDtypeStruct((NP_ROWS, D), F32)
        out_shape = (half, half)
        out_specs = (pl.BlockSpec((tm, D), lambda i, g: (jnp.minimum(i, npt - 1), 0)),
                     pl.BlockSpec((tm, D), lambda i, g: (jnp.maximum(i - npt, 0), 0)))
    else:
        out_shape = jax.ShapeDtypeStruct((T, D), F32)
        out_specs = pl.BlockSpec((tm, D), row)
    return pl.pallas_call(
        functools.partial(_moe_dense_kernel, tm=tm),
        out_shape=out_shape,
        grid=(T // tm, N_GROUPS),
        in_specs=[pl.BlockSpec((tm, D), row), _mod_spec(layer, 3, tm), _mod_spec(layer, 4, tm),
                  pl.BlockSpec((D, 128), const2), pl.BlockSpec((1, 128), const2),
                  pl.BlockSpec((1, EPG, D, EXPERT_FF), lambda i, g: (g, 0, 0, 0)),
                  pl.BlockSpec((1, EPG, D, EXPERT_FF), lambda i, g: (g, 0, 0, 0)),
                  pl.BlockSpec((1, EPG * EXPERT_FF, D), lambda i, g: (g, 0, 0)),
                  _mod_spec(layer, 5, tm), pl.BlockSpec((1, D), const2), pl.BlockSpec((1, D), const2)],
        out_specs=out_specs,
        scratch_shapes=[pltpu.VMEM((tm, D), BF16), pltpu.VMEM((tm, 128), F32), pltpu.VMEM((tm, D), F32)],
        compiler_params=_params(2, VMEM_LIMIT + (2 * tm * D * 4 if split_out else 0)),
        name="moe_dense",
    )(x, mods, mods, w_r, b_r, wg, wu, wd, mods, ln_g, ln_b)


def _permute_even_w(w):
    a_end = 4 * HA * DH
    g_end = a_end + 4 * HA
    c_end = g_end + 3 * HB * DH
    z_end = c_end + HB * DH
    small = jnp.concatenate([w[:, a_end:g_end], w[:, z_end:]], axis=1)
    pad = jnp.zeros((w.shape[0], 128 - small.shape[1]), w.dtype)
    return jnp.concatenate([w[:, g_end:c_end], w[:, :a_end], w[:, c_end:z_end], small, pad], axis=1)


def _lane_row(vals, offset):
    return jnp.zeros((1, 128), F32).at[0, offset:offset + vals.shape[0]].set(vals.astype(F32))


def _pair_rows(vals, offset):
    v32 = jnp.zeros((32,), F32).at[offset:offset + vals.shape[0]].set(vals.astype(F32))
    return jnp.repeat(v32.reshape(16, 2), CHUNK, axis=1)


def kernel(x_prompt, x_sample, c, c_ctx, state_mlstm_c, state_mlstm_n, state_mlstm_m, state_delta, cache_k, cache_v, w_mod, b_mod, ln_g, ln_b, w_in_even, mlstm_gate_b, mlstm_norm_g, delta_conv_w, delta_a_log, delta_dt_bias, delta_norm_g, w_out_even, w_qkv_odd, attn_sink, w_out_odd, w_grp, b_grp, w_erouter, b_erouter, w_gate, w_up, w_down):
    x = (x_prompt.reshape(NP_ROWS, D), x_sample.reshape(NS_ROWS, D))
    cvecs = jnp.concatenate([c_ctx[None, :], c, jnp.zeros((N_MOD_ROWS - 1 - DEC_BATCH, D), F32)], axis=0)
    mods = _modulation(cvecs, w_mod, b_mod)
    tables = _rope_tables()
    m0_all = jnp.broadcast_to(state_mlstm_m.reshape(DEC_BATCH, N_EVEN, 2 * HA, 1), (DEC_BATCH, N_EVEN, 2 * HA, 128))

    out_mc, out_mn, out_mm, out_ds, out_k, out_v = [], [], [], [], [], []
    for l in range(DEPTH):
        if l % 2 == 0:
            e = l // 2
            proj = _even_proj(x, mods, l, _bf(_permute_even_w(w_in_even[e])))
            gates_t2 = (proj[:, EVEN_W - 128:EVEN_W - 96].reshape(T // CHUNK, CHUNK, 32).transpose(0, 2, 1)
                        .reshape(T // CHUNK, 16, 128))
            gb = mlstm_gate_b[e].reshape(-1)
            *h_scan, mc, mn, mm = _mlstm_scan(proj, gates_t2, _lane_row(gb, 0), _pair_rows(gb, 0),
                                              state_mlstm_c, state_mlstm_n, m0_all, e)
            qkv = _delta_prep(proj, delta_conv_w[e])
            dtb = delta_dt_bias[e].reshape(-1)
            nea = -jnp.exp(delta_a_log[e].astype(F32)).reshape(-1)
            *o_scan, ds = _delta_scan(qkv, proj, gates_t2, _lane_row(dtb, 24), _pair_rows(dtb, 24),
                                      _lane_row(nea, 24), _pair_rows(nea, 24), state_delta, e)
            x = _even_out(h_scan, o_scan, proj, mlstm_norm_g[e][None, :], delta_norm_g[e][None, :],
                          _bf(w_out_even[e]), x, mods, l, ln_g[l, 0][None, :], ln_b[l, 0][None, :])
            out_mc.append(mc)
            out_mn.append(mn)
            out_mm.append(mm[:, :, 0].reshape(BATCH, 2, HA))
            out_ds.append(ds)
        else:
            o = l // 2
            qkv = _odd_proj(x, mods, l, _bf(w_qkv_odd[o]), tables)
            a_p = _attn_context(qkv, attn_sink[o])
            a_s = _attn_latent(qkv, cache_k, cache_v, attn_sink[o], o)
            x = _odd_out(a_p, a_s, _bf(w_out_odd[o]), x, mods, l, ln_g[l, 0][None, :], ln_b[l, 0][None, :])
            kp = qkv[:NP_ROWS, HC * HD:(HC + KVH) * HD].reshape(BATCH, SEQ, KVH, HD).transpose(0, 2, 1, 3)
            vp = qkv[:NP_ROWS, (HC + KVH) * HD:].reshape(BATCH, SEQ, KVH, HD).transpose(0, 2, 1, 3)
            out_k.append(kp)
            out_v.append(vp)
        w_r = jnp.concatenate([w_erouter[l].transpose(1, 0, 2).reshape(D, N_EXPERTS), w_grp[l],
                               jnp.zeros((D, 128 - N_EXPERTS - N_GROUPS), F32)], axis=1)
        b_r = jnp.concatenate([b_erouter[l].reshape(-1), b_grp[l],
                               jnp.zeros((128 - N_EXPERTS - N_GROUPS,), F32)])[None, :]
        x = _moe_dense(x, mods, l, w_r, b_r,
                       _bf(w_gate[l]).reshape(N_GROUPS, EPG, D, EXPERT_FF),
                       _bf(w_up[l]).reshape(N_GROUPS, EPG, D, EXPERT_FF),
                       _bf(w_down[l]).reshape(N_GROUPS, EPG * EXPERT_FF, D),
                       ln_g[l, 1][None, :], ln_b[l, 1][None, :], split_out=(l == DEPTH - 1))
    return (x[0].reshape(BATCH, SEQ, D), x[1].reshape(DEC_BATCH, DEC_SEQ, D),
            jnp.stack(out_mc, 1), jnp.stack(out_mn, 1), jnp.stack(out_mm, 1), jnp.stack(out_ds, 1),
            jnp.stack(out_k, 1), jnp.stack(out_v, 1))
```

```python
import functools

import numpy as np
import jax
import jax.numpy as jnp
from jax import lax
from jax.experimental import pallas as pl
from jax.experimental.pallas import tpu as pltpu

F32 = jnp.float32
BF16 = jnp.bfloat16
HI = lax.Precision.HIGHEST

D = 1024
BATCH = 32
SEQ = 256
DEPTH = 4
DEC_BATCH = 2
DEC_SEQ = 4096
PAST_LEN = 512
GRID_W = 64
N_EVEN = 2
N_ODD = 2
HA = 4
HB = 4
DH = 128
CHUNK = 64
HC = 16
KVH = 4
HD = 64
WINDOW = 128
QBLOCK = 128
ROPE_THETA = 10000.0
N_GROUPS = 4
EPG = 4
N_EXPERTS = 16
EXPERT_FF = 256
DN_ALPHA = (2 * DEPTH) ** 0.25
LN_EPS = 1e-5

NP_ROWS = BATCH * SEQ
NS_ROWS = DEC_BATCH * DEC_SEQ
T = NP_ROWS + NS_ROWS
N_MOD_ROWS = 8
EVEN_W = 4224
QKV_W = (HC + 2 * KVH) * HD

P_CHUNKS = SEQ // CHUNK
S_CHUNKS = DEC_SEQ // CHUNK
P_STEPS = BATCH * P_CHUNKS
S_STEPS = DEC_BATCH * S_CHUNKS
P_BLOCKS = NP_ROWS // CHUNK

VMEM_LIMIT = 48 * 1024 * 1024


def _params(n_axes, vmem_limit=VMEM_LIMIT):
    return pltpu.CompilerParams(dimension_semantics=("arbitrary",) * n_axes,
                                vmem_limit_bytes=vmem_limit)


def _mm(a, b, prec=None):
    return lax.dot_general(a, b, (((1,), (0,)), ((), ())), precision=prec, preferred_element_type=F32)


def _mm_nt(a, b, prec=None):
    return lax.dot_general(a, b, (((1,), (1,)), ((), ())), precision=prec, preferred_element_type=F32)


def _mm_tn(a, b, prec=None):
    return lax.dot_general(a, b, (((0,), (0,)), ((), ())), precision=prec, preferred_element_type=F32)


def _bf(x):
    return x.astype(BF16)


def _sigmoid(x):
    return 1.0 / (1.0 + jnp.exp(-x))


def _softplus(x):
    return jnp.maximum(x, 0.0) + jnp.log1p(jnp.exp(-jnp.abs(x)))


def _log_sigmoid(x):
    return jnp.minimum(x, 0.0) - jnp.log1p(jnp.exp(-jnp.abs(x)))


def _mod_row(tile, tm):
    npt = NP_ROWS // tm
    per = DEC_SEQ // tm
    return jnp.where(tile < npt, 0, 1 + (tile - npt) // per)


def _mod_spec(layer, chunk, tm):
    def imap(i, *_):
        return ((layer * N_MOD_ROWS + _mod_row(i, tm)) * 6 + chunk, 0, 0)
    return pl.BlockSpec((1, 1, D), imap)


def _modulation_kernel(c_ref, w_ref, b_ref, o_ref):
    x = c_ref[...]
    s = x * _sigmoid(x)
    o_ref[0] = _mm(s, w_ref[0], HI) + b_ref[0]


def _modulation(cvecs, w_mod, b_mod):
    out = pl.pallas_call(
        _modulation_kernel,
        out_shape=jax.ShapeDtypeStruct((DEPTH, N_MOD_ROWS, 6 * D), F32),
        grid=(DEPTH, 6),
        in_specs=[pl.BlockSpec((N_MOD_ROWS, D), lambda l, j: (0, 0)),
                  pl.BlockSpec((1, D, D), lambda l, j: (l, 0, j)),
                  pl.BlockSpec((1, 1, D), lambda l, j: (l * 6 + j, 0, 0))],
        out_specs=pl.BlockSpec((1, N_MOD_ROWS, D), lambda l, j: (l, 0, j)),
        compiler_params=_params(2),
        name="modulation",
    )(cvecs, w_mod, b_mod.reshape(DEPTH * 6, 1, D))
    return out.reshape(DEPTH * N_MOD_ROWS * 6, 1, D)


def _x_operands(x, tm):
    if isinstance(x, tuple):
        npt = NP_ROWS // tm
        return ([pl.BlockSpec((tm, D), lambda i, *_: (jnp.minimum(i, npt - 1), 0)),
                 pl.BlockSpec((tm, D), lambda i, *_: (jnp.maximum(i - npt, 0), 0))], list(x))
    return [pl.BlockSpec((tm, D), lambda i, *_: (i, 0))], [x]


def _load_x(x_refs, tm):
    if len(x_refs) == 1:
        return x_refs[0][...]
    return jnp.where(pl.program_id(0) < NP_ROWS // tm, x_refs[0][...], x_refs[1][...])


def _proj_kernel(*refs, tm, n_x):
    x_refs, (sh_ref, sc_ref, w_ref, o_ref) = refs[:n_x], refs[n_x:]
    xm = _load_x(x_refs, tm) * (1.0 + sc_ref[0]) + sh_ref[0]
    o_ref[...] = _mm(_bf(xm), w_ref[...])


def _even_proj(x, mods, layer, w):
    tm = 256
    x_specs, x_arrays = _x_operands(x, tm)
    return pl.pallas_call(
        functools.partial(_proj_kernel, tm=tm, n_x=len(x_arrays)),
        out_shape=jax.ShapeDtypeStruct((T, EVEN_W), F32),
        grid=(T // tm,),
        in_specs=x_specs + [_mod_spec(layer, 0, tm), _mod_spec(layer, 1, tm),
                            pl.BlockSpec((D, EVEN_W), lambda i: (0, 0))],
        out_specs=pl.BlockSpec((tm, EVEN_W), lambda i: (i, 0)),
        compiler_params=_params(1),
        name="even_proj",
    )(*x_arrays, mods, mods, w)


def _qkv_kernel(x_ref, sh_ref, sc_ref, w_ref, cos_ref, sa_ref, sb_ref, kc_in, vc_in, o_ref, kc_ref, vc_ref, *, tm):
    del kc_in, vc_in
    i = pl.program_id(0)
    xm = x_ref[...] * (1.0 + sc_ref[0]) + sh_ref[0]
    acc = _mm(_bf(xm), w_ref[...])

    @pl.when(i < NP_ROWS // tm)
    def _():
        for b in range(tm // SEQ):
            for kv in range(KVH):
                k0 = HC * HD + kv * HD
                v0 = (HC + KVH) * HD + kv * HD
                kc_ref[b, 0, kv] = acc[b * SEQ:(b + 1) * SEQ, k0:k0 + HD]
                vc_ref[b, 0, kv] = acc[b * SEQ:(b + 1) * SEQ, v0:v0 + HD]

    n_q = HC * HD // 128
    n_k = KVH * HD // 128
    is_latent = i >= NP_ROWS // tm
    cos = jnp.where(is_latent, cos_ref[...], 1.0)
    sa = jnp.where(is_latent, sa_ref[...], 0.0)
    sb = jnp.where(is_latent, sb_ref[...], 0.0)
    for g in range(n_q + n_k):
        blk = acc[:, g * 128:(g + 1) * 128]
        if g < n_q:
            blk = blk * (HD ** -0.5)
        rot = blk * cos + pltpu.roll(blk, 112, 1) * sa + pltpu.roll(blk, 16, 1) * sb
        o_ref[:, g * 128:(g + 1) * 128] = rot
    o_ref[:, (n_q + n_k) * 128:] = acc[:, (n_q + n_k) * 128:]


def _rope_tables():
    half = HD // 4
    inv = np.float32(ROPE_THETA) ** (-np.arange(half, dtype=np.float32) / np.float32(half))
    pos = np.arange(DEC_SEQ)
    row = (pos // GRID_W).astype(np.float32)[:, None] * inv[None, :]
    col = (pos % GRID_W).astype(np.float32)[:, None] * inv[None, :]
    cos = np.concatenate([np.cos(row), np.cos(row), np.cos(col), np.cos(col)], axis=-1)
    sin = np.concatenate([np.sin(row), np.sin(row), np.sin(col), np.sin(col)], axis=-1)
    first = (np.arange(HD) % 32) < 16
    sa = np.where(first, -sin, 0.0)
    sb = np.where(first, 0.0, sin)
    tile2 = lambda t: jnp.asarray(np.concatenate([t, t], axis=-1), F32)
    return tile2(cos), tile2(sa), tile2(sb)


def _odd_proj(x, mods, layer, w, tables, cache_k, cache_v, o):
    tm = 512
    npt = NP_ROWS // tm
    per = DEC_SEQ // tm
    tab_spec = pl.BlockSpec((tm, 128), lambda i: (jnp.where(i < npt, 0, (i - npt) % per), 0))
    cache_spec = pl.BlockSpec((tm // SEQ, 1, KVH, SEQ, HD), lambda i: (jnp.minimum(i, npt - 1), o, 0, 0, 0))
    cache_shape = jax.ShapeDtypeStruct((BATCH, N_ODD, KVH, SEQ, HD), F32)
    return pl.pallas_call(
        functools.partial(_qkv_kernel, tm=tm),
        out_shape=(jax.ShapeDtypeStruct((T, QKV_W), F32), cache_shape, cache_shape),
        grid=(T // tm,),
        in_specs=[pl.BlockSpec((tm, D), lambda i: (i, 0)),
                  _mod_spec(layer, 0, tm), _mod_spec(layer, 1, tm),
                  pl.BlockSpec((D, QKV_W), lambda i: (0, 0)),
                  tab_spec, tab_spec, tab_spec,
                  pl.BlockSpec(memory_space=pl.ANY), pl.BlockSpec(memory_space=pl.ANY)],
        out_specs=(pl.BlockSpec((tm, QKV_W), lambda i: (i, 0)), cache_spec, cache_spec),
        input_output_aliases={7: 1, 8: 2},
        compiler_params=_params(1),
        name="odd_qkv_proj",
    )(x, mods, mods, w, *tables, cache_k, cache_v)


PREP_ROWS = 256


def _delta_prep_kernel(x_ref, prev_ref, next_ref, w_ref, o_ref):
    i = pl.program_id(0)
    npb = NP_ROWS // PREP_ROWS
    per = DEC_SEQ // PREP_ROWS
    is_latent = i >= npb
    pos = (i - npb) % per
    has_prev = jnp.logical_and(is_latent, pos > 0)
    has_next = jnp.logical_and(is_latent, pos < per - 1)
    x = x_ref[...]
    w = w_ref[...]
    rows = lax.broadcasted_iota(jnp.int32, x.shape, 0)
    prev_row = jnp.where(has_prev, prev_ref[7:8, :], 0.0)
    next_row = jnp.where(has_next, next_ref[0:1, :], 0.0)
    xm1 = jnp.where(rows == 0, prev_row, pltpu.roll(x, 1, 0))
    xp1 = jnp.where(rows == PREP_ROWS - 1, next_row, pltpu.roll(x, PREP_ROWS - 1, 0))
    y = xm1 * w[0:1, :] + x * w[1:2, :] + xp1 * w[2:3, :]
    y = y * _sigmoid(y)
    for h in range(3 * HB):
        yh = y[:, h * DH:(h + 1) * DH]
        if h < 2 * HB:
            inv = lax.rsqrt(jnp.sum(yh * yh, axis=-1, keepdims=True) + 1e-6)
            yh = yh * (inv * (DH ** -0.5) if h < HB else inv)
        o_ref[:, h * DH:(h + 1) * DH] = yh


def _delta_prep(proj, conv_w):
    nblk = T // PREP_ROWS
    sub = PREP_ROWS // 8
    last8 = T // 8 - 1
    return pl.pallas_call(
        _delta_prep_kernel,
        out_shape=jax.ShapeDtypeStruct((T, 3 * 512), F32),
        grid=(nblk,),
        in_specs=[pl.BlockSpec((PREP_ROWS, 3 * 512), lambda i: (i, 0)),
                  pl.BlockSpec((8, 3 * 512), lambda i: (jnp.maximum(i * sub - 1, 0), 0)),
                  pl.BlockSpec((8, 3 * 512), lambda i: (jnp.minimum((i + 1) * sub, last8), 0)),
                  pl.BlockSpec((3, 3 * 512), lambda i: (0, 0))],
        out_specs=pl.BlockSpec((PREP_ROWS, 3 * 512), lambda i: (i, 0)),
        compiler_params=_params(1),
        name="delta_prep",
    )(proj, proj, proj, conv_w)


assert P_STEPS == S_STEPS
SCAN_STEPS = P_STEPS


def _bwd_local(s, nc):
    return (s // nc) * nc + nc - 1 - s % nc


def _scan_blocks():
    return (lambda s: s, lambda s: _bwd_local(s, P_CHUNKS),
            lambda s: P_BLOCKS + s, lambda s: P_BLOCKS + _bwd_local(s, S_CHUNKS))


def _scan_specs(xcol):
    gcol = EVEN_W // 128 - 1
    blocks = _scan_blocks()
    return ([pl.BlockSpec((CHUNK, 3 * 512), lambda s, f=f: (f(s), xcol)) for f in blocks]
            + [pl.BlockSpec((CHUNK, 128), lambda s, f=f: (f(s), gcol)) for f in blocks]
            + [pl.BlockSpec((1, 16, 128), lambda s, f=f: (f(s), 0, 0)) for f in blocks])


def _scan_out_specs():
    local = (lambda s: s, lambda s: _bwd_local(s, P_CHUNKS), lambda s: s, lambda s: _bwd_local(s, S_CHUNKS))
    return [pl.BlockSpec((CHUNK, 512), lambda s, f=f: (f(s), 0)) for f in local]


def _mlstm_kernel(xpf_ref, xpb_ref, xsf_ref, xsb_ref, gpf_ref, gpb_ref, gsf_ref, gsb_ref,
                  rpf_ref, rpb_ref, rsf_ref, rsb_ref,
                  brow_ref, b2_ref, c0_ref, n0_ref, m0_ref,
                  hpf_ref, hpb_ref, hsf_ref, hsb_ref, c_out, n_out, m_out,
                  cp_s, np_s, mp_s, cs_s, ns_s, ms_s):
    s = pl.program_id(0)
    jp = s % P_CHUNKS
    js = s % S_CHUNKS

    @pl.when(jp == 0)
    def _():
        cp_s[...] = jnp.zeros_like(cp_s)
        np_s[...] = jnp.zeros_like(np_s)
        mp_s[...] = jnp.zeros_like(mp_s)

    @pl.when(js == 0)
    def _():
        for d in range(2):
            for h in range(HA):
                cs_s[d * HA + h] = c0_ref[0, 0, d, h]
                ns_s[d * HA + h] = jnp.broadcast_to(n0_ref[0, 0, d, h:h + 1, :], (DH, 128)).T
        ms_s[...] = m0_ref[0, 0]

    states = ((cp_s, np_s, mp_s), (cs_s, ns_s, ms_s))

    row = lax.broadcasted_iota(jnp.int32, (CHUNK, 128), 0)
    lane = lax.broadcasted_iota(jnp.int32, (CHUNK, 128), 1)
    left = lane < CHUNK
    lcol = jnp.where(left, lane, lane - CHUNK)
    r64 = lax.broadcasted_iota(jnp.int32, (CHUNK, CHUNK), 0)
    c64 = lax.broadcasted_iota(jnp.int32, (CHUNK, CHUNK), 1)
    r128 = lax.broadcasted_iota(jnp.int32, (128, 128), 0)
    c128 = lax.broadcasted_iota(jnp.int32, (128, 128), 1)
    same_half = (r128 < CHUNK) == (c128 < CHUNK)
    neg = -jnp.inf
    ins = ((xpf_ref, gpf_ref, rpf_ref, hpf_ref), (xpb_ref, gpb_ref, rpb_ref, hpb_ref),
           (xsf_ref, gsf_ref, rsf_ref, hsf_ref), (xsb_ref, gsb_ref, rsb_ref, hsb_ref))

    sel_r = lax.broadcasted_iota(jnp.int32, (128, HA * 128), 0)
    sel_h = lax.broadcasted_iota(jnp.int32, (128, HA * 128), 1) // 128
    ones_lr = lax.broadcasted_iota(jnp.int32, (256, 256), 0)
    ones_lc = lax.broadcasted_iota(jnp.int32, (256, 256), 1)
    half_sum = (((ones_lr % 128) < CHUNK) == (ones_lc < 128)).astype(BF16)
    ones_l = jnp.ones((CHUNK, 128), BF16)
    rows256 = lax.broadcasted_iota(jnp.int32, (CHUNK, 256), 0)

    def running_max(x, d):
        sh = 1
        while sh < CHUNK:
            if d == 0:
                x = jnp.maximum(x, jnp.where(rows256 >= sh, pltpu.roll(x, sh, 0), neg))
            else:
                x = jnp.maximum(x, jnp.where(rows256 < CHUNK - sh, pltpu.roll(x, CHUNK - sh, 0), neg))
            sh *= 2
        return x

    gate = []
    for u in range(4):
        d = u % 2
        gc_ref, gr_ref = ins[u][1], ins[u][2]
        if d == 0:
            incl_p, tri_c, tri_r = lcol <= row, c64 <= r64, jnp.logical_and(same_half, r128 <= c128)
        else:
            incl_p, tri_c, tri_r = lcol >= row, c64 >= r64, jnp.logical_and(same_half, r128 >= c128)
        gc = gc_ref[...] + brow_ref[...]
        gr = gr_ref[0] + b2_ref[...]
        i_rep = _replicate(gc, (sel_r == d * 8 + sel_h).astype(BF16))
        f_rep = _replicate(_log_sigmoid(gc), (sel_r == d * 8 + 4 + sel_h).astype(BF16))
        b_rep = _prefix_cols(tri_c.astype(BF16), f_rep)
        cs_row = _prefix_rows(_log_sigmoid(gr), tri_r.astype(BF16))
        gate.append((incl_p, gr, cs_row, i_rep, b_rep))

    st = []
    for u in range(4):
        d = u % 2
        c_s, n_s, m_s = states[u // 2]
        for hp in range(HA // 2):
            x_ref = ins[u][0]
            incl_p, gr, cs_row, i_rep, b_rep = gate[u]
            end = CHUNK - 1 if d == 0 else 0
            heads = []
            for h in (2 * hp, 2 * hp + 1):
                c = d * HA + h
                heads.append(dict(c=c, h=h, q=x_ref[:, h * DH:(h + 1) * DH],
                                  k=x_ref[:, 512 + h * DH:512 + (h + 1) * DH] * (DH ** -0.5),
                                  v=x_ref[:, 1024 + h * DH:1024 + (h + 1) * DH],
                                  i=i_rep[:, h * 128:(h + 1) * 128], b=b_rep[:, h * 128:(h + 1) * 128],
                                  m=m_s[c:c + 1, :]))
            ha, hb_ = heads
            i_row = gr[d * 4 + hp:d * 4 + hp + 1, :]
            b_row = cs_row[d * 4 + 2 + hp:d * 4 + 3 + hp, :]
            run = running_max(jnp.concatenate([ha["i"] - ha["b"], hb_["i"] - hb_["b"]], axis=1), d)
            for idx, hd in enumerate(heads):
                top = jnp.maximum(hd["m"], run[:, idx * 128:(idx + 1) * 128])
                hd["m_t"] = hd["b"] + top
                hd["w_inter"] = jnp.exp(hd["m"] - top)
            b_p = jnp.where(left, ha["b"], hb_["b"])
            m_t_p = jnp.where(left, ha["m_t"], hb_["m_t"])
            dmat = jnp.where(incl_p, b_p - b_row + i_row, neg)
            q_cat = _bf(jnp.concatenate([ha["q"], hb_["q"]], axis=1))
            k_bd = _bf(_block_diag2(ha["k"], hb_["k"]))
            st.append(dict(u=u, d=d, heads=heads, sc=_mm_nt(q_cat, k_bd) * jnp.exp(dmat - m_t_p)))

    for p in st:
        ha, hb_ = p["heads"]
        h_ref = ins[p["u"]][3]
        c_s, n_s, m_s = states[p["u"] // 2]
        end = CHUNK - 1 if p["d"] == 0 else 0
        sc = p["sc"]
        sv = _mm(_bf(sc), _bf(_block_diag2(ha["v"], hb_["v"])))
        dens = _mm(jnp.concatenate(_split2(sc), axis=1), half_sum)
        for idx, hd in enumerate((ha, hb_)):
            c, h, q, k = hd["c"], hd["h"], hd["q"], hd["k"]
            c_mat = c_s[c]
            n_mat = n_s[c]
            qb = _bf(q)
            num = sv[:, idx * DH:(idx + 1) * DH] + hd["w_inter"] * _mm(qb, _bf(c_mat))
            den = dens[:, idx * 128:(idx + 1) * 128] + hd["w_inter"] * _mm(qb, _bf(n_mat))
            h_ref[:, h * DH:(h + 1) * DH] = num / jnp.maximum(jnp.abs(den), jnp.exp(-hd["m_t"]))
            b_last = hd["b"][end:end + 1, :]
            g_end = b_last - hd["b"] + hd["i"]
            m_new = jnp.maximum(b_last + hd["m"], jnp.max(g_end, axis=0, keepdims=True))
            kwb = _bf(k * jnp.exp(g_end - m_new))
            decay = jnp.exp(b_last + hd["m"] - m_new)
            c_s[c] = decay * c_mat + _mm_tn(kwb, _bf(hd["v"]))
            n_s[c] = decay * n_mat + _mm_tn(kwb, ones_l)
            m_s[c:c + 1, :] = m_new

    @pl.when(jp == P_CHUNKS - 1)
    def _():
        for d in range(2):
            for h in range(HA):
                c_out[0, d, h] = cp_s[d * HA + h]
                n_out[0, d, h:h + 1, :] = np_s[d * HA + h].T[0:1, :]
        m_out[0] = mp_s[...]


def _mlstm_scan(proj, gates_t2, bias_row, bias2, c0, n0, m0, e):
    const2 = lambda s: (0, 0)
    in_specs = (_scan_specs(1)
                + [pl.BlockSpec((1, 128), const2), pl.BlockSpec((16, 128), const2),
                   pl.BlockSpec((1, 1, 2, HA, DH, DH), lambda s: (s // S_CHUNKS, e, 0, 0, 0, 0)),
                   pl.BlockSpec((1, 1, 2, HA, DH), lambda s: (s // S_CHUNKS, e, 0, 0, 0)),
                   pl.BlockSpec((1, 1, 2 * HA, 128), lambda s: (s // S_CHUNKS, e, 0, 0))])
    half = jax.ShapeDtypeStruct((NP_ROWS, HA * DH), F32)
    out_shape = (half, half, half, half,
                 jax.ShapeDtypeStruct((BATCH, 2, HA, DH, DH), F32),
                 jax.ShapeDtypeStruct((BATCH, 2, HA, DH), F32),
                 jax.ShapeDtypeStruct((BATCH, 2 * HA, 128), F32))
    out_specs = _scan_out_specs() + [
        pl.BlockSpec((1, 2, HA, DH, DH), lambda s: (s // P_CHUNKS, 0, 0, 0, 0)),
        pl.BlockSpec((1, 2, HA, DH), lambda s: (s // P_CHUNKS, 0, 0, 0)),
        pl.BlockSpec((1, 2 * HA, 128), lambda s: (s // P_CHUNKS, 0, 0))]
    state = [pltpu.VMEM((2 * HA, DH, DH), F32), pltpu.VMEM((2 * HA, DH, 128), F32), pltpu.VMEM((2 * HA, 128), F32)]
    return pl.pallas_call(
        _mlstm_kernel,
        out_shape=out_shape,
        grid=(SCAN_STEPS,),
        in_specs=in_specs,
        out_specs=out_specs,
        scratch_shapes=state + state,
        compiler_params=_params(1),
        name="mlstm_scan",
    )(proj, proj, proj, proj, proj, proj, proj, proj, gates_t2, gates_t2, gates_t2, gates_t2,
      bias_row, bias2, c0, n0, m0)


def _split2(x):
    hi = _bf(x)
    return hi, _bf(x - hi.astype(F32))


def _split3(x):
    h1 = _bf(x)
    r1 = x - h1.astype(F32)
    h2 = _bf(r1)
    return h1, h2, _bf(r1 - h2.astype(F32))


def _lhs3(x):
    hi, lo = _split2(x)
    return jnp.concatenate([hi, lo, hi], axis=1)


def _rhs3(x):
    hi, lo = _split2(x)
    return jnp.concatenate([hi, hi, lo], axis=0)


def _prefix_cols(tri_bf, x):
    n = x.shape[1]
    r = _mm(tri_bf, jnp.concatenate(_split3(x), axis=1))
    return r[:, :n] + r[:, n:2 * n] + r[:, 2 * n:]


def _replicate(x, sel):
    return _mm(jnp.concatenate(_split3(x), axis=1), jnp.concatenate([sel, sel, sel], axis=0))


def _prefix_rows(x, tri_bf):
    m = x.shape[0]
    r = _mm(jnp.concatenate(_split3(x), axis=0), tri_bf)
    return r[:m] + r[m:2 * m] + r[2 * m:]


LEVELS = tuple(range(6))


def _block_diag2(a, b):
    z = jnp.zeros_like(a)
    return jnp.concatenate([jnp.concatenate([a, z], axis=1), jnp.concatenate([z, b], axis=1)], axis=0)


def _delta_kernel(xpf_ref, xpb_ref, xsf_ref, xsb_ref, gpf_ref, gpb_ref, gsf_ref, gsb_ref,
                  rpf_ref, rpb_ref, rsf_ref, rsb_ref,
                  dtrow_ref, dt2_ref, narow_ref, na2_ref, s0_ref,
                  opf_ref, opb_ref, osf_ref, osb_ref, s_out, sp_s, ss_s):
    s = pl.program_id(0)
    jp = s % P_CHUNKS
    js = s % S_CHUNKS

    @pl.when(jp == 0)
    def _():
        sp_s[...] = jnp.zeros_like(sp_s)

    @pl.when(js == 0)
    def _():
        for d in range(2):
            for h in range(HB):
                ss_s[d * HB + h] = s0_ref[0, 0, d, h]

    states = (sp_s, ss_s)

    row = lax.broadcasted_iota(jnp.int32, (CHUNK, 128), 0)
    lane = lax.broadcasted_iota(jnp.int32, (CHUNK, 128), 1)
    left = lane < CHUNK
    lcol = jnp.where(left, lane, lane - CHUNK)
    eye_p = (lcol == row).astype(F32)
    r64 = lax.broadcasted_iota(jnp.int32, (CHUNK, CHUNK), 0)
    c64 = lax.broadcasted_iota(jnp.int32, (CHUNK, CHUNK), 1)
    r128 = lax.broadcasted_iota(jnp.int32, (128, 128), 0)
    c128 = lax.broadcasted_iota(jnp.int32, (128, 128), 1)
    same_half = (r128 < CHUNK) == (c128 < CHUNK)
    ins = ((xpf_ref, gpf_ref, rpf_ref, opf_ref), (xpb_ref, gpb_ref, rpb_ref, opb_ref),
           (xsf_ref, gsf_ref, rsf_ref, osf_ref), (xsb_ref, gsb_ref, rsb_ref, osb_ref))

    gate = []
    for u in range(4):
        d = u % 2
        gc_ref, gr_ref = ins[u][1], ins[u][2]
        if d == 0:
            incl_p, strict_p = lcol <= row, lcol < row
            tri_c, tri_r = c64 <= r64, jnp.logical_and(same_half, r128 <= c128)
        else:
            incl_p, strict_p = lcol >= row, lcol > row
            tri_c, tri_r = c64 >= r64, jnp.logical_and(same_half, r128 >= c128)
        xc = gc_ref[...]
        xr = gr_ref[0]
        la_c = narow_ref[...] * _softplus(xc + dtrow_ref[...])
        la_r = na2_ref[...] * _softplus(xr + dt2_ref[...])
        g_c = _prefix_cols(tri_c.astype(BF16), la_c)
        g_r = _prefix_rows(la_r, tri_r.astype(BF16))
        gate.append((incl_p, strict_p, _sigmoid(xc), g_c, g_r))

    def bd_rhs(hi, lo, mask=None):
        top = left if mask is None else jnp.logical_and(left, mask)
        bot = jnp.logical_not(left) if mask is None else jnp.logical_and(jnp.logical_not(left), mask)
        zero = jnp.zeros_like(hi)
        blocks = [jnp.concatenate([jnp.where(top, x, zero), jnp.where(bot, x, zero)], axis=0) for x in (hi, lo)]
        return jnp.concatenate([blocks[0], blocks[0], blocks[1]], axis=0)

    off = ([], [])
    for lv in LEVELS:
        same = jnp.right_shift(row, lv + 1) == jnp.right_shift(lcol, lv + 1)
        r_hi = jnp.bitwise_and(jnp.right_shift(row, lv), 1) == 1
        c_hi = jnp.bitwise_and(jnp.right_shift(lcol, lv), 1) == 1
        off[0].append(same & r_hi & jnp.logical_not(c_hi))
        off[1].append(same & jnp.logical_not(r_hi) & c_hi)

    pairs = [(u, hp) for u in range(4) for hp in range(HB // 2)]
    st = []
    for u, hp in pairs:
        d = u % 2
        x_ref = ins[u][0]
        incl_p, strict_p, beta_c, g_c, g_r = gate[u]
        end = CHUNK - 1 if d == 0 else 0
        heads = []
        for h in (2 * hp, 2 * hp + 1):
            ib = 16 + d * HB + h
            ia = 24 + d * HB + h
            gcol = g_c[:, ia:ia + 1]
            bc = beta_c[:, ib:ib + 1]
            q = x_ref[:, h * DH:(h + 1) * DH]
            k = x_ref[:, 512 + h * DH:512 + (h + 1) * DH]
            v = x_ref[:, 1024 + h * DH:1024 + (h + 1) * DH]
            heads.append(dict(h=h, gcol=gcol, bc=bc, q=q, k=k, v=v, kb=k * bc, eg=jnp.exp(gcol),
                              g_last=gcol[end:end + 1, :]))
        ha, hb_ = heads
        r = 12 + d * 2 + hp
        gcol_p = jnp.where(left, ha["gcol"], hb_["gcol"])
        decay = jnp.exp(jnp.where(incl_p, gcol_p - g_r[r:r + 1, :], -jnp.inf))
        k_bd = _bf(_block_diag2(ha["k"], hb_["k"]))
        kb_cat = _bf(jnp.concatenate([ha["kb"], hb_["kb"]], axis=1))
        q_cat = _bf(jnp.concatenate([ha["q"], hb_["q"]], axis=1))
        a_mat = jnp.where(strict_p, _mm_nt(kb_cat, k_bd) * decay, 0.0)
        qk = _mm_nt(q_cat, k_bd) * decay
        a_hi, a_lo = _split2(a_mat)
        st.append(dict(u=u, d=d, heads=heads, t=eye_p - jnp.where(off[d][0], a_mat, 0.0), qk=qk,
                       am=[bd_rhs(a_hi, a_lo, m) for m in off[d][1:]]))

    for li in range(len(LEVELS) - 1):
        for p in st:
            p["t_parts"] = _split2(p["t"])
            t_hi, t_lo = p["t_parts"]
            p["w"] = _mm(jnp.concatenate([t_hi, t_lo, t_hi], axis=1), p["am"][li])
        for p in st:
            p["t"] = p["t"] - _mm(_lhs3(p["w"]), bd_rhs(*p["t_parts"]))

    for p in st:
        ha, hb_ = p["heads"]
        o_ref = ins[p["u"]][3]
        s_s = states[p["u"] // 2]
        rhs_a = jnp.concatenate([ha["v"] * ha["bc"], ha["kb"] * ha["eg"]], axis=1)
        rhs_b = jnp.concatenate([hb_["v"] * hb_["bc"], hb_["kb"] * hb_["eg"]], axis=1)
        (a_hi, a_lo), (b_hi, b_lo) = _split2(rhs_a), _split2(rhs_b)
        bd_hi, bd_lo = _block_diag2(a_hi, b_hi), _block_diag2(a_lo, b_lo)
        sol = _mm(_lhs3(p["t"]), jnp.concatenate([bd_hi, bd_hi, bd_lo], axis=0))
        vn = []
        for idx, hd in enumerate((ha, hb_)):
            c = p["d"] * HB + hd["h"]
            s_mat = s_s[c]
            sbf = _bf(s_mat)
            so = sol[:, idx * 2 * DH:(idx + 1) * 2 * DH]
            v_new = so[:, :DH] - _mm(_bf(so[:, DH:]), sbf)
            vn.append(v_new)
            hd["o1"] = _mm(_bf(hd["q"] * hd["eg"]), sbf)
            s_s[c] = (jnp.exp(hd["g_last"]) * s_mat
                      + _mm_tn(_bf(hd["k"] * jnp.exp(hd["g_last"] - hd["gcol"])), _bf(v_new)))
        o2 = _mm(_bf(p["qk"]), _bf(_block_diag2(vn[0], vn[1])))
        for idx, hd in enumerate((ha, hb_)):
            o_ref[:, hd["h"] * DH:(hd["h"] + 1) * DH] = hd["o1"] + o2[:, idx * DH:(idx + 1) * DH]

    @pl.when(jp == P_CHUNKS - 1)
    def _():
        for d in range(2):
            for h in range(HB):
                s_out[0, d, h] = sp_s[d * HB + h]


def _delta_scan(qkv, proj, gates_t2, dt_row, dt2, na_row, na2, s0, e):
    const2 = lambda s: (0, 0)
    in_specs = (_scan_specs(0)
                + [pl.BlockSpec((1, 128), const2), pl.BlockSpec((16, 128), const2),
                   pl.BlockSpec((1, 128), const2), pl.BlockSpec((16, 128), const2),
                   pl.BlockSpec((1, 1, 2, HB, DH, DH), lambda s: (s // S_CHUNKS, e, 0, 0, 0, 0))])
    half = jax.ShapeDtypeStruct((NP_ROWS, HB * DH), F32)
    out_shape = (half, half, half, half, jax.ShapeDtypeStruct((BATCH, 2, HB, DH, DH), F32))
    out_specs = _scan_out_specs() + [pl.BlockSpec((1, 2, HB, DH, DH), lambda s: (s // P_CHUNKS, 0, 0, 0, 0))]
    return pl.pallas_call(
        _delta_kernel,
        out_shape=out_shape,
        grid=(SCAN_STEPS,),
        in_specs=in_specs,
        out_specs=out_specs,
        scratch_shapes=[pltpu.VMEM((2 * HB, DH, DH), F32), pltpu.VMEM((2 * HB, DH, DH), F32)],
        compiler_params=_params(1),
        name="delta_scan",
    )(qkv, qkv, qkv, qkv, proj, proj, proj, proj, gates_t2, gates_t2, gates_t2, gates_t2,
      dt_row, dt2, na_row, na2, s0)


def _residual_ln(x, gate, y, g, b):
    r = DN_ALPHA * x + gate * y
    mu = jnp.mean(r, axis=-1, keepdims=True)
    var = jnp.mean(jnp.square(r - mu), axis=-1, keepdims=True)
    return (r - mu) * lax.rsqrt(var + LN_EPS) * g + b


def _even_out_kernel(*refs, tm, n_x):
    x_refs = refs[:n_x]
    (hpf_ref, hpb_ref, hsf_ref, hsb_ref, opf_ref, opb_ref, osf_ref, osb_ref,
     oa_ref, zb_ref, mg_ref, dg_ref, w_ref, gate_ref, g_ref, b_ref, o_ref) = refs[n_x:]
    is_p = pl.program_id(0) < NP_ROWS // tm
    hf_ref, hb_ref, of_ref, ob_ref = (
        lambda sl, p=p, q=q: jnp.where(is_p, p[:, sl], q[:, sl])
        for p, q in ((hpf_ref, hsf_ref), (hpb_ref, hsb_ref), (opf_ref, osf_ref), (opb_ref, osb_ref)))
    parts = []
    for h in range(HA):
        sl = slice(h * DH, (h + 1) * DH)
        hh = hf_ref(sl) + hb_ref(sl)
        mu = jnp.mean(hh, axis=-1, keepdims=True)
        var = jnp.mean(jnp.square(hh - mu), axis=-1, keepdims=True)
        parts.append(_sigmoid(oa_ref[:, sl]) * ((hh - mu) * lax.rsqrt(var + LN_EPS) * mg_ref[:, sl]))
    for h in range(HB):
        sl = slice(h * DH, (h + 1) * DH)
        oo = of_ref(sl) + ob_ref(sl)
        z = zb_ref[:, sl]
        nrm = oo * lax.rsqrt(jnp.mean(jnp.square(oo), axis=-1, keepdims=True) + LN_EPS) * dg_ref[:, sl]
        parts.append(nrm * (z * _sigmoid(z)))
    a = jnp.concatenate(parts, axis=1)
    y = _mm(_bf(a), w_ref[...])
    o_ref[...] = _residual_ln(_load_x(x_refs, tm), gate_ref[0], y, g_ref[...], b_ref[...])


def _even_out(h_scan, o_scan, proj, mg, dg, w, x, mods, layer, ln_g, ln_b):
    tm = 512
    npt = NP_ROWS // tm
    row512 = lambda i: (i, 0)
    const2 = lambda i: (0, 0)
    p_spec = pl.BlockSpec((tm, 512), lambda i: (jnp.minimum(i, npt - 1), 0))
    s_spec = pl.BlockSpec((tm, 512), lambda i: (jnp.maximum(i - npt, 0), 0))
    x_specs, x_arrays = _x_operands(x, tm)
    return pl.pallas_call(
        functools.partial(_even_out_kernel, tm=tm, n_x=len(x_arrays)),
        out_shape=jax.ShapeDtypeStruct((T, D), F32),
        grid=(T // tm,),
        in_specs=x_specs + [p_spec, p_spec, s_spec, s_spec, p_spec, p_spec, s_spec, s_spec,
                            pl.BlockSpec((tm, 512), lambda i: (i, 6)),
                            pl.BlockSpec((tm, 512), lambda i: (i, 7)),
                            pl.BlockSpec((1, 512), const2), pl.BlockSpec((1, 512), const2),
                            pl.BlockSpec((D, D), const2),
                            _mod_spec(layer, 2, tm),
                            pl.BlockSpec((1, D), const2), pl.BlockSpec((1, D), const2)],
        out_specs=pl.BlockSpec((tm, D), row512),
        compiler_params=_params(1),
        name="even_out_ln",
    )(*x_arrays, *h_scan, *o_scan, proj, proj, mg, dg, w, mods, ln_g, ln_b)


def _odd_out_kernel(ap_ref, as_ref, w_ref, x_ref, gate_ref, g_ref, b_ref, o_ref, *, tm):
    a = jnp.where(pl.program_id(0) < NP_ROWS // tm, ap_ref[...], as_ref[...])
    y = _mm(_bf(a), w_ref[...])
    o_ref[...] = _residual_ln(x_ref[...], gate_ref[0], y, g_ref[...], b_ref[...])


def _odd_out(a_prompt, a_latent, w, x, mods, layer, ln_g, ln_b):
    tm = 512
    npt = NP_ROWS // tm
    row = lambda i: (i, 0)
    const2 = lambda i: (0, 0)
    return pl.pallas_call(
        functools.partial(_odd_out_kernel, tm=tm),
        out_shape=jax.ShapeDtypeStruct((T, D), F32),
        grid=(T // tm,),
        in_specs=[pl.BlockSpec((tm, D), lambda i: (jnp.minimum(i, npt - 1), 0)),
                  pl.BlockSpec((tm, D), lambda i: (jnp.maximum(i - npt, 0), 0)),
                  pl.BlockSpec((D, D), const2), pl.BlockSpec((tm, D), row),
                  _mod_spec(layer, 2, tm), pl.BlockSpec((1, D), const2), pl.BlockSpec((1, D), const2)],
        out_specs=pl.BlockSpec((tm, D), row),
        compiler_params=_params(1),
        name="odd_out_ln",
    )(a_prompt, a_latent, w, x, mods, ln_g, ln_b)


def _attn_ctx_kernel(qkv_ref, sink_ref, o_ref):
    ones = jnp.ones((SEQ, HD), BF16)
    for kv in range(KVH):
        k = _bf(qkv_ref[:, HC * HD + kv * HD:HC * HD + (kv + 1) * HD])
        v = _bf(qkv_ref[:, (HC + KVH) * HD + kv * HD:(HC + KVH) * HD + (kv + 1) * HD])
        v1 = jnp.concatenate([v, ones], axis=1)
        heads = range(kv * (HC // KVH), (kv + 1) * (HC // KVH))
        sts = [_mm_nt(k, _bf(qkv_ref[:, h * HD:(h + 1) * HD])) for h in heads]
        ms = [jnp.maximum(jnp.max(st, axis=0, keepdims=True), sink_ref[h]) for st, h in zip(sts, heads)]
        ovs = [_mm_tn(v1, _bf(jnp.exp(st - m))) for st, m in zip(sts, ms)]
        outs = [ov[:HD] / (ov[HD:HD + 1] + jnp.exp(sink_ref[h] - m)) for ov, m, h in zip(ovs, ms, heads)]
        for pr in range(HC // KVH // 2):
            lane0 = (kv * (HC // KVH) + 2 * pr) * HD
            o_ref[:, lane0:lane0 + 2 * HD] = jnp.concatenate(outs[2 * pr:2 * pr + 2], axis=0).T


def _attn_context(qkv, sink):
    return pl.pallas_call(
        _attn_ctx_kernel,
        out_shape=jax.ShapeDtypeStruct((NP_ROWS, HC * HD), F32),
        grid=(BATCH,),
        in_specs=[pl.BlockSpec((SEQ, QKV_W), lambda b: (b, 0)),
                  pl.BlockSpec(memory_space=pltpu.SMEM)],
        out_specs=pl.BlockSpec((SEQ, HC * HD), lambda b: (b, 0)),
        compiler_params=_params(1),
        name="attn_context",
    )(qkv, sink)


def _attn_lat_kernel(q_ref, kp_ref, kc_ref, kn_ref, vp_ref, vc_ref, vn_ref, ck_ref, cv_ref,
                     sink_ref, o_ref, bias_s):
    j = pl.program_id(1)
    nb = DEC_SEQ // QBLOCK
    cc = lax.broadcasted_iota(jnp.int32, (3 * QBLOCK, QBLOCK), 0)
    r = lax.broadcasted_iota(jnp.int32, (3 * QBLOCK, QBLOCK), 1)
    lo = jnp.where(j >= 1, 0, QBLOCK)
    hi = jnp.where(j <= nb - 2, 3 * QBLOCK, 2 * QBLOCK)
    ok = (jnp.abs(QBLOCK + r - cc) <= WINDOW) & (cc >= lo) & (cc < hi)
    bias_s[...] = jnp.where(ok, 0.0, -jnp.inf)
    for kv in range(KVH):
        ks = slice(kv * HD, (kv + 1) * HD)
        k_all = _bf(jnp.concatenate([ck_ref[0, 0, kv], kp_ref[:, ks], kc_ref[:, ks], kn_ref[:, ks]], axis=0))
        v_all = _bf(jnp.concatenate([cv_ref[0, 0, kv], vp_ref[:, ks], vc_ref[:, ks], vn_ref[:, ks]], axis=0))
        v1 = jnp.concatenate([v_all, jnp.ones_like(v_all)], axis=1)
        outs = []
        for h in range(kv * (HC // KVH), (kv + 1) * (HC // KVH)):
            sink = sink_ref[h]
            st = _mm_nt(k_all, _bf(q_ref[:, h * HD:(h + 1) * HD]))
            s_ctx = st[:PAST_LEN]
            s_loc = st[PAST_LEN:] + bias_s[...]
            m = jnp.maximum(jnp.maximum(jnp.max(s_ctx, axis=0, keepdims=True),
                                        jnp.max(s_loc, axis=0, keepdims=True)), sink)
            p = _bf(jnp.concatenate([jnp.exp(s_ctx - m), jnp.exp(s_loc - m)], axis=0))
            ov = _mm_tn(v1, p)
            outs.append(ov[:HD] / (ov[HD:HD + 1] + jnp.exp(sink - m)))
        for pr in range(HC // KVH // 2):
            lane0 = (kv * (HC // KVH) + 2 * pr) * HD
            o_ref[:, lane0:lane0 + 2 * HD] = jnp.concatenate(outs[2 * pr:2 * pr + 2], axis=0).T


def _attn_latent(qkv, cache_k, cache_v, sink, o):
    nb = DEC_SEQ // QBLOCK
    base = NP_ROWS // QBLOCK
    blk = lambda b, j: base + b * nb + j
    prev = lambda b, j: base + b * nb + jnp.maximum(j - 1, 0)
    nxt = lambda b, j: base + b * nb + jnp.minimum(j + 1, nb - 1)
    kcol, vcol = HC * HD // 256, HC * HD // 256 + 1
    cache_spec = pl.BlockSpec((1, 1, KVH, PAST_LEN, HD), lambda b, j: (b, o, 0, 0, 0))
    return pl.pallas_call(
        _attn_lat_kernel,
        out_shape=jax.ShapeDtypeStruct((NS_ROWS, HC * HD), F32),
        grid=(DEC_BATCH, nb),
        in_specs=[pl.BlockSpec((QBLOCK, HC * HD), lambda b, j: (blk(b, j), 0)),
                  pl.BlockSpec((QBLOCK, 256), lambda b, j: (prev(b, j), kcol)),
                  pl.BlockSpec((QBLOCK, 256), lambda b, j: (blk(b, j), kcol)),
                  pl.BlockSpec((QBLOCK, 256), lambda b, j: (nxt(b, j), kcol)),
                  pl.BlockSpec((QBLOCK, 256), lambda b, j: (prev(b, j), vcol)),
                  pl.BlockSpec((QBLOCK, 256), lambda b, j: (blk(b, j), vcol)),
                  pl.BlockSpec((QBLOCK, 256), lambda b, j: (nxt(b, j), vcol)),
                  cache_spec, cache_spec,
                  pl.BlockSpec(memory_space=pltpu.SMEM)],
        out_specs=pl.BlockSpec((QBLOCK, HC * HD), lambda b, j: (b * nb + j, 0)),
        scratch_shapes=[pltpu.VMEM((3 * QBLOCK, QBLOCK), F32)],
        compiler_params=_params(2),
        name="attn_latent",
    )(qkv, qkv, qkv, qkv, qkv, qkv, qkv, cache_k, cache_v, sink)


def _route(lg):
    lane = lax.broadcasted_iota(jnp.int32, lg.shape, 1)
    neg = -jnp.inf
    big = 1 << 20
    is_grp = jnp.logical_and(lane >= N_EXPERTS, lane < N_EXPERTS + N_GROUPS)
    mg = jnp.max(jnp.where(is_grp, lg, neg), axis=1, keepdims=True)
    g_lane = jnp.min(jnp.where(jnp.logical_and(is_grp, lg == mg), lane, big), axis=1, keepdims=True)
    g_w = 1.0 / jnp.sum(jnp.where(is_grp, jnp.exp(lg - mg), 0.0), axis=1, keepdims=True)
    g_idx = g_lane - N_EXPERTS
    in_grp = jnp.logical_and(lane >= g_idx * EPG, lane < (g_idx + 1) * EPG)
    v1 = jnp.max(jnp.where(in_grp, lg, neg), axis=1, keepdims=True)
    i1 = jnp.min(jnp.where(jnp.logical_and(in_grp, lg == v1), lane, big), axis=1, keepdims=True)
    rest = jnp.logical_and(in_grp, lane != i1)
    v2 = jnp.max(jnp.where(rest, lg, neg), axis=1, keepdims=True)
    i2 = jnp.min(jnp.where(jnp.logical_and(rest, lg == v2), lane, big), axis=1, keepdims=True)
    e2 = jnp.exp(v2 - v1)
    p1 = 1.0 / (1.0 + e2)
    p2 = e2 / (1.0 + e2)
    return jnp.where(lane == 0, i1.astype(F32),
                     jnp.where(lane == 1, i2.astype(F32),
                               jnp.where(lane == 2, p1 * g_w, jnp.where(lane == 3, p2 * g_w, 0.0))))


def _moe_dense_kernel(x_ref, sh_ref, sc_ref, wr_ref, br_ref, wg_ref, wu_ref, wd_ref, gate_ref, g_ref, b_ref,
                      *out_and_scratch, tm):
    *o_refs, xm_s, meta_s, acc_s = out_and_scratch
    grp = pl.program_id(1)

    @pl.when(grp == 0)
    def _():
        xm = x_ref[...] * (1.0 + sc_ref[0]) + sh_ref[0]
        meta_s[...] = _route(_mm(_lhs3(xm), _rhs3(wr_ref[...])) + br_ref[...])
        xm_s[...] = _bf(xm)
        acc_s[...] = jnp.zeros_like(acc_s)

    xm = xm_s[...]
    meta = meta_s[...]
    i1 = meta[:, 0:1].astype(jnp.int32)
    i2 = meta[:, 1:2].astype(jnp.int32)
    w1 = meta[:, 2:3]
    w2 = meta[:, 3:4]
    hid = []
    for e in range(EPG):
        eid = grp * EPG + e
        gate = jnp.where(i1 == eid, w1, 0.0) + jnp.where(i2 == eid, w2, 0.0)
        a = _mm(xm, wg_ref[0, e])
        u = _mm(xm, wu_ref[0, e])
        hid.append(_bf((a * _sigmoid(a)) * u * gate))
    acc_s[...] += _mm(jnp.concatenate(hid, axis=1), wd_ref[0])

    def result():
        return _residual_ln(x_ref[...], gate_ref[0], acc_s[...], g_ref[...], b_ref[...])

    last = grp == N_GROUPS - 1
    if len(o_refs) == 1:
        @pl.when(last)
        def _():
            o_refs[0][...] = result()
    else:
        is_p = pl.program_id(0) < NP_ROWS // tm

        @pl.when(jnp.logical_and(last, is_p))
        def _():
            o_refs[0][...] = result()

        @pl.when(jnp.logical_and(last, jnp.logical_not(is_p)))
        def _():
            o_refs[1][...] = result()


def _moe_dense(x, mods, layer, w_r, b_r, wg, wu, wd, ln_g, ln_b, split_out=False):
    tm = 1024
    npt = NP_ROWS // tm
    row = lambda i, g: (i, 0)
    const2 = lambda i, g: (0, 0)
    if split_out:
        half = jax.ShapeDtypeStruct((NP_ROWS, D), F32)
        out_shape = (half, half)
        out_specs = (pl.BlockSpec((tm, D), lambda i, g: (jnp.minimum(i, npt - 1), 0)),
                     pl.BlockSpec((tm, D), lambda i, g: (jnp.maximum(i - npt, 0), 0)))
    else:
        out_shape = jax.ShapeDtypeStruct((T, D), F32)
        out_specs = pl.BlockSpec((tm, D), row)
    return pl.pallas_call(
        functools.partial(_moe_dense_kernel, tm=tm),
        out_shape=out_shape,
        grid=(T // tm, N_GROUPS),
        in_specs=[pl.BlockSpec((tm, D), row), _mod_spec(layer, 3, tm), _mod_spec(layer, 4, tm),
                  pl.BlockSpec((D, 128), const2), pl.BlockSpec((1, 128), const2),
                  pl.BlockSpec((1, EPG, D, EXPERT_FF), lambda i, g: (g, 0, 0, 0)),
                  pl.BlockSpec((1, EPG, D, EXPERT_FF), lambda i, g: (g, 0, 0, 0)),
                  pl.BlockSpec((1, EPG * EXPERT_FF, D), lambda i, g: (g, 0, 0)),
                  _mod_spec(layer, 5, tm), pl.BlockSpec((1, D), const2), pl.BlockSpec((1, D), const2)],
        out_specs=out_specs,
        scratch_shapes=[pltpu.VMEM((tm, D), BF16), pltpu.VMEM((tm, 128), F32), pltpu.VMEM((tm, D), F32)],
        compiler_params=_params(2, VMEM_LIMIT + (2 * tm * D * 4 if split_out else 0)),
        name="moe_dense",
    )(x, mods, mods, w_r, b_r, wg, wu, wd, mods, ln_g, ln_b)


def _permute_even_w(w):
    a_end = 4 * HA * DH
    g_end = a_end + 4 * HA
    c_end = g_end + 3 * HB * DH
    z_end = c_end + HB * DH
    small = jnp.concatenate([w[:, a_end:g_end], w[:, z_end:]], axis=1)
    pad = jnp.zeros((w.shape[0], 128 - small.shape[1]), w.dtype)
    return jnp.concatenate([w[:, g_end:c_end], w[:, :a_end], w[:, c_end:z_end], small, pad], axis=1)


def _lane_row(vals, offset):
    return jnp.zeros((1, 128), F32).at[0, offset:offset + vals.shape[0]].set(vals.astype(F32))


def _pair_rows(vals, offset):
    v32 = jnp.zeros((32,), F32).at[offset:offset + vals.shape[0]].set(vals.astype(F32))
    return jnp.repeat(v32.reshape(16, 2), CHUNK, axis=1)


def kernel(x_prompt, x_sample, c, c_ctx, state_mlstm_c, state_mlstm_n, state_mlstm_m, state_delta, cache_k, cache_v, w_mod, b_mod, ln_g, ln_b, w_in_even, mlstm_gate_b, mlstm_norm_g, delta_conv_w, delta_a_log, delta_dt_bias, delta_norm_g, w_out_even, w_qkv_odd, attn_sink, w_out_odd, w_grp, b_grp, w_erouter, b_erouter, w_gate, w_up, w_down):
    x = (x_prompt.reshape(NP_ROWS, D), x_sample.reshape(NS_ROWS, D))
    cvecs = jnp.concatenate([c_ctx[None, :], c, jnp.zeros((N_MOD_ROWS - 1 - DEC_BATCH, D), F32)], axis=0)
    mods = _modulation(cvecs, w_mod, b_mod)
    tables = _rope_tables()
    m0_all = jnp.broadcast_to(state_mlstm_m.reshape(DEC_BATCH, N_EVEN, 2 * HA, 1), (DEC_BATCH, N_EVEN, 2 * HA, 128))

    out_mc, out_mn, out_mm, out_ds = [], [], [], []
    new_k = jnp.zeros((BATCH, N_ODD, KVH, SEQ, HD), F32)
    new_v = jnp.zeros((BATCH, N_ODD, KVH, SEQ, HD), F32)
    for l in range(DEPTH):
        if l % 2 == 0:
            e = l // 2
            proj = _even_proj(x, mods, l, _bf(_permute_even_w(w_in_even[e])))
            gates_t2 = (proj[:, EVEN_W - 128:EVEN_W - 96].reshape(T // CHUNK, CHUNK, 32).transpose(0, 2, 1)
                        .reshape(T // CHUNK, 16, 128))
            gb = mlstm_gate_b[e].reshape(-1)
            *h_scan, mc, mn, mm = _mlstm_scan(proj, gates_t2, _lane_row(gb, 0), _pair_rows(gb, 0),
                                              state_mlstm_c, state_mlstm_n, m0_all, e)
            qkv = _delta_prep(proj, delta_conv_w[e])
            dtb = delta_dt_bias[e].reshape(-1)
            nea = -jnp.exp(delta_a_log[e].astype(F32)).reshape(-1)
            *o_scan, ds = _delta_scan(qkv, proj, gates_t2, _lane_row(dtb, 24), _pair_rows(dtb, 24),
                                      _lane_row(nea, 24), _pair_rows(nea, 24), state_delta, e)
            x = _even_out(h_scan, o_scan, proj, mlstm_norm_g[e][None, :], delta_norm_g[e][None, :],
                          _bf(w_out_even[e]), x, mods, l, ln_g[l, 0][None, :], ln_b[l, 0][None, :])
            out_mc.append(mc)
            out_mn.append(mn)
            out_mm.append(mm[:, :, 0].reshape(BATCH, 2, HA))
            out_ds.append(ds)
        else:
            o = l // 2
            qkv, new_k, new_v = _odd_proj(x, mods, l, _bf(w_qkv_odd[o]), tables, new_k, new_v, o)
            a_p = _attn_context(qkv, attn_sink[o])
            a_s = _attn_latent(qkv, cache_k, cache_v, attn_sink[o], o)
            x = _odd_out(a_p, a_s, _bf(w_out_odd[o]), x, mods, l, ln_g[l, 0][None, :], ln_b[l, 0][None, :])
        w_r = jnp.concatenate([w_erouter[l].transpose(1, 0, 2).reshape(D, N_EXPERTS), w_grp[l],
                               jnp.zeros((D, 128 - N_EXPERTS - N_GROUPS), F32)], axis=1)
        b_r = jnp.concatenate([b_erouter[l].reshape(-1), b_grp[l],
                               jnp.zeros((128 - N_EXPERTS - N_GROUPS,), F32)])[None, :]
        x = _moe_dense(x, mods, l, w_r, b_r,
                       _bf(w_gate[l]).reshape(N_GROUPS, EPG, D, EXPERT_FF),
                       _bf(w_up[l]).reshape(N_GROUPS, EPG, D, EXPERT_FF),
                       _bf(w_down[l]).reshape(N_GROUPS, EPG * EXPERT_FF, D),
                       ln_g[l, 1][None, :], ln_b[l, 1][None, :], split_out=(l == DEPTH - 1))
    return (x[0].reshape(BATCH, SEQ, D), x[1].reshape(DEC_BATCH, DEC_SEQ, D),
            jnp.stack(out_mc, 1), jnp.stack(out_mn, 1), jnp.stack(out_mm, 1), jnp.stack(out_ds, 1),
            new_k, new_v)
```

```python
import functools

import numpy as np
import jax
import jax.numpy as jnp
from jax import lax
from jax.experimental import pallas as pl
from jax.experimental.pallas import tpu as pltpu

F32 = jnp.float32
BF16 = jnp.bfloat16
HI = lax.Precision.HIGHEST

D = 1024
BATCH = 32
SEQ = 256
DEPTH = 4
DEC_BATCH = 2
DEC_SEQ = 4096
PAST_LEN = 512
GRID_W = 64
N_EVEN = 2
N_ODD = 2
HA = 4
HB = 4
DH = 128
CHUNK = 64
HC = 16
KVH = 4
HD = 64
WINDOW = 128
QBLOCK = 128
ROPE_THETA = 10000.0
N_GROUPS = 4
EPG = 4
N_EXPERTS = 16
EXPERT_FF = 256
DN_ALPHA = (2 * DEPTH) ** 0.25
LN_EPS = 1e-5

NP_ROWS = BATCH * SEQ
NS_ROWS = DEC_BATCH * DEC_SEQ
T = NP_ROWS + NS_ROWS
N_MOD_ROWS = 8
EVEN_W = 4224
QKV_W = (HC + 2 * KVH) * HD

P_CHUNKS = SEQ // CHUNK
S_CHUNKS = DEC_SEQ // CHUNK
P_STEPS = BATCH * P_CHUNKS
S_STEPS = DEC_BATCH * S_CHUNKS
P_BLOCKS = NP_ROWS // CHUNK

VMEM_LIMIT = 48 * 1024 * 1024


def _params(n_axes, vmem_limit=VMEM_LIMIT):
    return pltpu.CompilerParams(dimension_semantics=("arbitrary",) * n_axes,
                                vmem_limit_bytes=vmem_limit)


def _mm(a, b, prec=None):
    return lax.dot_general(a, b, (((1,), (0,)), ((), ())), precision=prec, preferred_element_type=F32)


def _mm_nt(a, b, prec=None):
    return lax.dot_general(a, b, (((1,), (1,)), ((), ())), precision=prec, preferred_element_type=F32)


def _mm_tn(a, b, prec=None):
    return lax.dot_general(a, b, (((0,), (0,)), ((), ())), precision=prec, preferred_element_type=F32)


def _bf(x):
    return x.astype(BF16)


def _sigmoid(x):
    return 1.0 / (1.0 + jnp.exp(-x))


def _softplus(x):
    return jnp.maximum(x, 0.0) + jnp.log1p(jnp.exp(-jnp.abs(x)))


def _log_sigmoid(x):
    return jnp.minimum(x, 0.0) - jnp.log1p(jnp.exp(-jnp.abs(x)))


def _mod_row(tile, tm):
    npt = NP_ROWS // tm
    per = DEC_SEQ // tm
    return jnp.where(tile < npt, 0, 1 + (tile - npt) // per)


def _mod_spec(layer, chunk, tm):
    def imap(i, *_):
        return ((layer * N_MOD_ROWS + _mod_row(i, tm)) * 6 + chunk, 0, 0)
    return pl.BlockSpec((1, 1, D), imap)


def _modulation_kernel(c_ref, w_ref, b_ref, o_ref):
    x = c_ref[...]
    s = x * _sigmoid(x)
    o_ref[0] = _mm(s, w_ref[0], HI) + b_ref[0]


def _modulation(cvecs, w_mod, b_mod):
    out = pl.pallas_call(
        _modulation_kernel,
        out_shape=jax.ShapeDtypeStruct((DEPTH, N_MOD_ROWS, 6 * D), F32),
        grid=(DEPTH, 6),
        in_specs=[pl.BlockSpec((N_MOD_ROWS, D), lambda l, j: (0, 0)),
                  pl.BlockSpec((1, D, D), lambda l, j: (l, 0, j)),
                  pl.BlockSpec((1, 1, D), lambda l, j: (l * 6 + j, 0, 0))],
        out_specs=pl.BlockSpec((1, N_MOD_ROWS, D), lambda l, j: (l, 0, j)),
        compiler_params=_params(2),
        name="modulation",
    )(cvecs, w_mod, b_mod.reshape(DEPTH * 6, 1, D))
    return out.reshape(DEPTH * N_MOD_ROWS * 6, 1, D)


def _x_operands(x, tm):
    if isinstance(x, tuple):
        npt = NP_ROWS // tm
        return ([pl.BlockSpec((tm, D), lambda i, *_: (jnp.minimum(i, npt - 1), 0)),
                 pl.BlockSpec((tm, D), lambda i, *_: (jnp.maximum(i - npt, 0), 0))], list(x))
    return [pl.BlockSpec((tm, D), lambda i, *_: (i, 0))], [x]


def _load_x(x_refs, tm):
    if len(x_refs) == 1:
        return x_refs[0][...]
    return jnp.where(pl.program_id(0) < NP_ROWS // tm, x_refs[0][...], x_refs[1][...])


def _proj_kernel(*refs, tm, n_x):
    x_refs, (sh_ref, sc_ref, w_ref, o_ref) = refs[:n_x], refs[n_x:]
    xm = _load_x(x_refs, tm) * (1.0 + sc_ref[0]) + sh_ref[0]
    o_ref[...] = _mm(_bf(xm), w_ref[...])


def _even_proj(x, mods, layer, w):
    tm = 256
    x_specs, x_arrays = _x_operands(x, tm)
    return pl.pallas_call(
        functools.partial(_proj_kernel, tm=tm, n_x=len(x_arrays)),
        out_shape=jax.ShapeDtypeStruct((T, EVEN_W), F32),
        grid=(T // tm,),
        in_specs=x_specs + [_mod_spec(layer, 0, tm), _mod_spec(layer, 1, tm),
                            pl.BlockSpec((D, EVEN_W), lambda i: (0, 0))],
        out_specs=pl.BlockSpec((tm, EVEN_W), lambda i: (i, 0)),
        compiler_params=_params(1),
        name="even_proj",
    )(*x_arrays, mods, mods, w)


def _qkv_kernel(x_ref, sh_ref, sc_ref, w_ref, cos_ref, sa_ref, sb_ref, kc_in, vc_in, o_ref, kc_ref, vc_ref, *, tm):
    del kc_in, vc_in
    i = pl.program_id(0)
    xm = x_ref[...] * (1.0 + sc_ref[0]) + sh_ref[0]
    acc = _mm(_bf(xm), w_ref[...])

    @pl.when(i < NP_ROWS // tm)
    def _():
        for b in range(tm // SEQ):
            for kv in range(KVH):
                k0 = HC * HD + kv * HD
                v0 = (HC + KVH) * HD + kv * HD
                kc_ref[b, 0, kv] = acc[b * SEQ:(b + 1) * SEQ, k0:k0 + HD]
                vc_ref[b, 0, kv] = acc[b * SEQ:(b + 1) * SEQ, v0:v0 + HD]

    n_q = HC * HD // 128
    n_k = KVH * HD // 128
    is_latent = i >= NP_ROWS // tm
    cos = jnp.where(is_latent, cos_ref[...], 1.0)
    sa = jnp.where(is_latent, sa_ref[...], 0.0)
    sb = jnp.where(is_latent, sb_ref[...], 0.0)
    for g in range(n_q + n_k):
        blk = acc[:, g * 128:(g + 1) * 128]
        if g < n_q:
            blk = blk * (HD ** -0.5)
        rot = blk * cos + pltpu.roll(blk, 112, 1) * sa + pltpu.roll(blk, 16, 1) * sb
        o_ref[:, g * 128:(g + 1) * 128] = _bf(rot)
    o_ref[:, (n_q + n_k) * 128:] = _bf(acc[:, (n_q + n_k) * 128:])


def _rope_tables():
    half = HD // 4
    inv = np.float32(ROPE_THETA) ** (-np.arange(half, dtype=np.float32) / np.float32(half))
    pos = np.arange(DEC_SEQ)
    row = (pos // GRID_W).astype(np.float32)[:, None] * inv[None, :]
    col = (pos % GRID_W).astype(np.float32)[:, None] * inv[None, :]
    cos = np.concatenate([np.cos(row), np.cos(row), np.cos(col), np.cos(col)], axis=-1)
    sin = np.concatenate([np.sin(row), np.sin(row), np.sin(col), np.sin(col)], axis=-1)
    first = (np.arange(HD) % 32) < 16
    sa = np.where(first, -sin, 0.0)
    sb = np.where(first, 0.0, sin)
    tile2 = lambda t: jnp.asarray(np.concatenate([t, t], axis=-1), F32)
    return tile2(cos), tile2(sa), tile2(sb)


def _odd_proj(x, mods, layer, w, tables, cache_k, cache_v, o):
    tm = 512
    npt = NP_ROWS // tm
    per = DEC_SEQ // tm
    tab_spec = pl.BlockSpec((tm, 128), lambda i: (jnp.where(i < npt, 0, (i - npt) % per), 0))
    cache_spec = pl.BlockSpec((tm // SEQ, 1, KVH, SEQ, HD), lambda i: (jnp.minimum(i, npt - 1), o, 0, 0, 0))
    cache_shape = jax.ShapeDtypeStruct((BATCH, N_ODD, KVH, SEQ, HD), F32)
    return pl.pallas_call(
        functools.partial(_qkv_kernel, tm=tm),
        out_shape=(jax.ShapeDtypeStruct((T, QKV_W), BF16), cache_shape, cache_shape),
        grid=(T // tm,),
        in_specs=[pl.BlockSpec((tm, D), lambda i: (i, 0)),
                  _mod_spec(layer, 0, tm), _mod_spec(layer, 1, tm),
                  pl.BlockSpec((D, QKV_W), lambda i: (0, 0)),
                  tab_spec, tab_spec, tab_spec,
                  pl.BlockSpec(memory_space=pl.ANY), pl.BlockSpec(memory_space=pl.ANY)],
        out_specs=(pl.BlockSpec((tm, QKV_W), lambda i: (i, 0)), cache_spec, cache_spec),
        input_output_aliases={7: 1, 8: 2},
        compiler_params=_params(1),
        name="odd_qkv_proj",
    )(x, mods, mods, w, *tables, cache_k, cache_v)


PREP_ROWS = 256


def _delta_prep_kernel(x_ref, prev_ref, next_ref, w_ref, o_ref):
    i = pl.program_id(0)
    npb = NP_ROWS // PREP_ROWS
    per = DEC_SEQ // PREP_ROWS
    is_latent = i >= npb
    pos = (i - npb) % per
    has_prev = jnp.logical_and(is_latent, pos > 0)
    has_next = jnp.logical_and(is_latent, pos < per - 1)
    x = x_ref[...]
    w = w_ref[...]
    rows = lax.broadcasted_iota(jnp.int32, x.shape, 0)
    prev_row = jnp.where(has_prev, prev_ref[7:8, :], 0.0)
    next_row = jnp.where(has_next, next_ref[0:1, :], 0.0)
    xm1 = jnp.where(rows == 0, prev_row, pltpu.roll(x, 1, 0))
    xp1 = jnp.where(rows == PREP_ROWS - 1, next_row, pltpu.roll(x, PREP_ROWS - 1, 0))
    y = xm1 * w[0:1, :] + x * w[1:2, :] + xp1 * w[2:3, :]
    y = y * _sigmoid(y)
    for h in range(3 * HB):
        yh = y[:, h * DH:(h + 1) * DH]
        if h < 2 * HB:
            inv = lax.rsqrt(jnp.sum(yh * yh, axis=-1, keepdims=True) + 1e-6)
            yh = yh * (inv * (DH ** -0.5) if h < HB else inv)
        o_ref[:, h * DH:(h + 1) * DH] = yh


def _delta_prep(proj, conv_w):
    nblk = T // PREP_ROWS
    sub = PREP_ROWS // 8
    last8 = T // 8 - 1
    return pl.pallas_call(
        _delta_prep_kernel,
        out_shape=jax.ShapeDtypeStruct((T, 3 * 512), F32),
        grid=(nblk,),
        in_specs=[pl.BlockSpec((PREP_ROWS, 3 * 512), lambda i: (i, 0)),
                  pl.BlockSpec((8, 3 * 512), lambda i: (jnp.maximum(i * sub - 1, 0), 0)),
                  pl.BlockSpec((8, 3 * 512), lambda i: (jnp.minimum((i + 1) * sub, last8), 0)),
                  pl.BlockSpec((3, 3 * 512), lambda i: (0, 0))],
        out_specs=pl.BlockSpec((PREP_ROWS, 3 * 512), lambda i: (i, 0)),
        compiler_params=_params(1),
        name="delta_prep",
    )(proj, proj, proj, conv_w)


assert P_STEPS == S_STEPS
SCAN_STEPS = P_STEPS


def _bwd_local(s, nc):
    return (s // nc) * nc + nc - 1 - s % nc


def _scan_blocks():
    return (lambda s: s, lambda s: _bwd_local(s, P_CHUNKS),
            lambda s: P_BLOCKS + s, lambda s: P_BLOCKS + _bwd_local(s, S_CHUNKS))


def _scan_specs(xcol):
    gcol = EVEN_W // 128 - 1
    blocks = _scan_blocks()
    return ([pl.BlockSpec((CHUNK, 3 * 512), lambda s, f=f: (f(s), xcol)) for f in blocks]
            + [pl.BlockSpec((CHUNK, 128), lambda s, f=f: (f(s), gcol)) for f in blocks]
            + [pl.BlockSpec((1, 16, 128), lambda s, f=f: (f(s), 0, 0)) for f in blocks])


def _scan_out_specs():
    local = (lambda s: s, lambda s: _bwd_local(s, P_CHUNKS), lambda s: s, lambda s: _bwd_local(s, S_CHUNKS))
    return [pl.BlockSpec((CHUNK, 512), lambda s, f=f: (f(s), 0)) for f in local]


def _mlstm_kernel(xpf_ref, xpb_ref, xsf_ref, xsb_ref, gpf_ref, gpb_ref, gsf_ref, gsb_ref,
                  rpf_ref, rpb_ref, rsf_ref, rsb_ref,
                  brow_ref, b2_ref, c0_ref, n0_ref, m0_ref,
                  hpf_ref, hpb_ref, hsf_ref, hsb_ref, c_out, n_out, m_out,
                  cp_s, np_s, mp_s, cs_s, ns_s, ms_s):
    s = pl.program_id(0)
    jp = s % P_CHUNKS
    js = s % S_CHUNKS

    @pl.when(jp == 0)
    def _():
        cp_s[...] = jnp.zeros_like(cp_s)
        np_s[...] = jnp.zeros_like(np_s)
        mp_s[...] = jnp.zeros_like(mp_s)

    @pl.when(js == 0)
    def _():
        for d in range(2):
            for h in range(HA):
                cs_s[d * HA + h] = c0_ref[0, 0, d, h]
                ns_s[d * HA + h] = jnp.broadcast_to(n0_ref[0, 0, d, h:h + 1, :], (DH, 128)).T
        ms_s[...] = m0_ref[0, 0]

    states = ((cp_s, np_s, mp_s), (cs_s, ns_s, ms_s))

    row = lax.broadcasted_iota(jnp.int32, (CHUNK, 128), 0)
    lane = lax.broadcasted_iota(jnp.int32, (CHUNK, 128), 1)
    left = lane < CHUNK
    lcol = jnp.where(left, lane, lane - CHUNK)
    r64 = lax.broadcasted_iota(jnp.int32, (CHUNK, CHUNK), 0)
    c64 = lax.broadcasted_iota(jnp.int32, (CHUNK, CHUNK), 1)
    r128 = lax.broadcasted_iota(jnp.int32, (128, 128), 0)
    c128 = lax.broadcasted_iota(jnp.int32, (128, 128), 1)
    same_half = (r128 < CHUNK) == (c128 < CHUNK)
    neg = -jnp.inf
    ins = ((xpf_ref, gpf_ref, rpf_ref, hpf_ref), (xpb_ref, gpb_ref, rpb_ref, hpb_ref),
           (xsf_ref, gsf_ref, rsf_ref, hsf_ref), (xsb_ref, gsb_ref, rsb_ref, hsb_ref))

    sel_r = lax.broadcasted_iota(jnp.int32, (128, HA * 128), 0)
    sel_h = lax.broadcasted_iota(jnp.int32, (128, HA * 128), 1) // 128
    ones_lr = lax.broadcasted_iota(jnp.int32, (256, 256), 0)
    ones_lc = lax.broadcasted_iota(jnp.int32, (256, 256), 1)
    half_sum = (((ones_lr % 128) < CHUNK) == (ones_lc < 128)).astype(BF16)
    ones_l = jnp.ones((CHUNK, 128), BF16)
    rows256 = lax.broadcasted_iota(jnp.int32, (CHUNK, 256), 0)

    def running_max(x, d):
        sh = 1
        while sh < CHUNK:
            if d == 0:
                x = jnp.maximum(x, jnp.where(rows256 >= sh, pltpu.roll(x, sh, 0), neg))
            else:
                x = jnp.maximum(x, jnp.where(rows256 < CHUNK - sh, pltpu.roll(x, CHUNK - sh, 0), neg))
            sh *= 2
        return x

    gate = []
    for u in range(4):
        d = u % 2
        gc_ref, gr_ref = ins[u][1], ins[u][2]
        if d == 0:
            incl_p, tri_c, tri_r = lcol <= row, c64 <= r64, jnp.logical_and(same_half, r128 <= c128)
        else:
            incl_p, tri_c, tri_r = lcol >= row, c64 >= r64, jnp.logical_and(same_half, r128 >= c128)
        gc = gc_ref[...] + brow_ref[...]
        gr = gr_ref[0] + b2_ref[...]
        i_rep = _replicate(gc, (sel_r == d * 8 + sel_h).astype(BF16))
        f_rep = _replicate(_log_sigmoid(gc), (sel_r == d * 8 + 4 + sel_h).astype(BF16))
        b_rep = _prefix_cols(tri_c.astype(BF16), f_rep)
        cs_row = _prefix_rows(_log_sigmoid(gr), tri_r.astype(BF16))
        gate.append((incl_p, gr, cs_row, i_rep, b_rep))

    st = []
    for u in range(4):
        d = u % 2
        c_s, n_s, m_s = states[u // 2]
        for hp in range(HA // 2):
            x_ref = ins[u][0]
            incl_p, gr, cs_row, i_rep, b_rep = gate[u]
            end = CHUNK - 1 if d == 0 else 0
            heads = []
            for h in (2 * hp, 2 * hp + 1):
                c = d * HA + h
                heads.append(dict(c=c, h=h, q=x_ref[:, h * DH:(h + 1) * DH],
                                  k=x_ref[:, 512 + h * DH:512 + (h + 1) * DH] * (DH ** -0.5),
                                  v=x_ref[:, 1024 + h * DH:1024 + (h + 1) * DH],
                                  i=i_rep[:, h * 128:(h + 1) * 128], b=b_rep[:, h * 128:(h + 1) * 128],
                                  m=m_s[c:c + 1, :]))
            ha, hb_ = heads
            i_row = gr[d * 4 + hp:d * 4 + hp + 1, :]
            b_row = cs_row[d * 4 + 2 + hp:d * 4 + 3 + hp, :]
            run = running_max(jnp.concatenate([ha["i"] - ha["b"], hb_["i"] - hb_["b"]], axis=1), d)
            for idx, hd in enumerate(heads):
                top = jnp.maximum(hd["m"], run[:, idx * 128:(idx + 1) * 128])
                hd["m_t"] = hd["b"] + top
                hd["w_inter"] = jnp.exp(hd["m"] - top)
            b_p = jnp.where(left, ha["b"], hb_["b"])
            m_t_p = jnp.where(left, ha["m_t"], hb_["m_t"])
            dmat = jnp.where(incl_p, b_p - b_row + i_row, neg)
            q_cat = _bf(jnp.concatenate([ha["q"], hb_["q"]], axis=1))
            k_bd = _bf(_block_diag2(ha["k"], hb_["k"]))
            st.append(dict(u=u, d=d, heads=heads, sc=_mm_nt(q_cat, k_bd) * jnp.exp(dmat - m_t_p)))

    for p in st:
        ha, hb_ = p["heads"]
        h_ref = ins[p["u"]][3]
        c_s, n_s, m_s = states[p["u"] // 2]
        end = CHUNK - 1 if p["d"] == 0 else 0
        sc = p["sc"]
        sv = _mm(_bf(sc), _bf(_block_diag2(ha["v"], hb_["v"])))
        dens = _mm(jnp.concatenate(_split2(sc), axis=1), half_sum)
        for idx, hd in enumerate((ha, hb_)):
            c, h, q, k = hd["c"], hd["h"], hd["q"], hd["k"]
            c_mat = c_s[c]
            n_mat = n_s[c]
            qb = _bf(q)
            num = sv[:, idx * DH:(idx + 1) * DH] + hd["w_inter"] * _mm(qb, _bf(c_mat))
            den = dens[:, idx * 128:(idx + 1) * 128] + hd["w_inter"] * _mm(qb, _bf(n_mat))
            h_ref[:, h * DH:(h + 1) * DH] = num / jnp.maximum(jnp.abs(den), jnp.exp(-hd["m_t"]))
            b_last = hd["b"][end:end + 1, :]
            g_end = b_last - hd["b"] + hd["i"]
            m_new = jnp.maximum(b_last + hd["m"], jnp.max(g_end, axis=0, keepdims=True))
            kwb = _bf(k * jnp.exp(g_end - m_new))
            decay = jnp.exp(b_last + hd["m"] - m_new)
            c_s[c] = decay * c_mat + _mm_tn(kwb, _bf(hd["v"]))
            n_s[c] = decay * n_mat + _mm_tn(kwb, ones_l)
            m_s[c:c + 1, :] = m_new

    @pl.when(jp == P_CHUNKS - 1)
    def _():
        for d in range(2):
            for h in range(HA):
                c_out[0, d, h] = cp_s[d * HA + h]
                n_out[0, d, h:h + 1, :] = np_s[d * HA + h].T[0:1, :]
        m_out[0] = mp_s[...]


def _mlstm_scan(proj, gates_t2, bias_row, bias2, c0, n0, m0, e):
    const2 = lambda s: (0, 0)
    in_specs = (_scan_specs(1)
                + [pl.BlockSpec((1, 128), const2), pl.BlockSpec((16, 128), const2),
                   pl.BlockSpec((1, 1, 2, HA, DH, DH), lambda s: (s // S_CHUNKS, e, 0, 0, 0, 0)),
                   pl.BlockSpec((1, 1, 2, HA, DH), lambda s: (s // S_CHUNKS, e, 0, 0, 0)),
                   pl.BlockSpec((1, 1, 2 * HA, 128), lambda s: (s // S_CHUNKS, e, 0, 0))])
    half = jax.ShapeDtypeStruct((NP_ROWS, HA * DH), F32)
    out_shape = (half, half, half, half,
                 jax.ShapeDtypeStruct((BATCH, 2, HA, DH, DH), F32),
                 jax.ShapeDtypeStruct((BATCH, 2, HA, DH), F32),
                 jax.ShapeDtypeStruct((BATCH, 2 * HA, 128), F32))
    out_specs = _scan_out_specs() + [
        pl.BlockSpec((1, 2, HA, DH, DH), lambda s: (s // P_CHUNKS, 0, 0, 0, 0)),
        pl.BlockSpec((1, 2, HA, DH), lambda s: (s // P_CHUNKS, 0, 0, 0)),
        pl.BlockSpec((1, 2 * HA, 128), lambda s: (s // P_CHUNKS, 0, 0))]
    state = [pltpu.VMEM((2 * HA, DH, DH), F32), pltpu.VMEM((2 * HA, DH, 128), F32), pltpu.VMEM((2 * HA, 128), F32)]
    return pl.pallas_call(
        _mlstm_kernel,
        out_shape=out_shape,
        grid=(SCAN_STEPS,),
        in_specs=in_specs,
        out_specs=out_specs,
        scratch_shapes=state + state,
        compiler_params=_params(1),
        name="mlstm_scan",
    )(proj, proj, proj, proj, proj, proj, proj, proj, gates_t2, gates_t2, gates_t2, gates_t2,
      bias_row, bias2, c0, n0, m0)


def _split2(x):
    hi = _bf(x)
    return hi, _bf(x - hi.astype(F32))


def _split3(x):
    h1 = _bf(x)
    r1 = x - h1.astype(F32)
    h2 = _bf(r1)
    return h1, h2, _bf(r1 - h2.astype(F32))


def _lhs3(x):
    hi, lo = _split2(x)
    return jnp.concatenate([hi, lo, hi], axis=1)


def _rhs3(x):
    hi, lo = _split2(x)
    return jnp.concatenate([hi, hi, lo], axis=0)


def _prefix_cols(tri_bf, x):
    n = x.shape[1]
    r = _mm(tri_bf, jnp.concatenate(_split3(x), axis=1))
    return r[:, :n] + r[:, n:2 * n] + r[:, 2 * n:]


def _replicate(x, sel):
    return _mm(jnp.concatenate(_split3(x), axis=1), jnp.concatenate([sel, sel, sel], axis=0))


def _prefix_rows(x, tri_bf):
    m = x.shape[0]
    r = _mm(jnp.concatenate(_split3(x), axis=0), tri_bf)
    return r[:m] + r[m:2 * m] + r[2 * m:]


LEVELS = tuple(range(6))


def _block_diag2(a, b):
    z = jnp.zeros_like(a)
    return jnp.concatenate([jnp.concatenate([a, z], axis=1), jnp.concatenate([z, b], axis=1)], axis=0)


def _delta_kernel(xpf_ref, xpb_ref, xsf_ref, xsb_ref, gpf_ref, gpb_ref, gsf_ref, gsb_ref,
                  rpf_ref, rpb_ref, rsf_ref, rsb_ref,
                  dtrow_ref, dt2_ref, narow_ref, na2_ref, s0_ref,
                  opf_ref, opb_ref, osf_ref, osb_ref, s_out, sp_s, ss_s):
    s = pl.program_id(0)
    jp = s % P_CHUNKS
    js = s % S_CHUNKS

    @pl.when(jp == 0)
    def _():
        sp_s[...] = jnp.zeros_like(sp_s)

    @pl.when(js == 0)
    def _():
        for d in range(2):
            for h in range(HB):
                ss_s[d * HB + h] = s0_ref[0, 0, d, h]

    states = (sp_s, ss_s)

    row = lax.broadcasted_iota(jnp.int32, (CHUNK, 128), 0)
    lane = lax.broadcasted_iota(jnp.int32, (CHUNK, 128), 1)
    left = lane < CHUNK
    lcol = jnp.where(left, lane, lane - CHUNK)
    eye_p = (lcol == row).astype(F32)
    r64 = lax.broadcasted_iota(jnp.int32, (CHUNK, CHUNK), 0)
    c64 = lax.broadcasted_iota(jnp.int32, (CHUNK, CHUNK), 1)
    r128 = lax.broadcasted_iota(jnp.int32, (128, 128), 0)
    c128 = lax.broadcasted_iota(jnp.int32, (128, 128), 1)
    same_half = (r128 < CHUNK) == (c128 < CHUNK)
    ins = ((xpf_ref, gpf_ref, rpf_ref, opf_ref), (xpb_ref, gpb_ref, rpb_ref, opb_ref),
           (xsf_ref, gsf_ref, rsf_ref, osf_ref), (xsb_ref, gsb_ref, rsb_ref, osb_ref))

    gate = []
    for u in range(4):
        d = u % 2
        gc_ref, gr_ref = ins[u][1], ins[u][2]
        if d == 0:
            incl_p, strict_p = lcol <= row, lcol < row
            tri_c, tri_r = c64 <= r64, jnp.logical_and(same_half, r128 <= c128)
        else:
            incl_p, strict_p = lcol >= row, lcol > row
            tri_c, tri_r = c64 >= r64, jnp.logical_and(same_half, r128 >= c128)
        xc = gc_ref[...]
        xr = gr_ref[0]
        la_c = narow_ref[...] * _softplus(xc + dtrow_ref[...])
        la_r = na2_ref[...] * _softplus(xr + dt2_ref[...])
        g_c = _prefix_cols(tri_c.astype(BF16), la_c)
        g_r = _prefix_rows(la_r, tri_r.astype(BF16))
        gate.append((incl_p, strict_p, _sigmoid(xc), g_c, g_r))

    def bd_rhs(hi, lo, mask=None):
        top = left if mask is None else jnp.logical_and(left, mask)
        bot = jnp.logical_not(left) if mask is None else jnp.logical_and(jnp.logical_not(left), mask)
        zero = jnp.zeros_like(hi)
        blocks = [jnp.concatenate([jnp.where(top, x, zero), jnp.where(bot, x, zero)], axis=0) for x in (hi, lo)]
        return jnp.concatenate([blocks[0], blocks[0], blocks[1]], axis=0)

    off = ([], [])
    for lv in LEVELS:
        same = jnp.right_shift(row, lv + 1) == jnp.right_shift(lcol, lv + 1)
        r_hi = jnp.bitwise_and(jnp.right_shift(row, lv), 1) == 1
        c_hi = jnp.bitwise_and(jnp.right_shift(lcol, lv), 1) == 1
        off[0].append(same & r_hi & jnp.logical_not(c_hi))
        off[1].append(same & jnp.logical_not(r_hi) & c_hi)

    pairs = [(u, hp) for u in range(4) for hp in range(HB // 2)]
    st = []
    for u, hp in pairs:
        d = u % 2
        x_ref = ins[u][0]
        incl_p, strict_p, beta_c, g_c, g_r = gate[u]
        end = CHUNK - 1 if d == 0 else 0
        heads = []
        for h in (2 * hp, 2 * hp + 1):
            ib = 16 + d * HB + h
            ia = 24 + d * HB + h
            gcol = g_c[:, ia:ia + 1]
            bc = beta_c[:, ib:ib + 1]
            q = x_ref[:, h * DH:(h + 1) * DH]
            k = x_ref[:, 512 + h * DH:512 + (h + 1) * DH]
            v = x_ref[:, 1024 + h * DH:1024 + (h + 1) * DH]
            heads.append(dict(h=h, gcol=gcol, bc=bc, q=q, k=k, v=v, kb=k * bc, eg=jnp.exp(gcol),
                              g_last=gcol[end:end + 1, :]))
        ha, hb_ = heads
        r = 12 + d * 2 + hp
        gcol_p = jnp.where(left, ha["gcol"], hb_["gcol"])
        decay = jnp.exp(jnp.where(incl_p, gcol_p - g_r[r:r + 1, :], -jnp.inf))
        k_bd = _bf(_block_diag2(ha["k"], hb_["k"]))
        kb_cat = _bf(jnp.concatenate([ha["kb"], hb_["kb"]], axis=1))
        q_cat = _bf(jnp.concatenate([ha["q"], hb_["q"]], axis=1))
        a_mat = jnp.where(strict_p, _mm_nt(kb_cat, k_bd) * decay, 0.0)
        qk = _mm_nt(q_cat, k_bd) * decay
        a_hi, a_lo = _split2(a_mat)
        st.append(dict(u=u, d=d, heads=heads, t=eye_p - jnp.where(off[d][0], a_mat, 0.0), qk=qk,
                       am=[bd_rhs(a_hi, a_lo, m) for m in off[d][1:]]))

    for li in range(len(LEVELS) - 1):
        for p in st:
            p["t_parts"] = _split2(p["t"])
            t_hi, t_lo = p["t_parts"]
            p["w"] = _mm(jnp.concatenate([t_hi, t_lo, t_hi], axis=1), p["am"][li])
        for p in st:
            p["t"] = p["t"] - _mm(_lhs3(p["w"]), bd_rhs(*p["t_parts"]))

    for p in st:
        ha, hb_ = p["heads"]
        o_ref = ins[p["u"]][3]
        s_s = states[p["u"] // 2]
        rhs_a = jnp.concatenate([ha["v"] * ha["bc"], ha["kb"] * ha["eg"]], axis=1)
        rhs_b = jnp.concatenate([hb_["v"] * hb_["bc"], hb_["kb"] * hb_["eg"]], axis=1)
        (a_hi, a_lo), (b_hi, b_lo) = _split2(rhs_a), _split2(rhs_b)
        bd_hi, bd_lo = _block_diag2(a_hi, b_hi), _block_diag2(a_lo, b_lo)
        sol = _mm(_lhs3(p["t"]), jnp.concatenate([bd_hi, bd_hi, bd_lo], axis=0))
        vn = []
        for idx, hd in enumerate((ha, hb_)):
            c = p["d"] * HB + hd["h"]
            s_mat = s_s[c]
            sbf = _bf(s_mat)
            so = sol[:, idx * 2 * DH:(idx + 1) * 2 * DH]
            v_new = so[:, :DH] - _mm(_bf(so[:, DH:]), sbf)
            vn.append(v_new)
            hd["o1"] = _mm(_bf(hd["q"] * hd["eg"]), sbf)
            s_s[c] = (jnp.exp(hd["g_last"]) * s_mat
                      + _mm_tn(_bf(hd["k"] * jnp.exp(hd["g_last"] - hd["gcol"])), _bf(v_new)))
        o2 = _mm(_bf(p["qk"]), _bf(_block_diag2(vn[0], vn[1])))
        for idx, hd in enumerate((ha, hb_)):
            o_ref[:, hd["h"] * DH:(hd["h"] + 1) * DH] = hd["o1"] + o2[:, idx * DH:(idx + 1) * DH]

    @pl.when(jp == P_CHUNKS - 1)
    def _():
        for d in range(2):
            for h in range(HB):
                s_out[0, d, h] = sp_s[d * HB + h]


def _delta_scan(qkv, proj, gates_t2, dt_row, dt2, na_row, na2, s0, e):
    const2 = lambda s: (0, 0)
    in_specs = (_scan_specs(0)
                + [pl.BlockSpec((1, 128), const2), pl.BlockSpec((16, 128), const2),
                   pl.BlockSpec((1, 128), const2), pl.BlockSpec((16, 128), const2),
                   pl.BlockSpec((1, 1, 2, HB, DH, DH), lambda s: (s // S_CHUNKS, e, 0, 0, 0, 0))])
    half = jax.ShapeDtypeStruct((NP_ROWS, HB * DH), F32)
    out_shape = (half, half, half, half, jax.ShapeDtypeStruct((BATCH, 2, HB, DH, DH), F32))
    out_specs = _scan_out_specs() + [pl.BlockSpec((1, 2, HB, DH, DH), lambda s: (s // P_CHUNKS, 0, 0, 0, 0))]
    return pl.pallas_call(
        _delta_kernel,
        out_shape=out_shape,
        grid=(SCAN_STEPS,),
        in_specs=in_specs,
        out_specs=out_specs,
        scratch_shapes=[pltpu.VMEM((2 * HB, DH, DH), F32), pltpu.VMEM((2 * HB, DH, DH), F32)],
        compiler_params=_params(1),
        name="delta_scan",
    )(qkv, qkv, qkv, qkv, proj, proj, proj, proj, gates_t2, gates_t2, gates_t2, gates_t2,
      dt_row, dt2, na_row, na2, s0)


def _residual_ln(x, gate, y, g, b):
    r = DN_ALPHA * x + gate * y
    mu = jnp.mean(r, axis=-1, keepdims=True)
    var = jnp.mean(jnp.square(r - mu), axis=-1, keepdims=True)
    return (r - mu) * lax.rsqrt(var + LN_EPS) * g + b


def _even_out_kernel(*refs, tm, n_x):
    x_refs = refs[:n_x]
    (hpf_ref, hpb_ref, hsf_ref, hsb_ref, opf_ref, opb_ref, osf_ref, osb_ref,
     oa_ref, zb_ref, mg_ref, dg_ref, w_ref, gate_ref, g_ref, b_ref, o_ref) = refs[n_x:]
    is_p = pl.program_id(0) < NP_ROWS // tm
    hf_ref, hb_ref, of_ref, ob_ref = (
        lambda sl, p=p, q=q: jnp.where(is_p, p[:, sl], q[:, sl])
        for p, q in ((hpf_ref, hsf_ref), (hpb_ref, hsb_ref), (opf_ref, osf_ref), (opb_ref, osb_ref)))
    parts = []
    for h in range(HA):
        sl = slice(h * DH, (h + 1) * DH)
        hh = hf_ref(sl) + hb_ref(sl)
        mu = jnp.mean(hh, axis=-1, keepdims=True)
        var = jnp.mean(jnp.square(hh - mu), axis=-1, keepdims=True)
        parts.append(_sigmoid(oa_ref[:, sl]) * ((hh - mu) * lax.rsqrt(var + LN_EPS) * mg_ref[:, sl]))
    for h in range(HB):
        sl = slice(h * DH, (h + 1) * DH)
        oo = of_ref(sl) + ob_ref(sl)
        z = zb_ref[:, sl]
        nrm = oo * lax.rsqrt(jnp.mean(jnp.square(oo), axis=-1, keepdims=True) + LN_EPS) * dg_ref[:, sl]
        parts.append(nrm * (z * _sigmoid(z)))
    a = jnp.concatenate(parts, axis=1)
    y = _mm(_bf(a), w_ref[...])
    o_ref[...] = _residual_ln(_load_x(x_refs, tm), gate_ref[0], y, g_ref[...], b_ref[...])


def _even_out(h_scan, o_scan, proj, mg, dg, w, x, mods, layer, ln_g, ln_b):
    tm = 512
    npt = NP_ROWS // tm
    row512 = lambda i: (i, 0)
    const2 = lambda i: (0, 0)
    p_spec = pl.BlockSpec((tm, 512), lambda i: (jnp.minimum(i, npt - 1), 0))
    s_spec = pl.BlockSpec((tm, 512), lambda i: (jnp.maximum(i - npt, 0), 0))
    x_specs, x_arrays = _x_operands(x, tm)
    return pl.pallas_call(
        functools.partial(_even_out_kernel, tm=tm, n_x=len(x_arrays)),
        out_shape=jax.ShapeDtypeStruct((T, D), F32),
        grid=(T // tm,),
        in_specs=x_specs + [p_spec, p_spec, s_spec, s_spec, p_spec, p_spec, s_spec, s_spec,
                            pl.BlockSpec((tm, 512), lambda i: (i, 6)),
                            pl.BlockSpec((tm, 512), lambda i: (i, 7)),
                            pl.BlockSpec((1, 512), const2), pl.BlockSpec((1, 512), const2),
                            pl.BlockSpec((D, D), const2),
                            _mod_spec(layer, 2, tm),
                            pl.BlockSpec((1, D), const2), pl.BlockSpec((1, D), const2)],
        out_specs=pl.BlockSpec((tm, D), row512),
        compiler_params=_params(1),
        name="even_out_ln",
    )(*x_arrays, *h_scan, *o_scan, proj, proj, mg, dg, w, mods, ln_g, ln_b)


def _odd_out_kernel(ap_ref, as_ref, w_ref, x_ref, gate_ref, g_ref, b_ref, o_ref, *, tm):
    a = jnp.where(pl.program_id(0) < NP_ROWS // tm, ap_ref[...], as_ref[...])
    y = _mm(_bf(a), w_ref[...])
    o_ref[...] = _residual_ln(x_ref[...], gate_ref[0], y, g_ref[...], b_ref[...])


def _odd_out(a_prompt, a_latent, w, x, mods, layer, ln_g, ln_b):
    tm = 512
    npt = NP_ROWS // tm
    row = lambda i: (i, 0)
    const2 = lambda i: (0, 0)
    return pl.pallas_call(
        functools.partial(_odd_out_kernel, tm=tm),
        out_shape=jax.ShapeDtypeStruct((T, D), F32),
        grid=(T // tm,),
        in_specs=[pl.BlockSpec((tm, D), lambda i: (jnp.minimum(i, npt - 1), 0)),
                  pl.BlockSpec((tm, D), lambda i: (jnp.maximum(i - npt, 0), 0)),
                  pl.BlockSpec((D, D), const2), pl.BlockSpec((tm, D), row),
                  _mod_spec(layer, 2, tm), pl.BlockSpec((1, D), const2), pl.BlockSpec((1, D), const2)],
        out_specs=pl.BlockSpec((tm, D), row),
        compiler_params=_params(1),
        name="odd_out_ln",
    )(a_prompt, a_latent, w, x, mods, ln_g, ln_b)


def _attn_ctx_kernel(qkv_ref, sink_ref, o_ref):
    ones = jnp.ones((SEQ, HD), BF16)
    for kv in range(KVH):
        k = _bf(qkv_ref[:, HC * HD + kv * HD:HC * HD + (kv + 1) * HD])
        v = _bf(qkv_ref[:, (HC + KVH) * HD + kv * HD:(HC + KVH) * HD + (kv + 1) * HD])
        v1 = jnp.concatenate([v, ones], axis=1)
        heads = range(kv * (HC // KVH), (kv + 1) * (HC // KVH))
        sts = [_mm_nt(k, _bf(qkv_ref[:, h * HD:(h + 1) * HD])) for h in heads]
        ms = [jnp.maximum(jnp.max(st, axis=0, keepdims=True), sink_ref[h]) for st, h in zip(sts, heads)]
        ovs = [_mm_tn(v1, _bf(jnp.exp(st - m))) for st, m in zip(sts, ms)]
        outs = [ov[:HD] / (ov[HD:HD + 1] + jnp.exp(sink_ref[h] - m)) for ov, m, h in zip(ovs, ms, heads)]
        for pr in range(HC // KVH // 2):
            lane0 = (kv * (HC // KVH) + 2 * pr) * HD
            o_ref[:, lane0:lane0 + 2 * HD] = _bf(jnp.concatenate(outs[2 * pr:2 * pr + 2], axis=0).T)


def _attn_context(qkv, sink):
    return pl.pallas_call(
        _attn_ctx_kernel,
        out_shape=jax.ShapeDtypeStruct((NP_ROWS, HC * HD), BF16),
        grid=(BATCH,),
        in_specs=[pl.BlockSpec((SEQ, QKV_W), lambda b: (b, 0)),
                  pl.BlockSpec(memory_space=pltpu.SMEM)],
        out_specs=pl.BlockSpec((SEQ, HC * HD), lambda b: (b, 0)),
        compiler_params=_params(1),
        name="attn_context",
    )(qkv, sink)


def _attn_lat_kernel(q_ref, kp_ref, kc_ref, kn_ref, vp_ref, vc_ref, vn_ref, ck_ref, cv_ref,
                     sink_ref, o_ref, bias_s):
    j = pl.program_id(1)
    nb = DEC_SEQ // QBLOCK
    cc = lax.broadcasted_iota(jnp.int32, (3 * QBLOCK, QBLOCK), 0)
    r = lax.broadcasted_iota(jnp.int32, (3 * QBLOCK, QBLOCK), 1)
    lo = jnp.where(j >= 1, 0, QBLOCK)
    hi = jnp.where(j <= nb - 2, 3 * QBLOCK, 2 * QBLOCK)
    ok = (jnp.abs(QBLOCK + r - cc) <= WINDOW) & (cc >= lo) & (cc < hi)
    bias_s[...] = jnp.where(ok, 0.0, -jnp.inf)
    for kv in range(KVH):
        ks = slice(kv * HD, (kv + 1) * HD)
        k_all = jnp.concatenate([_bf(ck_ref[0, 0, kv]), kp_ref[:, ks], kc_ref[:, ks], kn_ref[:, ks]], axis=0)
        v_all = jnp.concatenate([_bf(cv_ref[0, 0, kv]), vp_ref[:, ks], vc_ref[:, ks], vn_ref[:, ks]], axis=0)
        v1 = jnp.concatenate([v_all, jnp.ones_like(v_all)], axis=1)
        outs = []
        for h in range(kv * (HC // KVH), (kv + 1) * (HC // KVH)):
            sink = sink_ref[h]
            st = _mm_nt(k_all, _bf(q_ref[:, h * HD:(h + 1) * HD]))
            s_ctx = st[:PAST_LEN]
            s_loc = st[PAST_LEN:] + bias_s[...]
            m = jnp.maximum(jnp.maximum(jnp.max(s_ctx, axis=0, keepdims=True),
                                        jnp.max(s_loc, axis=0, keepdims=True)), sink)
            p = _bf(jnp.concatenate([jnp.exp(s_ctx - m), jnp.exp(s_loc - m)], axis=0))
            ov = _mm_tn(v1, p)
            outs.append(ov[:HD] / (ov[HD:HD + 1] + jnp.exp(sink - m)))
        for pr in range(HC // KVH // 2):
            lane0 = (kv * (HC // KVH) + 2 * pr) * HD
            o_ref[:, lane0:lane0 + 2 * HD] = _bf(jnp.concatenate(outs[2 * pr:2 * pr + 2], axis=0).T)


def _attn_latent(qkv, cache_k, cache_v, sink, o):
    nb = DEC_SEQ // QBLOCK
    base = NP_ROWS // QBLOCK
    blk = lambda b, j: base + b * nb + j
    prev = lambda b, j: base + b * nb + jnp.maximum(j - 1, 0)
    nxt = lambda b, j: base + b * nb + jnp.minimum(j + 1, nb - 1)
    kcol, vcol = HC * HD // 256, HC * HD // 256 + 1
    cache_spec = pl.BlockSpec((1, 1, KVH, PAST_LEN, HD), lambda b, j: (b, o, 0, 0, 0))
    return pl.pallas_call(
        _attn_lat_kernel,
        out_shape=jax.ShapeDtypeStruct((NS_ROWS, HC * HD), BF16),
        grid=(DEC_BATCH, nb),
        in_specs=[pl.BlockSpec((QBLOCK, HC * HD), lambda b, j: (blk(b, j), 0)),
                  pl.BlockSpec((QBLOCK, 256), lambda b, j: (prev(b, j), kcol)),
                  pl.BlockSpec((QBLOCK, 256), lambda b, j: (blk(b, j), kcol)),
                  pl.BlockSpec((QBLOCK, 256), lambda b, j: (nxt(b, j), kcol)),
                  pl.BlockSpec((QBLOCK, 256), lambda b, j: (prev(b, j), vcol)),
                  pl.BlockSpec((QBLOCK, 256), lambda b, j: (blk(b, j), vcol)),
                  pl.BlockSpec((QBLOCK, 256), lambda b, j: (nxt(b, j), vcol)),
                  cache_spec, cache_spec,
                  pl.BlockSpec(memory_space=pltpu.SMEM)],
        out_specs=pl.BlockSpec((QBLOCK, HC * HD), lambda b, j: (b * nb + j, 0)),
        scratch_shapes=[pltpu.VMEM((3 * QBLOCK, QBLOCK), F32)],
        compiler_params=_params(2),
        name="attn_latent",
    )(qkv, qkv, qkv, qkv, qkv, qkv, qkv, cache_k, cache_v, sink)


def _route(lg):
    lane = lax.broadcasted_iota(jnp.int32, lg.shape, 1)
    neg = -jnp.inf
    big = 1 << 20
    is_grp = jnp.logical_and(lane >= N_EXPERTS, lane < N_EXPERTS + N_GROUPS)
    mg = jnp.max(jnp.where(is_grp, lg, neg), axis=1, keepdims=True)
    g_lane = jnp.min(jnp.where(jnp.logical_and(is_grp, lg == mg), lane, big), axis=1, keepdims=True)
    g_w = 1.0 / jnp.sum(jnp.where(is_grp, jnp.exp(lg - mg), 0.0), axis=1, keepdims=True)
    g_idx = g_lane - N_EXPERTS
    in_grp = jnp.logical_and(lane >= g_idx * EPG, lane < (g_idx + 1) * EPG)
    v1 = jnp.max(jnp.where(in_grp, lg, neg), axis=1, keepdims=True)
    i1 = jnp.min(jnp.where(jnp.logical_and(in_grp, lg == v1), lane, big), axis=1, keepdims=True)
    rest = jnp.logical_and(in_grp, lane != i1)
    v2 = jnp.max(jnp.where(rest, lg, neg), axis=1, keepdims=True)
    i2 = jnp.min(jnp.where(jnp.logical_and(rest, lg == v2), lane, big), axis=1, keepdims=True)
    e2 = jnp.exp(v2 - v1)
    p1 = 1.0 / (1.0 + e2)
    p2 = e2 / (1.0 + e2)
    return jnp.where(lane == 0, i1.astype(F32),
                     jnp.where(lane == 1, i2.astype(F32),
                               jnp.where(lane == 2, p1 * g_w, jnp.where(lane == 3, p2 * g_w, 0.0))))


def _moe_dense_kernel(x_ref, sh_ref, sc_ref, wr_ref, br_ref, wg_ref, wu_ref, wd_ref, gate_ref, g_ref, b_ref,
                      *out_and_scratch, tm):
    *o_refs, xm_s, meta_s, acc_s = out_and_scratch
    grp = pl.program_id(1)

    @pl.when(grp == 0)
    def _():
        xm = x_ref[...] * (1.0 + sc_ref[0]) + sh_ref[0]
        meta_s[...] = _route(_mm(_lhs3(xm), _rhs3(wr_ref[...])) + br_ref[...])
        xm_s[...] = _bf(xm)
        acc_s[...] = jnp.zeros_like(acc_s)

    xm = xm_s[...]
    meta = meta_s[...]
    i1 = meta[:, 0:1].astype(jnp.int32)
    i2 = meta[:, 1:2].astype(jnp.int32)
    w1 = meta[:, 2:3]
    w2 = meta[:, 3:4]
    hid = []
    for e in range(EPG):
        eid = grp * EPG + e
        gate = jnp.where(i1 == eid, w1, 0.0) + jnp.where(i2 == eid, w2, 0.0)
        a = _mm(xm, wg_ref[0, e])
        u = _mm(xm, wu_ref[0, e])
        hid.append(_bf((a * _sigmoid(a)) * u * gate))
    acc_s[...] += _mm(jnp.concatenate(hid, axis=1), wd_ref[0])

    def result():
        return _residual_ln(x_ref[...], gate_ref[0], acc_s[...], g_ref[...], b_ref[...])

    last = grp == N_GROUPS - 1
    if len(o_refs) == 1:
        @pl.when(last)
        def _():
            o_refs[0][...] = result()
    else:
        is_p = pl.program_id(0) < NP_ROWS // tm

        @pl.when(jnp.logical_and(last, is_p))
        def _():
            o_refs[0][...] = result()

        @pl.when(jnp.logical_and(last, jnp.logical_not(is_p)))
        def _():
            o_refs[1][...] = result()


def _moe_dense(x, mods, layer, w_r, b_r, wg, wu, wd, ln_g, ln_b, split_out=False):
    tm = 1024
    npt = NP_ROWS // tm
    row = lambda i, g: (i, 0)
    const2 = lambda i, g: (0, 0)
    if split_out:
        half = jax.ShapeDtypeStruct((NP_ROWS, D), F32)
        out_shape = (half, half)
        out_specs = (pl.BlockSpec((tm, D), lambda i, g: (jnp.minimum(i, npt - 1), 0)),
                     pl.BlockSpec((tm, D), lambda i, g: (jnp.maximum(i - npt, 0), 0)))
    else:
        out_shape = jax.ShapeDtypeStruct((T, D), F32)
        out_specs = pl.BlockSpec((tm, D), row)
    return pl.pallas_call(
        functools.partial(_moe_dense_kernel, tm=tm),
        out_shape=out_shape,
        grid=(T // tm, N_GROUPS),
        in_specs=[pl.BlockSpec((tm, D), row), _mod_spec(layer, 3, tm), _mod_spec(layer, 4, tm),
                  pl.BlockSpec((D, 128), const2), pl.BlockSpec((1, 128), const2),
                  pl.BlockSpec((1, EPG, D, EXPERT_FF), lambda i, g: (g, 0, 0, 0)),
                  pl.BlockSpec((1, EPG, D, EXPERT_FF), lambda i, g: (g, 0, 0, 0)),
                  pl.BlockSpec((1, EPG * EXPERT_FF, D), lambda i, g: (g, 0, 0)),
                  _mod_spec(layer, 5, tm), pl.BlockSpec((1, D), const2), pl.BlockSpec((1, D), const2)],
        out_specs=out_specs,
        scratch_shapes=[pltpu.VMEM((tm, D), BF16), pltpu.VMEM((tm, 128), F32), pltpu.VMEM((tm, D), F32)],
        compiler_params=_params(2, VMEM_LIMIT + (2 * tm * D * 4 if split_out else 0)),
        name="moe_dense",
    )(x, mods, mods, w_r, b_r, wg, wu, wd, mods, ln_g, ln_b)


def _permute_even_w(w):
    a_end = 4 * HA * DH
    g_end = a_end + 4 * HA
    c_end = g_end + 3 * HB * DH
    z_end = c_end + HB * DH
    small = jnp.concatenate([w[:, a_end:g_end], w[:, z_end:]], axis=1)
    pad = jnp.zeros((w.shape[0], 128 - small.shape[1]), w.dtype)
    return jnp.concatenate([w[:, g_end:c_end], w[:, :a_end], w[:, c_end:z_end], small, pad], axis=1)


def _lane_row(vals, offset):
    return jnp.zeros((1, 128), F32).at[0, offset:offset + vals.shape[0]].set(vals.astype(F32))


def _pair_rows(vals, offset):
    v32 = jnp.zeros((32,), F32).at[offset:offset + vals.shape[0]].set(vals.astype(F32))
    return jnp.repeat(v32.reshape(16, 2), CHUNK, axis=1)


def kernel(x_prompt, x_sample, c, c_ctx, state_mlstm_c, state_mlstm_n, state_mlstm_m, state_delta, cache_k, cache_v, w_mod, b_mod, ln_g, ln_b, w_in_even, mlstm_gate_b, mlstm_norm_g, delta_conv_w, delta_a_log, delta_dt_bias, delta_norm_g, w_out_even, w_qkv_odd, attn_sink, w_out_odd, w_grp, b_grp, w_erouter, b_erouter, w_gate, w_up, w_down):
    x = (x_prompt.reshape(NP_ROWS, D), x_sample.reshape(NS_ROWS, D))
    cvecs = jnp.concatenate([c_ctx[None, :], c, jnp.zeros((N_MOD_ROWS - 1 - DEC_BATCH, D), F32)], axis=0)
    mods = _modulation(cvecs, w_mod, b_mod)
    tables = _rope_tables()
    m0_all = jnp.broadcast_to(state_mlstm_m.reshape(DEC_BATCH, N_EVEN, 2 * HA, 1), (DEC_BATCH, N_EVEN, 2 * HA, 128))

    out_mc, out_mn, out_mm, out_ds = [], [], [], []
    new_k = jnp.zeros((BATCH, N_ODD, KVH, SEQ, HD), F32)
    new_v = jnp.zeros((BATCH, N_ODD, KVH, SEQ, HD), F32)
    for l in range(DEPTH):
        if l % 2 == 0:
            e = l // 2
            proj = _even_proj(x, mods, l, _bf(_permute_even_w(w_in_even[e])))
            gates_t2 = (proj[:, EVEN_W - 128:EVEN_W - 96].reshape(T // CHUNK, CHUNK, 32).transpose(0, 2, 1)
                        .reshape(T // CHUNK, 16, 128))
            gb = mlstm_gate_b[e].reshape(-1)
            *h_scan, mc, mn, mm = _mlstm_scan(proj, gates_t2, _lane_row(gb, 0), _pair_rows(gb, 0),
                                              state_mlstm_c, state_mlstm_n, m0_all, e)
            qkv = _delta_prep(proj, delta_conv_w[e])
            dtb = delta_dt_bias[e].reshape(-1)
            nea = -jnp.exp(delta_a_log[e].astype(F32)).reshape(-1)
            *o_scan, ds = _delta_scan(qkv, proj, gates_t2, _lane_row(dtb, 24), _pair_rows(dtb, 24),
                                      _lane_row(nea, 24), _pair_rows(nea, 24), state_delta, e)
            x = _even_out(h_scan, o_scan, proj, mlstm_norm_g[e][None, :], delta_norm_g[e][None, :],
                          _bf(w_out_even[e]), x, mods, l, ln_g[l, 0][None, :], ln_b[l, 0][None, :])
            out_mc.append(mc)
            out_mn.append(mn)
            out_mm.append(mm[:, :, 0].reshape(BATCH, 2, HA))
            out_ds.append(ds)
        else:
            o = l // 2
            qkv, new_k, new_v = _odd_proj(x, mods, l, _bf(w_qkv_odd[o]), tables, new_k, new_v, o)
            a_p = _attn_context(qkv, attn_sink[o])
            a_s = _attn_latent(qkv, cache_k, cache_v, attn_sink[o], o)
            x = _odd_out(a_p, a_s, _bf(w_out_odd[o]), x, mods, l, ln_g[l, 0][None, :], ln_b[l, 0][None, :])
        w_r = jnp.concatenate([w_erouter[l].transpose(1, 0, 2).reshape(D, N_EXPERTS), w_grp[l],
                               jnp.zeros((D, 128 - N_EXPERTS - N_GROUPS), F32)], axis=1)
        b_r = jnp.concatenate([b_erouter[l].reshape(-1), b_grp[l],
                               jnp.zeros((128 - N_EXPERTS - N_GROUPS,), F32)])[None, :]
        x = _moe_dense(x, mods, l, w_r, b_r,
                       _bf(w_gate[l]).reshape(N_GROUPS, EPG, D, EXPERT_FF),
                       _bf(w_up[l]).reshape(N_GROUPS, EPG, D, EXPERT_FF),
                       _bf(w_down[l]).reshape(N_GROUPS, EPG * EXPERT_FF, D),
                       ln_g[l, 1][None, :], ln_b[l, 1][None, :], split_out=(l == DEPTH - 1))
    return (x[0].reshape(BATCH, SEQ, D), x[1].reshape(DEC_BATCH, DEC_SEQ, D),
            jnp.stack(out_mc, 1), jnp.stack(out_mn, 1), jnp.stack(out_mm, 1), jnp.stack(out_ds, 1),
            new_k, new_v)
```

```python
import functools

import numpy as np
import jax
import jax.numpy as jnp
from jax import lax
from jax.experimental import pallas as pl
from jax.experimental.pallas import tpu as pltpu

F32 = jnp.float32
BF16 = jnp.bfloat16
HI = lax.Precision.HIGHEST

D = 1024
BATCH = 32
SEQ = 256
DEPTH = 4
DEC_BATCH = 2
DEC_SEQ = 4096
PAST_LEN = 512
GRID_W = 64
N_EVEN = 2
N_ODD = 2
HA = 4
HB = 4
DH = 128
CHUNK = 64
HC = 16
KVH = 4
HD = 64
WINDOW = 128
QBLOCK = 128
ROPE_THETA = 10000.0
N_GROUPS = 4
EPG = 4
N_EXPERTS = 16
EXPERT_FF = 256
DN_ALPHA = (2 * DEPTH) ** 0.25
LN_EPS = 1e-5

NP_ROWS = BATCH * SEQ
NS_ROWS = DEC_BATCH * DEC_SEQ
T = NP_ROWS + NS_ROWS
N_MOD_ROWS = 8
EVEN_W = 4224
QKV_W = (HC + 2 * KVH) * HD

P_CHUNKS = SEQ // CHUNK
S_CHUNKS = DEC_SEQ // CHUNK
P_STEPS = BATCH * P_CHUNKS
S_STEPS = DEC_BATCH * S_CHUNKS
P_BLOCKS = NP_ROWS // CHUNK

VMEM_LIMIT = 48 * 1024 * 1024


def _params(n_axes, vmem_limit=VMEM_LIMIT):
    return pltpu.CompilerParams(dimension_semantics=("arbitrary",) * n_axes,
                                vmem_limit_bytes=vmem_limit)


def _mm(a, b, prec=None):
    return lax.dot_general(a, b, (((1,), (0,)), ((), ())), precision=prec, preferred_element_type=F32)


def _mm_nt(a, b, prec=None):
    return lax.dot_general(a, b, (((1,), (1,)), ((), ())), precision=prec, preferred_element_type=F32)


def _mm_tn(a, b, prec=None):
    return lax.dot_general(a, b, (((0,), (0,)), ((), ())), precision=prec, preferred_element_type=F32)


def _bf(x):
    return x.astype(BF16)


def _sigmoid(x):
    return 1.0 / (1.0 + jnp.exp(-x))


def _softplus(x):
    return jnp.maximum(x, 0.0) + jnp.log1p(jnp.exp(-jnp.abs(x)))


def _log_sigmoid(x):
    return jnp.minimum(x, 0.0) - jnp.log1p(jnp.exp(-jnp.abs(x)))


def _mod_row(tile, tm):
    npt = NP_ROWS // tm
    per = DEC_SEQ // tm
    return jnp.where(tile < npt, 0, 1 + (tile - npt) // per)


def _mod_spec(layer, chunk, tm):
    def imap(i, *_):
        return ((layer * N_MOD_ROWS + _mod_row(i, tm)) * 6 + chunk, 0, 0)
    return pl.BlockSpec((1, 1, D), imap)


def _modulation_kernel(c_ref, w_ref, b_ref, o_ref):
    x = c_ref[...]
    s = x * _sigmoid(x)
    o_ref[0] = _mm(s, w_ref[0], HI) + b_ref[0]


def _modulation(cvecs, w_mod, b_mod):
    out = pl.pallas_call(
        _modulation_kernel,
        out_shape=jax.ShapeDtypeStruct((DEPTH, N_MOD_ROWS, 6 * D), F32),
        grid=(DEPTH, 6),
        in_specs=[pl.BlockSpec((N_MOD_ROWS, D), lambda l, j: (0, 0)),
                  pl.BlockSpec((1, D, D), lambda l, j: (l, 0, j)),
                  pl.BlockSpec((1, 1, D), lambda l, j: (l * 6 + j, 0, 0))],
        out_specs=pl.BlockSpec((1, N_MOD_ROWS, D), lambda l, j: (l, 0, j)),
        compiler_params=_params(2),
        name="modulation",
    )(cvecs, w_mod, b_mod.reshape(DEPTH * 6, 1, D))
    return out.reshape(DEPTH * N_MOD_ROWS * 6, 1, D)


def _x_operands(x, tm):
    if isinstance(x, tuple):
        npt = NP_ROWS // tm
        return ([pl.BlockSpec((tm, D), lambda i, *_: (jnp.minimum(i, npt - 1), 0)),
                 pl.BlockSpec((tm, D), lambda i, *_: (jnp.maximum(i - npt, 0), 0))], list(x))
    return [pl.BlockSpec((tm, D), lambda i, *_: (i, 0))], [x]


def _load_x(x_refs, tm):
    if len(x_refs) == 1:
        return x_refs[0][...]
    return jnp.where(pl.program_id(0) < NP_ROWS // tm, x_refs[0][...], x_refs[1][...])


def _proj_kernel(*refs, tm, n_x):
    x_refs, (sh_ref, sc_ref, w_ref, o_ref) = refs[:n_x], refs[n_x:]
    xm = _load_x(x_refs, tm) * (1.0 + sc_ref[0]) + sh_ref[0]
    o_ref[...] = _mm(_bf(xm), w_ref[...])


def _even_proj(x, mods, layer, w):
    tm = 256
    x_specs, x_arrays = _x_operands(x, tm)
    return pl.pallas_call(
        functools.partial(_proj_kernel, tm=tm, n_x=len(x_arrays)),
        out_shape=jax.ShapeDtypeStruct((T, EVEN_W), F32),
        grid=(T // tm,),
        in_specs=x_specs + [_mod_spec(layer, 0, tm), _mod_spec(layer, 1, tm),
                            pl.BlockSpec((D, EVEN_W), lambda i: (0, 0))],
        out_specs=pl.BlockSpec((tm, EVEN_W), lambda i: (i, 0)),
        compiler_params=_params(1),
        name="even_proj",
    )(*x_arrays, mods, mods, w)


def _qkv_kernel(x_ref, sh_ref, sc_ref, w_ref, cos_ref, sa_ref, sb_ref, kc_in, vc_in, o_ref, kc_ref, vc_ref, *, tm):
    del kc_in, vc_in
    i = pl.program_id(0)
    xm = x_ref[...] * (1.0 + sc_ref[0]) + sh_ref[0]
    acc = _mm(_bf(xm), w_ref[...])

    @pl.when(i < NP_ROWS // tm)
    def _():
        for b in range(tm // SEQ):
            for kv in range(KVH):
                k0 = HC * HD + kv * HD
                v0 = (HC + KVH) * HD + kv * HD
                kc_ref[b, 0, kv] = acc[b * SEQ:(b + 1) * SEQ, k0:k0 + HD]
                vc_ref[b, 0, kv] = acc[b * SEQ:(b + 1) * SEQ, v0:v0 + HD]

    n_q = HC * HD // 128
    n_k = KVH * HD // 128
    is_latent = i >= NP_ROWS // tm
    cos = jnp.where(is_latent, cos_ref[...], 1.0)
    sa = jnp.where(is_latent, sa_ref[...], 0.0)
    sb = jnp.where(is_latent, sb_ref[...], 0.0)
    for g in range(n_q + n_k):
        blk = acc[:, g * 128:(g + 1) * 128]
        if g < n_q:
            blk = blk * (HD ** -0.5)
        rot = blk * cos + pltpu.roll(blk, 112, 1) * sa + pltpu.roll(blk, 16, 1) * sb
        o_ref[:, g * 128:(g + 1) * 128] = rot
    o_ref[:, (n_q + n_k) * 128:] = acc[:, (n_q + n_k) * 128:]


def _rope_tables():
    half = HD // 4
    inv = np.float32(ROPE_THETA) ** (-np.arange(half, dtype=np.float32) / np.float32(half))
    pos = np.arange(DEC_SEQ)
    row = (pos // GRID_W).astype(np.float32)[:, None] * inv[None, :]
    col = (pos % GRID_W).astype(np.float32)[:, None] * inv[None, :]
    cos = np.concatenate([np.cos(row), np.cos(row), np.cos(col), np.cos(col)], axis=-1)
    sin = np.concatenate([np.sin(row), np.sin(row), np.sin(col), np.sin(col)], axis=-1)
    first = (np.arange(HD) % 32) < 16
    sa = np.where(first, -sin, 0.0)
    sb = np.where(first, 0.0, sin)
    tile2 = lambda t: jnp.asarray(np.concatenate([t, t], axis=-1), F32)
    return tile2(cos), tile2(sa), tile2(sb)


def _odd_proj(x, mods, layer, w, tables, cache_k, cache_v, o):
    tm = 512
    npt = NP_ROWS // tm
    per = DEC_SEQ // tm
    tab_spec = pl.BlockSpec((tm, 128), lambda i: (jnp.where(i < npt, 0, (i - npt) % per), 0))
    cache_spec = pl.BlockSpec((tm // SEQ, 1, KVH, SEQ, HD), lambda i: (jnp.minimum(i, npt - 1), o, 0, 0, 0))
    cache_shape = jax.ShapeDtypeStruct((BATCH, N_ODD, KVH, SEQ, HD), F32)
    return pl.pallas_call(
        functools.partial(_qkv_kernel, tm=tm),
        out_shape=(jax.ShapeDtypeStruct((T, QKV_W), F32), cache_shape, cache_shape),
        grid=(T // tm,),
        in_specs=[pl.BlockSpec((tm, D), lambda i: (i, 0)),
                  _mod_spec(layer, 0, tm), _mod_spec(layer, 1, tm),
                  pl.BlockSpec((D, QKV_W), lambda i: (0, 0)),
                  tab_spec, tab_spec, tab_spec,
                  pl.BlockSpec(memory_space=pl.ANY), pl.BlockSpec(memory_space=pl.ANY)],
        out_specs=(pl.BlockSpec((tm, QKV_W), lambda i: (i, 0)), cache_spec, cache_spec),
        input_output_aliases={7: 1, 8: 2},
        compiler_params=_params(1),
        name="odd_qkv_proj",
    )(x, mods, mods, w, *tables, cache_k, cache_v)


PREP_ROWS = 256


def _delta_prep_kernel(x_ref, prev_ref, next_ref, w_ref, o_ref):
    i = pl.program_id(0)
    npb = NP_ROWS // PREP_ROWS
    per = DEC_SEQ // PREP_ROWS
    is_latent = i >= npb
    pos = (i - npb) % per
    has_prev = jnp.logical_and(is_latent, pos > 0)
    has_next = jnp.logical_and(is_latent, pos < per - 1)
    x = x_ref[...]
    w = w_ref[...]
    rows = lax.broadcasted_iota(jnp.int32, x.shape, 0)
    prev_row = jnp.where(has_prev, prev_ref[7:8, :], 0.0)
    next_row = jnp.where(has_next, next_ref[0:1, :], 0.0)
    xm1 = jnp.where(rows == 0, prev_row, pltpu.roll(x, 1, 0))
    xp1 = jnp.where(rows == PREP_ROWS - 1, next_row, pltpu.roll(x, PREP_ROWS - 1, 0))
    y = xm1 * w[0:1, :] + x * w[1:2, :] + xp1 * w[2:3, :]
    y = y * _sigmoid(y)
    for h in range(3 * HB):
        yh = y[:, h * DH:(h + 1) * DH]
        if h < 2 * HB:
            inv = lax.rsqrt(jnp.sum(yh * yh, axis=-1, keepdims=True) + 1e-6)
            yh = yh * (inv * (DH ** -0.5) if h < HB else inv)
        o_ref[:, h * DH:(h + 1) * DH] = yh


def _delta_prep(proj, conv_w):
    nblk = T // PREP_ROWS
    sub = PREP_ROWS // 8
    last8 = T // 8 - 1
    return pl.pallas_call(
        _delta_prep_kernel,
        out_shape=jax.ShapeDtypeStruct((T, 3 * 512), F32),
        grid=(nblk,),
        in_specs=[pl.BlockSpec((PREP_ROWS, 3 * 512), lambda i: (i, 0)),
                  pl.BlockSpec((8, 3 * 512), lambda i: (jnp.maximum(i * sub - 1, 0), 0)),
                  pl.BlockSpec((8, 3 * 512), lambda i: (jnp.minimum((i + 1) * sub, last8), 0)),
                  pl.BlockSpec((3, 3 * 512), lambda i: (0, 0))],
        out_specs=pl.BlockSpec((PREP_ROWS, 3 * 512), lambda i: (i, 0)),
        compiler_params=_params(1),
        name="delta_prep",
    )(proj, proj, proj, conv_w)


assert P_STEPS == S_STEPS
SCAN_STEPS = P_STEPS


def _bwd_local(s, nc):
    return (s // nc) * nc + nc - 1 - s % nc


def _scan_blocks():
    return (lambda s: s, lambda s: _bwd_local(s, P_CHUNKS),
            lambda s: P_BLOCKS + s, lambda s: P_BLOCKS + _bwd_local(s, S_CHUNKS))


def _scan_specs(xcol):
    gcol = EVEN_W // 128 - 1
    blocks = _scan_blocks()
    return ([pl.BlockSpec((CHUNK, 3 * 512), lambda s, f=f: (f(s), xcol)) for f in blocks]
            + [pl.BlockSpec((CHUNK, 128), lambda s, f=f: (f(s), gcol)) for f in blocks]
            + [pl.BlockSpec((1, 16, 128), lambda s, f=f: (f(s), 0, 0)) for f in blocks])


def _scan_out_specs():
    local = (lambda s: s, lambda s: _bwd_local(s, P_CHUNKS), lambda s: s, lambda s: _bwd_local(s, S_CHUNKS))
    return [pl.BlockSpec((CHUNK, 512), lambda s, f=f: (f(s), 0)) for f in local]


def _mlstm_kernel(xpf_ref, xpb_ref, xsf_ref, xsb_ref, gpf_ref, gpb_ref, gsf_ref, gsb_ref,
                  rpf_ref, rpb_ref, rsf_ref, rsb_ref,
                  brow_ref, b2_ref, c0_ref, n0_ref, m0_ref,
                  hpf_ref, hpb_ref, hsf_ref, hsb_ref, c_out, n_out, m_out,
                  cp_s, np_s, mp_s, cs_s, ns_s, ms_s):
    s = pl.program_id(0)
    jp = s % P_CHUNKS
    js = s % S_CHUNKS

    @pl.when(jp == 0)
    def _():
        cp_s[...] = jnp.zeros_like(cp_s)
        np_s[...] = jnp.zeros_like(np_s)
        mp_s[...] = jnp.zeros_like(mp_s)

    @pl.when(js == 0)
    def _():
        for d in range(2):
            for h in range(HA):
                cs_s[d * HA + h] = c0_ref[0, 0, d, h]
                ns_s[d * HA + h] = jnp.broadcast_to(n0_ref[0, 0, d, h:h + 1, :], (DH, 128)).T
        ms_s[...] = m0_ref[0, 0]

    states = ((cp_s, np_s, mp_s), (cs_s, ns_s, ms_s))

    row = lax.broadcasted_iota(jnp.int32, (CHUNK, 128), 0)
    lane = lax.broadcasted_iota(jnp.int32, (CHUNK, 128), 1)
    left = lane < CHUNK
    lcol = jnp.where(left, lane, lane - CHUNK)
    r64 = lax.broadcasted_iota(jnp.int32, (CHUNK, CHUNK), 0)
    c64 = lax.broadcasted_iota(jnp.int32, (CHUNK, CHUNK), 1)
    r128 = lax.broadcasted_iota(jnp.int32, (128, 128), 0)
    c128 = lax.broadcasted_iota(jnp.int32, (128, 128), 1)
    same_half = (r128 < CHUNK) == (c128 < CHUNK)
    neg = -jnp.inf
    ins = ((xpf_ref, gpf_ref, rpf_ref, hpf_ref), (xpb_ref, gpb_ref, rpb_ref, hpb_ref),
           (xsf_ref, gsf_ref, rsf_ref, hsf_ref), (xsb_ref, gsb_ref, rsb_ref, hsb_ref))

    sel_r = lax.broadcasted_iota(jnp.int32, (128, HA * 128), 0)
    sel_h = lax.broadcasted_iota(jnp.int32, (128, HA * 128), 1) // 128
    ones_lr = lax.broadcasted_iota(jnp.int32, (256, 256), 0)
    ones_lc = lax.broadcasted_iota(jnp.int32, (256, 256), 1)
    half_sum = (((ones_lr % 128) < CHUNK) == (ones_lc < 128)).astype(BF16)
    ones_l = jnp.ones((CHUNK, 128), BF16)
    rows256 = lax.broadcasted_iota(jnp.int32, (CHUNK, 256), 0)

    def running_max(x, d):
        sh = 1
        while sh < CHUNK:
            if d == 0:
                x = jnp.maximum(x, jnp.where(rows256 >= sh, pltpu.roll(x, sh, 0), neg))
            else:
                x = jnp.maximum(x, jnp.where(rows256 < CHUNK - sh, pltpu.roll(x, CHUNK - sh, 0), neg))
            sh *= 2
        return x

    gate = []
    for u in range(4):
        d = u % 2
        gc_ref, gr_ref = ins[u][1], ins[u][2]
        if d == 0:
            incl_p, tri_c, tri_r = lcol <= row, c64 <= r64, jnp.logical_and(same_half, r128 <= c128)
        else:
            incl_p, tri_c, tri_r = lcol >= row, c64 >= r64, jnp.logical_and(same_half, r128 >= c128)
        gc = gc_ref[...] + brow_ref[...]
        gr = gr_ref[0] + b2_ref[...]
        i_rep = _replicate(gc, (sel_r == d * 8 + sel_h).astype(BF16))
        b_rep = _replicate(_prefix_cols(tri_c.astype(BF16), _log_sigmoid(gc)),
                           (sel_r == d * 8 + 4 + sel_h).astype(BF16))
        cs_row = _prefix_rows(_log_sigmoid(gr), tri_r.astype(BF16))
        gate.append((incl_p, gr, cs_row, i_rep, b_rep))

    st = []
    for u in range(4):
        d = u % 2
        c_s, n_s, m_s = states[u // 2]
        for hp in range(HA // 2):
            x_ref = ins[u][0]
            incl_p, gr, cs_row, i_rep, b_rep = gate[u]
            end = CHUNK - 1 if d == 0 else 0
            heads = []
            for h in (2 * hp, 2 * hp + 1):
                c = d * HA + h
                heads.append(dict(c=c, h=h, q=x_ref[:, h * DH:(h + 1) * DH],
                                  k=x_ref[:, 512 + h * DH:512 + (h + 1) * DH] * (DH ** -0.5),
                                  v=x_ref[:, 1024 + h * DH:1024 + (h + 1) * DH],
                                  i=i_rep[:, h * 128:(h + 1) * 128], b=b_rep[:, h * 128:(h + 1) * 128],
                                  m=m_s[c:c + 1, :]))
            ha, hb_ = heads
            i_row = gr[d * 4 + hp:d * 4 + hp + 1, :]
            b_row = cs_row[d * 4 + 2 + hp:d * 4 + 3 + hp, :]
            run = running_max(jnp.concatenate([ha["i"] - ha["b"], hb_["i"] - hb_["b"]], axis=1), d)
            for idx, hd in enumerate(heads):
                top = jnp.maximum(hd["m"], run[:, idx * 128:(idx + 1) * 128])
                hd["m_t"] = hd["b"] + top
                hd["w_inter"] = jnp.exp(hd["m"] - top)
            b_p = jnp.where(left, ha["b"], hb_["b"])
            m_t_p = jnp.where(left, ha["m_t"], hb_["m_t"])
            dmat = jnp.where(incl_p, b_p - b_row + i_row, neg)
            q_cat = _bf(jnp.concatenate([ha["q"], hb_["q"]], axis=1))
            k_bd = _bf(_block_diag2(ha["k"], hb_["k"]))
            st.append(dict(u=u, d=d, heads=heads, sc=_mm_nt(q_cat, k_bd) * jnp.exp(dmat - m_t_p)))

    for p in st:
        ha, hb_ = p["heads"]
        h_ref = ins[p["u"]][3]
        c_s, n_s, m_s = states[p["u"] // 2]
        end = CHUNK - 1 if p["d"] == 0 else 0
        sc = p["sc"]
        sv = _mm(_bf(sc), _bf(_block_diag2(ha["v"], hb_["v"])))
        dens = _mm(jnp.concatenate(_split2(sc), axis=1), half_sum)
        for idx, hd in enumerate((ha, hb_)):
            c, h, q, k = hd["c"], hd["h"], hd["q"], hd["k"]
            c_mat = c_s[c]
            n_mat = n_s[c]
            q_cn = _mm(_bf(q), _bf(jnp.concatenate([c_mat, n_mat], axis=1)))
            num = sv[:, idx * DH:(idx + 1) * DH] + hd["w_inter"] * q_cn[:, :DH]
            den = dens[:, idx * 128:(idx + 1) * 128] + hd["w_inter"] * q_cn[:, DH:]
            h_ref[:, h * DH:(h + 1) * DH] = num / jnp.maximum(jnp.abs(den), jnp.exp(-hd["m_t"]))
            b_last = hd["b"][end:end + 1, :]
            g_end = b_last - hd["b"] + hd["i"]
            m_new = jnp.maximum(b_last + hd["m"], jnp.max(g_end, axis=0, keepdims=True))
            kwb = _bf(k * jnp.exp(g_end - m_new))
            decay = jnp.exp(b_last + hd["m"] - m_new)
            upd = _mm_tn(kwb, jnp.concatenate([_bf(hd["v"]), ones_l], axis=1))
            c_s[c] = decay * c_mat + upd[:, :DH]
            n_s[c] = decay * n_mat + upd[:, DH:]
            m_s[c:c + 1, :] = m_new

    @pl.when(jp == P_CHUNKS - 1)
    def _():
        for d in range(2):
            for h in range(HA):
                c_out[0, d, h] = cp_s[d * HA + h]
                n_out[0, d, h:h + 1, :] = np_s[d * HA + h].T[0:1, :]
        m_out[0] = mp_s[...]


def _mlstm_scan(proj, gates_t2, bias_row, bias2, c0, n0, m0, e):
    const2 = lambda s: (0, 0)
    in_specs = (_scan_specs(1)
                + [pl.BlockSpec((1, 128), const2), pl.BlockSpec((16, 128), const2),
                   pl.BlockSpec((1, 1, 2, HA, DH, DH), lambda s: (s // S_CHUNKS, e, 0, 0, 0, 0)),
                   pl.BlockSpec((1, 1, 2, HA, DH), lambda s: (s // S_CHUNKS, e, 0, 0, 0)),
                   pl.BlockSpec((1, 1, 2 * HA, 128), lambda s: (s // S_CHUNKS, e, 0, 0))])
    half = jax.ShapeDtypeStruct((NP_ROWS, HA * DH), F32)
    out_shape = (half, half, half, half,
                 jax.ShapeDtypeStruct((BATCH, 2, HA, DH, DH), F32),
                 jax.ShapeDtypeStruct((BATCH, 2, HA, DH), F32),
                 jax.ShapeDtypeStruct((BATCH, 2 * HA, 128), F32))
    out_specs = _scan_out_specs() + [
        pl.BlockSpec((1, 2, HA, DH, DH), lambda s: (s // P_CHUNKS, 0, 0, 0, 0)),
        pl.BlockSpec((1, 2, HA, DH), lambda s: (s // P_CHUNKS, 0, 0, 0)),
        pl.BlockSpec((1, 2 * HA, 128), lambda s: (s // P_CHUNKS, 0, 0))]
    state = [pltpu.VMEM((2 * HA, DH, DH), F32), pltpu.VMEM((2 * HA, DH, 128), F32), pltpu.VMEM((2 * HA, 128), F32)]
    return pl.pallas_call(
        _mlstm_kernel,
        out_shape=out_shape,
        grid=(SCAN_STEPS,),
        in_specs=in_specs,
        out_specs=out_specs,
        scratch_shapes=state + state,
        compiler_params=_params(1),
        name="mlstm_scan",
    )(proj, proj, proj, proj, proj, proj, proj, proj, gates_t2, gates_t2, gates_t2, gates_t2,
      bias_row, bias2, c0, n0, m0)


def _split2(x):
    hi = _bf(x)
    return hi, _bf(x - hi.astype(F32))


def _split3(x):
    h1 = _bf(x)
    r1 = x - h1.astype(F32)
    h2 = _bf(r1)
    return h1, h2, _bf(r1 - h2.astype(F32))


def _lhs3(x):
    hi, lo = _split2(x)
    return jnp.concatenate([hi, lo, hi], axis=1)


def _rhs3(x):
    hi, lo = _split2(x)
    return jnp.concatenate([hi, hi, lo], axis=0)


def _prefix_cols(tri_bf, x):
    n = x.shape[1]
    r = _mm(tri_bf, jnp.concatenate(_split3(x), axis=1))
    return r[:, :n] + r[:, n:2 * n] + r[:, 2 * n:]


def _replicate(x, sel):
    return _mm(jnp.concatenate(_split3(x), axis=1), jnp.concatenate([sel, sel, sel], axis=0))


def _prefix_rows(x, tri_bf):
    m = x.shape[0]
    r = _mm(jnp.concatenate(_split3(x), axis=0), tri_bf)
    return r[:m] + r[m:2 * m] + r[2 * m:]


LEVELS = tuple(range(6))


def _block_diag2(a, b):
    z = jnp.zeros_like(a)
    return jnp.concatenate([jnp.concatenate([a, z], axis=1), jnp.concatenate([z, b], axis=1)], axis=0)


def _delta_kernel(xpf_ref, xpb_ref, xsf_ref, xsb_ref, gpf_ref, gpb_ref, gsf_ref, gsb_ref,
                  rpf_ref, rpb_ref, rsf_ref, rsb_ref,
                  dtrow_ref, dt2_ref, narow_ref, na2_ref, s0_ref,
                  opf_ref, opb_ref, osf_ref, osb_ref, s_out, sp_s, ss_s):
    s = pl.program_id(0)
    jp = s % P_CHUNKS
    js = s % S_CHUNKS

    @pl.when(jp == 0)
    def _():
        sp_s[...] = jnp.zeros_like(sp_s)

    @pl.when(js == 0)
    def _():
        for d in range(2):
            for h in range(HB):
                ss_s[d * HB + h] = s0_ref[0, 0, d, h]

    states = (sp_s, ss_s)

    row = lax.broadcasted_iota(jnp.int32, (CHUNK, 128), 0)
    lane = lax.broadcasted_iota(jnp.int32, (CHUNK, 128), 1)
    left = lane < CHUNK
    lcol = jnp.where(left, lane, lane - CHUNK)
    eye_p = (lcol == row).astype(F32)
    r64 = lax.broadcasted_iota(jnp.int32, (CHUNK, CHUNK), 0)
    c64 = lax.broadcasted_iota(jnp.int32, (CHUNK, CHUNK), 1)
    r128 = lax.broadcasted_iota(jnp.int32, (128, 128), 0)
    c128 = lax.broadcasted_iota(jnp.int32, (128, 128), 1)
    same_half = (r128 < CHUNK) == (c128 < CHUNK)
    ins = ((xpf_ref, gpf_ref, rpf_ref, opf_ref), (xpb_ref, gpb_ref, rpb_ref, opb_ref),
           (xsf_ref, gsf_ref, rsf_ref, osf_ref), (xsb_ref, gsb_ref, rsb_ref, osb_ref))

    gate = []
    for u in range(4):
        d = u % 2
        gc_ref, gr_ref = ins[u][1], ins[u][2]
        if d == 0:
            incl_p, strict_p = lcol <= row, lcol < row
            tri_c, tri_r = c64 <= r64, jnp.logical_and(same_half, r128 <= c128)
        else:
            incl_p, strict_p = lcol >= row, lcol > row
            tri_c, tri_r = c64 >= r64, jnp.logical_and(same_half, r128 >= c128)
        xc = gc_ref[...]
        xr = gr_ref[0]
        la_c = narow_ref[...] * _softplus(xc + dtrow_ref[...])
        la_r = na2_ref[...] * _softplus(xr + dt2_ref[...])
        g_c = _prefix_cols(tri_c.astype(BF16), la_c)
        g_r = _prefix_rows(la_r, tri_r.astype(BF16))
        gate.append((incl_p, strict_p, _sigmoid(xc), g_c, g_r))

    def bd_rhs(hi, lo, mask=None):
        top = left if mask is None else jnp.logical_and(left, mask)
        bot = jnp.logical_not(left) if mask is None else jnp.logical_and(jnp.logical_not(left), mask)
        zero = jnp.zeros_like(hi)
        blocks = [jnp.concatenate([jnp.where(top, x, zero), jnp.where(bot, x, zero)], axis=0) for x in (hi, lo)]
        return jnp.concatenate([blocks[0], blocks[0], blocks[1]], axis=0)

    off = ([], [])
    for lv in LEVELS:
        same = jnp.right_shift(row, lv + 1) == jnp.right_shift(lcol, lv + 1)
        r_hi = jnp.bitwise_and(jnp.right_shift(row, lv), 1) == 1
        c_hi = jnp.bitwise_and(jnp.right_shift(lcol, lv), 1) == 1
        off[0].append(same & r_hi & jnp.logical_not(c_hi))
        off[1].append(same & jnp.logical_not(r_hi) & c_hi)

    pairs = [(u, hp) for u in range(4) for hp in range(HB // 2)]
    st = []
    for u, hp in pairs:
        d = u % 2
        x_ref = ins[u][0]
        incl_p, strict_p, beta_c, g_c, g_r = gate[u]
        end = CHUNK - 1 if d == 0 else 0
        heads = []
        for h in (2 * hp, 2 * hp + 1):
            ib = 16 + d * HB + h
            ia = 24 + d * HB + h
            gcol = g_c[:, ia:ia + 1]
            bc = beta_c[:, ib:ib + 1]
            q = x_ref[:, h * DH:(h + 1) * DH]
            k = x_ref[:, 512 + h * DH:512 + (h + 1) * DH]
            v = x_ref[:, 1024 + h * DH:1024 + (h + 1) * DH]
            heads.append(dict(h=h, gcol=gcol, bc=bc, q=q, k=k, v=v, kb=k * bc, eg=jnp.exp(gcol),
                              g_last=gcol[end:end + 1, :]))
        ha, hb_ = heads
        r = 12 + d * 2 + hp
        gcol_p = jnp.where(left, ha["gcol"], hb_["gcol"])
        decay = jnp.exp(jnp.where(incl_p, gcol_p - g_r[r:r + 1, :], -jnp.inf))
        k_bd = _bf(_block_diag2(ha["k"], hb_["k"]))
        kb_cat = _bf(jnp.concatenate([ha["kb"], hb_["kb"]], axis=1))
        q_cat = _bf(jnp.concatenate([ha["q"], hb_["q"]], axis=1))
        kq = _mm_nt(jnp.concatenate([kb_cat, q_cat], axis=0), k_bd)
        a_mat = jnp.where(strict_p, kq[:CHUNK] * decay, 0.0)
        qk = kq[CHUNK:] * decay
        a_hi, a_lo = _split2(a_mat)
        st.append(dict(u=u, d=d, heads=heads, t=eye_p - jnp.where(off[d][0], a_mat, 0.0), qk=qk,
                       am=[bd_rhs(a_hi, a_lo, m) for m in off[d][1:]]))

    for li in range(len(LEVELS) - 1):
        for p in st:
            p["t_parts"] = _split2(p["t"])
            t_hi, t_lo = p["t_parts"]
            p["w"] = _mm(jnp.concatenate([t_hi, t_lo, t_hi], axis=1), p["am"][li])
        for p in st:
            p["t"] = p["t"] - _mm(_lhs3(p["w"]), bd_rhs(*p["t_parts"]))

    for p in st:
        ha, hb_ = p["heads"]
        o_ref = ins[p["u"]][3]
        s_s = states[p["u"] // 2]
        rhs_a = jnp.concatenate([ha["v"] * ha["bc"], ha["kb"] * ha["eg"]], axis=1)
        rhs_b = jnp.concatenate([hb_["v"] * hb_["bc"], hb_["kb"] * hb_["eg"]], axis=1)
        (a_hi, a_lo), (b_hi, b_lo) = _split2(rhs_a), _split2(rhs_b)
        bd_hi, bd_lo = _block_diag2(a_hi, b_hi), _block_diag2(a_lo, b_lo)
        sol = _mm(_lhs3(p["t"]), jnp.concatenate([bd_hi, bd_hi, bd_lo], axis=0))
        vn = []
        for idx, hd in enumerate((ha, hb_)):
            c = p["d"] * HB + hd["h"]
            s_mat = s_s[c]
            sbf = _bf(s_mat)
            so = sol[:, idx * 2 * DH:(idx + 1) * 2 * DH]
            both = _mm(_bf(jnp.concatenate([so[:, DH:], hd["q"] * hd["eg"]], axis=0)), sbf)
            v_new = so[:, :DH] - both[:CHUNK]
            vn.append(v_new)
            hd["o1"] = both[CHUNK:]
            s_s[c] = (jnp.exp(hd["g_last"]) * s_mat
                      + _mm_tn(_bf(hd["k"] * jnp.exp(hd["g_last"] - hd["gcol"])), _bf(v_new)))
        o2 = _mm(_bf(p["qk"]), _bf(_block_diag2(vn[0], vn[1])))
        for idx, hd in enumerate((ha, hb_)):
            o_ref[:, hd["h"] * DH:(hd["h"] + 1) * DH] = hd["o1"] + o2[:, idx * DH:(idx + 1) * DH]

    @pl.when(jp == P_CHUNKS - 1)
    def _():
        for d in range(2):
            for h in range(HB):
                s_out[0, d, h] = sp_s[d * HB + h]


def _delta_scan(qkv, proj, gates_t2, dt_row, dt2, na_row, na2, s0, e):
    const2 = lambda s: (0, 0)
    in_specs = (_scan_specs(0)
                + [pl.BlockSpec((1, 128), const2), pl.BlockSpec((16, 128), const2),
                   pl.BlockSpec((1, 128), const2), pl.BlockSpec((16, 128), const2),
                   pl.BlockSpec((1, 1, 2, HB, DH, DH), lambda s: (s // S_CHUNKS, e, 0, 0, 0, 0))])
    half = jax.ShapeDtypeStruct((NP_ROWS, HB * DH), F32)
    out_shape = (half, half, half, half, jax.ShapeDtypeStruct((BATCH, 2, HB, DH, DH), F32))
    out_specs = _scan_out_specs() + [pl.BlockSpec((1, 2, HB, DH, DH), lambda s: (s // P_CHUNKS, 0, 0, 0, 0))]
    return pl.pallas_call(
        _delta_kernel,
        out_shape=out_shape,
        grid=(SCAN_STEPS,),
        in_specs=in_specs,
        out_specs=out_specs,
        scratch_shapes=[pltpu.VMEM((2 * HB, DH, DH), F32), pltpu.VMEM((2 * HB, DH, DH), F32)],
        compiler_params=_params(1),
        name="delta_scan",
    )(qkv, qkv, qkv, qkv, proj, proj, proj, proj, gates_t2, gates_t2, gates_t2, gates_t2,
      dt_row, dt2, na_row, na2, s0)


def _residual_ln(x, gate, y, g, b):
    r = DN_ALPHA * x + gate * y
    mu = jnp.mean(r, axis=-1, keepdims=True)
    var = jnp.mean(jnp.square(r - mu), axis=-1, keepdims=True)
    return (r - mu) * lax.rsqrt(var + LN_EPS) * g + b


def _even_out_kernel(*refs, tm, n_x):
    x_refs = refs[:n_x]
    (hpf_ref, hpb_ref, hsf_ref, hsb_ref, opf_ref, opb_ref, osf_ref, osb_ref,
     oa_ref, zb_ref, mg_ref, dg_ref, w_ref, gate_ref, g_ref, b_ref, o_ref) = refs[n_x:]
    is_p = pl.program_id(0) < NP_ROWS // tm
    hf_ref, hb_ref, of_ref, ob_ref = (
        lambda sl, p=p, q=q: jnp.where(is_p, p[:, sl], q[:, sl])
        for p, q in ((hpf_ref, hsf_ref), (hpb_ref, hsb_ref), (opf_ref, osf_ref), (opb_ref, osb_ref)))
    parts = []
    for h in range(HA):
        sl = slice(h * DH, (h + 1) * DH)
        hh = hf_ref(sl) + hb_ref(sl)
        mu = jnp.mean(hh, axis=-1, keepdims=True)
        var = jnp.mean(jnp.square(hh - mu), axis=-1, keepdims=True)
        parts.append(_sigmoid(oa_ref[:, sl]) * ((hh - mu) * lax.rsqrt(var + LN_EPS) * mg_ref[:, sl]))
    for h in range(HB):
        sl = slice(h * DH, (h + 1) * DH)
        oo = of_ref(sl) + ob_ref(sl)
        z = zb_ref[:, sl]
        nrm = oo * lax.rsqrt(jnp.mean(jnp.square(oo), axis=-1, keepdims=True) + LN_EPS) * dg_ref[:, sl]
        parts.append(nrm * (z * _sigmoid(z)))
    a = jnp.concatenate(parts, axis=1)
    y = _mm(_bf(a), w_ref[...])
    o_ref[...] = _residual_ln(_load_x(x_refs, tm), gate_ref[0], y, g_ref[...], b_ref[...])


def _even_out(h_scan, o_scan, proj, mg, dg, w, x, mods, layer, ln_g, ln_b):
    tm = 512
    npt = NP_ROWS // tm
    row512 = lambda i: (i, 0)
    const2 = lambda i: (0, 0)
    p_spec = pl.BlockSpec((tm, 512), lambda i: (jnp.minimum(i, npt - 1), 0))
    s_spec = pl.BlockSpec((tm, 512), lambda i: (jnp.maximum(i - npt, 0), 0))
    x_specs, x_arrays = _x_operands(x, tm)
    return pl.pallas_call(
        functools.partial(_even_out_kernel, tm=tm, n_x=len(x_arrays)),
        out_shape=jax.ShapeDtypeStruct((T, D), F32),
        grid=(T // tm,),
        in_specs=x_specs + [p_spec, p_spec, s_spec, s_spec, p_spec, p_spec, s_spec, s_spec,
                            pl.BlockSpec((tm, 512), lambda i: (i, 6)),
                            pl.BlockSpec((tm, 512), lambda i: (i, 7)),
                            pl.BlockSpec((1, 512), const2), pl.BlockSpec((1, 512), const2),
                            pl.BlockSpec((D, D), const2),
                            _mod_spec(layer, 2, tm),
                            pl.BlockSpec((1, D), const2), pl.BlockSpec((1, D), const2)],
        out_specs=pl.BlockSpec((tm, D), row512),
        compiler_params=_params(1),
        name="even_out_ln",
    )(*x_arrays, *h_scan, *o_scan, proj, proj, mg, dg, w, mods, ln_g, ln_b)


def _odd_out_kernel(ap_ref, as_ref, w_ref, x_ref, gate_ref, g_ref, b_ref, o_ref, *, tm):
    a = jnp.where(pl.program_id(0) < NP_ROWS // tm, ap_ref[...], as_ref[...])
    y = _mm(_bf(a), w_ref[...])
    o_ref[...] = _residual_ln(x_ref[...], gate_ref[0], y, g_ref[...], b_ref[...])


def _odd_out(a_prompt, a_latent, w, x, mods, layer, ln_g, ln_b):
    tm = 512
    npt = NP_ROWS // tm
    row = lambda i: (i, 0)
    const2 = lambda i: (0, 0)
    return pl.pallas_call(
        functools.partial(_odd_out_kernel, tm=tm),
        out_shape=jax.ShapeDtypeStruct((T, D), F32),
        grid=(T // tm,),
        in_specs=[pl.BlockSpec((tm, D), lambda i: (jnp.minimum(i, npt - 1), 0)),
                  pl.BlockSpec((tm, D), lambda i: (jnp.maximum(i - npt, 0), 0)),
                  pl.BlockSpec((D, D), const2), pl.BlockSpec((tm, D), row),
                  _mod_spec(layer, 2, tm), pl.BlockSpec((1, D), const2), pl.BlockSpec((1, D), const2)],
        out_specs=pl.BlockSpec((tm, D), row),
        compiler_params=_params(1),
        name="odd_out_ln",
    )(a_prompt, a_latent, w, x, mods, ln_g, ln_b)


def _attn_ctx_kernel(qkv_ref, sink_ref, o_ref):
    ones = jnp.ones((SEQ, HD), BF16)
    for kv in range(KVH):
        k = _bf(qkv_ref[:, HC * HD + kv * HD:HC * HD + (kv + 1) * HD])
        v = _bf(qkv_ref[:, (HC + KVH) * HD + kv * HD:(HC + KVH) * HD + (kv + 1) * HD])
        v1 = jnp.concatenate([v, ones], axis=1)
        heads = range(kv * (HC // KVH), (kv + 1) * (HC // KVH))
        sts = [_mm_nt(k, _bf(qkv_ref[:, h * HD:(h + 1) * HD])) for h in heads]
        ms = [jnp.maximum(jnp.max(st, axis=0, keepdims=True), sink_ref[h]) for st, h in zip(sts, heads)]
        ovs = [_mm_tn(v1, _bf(jnp.exp(st - m))) for st, m in zip(sts, ms)]
        outs = [ov[:HD] / (ov[HD:HD + 1] + jnp.exp(sink_ref[h] - m)) for ov, m, h in zip(ovs, ms, heads)]
        for pr in range(HC // KVH // 2):
            lane0 = (kv * (HC // KVH) + 2 * pr) * HD
            o_ref[:, lane0:lane0 + 2 * HD] = jnp.concatenate(outs[2 * pr:2 * pr + 2], axis=0).T


def _attn_context(qkv, sink):
    return pl.pallas_call(
        _attn_ctx_kernel,
        out_shape=jax.ShapeDtypeStruct((NP_ROWS, HC * HD), F32),
        grid=(BATCH,),
        in_specs=[pl.BlockSpec((SEQ, QKV_W), lambda b: (b, 0)),
                  pl.BlockSpec(memory_space=pltpu.SMEM)],
        out_specs=pl.BlockSpec((SEQ, HC * HD), lambda b: (b, 0)),
        compiler_params=_params(1),
        name="attn_context",
    )(qkv, sink)


def _attn_lat_kernel(q_ref, kp_ref, kc_ref, kn_ref, vp_ref, vc_ref, vn_ref, ck_ref, cv_ref,
                     sink_ref, o_ref, bias_s):
    j = pl.program_id(1)
    nb = DEC_SEQ // QBLOCK
    cc = lax.broadcasted_iota(jnp.int32, (3 * QBLOCK, QBLOCK), 0)
    r = lax.broadcasted_iota(jnp.int32, (3 * QBLOCK, QBLOCK), 1)
    lo = jnp.where(j >= 1, 0, QBLOCK)
    hi = jnp.where(j <= nb - 2, 3 * QBLOCK, 2 * QBLOCK)
    ok = (jnp.abs(QBLOCK + r - cc) <= WINDOW) & (cc >= lo) & (cc < hi)
    bias_s[...] = jnp.where(ok, 0.0, -jnp.inf)
    for kv in range(KVH):
        ks = slice(kv * HD, (kv + 1) * HD)
        k_all = _bf(jnp.concatenate([ck_ref[0, 0, kv], kp_ref[:, ks], kc_ref[:, ks], kn_ref[:, ks]], axis=0))
        v_all = _bf(jnp.concatenate([cv_ref[0, 0, kv], vp_ref[:, ks], vc_ref[:, ks], vn_ref[:, ks]], axis=0))
        v1 = jnp.concatenate([v_all, jnp.ones_like(v_all)], axis=1)
        outs = []
        for h in range(kv * (HC // KVH), (kv + 1) * (HC // KVH)):
            sink = sink_ref[h]
            st = _mm_nt(k_all, _bf(q_ref[:, h * HD:(h + 1) * HD]))
            s_ctx = st[:PAST_LEN]
            s_loc = st[PAST_LEN:] + bias_s[...]
            m = jnp.maximum(jnp.maximum(jnp.max(s_ctx, axis=0, keepdims=True),
                                        jnp.max(s_loc, axis=0, keepdims=True)), sink)
            p = _bf(jnp.concatenate([jnp.exp(s_ctx - m), jnp.exp(s_loc - m)], axis=0))
            ov = _mm_tn(v1, p)
            outs.append(ov[:HD] / (ov[HD:HD + 1] + jnp.exp(sink - m)))
        for pr in range(HC // KVH // 2):
            lane0 = (kv * (HC // KVH) + 2 * pr) * HD
            o_ref[:, lane0:lane0 + 2 * HD] = jnp.concatenate(outs[2 * pr:2 * pr + 2], axis=0).T


def _attn_latent(qkv, cache_k, cache_v, sink, o):
    nb = DEC_SEQ // QBLOCK
    base = NP_ROWS // QBLOCK
    blk = lambda b, j: base + b * nb + j
    prev = lambda b, j: base + b * nb + jnp.maximum(j - 1, 0)
    nxt = lambda b, j: base + b * nb + jnp.minimum(j + 1, nb - 1)
    kcol, vcol = HC * HD // 256, HC * HD // 256 + 1
    cache_spec = pl.BlockSpec((1, 1, KVH, PAST_LEN, HD), lambda b, j: (b, o, 0, 0, 0))
    return pl.pallas_call(
        _attn_lat_kernel,
        out_shape=jax.ShapeDtypeStruct((NS_ROWS, HC * HD), F32),
        grid=(DEC_BATCH, nb),
        in_specs=[pl.BlockSpec((QBLOCK, HC * HD), lambda b, j: (blk(b, j), 0)),
                  pl.BlockSpec((QBLOCK, 256), lambda b, j: (prev(b, j), kcol)),
                  pl.BlockSpec((QBLOCK, 256), lambda b, j: (blk(b, j), kcol)),
                  pl.BlockSpec((QBLOCK, 256), lambda b, j: (nxt(b, j), kcol)),
                  pl.BlockSpec((QBLOCK, 256), lambda b, j: (prev(b, j), vcol)),
                  pl.BlockSpec((QBLOCK, 256), lambda b, j: (blk(b, j), vcol)),
                  pl.BlockSpec((QBLOCK, 256), lambda b, j: (nxt(b, j), vcol)),
                  cache_spec, cache_spec,
                  pl.BlockSpec(memory_space=pltpu.SMEM)],
        out_specs=pl.BlockSpec((QBLOCK, HC * HD), lambda b, j: (b * nb + j, 0)),
        scratch_shapes=[pltpu.VMEM((3 * QBLOCK, QBLOCK), F32)],
        compiler_params=_params(2),
        name="attn_latent",
    )(qkv, qkv, qkv, qkv, qkv, qkv, qkv, cache_k, cache_v, sink)


def _route(lg):
    lane = lax.broadcasted_iota(jnp.int32, lg.shape, 1)
    neg = -jnp.inf
    big = 1 << 20
    is_grp = jnp.logical_and(lane >= N_EXPERTS, lane < N_EXPERTS + N_GROUPS)
    mg = jnp.max(jnp.where(is_grp, lg, neg), axis=1, keepdims=True)
    g_lane = jnp.min(jnp.where(jnp.logical_and(is_grp, lg == mg), lane, big), axis=1, keepdims=True)
    g_w = 1.0 / jnp.sum(jnp.where(is_grp, jnp.exp(lg - mg), 0.0), axis=1, keepdims=True)
    g_idx = g_lane - N_EXPERTS
    in_grp = jnp.logical_and(lane >= g_idx * EPG, lane < (g_idx + 1) * EPG)
    v1 = jnp.max(jnp.where(in_grp, lg, neg), axis=1, keepdims=True)
    i1 = jnp.min(jnp.where(jnp.logical_and(in_grp, lg == v1), lane, big), axis=1, keepdims=True)
    rest = jnp.logical_and(in_grp, lane != i1)
    v2 = jnp.max(jnp.where(rest, lg, neg), axis=1, keepdims=True)
    i2 = jnp.min(jnp.where(jnp.logical_and(rest, lg == v2), lane, big), axis=1, keepdims=True)
    e2 = jnp.exp(v2 - v1)
    p1 = 1.0 / (1.0 + e2)
    p2 = e2 / (1.0 + e2)
    return jnp.where(lane == 0, i1.astype(F32),
                     jnp.where(lane == 1, i2.astype(F32),
                               jnp.where(lane == 2, p1 * g_w, jnp.where(lane == 3, p2 * g_w, 0.0))))


def _moe_dense_kernel(x_ref, sh_ref, sc_ref, wr_ref, br_ref, wg_ref, wu_ref, wd_ref, gate_ref, g_ref, b_ref,
                      *out_and_scratch, tm):
    *o_refs, xm_s, meta_s, acc_s = out_and_scratch
    grp = pl.program_id(1)

    @pl.when(grp == 0)
    def _():
        xm = x_ref[...] * (1.0 + sc_ref[0]) + sh_ref[0]
        meta_s[...] = _route(_mm(_lhs3(xm), _rhs3(wr_ref[...])) + br_ref[...])
        xm_s[...] = _bf(xm)
        acc_s[...] = jnp.zeros_like(acc_s)

    xm = xm_s[...]
    meta = meta_s[...]
    i1 = meta[:, 0:1].astype(jnp.int32)
    i2 = meta[:, 1:2].astype(jnp.int32)
    w1 = meta[:, 2:3]
    w2 = meta[:, 3:4]
    hid = []
    for e in range(EPG):
        eid = grp * EPG + e
        gate = jnp.where(i1 == eid, w1, 0.0) + jnp.where(i2 == eid, w2, 0.0)
        a = _mm(xm, wg_ref[0, e])
        u = _mm(xm, wu_ref[0, e])
        hid.append(_bf((a * _sigmoid(a)) * u * gate))
    acc_s[...] += _mm(jnp.concatenate(hid, axis=1), wd_ref[0])

    def result():
        return _residual_ln(x_ref[...], gate_ref[0], acc_s[...], g_ref[...], b_ref[...])

    last = grp == N_GROUPS - 1
    if len(o_refs) == 1:
        @pl.when(last)
        def _():
            o_refs[0][...] = result()
    else:
        is_p = pl.program_id(0) < NP_ROWS // tm

        @pl.when(jnp.logical_and(last, is_p))
        def _():
            o_refs[0][...] = result()

        @pl.when(jnp.logical_and(last, jnp.logical_not(is_p)))
        def _():
            o_refs[1][...] = result()


def _moe_dense(x, mods, layer, w_r, b_r, wg, wu, wd, ln_g, ln_b, split_out=False):
    tm = 1024
    npt = NP_ROWS // tm
    row = lambda i, g: (i, 0)
    const2 = lambda i, g: (0, 0)
    if split_out:
        half = jax.ShapeDtypeStruct((NP_ROWS, D), F32)
        out_shape = (half, half)
        out_specs = (pl.BlockSpec((tm, D), lambda i, g: (jnp.minimum(i, npt - 1), 0)),
                     pl.BlockSpec((tm, D), lambda i, g: (jnp.maximum(i - npt, 0), 0)))
    else:
        out_shape = jax.ShapeDtypeStruct((T, D), F32)
        out_specs = pl.BlockSpec((tm, D), row)
    return pl.pallas_call(
        functools.partial(_moe_dense_kernel, tm=tm),
        out_shape=out_shape,
        grid=(T // tm, N_GROUPS),
        in_specs=[pl.BlockSpec((tm, D), row), _mod_spec(layer, 3, tm), _mod_spec(layer, 4, tm),
                  pl.BlockSpec((D, 128), const2), pl.BlockSpec((1, 128), const2),
                  pl.BlockSpec((1, EPG, D, EXPERT_FF), lambda i, g: (g, 0, 0, 0)),
                  pl.BlockSpec((1, EPG, D, EXPERT_FF), lambda i, g: (g, 0, 0, 0)),
                  pl.BlockSpec((1, EPG * EXPERT_FF, D), lambda i, g: (g, 0, 0)),
                  _mod_spec(layer, 5, tm), pl.BlockSpec((1, D), const2), pl.BlockSpec((1, D), const2)],
        out_specs=out_specs,
        scratch_shapes=[pltpu.VMEM((tm, D), BF16), pltpu.VMEM((tm, 128), F32), pltpu.VMEM((tm, D), F32)],
        compiler_params=_params(2, VMEM_LIMIT + (2 * tm * D * 4 if split_out else 0)),
        name="moe_dense",
    )(x, mods, mods, w_r, b_r, wg, wu, wd, mods, ln_g, ln_b)


def _permute_even_w(w):
    a_end = 4 * HA * DH
    g_end = a_end + 4 * HA
    c_end = g_end + 3 * HB * DH
    z_end = c_end + HB * DH
    small = jnp.concatenate([w[:, a_end:g_end], w[:, z_end:]], axis=1)
    pad = jnp.zeros((w.shape[0], 128 - small.shape[1]), w.dtype)
    return jnp.concatenate([w[:, g_end:c_end], w[:, :a_end], w[:, c_end:z_end], small, pad], axis=1)


def _lane_row(vals, offset):
    return jnp.zeros((1, 128), F32).at[0, offset:offset + vals.shape[0]].set(vals.astype(F32))


def _pair_rows(vals, offset):
    v32 = jnp.zeros((32,), F32).at[offset:offset + vals.shape[0]].set(vals.astype(F32))
    return jnp.repeat(v32.reshape(16, 2), CHUNK, axis=1)


def kernel(x_prompt, x_sample, c, c_ctx, state_mlstm_c, state_mlstm_n, state_mlstm_m, state_delta, cache_k, cache_v, w_mod, b_mod, ln_g, ln_b, w_in_even, mlstm_gate_b, mlstm_norm_g, delta_conv_w, delta_a_log, delta_dt_bias, delta_norm_g, w_out_even, w_qkv_odd, attn_sink, w_out_odd, w_grp, b_grp, w_erouter, b_erouter, w_gate, w_up, w_down):
    x = (x_prompt.reshape(NP_ROWS, D), x_sample.reshape(NS_ROWS, D))
    cvecs = jnp.concatenate([c_ctx[None, :], c, jnp.zeros((N_MOD_ROWS - 1 - DEC_BATCH, D), F32)], axis=0)
    mods = _modulation(cvecs, w_mod, b_mod)
    tables = _rope_tables()
    m0_all = jnp.broadcast_to(state_mlstm_m.reshape(DEC_BATCH, N_EVEN, 2 * HA, 1), (DEC_BATCH, N_EVEN, 2 * HA, 128))

    out_mc, out_mn, out_mm, out_ds = [], [], [], []
    new_k = jnp.zeros((BATCH, N_ODD, KVH, SEQ, HD), F32)
    new_v = jnp.zeros((BATCH, N_ODD, KVH, SEQ, HD), F32)
    for l in range(DEPTH):
        if l % 2 == 0:
            e = l // 2
            proj = _even_proj(x, mods, l, _bf(_permute_even_w(w_in_even[e])))
            gates_t2 = (proj[:, EVEN_W - 128:EVEN_W - 96].reshape(T // CHUNK, CHUNK, 32).transpose(0, 2, 1)
                        .reshape(T // CHUNK, 16, 128))
            gb = mlstm_gate_b[e].reshape(-1)
            *h_scan, mc, mn, mm = _mlstm_scan(proj, gates_t2, _lane_row(gb, 0), _pair_rows(gb, 0),
                                              state_mlstm_c, state_mlstm_n, m0_all, e)
            qkv = _delta_prep(proj, delta_conv_w[e])
            dtb = delta_dt_bias[e].reshape(-1)
            nea = -jnp.exp(delta_a_log[e].astype(F32)).reshape(-1)
            *o_scan, ds = _delta_scan(qkv, proj, gates_t2, _lane_row(dtb, 24), _pair_rows(dtb, 24),
                                      _lane_row(nea, 24), _pair_rows(nea, 24), state_delta, e)
            x = _even_out(h_scan, o_scan, proj, mlstm_norm_g[e][None, :], delta_norm_g[e][None, :],
                          _bf(w_out_even[e]), x, mods, l, ln_g[l, 0][None, :], ln_b[l, 0][None, :])
            out_mc.append(mc)
            out_mn.append(mn)
            out_mm.append(mm[:, :, 0].reshape(BATCH, 2, HA))
            out_ds.append(ds)
        else:
            o = l // 2
            qkv, new_k, new_v = _odd_proj(x, mods, l, _bf(w_qkv_odd[o]), tables, new_k, new_v, o)
            a_p = _attn_context(qkv, attn_sink[o])
            a_s = _attn_latent(qkv, cache_k, cache_v, attn_sink[o], o)
            x = _odd_out(a_p, a_s, _bf(w_out_odd[o]), x, mods, l, ln_g[l, 0][None, :], ln_b[l, 0][None, :])
        w_r = jnp.concatenate([w_erouter[l].transpose(1, 0, 2).reshape(D, N_EXPERTS), w_grp[l],
                               jnp.zeros((D, 128 - N_EXPERTS - N_GROUPS), F32)], axis=1)
        b_r = jnp.concatenate([b_erouter[l].reshape(-1), b_grp[l],
                               jnp.zeros((128 - N_EXPERTS - N_GROUPS,), F32)])[None, :]
        x = _moe_dense(x, mods, l, w_r, b_r,
                       _bf(w_gate[l]).reshape(N_GROUPS, EPG, D, EXPERT_FF),
                       _bf(w_up[l]).reshape(N_GROUPS, EPG, D, EXPERT_FF),
                       _bf(w_down[l]).reshape(N_GROUPS, EPG * EXPERT_FF, D),
                       ln_g[l, 1][None, :], ln_b[l, 1][None, :], split_out=(l == DEPTH - 1))
    return (x[0].reshape(BATCH, SEQ, D), x[1].reshape(DEC_BATCH, DEC_SEQ, D),
            jnp.stack(out_mc, 1), jnp.stack(out_mn, 1), jnp.stack(out_mm, 1), jnp.stack(out_ds, 1),
            new_k, new_v)
```

```python
import functools

import numpy as np
import jax
import jax.numpy as jnp
from jax import lax
from jax.experimental import pallas as pl
from jax.experimental.pallas import tpu as pltpu

F32 = jnp.float32
BF16 = jnp.bfloat16
HI = lax.Precision.HIGHEST

D = 1024
BATCH = 32
SEQ = 256
DEPTH = 4
DEC_BATCH = 2
DEC_SEQ = 4096
PAST_LEN = 512
GRID_W = 64
N_EVEN = 2
N_ODD = 2
HA = 4
HB = 4
DH = 128
CHUNK = 64
HC = 16
KVH = 4
HD = 64
WINDOW = 128
QBLOCK = 128
ROPE_THETA = 10000.0
N_GROUPS = 4
EPG = 4
N_EXPERTS = 16
EXPERT_FF = 256
DN_ALPHA = (2 * DEPTH) ** 0.25
LN_EPS = 1e-5

NP_ROWS = BATCH * SEQ
NS_ROWS = DEC_BATCH * DEC_SEQ
T = NP_ROWS + NS_ROWS
N_MOD_ROWS = 8
EVEN_W = 4224
N_GATE_COLS = 4 * HA + 4 * HB
BETA_COL0 = 4 * HA
DECAY_COL0 = 4 * HA + 2 * HB
QKV_W = (HC + 2 * KVH) * HD

P_CHUNKS = SEQ // CHUNK
S_CHUNKS = DEC_SEQ // CHUNK
P_STEPS = BATCH * P_CHUNKS
S_STEPS = DEC_BATCH * S_CHUNKS
P_BLOCKS = NP_ROWS // CHUNK

VMEM_LIMIT = 48 * 1024 * 1024


def _params(n_axes, vmem_limit=VMEM_LIMIT):
    return pltpu.CompilerParams(dimension_semantics=("arbitrary",) * n_axes,
                                vmem_limit_bytes=vmem_limit)


def _mm(a, b, prec=None):
    return lax.dot_general(a, b, (((1,), (0,)), ((), ())), precision=prec, preferred_element_type=F32)


def _mm_nt(a, b, prec=None):
    return lax.dot_general(a, b, (((1,), (1,)), ((), ())), precision=prec, preferred_element_type=F32)


def _mm_tn(a, b, prec=None):
    return lax.dot_general(a, b, (((0,), (0,)), ((), ())), precision=prec, preferred_element_type=F32)


def _bf(x):
    return x.astype(BF16)


def _sigmoid(x):
    return 1.0 / (1.0 + jnp.exp(-x))


def _softplus(x):
    return jnp.maximum(x, 0.0) + jnp.log1p(jnp.exp(-jnp.abs(x)))


def _log_sigmoid(x):
    return jnp.minimum(x, 0.0) - jnp.log1p(jnp.exp(-jnp.abs(x)))


def _mod_row(tile, tm):
    npt = NP_ROWS // tm
    per = DEC_SEQ // tm
    return jnp.where(tile < npt, 0, 1 + (tile - npt) // per)


def _mod_spec(layer, chunk, tm):
    def imap(i, *_):
        return ((layer * N_MOD_ROWS + _mod_row(i, tm)) * 6 + chunk, 0, 0)
    return pl.BlockSpec((1, 1, D), imap)


def _modulation_kernel(c_ref, w_ref, b_ref, o_ref):
    x = c_ref[...]
    s = x * _sigmoid(x)
    o_ref[0] = _mm(s, w_ref[0], HI) + b_ref[0]


def _modulation(cvecs, w_mod, b_mod):
    out = pl.pallas_call(
        _modulation_kernel,
        out_shape=jax.ShapeDtypeStruct((DEPTH, N_MOD_ROWS, 6 * D), F32),
        grid=(DEPTH, 6),
        in_specs=[pl.BlockSpec((N_MOD_ROWS, D), lambda l, j: (0, 0)),
                  pl.BlockSpec((1, D, D), lambda l, j: (l, 0, j)),
                  pl.BlockSpec((1, 1, D), lambda l, j: (l * 6 + j, 0, 0))],
        out_specs=pl.BlockSpec((1, N_MOD_ROWS, D), lambda l, j: (l, 0, j)),
        compiler_params=_params(2),
        name="modulation",
    )(cvecs, w_mod, b_mod.reshape(DEPTH * 6, 1, D))
    return out.reshape(DEPTH * N_MOD_ROWS * 6, 1, D)


def _x_operands(x, tm):
    if isinstance(x, tuple):
        npt = NP_ROWS // tm
        return ([pl.BlockSpec((tm, D), lambda i, *_: (jnp.minimum(i, npt - 1), 0)),
                 pl.BlockSpec((tm, D), lambda i, *_: (jnp.maximum(i - npt, 0), 0))], list(x))
    return [pl.BlockSpec((tm, D), lambda i, *_: (i, 0))], [x]


def _load_x(x_refs, tm):
    if len(x_refs) == 1:
        return x_refs[0][...]
    return jnp.where(pl.program_id(0) < NP_ROWS // tm, x_refs[0][...], x_refs[1][...])


def _proj_kernel(*refs, tm, n_x):
    x_refs, (sh_ref, sc_ref, w_ref, o_ref) = refs[:n_x], refs[n_x:]
    xm = _load_x(x_refs, tm) * (1.0 + sc_ref[0]) + sh_ref[0]
    o_ref[...] = _mm(_bf(xm), w_ref[...])


def _even_proj(x, mods, layer, w):
    tm = 256
    x_specs, x_arrays = _x_operands(x, tm)
    return pl.pallas_call(
        functools.partial(_proj_kernel, tm=tm, n_x=len(x_arrays)),
        out_shape=jax.ShapeDtypeStruct((T, EVEN_W), F32),
        grid=(T // tm,),
        in_specs=x_specs + [_mod_spec(layer, 0, tm), _mod_spec(layer, 1, tm),
                            pl.BlockSpec((D, EVEN_W), lambda i: (0, 0))],
        out_specs=pl.BlockSpec((tm, EVEN_W), lambda i: (i, 0)),
        compiler_params=_params(1),
        name="even_proj",
    )(*x_arrays, mods, mods, w)


def _qkv_kernel(x_ref, sh_ref, sc_ref, w_ref, cos_ref, sa_ref, sb_ref, kc_in, vc_in, o_ref, kc_ref, vc_ref, *, tm):
    del kc_in, vc_in
    i = pl.program_id(0)
    xm = x_ref[...] * (1.0 + sc_ref[0]) + sh_ref[0]
    acc = _mm(_bf(xm), w_ref[...])

    @pl.when(i < NP_ROWS // tm)
    def _():
        for b in range(tm // SEQ):
            for kv in range(KVH):
                k0 = HC * HD + kv * HD
                v0 = (HC + KVH) * HD + kv * HD
                kc_ref[b, 0, kv] = acc[b * SEQ:(b + 1) * SEQ, k0:k0 + HD]
                vc_ref[b, 0, kv] = acc[b * SEQ:(b + 1) * SEQ, v0:v0 + HD]

    n_q = HC * HD // 128
    n_k = KVH * HD // 128
    is_latent = i >= NP_ROWS // tm
    cos = jnp.where(is_latent, cos_ref[...], 1.0)
    sa = jnp.where(is_latent, sa_ref[...], 0.0)
    sb = jnp.where(is_latent, sb_ref[...], 0.0)
    for g in range(n_q + n_k):
        blk = acc[:, g * 128:(g + 1) * 128]
        if g < n_q:
            blk = blk * (HD ** -0.5)
        rot = blk * cos + pltpu.roll(blk, 112, 1) * sa + pltpu.roll(blk, 16, 1) * sb
        o_ref[:, g * 128:(g + 1) * 128] = rot
    o_ref[:, (n_q + n_k) * 128:] = acc[:, (n_q + n_k) * 128:]


def _rope_tables():
    half = HD // 4
    inv = np.float32(ROPE_THETA) ** (-np.arange(half, dtype=np.float32) / np.float32(half))
    pos = np.arange(DEC_SEQ)
    row = (pos // GRID_W).astype(np.float32)[:, None] * inv[None, :]
    col = (pos % GRID_W).astype(np.float32)[:, None] * inv[None, :]
    cos = np.concatenate([np.cos(row), np.cos(row), np.cos(col), np.cos(col)], axis=-1)
    sin = np.concatenate([np.sin(row), np.sin(row), np.sin(col), np.sin(col)], axis=-1)
    first = (np.arange(HD) % 32) < 16
    sa = np.where(first, -sin, 0.0)
    sb = np.where(first, 0.0, sin)
    tile2 = lambda t: jnp.asarray(np.concatenate([t, t], axis=-1), F32)
    return tile2(cos), tile2(sa), tile2(sb)


def _odd_proj(x, mods, layer, w, tables, cache_k, cache_v, o):
    tm = 512
    npt = NP_ROWS // tm
    per = DEC_SEQ // tm
    tab_spec = pl.BlockSpec((tm, 128), lambda i: (jnp.where(i < npt, 0, (i - npt) % per), 0))
    cache_spec = pl.BlockSpec((tm // SEQ, 1, KVH, SEQ, HD), lambda i: (jnp.minimum(i, npt - 1), o, 0, 0, 0))
    cache_shape = jax.ShapeDtypeStruct((BATCH, N_ODD, KVH, SEQ, HD), F32)
    return pl.pallas_call(
        functools.partial(_qkv_kernel, tm=tm),
        out_shape=(jax.ShapeDtypeStruct((T, QKV_W), F32), cache_shape, cache_shape),
        grid=(T // tm,),
        in_specs=[pl.BlockSpec((tm, D), lambda i: (i, 0)),
                  _mod_spec(layer, 0, tm), _mod_spec(layer, 1, tm),
                  pl.BlockSpec((D, QKV_W), lambda i: (0, 0)),
                  tab_spec, tab_spec, tab_spec,
                  pl.BlockSpec(memory_space=pl.ANY), pl.BlockSpec(memory_space=pl.ANY)],
        out_specs=(pl.BlockSpec((tm, QKV_W), lambda i: (i, 0)), cache_spec, cache_spec),
        input_output_aliases={7: 1, 8: 2},
        compiler_params=_params(1),
        name="odd_qkv_proj",
    )(x, mods, mods, w, *tables, cache_k, cache_v)


PREP_ROWS = 256


def _delta_prep_kernel(x_ref, prev_ref, next_ref, w_ref, o_ref):
    i = pl.program_id(0)
    npb = NP_ROWS // PREP_ROWS
    per = DEC_SEQ // PREP_ROWS
    is_latent = i >= npb
    pos = (i - npb) % per
    has_prev = jnp.logical_and(is_latent, pos > 0)
    has_next = jnp.logical_and(is_latent, pos < per - 1)
    x = x_ref[...]
    w = w_ref[...]
    rows = lax.broadcasted_iota(jnp.int32, x.shape, 0)
    prev_row = jnp.where(has_prev, prev_ref[7:8, :], 0.0)
    next_row = jnp.where(has_next, next_ref[0:1, :], 0.0)
    xm1 = jnp.where(rows == 0, prev_row, pltpu.roll(x, 1, 0))
    xp1 = jnp.where(rows == PREP_ROWS - 1, next_row, pltpu.roll(x, PREP_ROWS - 1, 0))
    y = xm1 * w[0:1, :] + x * w[1:2, :] + xp1 * w[2:3, :]
    y = y * _sigmoid(y)
    for h in range(3 * HB):
        yh = y[:, h * DH:(h + 1) * DH]
        if h < 2 * HB:
            inv = lax.rsqrt(jnp.sum(yh * yh, axis=-1, keepdims=True) + 1e-6)
            yh = yh * (inv * (DH ** -0.5) if h < HB else inv)
        o_ref[:, h * DH:(h + 1) * DH] = yh


def _delta_prep(proj, conv_w):
    nblk = T // PREP_ROWS
    sub = PREP_ROWS // 8
    last8 = T // 8 - 1
    return pl.pallas_call(
        _delta_prep_kernel,
        out_shape=jax.ShapeDtypeStruct((T, 3 * 512), F32),
        grid=(nblk,),
        in_specs=[pl.BlockSpec((PREP_ROWS, 3 * 512), lambda i: (i, 0)),
                  pl.BlockSpec((8, 3 * 512), lambda i: (jnp.maximum(i * sub - 1, 0), 0)),
                  pl.BlockSpec((8, 3 * 512), lambda i: (jnp.minimum((i + 1) * sub, last8), 0)),
                  pl.BlockSpec((3, 3 * 512), lambda i: (0, 0))],
        out_specs=pl.BlockSpec((PREP_ROWS, 3 * 512), lambda i: (i, 0)),
        compiler_params=_params(1),
        name="delta_prep",
    )(proj, proj, proj, conv_w)


assert P_STEPS == S_STEPS
SCAN_STEPS = P_STEPS


def _bwd_local(s, nc):
    return (s // nc) * nc + nc - 1 - s % nc


def _scan_blocks():
    return (lambda s: s, lambda s: _bwd_local(s, P_CHUNKS),
            lambda s: P_BLOCKS + s, lambda s: P_BLOCKS + _bwd_local(s, S_CHUNKS))


def _scan_specs(xcol):
    gcol = EVEN_W // 128 - 1
    blocks = _scan_blocks()
    return ([pl.BlockSpec((CHUNK, 3 * 512), lambda s, f=f: (f(s), xcol)) for f in blocks]
            + [pl.BlockSpec((CHUNK, 128), lambda s, f=f: (f(s), gcol)) for f in blocks]
            + [pl.BlockSpec((1, 16, 128), lambda s, f=f: (f(s), 0, 0)) for f in blocks])


def _scan_out_specs():
    local = (lambda s: s, lambda s: _bwd_local(s, P_CHUNKS), lambda s: s, lambda s: _bwd_local(s, S_CHUNKS))
    return [pl.BlockSpec((CHUNK, 512), lambda s, f=f: (f(s), 0)) for f in local]


def _mlstm_kernel(xpf_ref, xpb_ref, xsf_ref, xsb_ref, gpf_ref, gpb_ref, gsf_ref, gsb_ref,
                  rpf_ref, rpb_ref, rsf_ref, rsb_ref,
                  brow_ref, b2_ref, c0_ref, n0_ref, m0_ref,
                  hpf_ref, hpb_ref, hsf_ref, hsb_ref, c_out, n_out, m_out,
                  cp_s, np_s, mp_s, cs_s, ns_s, ms_s):
    s = pl.program_id(0)
    jp = s % P_CHUNKS
    js = s % S_CHUNKS

    @pl.when(jp == 0)
    def _():
        cp_s[...] = jnp.zeros_like(cp_s)
        np_s[...] = jnp.zeros_like(np_s)
        mp_s[...] = jnp.zeros_like(mp_s)

    @pl.when(js == 0)
    def _():
        for d in range(2):
            for h in range(HA):
                cs_s[d * HA + h] = c0_ref[0, 0, d, h]
                ns_s[d * HA + h] = jnp.broadcast_to(n0_ref[0, 0, d, h:h + 1, :], (DH, 128)).T
        ms_s[...] = m0_ref[0, 0]

    states = ((cp_s, np_s, mp_s), (cs_s, ns_s, ms_s))

    row = lax.broadcasted_iota(jnp.int32, (CHUNK, 128), 0)
    lane = lax.broadcasted_iota(jnp.int32, (CHUNK, 128), 1)
    left = lane < CHUNK
    lcol = jnp.where(left, lane, lane - CHUNK)
    r64 = lax.broadcasted_iota(jnp.int32, (CHUNK, CHUNK), 0)
    c64 = lax.broadcasted_iota(jnp.int32, (CHUNK, CHUNK), 1)
    r128 = lax.broadcasted_iota(jnp.int32, (128, 128), 0)
    c128 = lax.broadcasted_iota(jnp.int32, (128, 128), 1)
    same_half = (r128 < CHUNK) == (c128 < CHUNK)
    neg = -jnp.inf
    ins = ((xpf_ref, gpf_ref, rpf_ref, hpf_ref), (xpb_ref, gpb_ref, rpb_ref, hpb_ref),
           (xsf_ref, gsf_ref, rsf_ref, hsf_ref), (xsb_ref, gsb_ref, rsb_ref, hsb_ref))

    sel_r = lax.broadcasted_iota(jnp.int32, (128, HA * 128), 0)
    sel_h = lax.broadcasted_iota(jnp.int32, (128, HA * 128), 1) // 128
    ones_lr = lax.broadcasted_iota(jnp.int32, (256, 256), 0)
    ones_lc = lax.broadcasted_iota(jnp.int32, (256, 256), 1)
    half_sum = (((ones_lr % 128) < CHUNK) == (ones_lc < 128)).astype(BF16)
    ones_l = jnp.ones((CHUNK, 128), BF16)
    rows256 = lax.broadcasted_iota(jnp.int32, (CHUNK, 256), 0)

    def running_max(x, d):
        sh = 1
        while sh < CHUNK:
            if d == 0:
                x = jnp.maximum(x, jnp.where(rows256 >= sh, pltpu.roll(x, sh, 0), neg))
            else:
                x = jnp.maximum(x, jnp.where(rows256 < CHUNK - sh, pltpu.roll(x, CHUNK - sh, 0), neg))
            sh *= 2
        return x

    gate = []
    for u in range(4):
        d = u % 2
        gc_ref, gr_ref = ins[u][1], ins[u][2]
        if d == 0:
            incl_p, tri_c, tri_r = lcol <= row, c64 <= r64, jnp.logical_and(same_half, r128 <= c128)
        else:
            incl_p, tri_c, tri_r = lcol >= row, c64 >= r64, jnp.logical_and(same_half, r128 >= c128)
        gc = gc_ref[...] + brow_ref[...]
        gr = gr_ref[0] + b2_ref[...]
        i_rep = _replicate(gc, (sel_r == d * 2 * HA + sel_h).astype(BF16))
        b_rep = _replicate(_prefix_cols(tri_c.astype(BF16), _log_sigmoid(gc)),
                           (sel_r == d * 2 * HA + HA + sel_h).astype(BF16))
        cs_row = _prefix_rows(_log_sigmoid(gr), tri_r.astype(BF16))
        gate.append((incl_p, gr, cs_row, i_rep, b_rep))

    st = []
    for u in range(4):
        d = u % 2
        c_s, n_s, m_s = states[u // 2]
        for hp in range(HA // 2):
            x_ref = ins[u][0]
            incl_p, gr, cs_row, i_rep, b_rep = gate[u]
            end = CHUNK - 1 if d == 0 else 0
            heads = []
            for h in (2 * hp, 2 * hp + 1):
                c = d * HA + h
                heads.append(dict(c=c, h=h, q=x_ref[:, h * DH:(h + 1) * DH],
                                  k=x_ref[:, 512 + h * DH:512 + (h + 1) * DH] * (DH ** -0.5),
                                  v=x_ref[:, 1024 + h * DH:1024 + (h + 1) * DH],
                                  i=i_rep[:, h * 128:(h + 1) * 128], b=b_rep[:, h * 128:(h + 1) * 128],
                                  m=m_s[c:c + 1, :]))
            ha, hb_ = heads
            ri = d * HA + hp
            rf = d * HA + HA // 2 + hp
            i_row = gr[ri:ri + 1, :]
            b_row = cs_row[rf:rf + 1, :]
            run = running_max(jnp.concatenate([ha["i"] - ha["b"], hb_["i"] - hb_["b"]], axis=1), d)
            for idx, hd in enumerate(heads):
                top = jnp.maximum(hd["m"], run[:, idx * 128:(idx + 1) * 128])
                hd["m_t"] = hd["b"] + top
                hd["w_inter"] = jnp.exp(hd["m"] - top)
            b_p = jnp.where(left, ha["b"], hb_["b"])
            m_t_p = jnp.where(left, ha["m_t"], hb_["m_t"])
            dmat = jnp.where(incl_p, b_p - b_row + i_row, neg)
            q_cat = _bf(jnp.concatenate([ha["q"], hb_["q"]], axis=1))
            k_bd = _bf(_block_diag2(ha["k"], hb_["k"]))
            st.append(dict(u=u, d=d, heads=heads, sc=_mm_nt(q_cat, k_bd) * jnp.exp(dmat - m_t_p)))

    for p in st:
        ha, hb_ = p["heads"]
        h_ref = ins[p["u"]][3]
        c_s, n_s, m_s = states[p["u"] // 2]
        end = CHUNK - 1 if p["d"] == 0 else 0
        sc = p["sc"]
        sv = _mm(_bf(sc), _bf(_block_diag2(ha["v"], hb_["v"])))
        dens = _mm(jnp.concatenate(_split2(sc), axis=1), half_sum)
        for idx, hd in enumerate((ha, hb_)):
            c, h, q, k = hd["c"], hd["h"], hd["q"], hd["k"]
            c_mat = c_s[c]
            n_mat = n_s[c]
            q_cn = _mm(_bf(q), _bf(jnp.concatenate([c_mat, n_mat], axis=1)))
            num = sv[:, idx * DH:(idx + 1) * DH] + hd["w_inter"] * q_cn[:, :DH]
            den = dens[:, idx * 128:(idx + 1) * 128] + hd["w_inter"] * q_cn[:, DH:]
            h_ref[:, h * DH:(h + 1) * DH] = num / jnp.maximum(jnp.abs(den), jnp.exp(-hd["m_t"]))
            b_last = hd["b"][end:end + 1, :]
            g_end = b_last - hd["b"] + hd["i"]
            m_new = jnp.maximum(b_last + hd["m"], jnp.max(g_end, axis=0, keepdims=True))
            kwb = _bf(k * jnp.exp(g_end - m_new))
            decay = jnp.exp(b_last + hd["m"] - m_new)
            upd = _mm_tn(kwb, jnp.concatenate([_bf(hd["v"]), ones_l], axis=1))
            c_s[c] = decay * c_mat + upd[:, :DH]
            n_s[c] = decay * n_mat + upd[:, DH:]
            m_s[c:c + 1, :] = m_new

    @pl.when(jp == P_CHUNKS - 1)
    def _():
        for d in range(2):
            for h in range(HA):
                c_out[0, d, h] = cp_s[d * HA + h]
                n_out[0, d, h:h + 1, :] = np_s[d * HA + h].T[0:1, :]
        m_out[0] = mp_s[...]


def _mlstm_scan(proj, gates_t2, bias_row, bias2, c0, n0, m0, e):
    const2 = lambda s: (0, 0)
    in_specs = (_scan_specs(1)
                + [pl.BlockSpec((1, 128), const2), pl.BlockSpec((16, 128), const2),
                   pl.BlockSpec((1, 1, 2, HA, DH, DH), lambda s: (s // S_CHUNKS, e, 0, 0, 0, 0)),
                   pl.BlockSpec((1, 1, 2, HA, DH), lambda s: (s // S_CHUNKS, e, 0, 0, 0)),
                   pl.BlockSpec((1, 1, 2 * HA, 128), lambda s: (s // S_CHUNKS, e, 0, 0))])
    half = jax.ShapeDtypeStruct((NP_ROWS, HA * DH), F32)
    out_shape = (half, half, half, half,
                 jax.ShapeDtypeStruct((BATCH, 2, HA, DH, DH), F32),
                 jax.ShapeDtypeStruct((BATCH, 2, HA, DH), F32),
                 jax.ShapeDtypeStruct((BATCH, 2 * HA, 128), F32))
    out_specs = _scan_out_specs() + [
        pl.BlockSpec((1, 2, HA, DH, DH), lambda s: (s // P_CHUNKS, 0, 0, 0, 0)),
        pl.BlockSpec((1, 2, HA, DH), lambda s: (s // P_CHUNKS, 0, 0, 0)),
        pl.BlockSpec((1, 2 * HA, 128), lambda s: (s // P_CHUNKS, 0, 0))]
    state = [pltpu.VMEM((2 * HA, DH, DH), F32), pltpu.VMEM((2 * HA, DH, 128), F32), pltpu.VMEM((2 * HA, 128), F32)]
    return pl.pallas_call(
        _mlstm_kernel,
        out_shape=out_shape,
        grid=(SCAN_STEPS,),
        in_specs=in_specs,
        out_specs=out_specs,
        scratch_shapes=state + state,
        compiler_params=_params(1),
        name="mlstm_scan",
    )(proj, proj, proj, proj, proj, proj, proj, proj, gates_t2, gates_t2, gates_t2, gates_t2,
      bias_row, bias2, c0, n0, m0)


def _split2(x):
    hi = _bf(x)
    return hi, _bf(x - hi.astype(F32))


def _split3(x):
    h1 = _bf(x)
    r1 = x - h1.astype(F32)
    h2 = _bf(r1)
    return h1, h2, _bf(r1 - h2.astype(F32))


def _lhs3(x):
    hi, lo = _split2(x)
    return jnp.concatenate([hi, lo, hi], axis=1)


def _rhs3(x):
    hi, lo = _split2(x)
    return jnp.concatenate([hi, hi, lo], axis=0)


def _prefix_cols(tri_bf, x):
    n = x.shape[1]
    r = _mm(tri_bf, jnp.concatenate(_split3(x), axis=1))
    return r[:, :n] + r[:, n:2 * n] + r[:, 2 * n:]


def _replicate(x, sel):
    return _mm(jnp.concatenate(_split3(x), axis=1), jnp.concatenate([sel, sel, sel], axis=0))


def _prefix_rows(x, tri_bf):
    m = x.shape[0]
    r = _mm(jnp.concatenate(_split3(x), axis=0), tri_bf)
    return r[:m] + r[m:2 * m] + r[2 * m:]


LEVELS = tuple(range(6))


def _block_diag2(a, b):
    z = jnp.zeros_like(a)
    return jnp.concatenate([jnp.concatenate([a, z], axis=1), jnp.concatenate([z, b], axis=1)], axis=0)


def _delta_kernel(xpf_ref, xpb_ref, xsf_ref, xsb_ref, gpf_ref, gpb_ref, gsf_ref, gsb_ref,
                  rpf_ref, rpb_ref, rsf_ref, rsb_ref,
                  dtrow_ref, dt2_ref, narow_ref, na2_ref, s0_ref,
                  opf_ref, opb_ref, osf_ref, osb_ref, s_out, sp_s, ss_s):
    s = pl.program_id(0)
    jp = s % P_CHUNKS
    js = s % S_CHUNKS

    @pl.when(jp == 0)
    def _():
        sp_s[...] = jnp.zeros_like(sp_s)

    @pl.when(js == 0)
    def _():
        for d in range(2):
            for h in range(HB):
                ss_s[d * HB + h] = s0_ref[0, 0, d, h]

    states = (sp_s, ss_s)

    row = lax.broadcasted_iota(jnp.int32, (CHUNK, 128), 0)
    lane = lax.broadcasted_iota(jnp.int32, (CHUNK, 128), 1)
    left = lane < CHUNK
    lcol = jnp.where(left, lane, lane - CHUNK)
    eye_p = (lcol == row).astype(F32)
    r64 = lax.broadcasted_iota(jnp.int32, (CHUNK, CHUNK), 0)
    c64 = lax.broadcasted_iota(jnp.int32, (CHUNK, CHUNK), 1)
    r128 = lax.broadcasted_iota(jnp.int32, (128, 128), 0)
    c128 = lax.broadcasted_iota(jnp.int32, (128, 128), 1)
    same_half = (r128 < CHUNK) == (c128 < CHUNK)
    ins = ((xpf_ref, gpf_ref, rpf_ref, opf_ref), (xpb_ref, gpb_ref, rpb_ref, opb_ref),
           (xsf_ref, gsf_ref, rsf_ref, osf_ref), (xsb_ref, gsb_ref, rsb_ref, osb_ref))

    gate = []
    for u in range(4):
        d = u % 2
        gc_ref, gr_ref = ins[u][1], ins[u][2]
        if d == 0:
            incl_p, strict_p = lcol <= row, lcol < row
            tri_c, tri_r = c64 <= r64, jnp.logical_and(same_half, r128 <= c128)
        else:
            incl_p, strict_p = lcol >= row, lcol > row
            tri_c, tri_r = c64 >= r64, jnp.logical_and(same_half, r128 >= c128)
        xc = gc_ref[...]
        xr = gr_ref[0]
        la_c = narow_ref[...] * _softplus(xc + dtrow_ref[...])
        la_r = na2_ref[...] * _softplus(xr + dt2_ref[...])
        g_c = _prefix_cols(tri_c.astype(BF16), la_c)
        g_r = _prefix_rows(la_r, tri_r.astype(BF16))
        gate.append((incl_p, strict_p, _sigmoid(xc), g_c, g_r))

    def bd_rhs(hi, lo, mask=None):
        top = left if mask is None else jnp.logical_and(left, mask)
        bot = jnp.logical_not(left) if mask is None else jnp.logical_and(jnp.logical_not(left), mask)
        zero = jnp.zeros_like(hi)
        blocks = [jnp.concatenate([jnp.where(top, x, zero), jnp.where(bot, x, zero)], axis=0) for x in (hi, lo)]
        return jnp.concatenate([blocks[0], blocks[0], blocks[1]], axis=0)

    off = ([], [])
    for lv in LEVELS:
        same = jnp.right_shift(row, lv + 1) == jnp.right_shift(lcol, lv + 1)
        r_hi = jnp.bitwise_and(jnp.right_shift(row, lv), 1) == 1
        c_hi = jnp.bitwise_and(jnp.right_shift(lcol, lv), 1) == 1
        off[0].append(same & r_hi & jnp.logical_not(c_hi))
        off[1].append(same & jnp.logical_not(r_hi) & c_hi)

    pairs = [(u, hp) for u in range(4) for hp in range(HB // 2)]
    st = []
    for u, hp in pairs:
        d = u % 2
        x_ref = ins[u][0]
        incl_p, strict_p, beta_c, g_c, g_r = gate[u]
        end = CHUNK - 1 if d == 0 else 0
        heads = []
        for h in (2 * hp, 2 * hp + 1):
            ib = BETA_COL0 + d * HB + h
            ia = DECAY_COL0 + d * HB + h
            gcol = g_c[:, ia:ia + 1]
            bc = beta_c[:, ib:ib + 1]
            q = x_ref[:, h * DH:(h + 1) * DH]
            k = x_ref[:, 512 + h * DH:512 + (h + 1) * DH]
            v = x_ref[:, 1024 + h * DH:1024 + (h + 1) * DH]
            heads.append(dict(h=h, gcol=gcol, bc=bc, q=q, k=k, v=v, kb=k * bc, eg=jnp.exp(gcol),
                              g_last=gcol[end:end + 1, :]))
        ha, hb_ = heads
        r = (DECAY_COL0 + d * HB) // 2 + hp
        gcol_p = jnp.where(left, ha["gcol"], hb_["gcol"])
        decay = jnp.exp(jnp.where(incl_p, gcol_p - g_r[r:r + 1, :], -jnp.inf))
        k_bd = _bf(_block_diag2(ha["k"], hb_["k"]))
        kb_cat = _bf(jnp.concatenate([ha["kb"], hb_["kb"]], axis=1))
        q_cat = _bf(jnp.concatenate([ha["q"], hb_["q"]], axis=1))
        kq = _mm_nt(jnp.concatenate([kb_cat, q_cat], axis=0), k_bd)
        a_mat = jnp.where(strict_p, kq[:CHUNK] * decay, 0.0)
        qk = kq[CHUNK:] * decay
        a_hi, a_lo = _split2(a_mat)
        st.append(dict(u=u, d=d, heads=heads, t=eye_p - jnp.where(off[d][0], a_mat, 0.0), qk=qk,
                       am=[bd_rhs(a_hi, a_lo, m) for m in off[d][1:]]))

    for li in range(len(LEVELS) - 1):
        for p in st:
            p["t_parts"] = _split2(p["t"])
            t_hi, t_lo = p["t_parts"]
            p["w"] = _mm(jnp.concatenate([t_hi, t_lo, t_hi], axis=1), p["am"][li])
        for p in st:
            p["t"] = p["t"] - _mm(_lhs3(p["w"]), bd_rhs(*p["t_parts"]))

    for p in st:
        ha, hb_ = p["heads"]
        o_ref = ins[p["u"]][3]
        s_s = states[p["u"] // 2]
        rhs_a = jnp.concatenate([ha["v"] * ha["bc"], ha["kb"] * ha["eg"]], axis=1)
        rhs_b = jnp.concatenate([hb_["v"] * hb_["bc"], hb_["kb"] * hb_["eg"]], axis=1)
        (a_hi, a_lo), (b_hi, b_lo) = _split2(rhs_a), _split2(rhs_b)
        bd_hi, bd_lo = _block_diag2(a_hi, b_hi), _block_diag2(a_lo, b_lo)
        sol = _mm(_lhs3(p["t"]), jnp.concatenate([bd_hi, bd_hi, bd_lo], axis=0))
        vn = []
        for idx, hd in enumerate((ha, hb_)):
            c = p["d"] * HB + hd["h"]
            s_mat = s_s[c]
            sbf = _bf(s_mat)
            so = sol[:, idx * 2 * DH:(idx + 1) * 2 * DH]
            both = _mm(_bf(jnp.concatenate([so[:, DH:], hd["q"] * hd["eg"]], axis=0)), sbf)
            v_new = so[:, :DH] - both[:CHUNK]
            vn.append(v_new)
            hd["o1"] = both[CHUNK:]
            s_s[c] = (jnp.exp(hd["g_last"]) * s_mat
                      + _mm_tn(_bf(hd["k"] * jnp.exp(hd["g_last"] - hd["gcol"])), _bf(v_new)))
        o2 = _mm(_bf(p["qk"]), _bf(_block_diag2(vn[0], vn[1])))
        for idx, hd in enumerate((ha, hb_)):
            o_ref[:, hd["h"] * DH:(hd["h"] + 1) * DH] = hd["o1"] + o2[:, idx * DH:(idx + 1) * DH]

    @pl.when(jp == P_CHUNKS - 1)
    def _():
        for d in range(2):
            for h in range(HB):
                s_out[0, d, h] = sp_s[d * HB + h]


def _delta_scan(qkv, proj, gates_t2, dt_row, dt2, na_row, na2, s0, e):
    const2 = lambda s: (0, 0)
    in_specs = (_scan_specs(0)
                + [pl.BlockSpec((1, 128), const2), pl.BlockSpec((16, 128), const2),
                   pl.BlockSpec((1, 128), const2), pl.BlockSpec((16, 128), const2),
                   pl.BlockSpec((1, 1, 2, HB, DH, DH), lambda s: (s // S_CHUNKS, e, 0, 0, 0, 0))])
    half = jax.ShapeDtypeStruct((NP_ROWS, HB * DH), F32)
    out_shape = (half, half, half, half, jax.ShapeDtypeStruct((BATCH, 2, HB, DH, DH), F32))
    out_specs = _scan_out_specs() + [pl.BlockSpec((1, 2, HB, DH, DH), lambda s: (s // P_CHUNKS, 0, 0, 0, 0))]
    return pl.pallas_call(
        _delta_kernel,
        out_shape=out_shape,
        grid=(SCAN_STEPS,),
        in_specs=in_specs,
        out_specs=out_specs,
        scratch_shapes=[pltpu.VMEM((2 * HB, DH, DH), F32), pltpu.VMEM((2 * HB, DH, DH), F32)],
        compiler_params=_params(1),
        name="delta_scan",
    )(qkv, qkv, qkv, qkv, proj, proj, proj, proj, gates_t2, gates_t2, gates_t2, gates_t2,
      dt_row, dt2, na_row, na2, s0)


def _residual_ln(x, gate, y, g, b):
    r = DN_ALPHA * x + gate * y
    mu = jnp.mean(r, axis=-1, keepdims=True)
    var = jnp.mean(jnp.square(r - mu), axis=-1, keepdims=True)
    return (r - mu) * lax.rsqrt(var + LN_EPS) * g + b


def _even_out_kernel(*refs, tm, n_x):
    x_refs = refs[:n_x]
    (hpf_ref, hpb_ref, hsf_ref, hsb_ref, opf_ref, opb_ref, osf_ref, osb_ref,
     oa_ref, zb_ref, mg_ref, dg_ref, w_ref, gate_ref, g_ref, b_ref, o_ref) = refs[n_x:]
    is_p = pl.program_id(0) < NP_ROWS // tm
    hf_ref, hb_ref, of_ref, ob_ref = (
        lambda sl, p=p, q=q: jnp.where(is_p, p[:, sl], q[:, sl])
        for p, q in ((hpf_ref, hsf_ref), (hpb_ref, hsb_ref), (opf_ref, osf_ref), (opb_ref, osb_ref)))
    parts = []
    for h in range(HA):
        sl = slice(h * DH, (h + 1) * DH)
        hh = hf_ref(sl) + hb_ref(sl)
        mu = jnp.mean(hh, axis=-1, keepdims=True)
        var = jnp.mean(jnp.square(hh - mu), axis=-1, keepdims=True)
        parts.append(_sigmoid(oa_ref[:, sl]) * ((hh - mu) * lax.rsqrt(var + LN_EPS) * mg_ref[:, sl]))
    for h in range(HB):
        sl = slice(h * DH, (h + 1) * DH)
        oo = of_ref(sl) + ob_ref(sl)
        z = zb_ref[:, sl]
        nrm = oo * lax.rsqrt(jnp.mean(jnp.square(oo), axis=-1, keepdims=True) + LN_EPS) * dg_ref[:, sl]
        parts.append(nrm * (z * _sigmoid(z)))
    a = jnp.concatenate(parts, axis=1)
    y = _mm(_bf(a), w_ref[...])
    o_ref[...] = _residual_ln(_load_x(x_refs, tm), gate_ref[0], y, g_ref[...], b_ref[...])


def _even_out(h_scan, o_scan, proj, mg, dg, w, x, mods, layer, ln_g, ln_b):
    tm = 512
    npt = NP_ROWS // tm
    row512 = lambda i: (i, 0)
    const2 = lambda i: (0, 0)
    p_spec = pl.BlockSpec((tm, 512), lambda i: (jnp.minimum(i, npt - 1), 0))
    s_spec = pl.BlockSpec((tm, 512), lambda i: (jnp.maximum(i - npt, 0), 0))
    x_specs, x_arrays = _x_operands(x, tm)
    return pl.pallas_call(
        functools.partial(_even_out_kernel, tm=tm, n_x=len(x_arrays)),
        out_shape=jax.ShapeDtypeStruct((T, D), F32),
        grid=(T // tm,),
        in_specs=x_specs + [p_spec, p_spec, s_spec, s_spec, p_spec, p_spec, s_spec, s_spec,
                            pl.BlockSpec((tm, 512), lambda i: (i, 6)),
                            pl.BlockSpec((tm, 512), lambda i: (i, 7)),
                            pl.BlockSpec((1, 512), const2), pl.BlockSpec((1, 512), const2),
                            pl.BlockSpec((D, D), const2),
                            _mod_spec(layer, 2, tm),
                            pl.BlockSpec((1, D), const2), pl.BlockSpec((1, D), const2)],
        out_specs=pl.BlockSpec((tm, D), row512),
        compiler_params=_params(1),
        name="even_out_ln",
    )(*x_arrays, *h_scan, *o_scan, proj, proj, mg, dg, w, mods, ln_g, ln_b)


def _odd_out_kernel(ap_ref, as_ref, w_ref, x_ref, gate_ref, g_ref, b_ref, o_ref, *, tm):
    a = jnp.where(pl.program_id(0) < NP_ROWS // tm, ap_ref[...], as_ref[...])
    y = _mm(_bf(a), w_ref[...])
    o_ref[...] = _residual_ln(x_ref[...], gate_ref[0], y, g_ref[...], b_ref[...])


def _odd_out(a_prompt, a_latent, w, x, mods, layer, ln_g, ln_b):
    tm = 512
    npt = NP_ROWS // tm
    row = lambda i: (i, 0)
    const2 = lambda i: (0, 0)
    return pl.pallas_call(
        functools.partial(_odd_out_kernel, tm=tm),
        out_shape=jax.ShapeDtypeStruct((T, D), F32),
        grid=(T // tm,),
        in_specs=[pl.BlockSpec((tm, D), lambda i: (jnp.minimum(i, npt - 1), 0)),
                  pl.BlockSpec((tm, D), lambda i: (jnp.maximum(i - npt, 0), 0)),
                  pl.BlockSpec((D, D), const2), pl.BlockSpec((tm, D), row),
                  _mod_spec(layer, 2, tm), pl.BlockSpec((1, D), const2), pl.BlockSpec((1, D), const2)],
        out_specs=pl.BlockSpec((tm, D), row),
        compiler_params=_params(1),
        name="odd_out_ln",
    )(a_prompt, a_latent, w, x, mods, ln_g, ln_b)


def _attn_ctx_kernel(qkv_ref, sink_ref, o_ref):
    ones = jnp.ones((SEQ, HD), BF16)
    for kv in range(KVH):
        k = _bf(qkv_ref[:, HC * HD + kv * HD:HC * HD + (kv + 1) * HD])
        v = _bf(qkv_ref[:, (HC + KVH) * HD + kv * HD:(HC + KVH) * HD + (kv + 1) * HD])
        v1 = jnp.concatenate([v, ones], axis=1)
        heads = range(kv * (HC // KVH), (kv + 1) * (HC // KVH))
        sts = [_mm_nt(k, _bf(qkv_ref[:, h * HD:(h + 1) * HD])) for h in heads]
        ms = [jnp.maximum(jnp.max(st, axis=0, keepdims=True), sink_ref[h]) for st, h in zip(sts, heads)]
        ovs = [_mm_tn(v1, _bf(jnp.exp(st - m))) for st, m in zip(sts, ms)]
        outs = [ov[:HD] / (ov[HD:HD + 1] + jnp.exp(sink_ref[h] - m)) for ov, m, h in zip(ovs, ms, heads)]
        for pr in range(HC // KVH // 2):
            lane0 = (kv * (HC // KVH) + 2 * pr) * HD
            o_ref[:, lane0:lane0 + 2 * HD] = jnp.concatenate(outs[2 * pr:2 * pr + 2], axis=0).T


def _attn_context(qkv, sink):
    return pl.pallas_call(
        _attn_ctx_kernel,
        out_shape=jax.ShapeDtypeStruct((NP_ROWS, HC * HD), F32),
        grid=(BATCH,),
        in_specs=[pl.BlockSpec((SEQ, QKV_W), lambda b: (b, 0)),
                  pl.BlockSpec(memory_space=pltpu.SMEM)],
        out_specs=pl.BlockSpec((SEQ, HC * HD), lambda b: (b, 0)),
        compiler_params=_params(1),
        name="attn_context",
    )(qkv, sink)


def _attn_lat_kernel(q_ref, kp_ref, kc_ref, kn_ref, vp_ref, vc_ref, vn_ref, ck_ref, cv_ref,
                     sink_ref, o_ref, bias_s):
    j = pl.program_id(1)
    nb = DEC_SEQ // QBLOCK
    cc = lax.broadcasted_iota(jnp.int32, (3 * QBLOCK, QBLOCK), 0)
    r = lax.broadcasted_iota(jnp.int32, (3 * QBLOCK, QBLOCK), 1)
    lo = jnp.where(j >= 1, 0, QBLOCK)
    hi = jnp.where(j <= nb - 2, 3 * QBLOCK, 2 * QBLOCK)
    ok = (jnp.abs(QBLOCK + r - cc) <= WINDOW) & (cc >= lo) & (cc < hi)
    bias_s[...] = jnp.where(ok, 0.0, -jnp.inf)
    for kv in range(KVH):
        ks = slice(kv * HD, (kv + 1) * HD)
        k_all = _bf(jnp.concatenate([ck_ref[0, 0, kv], kp_ref[:, ks], kc_ref[:, ks], kn_ref[:, ks]], axis=0))
        v_all = _bf(jnp.concatenate([cv_ref[0, 0, kv], vp_ref[:, ks], vc_ref[:, ks], vn_ref[:, ks]], axis=0))
        v1 = jnp.concatenate([v_all, jnp.ones_like(v_all)], axis=1)
        outs = []
        for h in range(kv * (HC // KVH), (kv + 1) * (HC // KVH)):
            sink = sink_ref[h]
            st = _mm_nt(k_all, _bf(q_ref[:, h * HD:(h + 1) * HD]))
            s_ctx = st[:PAST_LEN]
            s_loc = st[PAST_LEN:] + bias_s[...]
            m = jnp.maximum(jnp.maximum(jnp.max(s_ctx, axis=0, keepdims=True),
                                        jnp.max(s_loc, axis=0, keepdims=True)), sink)
            p = _bf(jnp.concatenate([jnp.exp(s_ctx - m), jnp.exp(s_loc - m)], axis=0))
            ov = _mm_tn(v1, p)
            outs.append(ov[:HD] / (ov[HD:HD + 1] + jnp.exp(sink - m)))
        for pr in range(HC // KVH // 2):
            lane0 = (kv * (HC // KVH) + 2 * pr) * HD
            o_ref[:, lane0:lane0 + 2 * HD] = jnp.concatenate(outs[2 * pr:2 * pr + 2], axis=0).T


def _attn_latent(qkv, cache_k, cache_v, sink, o):
    nb = DEC_SEQ // QBLOCK
    base = NP_ROWS // QBLOCK
    blk = lambda b, j: base + b * nb + j
    prev = lambda b, j: base + b * nb + jnp.maximum(j - 1, 0)
    nxt = lambda b, j: base + b * nb + jnp.minimum(j + 1, nb - 1)
    kcol, vcol = HC * HD // 256, HC * HD // 256 + 1
    cache_spec = pl.BlockSpec((1, 1, KVH, PAST_LEN, HD), lambda b, j: (b, o, 0, 0, 0))
    return pl.pallas_call(
        _attn_lat_kernel,
        out_shape=jax.ShapeDtypeStruct((NS_ROWS, HC * HD), F32),
        grid=(DEC_BATCH, nb),
        in_specs=[pl.BlockSpec((QBLOCK, HC * HD), lambda b, j: (blk(b, j), 0)),
                  pl.BlockSpec((QBLOCK, 256), lambda b, j: (prev(b, j), kcol)),
                  pl.BlockSpec((QBLOCK, 256), lambda b, j: (blk(b, j), kcol)),
                  pl.BlockSpec((QBLOCK, 256), lambda b, j: (nxt(b, j), kcol)),
                  pl.BlockSpec((QBLOCK, 256), lambda b, j: (prev(b, j), vcol)),
                  pl.BlockSpec((QBLOCK, 256), lambda b, j: (blk(b, j), vcol)),
                  pl.BlockSpec((QBLOCK, 256), lambda b, j: (nxt(b, j), vcol)),
                  cache_spec, cache_spec,
                  pl.BlockSpec(memory_space=pltpu.SMEM)],
        out_specs=pl.BlockSpec((QBLOCK, HC * HD), lambda b, j: (b * nb + j, 0)),
        scratch_shapes=[pltpu.VMEM((3 * QBLOCK, QBLOCK), F32)],
        compiler_params=_params(2),
        name="attn_latent",
    )(qkv, qkv, qkv, qkv, qkv, qkv, qkv, cache_k, cache_v, sink)


def _route(lg):
    lane = lax.broadcasted_iota(jnp.int32, lg.shape, 1)
    neg = -jnp.inf
    big = 1 << 20
    is_grp = jnp.logical_and(lane >= N_EXPERTS, lane < N_EXPERTS + N_GROUPS)
    mg = jnp.max(jnp.where(is_grp, lg, neg), axis=1, keepdims=True)
    g_lane = jnp.min(jnp.where(jnp.logical_and(is_grp, lg == mg), lane, big), axis=1, keepdims=True)
    g_w = 1.0 / jnp.sum(jnp.where(is_grp, jnp.exp(lg - mg), 0.0), axis=1, keepdims=True)
    g_idx = g_lane - N_EXPERTS
    in_grp = jnp.logical_and(lane >= g_idx * EPG, lane < (g_idx + 1) * EPG)
    v1 = jnp.max(jnp.where(in_grp, lg, neg), axis=1, keepdims=True)
    i1 = jnp.min(jnp.where(jnp.logical_and(in_grp, lg == v1), lane, big), axis=1, keepdims=True)
    rest = jnp.logical_and(in_grp, lane != i1)
    v2 = jnp.max(jnp.where(rest, lg, neg), axis=1, keepdims=True)
    i2 = jnp.min(jnp.where(jnp.logical_and(rest, lg == v2), lane, big), axis=1, keepdims=True)
    e2 = jnp.exp(v2 - v1)
    p1 = 1.0 / (1.0 + e2)
    p2 = e2 / (1.0 + e2)
    return jnp.where(lane == 0, i1.astype(F32),
                     jnp.where(lane == 1, i2.astype(F32),
                               jnp.where(lane == 2, p1 * g_w, jnp.where(lane == 3, p2 * g_w, 0.0))))


def _moe_dense_kernel(x_ref, sh_ref, sc_ref, wr_ref, br_ref, wg_ref, wu_ref, wd_ref, gate_ref, g_ref, b_ref,
                      *out_and_scratch, tm):
    *o_refs, xm_s, meta_s, acc_s = out_and_scratch
    grp = pl.program_id(1)

    @pl.when(grp == 0)
    def _():
        xm = x_ref[...] * (1.0 + sc_ref[0]) + sh_ref[0]
        meta_s[...] = _route(_mm(_lhs3(xm), _rhs3(wr_ref[...])) + br_ref[...])
        xm_s[...] = _bf(xm)
        acc_s[...] = jnp.zeros_like(acc_s)

    xm = xm_s[...]
    meta = meta_s[...]
    i1 = meta[:, 0:1].astype(jnp.int32)
    i2 = meta[:, 1:2].astype(jnp.int32)
    w1 = meta[:, 2:3]
    w2 = meta[:, 3:4]
    hid = []
    for e in range(EPG):
        eid = grp * EPG + e
        gate = jnp.where(i1 == eid, w1, 0.0) + jnp.where(i2 == eid, w2, 0.0)
        a = _mm(xm, wg_ref[0, e])
        u = _mm(xm, wu_ref[0, e])
        hid.append(_bf((a * _sigmoid(a)) * u * gate))
    acc_s[...] += _mm(jnp.concatenate(hid, axis=1), wd_ref[0])

    def result():
        return _residual_ln(x_ref[...], gate_ref[0], acc_s[...], g_ref[...], b_ref[...])

    last = grp == N_GROUPS - 1
    if len(o_refs) == 1:
        @pl.when(last)
        def _():
            o_refs[0][...] = result()
    else:
        is_p = pl.program_id(0) < NP_ROWS // tm

        @pl.when(jnp.logical_and(last, is_p))
        def _():
            o_refs[0][...] = result()

        @pl.when(jnp.logical_and(last, jnp.logical_not(is_p)))
        def _():
            o_refs[1][...] = result()


def _moe_dense(x, mods, layer, w_r, b_r, wg, wu, wd, ln_g, ln_b, split_out=False):
    tm = 1024
    npt = NP_ROWS // tm
    row = lambda i, g: (i, 0)
    const2 = lambda i, g: (0, 0)
    if split_out:
        half = jax.ShapeDtypeStruct((NP_ROWS, D), F32)
        out_shape = (half, half)
        out_specs = (pl.BlockSpec((tm, D), lambda i, g: (jnp.minimum(i, npt - 1), 0)),
                     pl.BlockSpec((tm, D), lambda i, g: (jnp.maximum(i - npt, 0), 0)))
    else:
        out_shape = jax.ShapeDtypeStruct((T, D), F32)
        out_specs = pl.BlockSpec((tm, D), row)
    return pl.pallas_call(
        functools.partial(_moe_dense_kernel, tm=tm),
        out_shape=out_shape,
        grid=(T // tm, N_GROUPS),
        in_specs=[pl.BlockSpec((tm, D), row), _mod_spec(layer, 3, tm), _mod_spec(layer, 4, tm),
                  pl.BlockSpec((D, 128), const2), pl.BlockSpec((1, 128), const2),
                  pl.BlockSpec((1, EPG, D, EXPERT_FF), lambda i, g: (g, 0, 0, 0)),
                  pl.BlockSpec((1, EPG, D, EXPERT_FF), lambda i, g: (g, 0, 0, 0)),
                  pl.BlockSpec((1, EPG * EXPERT_FF, D), lambda i, g: (g, 0, 0)),
                  _mod_spec(layer, 5, tm), pl.BlockSpec((1, D), const2), pl.BlockSpec((1, D), const2)],
        out_specs=out_specs,
        scratch_shapes=[pltpu.VMEM((tm, D), BF16), pltpu.VMEM((tm, 128), F32), pltpu.VMEM((tm, D), F32)],
        compiler_params=_params(2, VMEM_LIMIT + (2 * tm * D * 4 if split_out else 0)),
        name="moe_dense",
    )(x, mods, mods, w_r, b_r, wg, wu, wd, mods, ln_g, ln_b)


def _permute_even_w(w):
    a_end = 4 * HA * DH
    g_end = a_end + 4 * HA
    c_end = g_end + 3 * HB * DH
    z_end = c_end + HB * DH
    small = jnp.concatenate([w[:, a_end:g_end], w[:, z_end:]], axis=1)
    pad = jnp.zeros((w.shape[0], 128 - small.shape[1]), w.dtype)
    return jnp.concatenate([w[:, g_end:c_end], w[:, :a_end], w[:, c_end:z_end], small, pad], axis=1)


def _lane_row(vals, offset):
    return jnp.zeros((1, 128), F32).at[0, offset:offset + vals.shape[0]].set(vals.astype(F32))


def _pair_rows(vals, offset):
    cols = jnp.zeros((N_GATE_COLS,), F32).at[offset:offset + vals.shape[0]].set(vals.astype(F32))
    return jnp.repeat(cols.reshape(N_GATE_COLS // 2, 2), CHUNK, axis=1)


def kernel(x_prompt, x_sample, c, c_ctx, state_mlstm_c, state_mlstm_n, state_mlstm_m, state_delta, cache_k, cache_v, w_mod, b_mod, ln_g, ln_b, w_in_even, mlstm_gate_b, mlstm_norm_g, delta_conv_w, delta_a_log, delta_dt_bias, delta_norm_g, w_out_even, w_qkv_odd, attn_sink, w_out_odd, w_grp, b_grp, w_erouter, b_erouter, w_gate, w_up, w_down):
    x = (x_prompt.reshape(NP_ROWS, D), x_sample.reshape(NS_ROWS, D))
    cvecs = jnp.concatenate([c_ctx[None, :], c, jnp.zeros((N_MOD_ROWS - 1 - DEC_BATCH, D), F32)], axis=0)
    mods = _modulation(cvecs, w_mod, b_mod)
    tables = _rope_tables()
    m0_all = jnp.broadcast_to(state_mlstm_m.reshape(DEC_BATCH, N_EVEN, 2 * HA, 1), (DEC_BATCH, N_EVEN, 2 * HA, 128))

    out_mc, out_mn, out_mm, out_ds = [], [], [], []
    new_k = jnp.zeros((BATCH, N_ODD, KVH, SEQ, HD), F32)
    new_v = jnp.zeros((BATCH, N_ODD, KVH, SEQ, HD), F32)
    for l in range(DEPTH):
        if l % 2 == 0:
            e = l // 2
            proj = _even_proj(x, mods, l, _bf(_permute_even_w(w_in_even[e])))
            gates_t2 = (proj[:, EVEN_W - 128:EVEN_W - 128 + N_GATE_COLS]
                        .reshape(T // CHUNK, CHUNK, N_GATE_COLS).transpose(0, 2, 1)
                        .reshape(T // CHUNK, N_GATE_COLS // 2, 2 * CHUNK))
            gb = mlstm_gate_b[e].reshape(-1)
            *h_scan, mc, mn, mm = _mlstm_scan(proj, gates_t2, _lane_row(gb, 0), _pair_rows(gb, 0),
                                              state_mlstm_c, state_mlstm_n, m0_all, e)
            qkv = _delta_prep(proj, delta_conv_w[e])
            dtb = delta_dt_bias[e].reshape(-1)
            nea = -jnp.exp(delta_a_log[e].astype(F32)).reshape(-1)
            *o_scan, ds = _delta_scan(qkv, proj, gates_t2, _lane_row(dtb, DECAY_COL0), _pair_rows(dtb, DECAY_COL0),
                                      _lane_row(nea, DECAY_COL0), _pair_rows(nea, DECAY_COL0), state_delta, e)
            x = _even_out(h_scan, o_scan, proj, mlstm_norm_g[e][None, :], delta_norm_g[e][None, :],
                          _bf(w_out_even[e]), x, mods, l, ln_g[l, 0][None, :], ln_b[l, 0][None, :])
            out_mc.append(mc)
            out_mn.append(mn)
            out_mm.append(mm[:, :, 0].reshape(BATCH, 2, HA))
            out_ds.append(ds)
        else:
            o = l // 2
            qkv, new_k, new_v = _odd_proj(x, mods, l, _bf(w_qkv_odd[o]), tables, new_k, new_v, o)
            a_p = _attn_context(qkv, attn_sink[o])
            a_s = _attn_latent(qkv, cache_k, cache_v, attn_sink[o], o)
            x = _odd_out(a_p, a_s, _bf(w_out_odd[o]), x, mods, l, ln_g[l, 0][None, :], ln_b[l, 0][None, :])
        w_r = jnp.concatenate([w_erouter[l].transpose(1, 0, 2).reshape(D, N_EXPERTS), w_grp[l],
                               jnp.zeros((D, 128 - N_EXPERTS - N_GROUPS), F32)], axis=1)
        b_r = jnp.concatenate([b_erouter[l].reshape(-1), b_grp[l],
                               jnp.zeros((128 - N_EXPERTS - N_GROUPS,), F32)])[None, :]
        x = _moe_dense(x, mods, l, w_r, b_r,
                       _bf(w_gate[l]).reshape(N_GROUPS, EPG, D, EXPERT_FF),
                       _bf(w_up[l]).reshape(N_GROUPS, EPG, D, EXPERT_FF),
                       _bf(w_down[l]).reshape(N_GROUPS, EPG * EXPERT_FF, D),
                       ln_g[l, 1][None, :], ln_b[l, 1][None, :], split_out=(l == DEPTH - 1))
    return (x[0].reshape(BATCH, SEQ, D), x[1].reshape(DEC_BATCH, DEC_SEQ, D),
            jnp.stack(out_mc, 1), jnp.stack(out_mn, 1), jnp.stack(out_mm, 1), jnp.stack(out_ds, 1),
            new_k, new_v)
```

```python
import functools

import numpy as np
import jax
import jax.numpy as jnp
from jax import lax
from jax.experimental import pallas as pl
from jax.experimental.pallas import tpu as pltpu

F32 = jnp.float32
BF16 = jnp.bfloat16
HI = lax.Precision.HIGHEST

D = 1024
BATCH = 32
SEQ = 256
DEPTH = 4
DEC_BATCH = 2
DEC_SEQ = 4096
PAST_LEN = 512
GRID_W = 64
N_EVEN = 2
N_ODD = 2
HA = 4
HB = 4
DH = 128
CHUNK = 64
HC = 16
KVH = 4
HD = 64
WINDOW = 128
QBLOCK = 128
ROPE_THETA = 10000.0
N_GROUPS = 4
EPG = 4
N_EXPERTS = 16
EXPERT_FF = 256
DN_ALPHA = (2 * DEPTH) ** 0.25
LN_EPS = 1e-5

NP_ROWS = BATCH * SEQ
NS_ROWS = DEC_BATCH * DEC_SEQ
T = NP_ROWS + NS_ROWS
N_MOD_ROWS = 8
EVEN_W = 4224
N_GATE_COLS = 4 * HA + 4 * HB
BETA_COL0 = 4 * HA
DECAY_COL0 = 4 * HA + 2 * HB
QKV_W = (HC + 2 * KVH) * HD

P_CHUNKS = SEQ // CHUNK
S_CHUNKS = DEC_SEQ // CHUNK
P_STEPS = BATCH * P_CHUNKS
S_STEPS = DEC_BATCH * S_CHUNKS
P_BLOCKS = NP_ROWS // CHUNK

VMEM_LIMIT = 48 * 1024 * 1024


def _params(n_axes, vmem_limit=VMEM_LIMIT):
    return pltpu.CompilerParams(dimension_semantics=("arbitrary",) * n_axes,
                                vmem_limit_bytes=vmem_limit)


def _mm(a, b, prec=None):
    return lax.dot_general(a, b, (((1,), (0,)), ((), ())), precision=prec, preferred_element_type=F32)


def _mm_nt(a, b, prec=None):
    return lax.dot_general(a, b, (((1,), (1,)), ((), ())), precision=prec, preferred_element_type=F32)


def _mm_tn(a, b, prec=None):
    return lax.dot_general(a, b, (((0,), (0,)), ((), ())), precision=prec, preferred_element_type=F32)


def _bf(x):
    return x.astype(BF16)


def _sigmoid(x):
    return 1.0 / (1.0 + jnp.exp(-x))


def _softplus(x):
    return jnp.maximum(x, 0.0) + jnp.log1p(jnp.exp(-jnp.abs(x)))


def _log_sigmoid(x):
    return jnp.minimum(x, 0.0) - jnp.log1p(jnp.exp(-jnp.abs(x)))


def _mod_row(tile, tm):
    npt = NP_ROWS // tm
    per = DEC_SEQ // tm
    return jnp.where(tile < npt, 0, 1 + (tile - npt) // per)


def _mod_spec(layer, chunk, tm):
    def imap(i, *_):
        return ((layer * N_MOD_ROWS + _mod_row(i, tm)) * 6 + chunk, 0, 0)
    return pl.BlockSpec((1, 1, D), imap)


def _modulation_kernel(c_ref, w_ref, b_ref, o_ref):
    x = c_ref[...]
    s = x * _sigmoid(x)
    o_ref[0] = _mm(s, w_ref[0], HI) + b_ref[0]


def _modulation(cvecs, w_mod, b_mod):
    out = pl.pallas_call(
        _modulation_kernel,
        out_shape=jax.ShapeDtypeStruct((DEPTH, N_MOD_ROWS, 6 * D), F32),
        grid=(DEPTH, 6),
        in_specs=[pl.BlockSpec((N_MOD_ROWS, D), lambda l, j: (0, 0)),
                  pl.BlockSpec((1, D, D), lambda l, j: (l, 0, j)),
                  pl.BlockSpec((1, 1, D), lambda l, j: (l * 6 + j, 0, 0))],
        out_specs=pl.BlockSpec((1, N_MOD_ROWS, D), lambda l, j: (l, 0, j)),
        compiler_params=_params(2),
        name="modulation",
    )(cvecs, w_mod, b_mod.reshape(DEPTH * 6, 1, D))
    return out.reshape(DEPTH * N_MOD_ROWS * 6, 1, D)


def _x_operands(x, tm):
    if isinstance(x, tuple):
        npt = NP_ROWS // tm
        return ([pl.BlockSpec((tm, D), lambda i, *_: (jnp.minimum(i, npt - 1), 0)),
                 pl.BlockSpec((tm, D), lambda i, *_: (jnp.maximum(i - npt, 0), 0))], list(x))
    return [pl.BlockSpec((tm, D), lambda i, *_: (i, 0))], [x]


def _load_x(x_refs, tm):
    if len(x_refs) == 1:
        return x_refs[0][...]
    return jnp.where(pl.program_id(0) < NP_ROWS // tm, x_refs[0][...], x_refs[1][...])


def _proj_kernel(*refs, tm, n_x):
    x_refs, (sh_ref, sc_ref, w_ref, o_ref) = refs[:n_x], refs[n_x:]
    xm = _load_x(x_refs, tm) * (1.0 + sc_ref[0]) + sh_ref[0]
    o_ref[...] = _mm(_bf(xm), w_ref[...])


def _even_proj(x, mods, layer, w):
    tm = 256
    x_specs, x_arrays = _x_operands(x, tm)
    return pl.pallas_call(
        functools.partial(_proj_kernel, tm=tm, n_x=len(x_arrays)),
        out_shape=jax.ShapeDtypeStruct((T, EVEN_W), F32),
        grid=(T // tm,),
        in_specs=x_specs + [_mod_spec(layer, 0, tm), _mod_spec(layer, 1, tm),
                            pl.BlockSpec((D, EVEN_W), lambda i: (0, 0))],
        out_specs=pl.BlockSpec((tm, EVEN_W), lambda i: (i, 0)),
        compiler_params=_params(1),
        name="even_proj",
    )(*x_arrays, mods, mods, w)


def _qkv_kernel(x_ref, sh_ref, sc_ref, w_ref, cos_ref, sa_ref, sb_ref, kc_in, vc_in, o_ref, kc_ref, vc_ref, *, tm):
    del kc_in, vc_in
    i = pl.program_id(0)
    xm = x_ref[...] * (1.0 + sc_ref[0]) + sh_ref[0]
    acc = _mm(_bf(xm), w_ref[...])

    @pl.when(i < NP_ROWS // tm)
    def _():
        for b in range(tm // SEQ):
            for kv in range(KVH):
                k0 = HC * HD + kv * HD
                v0 = (HC + KVH) * HD + kv * HD
                kc_ref[b, 0, kv] = acc[b * SEQ:(b + 1) * SEQ, k0:k0 + HD]
                vc_ref[b, 0, kv] = acc[b * SEQ:(b + 1) * SEQ, v0:v0 + HD]

    n_q = HC * HD // 128
    n_k = KVH * HD // 128
    is_latent = i >= NP_ROWS // tm

    def scaled(g):
        blk = acc[:, g * 128:(g + 1) * 128]
        return blk * (HD ** -0.5) if g < n_q else blk

    @pl.when(is_latent)
    def _():
        cos, sa, sb = cos_ref[...], sa_ref[...], sb_ref[...]
        for g in range(n_q + n_k):
            blk = scaled(g)
            o_ref[:, g * 128:(g + 1) * 128] = blk * cos + pltpu.roll(blk, 112, 1) * sa + pltpu.roll(blk, 16, 1) * sb

    @pl.when(jnp.logical_not(is_latent))
    def _():
        for g in range(n_q + n_k):
            o_ref[:, g * 128:(g + 1) * 128] = scaled(g)

    o_ref[:, (n_q + n_k) * 128:] = acc[:, (n_q + n_k) * 128:]


def _rope_tables():
    half = HD // 4
    inv = np.float32(ROPE_THETA) ** (-np.arange(half, dtype=np.float32) / np.float32(half))
    pos = np.arange(DEC_SEQ)
    row = (pos // GRID_W).astype(np.float32)[:, None] * inv[None, :]
    col = (pos % GRID_W).astype(np.float32)[:, None] * inv[None, :]
    cos = np.concatenate([np.cos(row), np.cos(row), np.cos(col), np.cos(col)], axis=-1)
    sin = np.concatenate([np.sin(row), np.sin(row), np.sin(col), np.sin(col)], axis=-1)
    first = (np.arange(HD) % 32) < 16
    sa = np.where(first, -sin, 0.0)
    sb = np.where(first, 0.0, sin)
    tile2 = lambda t: jnp.asarray(np.concatenate([t, t], axis=-1), F32)
    return tile2(cos), tile2(sa), tile2(sb)


def _odd_proj(x, mods, layer, w, tables, cache_k, cache_v, o):
    tm = 512
    npt = NP_ROWS // tm
    per = DEC_SEQ // tm
    tab_spec = pl.BlockSpec((tm, 128), lambda i: (jnp.where(i < npt, 0, (i - npt) % per), 0))
    cache_spec = pl.BlockSpec((tm // SEQ, 1, KVH, SEQ, HD), lambda i: (jnp.minimum(i, npt - 1), o, 0, 0, 0))
    cache_shape = jax.ShapeDtypeStruct((BATCH, N_ODD, KVH, SEQ, HD), F32)
    return pl.pallas_call(
        functools.partial(_qkv_kernel, tm=tm),
        out_shape=(jax.ShapeDtypeStruct((T, QKV_W), F32), cache_shape, cache_shape),
        grid=(T // tm,),
        in_specs=[pl.BlockSpec((tm, D), lambda i: (i, 0)),
                  _mod_spec(layer, 0, tm), _mod_spec(layer, 1, tm),
                  pl.BlockSpec((D, QKV_W), lambda i: (0, 0)),
                  tab_spec, tab_spec, tab_spec,
                  pl.BlockSpec(memory_space=pl.ANY), pl.BlockSpec(memory_space=pl.ANY)],
        out_specs=(pl.BlockSpec((tm, QKV_W), lambda i: (i, 0)), cache_spec, cache_spec),
        input_output_aliases={7: 1, 8: 2},
        compiler_params=_params(1),
        name="odd_qkv_proj",
    )(x, mods, mods, w, *tables, cache_k, cache_v)


PREP_ROWS = 256


def _delta_prep_kernel(x_ref, prev_ref, next_ref, w_ref, o_ref):
    i = pl.program_id(0)
    npb = NP_ROWS // PREP_ROWS
    per = DEC_SEQ // PREP_ROWS
    is_latent = i >= npb
    pos = (i - npb) % per
    has_prev = jnp.logical_and(is_latent, pos > 0)
    has_next = jnp.logical_and(is_latent, pos < per - 1)
    x = x_ref[...]
    w = w_ref[...]
    rows = lax.broadcasted_iota(jnp.int32, x.shape, 0)
    prev_row = jnp.where(has_prev, prev_ref[7:8, :], 0.0)
    next_row = jnp.where(has_next, next_ref[0:1, :], 0.0)
    xm1 = jnp.where(rows == 0, prev_row, pltpu.roll(x, 1, 0))
    xp1 = jnp.where(rows == PREP_ROWS - 1, next_row, pltpu.roll(x, PREP_ROWS - 1, 0))
    y = xm1 * w[0:1, :] + x * w[1:2, :] + xp1 * w[2:3, :]
    y = y * _sigmoid(y)
    for h in range(3 * HB):
        yh = y[:, h * DH:(h + 1) * DH]
        if h < 2 * HB:
            inv = lax.rsqrt(jnp.sum(yh * yh, axis=-1, keepdims=True) + 1e-6)
            yh = yh * (inv * (DH ** -0.5) if h < HB else inv)
        o_ref[:, h * DH:(h + 1) * DH] = yh


def _delta_prep(proj, conv_w):
    nblk = T // PREP_ROWS
    sub = PREP_ROWS // 8
    last8 = T // 8 - 1
    return pl.pallas_call(
        _delta_prep_kernel,
        out_shape=jax.ShapeDtypeStruct((T, 3 * 512), F32),
        grid=(nblk,),
        in_specs=[pl.BlockSpec((PREP_ROWS, 3 * 512), lambda i: (i, 0)),
                  pl.BlockSpec((8, 3 * 512), lambda i: (jnp.maximum(i * sub - 1, 0), 0)),
                  pl.BlockSpec((8, 3 * 512), lambda i: (jnp.minimum((i + 1) * sub, last8), 0)),
                  pl.BlockSpec((3, 3 * 512), lambda i: (0, 0))],
        out_specs=pl.BlockSpec((PREP_ROWS, 3 * 512), lambda i: (i, 0)),
        compiler_params=_params(1),
        name="delta_prep",
    )(proj, proj, proj, conv_w)


assert P_STEPS == S_STEPS
SCAN_STEPS = P_STEPS


def _bwd_local(s, nc):
    return (s // nc) * nc + nc - 1 - s % nc


def _scan_blocks():
    return (lambda s: s, lambda s: _bwd_local(s, P_CHUNKS),
            lambda s: P_BLOCKS + s, lambda s: P_BLOCKS + _bwd_local(s, S_CHUNKS))


def _scan_specs(xcol):
    gcol = EVEN_W // 128 - 1
    blocks = _scan_blocks()
    return ([pl.BlockSpec((CHUNK, 3 * 512), lambda s, f=f: (f(s), xcol)) for f in blocks]
            + [pl.BlockSpec((CHUNK, 128), lambda s, f=f: (f(s), gcol)) for f in blocks]
            + [pl.BlockSpec((1, 16, 128), lambda s, f=f: (f(s), 0, 0)) for f in blocks])


def _scan_out_specs():
    local = (lambda s: s, lambda s: _bwd_local(s, P_CHUNKS), lambda s: s, lambda s: _bwd_local(s, S_CHUNKS))
    return [pl.BlockSpec((CHUNK, 512), lambda s, f=f: (f(s), 0)) for f in local]


def _mlstm_kernel(xpf_ref, xpb_ref, xsf_ref, xsb_ref, gpf_ref, gpb_ref, gsf_ref, gsb_ref,
                  rpf_ref, rpb_ref, rsf_ref, rsb_ref,
                  brow_ref, b2_ref, c0_ref, n0_ref, m0_ref,
                  hpf_ref, hpb_ref, hsf_ref, hsb_ref, c_out, n_out, m_out,
                  cp_s, np_s, mp_s, cs_s, ns_s, ms_s):
    s = pl.program_id(0)
    jp = s % P_CHUNKS
    js = s % S_CHUNKS

    @pl.when(jp == 0)
    def _():
        cp_s[...] = jnp.zeros_like(cp_s)
        np_s[...] = jnp.zeros_like(np_s)
        mp_s[...] = jnp.zeros_like(mp_s)

    @pl.when(js == 0)
    def _():
        for d in range(2):
            for h in range(HA):
                cs_s[d * HA + h] = c0_ref[0, 0, d, h]
                ns_s[d * HA + h] = jnp.broadcast_to(n0_ref[0, 0, d, h:h + 1, :], (DH, 128)).T
        ms_s[...] = m0_ref[0, 0]

    states = ((cp_s, np_s, mp_s), (cs_s, ns_s, ms_s))

    row = lax.broadcasted_iota(jnp.int32, (CHUNK, 128), 0)
    lane = lax.broadcasted_iota(jnp.int32, (CHUNK, 128), 1)
    left = lane < CHUNK
    lcol = jnp.where(left, lane, lane - CHUNK)
    r64 = lax.broadcasted_iota(jnp.int32, (CHUNK, CHUNK), 0)
    c64 = lax.broadcasted_iota(jnp.int32, (CHUNK, CHUNK), 1)
    r128 = lax.broadcasted_iota(jnp.int32, (128, 128), 0)
    c128 = lax.broadcasted_iota(jnp.int32, (128, 128), 1)
    same_half = (r128 < CHUNK) == (c128 < CHUNK)
    neg = -jnp.inf
    ins = ((xpf_ref, gpf_ref, rpf_ref, hpf_ref), (xpb_ref, gpb_ref, rpb_ref, hpb_ref),
           (xsf_ref, gsf_ref, rsf_ref, hsf_ref), (xsb_ref, gsb_ref, rsb_ref, hsb_ref))

    sel_r = lax.broadcasted_iota(jnp.int32, (128, HA * 128), 0)
    sel_h = lax.broadcasted_iota(jnp.int32, (128, HA * 128), 1) // 128
    ones_lr = lax.broadcasted_iota(jnp.int32, (256, 256), 0)
    ones_lc = lax.broadcasted_iota(jnp.int32, (256, 256), 1)
    half_sum = (((ones_lr % 128) < CHUNK) == (ones_lc < 128)).astype(BF16)
    ones_l = jnp.ones((CHUNK, 128), BF16)
    rows256 = lax.broadcasted_iota(jnp.int32, (CHUNK, 256), 0)

    def running_max(x, d):
        sh = 1
        while sh < CHUNK:
            if d == 0:
                x = jnp.maximum(x, jnp.where(rows256 >= sh, pltpu.roll(x, sh, 0), neg))
            else:
                x = jnp.maximum(x, jnp.where(rows256 < CHUNK - sh, pltpu.roll(x, CHUNK - sh, 0), neg))
            sh *= 2
        return x

    gate = []
    for u in range(4):
        d = u % 2
        gc_ref, gr_ref = ins[u][1], ins[u][2]
        if d == 0:
            incl_p, tri_c, tri_r = lcol <= row, c64 <= r64, jnp.logical_and(same_half, r128 <= c128)
        else:
            incl_p, tri_c, tri_r = lcol >= row, c64 >= r64, jnp.logical_and(same_half, r128 >= c128)
        gc = gc_ref[...] + brow_ref[...]
        gr = gr_ref[0] + b2_ref[...]
        i_rep = _replicate(gc, (sel_r == d * 2 * HA + sel_h).astype(BF16))
        b_rep = _replicate(_prefix_cols(tri_c.astype(BF16), _log_sigmoid(gc)),
                           (sel_r == d * 2 * HA + HA + sel_h).astype(BF16))
        cs_row = _prefix_rows(_log_sigmoid(gr), tri_r.astype(BF16))
        gate.append((incl_p, gr, cs_row, i_rep, b_rep))

    st = []
    for u in range(4):
        d = u % 2
        c_s, n_s, m_s = states[u // 2]
        for hp in range(HA // 2):
            x_ref = ins[u][0]
            incl_p, gr, cs_row, i_rep, b_rep = gate[u]
            end = CHUNK - 1 if d == 0 else 0
            heads = []
            for h in (2 * hp, 2 * hp + 1):
                c = d * HA + h
                heads.append(dict(c=c, h=h, q=x_ref[:, h * DH:(h + 1) * DH],
                                  k=x_ref[:, 512 + h * DH:512 + (h + 1) * DH] * (DH ** -0.5),
                                  v=x_ref[:, 1024 + h * DH:1024 + (h + 1) * DH],
                                  i=i_rep[:, h * 128:(h + 1) * 128], b=b_rep[:, h * 128:(h + 1) * 128],
                                  m=m_s[c:c + 1, :]))
            ha, hb_ = heads
            ri = d * HA + hp
            rf = d * HA + HA // 2 + hp
            i_row = gr[ri:ri + 1, :]
            b_row = cs_row[rf:rf + 1, :]
            run = running_max(jnp.concatenate([ha["i"] - ha["b"], hb_["i"] - hb_["b"]], axis=1), d)
            for idx, hd in enumerate(heads):
                top = jnp.maximum(hd["m"], run[:, idx * 128:(idx + 1) * 128])
                hd["m_t"] = hd["b"] + top
                hd["w_inter"] = jnp.exp(hd["m"] - top)
            b_p = jnp.where(left, ha["b"], hb_["b"])
            m_t_p = jnp.where(left, ha["m_t"], hb_["m_t"])
            dmat = jnp.where(incl_p, b_p - b_row + i_row, neg)
            q_cat = _bf(jnp.concatenate([ha["q"], hb_["q"]], axis=1))
            k_bd = _bf(_block_diag2(ha["k"], hb_["k"]))
            st.append(dict(u=u, d=d, heads=heads, sc=_mm_nt(q_cat, k_bd) * jnp.exp(dmat - m_t_p)))

    for p in st:
        ha, hb_ = p["heads"]
        h_ref = ins[p["u"]][3]
        c_s, n_s, m_s = states[p["u"] // 2]
        end = CHUNK - 1 if p["d"] == 0 else 0
        sc = p["sc"]
        sv = _mm(_bf(sc), _bf(_block_diag2(ha["v"], hb_["v"])))
        dens = _mm(jnp.concatenate(_split2(sc), axis=1), half_sum)
        for idx, hd in enumerate((ha, hb_)):
            c, h, q, k = hd["c"], hd["h"], hd["q"], hd["k"]
            c_mat = c_s[c]
            n_mat = n_s[c]
            q_cn = _mm(_bf(q), _bf(jnp.concatenate([c_mat, n_mat], axis=1)))
            num = sv[:, idx * DH:(idx + 1) * DH] + hd["w_inter"] * q_cn[:, :DH]
            den = dens[:, idx * 128:(idx + 1) * 128] + hd["w_inter"] * q_cn[:, DH:]
            h_ref[:, h * DH:(h + 1) * DH] = num / jnp.maximum(jnp.abs(den), jnp.exp(-hd["m_t"]))
            b_last = hd["b"][end:end + 1, :]
            g_end = b_last - hd["b"] + hd["i"]
            m_new = jnp.maximum(b_last + hd["m"], jnp.max(g_end, axis=0, keepdims=True))
            kwb = _bf(k * jnp.exp(g_end - m_new))
            decay = jnp.exp(b_last + hd["m"] - m_new)
            upd = _mm_tn(kwb, jnp.concatenate([_bf(hd["v"]), ones_l], axis=1))
            c_s[c] = decay * c_mat + upd[:, :DH]
            n_s[c] = decay * n_mat + upd[:, DH:]
            m_s[c:c + 1, :] = m_new

    @pl.when(jp == P_CHUNKS - 1)
    def _():
        for d in range(2):
            for h in range(HA):
                c_out[0, d, h] = cp_s[d * HA + h]
                n_out[0, d, h:h + 1, :] = np_s[d * HA + h].T[0:1, :]
        m_out[0] = mp_s[...]


def _mlstm_scan(proj, gates_t2, bias_row, bias2, c0, n0, m0, e):
    const2 = lambda s: (0, 0)
    in_specs = (_scan_specs(1)
                + [pl.BlockSpec((1, 128), const2), pl.BlockSpec((16, 128), const2),
                   pl.BlockSpec((1, 1, 2, HA, DH, DH), lambda s: (s // S_CHUNKS, e, 0, 0, 0, 0)),
                   pl.BlockSpec((1, 1, 2, HA, DH), lambda s: (s // S_CHUNKS, e, 0, 0, 0)),
                   pl.BlockSpec((1, 1, 2 * HA, 128), lambda s: (s // S_CHUNKS, e, 0, 0))])
    half = jax.ShapeDtypeStruct((NP_ROWS, HA * DH), F32)
    out_shape = (half, half, half, half,
                 jax.ShapeDtypeStruct((BATCH, 2, HA, DH, DH), F32),
                 jax.ShapeDtypeStruct((BATCH, 2, HA, DH), F32),
                 jax.ShapeDtypeStruct((BATCH, 2 * HA, 128), F32))
    out_specs = _scan_out_specs() + [
        pl.BlockSpec((1, 2, HA, DH, DH), lambda s: (s // P_CHUNKS, 0, 0, 0, 0)),
        pl.BlockSpec((1, 2, HA, DH), lambda s: (s // P_CHUNKS, 0, 0, 0)),
        pl.BlockSpec((1, 2 * HA, 128), lambda s: (s // P_CHUNKS, 0, 0))]
    state = [pltpu.VMEM((2 * HA, DH, DH), F32), pltpu.VMEM((2 * HA, DH, 128), F32), pltpu.VMEM((2 * HA, 128), F32)]
    return pl.pallas_call(
        _mlstm_kernel,
        out_shape=out_shape,
        grid=(SCAN_STEPS,),
        in_specs=in_specs,
        out_specs=out_specs,
        scratch_shapes=state + state,
        compiler_params=_params(1),
        name="mlstm_scan",
    )(proj, proj, proj, proj, proj, proj, proj, proj, gates_t2, gates_t2, gates_t2, gates_t2,
      bias_row, bias2, c0, n0, m0)


def _split2(x):
    hi = _bf(x)
    return hi, _bf(x - hi.astype(F32))


def _split3(x):
    h1 = _bf(x)
    r1 = x - h1.astype(F32)
    h2 = _bf(r1)
    return h1, h2, _bf(r1 - h2.astype(F32))


def _lhs3(x):
    hi, lo = _split2(x)
    return jnp.concatenate([hi, lo, hi], axis=1)


def _rhs3(x):
    hi, lo = _split2(x)
    return jnp.concatenate([hi, hi, lo], axis=0)


def _prefix_cols(tri_bf, x):
    n = x.shape[1]
    r = _mm(tri_bf, jnp.concatenate(_split3(x), axis=1))
    return r[:, :n] + r[:, n:2 * n] + r[:, 2 * n:]


def _replicate(x, sel):
    return _mm(jnp.concatenate(_split3(x), axis=1), jnp.concatenate([sel, sel, sel], axis=0))


def _prefix_rows(x, tri_bf):
    m = x.shape[0]
    r = _mm(jnp.concatenate(_split3(x), axis=0), tri_bf)
    return r[:m] + r[m:2 * m] + r[2 * m:]


LEVELS = tuple(range(6))


def _block_diag2(a, b):
    z = jnp.zeros_like(a)
    return jnp.concatenate([jnp.concatenate([a, z], axis=1), jnp.concatenate([z, b], axis=1)], axis=0)


def _delta_kernel(xpf_ref, xpb_ref, xsf_ref, xsb_ref, gpf_ref, gpb_ref, gsf_ref, gsb_ref,
                  rpf_ref, rpb_ref, rsf_ref, rsb_ref,
                  dtrow_ref, dt2_ref, narow_ref, na2_ref, s0_ref,
                  opf_ref, opb_ref, osf_ref, osb_ref, s_out, sp_s, ss_s):
    s = pl.program_id(0)
    jp = s % P_CHUNKS
    js = s % S_CHUNKS

    @pl.when(jp == 0)
    def _():
        sp_s[...] = jnp.zeros_like(sp_s)

    @pl.when(js == 0)
    def _():
        for d in range(2):
            for h in range(HB):
                ss_s[d * HB + h] = s0_ref[0, 0, d, h]

    states = (sp_s, ss_s)

    row = lax.broadcasted_iota(jnp.int32, (CHUNK, 128), 0)
    lane = lax.broadcasted_iota(jnp.int32, (CHUNK, 128), 1)
    left = lane < CHUNK
    lcol = jnp.where(left, lane, lane - CHUNK)
    eye_p = (lcol == row).astype(F32)
    r64 = lax.broadcasted_iota(jnp.int32, (CHUNK, CHUNK), 0)
    c64 = lax.broadcasted_iota(jnp.int32, (CHUNK, CHUNK), 1)
    r128 = lax.broadcasted_iota(jnp.int32, (128, 128), 0)
    c128 = lax.broadcasted_iota(jnp.int32, (128, 128), 1)
    same_half = (r128 < CHUNK) == (c128 < CHUNK)
    ins = ((xpf_ref, gpf_ref, rpf_ref, opf_ref), (xpb_ref, gpb_ref, rpb_ref, opb_ref),
           (xsf_ref, gsf_ref, rsf_ref, osf_ref), (xsb_ref, gsb_ref, rsb_ref, osb_ref))

    gate = []
    for u in range(4):
        d = u % 2
        gc_ref, gr_ref = ins[u][1], ins[u][2]
        if d == 0:
            incl_p, strict_p = lcol <= row, lcol < row
            tri_c, tri_r = c64 <= r64, jnp.logical_and(same_half, r128 <= c128)
        else:
            incl_p, strict_p = lcol >= row, lcol > row
            tri_c, tri_r = c64 >= r64, jnp.logical_and(same_half, r128 >= c128)
        xc = gc_ref[...]
        xr = gr_ref[0]
        la_c = narow_ref[...] * _softplus(xc + dtrow_ref[...])
        la_r = na2_ref[...] * _softplus(xr + dt2_ref[...])
        g_c = _prefix_cols(tri_c.astype(BF16), la_c)
        g_r = _prefix_rows(la_r, tri_r.astype(BF16))
        gate.append((incl_p, strict_p, _sigmoid(xc), g_c, g_r))

    def bd_rhs(hi, lo, mask=None):
        top = left if mask is None else jnp.logical_and(left, mask)
        bot = jnp.logical_not(left) if mask is None else jnp.logical_and(jnp.logical_not(left), mask)
        zero = jnp.zeros_like(hi)
        blocks = [jnp.concatenate([jnp.where(top, x, zero), jnp.where(bot, x, zero)], axis=0) for x in (hi, lo)]
        return jnp.concatenate([blocks[0], blocks[0], blocks[1]], axis=0)

    off = ([], [])
    for lv in LEVELS:
        same = jnp.right_shift(row, lv + 1) == jnp.right_shift(lcol, lv + 1)
        r_hi = jnp.bitwise_and(jnp.right_shift(row, lv), 1) == 1
        c_hi = jnp.bitwise_and(jnp.right_shift(lcol, lv), 1) == 1
        off[0].append(same & r_hi & jnp.logical_not(c_hi))
        off[1].append(same & jnp.logical_not(r_hi) & c_hi)

    pairs = [(u, hp) for u in range(4) for hp in range(HB // 2)]
    st = []
    for u, hp in pairs:
        d = u % 2
        x_ref = ins[u][0]
        incl_p, strict_p, beta_c, g_c, g_r = gate[u]
        end = CHUNK - 1 if d == 0 else 0
        heads = []
        for h in (2 * hp, 2 * hp + 1):
            ib = BETA_COL0 + d * HB + h
            ia = DECAY_COL0 + d * HB + h
            gcol = g_c[:, ia:ia + 1]
            bc = beta_c[:, ib:ib + 1]
            q = x_ref[:, h * DH:(h + 1) * DH]
            k = x_ref[:, 512 + h * DH:512 + (h + 1) * DH]
            v = x_ref[:, 1024 + h * DH:1024 + (h + 1) * DH]
            heads.append(dict(h=h, gcol=gcol, bc=bc, q=q, k=k, v=v, kb=k * bc, eg=jnp.exp(gcol),
                              g_last=gcol[end:end + 1, :]))
        ha, hb_ = heads
        r = (DECAY_COL0 + d * HB) // 2 + hp
        gcol_p = jnp.where(left, ha["gcol"], hb_["gcol"])
        decay = jnp.exp(jnp.where(incl_p, gcol_p - g_r[r:r + 1, :], -jnp.inf))
        k_bd = _bf(_block_diag2(ha["k"], hb_["k"]))
        kb_cat = _bf(jnp.concatenate([ha["kb"], hb_["kb"]], axis=1))
        q_cat = _bf(jnp.concatenate([ha["q"], hb_["q"]], axis=1))
        kq = _mm_nt(jnp.concatenate([kb_cat, q_cat], axis=0), k_bd)
        a_mat = jnp.where(strict_p, kq[:CHUNK] * decay, 0.0)
        qk = kq[CHUNK:] * decay
        a_hi, a_lo = _split2(a_mat)
        st.append(dict(u=u, d=d, heads=heads, t=eye_p - jnp.where(off[d][0], a_mat, 0.0), qk=qk,
                       am=[bd_rhs(a_hi, a_lo, m) for m in off[d][1:]]))

    for li in range(len(LEVELS) - 1):
        for p in st:
            p["t_parts"] = _split2(p["t"])
            t_hi, t_lo = p["t_parts"]
            p["w"] = _mm(jnp.concatenate([t_hi, t_lo, t_hi], axis=1), p["am"][li])
        for p in st:
            p["t"] = p["t"] - _mm(_lhs3(p["w"]), bd_rhs(*p["t_parts"]))

    for p in st:
        ha, hb_ = p["heads"]
        o_ref = ins[p["u"]][3]
        s_s = states[p["u"] // 2]
        rhs_a = jnp.concatenate([ha["v"] * ha["bc"], ha["kb"] * ha["eg"]], axis=1)
        rhs_b = jnp.concatenate([hb_["v"] * hb_["bc"], hb_["kb"] * hb_["eg"]], axis=1)
        (a_hi, a_lo), (b_hi, b_lo) = _split2(rhs_a), _split2(rhs_b)
        bd_hi, bd_lo = _block_diag2(a_hi, b_hi), _block_diag2(a_lo, b_lo)
        sol = _mm(_lhs3(p["t"]), jnp.concatenate([bd_hi, bd_hi, bd_lo], axis=0))
        vn = []
        for idx, hd in enumerate((ha, hb_)):
            c = p["d"] * HB + hd["h"]
            s_mat = s_s[c]
            sbf = _bf(s_mat)
            so = sol[:, idx * 2 * DH:(idx + 1) * 2 * DH]
            both = _mm(_bf(jnp.concatenate([so[:, DH:], hd["q"] * hd["eg"]], axis=0)), sbf)
            v_new = so[:, :DH] - both[:CHUNK]
            vn.append(v_new)
            hd["o1"] = both[CHUNK:]
            s_s[c] = (jnp.exp(hd["g_last"]) * s_mat
                      + _mm_tn(_bf(hd["k"] * jnp.exp(hd["g_last"] - hd["gcol"])), _bf(v_new)))
        o2 = _mm(_bf(p["qk"]), _bf(_block_diag2(vn[0], vn[1])))
        for idx, hd in enumerate((ha, hb_)):
            o_ref[:, hd["h"] * DH:(hd["h"] + 1) * DH] = hd["o1"] + o2[:, idx * DH:(idx + 1) * DH]

    @pl.when(jp == P_CHUNKS - 1)
    def _():
        for d in range(2):
            for h in range(HB):
                s_out[0, d, h] = sp_s[d * HB + h]


def _delta_scan(qkv, proj, gates_t2, dt_row, dt2, na_row, na2, s0, e):
    const2 = lambda s: (0, 0)
    in_specs = (_scan_specs(0)
                + [pl.BlockSpec((1, 128), const2), pl.BlockSpec((16, 128), const2),
                   pl.BlockSpec((1, 128), const2), pl.BlockSpec((16, 128), const2),
                   pl.BlockSpec((1, 1, 2, HB, DH, DH), lambda s: (s // S_CHUNKS, e, 0, 0, 0, 0))])
    half = jax.ShapeDtypeStruct((NP_ROWS, HB * DH), F32)
    out_shape = (half, half, half, half, jax.ShapeDtypeStruct((BATCH, 2, HB, DH, DH), F32))
    out_specs = _scan_out_specs() + [pl.BlockSpec((1, 2, HB, DH, DH), lambda s: (s // P_CHUNKS, 0, 0, 0, 0))]
    return pl.pallas_call(
        _delta_kernel,
        out_shape=out_shape,
        grid=(SCAN_STEPS,),
        in_specs=in_specs,
        out_specs=out_specs,
        scratch_shapes=[pltpu.VMEM((2 * HB, DH, DH), F32), pltpu.VMEM((2 * HB, DH, DH), F32)],
        compiler_params=_params(1),
        name="delta_scan",
    )(qkv, qkv, qkv, qkv, proj, proj, proj, proj, gates_t2, gates_t2, gates_t2, gates_t2,
      dt_row, dt2, na_row, na2, s0)


def _residual_ln(x, gate, y, g, b):
    r = DN_ALPHA * x + gate * y
    mu = jnp.mean(r, axis=-1, keepdims=True)
    var = jnp.mean(jnp.square(r - mu), axis=-1, keepdims=True)
    return (r - mu) * lax.rsqrt(var + LN_EPS) * g + b


def _even_out_kernel(*refs, tm, n_x):
    x_refs = refs[:n_x]
    (hpf_ref, hpb_ref, hsf_ref, hsb_ref, opf_ref, opb_ref, osf_ref, osb_ref,
     oa_ref, zb_ref, mg_ref, dg_ref, w_ref, gate_ref, g_ref, b_ref, o_ref) = refs[n_x:]
    is_p = pl.program_id(0) < NP_ROWS // tm
    hf_ref, hb_ref, of_ref, ob_ref = (
        lambda sl, p=p, q=q: jnp.where(is_p, p[:, sl], q[:, sl])
        for p, q in ((hpf_ref, hsf_ref), (hpb_ref, hsb_ref), (opf_ref, osf_ref), (opb_ref, osb_ref)))
    parts = []
    for h in range(HA):
        sl = slice(h * DH, (h + 1) * DH)
        hh = hf_ref(sl) + hb_ref(sl)
        mu = jnp.mean(hh, axis=-1, keepdims=True)
        var = jnp.mean(jnp.square(hh - mu), axis=-1, keepdims=True)
        parts.append(_sigmoid(oa_ref[:, sl]) * ((hh - mu) * lax.rsqrt(var + LN_EPS) * mg_ref[:, sl]))
    for h in range(HB):
        sl = slice(h * DH, (h + 1) * DH)
        oo = of_ref(sl) + ob_ref(sl)
        z = zb_ref[:, sl]
        nrm = oo * lax.rsqrt(jnp.mean(jnp.square(oo), axis=-1, keepdims=True) + LN_EPS) * dg_ref[:, sl]
        parts.append(nrm * (z * _sigmoid(z)))
    a = jnp.concatenate(parts, axis=1)
    y = _mm(_bf(a), w_ref[...])
    o_ref[...] = _residual_ln(_load_x(x_refs, tm), gate_ref[0], y, g_ref[...], b_ref[...])


def _even_out(h_scan, o_scan, proj, mg, dg, w, x, mods, layer, ln_g, ln_b):
    tm = 512
    npt = NP_ROWS // tm
    row512 = lambda i: (i, 0)
    const2 = lambda i: (0, 0)
    p_spec = pl.BlockSpec((tm, 512), lambda i: (jnp.minimum(i, npt - 1), 0))
    s_spec = pl.BlockSpec((tm, 512), lambda i: (jnp.maximum(i - npt, 0), 0))
    x_specs, x_arrays = _x_operands(x, tm)
    return pl.pallas_call(
        functools.partial(_even_out_kernel, tm=tm, n_x=len(x_arrays)),
        out_shape=jax.ShapeDtypeStruct((T, D), F32),
        grid=(T // tm,),
        in_specs=x_specs + [p_spec, p_spec, s_spec, s_spec, p_spec, p_spec, s_spec, s_spec,
                            pl.BlockSpec((tm, 512), lambda i: (i, 6)),
                            pl.BlockSpec((tm, 512), lambda i: (i, 7)),
                            pl.BlockSpec((1, 512), const2), pl.BlockSpec((1, 512), const2),
                            pl.BlockSpec((D, D), const2),
                            _mod_spec(layer, 2, tm),
                            pl.BlockSpec((1, D), const2), pl.BlockSpec((1, D), const2)],
        out_specs=pl.BlockSpec((tm, D), row512),
        compiler_params=_params(1),
        name="even_out_ln",
    )(*x_arrays, *h_scan, *o_scan, proj, proj, mg, dg, w, mods, ln_g, ln_b)


def _odd_out_kernel(ap_ref, as_ref, w_ref, x_ref, gate_ref, g_ref, b_ref, o_ref, *, tm):
    a = jnp.where(pl.program_id(0) < NP_ROWS // tm, ap_ref[...], as_ref[...])
    y = _mm(_bf(a), w_ref[...])
    o_ref[...] = _residual_ln(x_ref[...], gate_ref[0], y, g_ref[...], b_ref[...])


def _odd_out(a_prompt, a_latent, w, x, mods, layer, ln_g, ln_b):
    tm = 512
    npt = NP_ROWS // tm
    row = lambda i: (i, 0)
    const2 = lambda i: (0, 0)
    return pl.pallas_call(
        functools.partial(_odd_out_kernel, tm=tm),
        out_shape=jax.ShapeDtypeStruct((T, D), F32),
        grid=(T // tm,),
        in_specs=[pl.BlockSpec((tm, D), lambda i: (jnp.minimum(i, npt - 1), 0)),
                  pl.BlockSpec((tm, D), lambda i: (jnp.maximum(i - npt, 0), 0)),
                  pl.BlockSpec((D, D), const2), pl.BlockSpec((tm, D), row),
                  _mod_spec(layer, 2, tm), pl.BlockSpec((1, D), const2), pl.BlockSpec((1, D), const2)],
        out_specs=pl.BlockSpec((tm, D), row),
        compiler_params=_params(1),
        name="odd_out_ln",
    )(a_prompt, a_latent, w, x, mods, ln_g, ln_b)


def _attn_ctx_kernel(qkv_ref, sink_ref, o_ref):
    ones = jnp.ones((SEQ, HD), BF16)
    for kv in range(KVH):
        k = _bf(qkv_ref[:, HC * HD + kv * HD:HC * HD + (kv + 1) * HD])
        v = _bf(qkv_ref[:, (HC + KVH) * HD + kv * HD:(HC + KVH) * HD + (kv + 1) * HD])
        v1 = jnp.concatenate([v, ones], axis=1)
        heads = range(kv * (HC // KVH), (kv + 1) * (HC // KVH))
        sts = [_mm_nt(k, _bf(qkv_ref[:, h * HD:(h + 1) * HD])) for h in heads]
        ms = [jnp.maximum(jnp.max(st, axis=0, keepdims=True), sink_ref[h]) for st, h in zip(sts, heads)]
        ovs = [_mm_tn(v1, _bf(jnp.exp(st - m))) for st, m in zip(sts, ms)]
        outs = [ov[:HD] / (ov[HD:HD + 1] + jnp.exp(sink_ref[h] - m)) for ov, m, h in zip(ovs, ms, heads)]
        for pr in range(HC // KVH // 2):
            lane0 = (kv * (HC // KVH) + 2 * pr) * HD
            o_ref[:, lane0:lane0 + 2 * HD] = jnp.concatenate(outs[2 * pr:2 * pr + 2], axis=0).T


def _attn_context(qkv, sink):
    return pl.pallas_call(
        _attn_ctx_kernel,
        out_shape=jax.ShapeDtypeStruct((NP_ROWS, HC * HD), F32),
        grid=(BATCH,),
        in_specs=[pl.BlockSpec((SEQ, QKV_W), lambda b: (b, 0)),
                  pl.BlockSpec(memory_space=pltpu.SMEM)],
        out_specs=pl.BlockSpec((SEQ, HC * HD), lambda b: (b, 0)),
        compiler_params=_params(1),
        name="attn_context",
    )(qkv, sink)


def _attn_lat_kernel(q_ref, kp_ref, kc_ref, kn_ref, vp_ref, vc_ref, vn_ref, ck_ref, cv_ref,
                     sink_ref, o_ref, bias_s):
    j = pl.program_id(1)
    nb = DEC_SEQ // QBLOCK
    cc = lax.broadcasted_iota(jnp.int32, (3 * QBLOCK, QBLOCK), 0)
    r = lax.broadcasted_iota(jnp.int32, (3 * QBLOCK, QBLOCK), 1)
    lo = jnp.where(j >= 1, 0, QBLOCK)
    hi = jnp.where(j <= nb - 2, 3 * QBLOCK, 2 * QBLOCK)
    ok = (jnp.abs(QBLOCK + r - cc) <= WINDOW) & (cc >= lo) & (cc < hi)
    bias_s[...] = jnp.where(ok, 0.0, -jnp.inf)
    for kv in range(KVH):
        ks = slice(kv * HD, (kv + 1) * HD)
        k_all = _bf(jnp.concatenate([ck_ref[0, 0, kv], kp_ref[:, ks], kc_ref[:, ks], kn_ref[:, ks]], axis=0))
        v_all = _bf(jnp.concatenate([cv_ref[0, 0, kv], vp_ref[:, ks], vc_ref[:, ks], vn_ref[:, ks]], axis=0))
        v1 = jnp.concatenate([v_all, jnp.ones_like(v_all)], axis=1)
        outs = []
        for h in range(kv * (HC // KVH), (kv + 1) * (HC // KVH)):
            sink = sink_ref[h]
            st = _mm_nt(k_all, _bf(q_ref[:, h * HD:(h + 1) * HD]))
            s_ctx = st[:PAST_LEN]
            s_loc = st[PAST_LEN:] + bias_s[...]
            m = jnp.maximum(jnp.maximum(jnp.max(s_ctx, axis=0, keepdims=True),
                                        jnp.max(s_loc, axis=0, keepdims=True)), sink)
            p = _bf(jnp.concatenate([jnp.exp(s_ctx - m), jnp.exp(s_loc - m)], axis=0))
            ov = _mm_tn(v1, p)
            outs.append(ov[:HD] / (ov[HD:HD + 1] + jnp.exp(sink - m)))
        for pr in range(HC // KVH // 2):
            lane0 = (kv * (HC // KVH) + 2 * pr) * HD
            o_ref[:, lane0:lane0 + 2 * HD] = jnp.concatenate(outs[2 * pr:2 * pr + 2], axis=0).T


def _attn_latent(qkv, cache_k, cache_v, sink, o):
    nb = DEC_SEQ // QBLOCK
    base = NP_ROWS // QBLOCK
    blk = lambda b, j: base + b * nb + j
    prev = lambda b, j: base + b * nb + jnp.maximum(j - 1, 0)
    nxt = lambda b, j: base + b * nb + jnp.minimum(j + 1, nb - 1)
    kcol, vcol = HC * HD // 256, HC * HD // 256 + 1
    cache_spec = pl.BlockSpec((1, 1, KVH, PAST_LEN, HD), lambda b, j: (b, o, 0, 0, 0))
    return pl.pallas_call(
        _attn_lat_kernel,
        out_shape=jax.ShapeDtypeStruct((NS_ROWS, HC * HD), F32),
        grid=(DEC_BATCH, nb),
        in_specs=[pl.BlockSpec((QBLOCK, HC * HD), lambda b, j: (blk(b, j), 0)),
                  pl.BlockSpec((QBLOCK, 256), lambda b, j: (prev(b, j), kcol)),
                  pl.BlockSpec((QBLOCK, 256), lambda b, j: (blk(b, j), kcol)),
                  pl.BlockSpec((QBLOCK, 256), lambda b, j: (nxt(b, j), kcol)),
                  pl.BlockSpec((QBLOCK, 256), lambda b, j: (prev(b, j), vcol)),
                  pl.BlockSpec((QBLOCK, 256), lambda b, j: (blk(b, j), vcol)),
                  pl.BlockSpec((QBLOCK, 256), lambda b, j: (nxt(b, j), vcol)),
                  cache_spec, cache_spec,
                  pl.BlockSpec(memory_space=pltpu.SMEM)],
        out_specs=pl.BlockSpec((QBLOCK, HC * HD), lambda b, j: (b * nb + j, 0)),
        scratch_shapes=[pltpu.VMEM((3 * QBLOCK, QBLOCK), F32)],
        compiler_params=_params(2),
        name="attn_latent",
    )(qkv, qkv, qkv, qkv, qkv, qkv, qkv, cache_k, cache_v, sink)


def _route(lg):
    lane = lax.broadcasted_iota(jnp.int32, lg.shape, 1)
    neg = -jnp.inf
    big = 1 << 20
    is_grp = jnp.logical_and(lane >= N_EXPERTS, lane < N_EXPERTS + N_GROUPS)
    mg = jnp.max(jnp.where(is_grp, lg, neg), axis=1, keepdims=True)
    g_lane = jnp.min(jnp.where(jnp.logical_and(is_grp, lg == mg), lane, big), axis=1, keepdims=True)
    g_w = 1.0 / jnp.sum(jnp.where(is_grp, jnp.exp(lg - mg), 0.0), axis=1, keepdims=True)
    g_idx = g_lane - N_EXPERTS
    in_grp = jnp.logical_and(lane >= g_idx * EPG, lane < (g_idx + 1) * EPG)
    v1 = jnp.max(jnp.where(in_grp, lg, neg), axis=1, keepdims=True)
    i1 = jnp.min(jnp.where(jnp.logical_and(in_grp, lg == v1), lane, big), axis=1, keepdims=True)
    rest = jnp.logical_and(in_grp, lane != i1)
    v2 = jnp.max(jnp.where(rest, lg, neg), axis=1, keepdims=True)
    i2 = jnp.min(jnp.where(jnp.logical_and(rest, lg == v2), lane, big), axis=1, keepdims=True)
    e2 = jnp.exp(v2 - v1)
    p1 = 1.0 / (1.0 + e2)
    p2 = e2 / (1.0 + e2)
    return jnp.where(lane == 0, i1.astype(F32),
                     jnp.where(lane == 1, i2.astype(F32),
                               jnp.where(lane == 2, p1 * g_w, jnp.where(lane == 3, p2 * g_w, 0.0))))


def _moe_dense_kernel(x_ref, sh_ref, sc_ref, wr_ref, br_ref, wg_ref, wu_ref, wd_ref, gate_ref, g_ref, b_ref,
                      *out_and_scratch, tm):
    *o_refs, xm_s, meta_s, acc_s = out_and_scratch
    grp = pl.program_id(1)

    @pl.when(grp == 0)
    def _():
        xm = x_ref[...] * (1.0 + sc_ref[0]) + sh_ref[0]
        meta_s[...] = _route(_mm(_lhs3(xm), _rhs3(wr_ref[...])) + br_ref[...])
        xm_s[...] = _bf(xm)
        acc_s[...] = jnp.zeros_like(acc_s)

    xm = xm_s[...]
    meta = meta_s[...]
    i1 = meta[:, 0:1].astype(jnp.int32)
    i2 = meta[:, 1:2].astype(jnp.int32)
    w1 = meta[:, 2:3]
    w2 = meta[:, 3:4]
    hid = []
    for e in range(EPG):
        eid = grp * EPG + e
        gate = jnp.where(i1 == eid, w1, 0.0) + jnp.where(i2 == eid, w2, 0.0)
        a = _mm(xm, wg_ref[0, e])
        u = _mm(xm, wu_ref[0, e])
        hid.append(_bf((a * _sigmoid(a)) * u * gate))
    acc_s[...] += _mm(jnp.concatenate(hid, axis=1), wd_ref[0])

    def result():
        return _residual_ln(x_ref[...], gate_ref[0], acc_s[...], g_ref[...], b_ref[...])

    last = grp == N_GROUPS - 1
    if len(o_refs) == 1:
        @pl.when(last)
        def _():
            o_refs[0][...] = result()
    else:
        is_p = pl.program_id(0) < NP_ROWS // tm

        @pl.when(jnp.logical_and(last, is_p))
        def _():
            o_refs[0][...] = result()

        @pl.when(jnp.logical_and(last, jnp.logical_not(is_p)))
        def _():
            o_refs[1][...] = result()


def _moe_dense(x, mods, layer, w_r, b_r, wg, wu, wd, ln_g, ln_b, split_out=False):
    tm = 1024
    npt = NP_ROWS // tm
    row = lambda i, g: (i, 0)
    const2 = lambda i, g: (0, 0)
    if split_out:
        half = jax.ShapeDtypeStruct((NP_ROWS, D), F32)
        out_shape = (half, half)
        out_specs = (pl.BlockSpec((tm, D), lambda i, g: (jnp.minimum(i, npt - 1), 0)),
                     pl.BlockSpec((tm, D), lambda i, g: (jnp.maximum(i - npt, 0), 0)))
    else:
        out_shape = jax.ShapeDtypeStruct((T, D), F32)
        out_specs = pl.BlockSpec((tm, D), row)
    return pl.pallas_call(
        functools.partial(_moe_dense_kernel, tm=tm),
        out_shape=out_shape,
        grid=(T // tm, N_GROUPS),
        in_specs=[pl.BlockSpec((tm, D), row), _mod_spec(layer, 3, tm), _mod_spec(layer, 4, tm),
                  pl.BlockSpec((D, 128), const2), pl.BlockSpec((1, 128), const2),
                  pl.BlockSpec((1, EPG, D, EXPERT_FF), lambda i, g: (g, 0, 0, 0)),
                  pl.BlockSpec((1, EPG, D, EXPERT_FF), lambda i, g: (g, 0, 0, 0)),
                  pl.BlockSpec((1, EPG * EXPERT_FF, D), lambda i, g: (g, 0, 0)),
                  _mod_spec(layer, 5, tm), pl.BlockSpec((1, D), const2), pl.BlockSpec((1, D), const2)],
        out_specs=out_specs,
        scratch_shapes=[pltpu.VMEM((tm, D), BF16), pltpu.VMEM((tm, 128), F32), pltpu.VMEM((tm, D), F32)],
        compiler_params=_params(2, VMEM_LIMIT + (2 * tm * D * 4 if split_out else 0)),
        name="moe_dense",
    )(x, mods, mods, w_r, b_r, wg, wu, wd, mods, ln_g, ln_b)


def _permute_even_w(w):
    a_end = 4 * HA * DH
    g_end = a_end + 4 * HA
    c_end = g_end + 3 * HB * DH
    z_end = c_end + HB * DH
    small = jnp.concatenate([w[:, a_end:g_end], w[:, z_end:]], axis=1)
    pad = jnp.zeros((w.shape[0], 128 - small.shape[1]), w.dtype)
    return jnp.concatenate([w[:, g_end:c_end], w[:, :a_end], w[:, c_end:z_end], small, pad], axis=1)


def _lane_row(vals, offset):
    return jnp.zeros((1, 128), F32).at[0, offset:offset + vals.shape[0]].set(vals.astype(F32))


def _pair_rows(vals, offset):
    cols = jnp.zeros((N_GATE_COLS,), F32).at[offset:offset + vals.shape[0]].set(vals.astype(F32))
    return jnp.repeat(cols.reshape(N_GATE_COLS // 2, 2), CHUNK, axis=1)


def kernel(x_prompt, x_sample, c, c_ctx, state_mlstm_c, state_mlstm_n, state_mlstm_m, state_delta, cache_k, cache_v, w_mod, b_mod, ln_g, ln_b, w_in_even, mlstm_gate_b, mlstm_norm_g, delta_conv_w, delta_a_log, delta_dt_bias, delta_norm_g, w_out_even, w_qkv_odd, attn_sink, w_out_odd, w_grp, b_grp, w_erouter, b_erouter, w_gate, w_up, w_down):
    x = (x_prompt.reshape(NP_ROWS, D), x_sample.reshape(NS_ROWS, D))
    cvecs = jnp.concatenate([c_ctx[None, :], c, jnp.zeros((N_MOD_ROWS - 1 - DEC_BATCH, D), F32)], axis=0)
    mods = _modulation(cvecs, w_mod, b_mod)
    tables = _rope_tables()
    m0_all = jnp.broadcast_to(state_mlstm_m.reshape(DEC_BATCH, N_EVEN, 2 * HA, 1), (DEC_BATCH, N_EVEN, 2 * HA, 128))

    out_mc, out_mn, out_mm, out_ds = [], [], [], []
    new_k = jnp.zeros((BATCH, N_ODD, KVH, SEQ, HD), F32)
    new_v = jnp.zeros((BATCH, N_ODD, KVH, SEQ, HD), F32)
    for l in range(DEPTH):
        if l % 2 == 0:
            e = l // 2
            proj = _even_proj(x, mods, l, _bf(_permute_even_w(w_in_even[e])))
            gates_t2 = (proj[:, EVEN_W - 128:EVEN_W - 128 + N_GATE_COLS]
                        .reshape(T // CHUNK, CHUNK, N_GATE_COLS).transpose(0, 2, 1)
                        .reshape(T // CHUNK, N_GATE_COLS // 2, 2 * CHUNK))
            gb = mlstm_gate_b[e].reshape(-1)
            *h_scan, mc, mn, mm = _mlstm_scan(proj, gates_t2, _lane_row(gb, 0), _pair_rows(gb, 0),
                                              state_mlstm_c, state_mlstm_n, m0_all, e)
            qkv = _delta_prep(proj, delta_conv_w[e])
            dtb = delta_dt_bias[e].reshape(-1)
            nea = -jnp.exp(delta_a_log[e].astype(F32)).reshape(-1)
            *o_scan, ds = _delta_scan(qkv, proj, gates_t2, _lane_row(dtb, DECAY_COL0), _pair_rows(dtb, DECAY_COL0),
                                      _lane_row(nea, DECAY_COL0), _pair_rows(nea, DECAY_COL0), state_delta, e)
            x = _even_out(h_scan, o_scan, proj, mlstm_norm_g[e][None, :], delta_norm_g[e][None, :],
                          _bf(w_out_even[e]), x, mods, l, ln_g[l, 0][None, :], ln_b[l, 0][None, :])
            out_mc.append(mc)
            out_mn.append(mn)
            out_mm.append(mm[:, :, 0].reshape(BATCH, 2, HA))
            out_ds.append(ds)
        else:
            o = l // 2
            qkv, new_k, new_v = _odd_proj(x, mods, l, _bf(w_qkv_odd[o]), tables, new_k, new_v, o)
            a_p = _attn_context(qkv, attn_sink[o])
            a_s = _attn_latent(qkv, cache_k, cache_v, attn_sink[o], o)
            x = _odd_out(a_p, a_s, _bf(w_out_odd[o]), x, mods, l, ln_g[l, 0][None, :], ln_b[l, 0][None, :])
        w_r = jnp.concatenate([w_erouter[l].transpose(1, 0, 2).reshape(D, N_EXPERTS), w_grp[l],
                               jnp.zeros((D, 128 - N_EXPERTS - N_GROUPS), F32)], axis=1)
        b_r = jnp.concatenate([b_erouter[l].reshape(-1), b_grp[l],
                               jnp.zeros((128 - N_EXPERTS - N_GROUPS,), F32)])[None, :]
        x = _moe_dense(x, mods, l, w_r, b_r,
                       _bf(w_gate[l]).reshape(N_GROUPS, EPG, D, EXPERT_FF),
                       _bf(w_up[l]).reshape(N_GROUPS, EPG, D, EXPERT_FF),
                       _bf(w_down[l]).reshape(N_GROUPS, EPG * EXPERT_FF, D),
                       ln_g[l, 1][None, :], ln_b[l, 1][None, :], split_out=(l == DEPTH - 1))
    return (x[0].reshape(BATCH, SEQ, D), x[1].reshape(DEC_BATCH, DEC_SEQ, D),
            jnp.stack(out_mc, 1), jnp.stack(out_mn, 1), jnp.stack(out_mm, 1), jnp.stack(out_ds, 1),
            new_k, new_v)
```

```python
import functools

import numpy as np
import jax
import jax.numpy as jnp
from jax import lax
from jax.experimental import pallas as pl
from jax.experimental.pallas import tpu as pltpu

F32 = jnp.float32
BF16 = jnp.bfloat16

D = 1024
BATCH = 32
SEQ = 256
DEPTH = 4
DEC_BATCH = 2
DEC_SEQ = 4096
PAST_LEN = 512
GRID_W = 64
N_EVEN = 2
N_ODD = 2
HA = 4
HB = 4
DH = 128
CHUNK = 64
HC = 16
KVH = 4
HD = 64
WINDOW = 128
QBLOCK = 128
ROPE_THETA = 10000.0
N_GROUPS = 4
EPG = 4
N_EXPERTS = 16
EXPERT_FF = 256
DN_ALPHA = (2 * DEPTH) ** 0.25
LN_EPS = 1e-5

NP_ROWS = BATCH * SEQ
NS_ROWS = DEC_BATCH * DEC_SEQ
T = NP_ROWS + NS_ROWS
N_MOD_ROWS = 8
EVEN_W = 4224
N_GATE_COLS = 4 * HA + 4 * HB
BETA_COL0 = 4 * HA
DECAY_COL0 = 4 * HA + 2 * HB
QKV_W = (HC + 2 * KVH) * HD

P_CHUNKS = SEQ // CHUNK
S_CHUNKS = DEC_SEQ // CHUNK
P_STEPS = BATCH * P_CHUNKS
S_STEPS = DEC_BATCH * S_CHUNKS
P_BLOCKS = NP_ROWS // CHUNK

VMEM_LIMIT = 48 * 1024 * 1024


def _params(n_axes, vmem_limit=VMEM_LIMIT):
    return pltpu.CompilerParams(dimension_semantics=("arbitrary",) * n_axes,
                                vmem_limit_bytes=vmem_limit)


def _mm(a, b):
    return lax.dot_general(a, b, (((1,), (0,)), ((), ())), preferred_element_type=F32)


def _mm_nt(a, b):
    return lax.dot_general(a, b, (((1,), (1,)), ((), ())), preferred_element_type=F32)


def _mm_tn(a, b):
    return lax.dot_general(a, b, (((0,), (0,)), ((), ())), preferred_element_type=F32)


def _bf(x):
    return x.astype(BF16)


def _sigmoid(x):
    return 1.0 / (1.0 + jnp.exp(-x))


def _softplus(x):
    return jnp.maximum(x, 0.0) + jnp.log1p(jnp.exp(-jnp.abs(x)))


def _log_sigmoid(x):
    return jnp.minimum(x, 0.0) - jnp.log1p(jnp.exp(-jnp.abs(x)))


def _mod_row(tile, tm):
    npt = NP_ROWS // tm
    per = DEC_SEQ // tm
    return jnp.where(tile < npt, 0, 1 + (tile - npt) // per)


def _mod_spec(layer, chunk, tm):
    def imap(i, *_):
        return ((layer * N_MOD_ROWS + _mod_row(i, tm)) * 6 + chunk, 0, 0)
    return pl.BlockSpec((1, 1, D), imap)


def _modulation_kernel(c_ref, w_ref, b_ref, o_ref):
    x = c_ref[...]
    s = x * _sigmoid(x)
    o_ref[0] = _mm(_lhs3(s), _rhs3(w_ref[0])) + b_ref[0]


def _modulation(cvecs, w_mod, b_mod):
    out = pl.pallas_call(
        _modulation_kernel,
        out_shape=jax.ShapeDtypeStruct((DEPTH, N_MOD_ROWS, 6 * D), F32),
        grid=(DEPTH, 6),
        in_specs=[pl.BlockSpec((N_MOD_ROWS, D), lambda l, j: (0, 0)),
                  pl.BlockSpec((1, D, D), lambda l, j: (l, 0, j)),
                  pl.BlockSpec((1, 1, D), lambda l, j: (l * 6 + j, 0, 0))],
        out_specs=pl.BlockSpec((1, N_MOD_ROWS, D), lambda l, j: (l, 0, j)),
        compiler_params=_params(2),
        name="modulation",
    )(cvecs, w_mod, b_mod.reshape(DEPTH * 6, 1, D))
    return out.reshape(DEPTH * N_MOD_ROWS * 6, 1, D)


def _x_operands(x, tm):
    if isinstance(x, tuple):
        npt = NP_ROWS // tm
        return ([pl.BlockSpec((tm, D), lambda i, *_: (jnp.minimum(i, npt - 1), 0)),
                 pl.BlockSpec((tm, D), lambda i, *_: (jnp.maximum(i - npt, 0), 0))], list(x))
    return [pl.BlockSpec((tm, D), lambda i, *_: (i, 0))], [x]


def _load_x(x_refs, tm):
    if len(x_refs) == 1:
        return x_refs[0][...]
    return jnp.where(pl.program_id(0) < NP_ROWS // tm, x_refs[0][...], x_refs[1][...])


def _proj_kernel(*refs, tm, n_x):
    x_refs, (sh_ref, sc_ref, w_ref, o_ref) = refs[:n_x], refs[n_x:]
    xm = _load_x(x_refs, tm) * (1.0 + sc_ref[0]) + sh_ref[0]
    o_ref[...] = _mm(_bf(xm), w_ref[...])


def _even_proj(x, mods, layer, w):
    tm = 512
    x_specs, x_arrays = _x_operands(x, tm)
    return pl.pallas_call(
        functools.partial(_proj_kernel, tm=tm, n_x=len(x_arrays)),
        out_shape=jax.ShapeDtypeStruct((T, EVEN_W), F32),
        grid=(T // tm,),
        in_specs=x_specs + [_mod_spec(layer, 0, tm), _mod_spec(layer, 1, tm),
                            pl.BlockSpec((D, EVEN_W), lambda i: (0, 0), pipeline_mode=pl.Buffered(1))],
        out_specs=pl.BlockSpec((tm, EVEN_W), lambda i: (i, 0)),
        compiler_params=_params(1),
        name="even_proj",
    )(*x_arrays, mods, mods, w)


def _qkv_kernel(x_ref, sh_ref, sc_ref, w_ref, cos_ref, sa_ref, sb_ref, kc_in, vc_in, o_ref, kc_ref, vc_ref, *, tm):
    del kc_in, vc_in
    i = pl.program_id(0)
    xm = x_ref[...] * (1.0 + sc_ref[0]) + sh_ref[0]
    acc = _mm(_bf(xm), w_ref[...])

    @pl.when(i < NP_ROWS // tm)
    def _():
        for b in range(tm // SEQ):
            for kv in range(KVH):
                k0 = HC * HD + kv * HD
                v0 = (HC + KVH) * HD + kv * HD
                kc_ref[b, 0, kv] = acc[b * SEQ:(b + 1) * SEQ, k0:k0 + HD]
                vc_ref[b, 0, kv] = acc[b * SEQ:(b + 1) * SEQ, v0:v0 + HD]

    n_q = HC * HD // 128
    n_k = KVH * HD // 128
    is_latent = i >= NP_ROWS // tm

    def scaled(g):
        blk = acc[:, g * 128:(g + 1) * 128]
        return blk * (HD ** -0.5) if g < n_q else blk

    @pl.when(is_latent)
    def _():
        cos, sa, sb = cos_ref[...], sa_ref[...], sb_ref[...]
        for g in range(n_q + n_k):
            blk = scaled(g)
            o_ref[:, g * 128:(g + 1) * 128] = blk * cos + pltpu.roll(blk, 112, 1) * sa + pltpu.roll(blk, 16, 1) * sb

    @pl.when(jnp.logical_not(is_latent))
    def _():
        for g in range(n_q + n_k):
            o_ref[:, g * 128:(g + 1) * 128] = scaled(g)

    o_ref[:, (n_q + n_k) * 128:] = acc[:, (n_q + n_k) * 128:]


def _rope_tables():
    half = HD // 4
    inv = np.float32(ROPE_THETA) ** (-np.arange(half, dtype=np.float32) / np.float32(half))
    pos = np.arange(DEC_SEQ)
    row = (pos // GRID_W).astype(np.float32)[:, None] * inv[None, :]
    col = (pos % GRID_W).astype(np.float32)[:, None] * inv[None, :]
    cos = np.concatenate([np.cos(row), np.cos(row), np.cos(col), np.cos(col)], axis=-1)
    sin = np.concatenate([np.sin(row), np.sin(row), np.sin(col), np.sin(col)], axis=-1)
    first = (np.arange(HD) % 32) < 16
    sa = np.where(first, -sin, 0.0)
    sb = np.where(first, 0.0, sin)
    tile2 = lambda t: jnp.asarray(np.concatenate([t, t], axis=-1), F32)
    return tile2(cos), tile2(sa), tile2(sb)


def _odd_proj(x, mods, layer, w, tables, cache_k, cache_v, o):
    tm = 512
    npt = NP_ROWS // tm
    per = DEC_SEQ // tm
    tab_spec = pl.BlockSpec((tm, 128), lambda i: (jnp.where(i < npt, 0, (i - npt) % per), 0))
    cache_spec = pl.BlockSpec((tm // SEQ, 1, KVH, SEQ, HD), lambda i: (jnp.minimum(i, npt - 1), o, 0, 0, 0))
    cache_shape = jax.ShapeDtypeStruct((BATCH, N_ODD, KVH, SEQ, HD), F32)
    return pl.pallas_call(
        functools.partial(_qkv_kernel, tm=tm),
        out_shape=(jax.ShapeDtypeStruct((T, QKV_W), F32), cache_shape, cache_shape),
        grid=(T // tm,),
        in_specs=[pl.BlockSpec((tm, D), lambda i: (i, 0)),
                  _mod_spec(layer, 0, tm), _mod_spec(layer, 1, tm),
                  pl.BlockSpec((D, QKV_W), lambda i: (0, 0)),
                  tab_spec, tab_spec, tab_spec,
                  pl.BlockSpec(memory_space=pl.ANY), pl.BlockSpec(memory_space=pl.ANY)],
        out_specs=(pl.BlockSpec((tm, QKV_W), lambda i: (i, 0)), cache_spec, cache_spec),
        input_output_aliases={7: 1, 8: 2},
        compiler_params=_params(1),
        name="odd_qkv_proj",
    )(x, mods, mods, w, *tables, cache_k, cache_v)


PREP_ROWS = 256


def _delta_prep_kernel(x_ref, prev_ref, next_ref, w_ref, o_ref):
    i = pl.program_id(0)
    npb = NP_ROWS // PREP_ROWS
    per = DEC_SEQ // PREP_ROWS
    is_latent = i >= npb
    pos = (i - npb) % per
    has_prev = jnp.logical_and(is_latent, pos > 0)
    has_next = jnp.logical_and(is_latent, pos < per - 1)
    x = x_ref[...]
    w = w_ref[...]
    rows = lax.broadcasted_iota(jnp.int32, x.shape, 0)
    prev_row = jnp.where(has_prev, prev_ref[7:8, :], 0.0)
    next_row = jnp.where(has_next, next_ref[0:1, :], 0.0)
    xm1 = jnp.where(rows == 0, prev_row, pltpu.roll(x, 1, 0))
    xp1 = jnp.where(rows == PREP_ROWS - 1, next_row, pltpu.roll(x, PREP_ROWS - 1, 0))
    y = xm1 * w[0:1, :] + x * w[1:2, :] + xp1 * w[2:3, :]
    y = y * _sigmoid(y)
    for h in range(3 * HB):
        yh = y[:, h * DH:(h + 1) * DH]
        if h < 2 * HB:
            inv = lax.rsqrt(jnp.sum(yh * yh, axis=-1, keepdims=True) + 1e-6)
            yh = yh * (inv * (DH ** -0.5) if h < HB else inv)
        o_ref[:, h * DH:(h + 1) * DH] = yh


def _delta_prep(proj, conv_w):
    nblk = T // PREP_ROWS
    sub = PREP_ROWS // 8
    last8 = T // 8 - 1
    return pl.pallas_call(
        _delta_prep_kernel,
        out_shape=jax.ShapeDtypeStruct((T, 3 * 512), F32),
        grid=(nblk,),
        in_specs=[pl.BlockSpec((PREP_ROWS, 3 * 512), lambda i: (i, 0)),
                  pl.BlockSpec((8, 3 * 512), lambda i: (jnp.maximum(i * sub - 1, 0), 0)),
                  pl.BlockSpec((8, 3 * 512), lambda i: (jnp.minimum((i + 1) * sub, last8), 0)),
                  pl.BlockSpec((3, 3 * 512), lambda i: (0, 0))],
        out_specs=pl.BlockSpec((PREP_ROWS, 3 * 512), lambda i: (i, 0)),
        compiler_params=_params(1),
        name="delta_prep",
    )(proj, proj, proj, conv_w)


assert P_STEPS == S_STEPS
SCAN_STEPS = P_STEPS


def _bwd_local(s, nc):
    return (s // nc) * nc + nc - 1 - s % nc


def _scan_blocks():
    return (lambda s: s, lambda s: _bwd_local(s, P_CHUNKS),
            lambda s: P_BLOCKS + s, lambda s: P_BLOCKS + _bwd_local(s, S_CHUNKS))


def _scan_specs(xcol):
    gcol = EVEN_W // 128 - 1
    blocks = _scan_blocks()
    return ([pl.BlockSpec((CHUNK, 3 * 512), lambda s, f=f: (f(s), xcol)) for f in blocks]
            + [pl.BlockSpec((CHUNK, 128), lambda s, f=f: (f(s), gcol)) for f in blocks]
            + [pl.BlockSpec((1, 16, 128), lambda s, f=f: (f(s), 0, 0)) for f in blocks])


def _scan_out_specs():
    local = (lambda s: s, lambda s: _bwd_local(s, P_CHUNKS), lambda s: s, lambda s: _bwd_local(s, S_CHUNKS))
    return [pl.BlockSpec((CHUNK, 512), lambda s, f=f: (f(s), 0)) for f in local]


def _mlstm_kernel(xpf_ref, xpb_ref, xsf_ref, xsb_ref, gpf_ref, gpb_ref, gsf_ref, gsb_ref,
                  rpf_ref, rpb_ref, rsf_ref, rsb_ref,
                  brow_ref, b2_ref, c0_ref, n0_ref, m0_ref,
                  hpf_ref, hpb_ref, hsf_ref, hsb_ref, c_out, n_out, m_out,
                  cp_s, np_s, mp_s, cs_s, ns_s, ms_s):
    s = pl.program_id(0)
    jp = s % P_CHUNKS
    js = s % S_CHUNKS

    @pl.when(jp == 0)
    def _():
        cp_s[...] = jnp.zeros_like(cp_s)
        np_s[...] = jnp.zeros_like(np_s)
        mp_s[...] = jnp.zeros_like(mp_s)

    @pl.when(js == 0)
    def _():
        for d in range(2):
            for h in range(HA):
                cs_s[d * HA + h] = c0_ref[0, 0, d, h]
                ns_s[d * HA + h] = jnp.broadcast_to(n0_ref[0, 0, d, h:h + 1, :], (DH, 128)).T
        ms_s[...] = m0_ref[0, 0]

    states = ((cp_s, np_s, mp_s), (cs_s, ns_s, ms_s))

    row = lax.broadcasted_iota(jnp.int32, (CHUNK, 128), 0)
    lane = lax.broadcasted_iota(jnp.int32, (CHUNK, 128), 1)
    left = lane < CHUNK
    lcol = jnp.where(left, lane, lane - CHUNK)
    r64 = lax.broadcasted_iota(jnp.int32, (CHUNK, CHUNK), 0)
    c64 = lax.broadcasted_iota(jnp.int32, (CHUNK, CHUNK), 1)
    r128 = lax.broadcasted_iota(jnp.int32, (128, 128), 0)
    c128 = lax.broadcasted_iota(jnp.int32, (128, 128), 1)
    same_half = (r128 < CHUNK) == (c128 < CHUNK)
    neg = -jnp.inf
    ins = ((xpf_ref, gpf_ref, rpf_ref, hpf_ref), (xpb_ref, gpb_ref, rpb_ref, hpb_ref),
           (xsf_ref, gsf_ref, rsf_ref, hsf_ref), (xsb_ref, gsb_ref, rsb_ref, hsb_ref))

    sel_r = lax.broadcasted_iota(jnp.int32, (128, HA * 128), 0)
    sel_h = lax.broadcasted_iota(jnp.int32, (128, HA * 128), 1) // 128
    ones_lr = lax.broadcasted_iota(jnp.int32, (256, 256), 0)
    ones_lc = lax.broadcasted_iota(jnp.int32, (256, 256), 1)
    half_sum = (((ones_lr % 128) < CHUNK) == (ones_lc < 128)).astype(BF16)
    ones_l = jnp.ones((CHUNK, 128), BF16)
    rows256 = lax.broadcasted_iota(jnp.int32, (CHUNK, 256), 0)

    def running_max(x, d):
        sh = 1
        while sh < CHUNK:
            if d == 0:
                x = jnp.maximum(x, jnp.where(rows256 >= sh, pltpu.roll(x, sh, 0), neg))
            else:
                x = jnp.maximum(x, jnp.where(rows256 < CHUNK - sh, pltpu.roll(x, CHUNK - sh, 0), neg))
            sh *= 2
        return x

    gate = []
    for u in range(4):
        d = u % 2
        gc_ref, gr_ref = ins[u][1], ins[u][2]
        if d == 0:
            incl_p, tri_c, tri_r = lcol <= row, c64 <= r64, jnp.logical_and(same_half, r128 <= c128)
        else:
            incl_p, tri_c, tri_r = lcol >= row, c64 >= r64, jnp.logical_and(same_half, r128 >= c128)
        gc = gc_ref[...] + brow_ref[...]
        gr = gr_ref[0] + b2_ref[...]
        i_rep = _replicate(gc, (sel_r == d * 2 * HA + sel_h).astype(BF16))
        b_rep = _replicate(_prefix_cols(tri_c.astype(BF16), _log_sigmoid(gc)),
                           (sel_r == d * 2 * HA + HA + sel_h).astype(BF16))
        cs_row = _prefix_rows(_log_sigmoid(gr), tri_r.astype(BF16))
        gate.append((incl_p, gr, cs_row, i_rep, b_rep))

    st = []
    for u in range(4):
        d = u % 2
        c_s, n_s, m_s = states[u // 2]
        for hp in range(HA // 2):
            x_ref = ins[u][0]
            incl_p, gr, cs_row, i_rep, b_rep = gate[u]
            end = CHUNK - 1 if d == 0 else 0
            heads = []
            for h in (2 * hp, 2 * hp + 1):
                c = d * HA + h
                heads.append(dict(c=c, h=h, q=x_ref[:, h * DH:(h + 1) * DH],
                                  k=x_ref[:, 512 + h * DH:512 + (h + 1) * DH] * (DH ** -0.5),
                                  v=x_ref[:, 1024 + h * DH:1024 + (h + 1) * DH],
                                  i=i_rep[:, h * 128:(h + 1) * 128], b=b_rep[:, h * 128:(h + 1) * 128],
                                  m=m_s[c:c + 1, :]))
            ha, hb_ = heads
            ri = d * HA + hp
            rf = d * HA + HA // 2 + hp
            i_row = gr[ri:ri + 1, :]
            b_row = cs_row[rf:rf + 1, :]
            run = running_max(jnp.concatenate([ha["i"] - ha["b"], hb_["i"] - hb_["b"]], axis=1), d)
            for idx, hd in enumerate(heads):
                top = jnp.maximum(hd["m"], run[:, idx * 128:(idx + 1) * 128])
                hd["m_t"] = hd["b"] + top
                hd["w_inter"] = jnp.exp(hd["m"] - top)
            b_p = jnp.where(left, ha["b"], hb_["b"])
            m_t_p = jnp.where(left, ha["m_t"], hb_["m_t"])
            dmat = jnp.where(incl_p, b_p - b_row + i_row, neg)
            q_cat = _bf(jnp.concatenate([ha["q"], hb_["q"]], axis=1))
            k_bd = _bf(_block_diag2(ha["k"], hb_["k"]))
            st.append(dict(u=u, d=d, heads=heads, sc=_mm_nt(q_cat, k_bd) * jnp.exp(dmat - m_t_p)))

    for p in st:
        ha, hb_ = p["heads"]
        h_ref = ins[p["u"]][3]
        c_s, n_s, m_s = states[p["u"] // 2]
        end = CHUNK - 1 if p["d"] == 0 else 0
        sc = p["sc"]
        sv = _mm(_bf(sc), _bf(_block_diag2(ha["v"], hb_["v"])))
        dens = _mm(jnp.concatenate(_split2(sc), axis=1), half_sum)
        for idx, hd in enumerate((ha, hb_)):
            c, h, q, k = hd["c"], hd["h"], hd["q"], hd["k"]
            c_mat = c_s[c]
            n_mat = n_s[c]
            q_cn = _mm(_bf(q), _bf(jnp.concatenate([c_mat, n_mat], axis=1)))
            num = sv[:, idx * DH:(idx + 1) * DH] + hd["w_inter"] * q_cn[:, :DH]
            den = dens[:, idx * 128:(idx + 1) * 128] + hd["w_inter"] * q_cn[:, DH:]
            h_ref[:, h * DH:(h + 1) * DH] = num / jnp.maximum(jnp.abs(den), jnp.exp(-hd["m_t"]))
            b_last = hd["b"][end:end + 1, :]
            g_end = b_last - hd["b"] + hd["i"]
            m_new = jnp.maximum(b_last + hd["m"], jnp.max(g_end, axis=0, keepdims=True))
            kwb = _bf(k * jnp.exp(g_end - m_new))
            decay = jnp.exp(b_last + hd["m"] - m_new)
            upd = _mm_tn(kwb, jnp.concatenate([_bf(hd["v"]), ones_l], axis=1))
            c_s[c] = decay * c_mat + upd[:, :DH]
            n_s[c] = decay * n_mat + upd[:, DH:]
            m_s[c:c + 1, :] = m_new

    @pl.when(jp == P_CHUNKS - 1)
    def _():
        for d in range(2):
            for h in range(HA):
                c_out[0, d, h] = cp_s[d * HA + h]
                n_out[0, d, h:h + 1, :] = np_s[d * HA + h].T[0:1, :]
        m_out[0] = mp_s[...]


def _mlstm_scan(proj, gates_t2, bias_row, bias2, c0, n0, m0, e):
    const2 = lambda s: (0, 0)
    in_specs = (_scan_specs(1)
                + [pl.BlockSpec((1, 128), const2), pl.BlockSpec((16, 128), const2),
                   pl.BlockSpec((1, 1, 2, HA, DH, DH), lambda s: (s // S_CHUNKS, e, 0, 0, 0, 0)),
                   pl.BlockSpec((1, 1, 2, HA, DH), lambda s: (s // S_CHUNKS, e, 0, 0, 0)),
                   pl.BlockSpec((1, 1, 2 * HA, 128), lambda s: (s // S_CHUNKS, e, 0, 0))])
    half = jax.ShapeDtypeStruct((NP_ROWS, HA * DH), F32)
    out_shape = (half, half, half, half,
                 jax.ShapeDtypeStruct((BATCH, 2, HA, DH, DH), F32),
                 jax.ShapeDtypeStruct((BATCH, 2, HA, DH), F32),
                 jax.ShapeDtypeStruct((BATCH, 2 * HA, 128), F32))
    out_specs = _scan_out_specs() + [
        pl.BlockSpec((1, 2, HA, DH, DH), lambda s: (s // P_CHUNKS, 0, 0, 0, 0)),
        pl.BlockSpec((1, 2, HA, DH), lambda s: (s // P_CHUNKS, 0, 0, 0)),
        pl.BlockSpec((1, 2 * HA, 128), lambda s: (s // P_CHUNKS, 0, 0))]
    state = [pltpu.VMEM((2 * HA, DH, DH), F32), pltpu.VMEM((2 * HA, DH, 128), F32), pltpu.VMEM((2 * HA, 128), F32)]
    return pl.pallas_call(
        _mlstm_kernel,
        out_shape=out_shape,
        grid=(SCAN_STEPS,),
        in_specs=in_specs,
        out_specs=out_specs,
        scratch_shapes=state + state,
        compiler_params=_params(1),
        name="mlstm_scan",
    )(proj, proj, proj, proj, proj, proj, proj, proj, gates_t2, gates_t2, gates_t2, gates_t2,
      bias_row, bias2, c0, n0, m0)


def _split2(x):
    hi = _bf(x)
    return hi, _bf(x - hi.astype(F32))


def _split3(x):
    h1 = _bf(x)
    r1 = x - h1.astype(F32)
    h2 = _bf(r1)
    return h1, h2, _bf(r1 - h2.astype(F32))


def _lhs3(x):
    hi, lo = _split2(x)
    return jnp.concatenate([hi, lo, hi], axis=1)


def _rhs3(x):
    hi, lo = _split2(x)
    return jnp.concatenate([hi, hi, lo], axis=0)


def _prefix_cols(tri_bf, x):
    n = x.shape[1]
    r = _mm(tri_bf, jnp.concatenate(_split3(x), axis=1))
    return r[:, :n] + r[:, n:2 * n] + r[:, 2 * n:]


def _replicate(x, sel):
    return _mm(jnp.concatenate(_split3(x), axis=1), jnp.concatenate([sel, sel, sel], axis=0))


def _prefix_rows(x, tri_bf):
    m = x.shape[0]
    r = _mm(jnp.concatenate(_split3(x), axis=0), tri_bf)
    return r[:m] + r[m:2 * m] + r[2 * m:]


LEVELS = tuple(range(6))


def _block_diag2(a, b):
    z = jnp.zeros_like(a)
    return jnp.concatenate([jnp.concatenate([a, z], axis=1), jnp.concatenate([z, b], axis=1)], axis=0)


def _delta_kernel(xpf_ref, xpb_ref, xsf_ref, xsb_ref, gpf_ref, gpb_ref, gsf_ref, gsb_ref,
                  rpf_ref, rpb_ref, rsf_ref, rsb_ref,
                  dtrow_ref, dt2_ref, narow_ref, na2_ref, s0_ref,
                  opf_ref, opb_ref, osf_ref, osb_ref, s_out, sp_s, ss_s):
    s = pl.program_id(0)
    jp = s % P_CHUNKS
    js = s % S_CHUNKS

    @pl.when(jp == 0)
    def _():
        sp_s[...] = jnp.zeros_like(sp_s)

    @pl.when(js == 0)
    def _():
        for d in range(2):
            for h in range(HB):
                ss_s[d * HB + h] = s0_ref[0, 0, d, h]

    states = (sp_s, ss_s)

    row = lax.broadcasted_iota(jnp.int32, (CHUNK, 128), 0)
    lane = lax.broadcasted_iota(jnp.int32, (CHUNK, 128), 1)
    left = lane < CHUNK
    lcol = jnp.where(left, lane, lane - CHUNK)
    eye_p = (lcol == row).astype(F32)
    r64 = lax.broadcasted_iota(jnp.int32, (CHUNK, CHUNK), 0)
    c64 = lax.broadcasted_iota(jnp.int32, (CHUNK, CHUNK), 1)
    r128 = lax.broadcasted_iota(jnp.int32, (128, 128), 0)
    c128 = lax.broadcasted_iota(jnp.int32, (128, 128), 1)
    same_half = (r128 < CHUNK) == (c128 < CHUNK)
    ins = ((xpf_ref, gpf_ref, rpf_ref, opf_ref), (xpb_ref, gpb_ref, rpb_ref, opb_ref),
           (xsf_ref, gsf_ref, rsf_ref, osf_ref), (xsb_ref, gsb_ref, rsb_ref, osb_ref))

    gate = []
    for u in range(4):
        d = u % 2
        gc_ref, gr_ref = ins[u][1], ins[u][2]
        if d == 0:
            incl_p, strict_p = lcol <= row, lcol < row
            tri_c, tri_r = c64 <= r64, jnp.logical_and(same_half, r128 <= c128)
        else:
            incl_p, strict_p = lcol >= row, lcol > row
            tri_c, tri_r = c64 >= r64, jnp.logical_and(same_half, r128 >= c128)
        xc = gc_ref[...]
        xr = gr_ref[0]
        la_c = narow_ref[...] * _softplus(xc + dtrow_ref[...])
        la_r = na2_ref[...] * _softplus(xr + dt2_ref[...])
        g_c = _prefix_cols(tri_c.astype(BF16), la_c)
        g_r = _prefix_rows(la_r, tri_r.astype(BF16))
        gate.append((incl_p, strict_p, _sigmoid(xc), g_c, g_r))

    def bd_rhs(hi, lo, mask=None):
        top = left if mask is None else jnp.logical_and(left, mask)
        bot = jnp.logical_not(left) if mask is None else jnp.logical_and(jnp.logical_not(left), mask)
        zero = jnp.zeros_like(hi)
        blocks = [jnp.concatenate([jnp.where(top, x, zero), jnp.where(bot, x, zero)], axis=0) for x in (hi, lo)]
        return jnp.concatenate([blocks[0], blocks[0], blocks[1]], axis=0)

    off = ([], [])
    for lv in LEVELS:
        same = jnp.right_shift(row, lv + 1) == jnp.right_shift(lcol, lv + 1)
        r_hi = jnp.bitwise_and(jnp.right_shift(row, lv), 1) == 1
        c_hi = jnp.bitwise_and(jnp.right_shift(lcol, lv), 1) == 1
        off[0].append(same & r_hi & jnp.logical_not(c_hi))
        off[1].append(same & jnp.logical_not(r_hi) & c_hi)

    pairs = [(u, hp) for u in range(4) for hp in range(HB // 2)]
    st = []
    for u, hp in pairs:
        d = u % 2
        x_ref = ins[u][0]
        incl_p, strict_p, beta_c, g_c, g_r = gate[u]
        end = CHUNK - 1 if d == 0 else 0
        heads = []
        for h in (2 * hp, 2 * hp + 1):
            ib = BETA_COL0 + d * HB + h
            ia = DECAY_COL0 + d * HB + h
            gcol = g_c[:, ia:ia + 1]
            bc = beta_c[:, ib:ib + 1]
            q = x_ref[:, h * DH:(h + 1) * DH]
            k = x_ref[:, 512 + h * DH:512 + (h + 1) * DH]
            v = x_ref[:, 1024 + h * DH:1024 + (h + 1) * DH]
            heads.append(dict(h=h, gcol=gcol, bc=bc, q=q, k=k, v=v, kb=k * bc, eg=jnp.exp(gcol),
                              g_last=gcol[end:end + 1, :]))
        ha, hb_ = heads
        r = (DECAY_COL0 + d * HB) // 2 + hp
        gcol_p = jnp.where(left, ha["gcol"], hb_["gcol"])
        decay = jnp.exp(jnp.where(incl_p, gcol_p - g_r[r:r + 1, :], -jnp.inf))
        k_bd = _bf(_block_diag2(ha["k"], hb_["k"]))
        kb_cat = _bf(jnp.concatenate([ha["kb"], hb_["kb"]], axis=1))
        q_cat = _bf(jnp.concatenate([ha["q"], hb_["q"]], axis=1))
        kq = _mm_nt(jnp.concatenate([kb_cat, q_cat], axis=0), k_bd)
        a_mat = jnp.where(strict_p, kq[:CHUNK] * decay, 0.0)
        qk = kq[CHUNK:] * decay
        a_hi, a_lo = _split2(a_mat)
        st.append(dict(u=u, d=d, heads=heads, t=eye_p - jnp.where(off[d][0], a_mat, 0.0), qk=qk,
                       am=[bd_rhs(a_hi, a_lo, m) for m in off[d][1:]]))

    for li in range(len(LEVELS) - 1):
        for p in st:
            p["t_parts"] = _split2(p["t"])
            t_hi, t_lo = p["t_parts"]
            p["w"] = _mm(jnp.concatenate([t_hi, t_lo, t_hi], axis=1), p["am"][li])
        for p in st:
            p["t"] = p["t"] - _mm(_lhs3(p["w"]), bd_rhs(*p["t_parts"]))

    for p in st:
        ha, hb_ = p["heads"]
        o_ref = ins[p["u"]][3]
        s_s = states[p["u"] // 2]
        rhs_a = jnp.concatenate([ha["v"] * ha["bc"], ha["kb"] * ha["eg"]], axis=1)
        rhs_b = jnp.concatenate([hb_["v"] * hb_["bc"], hb_["kb"] * hb_["eg"]], axis=1)
        (a_hi, a_lo), (b_hi, b_lo) = _split2(rhs_a), _split2(rhs_b)
        bd_hi, bd_lo = _block_diag2(a_hi, b_hi), _block_diag2(a_lo, b_lo)
        sol = _mm(_lhs3(p["t"]), jnp.concatenate([bd_hi, bd_hi, bd_lo], axis=0))
        vn = []
        for idx, hd in enumerate((ha, hb_)):
            c = p["d"] * HB + hd["h"]
            s_mat = s_s[c]
            sbf = _bf(s_mat)
            so = sol[:, idx * 2 * DH:(idx + 1) * 2 * DH]
            both = _mm(_bf(jnp.concatenate([so[:, DH:], hd["q"] * hd["eg"]], axis=0)), sbf)
            v_new = so[:, :DH] - both[:CHUNK]
            vn.append(v_new)
            hd["o1"] = both[CHUNK:]
            s_s[c] = (jnp.exp(hd["g_last"]) * s_mat
                      + _mm_tn(_bf(hd["k"] * jnp.exp(hd["g_last"] - hd["gcol"])), _bf(v_new)))
        o2 = _mm(_bf(p["qk"]), _bf(_block_diag2(vn[0], vn[1])))
        for idx, hd in enumerate((ha, hb_)):
            o_ref[:, hd["h"] * DH:(hd["h"] + 1) * DH] = hd["o1"] + o2[:, idx * DH:(idx + 1) * DH]

    @pl.when(jp == P_CHUNKS - 1)
    def _():
        for d in range(2):
            for h in range(HB):
                s_out[0, d, h] = sp_s[d * HB + h]


def _delta_scan(qkv, proj, gates_t2, dt_row, dt2, na_row, na2, s0, e):
    const2 = lambda s: (0, 0)
    in_specs = (_scan_specs(0)
                + [pl.BlockSpec((1, 128), const2), pl.BlockSpec((16, 128), const2),
                   pl.BlockSpec((1, 128), const2), pl.BlockSpec((16, 128), const2),
                   pl.BlockSpec((1, 1, 2, HB, DH, DH), lambda s: (s // S_CHUNKS, e, 0, 0, 0, 0))])
    half = jax.ShapeDtypeStruct((NP_ROWS, HB * DH), F32)
    out_shape = (half, half, half, half, jax.ShapeDtypeStruct((BATCH, 2, HB, DH, DH), F32))
    out_specs = _scan_out_specs() + [pl.BlockSpec((1, 2, HB, DH, DH), lambda s: (s // P_CHUNKS, 0, 0, 0, 0))]
    return pl.pallas_call(
        _delta_kernel,
        out_shape=out_shape,
        grid=(SCAN_STEPS,),
        in_specs=in_specs,
        out_specs=out_specs,
        scratch_shapes=[pltpu.VMEM((2 * HB, DH, DH), F32), pltpu.VMEM((2 * HB, DH, DH), F32)],
        compiler_params=_params(1),
        name="delta_scan",
    )(qkv, qkv, qkv, qkv, proj, proj, proj, proj, gates_t2, gates_t2, gates_t2, gates_t2,
      dt_row, dt2, na_row, na2, s0)


def _residual_ln(x, gate, y, g, b):
    r = DN_ALPHA * x + gate * y
    mu = jnp.mean(r, axis=-1, keepdims=True)
    var = jnp.mean(jnp.square(r - mu), axis=-1, keepdims=True)
    return (r - mu) * lax.rsqrt(var + LN_EPS) * g + b


def _even_out_kernel(*refs, tm, n_x):
    x_refs = refs[:n_x]
    (hpf_ref, hpb_ref, hsf_ref, hsb_ref, opf_ref, opb_ref, osf_ref, osb_ref,
     oa_ref, zb_ref, mg_ref, dg_ref, w_ref, gate_ref, g_ref, b_ref, o_ref) = refs[n_x:]
    is_p = pl.program_id(0) < NP_ROWS // tm
    hf_ref, hb_ref, of_ref, ob_ref = (
        lambda sl, p=p, q=q: jnp.where(is_p, p[:, sl], q[:, sl])
        for p, q in ((hpf_ref, hsf_ref), (hpb_ref, hsb_ref), (opf_ref, osf_ref), (opb_ref, osb_ref)))
    parts = []
    for h in range(HA):
        sl = slice(h * DH, (h + 1) * DH)
        hh = hf_ref(sl) + hb_ref(sl)
        mu = jnp.mean(hh, axis=-1, keepdims=True)
        var = jnp.mean(jnp.square(hh - mu), axis=-1, keepdims=True)
        parts.append(_sigmoid(oa_ref[:, sl]) * ((hh - mu) * lax.rsqrt(var + LN_EPS) * mg_ref[:, sl]))
    for h in range(HB):
        sl = slice(h * DH, (h + 1) * DH)
        oo = of_ref(sl) + ob_ref(sl)
        z = zb_ref[:, sl]
        nrm = oo * lax.rsqrt(jnp.mean(jnp.square(oo), axis=-1, keepdims=True) + LN_EPS) * dg_ref[:, sl]
        parts.append(nrm * (z * _sigmoid(z)))
    a = jnp.concatenate(parts, axis=1)
    y = _mm(_bf(a), w_ref[...])
    o_ref[...] = _residual_ln(_load_x(x_refs, tm), gate_ref[0], y, g_ref[...], b_ref[...])


def _even_out(h_scan, o_scan, proj, mg, dg, w, x, mods, layer, ln_g, ln_b):
    tm = 512
    npt = NP_ROWS // tm
    row512 = lambda i: (i, 0)
    const2 = lambda i: (0, 0)
    p_spec = pl.BlockSpec((tm, 512), lambda i: (jnp.minimum(i, npt - 1), 0))
    s_spec = pl.BlockSpec((tm, 512), lambda i: (jnp.maximum(i - npt, 0), 0))
    x_specs, x_arrays = _x_operands(x, tm)
    return pl.pallas_call(
        functools.partial(_even_out_kernel, tm=tm, n_x=len(x_arrays)),
        out_shape=jax.ShapeDtypeStruct((T, D), F32),
        grid=(T // tm,),
        in_specs=x_specs + [p_spec, p_spec, s_spec, s_spec, p_spec, p_spec, s_spec, s_spec,
                            pl.BlockSpec((tm, 512), lambda i: (i, 6)),
                            pl.BlockSpec((tm, 512), lambda i: (i, 7)),
                            pl.BlockSpec((1, 512), const2), pl.BlockSpec((1, 512), const2),
                            pl.BlockSpec((D, D), const2),
                            _mod_spec(layer, 2, tm),
                            pl.BlockSpec((1, D), const2), pl.BlockSpec((1, D), const2)],
        out_specs=pl.BlockSpec((tm, D), row512),
        compiler_params=_params(1),
        name="even_out_ln",
    )(*x_arrays, *h_scan, *o_scan, proj, proj, mg, dg, w, mods, ln_g, ln_b)


def _odd_out_kernel(ap_ref, as_ref, w_ref, x_ref, gate_ref, g_ref, b_ref, o_ref, *, tm):
    a = jnp.where(pl.program_id(0) < NP_ROWS // tm, ap_ref[...], as_ref[...])
    y = _mm(_bf(a), w_ref[...])
    o_ref[...] = _residual_ln(x_ref[...], gate_ref[0], y, g_ref[...], b_ref[...])


def _odd_out(a_prompt, a_latent, w, x, mods, layer, ln_g, ln_b):
    tm = 512
    npt = NP_ROWS // tm
    row = lambda i: (i, 0)
    const2 = lambda i: (0, 0)
    return pl.pallas_call(
        functools.partial(_odd_out_kernel, tm=tm),
        out_shape=jax.ShapeDtypeStruct((T, D), F32),
        grid=(T // tm,),
        in_specs=[pl.BlockSpec((tm, D), lambda i: (jnp.minimum(i, npt - 1), 0)),
                  pl.BlockSpec((tm, D), lambda i: (jnp.maximum(i - npt, 0), 0)),
                  pl.BlockSpec((D, D), const2), pl.BlockSpec((tm, D), row),
                  _mod_spec(layer, 2, tm), pl.BlockSpec((1, D), const2), pl.BlockSpec((1, D), const2)],
        out_specs=pl.BlockSpec((tm, D), row),
        compiler_params=_params(1),
        name="odd_out_ln",
    )(a_prompt, a_latent, w, x, mods, ln_g, ln_b)


def _attn_ctx_kernel(qkv_ref, sink_ref, o_ref):
    ones = jnp.ones((SEQ, HD), BF16)
    for kv in range(KVH):
        k = _bf(qkv_ref[:, HC * HD + kv * HD:HC * HD + (kv + 1) * HD])
        v = _bf(qkv_ref[:, (HC + KVH) * HD + kv * HD:(HC + KVH) * HD + (kv + 1) * HD])
        v1 = jnp.concatenate([v, ones], axis=1)
        heads = range(kv * (HC // KVH), (kv + 1) * (HC // KVH))
        sts = [_mm_nt(k, _bf(qkv_ref[:, h * HD:(h + 1) * HD])) for h in heads]
        ms = [jnp.maximum(jnp.max(st, axis=0, keepdims=True), sink_ref[h]) for st, h in zip(sts, heads)]
        ovs = [_mm_tn(v1, _bf(jnp.exp(st - m))) for st, m in zip(sts, ms)]
        outs = [ov[:HD] / (ov[HD:HD + 1] + jnp.exp(sink_ref[h] - m)) for ov, m, h in zip(ovs, ms, heads)]
        for pr in range(HC // KVH // 2):
            lane0 = (kv * (HC // KVH) + 2 * pr) * HD
            o_ref[:, lane0:lane0 + 2 * HD] = jnp.concatenate(outs[2 * pr:2 * pr + 2], axis=0).T


def _attn_context(qkv, sink):
    return pl.pallas_call(
        _attn_ctx_kernel,
        out_shape=jax.ShapeDtypeStruct((NP_ROWS, HC * HD), F32),
        grid=(BATCH,),
        in_specs=[pl.BlockSpec((SEQ, QKV_W), lambda b: (b, 0)),
                  pl.BlockSpec(memory_space=pltpu.SMEM)],
        out_specs=pl.BlockSpec((SEQ, HC * HD), lambda b: (b, 0)),
        compiler_params=_params(1),
        name="attn_context",
    )(qkv, sink)


def _attn_lat_kernel(q_ref, kp_ref, kc_ref, kn_ref, vp_ref, vc_ref, vn_ref, ck_ref, cv_ref,
                     sink_ref, o_ref, bias_s):
    j = pl.program_id(1)
    nb = DEC_SEQ // QBLOCK
    cc = lax.broadcasted_iota(jnp.int32, (3 * QBLOCK, QBLOCK), 0)
    r = lax.broadcasted_iota(jnp.int32, (3 * QBLOCK, QBLOCK), 1)
    lo = jnp.where(j >= 1, 0, QBLOCK)
    hi = jnp.where(j <= nb - 2, 3 * QBLOCK, 2 * QBLOCK)
    ok = (jnp.abs(QBLOCK + r - cc) <= WINDOW) & (cc >= lo) & (cc < hi)
    bias_s[...] = jnp.where(ok, 0.0, -jnp.inf)
    for kv in range(KVH):
        ks = slice(kv * HD, (kv + 1) * HD)
        k_all = _bf(jnp.concatenate([ck_ref[0, 0, kv], kp_ref[:, ks], kc_ref[:, ks], kn_ref[:, ks]], axis=0))
        v_all = _bf(jnp.concatenate([cv_ref[0, 0, kv], vp_ref[:, ks], vc_ref[:, ks], vn_ref[:, ks]], axis=0))
        v1 = jnp.concatenate([v_all, jnp.ones_like(v_all)], axis=1)
        outs = []
        for h in range(kv * (HC // KVH), (kv + 1) * (HC // KVH)):
            sink = sink_ref[h]
            st = _mm_nt(k_all, _bf(q_ref[:, h * HD:(h + 1) * HD]))
            s_ctx = st[:PAST_LEN]
            s_loc = st[PAST_LEN:] + bias_s[...]
            m = jnp.maximum(jnp.maximum(jnp.max(s_ctx, axis=0, keepdims=True),
                                        jnp.max(s_loc, axis=0, keepdims=True)), sink)
            p = _bf(jnp.concatenate([jnp.exp(s_ctx - m), jnp.exp(s_loc - m)], axis=0))
            ov = _mm_tn(v1, p)
            outs.append(ov[:HD] / (ov[HD:HD + 1] + jnp.exp(sink - m)))
        for pr in range(HC // KVH // 2):
            lane0 = (kv * (HC // KVH) + 2 * pr) * HD
            o_ref[:, lane0:lane0 + 2 * HD] = jnp.concatenate(outs[2 * pr:2 * pr + 2], axis=0).T


def _attn_latent(qkv, cache_k, cache_v, sink, o):
    nb = DEC_SEQ // QBLOCK
    base = NP_ROWS // QBLOCK
    blk = lambda b, j: base + b * nb + j
    prev = lambda b, j: base + b * nb + jnp.maximum(j - 1, 0)
    nxt = lambda b, j: base + b * nb + jnp.minimum(j + 1, nb - 1)
    kcol, vcol = HC * HD // 256, HC * HD // 256 + 1
    cache_spec = pl.BlockSpec((1, 1, KVH, PAST_LEN, HD), lambda b, j: (b, o, 0, 0, 0))
    return pl.pallas_call(
        _attn_lat_kernel,
        out_shape=jax.ShapeDtypeStruct((NS_ROWS, HC * HD), F32),
        grid=(DEC_BATCH, nb),
        in_specs=[pl.BlockSpec((QBLOCK, HC * HD), lambda b, j: (blk(b, j), 0)),
                  pl.BlockSpec((QBLOCK, 256), lambda b, j: (prev(b, j), kcol)),
                  pl.BlockSpec((QBLOCK, 256), lambda b, j: (blk(b, j), kcol)),
                  pl.BlockSpec((QBLOCK, 256), lambda b, j: (nxt(b, j), kcol)),
                  pl.BlockSpec((QBLOCK, 256), lambda b, j: (prev(b, j), vcol)),
                  pl.BlockSpec((QBLOCK, 256), lambda b, j: (blk(b, j), vcol)),
                  pl.BlockSpec((QBLOCK, 256), lambda b, j: (nxt(b, j), vcol)),
                  cache_spec, cache_spec,
                  pl.BlockSpec(memory_space=pltpu.SMEM)],
        out_specs=pl.BlockSpec((QBLOCK, HC * HD), lambda b, j: (b * nb + j, 0)),
        scratch_shapes=[pltpu.VMEM((3 * QBLOCK, QBLOCK), F32)],
        compiler_params=_params(2),
        name="attn_latent",
    )(qkv, qkv, qkv, qkv, qkv, qkv, qkv, cache_k, cache_v, sink)


def _route(lg):
    lane = lax.broadcasted_iota(jnp.int32, lg.shape, 1)
    neg = -jnp.inf
    big = 1 << 20
    is_grp = jnp.logical_and(lane >= N_EXPERTS, lane < N_EXPERTS + N_GROUPS)
    mg = jnp.max(jnp.where(is_grp, lg, neg), axis=1, keepdims=True)
    g_lane = jnp.min(jnp.where(jnp.logical_and(is_grp, lg == mg), lane, big), axis=1, keepdims=True)
    g_w = 1.0 / jnp.sum(jnp.where(is_grp, jnp.exp(lg - mg), 0.0), axis=1, keepdims=True)
    g_idx = g_lane - N_EXPERTS
    in_grp = jnp.logical_and(lane >= g_idx * EPG, lane < (g_idx + 1) * EPG)
    v1 = jnp.max(jnp.where(in_grp, lg, neg), axis=1, keepdims=True)
    i1 = jnp.min(jnp.where(jnp.logical_and(in_grp, lg == v1), lane, big), axis=1, keepdims=True)
    rest = jnp.logical_and(in_grp, lane != i1)
    v2 = jnp.max(jnp.where(rest, lg, neg), axis=1, keepdims=True)
    i2 = jnp.min(jnp.where(jnp.logical_and(rest, lg == v2), lane, big), axis=1, keepdims=True)
    e2 = jnp.exp(v2 - v1)
    p1 = 1.0 / (1.0 + e2)
    p2 = e2 / (1.0 + e2)
    return jnp.where(lane == 0, i1.astype(F32),
                     jnp.where(lane == 1, i2.astype(F32),
                               jnp.where(lane == 2, p1 * g_w, jnp.where(lane == 3, p2 * g_w, 0.0))))


def _moe_dense_kernel(x_ref, sh_ref, sc_ref, wr_ref, br_ref, wg_ref, wu_ref, wd_ref, gate_ref, g_ref, b_ref,
                      *out_and_scratch, tm):
    *o_refs, xm_s, meta_s, acc_s = out_and_scratch
    grp = pl.program_id(1)

    @pl.when(grp == 0)
    def _():
        xm = x_ref[...] * (1.0 + sc_ref[0]) + sh_ref[0]
        meta_s[...] = _route(_mm(_lhs3(xm), _rhs3(wr_ref[...])) + br_ref[...])
        xm_s[...] = _bf(xm)
        acc_s[...] = jnp.zeros_like(acc_s)

    xm = xm_s[...]
    meta = meta_s[...]
    i1 = meta[:, 0:1].astype(jnp.int32)
    i2 = meta[:, 1:2].astype(jnp.int32)
    w1 = meta[:, 2:3]
    w2 = meta[:, 3:4]
    hid = []
    for e in range(EPG):
        eid = grp * EPG + e
        gate = jnp.where(i1 == eid, w1, 0.0) + jnp.where(i2 == eid, w2, 0.0)
        a = _mm(xm, wg_ref[0, e])
        u = _mm(xm, wu_ref[0, e])
        hid.append(_bf((a * _sigmoid(a)) * u * gate))
    acc_s[...] += _mm(jnp.concatenate(hid, axis=1), wd_ref[0])

    def result():
        return _residual_ln(x_ref[...], gate_ref[0], acc_s[...], g_ref[...], b_ref[...])

    last = grp == N_GROUPS - 1
    if len(o_refs) == 1:
        @pl.when(last)
        def _():
            o_refs[0][...] = result()
    else:
        is_p = pl.program_id(0) < NP_ROWS // tm

        @pl.when(jnp.logical_and(last, is_p))
        def _():
            o_refs[0][...] = result()

        @pl.when(jnp.logical_and(last, jnp.logical_not(is_p)))
        def _():
            o_refs[1][...] = result()


def _moe_dense(x, mods, layer, w_r, b_r, wg, wu, wd, ln_g, ln_b, split_out=False):
    tm = 1024
    npt = NP_ROWS // tm
    row = lambda i, g: (i, 0)
    const2 = lambda i, g: (0, 0)
    if split_out:
        half = jax.ShapeDtypeStruct((NP_ROWS, D), F32)
        out_shape = (half, half)
        out_specs = (pl.BlockSpec((tm, D), lambda i, g: (jnp.minimum(i, npt - 1), 0)),
                     pl.BlockSpec((tm, D), lambda i, g: (jnp.maximum(i - npt, 0), 0)))
    else:
        out_shape = jax.ShapeDtypeStruct((T, D), F32)
        out_specs = pl.BlockSpec((tm, D), row)
    return pl.pallas_call(
        functools.partial(_moe_dense_kernel, tm=tm),
        out_shape=out_shape,
        grid=(T // tm, N_GROUPS),
        in_specs=[pl.BlockSpec((tm, D), row), _mod_spec(layer, 3, tm), _mod_spec(layer, 4, tm),
                  pl.BlockSpec((D, 128), const2), pl.BlockSpec((1, 128), const2),
                  pl.BlockSpec((1, EPG, D, EXPERT_FF), lambda i, g: (g, 0, 0, 0)),
                  pl.BlockSpec((1, EPG, D, EXPERT_FF), lambda i, g: (g, 0, 0, 0)),
                  pl.BlockSpec((1, EPG * EXPERT_FF, D), lambda i, g: (g, 0, 0)),
                  _mod_spec(layer, 5, tm), pl.BlockSpec((1, D), const2), pl.BlockSpec((1, D), const2)],
        out_specs=out_specs,
        scratch_shapes=[pltpu.VMEM((tm, D), BF16), pltpu.VMEM((tm, 128), F32), pltpu.VMEM((tm, D), F32)],
        compiler_params=_params(2, VMEM_LIMIT + (2 * tm * D * 4 if split_out else 0)),
        name="moe_dense",
    )(x, mods, mods, w_r, b_r, wg, wu, wd, mods, ln_g, ln_b)


def _permute_even_w(w):
    a_end = 4 * HA * DH
    g_end = a_end + 4 * HA
    c_end = g_end + 3 * HB * DH
    z_end = c_end + HB * DH
    small = jnp.concatenate([w[:, a_end:g_end], w[:, z_end:]], axis=1)
    pad = jnp.zeros((w.shape[0], 128 - small.shape[1]), w.dtype)
    return jnp.concatenate([w[:, g_end:c_end], w[:, :a_end], w[:, c_end:z_end], small, pad], axis=1)


def _lane_row(vals, offset):
    return jnp.zeros((1, 128), F32).at[0, offset:offset + vals.shape[0]].set(vals.astype(F32))


def _pair_rows(vals, offset):
    cols = jnp.zeros((N_GATE_COLS,), F32).at[offset:offset + vals.shape[0]].set(vals.astype(F32))
    return jnp.repeat(cols.reshape(N_GATE_COLS // 2, 2), CHUNK, axis=1)


def kernel(x_prompt, x_sample, c, c_ctx, state_mlstm_c, state_mlstm_n, state_mlstm_m, state_delta, cache_k, cache_v, w_mod, b_mod, ln_g, ln_b, w_in_even, mlstm_gate_b, mlstm_norm_g, delta_conv_w, delta_a_log, delta_dt_bias, delta_norm_g, w_out_even, w_qkv_odd, attn_sink, w_out_odd, w_grp, b_grp, w_erouter, b_erouter, w_gate, w_up, w_down):
    x = (x_prompt.reshape(NP_ROWS, D), x_sample.reshape(NS_ROWS, D))
    cvecs = jnp.concatenate([c_ctx[None, :], c, jnp.zeros((N_MOD_ROWS - 1 - DEC_BATCH, D), F32)], axis=0)
    mods = _modulation(cvecs, w_mod, b_mod)
    tables = _rope_tables()
    m0_all = jnp.broadcast_to(state_mlstm_m.reshape(DEC_BATCH, N_EVEN, 2 * HA, 1), (DEC_BATCH, N_EVEN, 2 * HA, 128))

    out_mc, out_mn, out_mm, out_ds = [], [], [], []
    new_k = jnp.zeros((BATCH, N_ODD, KVH, SEQ, HD), F32)
    new_v = jnp.zeros((BATCH, N_ODD, KVH, SEQ, HD), F32)
    for l in range(DEPTH):
        if l % 2 == 0:
            e = l // 2
            proj = _even_proj(x, mods, l, _bf(_permute_even_w(w_in_even[e])))
            gates_t2 = (proj[:, EVEN_W - 128:EVEN_W - 128 + N_GATE_COLS]
                        .reshape(T // CHUNK, CHUNK, N_GATE_COLS).transpose(0, 2, 1)
                        .reshape(T // CHUNK, N_GATE_COLS // 2, 2 * CHUNK))
            gb = mlstm_gate_b[e].reshape(-1)
            *h_scan, mc, mn, mm = _mlstm_scan(proj, gates_t2, _lane_row(gb, 0), _pair_rows(gb, 0),
                                              state_mlstm_c, state_mlstm_n, m0_all, e)
            qkv = _delta_prep(proj, delta_conv_w[e])
            dtb = delta_dt_bias[e].reshape(-1)
            nea = -jnp.exp(delta_a_log[e].astype(F32)).reshape(-1)
            *o_scan, ds = _delta_scan(qkv, proj, gates_t2, _lane_row(dtb, DECAY_COL0), _pair_rows(dtb, DECAY_COL0),
                                      _lane_row(nea, DECAY_COL0), _pair_rows(nea, DECAY_COL0), state_delta, e)
            x = _even_out(h_scan, o_scan, proj, mlstm_norm_g[e][None, :], delta_norm_g[e][None, :],
                          _bf(w_out_even[e]), x, mods, l, ln_g[l, 0][None, :], ln_b[l, 0][None, :])
            out_mc.append(mc)
            out_mn.append(mn)
            out_mm.append(mm[:, :, 0].reshape(BATCH, 2, HA))
            out_ds.append(ds)
        else:
            o = l // 2
            qkv, new_k, new_v = _odd_proj(x, mods, l, _bf(w_qkv_odd[o]), tables, new_k, new_v, o)
            a_p = _attn_context(qkv, attn_sink[o])
            a_s = _attn_latent(qkv, cache_k, cache_v, attn_sink[o], o)
            x = _odd_out(a_p, a_s, _bf(w_out_odd[o]), x, mods, l, ln_g[l, 0][None, :], ln_b[l, 0][None, :])
        w_r = jnp.concatenate([w_erouter[l].transpose(1, 0, 2).reshape(D, N_EXPERTS), w_grp[l],
                               jnp.zeros((D, 128 - N_EXPERTS - N_GROUPS), F32)], axis=1)
        b_r = jnp.concatenate([b_erouter[l].reshape(-1), b_grp[l],
                               jnp.zeros((128 - N_EXPERTS - N_GROUPS,), F32)])[None, :]
        x = _moe_dense(x, mods, l, w_r, b_r,
                       _bf(w_gate[l]).reshape(N_GROUPS, EPG, D, EXPERT_FF),
                       _bf(w_up[l]).reshape(N_GROUPS, EPG, D, EXPERT_FF),
                       _bf(w_down[l]).reshape(N_GROUPS, EPG * EXPERT_FF, D),
                       ln_g[l, 1][None, :], ln_b[l, 1][None, :], split_out=(l == DEPTH - 1))
    return (x[0].reshape(BATCH, SEQ, D), x[1].reshape(DEC_BATCH, DEC_SEQ, D),
            jnp.stack(out_mc, 1), jnp.stack(out_mn, 1), jnp.stack(out_mm, 1), jnp.stack(out_ds, 1),
            new_k, new_v)
```

```python
import functools

import numpy as np
import jax
import jax.numpy as jnp
from jax import lax
from jax.experimental import pallas as pl
from jax.experimental.pallas import tpu as pltpu

F32 = jnp.float32
BF16 = jnp.bfloat16

D = 1024
BATCH = 32
SEQ = 256
DEPTH = 4
DEC_BATCH = 2
DEC_SEQ = 4096
PAST_LEN = 512
GRID_W = 64
N_EVEN = 2
N_ODD = 2
HA = 4
HB = 4
DH = 128
CHUNK = 64
HC = 16
KVH = 4
HD = 64
WINDOW = 128
QBLOCK = 128
ROPE_THETA = 10000.0
N_GROUPS = 4
EPG = 4
N_EXPERTS = 16
EXPERT_FF = 256
DN_ALPHA = (2 * DEPTH) ** 0.25
LN_EPS = 1e-5

NP_ROWS = BATCH * SEQ
NS_ROWS = DEC_BATCH * DEC_SEQ
T = NP_ROWS + NS_ROWS
N_MOD_ROWS = 8
EVEN_W = 4224
N_GATE_COLS = 4 * HA + 4 * HB
BETA_COL0 = 4 * HA
DECAY_COL0 = 4 * HA + 2 * HB
QKV_W = (HC + 2 * KVH) * HD

P_CHUNKS = SEQ // CHUNK
S_CHUNKS = DEC_SEQ // CHUNK
P_STEPS = BATCH * P_CHUNKS
S_STEPS = DEC_BATCH * S_CHUNKS
P_BLOCKS = NP_ROWS // CHUNK

VMEM_LIMIT = 48 * 1024 * 1024


def _params(n_axes, vmem_limit=VMEM_LIMIT):
    return pltpu.CompilerParams(dimension_semantics=("arbitrary",) * n_axes,
                                vmem_limit_bytes=vmem_limit)


def _mm(a, b):
    return lax.dot_general(a, b, (((1,), (0,)), ((), ())), preferred_element_type=F32)


def _mm_nt(a, b):
    return lax.dot_general(a, b, (((1,), (1,)), ((), ())), preferred_element_type=F32)


def _mm_tn(a, b):
    return lax.dot_general(a, b, (((0,), (0,)), ((), ())), preferred_element_type=F32)


def _bf(x):
    return x.astype(BF16)


def _sigmoid(x):
    return 1.0 / (1.0 + jnp.exp(-x))


def _softplus(x):
    return jnp.maximum(x, 0.0) + jnp.log1p(jnp.exp(-jnp.abs(x)))


def _log_sigmoid(x):
    return jnp.minimum(x, 0.0) - jnp.log1p(jnp.exp(-jnp.abs(x)))


def _mod_row(tile, tm):
    npt = NP_ROWS // tm
    per = DEC_SEQ // tm
    return jnp.where(tile < npt, 0, 1 + (tile - npt) // per)


def _mod_spec(layer, chunk, tm):
    def imap(i, *_):
        return ((layer * N_MOD_ROWS + _mod_row(i, tm)) * 6 + chunk, 0, 0)
    return pl.BlockSpec((1, 1, D), imap)


def _modulation_kernel(c_ref, w_ref, b_ref, o_ref):
    x = c_ref[...]
    s = x * _sigmoid(x)
    o_ref[0] = _mm(_lhs3(s), _rhs3(w_ref[0])) + b_ref[0]


def _modulation(cvecs, w_mod, b_mod):
    out = pl.pallas_call(
        _modulation_kernel,
        out_shape=jax.ShapeDtypeStruct((DEPTH, N_MOD_ROWS, 6 * D), F32),
        grid=(DEPTH, 6),
        in_specs=[pl.BlockSpec((N_MOD_ROWS, D), lambda l, j: (0, 0)),
                  pl.BlockSpec((1, D, D), lambda l, j: (l, 0, j)),
                  pl.BlockSpec((1, 1, D), lambda l, j: (l * 6 + j, 0, 0))],
        out_specs=pl.BlockSpec((1, N_MOD_ROWS, D), lambda l, j: (l, 0, j)),
        compiler_params=_params(2),
        name="modulation",
    )(cvecs, w_mod, b_mod.reshape(DEPTH * 6, 1, D))
    return out.reshape(DEPTH * N_MOD_ROWS * 6, 1, D)


def _x_operands(x, tm):
    if isinstance(x, tuple):
        npt = NP_ROWS // tm
        return ([pl.BlockSpec((tm, D), lambda i, *_: (jnp.minimum(i, npt - 1), 0)),
                 pl.BlockSpec((tm, D), lambda i, *_: (jnp.maximum(i - npt, 0), 0))], list(x))
    return [pl.BlockSpec((tm, D), lambda i, *_: (i, 0))], [x]


def _load_x(x_refs, tm):
    if len(x_refs) == 1:
        return x_refs[0][...]
    return jnp.where(pl.program_id(0) < NP_ROWS // tm, x_refs[0][...], x_refs[1][...])


def _proj_kernel(*refs, tm, n_x):
    x_refs, (sh_ref, sc_ref, w_ref, o_ref) = refs[:n_x], refs[n_x:]
    xm = _load_x(x_refs, tm) * (1.0 + sc_ref[0]) + sh_ref[0]
    o_ref[...] = _mm(_bf(xm), w_ref[...])


def _even_proj(x, mods, layer, w):
    tm = 512
    x_specs, x_arrays = _x_operands(x, tm)
    return pl.pallas_call(
        functools.partial(_proj_kernel, tm=tm, n_x=len(x_arrays)),
        out_shape=jax.ShapeDtypeStruct((T, EVEN_W), F32),
        grid=(T // tm,),
        in_specs=x_specs + [_mod_spec(layer, 0, tm), _mod_spec(layer, 1, tm),
                            pl.BlockSpec((D, EVEN_W), lambda i: (0, 0), pipeline_mode=pl.Buffered(1))],
        out_specs=pl.BlockSpec((tm, EVEN_W), lambda i: (i, 0)),
        compiler_params=_params(1),
        name="even_proj",
    )(*x_arrays, mods, mods, w)


def _qkv_kernel(x_ref, sh_ref, sc_ref, w_ref, cos_ref, sa_ref, sb_ref, kc_in, vc_in, o_ref, kc_ref, vc_ref, *, tm):
    del kc_in, vc_in
    i = pl.program_id(0)
    xm = x_ref[...] * (1.0 + sc_ref[0]) + sh_ref[0]
    acc = _mm(_bf(xm), w_ref[...])

    @pl.when(i < NP_ROWS // tm)
    def _():
        for b in range(tm // SEQ):
            for kv in range(KVH):
                k0 = HC * HD + kv * HD
                v0 = (HC + KVH) * HD + kv * HD
                kc_ref[b, 0, kv] = acc[b * SEQ:(b + 1) * SEQ, k0:k0 + HD]
                vc_ref[b, 0, kv] = acc[b * SEQ:(b + 1) * SEQ, v0:v0 + HD]

    n_q = HC * HD // 128
    n_k = KVH * HD // 128
    is_latent = i >= NP_ROWS // tm

    def scaled(g):
        blk = acc[:, g * 128:(g + 1) * 128]
        return blk * (HD ** -0.5) if g < n_q else blk

    @pl.when(is_latent)
    def _():
        cos, sa, sb = cos_ref[...], sa_ref[...], sb_ref[...]
        for g in range(n_q + n_k):
            blk = scaled(g)
            o_ref[:, g * 128:(g + 1) * 128] = blk * cos + pltpu.roll(blk, 112, 1) * sa + pltpu.roll(blk, 16, 1) * sb

    @pl.when(jnp.logical_not(is_latent))
    def _():
        for g in range(n_q + n_k):
            o_ref[:, g * 128:(g + 1) * 128] = scaled(g)

    o_ref[:, (n_q + n_k) * 128:] = acc[:, (n_q + n_k) * 128:]


def _rope_tables():
    half = HD // 4
    inv = np.float32(ROPE_THETA) ** (-np.arange(half, dtype=np.float32) / np.float32(half))
    pos = np.arange(DEC_SEQ)
    row = (pos // GRID_W).astype(np.float32)[:, None] * inv[None, :]
    col = (pos % GRID_W).astype(np.float32)[:, None] * inv[None, :]
    cos = np.concatenate([np.cos(row), np.cos(row), np.cos(col), np.cos(col)], axis=-1)
    sin = np.concatenate([np.sin(row), np.sin(row), np.sin(col), np.sin(col)], axis=-1)
    first = (np.arange(HD) % 32) < 16
    sa = np.where(first, -sin, 0.0)
    sb = np.where(first, 0.0, sin)
    tile2 = lambda t: jnp.asarray(np.concatenate([t, t], axis=-1), F32)
    return tile2(cos), tile2(sa), tile2(sb)


def _odd_proj(x, mods, layer, w, tables, cache_k, cache_v, o):
    tm = 512
    npt = NP_ROWS // tm
    per = DEC_SEQ // tm
    tab_spec = pl.BlockSpec((tm, 128), lambda i: (jnp.where(i < npt, 0, (i - npt) % per), 0))
    cache_spec = pl.BlockSpec((tm // SEQ, 1, KVH, SEQ, HD), lambda i: (jnp.minimum(i, npt - 1), o, 0, 0, 0))
    cache_shape = jax.ShapeDtypeStruct((BATCH, N_ODD, KVH, SEQ, HD), F32)
    return pl.pallas_call(
        functools.partial(_qkv_kernel, tm=tm),
        out_shape=(jax.ShapeDtypeStruct((T, QKV_W), F32), cache_shape, cache_shape),
        grid=(T // tm,),
        in_specs=[pl.BlockSpec((tm, D), lambda i: (i, 0)),
                  _mod_spec(layer, 0, tm), _mod_spec(layer, 1, tm),
                  pl.BlockSpec((D, QKV_W), lambda i: (0, 0)),
                  tab_spec, tab_spec, tab_spec,
                  pl.BlockSpec(memory_space=pl.ANY), pl.BlockSpec(memory_space=pl.ANY)],
        out_specs=(pl.BlockSpec((tm, QKV_W), lambda i: (i, 0)), cache_spec, cache_spec),
        input_output_aliases={7: 1, 8: 2},
        compiler_params=_params(1),
        name="odd_qkv_proj",
    )(x, mods, mods, w, *tables, cache_k, cache_v)


PREP_ROWS = 256


def _delta_prep_kernel(x_ref, prev_ref, next_ref, w_ref, o_ref):
    i = pl.program_id(0)
    npb = NP_ROWS // PREP_ROWS
    per = DEC_SEQ // PREP_ROWS
    is_latent = i >= npb
    pos = (i - npb) % per
    has_prev = jnp.logical_and(is_latent, pos > 0)
    has_next = jnp.logical_and(is_latent, pos < per - 1)
    x = x_ref[...]
    w = w_ref[...]
    rows = lax.broadcasted_iota(jnp.int32, x.shape, 0)
    prev_row = jnp.where(has_prev, prev_ref[7:8, :], 0.0)
    next_row = jnp.where(has_next, next_ref[0:1, :], 0.0)
    xm1 = jnp.where(rows == 0, prev_row, pltpu.roll(x, 1, 0))
    xp1 = jnp.where(rows == PREP_ROWS - 1, next_row, pltpu.roll(x, PREP_ROWS - 1, 0))
    y = xm1 * w[0:1, :] + x * w[1:2, :] + xp1 * w[2:3, :]
    y = y * _sigmoid(y)
    for h in range(3 * HB):
        yh = y[:, h * DH:(h + 1) * DH]
        if h < 2 * HB:
            inv = lax.rsqrt(jnp.sum(yh * yh, axis=-1, keepdims=True) + 1e-6)
            yh = yh * (inv * (DH ** -0.5) if h < HB else inv)
        o_ref[:, h * DH:(h + 1) * DH] = yh


def _delta_prep(proj, conv_w):
    nblk = T // PREP_ROWS
    sub = PREP_ROWS // 8
    last8 = T // 8 - 1
    return pl.pallas_call(
        _delta_prep_kernel,
        out_shape=jax.ShapeDtypeStruct((T, 3 * 512), F32),
        grid=(nblk,),
        in_specs=[pl.BlockSpec((PREP_ROWS, 3 * 512), lambda i: (i, 0)),
                  pl.BlockSpec((8, 3 * 512), lambda i: (jnp.maximum(i * sub - 1, 0), 0)),
                  pl.BlockSpec((8, 3 * 512), lambda i: (jnp.minimum((i + 1) * sub, last8), 0)),
                  pl.BlockSpec((3, 3 * 512), lambda i: (0, 0))],
        out_specs=pl.BlockSpec((PREP_ROWS, 3 * 512), lambda i: (i, 0)),
        compiler_params=_params(1),
        name="delta_prep",
    )(proj, proj, proj, conv_w)


assert P_STEPS == S_STEPS
SCAN_STEPS = P_STEPS


def _bwd_local(s, nc):
    return (s // nc) * nc + nc - 1 - s % nc


def _scan_blocks():
    return (lambda s: s, lambda s: _bwd_local(s, P_CHUNKS),
            lambda s: P_BLOCKS + s, lambda s: P_BLOCKS + _bwd_local(s, S_CHUNKS))


def _scan_specs(xcol):
    gcol = EVEN_W // 128 - 1
    blocks = _scan_blocks()
    return ([pl.BlockSpec((CHUNK, 3 * 512), lambda s, f=f: (f(s), xcol)) for f in blocks]
            + [pl.BlockSpec((CHUNK, 128), lambda s, f=f: (f(s), gcol)) for f in blocks]
            + [pl.BlockSpec((1, 16, 128), lambda s, f=f: (f(s), 0, 0)) for f in blocks])


def _scan_out_specs():
    local = (lambda s: s, lambda s: _bwd_local(s, P_CHUNKS), lambda s: s, lambda s: _bwd_local(s, S_CHUNKS))
    return [pl.BlockSpec((CHUNK, 512), lambda s, f=f: (f(s), 0)) for f in local]


def _mlstm_kernel(xpf_ref, xpb_ref, xsf_ref, xsb_ref, gpf_ref, gpb_ref, gsf_ref, gsb_ref,
                  rpf_ref, rpb_ref, rsf_ref, rsb_ref,
                  brow_ref, b2_ref, c0_ref, n0_ref, m0_ref, c_all_in,
                  hpf_ref, hpb_ref, hsf_ref, hsb_ref, c_out, n_out, m_out,
                  cp_s, np_s, mp_s, cs_s, ns_s, ms_s):
    del c_all_in
    s = pl.program_id(0)
    jp = s % P_CHUNKS
    js = s % S_CHUNKS

    @pl.when(jp == 0)
    def _():
        cp_s[...] = jnp.zeros_like(cp_s)
        np_s[...] = jnp.zeros_like(np_s)
        mp_s[...] = jnp.zeros_like(mp_s)

    @pl.when(js == 0)
    def _():
        for d in range(2):
            for h in range(HA):
                cs_s[d * HA + h] = c0_ref[0, 0, d, h]
                ns_s[d * HA + h] = jnp.broadcast_to(n0_ref[0, 0, d, h:h + 1, :], (DH, 128)).T
        ms_s[...] = m0_ref[0, 0]

    states = ((cp_s, np_s, mp_s), (cs_s, ns_s, ms_s))

    row = lax.broadcasted_iota(jnp.int32, (CHUNK, 128), 0)
    lane = lax.broadcasted_iota(jnp.int32, (CHUNK, 128), 1)
    left = lane < CHUNK
    lcol = jnp.where(left, lane, lane - CHUNK)
    r64 = lax.broadcasted_iota(jnp.int32, (CHUNK, CHUNK), 0)
    c64 = lax.broadcasted_iota(jnp.int32, (CHUNK, CHUNK), 1)
    r128 = lax.broadcasted_iota(jnp.int32, (128, 128), 0)
    c128 = lax.broadcasted_iota(jnp.int32, (128, 128), 1)
    same_half = (r128 < CHUNK) == (c128 < CHUNK)
    neg = -jnp.inf
    ins = ((xpf_ref, gpf_ref, rpf_ref, hpf_ref), (xpb_ref, gpb_ref, rpb_ref, hpb_ref),
           (xsf_ref, gsf_ref, rsf_ref, hsf_ref), (xsb_ref, gsb_ref, rsb_ref, hsb_ref))

    sel_r = lax.broadcasted_iota(jnp.int32, (128, HA * 128), 0)
    sel_h = lax.broadcasted_iota(jnp.int32, (128, HA * 128), 1) // 128
    ones_lr = lax.broadcasted_iota(jnp.int32, (256, 256), 0)
    ones_lc = lax.broadcasted_iota(jnp.int32, (256, 256), 1)
    half_sum = (((ones_lr % 128) < CHUNK) == (ones_lc < 128)).astype(BF16)
    ones_l = jnp.ones((CHUNK, 128), BF16)
    rows256 = lax.broadcasted_iota(jnp.int32, (CHUNK, 256), 0)

    def running_max(x, d):
        sh = 1
        while sh < CHUNK:
            if d == 0:
                x = jnp.maximum(x, jnp.where(rows256 >= sh, pltpu.roll(x, sh, 0), neg))
            else:
                x = jnp.maximum(x, jnp.where(rows256 < CHUNK - sh, pltpu.roll(x, CHUNK - sh, 0), neg))
            sh *= 2
        return x

    gate = []
    for u in range(4):
        d = u % 2
        gc_ref, gr_ref = ins[u][1], ins[u][2]
        if d == 0:
            incl_p, tri_c, tri_r = lcol <= row, c64 <= r64, jnp.logical_and(same_half, r128 <= c128)
        else:
            incl_p, tri_c, tri_r = lcol >= row, c64 >= r64, jnp.logical_and(same_half, r128 >= c128)
        gc = gc_ref[...] + brow_ref[...]
        gr = gr_ref[0] + b2_ref[...]
        i_rep = _replicate(gc, (sel_r == d * 2 * HA + sel_h).astype(BF16))
        b_rep = _replicate(_prefix_cols(tri_c.astype(BF16), _log_sigmoid(gc)),
                           (sel_r == d * 2 * HA + HA + sel_h).astype(BF16))
        cs_row = _prefix_rows(_log_sigmoid(gr), tri_r.astype(BF16))
        gate.append((incl_p, gr, cs_row, i_rep, b_rep))

    st = []
    for u in range(4):
        d = u % 2
        c_s, n_s, m_s = states[u // 2]
        for hp in range(HA // 2):
            x_ref = ins[u][0]
            incl_p, gr, cs_row, i_rep, b_rep = gate[u]
            end = CHUNK - 1 if d == 0 else 0
            heads = []
            for h in (2 * hp, 2 * hp + 1):
                c = d * HA + h
                heads.append(dict(c=c, h=h, q=x_ref[:, h * DH:(h + 1) * DH],
                                  k=x_ref[:, 512 + h * DH:512 + (h + 1) * DH] * (DH ** -0.5),
                                  v=x_ref[:, 1024 + h * DH:1024 + (h + 1) * DH],
                                  i=i_rep[:, h * 128:(h + 1) * 128], b=b_rep[:, h * 128:(h + 1) * 128],
                                  m=m_s[c:c + 1, :]))
            ha, hb_ = heads
            ri = d * HA + hp
            rf = d * HA + HA // 2 + hp
            i_row = gr[ri:ri + 1, :]
            b_row = cs_row[rf:rf + 1, :]
            run = running_max(jnp.concatenate([ha["i"] - ha["b"], hb_["i"] - hb_["b"]], axis=1), d)
            for idx, hd in enumerate(heads):
                top = jnp.maximum(hd["m"], run[:, idx * 128:(idx + 1) * 128])
                hd["m_t"] = hd["b"] + top
                hd["w_inter"] = jnp.exp(hd["m"] - top)
            b_p = jnp.where(left, ha["b"], hb_["b"])
            m_t_p = jnp.where(left, ha["m_t"], hb_["m_t"])
            dmat = jnp.where(incl_p, b_p - b_row + i_row, neg)
            q_cat = _bf(jnp.concatenate([ha["q"], hb_["q"]], axis=1))
            k_bd = _bf(_block_diag2(ha["k"], hb_["k"]))
            st.append(dict(u=u, d=d, heads=heads, sc=_mm_nt(q_cat, k_bd) * jnp.exp(dmat - m_t_p)))

    for p in st:
        ha, hb_ = p["heads"]
        h_ref = ins[p["u"]][3]
        c_s, n_s, m_s = states[p["u"] // 2]
        end = CHUNK - 1 if p["d"] == 0 else 0
        sc = p["sc"]
        sv = _mm(_bf(sc), _bf(_block_diag2(ha["v"], hb_["v"])))
        dens = _mm(jnp.concatenate(_split2(sc), axis=1), half_sum)
        for idx, hd in enumerate((ha, hb_)):
            c, h, q, k = hd["c"], hd["h"], hd["q"], hd["k"]
            c_mat = c_s[c]
            n_mat = n_s[c]
            q_cn = _mm(_bf(q), _bf(jnp.concatenate([c_mat, n_mat], axis=1)))
            num = sv[:, idx * DH:(idx + 1) * DH] + hd["w_inter"] * q_cn[:, :DH]
            den = dens[:, idx * 128:(idx + 1) * 128] + hd["w_inter"] * q_cn[:, DH:]
            h_ref[:, h * DH:(h + 1) * DH] = num / jnp.maximum(jnp.abs(den), jnp.exp(-hd["m_t"]))
            b_last = hd["b"][end:end + 1, :]
            g_end = b_last - hd["b"] + hd["i"]
            m_new = jnp.maximum(b_last + hd["m"], jnp.max(g_end, axis=0, keepdims=True))
            kwb = _bf(k * jnp.exp(g_end - m_new))
            decay = jnp.exp(b_last + hd["m"] - m_new)
            upd = _mm_tn(kwb, jnp.concatenate([_bf(hd["v"]), ones_l], axis=1))
            c_s[c] = decay * c_mat + upd[:, :DH]
            n_s[c] = decay * n_mat + upd[:, DH:]
            m_s[c:c + 1, :] = m_new

    @pl.when(jp == P_CHUNKS - 1)
    def _():
        for d in range(2):
            for h in range(HA):
                c_out[0, 0, d, h] = cp_s[d * HA + h]
                n_out[0, d, h:h + 1, :] = np_s[d * HA + h].T[0:1, :]
        m_out[0] = mp_s[...]


def _mlstm_scan(proj, gates_t2, bias_row, bias2, c0, n0, m0, c_all, e):
    const2 = lambda s: (0, 0)
    in_specs = (_scan_specs(1)
                + [pl.BlockSpec((1, 128), const2), pl.BlockSpec((16, 128), const2),
                   pl.BlockSpec((1, 1, 2, HA, DH, DH), lambda s: (s // S_CHUNKS, e, 0, 0, 0, 0)),
                   pl.BlockSpec((1, 1, 2, HA, DH), lambda s: (s // S_CHUNKS, e, 0, 0, 0)),
                   pl.BlockSpec((1, 1, 2 * HA, 128), lambda s: (s // S_CHUNKS, e, 0, 0)),
                   pl.BlockSpec(memory_space=pl.ANY)])
    half = jax.ShapeDtypeStruct((NP_ROWS, HA * DH), F32)
    out_shape = (half, half, half, half,
                 jax.ShapeDtypeStruct((BATCH, N_EVEN, 2, HA, DH, DH), F32),
                 jax.ShapeDtypeStruct((BATCH, 2, HA, DH), F32),
                 jax.ShapeDtypeStruct((BATCH, 2 * HA, 128), F32))
    out_specs = _scan_out_specs() + [
        pl.BlockSpec((1, 1, 2, HA, DH, DH), lambda s: (s // P_CHUNKS, e, 0, 0, 0, 0)),
        pl.BlockSpec((1, 2, HA, DH), lambda s: (s // P_CHUNKS, 0, 0, 0)),
        pl.BlockSpec((1, 2 * HA, 128), lambda s: (s // P_CHUNKS, 0, 0))]
    state = [pltpu.VMEM((2 * HA, DH, DH), F32), pltpu.VMEM((2 * HA, DH, 128), F32), pltpu.VMEM((2 * HA, 128), F32)]
    return pl.pallas_call(
        _mlstm_kernel,
        out_shape=out_shape,
        grid=(SCAN_STEPS,),
        in_specs=in_specs,
        out_specs=out_specs,
        scratch_shapes=state + state,
        input_output_aliases={17: 4},
        compiler_params=_params(1),
        name="mlstm_scan",
    )(proj, proj, proj, proj, proj, proj, proj, proj, gates_t2, gates_t2, gates_t2, gates_t2,
      bias_row, bias2, c0, n0, m0, c_all)


def _split2(x):
    hi = _bf(x)
    return hi, _bf(x - hi.astype(F32))


def _split3(x):
    h1 = _bf(x)
    r1 = x - h1.astype(F32)
    h2 = _bf(r1)
    return h1, h2, _bf(r1 - h2.astype(F32))


def _lhs3(x):
    hi, lo = _split2(x)
    return jnp.concatenate([hi, lo, hi], axis=1)


def _rhs3(x):
    hi, lo = _split2(x)
    return jnp.concatenate([hi, hi, lo], axis=0)


def _prefix_cols(tri_bf, x):
    n = x.shape[1]
    r = _mm(tri_bf, jnp.concatenate(_split3(x), axis=1))
    return r[:, :n] + r[:, n:2 * n] + r[:, 2 * n:]


def _replicate(x, sel):
    return _mm(jnp.concatenate(_split3(x), axis=1), jnp.concatenate([sel, sel, sel], axis=0))


def _prefix_rows(x, tri_bf):
    m = x.shape[0]
    r = _mm(jnp.concatenate(_split3(x), axis=0), tri_bf)
    return r[:m] + r[m:2 * m] + r[2 * m:]


LEVELS = tuple(range(6))


def _block_diag2(a, b):
    z = jnp.zeros_like(a)
    return jnp.concatenate([jnp.concatenate([a, z], axis=1), jnp.concatenate([z, b], axis=1)], axis=0)


def _delta_kernel(xpf_ref, xpb_ref, xsf_ref, xsb_ref, gpf_ref, gpb_ref, gsf_ref, gsb_ref,
                  rpf_ref, rpb_ref, rsf_ref, rsb_ref,
                  dtrow_ref, dt2_ref, narow_ref, na2_ref, s0_ref, s_all_in,
                  opf_ref, opb_ref, osf_ref, osb_ref, s_out, sp_s, ss_s):
    del s_all_in
    s = pl.program_id(0)
    jp = s % P_CHUNKS
    js = s % S_CHUNKS

    @pl.when(jp == 0)
    def _():
        sp_s[...] = jnp.zeros_like(sp_s)

    @pl.when(js == 0)
    def _():
        for d in range(2):
            for h in range(HB):
                ss_s[d * HB + h] = s0_ref[0, 0, d, h]

    states = (sp_s, ss_s)

    row = lax.broadcasted_iota(jnp.int32, (CHUNK, 128), 0)
    lane = lax.broadcasted_iota(jnp.int32, (CHUNK, 128), 1)
    left = lane < CHUNK
    lcol = jnp.where(left, lane, lane - CHUNK)
    eye_p = (lcol == row).astype(F32)
    r64 = lax.broadcasted_iota(jnp.int32, (CHUNK, CHUNK), 0)
    c64 = lax.broadcasted_iota(jnp.int32, (CHUNK, CHUNK), 1)
    r128 = lax.broadcasted_iota(jnp.int32, (128, 128), 0)
    c128 = lax.broadcasted_iota(jnp.int32, (128, 128), 1)
    same_half = (r128 < CHUNK) == (c128 < CHUNK)
    ins = ((xpf_ref, gpf_ref, rpf_ref, opf_ref), (xpb_ref, gpb_ref, rpb_ref, opb_ref),
           (xsf_ref, gsf_ref, rsf_ref, osf_ref), (xsb_ref, gsb_ref, rsb_ref, osb_ref))

    gate = []
    for u in range(4):
        d = u % 2
        gc_ref, gr_ref = ins[u][1], ins[u][2]
        if d == 0:
            incl_p, strict_p = lcol <= row, lcol < row
            tri_c, tri_r = c64 <= r64, jnp.logical_and(same_half, r128 <= c128)
        else:
            incl_p, strict_p = lcol >= row, lcol > row
            tri_c, tri_r = c64 >= r64, jnp.logical_and(same_half, r128 >= c128)
        xc = gc_ref[...]
        xr = gr_ref[0]
        la_c = narow_ref[...] * _softplus(xc + dtrow_ref[...])
        la_r = na2_ref[...] * _softplus(xr + dt2_ref[...])
        g_c = _prefix_cols(tri_c.astype(BF16), la_c)
        g_r = _prefix_rows(la_r, tri_r.astype(BF16))
        gate.append((incl_p, strict_p, _sigmoid(xc), g_c, g_r))

    def bd_rhs(hi, lo, mask=None):
        top = left if mask is None else jnp.logical_and(left, mask)
        bot = jnp.logical_not(left) if mask is None else jnp.logical_and(jnp.logical_not(left), mask)
        zero = jnp.zeros_like(hi)
        blocks = [jnp.concatenate([jnp.where(top, x, zero), jnp.where(bot, x, zero)], axis=0) for x in (hi, lo)]
        return jnp.concatenate([blocks[0], blocks[0], blocks[1]], axis=0)

    off = ([], [])
    for lv in LEVELS:
        same = jnp.right_shift(row, lv + 1) == jnp.right_shift(lcol, lv + 1)
        r_hi = jnp.bitwise_and(jnp.right_shift(row, lv), 1) == 1
        c_hi = jnp.bitwise_and(jnp.right_shift(lcol, lv), 1) == 1
        off[0].append(same & r_hi & jnp.logical_not(c_hi))
        off[1].append(same & jnp.logical_not(r_hi) & c_hi)

    pairs = [(u, hp) for u in range(4) for hp in range(HB // 2)]
    st = []
    for u, hp in pairs:
        d = u % 2
        x_ref = ins[u][0]
        incl_p, strict_p, beta_c, g_c, g_r = gate[u]
        end = CHUNK - 1 if d == 0 else 0
        heads = []
        for h in (2 * hp, 2 * hp + 1):
            ib = BETA_COL0 + d * HB + h
            ia = DECAY_COL0 + d * HB + h
            gcol = g_c[:, ia:ia + 1]
            bc = beta_c[:, ib:ib + 1]
            q = x_ref[:, h * DH:(h + 1) * DH]
            k = x_ref[:, 512 + h * DH:512 + (h + 1) * DH]
            v = x_ref[:, 1024 + h * DH:1024 + (h + 1) * DH]
            heads.append(dict(h=h, gcol=gcol, bc=bc, q=q, k=k, v=v, kb=k * bc, eg=jnp.exp(gcol),
                              g_last=gcol[end:end + 1, :]))
        ha, hb_ = heads
        r = (DECAY_COL0 + d * HB) // 2 + hp
        gcol_p = jnp.where(left, ha["gcol"], hb_["gcol"])
        decay = jnp.exp(jnp.where(incl_p, gcol_p - g_r[r:r + 1, :], -jnp.inf))
        k_bd = _bf(_block_diag2(ha["k"], hb_["k"]))
        kb_cat = _bf(jnp.concatenate([ha["kb"], hb_["kb"]], axis=1))
        q_cat = _bf(jnp.concatenate([ha["q"], hb_["q"]], axis=1))
        kq = _mm_nt(jnp.concatenate([kb_cat, q_cat], axis=0), k_bd)
        a_mat = jnp.where(strict_p, kq[:CHUNK] * decay, 0.0)
        qk = kq[CHUNK:] * decay
        a_hi, a_lo = _split2(a_mat)
        st.append(dict(u=u, d=d, heads=heads, t=eye_p - jnp.where(off[d][0], a_mat, 0.0), qk=qk,
                       am=[bd_rhs(a_hi, a_lo, m) for m in off[d][1:]]))

    for li in range(len(LEVELS) - 1):
        for p in st:
            p["t_parts"] = _split2(p["t"])
            t_hi, t_lo = p["t_parts"]
            p["w"] = _mm(jnp.concatenate([t_hi, t_lo, t_hi], axis=1), p["am"][li])
        for p in st:
            p["t"] = p["t"] - _mm(_lhs3(p["w"]), bd_rhs(*p["t_parts"]))

    for p in st:
        ha, hb_ = p["heads"]
        o_ref = ins[p["u"]][3]
        s_s = states[p["u"] // 2]
        rhs_a = jnp.concatenate([ha["v"] * ha["bc"], ha["kb"] * ha["eg"]], axis=1)
        rhs_b = jnp.concatenate([hb_["v"] * hb_["bc"], hb_["kb"] * hb_["eg"]], axis=1)
        (a_hi, a_lo), (b_hi, b_lo) = _split2(rhs_a), _split2(rhs_b)
        bd_hi, bd_lo = _block_diag2(a_hi, b_hi), _block_diag2(a_lo, b_lo)
        sol = _mm(_lhs3(p["t"]), jnp.concatenate([bd_hi, bd_hi, bd_lo], axis=0))
        vn = []
        for idx, hd in enumerate((ha, hb_)):
            c = p["d"] * HB + hd["h"]
            s_mat = s_s[c]
            sbf = _bf(s_mat)
            so = sol[:, idx * 2 * DH:(idx + 1) * 2 * DH]
            both = _mm(_bf(jnp.concatenate([so[:, DH:], hd["q"] * hd["eg"]], axis=0)), sbf)
            v_new = so[:, :DH] - both[:CHUNK]
            vn.append(v_new)
            hd["o1"] = both[CHUNK:]
            s_s[c] = (jnp.exp(hd["g_last"]) * s_mat
                      + _mm_tn(_bf(hd["k"] * jnp.exp(hd["g_last"] - hd["gcol"])), _bf(v_new)))
        o2 = _mm(_bf(p["qk"]), _bf(_block_diag2(vn[0], vn[1])))
        for idx, hd in enumerate((ha, hb_)):
            o_ref[:, hd["h"] * DH:(hd["h"] + 1) * DH] = hd["o1"] + o2[:, idx * DH:(idx + 1) * DH]

    @pl.when(jp == P_CHUNKS - 1)
    def _():
        for d in range(2):
            for h in range(HB):
                s_out[0, 0, d, h] = sp_s[d * HB + h]


def _delta_scan(qkv, proj, gates_t2, dt_row, dt2, na_row, na2, s0, s_all, e):
    const2 = lambda s: (0, 0)
    in_specs = (_scan_specs(0)
                + [pl.BlockSpec((1, 128), const2), pl.BlockSpec((16, 128), const2),
                   pl.BlockSpec((1, 128), const2), pl.BlockSpec((16, 128), const2),
                   pl.BlockSpec((1, 1, 2, HB, DH, DH), lambda s: (s // S_CHUNKS, e, 0, 0, 0, 0)),
                   pl.BlockSpec(memory_space=pl.ANY)])
    half = jax.ShapeDtypeStruct((NP_ROWS, HB * DH), F32)
    out_shape = (half, half, half, half, jax.ShapeDtypeStruct((BATCH, N_EVEN, 2, HB, DH, DH), F32))
    out_specs = _scan_out_specs() + [
        pl.BlockSpec((1, 1, 2, HB, DH, DH), lambda s: (s // P_CHUNKS, e, 0, 0, 0, 0))]
    return pl.pallas_call(
        _delta_kernel,
        out_shape=out_shape,
        grid=(SCAN_STEPS,),
        in_specs=in_specs,
        out_specs=out_specs,
        scratch_shapes=[pltpu.VMEM((2 * HB, DH, DH), F32), pltpu.VMEM((2 * HB, DH, DH), F32)],
        input_output_aliases={17: 4},
        compiler_params=_params(1),
        name="delta_scan",
    )(qkv, qkv, qkv, qkv, proj, proj, proj, proj, gates_t2, gates_t2, gates_t2, gates_t2,
      dt_row, dt2, na_row, na2, s0, s_all)


def _residual_ln(x, gate, y, g, b):
    r = DN_ALPHA * x + gate * y
    mu = jnp.mean(r, axis=-1, keepdims=True)
    var = jnp.mean(jnp.square(r - mu), axis=-1, keepdims=True)
    return (r - mu) * lax.rsqrt(var + LN_EPS) * g + b


def _even_out_kernel(*refs, tm, n_x):
    x_refs = refs[:n_x]
    (hpf_ref, hpb_ref, hsf_ref, hsb_ref, opf_ref, opb_ref, osf_ref, osb_ref,
     oa_ref, zb_ref, mg_ref, dg_ref, w_ref, gate_ref, g_ref, b_ref, o_ref) = refs[n_x:]
    is_p = pl.program_id(0) < NP_ROWS // tm
    hf_ref, hb_ref, of_ref, ob_ref = (
        lambda sl, p=p, q=q: jnp.where(is_p, p[:, sl], q[:, sl])
        for p, q in ((hpf_ref, hsf_ref), (hpb_ref, hsb_ref), (opf_ref, osf_ref), (opb_ref, osb_ref)))
    parts = []
    for h in range(HA):
        sl = slice(h * DH, (h + 1) * DH)
        hh = hf_ref(sl) + hb_ref(sl)
        mu = jnp.mean(hh, axis=-1, keepdims=True)
        var = jnp.mean(jnp.square(hh - mu), axis=-1, keepdims=True)
        parts.append(_sigmoid(oa_ref[:, sl]) * ((hh - mu) * lax.rsqrt(var + LN_EPS) * mg_ref[:, sl]))
    for h in range(HB):
        sl = slice(h * DH, (h + 1) * DH)
        oo = of_ref(sl) + ob_ref(sl)
        z = zb_ref[:, sl]
        nrm = oo * lax.rsqrt(jnp.mean(jnp.square(oo), axis=-1, keepdims=True) + LN_EPS) * dg_ref[:, sl]
        parts.append(nrm * (z * _sigmoid(z)))
    a = jnp.concatenate(parts, axis=1)
    y = _mm(_bf(a), w_ref[...])
    o_ref[...] = _residual_ln(_load_x(x_refs, tm), gate_ref[0], y, g_ref[...], b_ref[...])


def _even_out(h_scan, o_scan, proj, mg, dg, w, x, mods, layer, ln_g, ln_b):
    tm = 512
    npt = NP_ROWS // tm
    row512 = lambda i: (i, 0)
    const2 = lambda i: (0, 0)
    p_spec = pl.BlockSpec((tm, 512), lambda i: (jnp.minimum(i, npt - 1), 0))
    s_spec = pl.BlockSpec((tm, 512), lambda i: (jnp.maximum(i - npt, 0), 0))
    x_specs, x_arrays = _x_operands(x, tm)
    return pl.pallas_call(
        functools.partial(_even_out_kernel, tm=tm, n_x=len(x_arrays)),
        out_shape=jax.ShapeDtypeStruct((T, D), F32),
        grid=(T // tm,),
        in_specs=x_specs + [p_spec, p_spec, s_spec, s_spec, p_spec, p_spec, s_spec, s_spec,
                            pl.BlockSpec((tm, 512), lambda i: (i, 6)),
                            pl.BlockSpec((tm, 512), lambda i: (i, 7)),
                            pl.BlockSpec((1, 512), const2), pl.BlockSpec((1, 512), const2),
                            pl.BlockSpec((D, D), const2),
                            _mod_spec(layer, 2, tm),
                            pl.BlockSpec((1, D), const2), pl.BlockSpec((1, D), const2)],
        out_specs=pl.BlockSpec((tm, D), row512),
        compiler_params=_params(1),
        name="even_out_ln",
    )(*x_arrays, *h_scan, *o_scan, proj, proj, mg, dg, w, mods, ln_g, ln_b)


def _odd_out_kernel(ap_ref, as_ref, w_ref, x_ref, gate_ref, g_ref, b_ref, o_ref, *, tm):
    a = jnp.where(pl.program_id(0) < NP_ROWS // tm, ap_ref[...], as_ref[...])
    y = _mm(_bf(a), w_ref[...])
    o_ref[...] = _residual_ln(x_ref[...], gate_ref[0], y, g_ref[...], b_ref[...])


def _odd_out(a_prompt, a_latent, w, x, mods, layer, ln_g, ln_b):
    tm = 512
    npt = NP_ROWS // tm
    row = lambda i: (i, 0)
    const2 = lambda i: (0, 0)
    return pl.pallas_call(
        functools.partial(_odd_out_kernel, tm=tm),
        out_shape=jax.ShapeDtypeStruct((T, D), F32),
        grid=(T // tm,),
        in_specs=[pl.BlockSpec((tm, D), lambda i: (jnp.minimum(i, npt - 1), 0)),
                  pl.BlockSpec((tm, D), lambda i: (jnp.maximum(i - npt, 0), 0)),
                  pl.BlockSpec((D, D), const2), pl.BlockSpec((tm, D), row),
                  _mod_spec(layer, 2, tm), pl.BlockSpec((1, D), const2), pl.BlockSpec((1, D), const2)],
        out_specs=pl.BlockSpec((tm, D), row),
        compiler_params=_params(1),
        name="odd_out_ln",
    )(a_prompt, a_latent, w, x, mods, ln_g, ln_b)


def _attn_ctx_kernel(qkv_ref, sink_ref, o_ref):
    ones = jnp.ones((SEQ, HD), BF16)
    for kv in range(KVH):
        k = _bf(qkv_ref[:, HC * HD + kv * HD:HC * HD + (kv + 1) * HD])
        v = _bf(qkv_ref[:, (HC + KVH) * HD + kv * HD:(HC + KVH) * HD + (kv + 1) * HD])
        v1 = jnp.concatenate([v, ones], axis=1)
        heads = range(kv * (HC // KVH), (kv + 1) * (HC // KVH))
        sts = [_mm_nt(k, _bf(qkv_ref[:, h * HD:(h + 1) * HD])) for h in heads]
        ms = [jnp.maximum(jnp.max(st, axis=0, keepdims=True), sink_ref[h]) for st, h in zip(sts, heads)]
        ovs = [_mm_tn(v1, _bf(jnp.exp(st - m))) for st, m in zip(sts, ms)]
        outs = [ov[:HD] / (ov[HD:HD + 1] + jnp.exp(sink_ref[h] - m)) for ov, m, h in zip(ovs, ms, heads)]
        for pr in range(HC // KVH // 2):
            lane0 = (kv * (HC // KVH) + 2 * pr) * HD
            o_ref[:, lane0:lane0 + 2 * HD] = jnp.concatenate(outs[2 * pr:2 * pr + 2], axis=0).T


def _attn_context(qkv, sink):
    return pl.pallas_call(
        _attn_ctx_kernel,
        out_shape=jax.ShapeDtypeStruct((NP_ROWS, HC * HD), F32),
        grid=(BATCH,),
        in_specs=[pl.BlockSpec((SEQ, QKV_W), lambda b: (b, 0)),
                  pl.BlockSpec(memory_space=pltpu.SMEM)],
        out_specs=pl.BlockSpec((SEQ, HC * HD), lambda b: (b, 0)),
        compiler_params=_params(1),
        name="attn_context",
    )(qkv, sink)


def _attn_lat_kernel(q_ref, kp_ref, kc_ref, kn_ref, vp_ref, vc_ref, vn_ref, ck_ref, cv_ref,
                     sink_ref, o_ref, bias_s):
    j = pl.program_id(1)
    nb = DEC_SEQ // QBLOCK
    cc = lax.broadcasted_iota(jnp.int32, (3 * QBLOCK, QBLOCK), 0)
    r = lax.broadcasted_iota(jnp.int32, (3 * QBLOCK, QBLOCK), 1)
    lo = jnp.where(j >= 1, 0, QBLOCK)
    hi = jnp.where(j <= nb - 2, 3 * QBLOCK, 2 * QBLOCK)
    ok = (jnp.abs(QBLOCK + r - cc) <= WINDOW) & (cc >= lo) & (cc < hi)
    bias_s[...] = jnp.where(ok, 0.0, -jnp.inf)
    for kv in range(KVH):
        ks = slice(kv * HD, (kv + 1) * HD)
        k_all = _bf(jnp.concatenate([ck_ref[0, 0, kv], kp_ref[:, ks], kc_ref[:, ks], kn_ref[:, ks]], axis=0))
        v_all = _bf(jnp.concatenate([cv_ref[0, 0, kv], vp_ref[:, ks], vc_ref[:, ks], vn_ref[:, ks]], axis=0))
        v1 = jnp.concatenate([v_all, jnp.ones_like(v_all)], axis=1)
        outs = []
        for h in range(kv * (HC // KVH), (kv + 1) * (HC // KVH)):
            sink = sink_ref[h]
            st = _mm_nt(k_all, _bf(q_ref[:, h * HD:(h + 1) * HD]))
            s_ctx = st[:PAST_LEN]
            s_loc = st[PAST_LEN:] + bias_s[...]
            m = jnp.maximum(jnp.maximum(jnp.max(s_ctx, axis=0, keepdims=True),
                                        jnp.max(s_loc, axis=0, keepdims=True)), sink)
            p = _bf(jnp.concatenate([jnp.exp(s_ctx - m), jnp.exp(s_loc - m)], axis=0))
            ov = _mm_tn(v1, p)
            outs.append(ov[:HD] / (ov[HD:HD + 1] + jnp.exp(sink - m)))
        for pr in range(HC // KVH // 2):
            lane0 = (kv * (HC // KVH) + 2 * pr) * HD
            o_ref[:, lane0:lane0 + 2 * HD] = jnp.concatenate(outs[2 * pr:2 * pr + 2], axis=0).T


def _attn_latent(qkv, cache_k, cache_v, sink, o):
    nb = DEC_SEQ // QBLOCK
    base = NP_ROWS // QBLOCK
    blk = lambda b, j: base + b * nb + j
    prev = lambda b, j: base + b * nb + jnp.maximum(j - 1, 0)
    nxt = lambda b, j: base + b * nb + jnp.minimum(j + 1, nb - 1)
    kcol, vcol = HC * HD // 256, HC * HD // 256 + 1
    cache_spec = pl.BlockSpec((1, 1, KVH, PAST_LEN, HD), lambda b, j: (b, o, 0, 0, 0))
    return pl.pallas_call(
        _attn_lat_kernel,
        out_shape=jax.ShapeDtypeStruct((NS_ROWS, HC * HD), F32),
        grid=(DEC_BATCH, nb),
        in_specs=[pl.BlockSpec((QBLOCK, HC * HD), lambda b, j: (blk(b, j), 0)),
                  pl.BlockSpec((QBLOCK, 256), lambda b, j: (prev(b, j), kcol)),
                  pl.BlockSpec((QBLOCK, 256), lambda b, j: (blk(b, j), kcol)),
                  pl.BlockSpec((QBLOCK, 256), lambda b, j: (nxt(b, j), kcol)),
                  pl.BlockSpec((QBLOCK, 256), lambda b, j: (prev(b, j), vcol)),
                  pl.BlockSpec((QBLOCK, 256), lambda b, j: (blk(b, j), vcol)),
                  pl.BlockSpec((QBLOCK, 256), lambda b, j: (nxt(b, j), vcol)),
                  cache_spec, cache_spec,
                  pl.BlockSpec(memory_space=pltpu.SMEM)],
        out_specs=pl.BlockSpec((QBLOCK, HC * HD), lambda b, j: (b * nb + j, 0)),
        scratch_shapes=[pltpu.VMEM((3 * QBLOCK, QBLOCK), F32)],
        compiler_params=_params(2),
        name="attn_latent",
    )(qkv, qkv, qkv, qkv, qkv, qkv, qkv, cache_k, cache_v, sink)


def _route(lg):
    lane = lax.broadcasted_iota(jnp.int32, lg.shape, 1)
    neg = -jnp.inf
    big = 1 << 20
    is_grp = jnp.logical_and(lane >= N_EXPERTS, lane < N_EXPERTS + N_GROUPS)
    mg = jnp.max(jnp.where(is_grp, lg, neg), axis=1, keepdims=True)
    g_lane = jnp.min(jnp.where(jnp.logical_and(is_grp, lg == mg), lane, big), axis=1, keepdims=True)
    g_w = 1.0 / jnp.sum(jnp.where(is_grp, jnp.exp(lg - mg), 0.0), axis=1, keepdims=True)
    g_idx = g_lane - N_EXPERTS
    in_grp = jnp.logical_and(lane >= g_idx * EPG, lane < (g_idx + 1) * EPG)
    v1 = jnp.max(jnp.where(in_grp, lg, neg), axis=1, keepdims=True)
    i1 = jnp.min(jnp.where(jnp.logical_and(in_grp, lg == v1), lane, big), axis=1, keepdims=True)
    rest = jnp.logical_and(in_grp, lane != i1)
    v2 = jnp.max(jnp.where(rest, lg, neg), axis=1, keepdims=True)
    i2 = jnp.min(jnp.where(jnp.logical_and(rest, lg == v2), lane, big), axis=1, keepdims=True)
    e2 = jnp.exp(v2 - v1)
    p1 = 1.0 / (1.0 + e2)
    p2 = e2 / (1.0 + e2)
    return jnp.where(lane == 0, i1.astype(F32),
                     jnp.where(lane == 1, i2.astype(F32),
                               jnp.where(lane == 2, p1 * g_w, jnp.where(lane == 3, p2 * g_w, 0.0))))


def _moe_dense_kernel(x_ref, sh_ref, sc_ref, wr_ref, br_ref, wg_ref, wu_ref, wd_ref, gate_ref, g_ref, b_ref,
                      *out_and_scratch, tm):
    *o_refs, xm_s, meta_s, acc_s = out_and_scratch
    grp = pl.program_id(1)

    @pl.when(grp == 0)
    def _():
        xm = x_ref[...] * (1.0 + sc_ref[0]) + sh_ref[0]
        meta_s[...] = _route(_mm(_lhs3(xm), _rhs3(wr_ref[...])) + br_ref[...])
        xm_s[...] = _bf(xm)
        acc_s[...] = jnp.zeros_like(acc_s)

    xm = xm_s[...]
    meta = meta_s[...]
    i1 = meta[:, 0:1].astype(jnp.int32)
    i2 = meta[:, 1:2].astype(jnp.int32)
    w1 = meta[:, 2:3]
    w2 = meta[:, 3:4]
    hid = []
    for e in range(EPG):
        eid = grp * EPG + e
        gate = jnp.where(i1 == eid, w1, 0.0) + jnp.where(i2 == eid, w2, 0.0)
        a = _mm(xm, wg_ref[0, e])
        u = _mm(xm, wu_ref[0, e])
        hid.append(_bf((a * _sigmoid(a)) * u * gate))
    acc_s[...] += _mm(jnp.concatenate(hid, axis=1), wd_ref[0])

    def result():
        return _residual_ln(x_ref[...], gate_ref[0], acc_s[...], g_ref[...], b_ref[...])

    last = grp == N_GROUPS - 1
    if len(o_refs) == 1:
        @pl.when(last)
        def _():
            o_refs[0][...] = result()
    else:
        is_p = pl.program_id(0) < NP_ROWS // tm

        @pl.when(jnp.logical_and(last, is_p))
        def _():
            o_refs[0][...] = result()

        @pl.when(jnp.logical_and(last, jnp.logical_not(is_p)))
        def _():
            o_refs[1][...] = result()


def _moe_dense(x, mods, layer, w_r, b_r, wg, wu, wd, ln_g, ln_b, split_out=False):
    tm = 1024
    npt = NP_ROWS // tm
    row = lambda i, g: (i, 0)
    const2 = lambda i, g: (0, 0)
    if split_out:
        half = jax.ShapeDtypeStruct((NP_ROWS, D), F32)
        out_shape = (half, half)
        out_specs = (pl.BlockSpec((tm, D), lambda i, g: (jnp.minimum(i, npt - 1), 0)),
                     pl.BlockSpec((tm, D), lambda i, g: (jnp.maximum(i - npt, 0), 0)))
    else:
        out_shape = jax.ShapeDtypeStruct((T, D), F32)
        out_specs = pl.BlockSpec((tm, D), row)
    return pl.pallas_call(
        functools.partial(_moe_dense_kernel, tm=tm),
        out_shape=out_shape,
        grid=(T // tm, N_GROUPS),
        in_specs=[pl.BlockSpec((tm, D), row), _mod_spec(layer, 3, tm), _mod_spec(layer, 4, tm),
                  pl.BlockSpec((D, 128), const2), pl.BlockSpec((1, 128), const2),
                  pl.BlockSpec((1, EPG, D, EXPERT_FF), lambda i, g: (g, 0, 0, 0)),
                  pl.BlockSpec((1, EPG, D, EXPERT_FF), lambda i, g: (g, 0, 0, 0)),
                  pl.BlockSpec((1, EPG * EXPERT_FF, D), lambda i, g: (g, 0, 0)),
                  _mod_spec(layer, 5, tm), pl.BlockSpec((1, D), const2), pl.BlockSpec((1, D), const2)],
        out_specs=out_specs,
        scratch_shapes=[pltpu.VMEM((tm, D), BF16), pltpu.VMEM((tm, 128), F32), pltpu.VMEM((tm, D), F32)],
        compiler_params=_params(2, VMEM_LIMIT + (2 * tm * D * 4 if split_out else 0)),
        name="moe_dense",
    )(x, mods, mods, w_r, b_r, wg, wu, wd, mods, ln_g, ln_b)


def _permute_even_w(w):
    a_end = 4 * HA * DH
    g_end = a_end + 4 * HA
    c_end = g_end + 3 * HB * DH
    z_end = c_end + HB * DH
    small = jnp.concatenate([w[:, a_end:g_end], w[:, z_end:]], axis=1)
    pad = jnp.zeros((w.shape[0], 128 - small.shape[1]), w.dtype)
    return jnp.concatenate([w[:, g_end:c_end], w[:, :a_end], w[:, c_end:z_end], small, pad], axis=1)


def _lane_row(vals, offset):
    return jnp.zeros((1, 128), F32).at[0, offset:offset + vals.shape[0]].set(vals.astype(F32))


def _pair_rows(vals, offset):
    cols = jnp.zeros((N_GATE_COLS,), F32).at[offset:offset + vals.shape[0]].set(vals.astype(F32))
    return jnp.repeat(cols.reshape(N_GATE_COLS // 2, 2), CHUNK, axis=1)


def kernel(x_prompt, x_sample, c, c_ctx, state_mlstm_c, state_mlstm_n, state_mlstm_m, state_delta, cache_k, cache_v, w_mod, b_mod, ln_g, ln_b, w_in_even, mlstm_gate_b, mlstm_norm_g, delta_conv_w, delta_a_log, delta_dt_bias, delta_norm_g, w_out_even, w_qkv_odd, attn_sink, w_out_odd, w_grp, b_grp, w_erouter, b_erouter, w_gate, w_up, w_down):
    x = (x_prompt.reshape(NP_ROWS, D), x_sample.reshape(NS_ROWS, D))
    cvecs = jnp.concatenate([c_ctx[None, :], c, jnp.zeros((N_MOD_ROWS - 1 - DEC_BATCH, D), F32)], axis=0)
    mods = _modulation(cvecs, w_mod, b_mod)
    tables = _rope_tables()
    m0_all = jnp.broadcast_to(state_mlstm_m.reshape(DEC_BATCH, N_EVEN, 2 * HA, 1), (DEC_BATCH, N_EVEN, 2 * HA, 128))

    out_mn, out_mm = [], []
    new_mc = jnp.zeros((BATCH, N_EVEN, 2, HA, DH, DH), F32)
    new_ds = jnp.zeros((BATCH, N_EVEN, 2, HB, DH, DH), F32)
    new_k = jnp.zeros((BATCH, N_ODD, KVH, SEQ, HD), F32)
    new_v = jnp.zeros((BATCH, N_ODD, KVH, SEQ, HD), F32)
    for l in range(DEPTH):
        if l % 2 == 0:
            e = l // 2
            proj = _even_proj(x, mods, l, _bf(_permute_even_w(w_in_even[e])))
            gates_t2 = (proj[:, EVEN_W - 128:EVEN_W - 128 + N_GATE_COLS]
                        .reshape(T // CHUNK, CHUNK, N_GATE_COLS).transpose(0, 2, 1)
                        .reshape(T // CHUNK, N_GATE_COLS // 2, 2 * CHUNK))
            gb = mlstm_gate_b[e].reshape(-1)
            *h_scan, new_mc, mn, mm = _mlstm_scan(proj, gates_t2, _lane_row(gb, 0), _pair_rows(gb, 0),
                                                  state_mlstm_c, state_mlstm_n, m0_all, new_mc, e)
            qkv = _delta_prep(proj, delta_conv_w[e])
            dtb = delta_dt_bias[e].reshape(-1)
            nea = -jnp.exp(delta_a_log[e].astype(F32)).reshape(-1)
            *o_scan, new_ds = _delta_scan(qkv, proj, gates_t2, _lane_row(dtb, DECAY_COL0),
                                          _pair_rows(dtb, DECAY_COL0), _lane_row(nea, DECAY_COL0),
                                          _pair_rows(nea, DECAY_COL0), state_delta, new_ds, e)
            x = _even_out(h_scan, o_scan, proj, mlstm_norm_g[e][None, :], delta_norm_g[e][None, :],
                          _bf(w_out_even[e]), x, mods, l, ln_g[l, 0][None, :], ln_b[l, 0][None, :])
            out_mn.append(mn)
            out_mm.append(mm[:, :, 0].reshape(BATCH, 2, HA))
        else:
            o = l // 2
            qkv, new_k, new_v = _odd_proj(x, mods, l, _bf(w_qkv_odd[o]), tables, new_k, new_v, o)
            a_p = _attn_context(qkv, attn_sink[o])
            a_s = _attn_latent(qkv, cache_k, cache_v, attn_sink[o], o)
            x = _odd_out(a_p, a_s, _bf(w_out_odd[o]), x, mods, l, ln_g[l, 0][None, :], ln_b[l, 0][None, :])
        w_r = jnp.concatenate([w_erouter[l].transpose(1, 0, 2).reshape(D, N_EXPERTS), w_grp[l],
                               jnp.zeros((D, 128 - N_EXPERTS - N_GROUPS), F32)], axis=1)
        b_r = jnp.concatenate([b_erouter[l].reshape(-1), b_grp[l],
                               jnp.zeros((128 - N_EXPERTS - N_GROUPS,), F32)])[None, :]
        x = _moe_dense(x, mods, l, w_r, b_r,
                       _bf(w_gate[l]).reshape(N_GROUPS, EPG, D, EXPERT_FF),
                       _bf(w_up[l]).reshape(N_GROUPS, EPG, D, EXPERT_FF),
                       _bf(w_down[l]).reshape(N_GROUPS, EPG * EXPERT_FF, D),
                       ln_g[l, 1][None, :], ln_b[l, 1][None, :], split_out=(l == DEPTH - 1))
    return (x[0].reshape(BATCH, SEQ, D), x[1].reshape(DEC_BATCH, DEC_SEQ, D),
            new_mc, jnp.stack(out_mn, 1), jnp.stack(out_mm, 1), new_ds,
            new_k, new_v)
```

```python
import functools

import numpy as np
import jax
import jax.numpy as jnp
from jax import lax
from jax.experimental import pallas as pl
from jax.experimental.pallas import tpu as pltpu

F32 = jnp.float32
BF16 = jnp.bfloat16

D = 1024
BATCH = 32
SEQ = 256
DEPTH = 4
DEC_BATCH = 2
DEC_SEQ = 4096
PAST_LEN = 512
GRID_W = 64
N_EVEN = 2
N_ODD = 2
HA = 4
HB = 4
DH = 128
CHUNK = 64
HC = 16
KVH = 4
HD = 64
WINDOW = 128
QBLOCK = 128
ROPE_THETA = 10000.0
N_GROUPS = 4
EPG = 4
N_EXPERTS = 16
EXPERT_FF = 256
DN_ALPHA = (2 * DEPTH) ** 0.25
LN_EPS = 1e-5

NP_ROWS = BATCH * SEQ
NS_ROWS = DEC_BATCH * DEC_SEQ
T = NP_ROWS + NS_ROWS
N_MOD_ROWS = 8
EVEN_W = 4224
N_GATE_COLS = 4 * HA + 4 * HB
BETA_COL0 = 4 * HA
DECAY_COL0 = 4 * HA + 2 * HB
QKV_W = (HC + 2 * KVH) * HD

P_CHUNKS = SEQ // CHUNK
S_CHUNKS = DEC_SEQ // CHUNK
P_STEPS = BATCH * P_CHUNKS
S_STEPS = DEC_BATCH * S_CHUNKS
P_BLOCKS = NP_ROWS // CHUNK

VMEM_LIMIT = 48 * 1024 * 1024


def _params(n_axes, vmem_limit=VMEM_LIMIT):
    return pltpu.CompilerParams(dimension_semantics=("arbitrary",) * n_axes,
                                vmem_limit_bytes=vmem_limit)


def _mm(a, b):
    return lax.dot_general(a, b, (((1,), (0,)), ((), ())), preferred_element_type=F32)


def _mm_nt(a, b):
    return lax.dot_general(a, b, (((1,), (1,)), ((), ())), preferred_element_type=F32)


def _mm_tn(a, b):
    return lax.dot_general(a, b, (((0,), (0,)), ((), ())), preferred_element_type=F32)


def _bf(x):
    return x.astype(BF16)


def _sigmoid(x):
    return 1.0 / (1.0 + jnp.exp(-x))


def _softplus(x):
    return jnp.maximum(x, 0.0) + jnp.log1p(jnp.exp(-jnp.abs(x)))


def _log_sigmoid(x):
    return jnp.minimum(x, 0.0) - jnp.log1p(jnp.exp(-jnp.abs(x)))


def _mod_row(tile, tm):
    npt = NP_ROWS // tm
    per = DEC_SEQ // tm
    return jnp.where(tile < npt, 0, 1 + (tile - npt) // per)


def _mod_spec(layer, chunk, tm):
    def imap(i, *_):
        return ((layer * N_MOD_ROWS + _mod_row(i, tm)) * 6 + chunk, 0, 0)
    return pl.BlockSpec((1, 1, D), imap)


def _modulation_kernel(c_ref, w_ref, b_ref, o_ref):
    x = c_ref[...]
    s = x * _sigmoid(x)
    o_ref[0] = _mm(_lhs3(s), _rhs3(w_ref[0])) + b_ref[0]


def _modulation(cvecs, w_mod, b_mod):
    out = pl.pallas_call(
        _modulation_kernel,
        out_shape=jax.ShapeDtypeStruct((DEPTH, N_MOD_ROWS, 6 * D), F32),
        grid=(DEPTH, 6),
        in_specs=[pl.BlockSpec((N_MOD_ROWS, D), lambda l, j: (0, 0)),
                  pl.BlockSpec((1, D, D), lambda l, j: (l, 0, j)),
                  pl.BlockSpec((1, 1, D), lambda l, j: (l * 6 + j, 0, 0))],
        out_specs=pl.BlockSpec((1, N_MOD_ROWS, D), lambda l, j: (l, 0, j)),
        compiler_params=_params(2),
        name="modulation",
    )(cvecs, w_mod, b_mod.reshape(DEPTH * 6, 1, D))
    return out.reshape(DEPTH * N_MOD_ROWS * 6, 1, D)


def _x_operands(x, tm):
    if isinstance(x, tuple):
        npt = NP_ROWS // tm
        return ([pl.BlockSpec((tm, D), lambda i, *_: (jnp.minimum(i, npt - 1), 0)),
                 pl.BlockSpec((tm, D), lambda i, *_: (jnp.maximum(i - npt, 0), 0))], list(x))
    return [pl.BlockSpec((tm, D), lambda i, *_: (i, 0))], [x]


def _load_x(x_refs, tm):
    if len(x_refs) == 1:
        return x_refs[0][...]
    return jnp.where(pl.program_id(0) < NP_ROWS // tm, x_refs[0][...], x_refs[1][...])


def _proj_kernel(*refs, tm, n_x):
    x_refs, (sh_ref, sc_ref, w_ref, o_ref) = refs[:n_x], refs[n_x:]
    xm = _load_x(x_refs, tm) * (1.0 + sc_ref[0]) + sh_ref[0]
    o_ref[...] = _mm(_bf(xm), w_ref[...])


def _even_proj(x, mods, layer, w):
    tm = 512
    x_specs, x_arrays = _x_operands(x, tm)
    return pl.pallas_call(
        functools.partial(_proj_kernel, tm=tm, n_x=len(x_arrays)),
        out_shape=jax.ShapeDtypeStruct((T, EVEN_W), F32),
        grid=(T // tm,),
        in_specs=x_specs + [_mod_spec(layer, 0, tm), _mod_spec(layer, 1, tm),
                            pl.BlockSpec((D, EVEN_W), lambda i: (0, 0), pipeline_mode=pl.Buffered(1))],
        out_specs=pl.BlockSpec((tm, EVEN_W), lambda i: (i, 0)),
        compiler_params=_params(1),
        name="even_proj",
    )(*x_arrays, mods, mods, w)


def _qkv_kernel(x_ref, sh_ref, sc_ref, w_ref, cos_ref, sa_ref, sb_ref, kc_in, vc_in, o_ref, kc_ref, vc_ref, *, tm):
    del kc_in, vc_in
    i = pl.program_id(0)
    xm = x_ref[...] * (1.0 + sc_ref[0]) + sh_ref[0]
    acc = _mm(_bf(xm), w_ref[...])

    @pl.when(i < NP_ROWS // tm)
    def _():
        for b in range(tm // SEQ):
            for kv in range(KVH):
                k0 = HC * HD + kv * HD
                v0 = (HC + KVH) * HD + kv * HD
                kc_ref[b, 0, kv] = acc[b * SEQ:(b + 1) * SEQ, k0:k0 + HD]
                vc_ref[b, 0, kv] = acc[b * SEQ:(b + 1) * SEQ, v0:v0 + HD]

    n_q = HC * HD // 128
    n_k = KVH * HD // 128
    is_latent = i >= NP_ROWS // tm

    def scaled(g):
        blk = acc[:, g * 128:(g + 1) * 128]
        return blk * (HD ** -0.5) if g < n_q else blk

    @pl.when(is_latent)
    def _():
        cos, sa, sb = cos_ref[...], sa_ref[...], sb_ref[...]
        for g in range(n_q + n_k):
            blk = scaled(g)
            o_ref[:, g * 128:(g + 1) * 128] = blk * cos + pltpu.roll(blk, 112, 1) * sa + pltpu.roll(blk, 16, 1) * sb

    @pl.when(jnp.logical_not(is_latent))
    def _():
        for g in range(n_q + n_k):
            o_ref[:, g * 128:(g + 1) * 128] = scaled(g)

    o_ref[:, (n_q + n_k) * 128:] = acc[:, (n_q + n_k) * 128:]


def _rope_tables():
    half = HD // 4
    inv = np.float32(ROPE_THETA) ** (-np.arange(half, dtype=np.float32) / np.float32(half))
    pos = np.arange(DEC_SEQ)
    row = (pos // GRID_W).astype(np.float32)[:, None] * inv[None, :]
    col = (pos % GRID_W).astype(np.float32)[:, None] * inv[None, :]
    cos = np.concatenate([np.cos(row), np.cos(row), np.cos(col), np.cos(col)], axis=-1)
    sin = np.concatenate([np.sin(row), np.sin(row), np.sin(col), np.sin(col)], axis=-1)
    first = (np.arange(HD) % 32) < 16
    sa = np.where(first, -sin, 0.0)
    sb = np.where(first, 0.0, sin)
    tile2 = lambda t: jnp.asarray(np.concatenate([t, t], axis=-1), F32)
    return tile2(cos), tile2(sa), tile2(sb)


def _odd_proj(x, mods, layer, w, tables, cache_k, cache_v, o):
    tm = 512
    npt = NP_ROWS // tm
    per = DEC_SEQ // tm
    tab_spec = pl.BlockSpec((tm, 128), lambda i: (jnp.where(i < npt, 0, (i - npt) % per), 0))
    cache_spec = pl.BlockSpec((tm // SEQ, 1, KVH, SEQ, HD), lambda i: (jnp.minimum(i, npt - 1), o, 0, 0, 0))
    cache_shape = jax.ShapeDtypeStruct((BATCH, N_ODD, KVH, SEQ, HD), F32)
    return pl.pallas_call(
        functools.partial(_qkv_kernel, tm=tm),
        out_shape=(jax.ShapeDtypeStruct((T, QKV_W), F32), cache_shape, cache_shape),
        grid=(T // tm,),
        in_specs=[pl.BlockSpec((tm, D), lambda i: (i, 0)),
                  _mod_spec(layer, 0, tm), _mod_spec(layer, 1, tm),
                  pl.BlockSpec((D, QKV_W), lambda i: (0, 0)),
                  tab_spec, tab_spec, tab_spec,
                  pl.BlockSpec(memory_space=pl.ANY), pl.BlockSpec(memory_space=pl.ANY)],
        out_specs=(pl.BlockSpec((tm, QKV_W), lambda i: (i, 0)), cache_spec, cache_spec),
        input_output_aliases={7: 1, 8: 2},
        compiler_params=_params(1),
        name="odd_qkv_proj",
    )(x, mods, mods, w, *tables, cache_k, cache_v)


PREP_ROWS = 256


def _delta_prep_kernel(x_ref, prev_ref, next_ref, w_ref, o_ref):
    i = pl.program_id(0)
    npb = NP_ROWS // PREP_ROWS
    per = DEC_SEQ // PREP_ROWS
    is_latent = i >= npb
    pos = (i - npb) % per
    has_prev = jnp.logical_and(is_latent, pos > 0)
    has_next = jnp.logical_and(is_latent, pos < per - 1)
    x = x_ref[...]
    w = w_ref[...]
    rows = lax.broadcasted_iota(jnp.int32, x.shape, 0)
    prev_row = jnp.where(has_prev, prev_ref[7:8, :], 0.0)
    next_row = jnp.where(has_next, next_ref[0:1, :], 0.0)
    xm1 = jnp.where(rows == 0, prev_row, pltpu.roll(x, 1, 0))
    xp1 = jnp.where(rows == PREP_ROWS - 1, next_row, pltpu.roll(x, PREP_ROWS - 1, 0))
    y = xm1 * w[0:1, :] + x * w[1:2, :] + xp1 * w[2:3, :]
    y = y * _sigmoid(y)
    for h in range(3 * HB):
        yh = y[:, h * DH:(h + 1) * DH]
        if h < 2 * HB:
            inv = lax.rsqrt(jnp.sum(yh * yh, axis=-1, keepdims=True) + 1e-6)
            yh = yh * (inv * (DH ** -0.5) if h < HB else inv)
        o_ref[:, h * DH:(h + 1) * DH] = yh


def _delta_prep(proj, conv_w):
    nblk = T // PREP_ROWS
    sub = PREP_ROWS // 8
    last8 = T // 8 - 1
    return pl.pallas_call(
        _delta_prep_kernel,
        out_shape=jax.ShapeDtypeStruct((T, 3 * 512), F32),
        grid=(nblk,),
        in_specs=[pl.BlockSpec((PREP_ROWS, 3 * 512), lambda i: (i, 0)),
                  pl.BlockSpec((8, 3 * 512), lambda i: (jnp.maximum(i * sub - 1, 0), 0)),
                  pl.BlockSpec((8, 3 * 512), lambda i: (jnp.minimum((i + 1) * sub, last8), 0)),
                  pl.BlockSpec((3, 3 * 512), lambda i: (0, 0))],
        out_specs=pl.BlockSpec((PREP_ROWS, 3 * 512), lambda i: (i, 0)),
        compiler_params=_params(1),
        name="delta_prep",
    )(proj, proj, proj, conv_w)


assert P_STEPS == S_STEPS
SCAN_STEPS = P_STEPS


def _bwd_local(s, nc):
    return (s // nc) * nc + nc - 1 - s % nc


def _scan_blocks():
    return (lambda s: s, lambda s: _bwd_local(s, P_CHUNKS),
            lambda s: P_BLOCKS + s, lambda s: P_BLOCKS + _bwd_local(s, S_CHUNKS))


def _scan_specs(xcol):
    gcol = EVEN_W // 128 - 1
    blocks = _scan_blocks()
    return ([pl.BlockSpec((CHUNK, 3 * 512), lambda s, f=f: (f(s), xcol)) for f in blocks]
            + [pl.BlockSpec((CHUNK, 128), lambda s, f=f: (f(s), gcol)) for f in blocks]
            + [pl.BlockSpec((1, 16, 128), lambda s, f=f: (f(s), 0, 0)) for f in blocks])


def _scan_out_specs():
    local = (lambda s: s, lambda s: _bwd_local(s, P_CHUNKS), lambda s: s, lambda s: _bwd_local(s, S_CHUNKS))
    return [pl.BlockSpec((CHUNK, 512), lambda s, f=f: (f(s), 0)) for f in local]


def _mlstm_kernel(xpf_ref, xpb_ref, xsf_ref, xsb_ref, gpf_ref, gpb_ref, gsf_ref, gsb_ref,
                  rpf_ref, rpb_ref, rsf_ref, rsb_ref,
                  brow_ref, b2_ref, c0_ref, n0_ref, m0_ref, c_all_in,
                  hpf_ref, hpb_ref, hsf_ref, hsb_ref, c_out, n_out, m_out,
                  cp_s, np_s, mp_s, cs_s, ns_s, ms_s):
    del c_all_in
    s = pl.program_id(0)
    jp = s % P_CHUNKS
    js = s % S_CHUNKS

    @pl.when(jp == 0)
    def _():
        cp_s[...] = jnp.zeros_like(cp_s)
        np_s[...] = jnp.zeros_like(np_s)
        mp_s[...] = jnp.zeros_like(mp_s)

    @pl.when(js == 0)
    def _():
        for d in range(2):
            for h in range(HA):
                cs_s[d * HA + h] = c0_ref[0, 0, d, h]
                ns_s[d * HA + h] = jnp.broadcast_to(n0_ref[0, 0, d, h:h + 1, :], (DH, 128)).T
        ms_s[...] = m0_ref[0, 0]

    states = ((cp_s, np_s, mp_s), (cs_s, ns_s, ms_s))

    row = lax.broadcasted_iota(jnp.int32, (CHUNK, 128), 0)
    lane = lax.broadcasted_iota(jnp.int32, (CHUNK, 128), 1)
    left = lane < CHUNK
    lcol = jnp.where(left, lane, lane - CHUNK)
    r64 = lax.broadcasted_iota(jnp.int32, (CHUNK, CHUNK), 0)
    c64 = lax.broadcasted_iota(jnp.int32, (CHUNK, CHUNK), 1)
    r128 = lax.broadcasted_iota(jnp.int32, (128, 128), 0)
    c128 = lax.broadcasted_iota(jnp.int32, (128, 128), 1)
    same_half = (r128 < CHUNK) == (c128 < CHUNK)
    neg = -jnp.inf
    ins = ((xpf_ref, gpf_ref, rpf_ref, hpf_ref), (xpb_ref, gpb_ref, rpb_ref, hpb_ref),
           (xsf_ref, gsf_ref, rsf_ref, hsf_ref), (xsb_ref, gsb_ref, rsb_ref, hsb_ref))

    sel_r = lax.broadcasted_iota(jnp.int32, (128, HA * 128), 0)
    sel_h = lax.broadcasted_iota(jnp.int32, (128, HA * 128), 1) // 128
    ones_lr = lax.broadcasted_iota(jnp.int32, (256, 256), 0)
    ones_lc = lax.broadcasted_iota(jnp.int32, (256, 256), 1)
    half_sum = (((ones_lr % 128) < CHUNK) == (ones_lc < 128)).astype(BF16)
    ones_l = jnp.ones((CHUNK, 128), BF16)
    rows256 = lax.broadcasted_iota(jnp.int32, (CHUNK, 256), 0)

    def running_max(x, d):
        sh = 1
        while sh < CHUNK:
            if d == 0:
                x = jnp.maximum(x, jnp.where(rows256 >= sh, pltpu.roll(x, sh, 0), neg))
            else:
                x = jnp.maximum(x, jnp.where(rows256 < CHUNK - sh, pltpu.roll(x, CHUNK - sh, 0), neg))
            sh *= 2
        return x

    gate = []
    for u in range(4):
        d = u % 2
        gc_ref, gr_ref = ins[u][1], ins[u][2]
        if d == 0:
            incl_p, tri_c, tri_r = lcol <= row, c64 <= r64, jnp.logical_and(same_half, r128 <= c128)
        else:
            incl_p, tri_c, tri_r = lcol >= row, c64 >= r64, jnp.logical_and(same_half, r128 >= c128)
        gc = gc_ref[...] + brow_ref[...]
        gr = gr_ref[0] + b2_ref[...]
        i_rep = _replicate(gc, (sel_r == d * 2 * HA + sel_h).astype(BF16))
        b_rep = _replicate(_prefix_cols(tri_c.astype(BF16), _log_sigmoid(gc)),
                           (sel_r == d * 2 * HA + HA + sel_h).astype(BF16))
        cs_row = _prefix_rows(_log_sigmoid(gr), tri_r.astype(BF16))
        gate.append((incl_p, gr, cs_row, i_rep, b_rep))

    st = []
    for u in range(4):
        d = u % 2
        c_s, n_s, m_s = states[u // 2]
        for hp in range(HA // 2):
            x_ref = ins[u][0]
            incl_p, gr, cs_row, i_rep, b_rep = gate[u]
            end = CHUNK - 1 if d == 0 else 0
            heads = []
            for h in (2 * hp, 2 * hp + 1):
                c = d * HA + h
                heads.append(dict(c=c, h=h, q=x_ref[:, h * DH:(h + 1) * DH],
                                  k=x_ref[:, 512 + h * DH:512 + (h + 1) * DH] * (DH ** -0.5),
                                  v=x_ref[:, 1024 + h * DH:1024 + (h + 1) * DH],
                                  i=i_rep[:, h * 128:(h + 1) * 128], b=b_rep[:, h * 128:(h + 1) * 128],
                                  m=m_s[c:c + 1, :]))
            ha, hb_ = heads
            ri = d * HA + hp
            rf = d * HA + HA // 2 + hp
            i_row = gr[ri:ri + 1, :]
            b_row = cs_row[rf:rf + 1, :]
            run = running_max(jnp.concatenate([ha["i"] - ha["b"], hb_["i"] - hb_["b"]], axis=1), d)
            for idx, hd in enumerate(heads):
                top = jnp.maximum(hd["m"], run[:, idx * 128:(idx + 1) * 128])
                hd["m_t"] = hd["b"] + top
                hd["w_inter"] = jnp.exp(hd["m"] - top)
            b_p = jnp.where(left, ha["b"], hb_["b"])
            m_t_p = jnp.where(left, ha["m_t"], hb_["m_t"])
            dmat = jnp.where(incl_p, b_p - b_row + i_row, neg)
            q_cat = _bf(jnp.concatenate([ha["q"], hb_["q"]], axis=1))
            k_bd = _bf(_block_diag2(ha["k"], hb_["k"]))
            st.append(dict(u=u, d=d, heads=heads, sc=_mm_nt(q_cat, k_bd) * jnp.exp(dmat - m_t_p)))

    for p in st:
        ha, hb_ = p["heads"]
        h_ref = ins[p["u"]][3]
        c_s, n_s, m_s = states[p["u"] // 2]
        end = CHUNK - 1 if p["d"] == 0 else 0
        sc = p["sc"]
        sv = _mm(_bf(sc), _bf(_block_diag2(ha["v"], hb_["v"])))
        dens = _mm(jnp.concatenate(_split2(sc), axis=1), half_sum)
        for idx, hd in enumerate((ha, hb_)):
            c, h, q, k = hd["c"], hd["h"], hd["q"], hd["k"]
            c_mat = c_s[c]
            n_mat = n_s[c]
            q_cn = _mm(_bf(q), _bf(jnp.concatenate([c_mat, n_mat], axis=1)))
            num = sv[:, idx * DH:(idx + 1) * DH] + hd["w_inter"] * q_cn[:, :DH]
            den = dens[:, idx * 128:(idx + 1) * 128] + hd["w_inter"] * q_cn[:, DH:]
            h_ref[:, h * DH:(h + 1) * DH] = num / jnp.maximum(jnp.abs(den), jnp.exp(-hd["m_t"]))
            b_last = hd["b"][end:end + 1, :]
            g_end = b_last - hd["b"] + hd["i"]
            m_new = jnp.maximum(b_last + hd["m"], jnp.max(g_end, axis=0, keepdims=True))
            kwb = _bf(k * jnp.exp(g_end - m_new))
            decay = jnp.exp(b_last + hd["m"] - m_new)
            upd = _mm_tn(kwb, jnp.concatenate([_bf(hd["v"]), ones_l], axis=1))
            c_s[c] = decay * c_mat + upd[:, :DH]
            n_s[c] = decay * n_mat + upd[:, DH:]
            m_s[c:c + 1, :] = m_new

    @pl.when(jp == P_CHUNKS - 1)
    def _():
        for d in range(2):
            for h in range(HA):
                c_out[0, 0, d, h] = cp_s[d * HA + h]
                n_out[0, d, h:h + 1, :] = np_s[d * HA + h].T[0:1, :]
        m_out[0] = mp_s[...]


def _mlstm_scan(proj, gates_t2, bias_row, bias2, c0, n0, m0, c_all, e):
    const2 = lambda s: (0, 0)
    in_specs = (_scan_specs(1)
                + [pl.BlockSpec((1, 128), const2), pl.BlockSpec((16, 128), const2),
                   pl.BlockSpec((1, 1, 2, HA, DH, DH), lambda s: (s // S_CHUNKS, e, 0, 0, 0, 0)),
                   pl.BlockSpec((1, 1, 2, HA, DH), lambda s: (s // S_CHUNKS, e, 0, 0, 0)),
                   pl.BlockSpec((1, 1, 2 * HA, 128), lambda s: (s // S_CHUNKS, e, 0, 0)),
                   pl.BlockSpec(memory_space=pl.ANY)])
    half = jax.ShapeDtypeStruct((NP_ROWS, HA * DH), F32)
    out_shape = (half, half, half, half,
                 jax.ShapeDtypeStruct((BATCH, N_EVEN, 2, HA, DH, DH), F32),
                 jax.ShapeDtypeStruct((BATCH, 2, HA, DH), F32),
                 jax.ShapeDtypeStruct((BATCH, 2 * HA, 128), F32))
    out_specs = _scan_out_specs() + [
        pl.BlockSpec((1, 1, 2, HA, DH, DH), lambda s: (s // P_CHUNKS, e, 0, 0, 0, 0)),
        pl.BlockSpec((1, 2, HA, DH), lambda s: (s // P_CHUNKS, 0, 0, 0)),
        pl.BlockSpec((1, 2 * HA, 128), lambda s: (s // P_CHUNKS, 0, 0))]
    state = [pltpu.VMEM((2 * HA, DH, DH), F32), pltpu.VMEM((2 * HA, DH, 128), F32), pltpu.VMEM((2 * HA, 128), F32)]
    return pl.pallas_call(
        _mlstm_kernel,
        out_shape=out_shape,
        grid=(SCAN_STEPS,),
        in_specs=in_specs,
        out_specs=out_specs,
        scratch_shapes=state + state,
        input_output_aliases={17: 4},
        compiler_params=_params(1),
        name="mlstm_scan",
    )(proj, proj, proj, proj, proj, proj, proj, proj, gates_t2, gates_t2, gates_t2, gates_t2,
      bias_row, bias2, c0, n0, m0, c_all)


def _split2(x):
    hi = _bf(x)
    return hi, _bf(x - hi.astype(F32))


def _split3(x):
    h1 = _bf(x)
    r1 = x - h1.astype(F32)
    h2 = _bf(r1)
    return h1, h2, _bf(r1 - h2.astype(F32))


def _lhs3(x):
    hi, lo = _split2(x)
    return jnp.concatenate([hi, lo, hi], axis=1)


def _rhs3(x):
    hi, lo = _split2(x)
    return jnp.concatenate([hi, hi, lo], axis=0)


def _prefix_cols(tri_bf, x):
    n = x.shape[1]
    r = _mm(tri_bf, jnp.concatenate(_split3(x), axis=1))
    return r[:, :n] + r[:, n:2 * n] + r[:, 2 * n:]


def _replicate(x, sel):
    return _mm(jnp.concatenate(_split3(x), axis=1), jnp.concatenate([sel, sel, sel], axis=0))


def _prefix_rows(x, tri_bf):
    m = x.shape[0]
    r = _mm(jnp.concatenate(_split3(x), axis=0), tri_bf)
    return r[:m] + r[m:2 * m] + r[2 * m:]


LEVELS = tuple(range(6))


def _block_diag2(a, b):
    z = jnp.zeros_like(a)
    return jnp.concatenate([jnp.concatenate([a, z], axis=1), jnp.concatenate([z, b], axis=1)], axis=0)


def _delta_kernel(xpf_ref, xpb_ref, xsf_ref, xsb_ref, gpf_ref, gpb_ref, gsf_ref, gsb_ref,
                  rpf_ref, rpb_ref, rsf_ref, rsb_ref,
                  dtrow_ref, dt2_ref, narow_ref, na2_ref, s0_ref, s_all_in,
                  opf_ref, opb_ref, osf_ref, osb_ref, s_out, sp_s, ss_s):
    del s_all_in
    s = pl.program_id(0)
    jp = s % P_CHUNKS
    js = s % S_CHUNKS

    @pl.when(jp == 0)
    def _():
        sp_s[...] = jnp.zeros_like(sp_s)

    @pl.when(js == 0)
    def _():
        for d in range(2):
            for h in range(HB):
                ss_s[d * HB + h] = s0_ref[0, 0, d, h]

    states = (sp_s, ss_s)

    row = lax.broadcasted_iota(jnp.int32, (CHUNK, 128), 0)
    lane = lax.broadcasted_iota(jnp.int32, (CHUNK, 128), 1)
    left = lane < CHUNK
    lcol = jnp.where(left, lane, lane - CHUNK)
    eye_p = (lcol == row).astype(F32)
    r64 = lax.broadcasted_iota(jnp.int32, (CHUNK, CHUNK), 0)
    c64 = lax.broadcasted_iota(jnp.int32, (CHUNK, CHUNK), 1)
    r128 = lax.broadcasted_iota(jnp.int32, (128, 128), 0)
    c128 = lax.broadcasted_iota(jnp.int32, (128, 128), 1)
    same_half = (r128 < CHUNK) == (c128 < CHUNK)
    ins = ((xpf_ref, gpf_ref, rpf_ref, opf_ref), (xpb_ref, gpb_ref, rpb_ref, opb_ref),
           (xsf_ref, gsf_ref, rsf_ref, osf_ref), (xsb_ref, gsb_ref, rsb_ref, osb_ref))

    gate = []
    for u in range(4):
        d = u % 2
        gc_ref, gr_ref = ins[u][1], ins[u][2]
        if d == 0:
            incl_p, strict_p = lcol <= row, lcol < row
            tri_c, tri_r = c64 <= r64, jnp.logical_and(same_half, r128 <= c128)
        else:
            incl_p, strict_p = lcol >= row, lcol > row
            tri_c, tri_r = c64 >= r64, jnp.logical_and(same_half, r128 >= c128)
        xc = gc_ref[...]
        xr = gr_ref[0]
        la_c = narow_ref[...] * _softplus(xc + dtrow_ref[...])
        la_r = na2_ref[...] * _softplus(xr + dt2_ref[...])
        g_c = _prefix_cols(tri_c.astype(BF16), la_c)
        g_r = _prefix_rows(la_r, tri_r.astype(BF16))
        gate.append((incl_p, strict_p, _sigmoid(xc), g_c, g_r))

    def bd_rhs(hi, lo, mask=None):
        top = left if mask is None else jnp.logical_and(left, mask)
        bot = jnp.logical_not(left) if mask is None else jnp.logical_and(jnp.logical_not(left), mask)
        zero = jnp.zeros_like(hi)
        blocks = [jnp.concatenate([jnp.where(top, x, zero), jnp.where(bot, x, zero)], axis=0) for x in (hi, lo)]
        return jnp.concatenate([blocks[0], blocks[0], blocks[1]], axis=0)

    off = ([], [])
    for lv in LEVELS:
        same = jnp.right_shift(row, lv + 1) == jnp.right_shift(lcol, lv + 1)
        r_hi = jnp.bitwise_and(jnp.right_shift(row, lv), 1) == 1
        c_hi = jnp.bitwise_and(jnp.right_shift(lcol, lv), 1) == 1
        off[0].append(same & r_hi & jnp.logical_not(c_hi))
        off[1].append(same & jnp.logical_not(r_hi) & c_hi)

    pairs = [(u, hp) for u in range(4) for hp in range(HB // 2)]
    st = []
    for u, hp in pairs:
        d = u % 2
        x_ref = ins[u][0]
        incl_p, strict_p, beta_c, g_c, g_r = gate[u]
        end = CHUNK - 1 if d == 0 else 0
        heads = []
        for h in (2 * hp, 2 * hp + 1):
            ib = BETA_COL0 + d * HB + h
            ia = DECAY_COL0 + d * HB + h
            gcol = g_c[:, ia:ia + 1]
            bc = beta_c[:, ib:ib + 1]
            q = x_ref[:, h * DH:(h + 1) * DH]
            k = x_ref[:, 512 + h * DH:512 + (h + 1) * DH]
            v = x_ref[:, 1024 + h * DH:1024 + (h + 1) * DH]
            heads.append(dict(h=h, gcol=gcol, bc=bc, q=q, k=k, v=v, kb=k * bc, eg=jnp.exp(gcol),
                              g_last=gcol[end:end + 1, :]))
        ha, hb_ = heads
        r = (DECAY_COL0 + d * HB) // 2 + hp
        gcol_p = jnp.where(left, ha["gcol"], hb_["gcol"])
        decay = jnp.exp(jnp.where(incl_p, gcol_p - g_r[r:r + 1, :], -jnp.inf))
        k_bd = _bf(_block_diag2(ha["k"], hb_["k"]))
        kb_cat = _bf(jnp.concatenate([ha["kb"], hb_["kb"]], axis=1))
        q_cat = _bf(jnp.concatenate([ha["q"], hb_["q"]], axis=1))
        kq = _mm_nt(jnp.concatenate([kb_cat, q_cat], axis=0), k_bd)
        a_mat = jnp.where(strict_p, kq[:CHUNK] * decay, 0.0)
        qk = kq[CHUNK:] * decay
        a_hi, a_lo = _split2(a_mat)
        st.append(dict(u=u, d=d, heads=heads, t=eye_p - jnp.where(off[d][0], a_mat, 0.0), qk=qk,
                       am=[bd_rhs(a_hi, a_lo, m) for m in off[d][1:]]))

    for li in range(len(LEVELS) - 1):
        for p in st:
            p["t_parts"] = _split2(p["t"])
            t_hi, t_lo = p["t_parts"]
            p["w"] = _mm(jnp.concatenate([t_hi, t_lo, t_hi], axis=1), p["am"][li])
        for p in st:
            p["t"] = p["t"] - _mm(_lhs3(p["w"]), bd_rhs(*p["t_parts"]))

    for p in st:
        ha, hb_ = p["heads"]
        o_ref = ins[p["u"]][3]
        s_s = states[p["u"] // 2]
        rhs_a = jnp.concatenate([ha["v"] * ha["bc"], ha["kb"] * ha["eg"]], axis=1)
        rhs_b = jnp.concatenate([hb_["v"] * hb_["bc"], hb_["kb"] * hb_["eg"]], axis=1)
        (a_hi, a_lo), (b_hi, b_lo) = _split2(rhs_a), _split2(rhs_b)
        bd_hi, bd_lo = _block_diag2(a_hi, b_hi), _block_diag2(a_lo, b_lo)
        sol = _mm(_lhs3(p["t"]), jnp.concatenate([bd_hi, bd_hi, bd_lo], axis=0))
        vn = []
        for idx, hd in enumerate((ha, hb_)):
            c = p["d"] * HB + hd["h"]
            s_mat = s_s[c]
            sbf = _bf(s_mat)
            so = sol[:, idx * 2 * DH:(idx + 1) * 2 * DH]
            both = _mm(_bf(jnp.concatenate([so[:, DH:], hd["q"] * hd["eg"]], axis=0)), sbf)
            v_new = so[:, :DH] - both[:CHUNK]
            vn.append(v_new)
            hd["o1"] = both[CHUNK:]
            s_s[c] = (jnp.exp(hd["g_last"]) * s_mat
                      + _mm_tn(_bf(hd["k"] * jnp.exp(hd["g_last"] - hd["gcol"])), _bf(v_new)))
        o2 = _mm(_bf(p["qk"]), _bf(_block_diag2(vn[0], vn[1])))
        for idx, hd in enumerate((ha, hb_)):
            o_ref[:, hd["h"] * DH:(hd["h"] + 1) * DH] = hd["o1"] + o2[:, idx * DH:(idx + 1) * DH]

    @pl.when(jp == P_CHUNKS - 1)
    def _():
        for d in range(2):
            for h in range(HB):
                s_out[0, 0, d, h] = sp_s[d * HB + h]


def _delta_scan(qkv, proj, gates_t2, dt_row, dt2, na_row, na2, s0, s_all, e):
    const2 = lambda s: (0, 0)
    in_specs = (_scan_specs(0)
                + [pl.BlockSpec((1, 128), const2), pl.BlockSpec((16, 128), const2),
                   pl.BlockSpec((1, 128), const2), pl.BlockSpec((16, 128), const2),
                   pl.BlockSpec((1, 1, 2, HB, DH, DH), lambda s: (s // S_CHUNKS, e, 0, 0, 0, 0)),
                   pl.BlockSpec(memory_space=pl.ANY)])
    half = jax.ShapeDtypeStruct((NP_ROWS, HB * DH), F32)
    out_shape = (half, half, half, half, jax.ShapeDtypeStruct((BATCH, N_EVEN, 2, HB, DH, DH), F32))
    out_specs = _scan_out_specs() + [
        pl.BlockSpec((1, 1, 2, HB, DH, DH), lambda s: (s // P_CHUNKS, e, 0, 0, 0, 0))]
    return pl.pallas_call(
        _delta_kernel,
        out_shape=out_shape,
        grid=(SCAN_STEPS,),
        in_specs=in_specs,
        out_specs=out_specs,
        scratch_shapes=[pltpu.VMEM((2 * HB, DH, DH), F32), pltpu.VMEM((2 * HB, DH, DH), F32)],
        input_output_aliases={17: 4},
        compiler_params=_params(1),
        name="delta_scan",
    )(qkv, qkv, qkv, qkv, proj, proj, proj, proj, gates_t2, gates_t2, gates_t2, gates_t2,
      dt_row, dt2, na_row, na2, s0, s_all)


def _residual_ln(x, gate, y, g, b):
    r = DN_ALPHA * x + gate * y
    mu = jnp.mean(r, axis=-1, keepdims=True)
    var = jnp.mean(jnp.square(r - mu), axis=-1, keepdims=True)
    return (r - mu) * lax.rsqrt(var + LN_EPS) * g + b


def _even_out_kernel(*refs, tm, n_x):
    x_refs = refs[:n_x]
    (hpf_ref, hpb_ref, hsf_ref, hsb_ref, opf_ref, opb_ref, osf_ref, osb_ref,
     oa_ref, zb_ref, mg_ref, dg_ref, w_ref, gate_ref, g_ref, b_ref, o_ref) = refs[n_x:]
    is_p = pl.program_id(0) < NP_ROWS // tm
    hf_ref, hb_ref, of_ref, ob_ref = (
        lambda sl, p=p, q=q: jnp.where(is_p, p[:, sl], q[:, sl])
        for p, q in ((hpf_ref, hsf_ref), (hpb_ref, hsb_ref), (opf_ref, osf_ref), (opb_ref, osb_ref)))
    parts = []
    for h in range(HA):
        sl = slice(h * DH, (h + 1) * DH)
        hh = hf_ref(sl) + hb_ref(sl)
        mu = jnp.mean(hh, axis=-1, keepdims=True)
        var = jnp.mean(jnp.square(hh - mu), axis=-1, keepdims=True)
        parts.append(_sigmoid(oa_ref[:, sl]) * ((hh - mu) * lax.rsqrt(var + LN_EPS) * mg_ref[:, sl]))
    for h in range(HB):
        sl = slice(h * DH, (h + 1) * DH)
        oo = of_ref(sl) + ob_ref(sl)
        z = zb_ref[:, sl]
        nrm = oo * lax.rsqrt(jnp.mean(jnp.square(oo), axis=-1, keepdims=True) + LN_EPS) * dg_ref[:, sl]
        parts.append(nrm * (z * _sigmoid(z)))
    a = jnp.concatenate(parts, axis=1)
    y = _mm(_bf(a), w_ref[...])
    o_ref[...] = _residual_ln(_load_x(x_refs, tm), gate_ref[0], y, g_ref[...], b_ref[...])


def _even_out(h_scan, o_scan, proj, mg, dg, w, x, mods, layer, ln_g, ln_b):
    tm = 512
    npt = NP_ROWS // tm
    row512 = lambda i: (i, 0)
    const2 = lambda i: (0, 0)
    p_spec = pl.BlockSpec((tm, 512), lambda i: (jnp.minimum(i, npt - 1), 0))
    s_spec = pl.BlockSpec((tm, 512), lambda i: (jnp.maximum(i - npt, 0), 0))
    x_specs, x_arrays = _x_operands(x, tm)
    return pl.pallas_call(
        functools.partial(_even_out_kernel, tm=tm, n_x=len(x_arrays)),
        out_shape=jax.ShapeDtypeStruct((T, D), F32),
        grid=(T // tm,),
        in_specs=x_specs + [p_spec, p_spec, s_spec, s_spec, p_spec, p_spec, s_spec, s_spec,
                            pl.BlockSpec((tm, 512), lambda i: (i, 6)),
                            pl.BlockSpec((tm, 512), lambda i: (i, 7)),
                            pl.BlockSpec((1, 512), const2), pl.BlockSpec((1, 512), const2),
                            pl.BlockSpec((D, D), const2),
                            _mod_spec(layer, 2, tm),
                            pl.BlockSpec((1, D), const2), pl.BlockSpec((1, D), const2)],
        out_specs=pl.BlockSpec((tm, D), row512),
        compiler_params=_params(1),
        name="even_out_ln",
    )(*x_arrays, *h_scan, *o_scan, proj, proj, mg, dg, w, mods, ln_g, ln_b)


def _odd_out_kernel(ap_ref, as_ref, w_ref, x_ref, gate_ref, g_ref, b_ref, o_ref, *, tm):
    a = jnp.where(pl.program_id(0) < NP_ROWS // tm, ap_ref[...], as_ref[...])
    y = _mm(_bf(a), w_ref[...])
    o_ref[...] = _residual_ln(x_ref[...], gate_ref[0], y, g_ref[...], b_ref[...])


def _odd_out(a_prompt, a_latent, w, x, mods, layer, ln_g, ln_b):
    tm = 1024
    npt = NP_ROWS // tm
    row = lambda i: (i, 0)
    const2 = lambda i: (0, 0)
    return pl.pallas_call(
        functools.partial(_odd_out_kernel, tm=tm),
        out_shape=jax.ShapeDtypeStruct((T, D), F32),
        grid=(T // tm,),
        in_specs=[pl.BlockSpec((tm, D), lambda i: (jnp.minimum(i, npt - 1), 0)),
                  pl.BlockSpec((tm, D), lambda i: (jnp.maximum(i - npt, 0), 0)),
                  pl.BlockSpec((D, D), const2), pl.BlockSpec((tm, D), row),
                  _mod_spec(layer, 2, tm), pl.BlockSpec((1, D), const2), pl.BlockSpec((1, D), const2)],
        out_specs=pl.BlockSpec((tm, D), row),
        compiler_params=_params(1),
        name="odd_out_ln",
    )(a_prompt, a_latent, w, x, mods, ln_g, ln_b)


def _attn_ctx_kernel(qkv_ref, sink_ref, o_ref):
    ones = jnp.ones((SEQ, HD), BF16)
    for kv in range(KVH):
        k = _bf(qkv_ref[:, HC * HD + kv * HD:HC * HD + (kv + 1) * HD])
        v = _bf(qkv_ref[:, (HC + KVH) * HD + kv * HD:(HC + KVH) * HD + (kv + 1) * HD])
        v1 = jnp.concatenate([v, ones], axis=1)
        heads = range(kv * (HC // KVH), (kv + 1) * (HC // KVH))
        sts = [_mm_nt(k, _bf(qkv_ref[:, h * HD:(h + 1) * HD])) for h in heads]
        ms = [jnp.maximum(jnp.max(st, axis=0, keepdims=True), sink_ref[h]) for st, h in zip(sts, heads)]
        ovs = [_mm_tn(v1, _bf(jnp.exp(st - m))) for st, m in zip(sts, ms)]
        outs = [ov[:HD] / (ov[HD:HD + 1] + jnp.exp(sink_ref[h] - m)) for ov, m, h in zip(ovs, ms, heads)]
        for pr in range(HC // KVH // 2):
            lane0 = (kv * (HC // KVH) + 2 * pr) * HD
            o_ref[:, lane0:lane0 + 2 * HD] = jnp.concatenate(outs[2 * pr:2 * pr + 2], axis=0).T


def _attn_context(qkv, sink):
    return pl.pallas_call(
        _attn_ctx_kernel,
        out_shape=jax.ShapeDtypeStruct((NP_ROWS, HC * HD), F32),
        grid=(BATCH,),
        in_specs=[pl.BlockSpec((SEQ, QKV_W), lambda b: (b, 0)),
                  pl.BlockSpec(memory_space=pltpu.SMEM)],
        out_specs=pl.BlockSpec((SEQ, HC * HD), lambda b: (b, 0)),
        compiler_params=_params(1),
        name="attn_context",
    )(qkv, sink)


def _attn_lat_kernel(q_ref, kp_ref, kc_ref, kn_ref, vp_ref, vc_ref, vn_ref, ck_ref, cv_ref,
                     sink_ref, o_ref, bias_s):
    j = pl.program_id(1)
    nb = DEC_SEQ // QBLOCK
    cc = lax.broadcasted_iota(jnp.int32, (3 * QBLOCK, QBLOCK), 0)
    r = lax.broadcasted_iota(jnp.int32, (3 * QBLOCK, QBLOCK), 1)
    lo = jnp.where(j >= 1, 0, QBLOCK)
    hi = jnp.where(j <= nb - 2, 3 * QBLOCK, 2 * QBLOCK)
    ok = (jnp.abs(QBLOCK + r - cc) <= WINDOW) & (cc >= lo) & (cc < hi)
    bias_s[...] = jnp.where(ok, 0.0, -jnp.inf)
    for kv in range(KVH):
        ks = slice(kv * HD, (kv + 1) * HD)
        k_all = _bf(jnp.concatenate([ck_ref[0, 0, kv], kp_ref[:, ks], kc_ref[:, ks], kn_ref[:, ks]], axis=0))
        v_all = _bf(jnp.concatenate([cv_ref[0, 0, kv], vp_ref[:, ks], vc_ref[:, ks], vn_ref[:, ks]], axis=0))
        v1 = jnp.concatenate([v_all, jnp.ones_like(v_all)], axis=1)
        outs = []
        for h in range(kv * (HC // KVH), (kv + 1) * (HC // KVH)):
            sink = sink_ref[h]
            st = _mm_nt(k_all, _bf(q_ref[:, h * HD:(h + 1) * HD]))
            s_ctx = st[:PAST_LEN]
            s_loc = st[PAST_LEN:] + bias_s[...]
            m = jnp.maximum(jnp.maximum(jnp.max(s_ctx, axis=0, keepdims=True),
                                        jnp.max(s_loc, axis=0, keepdims=True)), sink)
            p = _bf(jnp.concatenate([jnp.exp(s_ctx - m), jnp.exp(s_loc - m)], axis=0))
            ov = _mm_tn(v1, p)
            outs.append(ov[:HD] / (ov[HD:HD + 1] + jnp.exp(sink - m)))
        for pr in range(HC // KVH // 2):
            lane0 = (kv * (HC // KVH) + 2 * pr) * HD
            o_ref[:, lane0:lane0 + 2 * HD] = jnp.concatenate(outs[2 * pr:2 * pr + 2], axis=0).T


def _attn_latent(qkv, cache_k, cache_v, sink, o):
    nb = DEC_SEQ // QBLOCK
    base = NP_ROWS // QBLOCK
    blk = lambda b, j: base + b * nb + j
    prev = lambda b, j: base + b * nb + jnp.maximum(j - 1, 0)
    nxt = lambda b, j: base + b * nb + jnp.minimum(j + 1, nb - 1)
    kcol, vcol = HC * HD // 256, HC * HD // 256 + 1
    cache_spec = pl.BlockSpec((1, 1, KVH, PAST_LEN, HD), lambda b, j: (b, o, 0, 0, 0))
    return pl.pallas_call(
        _attn_lat_kernel,
        out_shape=jax.ShapeDtypeStruct((NS_ROWS, HC * HD), F32),
        grid=(DEC_BATCH, nb),
        in_specs=[pl.BlockSpec((QBLOCK, HC * HD), lambda b, j: (blk(b, j), 0)),
                  pl.BlockSpec((QBLOCK, 256), lambda b, j: (prev(b, j), kcol)),
                  pl.BlockSpec((QBLOCK, 256), lambda b, j: (blk(b, j), kcol)),
                  pl.BlockSpec((QBLOCK, 256), lambda b, j: (nxt(b, j), kcol)),
                  pl.BlockSpec((QBLOCK, 256), lambda b, j: (prev(b, j), vcol)),
                  pl.BlockSpec((QBLOCK, 256), lambda b, j: (blk(b, j), vcol)),
                  pl.BlockSpec((QBLOCK, 256), lambda b, j: (nxt(b, j), vcol)),
                  cache_spec, cache_spec,
                  pl.BlockSpec(memory_space=pltpu.SMEM)],
        out_specs=pl.BlockSpec((QBLOCK, HC * HD), lambda b, j: (b * nb + j, 0)),
        scratch_shapes=[pltpu.VMEM((3 * QBLOCK, QBLOCK), F32)],
        compiler_params=_params(2),
        name="attn_latent",
    )(qkv, qkv, qkv, qkv, qkv, qkv, qkv, cache_k, cache_v, sink)


def _route(lg):
    lane = lax.broadcasted_iota(jnp.int32, lg.shape, 1)
    neg = -jnp.inf
    big = 1 << 20
    is_grp = jnp.logical_and(lane >= N_EXPERTS, lane < N_EXPERTS + N_GROUPS)
    mg = jnp.max(jnp.where(is_grp, lg, neg), axis=1, keepdims=True)
    g_lane = jnp.min(jnp.where(jnp.logical_and(is_grp, lg == mg), lane, big), axis=1, keepdims=True)
    g_w = 1.0 / jnp.sum(jnp.where(is_grp, jnp.exp(lg - mg), 0.0), axis=1, keepdims=True)
    g_idx = g_lane - N_EXPERTS
    in_grp = jnp.logical_and(lane >= g_idx * EPG, lane < (g_idx + 1) * EPG)
    v1 = jnp.max(jnp.where(in_grp, lg, neg), axis=1, keepdims=True)
    i1 = jnp.min(jnp.where(jnp.logical_and(in_grp, lg == v1), lane, big), axis=1, keepdims=True)
    rest = jnp.logical_and(in_grp, lane != i1)
    v2 = jnp.max(jnp.where(rest, lg, neg), axis=1, keepdims=True)
    i2 = jnp.min(jnp.where(jnp.logical_and(rest, lg == v2), lane, big), axis=1, keepdims=True)
    e2 = jnp.exp(v2 - v1)
    p1 = 1.0 / (1.0 + e2)
    p2 = e2 / (1.0 + e2)
    return jnp.where(lane == 0, i1.astype(F32),
                     jnp.where(lane == 1, i2.astype(F32),
                               jnp.where(lane == 2, p1 * g_w, jnp.where(lane == 3, p2 * g_w, 0.0))))


def _moe_dense_kernel(x_ref, sh_ref, sc_ref, wr_ref, br_ref, wg_ref, wu_ref, wd_ref, gate_ref, g_ref, b_ref,
                      *out_and_scratch, tm):
    *o_refs, xm_s, meta_s, acc_s = out_and_scratch
    grp = pl.program_id(1)

    @pl.when(grp == 0)
    def _():
        xm = x_ref[...] * (1.0 + sc_ref[0]) + sh_ref[0]
        meta_s[...] = _route(_mm(_lhs3(xm), _rhs3(wr_ref[...])) + br_ref[...])
        xm_s[...] = _bf(xm)
        acc_s[...] = jnp.zeros_like(acc_s)

    xm = xm_s[...]
    meta = meta_s[...]
    i1 = meta[:, 0:1].astype(jnp.int32)
    i2 = meta[:, 1:2].astype(jnp.int32)
    w1 = meta[:, 2:3]
    w2 = meta[:, 3:4]
    hid = []
    for e in range(EPG):
        eid = grp * EPG + e
        gate = jnp.where(i1 == eid, w1, 0.0) + jnp.where(i2 == eid, w2, 0.0)
        a = _mm(xm, wg_ref[0, e])
        u = _mm(xm, wu_ref[0, e])
        hid.append(_bf((a * _sigmoid(a)) * u * gate))
    acc_s[...] += _mm(jnp.concatenate(hid, axis=1), wd_ref[0])

    def result():
        return _residual_ln(x_ref[...], gate_ref[0], acc_s[...], g_ref[...], b_ref[...])

    last = grp == N_GROUPS - 1
    if len(o_refs) == 1:
        @pl.when(last)
        def _():
            o_refs[0][...] = result()
    else:
        is_p = pl.program_id(0) < NP_ROWS // tm

        @pl.when(jnp.logical_and(last, is_p))
        def _():
            o_refs[0][...] = result()

        @pl.when(jnp.logical_and(last, jnp.logical_not(is_p)))
        def _():
            o_refs[1][...] = result()


def _moe_dense(x, mods, layer, w_r, b_r, wg, wu, wd, ln_g, ln_b, split_out=False):
    tm = 1024
    npt = NP_ROWS // tm
    row = lambda i, g: (i, 0)
    const2 = lambda i, g: (0, 0)
    if split_out:
        half = jax.ShapeDtypeStruct((NP_ROWS, D), F32)
        out_shape = (half, half)
        out_specs = (pl.BlockSpec((tm, D), lambda i, g: (jnp.minimum(i, npt - 1), 0)),
                     pl.BlockSpec((tm, D), lambda i, g: (jnp.maximum(i - npt, 0), 0)))
    else:
        out_shape = jax.ShapeDtypeStruct((T, D), F32)
        out_specs = pl.BlockSpec((tm, D), row)
    return pl.pallas_call(
        functools.partial(_moe_dense_kernel, tm=tm),
        out_shape=out_shape,
        grid=(T // tm, N_GROUPS),
        in_specs=[pl.BlockSpec((tm, D), row), _mod_spec(layer, 3, tm), _mod_spec(layer, 4, tm),
                  pl.BlockSpec((D, 128), const2), pl.BlockSpec((1, 128), const2),
                  pl.BlockSpec((1, EPG, D, EXPERT_FF), lambda i, g: (g, 0, 0, 0)),
                  pl.BlockSpec((1, EPG, D, EXPERT_FF), lambda i, g: (g, 0, 0, 0)),
                  pl.BlockSpec((1, EPG * EXPERT_FF, D), lambda i, g: (g, 0, 0)),
                  _mod_spec(layer, 5, tm), pl.BlockSpec((1, D), const2), pl.BlockSpec((1, D), const2)],
        out_specs=out_specs,
        scratch_shapes=[pltpu.VMEM((tm, D), BF16), pltpu.VMEM((tm, 128), F32), pltpu.VMEM((tm, D), F32)],
        compiler_params=_params(2, VMEM_LIMIT + (2 * tm * D * 4 if split_out else 0)),
        name="moe_dense",
    )(x, mods, mods, w_r, b_r, wg, wu, wd, mods, ln_g, ln_b)


def _permute_even_w(w):
    a_end = 4 * HA * DH
    g_end = a_end + 4 * HA
    c_end = g_end + 3 * HB * DH
    z_end = c_end + HB * DH
    small = jnp.concatenate([w[:, a_end:g_end], w[:, z_end:]], axis=1)
    pad = jnp.zeros((w.shape[0], 128 - small.shape[1]), w.dtype)
    return jnp.concatenate([w[:, g_end:c_end], w[:, :a_end], w[:, c_end:z_end], small, pad], axis=1)


def _lane_row(vals, offset):
    return jnp.zeros((1, 128), F32).at[0, offset:offset + vals.shape[0]].set(vals.astype(F32))


def _pair_rows(vals, offset):
    cols = jnp.zeros((N_GATE_COLS,), F32).at[offset:offset + vals.shape[0]].set(vals.astype(F32))
    return jnp.repeat(cols.reshape(N_GATE_COLS // 2, 2), CHUNK, axis=1)


def kernel(x_prompt, x_sample, c, c_ctx, state_mlstm_c, state_mlstm_n, state_mlstm_m, state_delta, cache_k, cache_v, w_mod, b_mod, ln_g, ln_b, w_in_even, mlstm_gate_b, mlstm_norm_g, delta_conv_w, delta_a_log, delta_dt_bias, delta_norm_g, w_out_even, w_qkv_odd, attn_sink, w_out_odd, w_grp, b_grp, w_erouter, b_erouter, w_gate, w_up, w_down):
    x = (x_prompt.reshape(NP_ROWS, D), x_sample.reshape(NS_ROWS, D))
    cvecs = jnp.concatenate([c_ctx[None, :], c, jnp.zeros((N_MOD_ROWS - 1 - DEC_BATCH, D), F32)], axis=0)
    mods = _modulation(cvecs, w_mod, b_mod)
    tables = _rope_tables()
    m0_all = jnp.broadcast_to(state_mlstm_m.reshape(DEC_BATCH, N_EVEN, 2 * HA, 1), (DEC_BATCH, N_EVEN, 2 * HA, 128))

    out_mn, out_mm = [], []
    new_mc = jnp.zeros((BATCH, N_EVEN, 2, HA, DH, DH), F32)
    new_ds = jnp.zeros((BATCH, N_EVEN, 2, HB, DH, DH), F32)
    new_k = jnp.zeros((BATCH, N_ODD, KVH, SEQ, HD), F32)
    new_v = jnp.zeros((BATCH, N_ODD, KVH, SEQ, HD), F32)
    for l in range(DEPTH):
        if l % 2 == 0:
            e = l // 2
            proj = _even_proj(x, mods, l, _permute_even_w(_bf(w_in_even[e])))
            gates_t2 = (proj[:, EVEN_W - 128:EVEN_W - 128 + N_GATE_COLS]
                        .reshape(T // CHUNK, CHUNK, N_GATE_COLS).transpose(0, 2, 1)
                        .reshape(T // CHUNK, N_GATE_COLS // 2, 2 * CHUNK))
            gb = mlstm_gate_b[e].reshape(-1)
            *h_scan, new_mc, mn, mm = _mlstm_scan(proj, gates_t2, _lane_row(gb, 0), _pair_rows(gb, 0),
                                                  state_mlstm_c, state_mlstm_n, m0_all, new_mc, e)
            qkv = _delta_prep(proj, delta_conv_w[e])
            dtb = delta_dt_bias[e].reshape(-1)
            nea = -jnp.exp(delta_a_log[e].astype(F32)).reshape(-1)
            *o_scan, new_ds = _delta_scan(qkv, proj, gates_t2, _lane_row(dtb, DECAY_COL0),
                                          _pair_rows(dtb, DECAY_COL0), _lane_row(nea, DECAY_COL0),
                                          _pair_rows(nea, DECAY_COL0), state_delta, new_ds, e)
            x = _even_out(h_scan, o_scan, proj, mlstm_norm_g[e][None, :], delta_norm_g[e][None, :],
                          _bf(w_out_even[e]), x, mods, l, ln_g[l, 0][None, :], ln_b[l, 0][None, :])
            out_mn.append(mn)
            out_mm.append(mm[:, :, 0].reshape(BATCH, 2, HA))
        else:
            o = l // 2
            qkv, new_k, new_v = _odd_proj(x, mods, l, _bf(w_qkv_odd[o]), tables, new_k, new_v, o)
            a_p = _attn_context(qkv, attn_sink[o])
            a_s = _attn_latent(qkv, cache_k, cache_v, attn_sink[o], o)
            x = _odd_out(a_p, a_s, _bf(w_out_odd[o]), x, mods, l, ln_g[l, 0][None, :], ln_b[l, 0][None, :])
        w_r = jnp.concatenate([w_erouter[l].transpose(1, 0, 2).reshape(D, N_EXPERTS), w_grp[l],
                               jnp.zeros((D, 128 - N_EXPERTS - N_GROUPS), F32)], axis=1)
        b_r = jnp.concatenate([b_erouter[l].reshape(-1), b_grp[l],
                               jnp.zeros((128 - N_EXPERTS - N_GROUPS,), F32)])[None, :]
        x = _moe_dense(x, mods, l, w_r, b_r,
                       _bf(w_gate[l]).reshape(N_GROUPS, EPG, D, EXPERT_FF),
                       _bf(w_up[l]).reshape(N_GROUPS, EPG, D, EXPERT_FF),
                       _bf(w_down[l]).reshape(N_GROUPS, EPG * EXPERT_FF, D),
                       ln_g[l, 1][None, :], ln_b[l, 1][None, :], split_out=(l == DEPTH - 1))
    return (x[0].reshape(BATCH, SEQ, D), x[1].reshape(DEC_BATCH, DEC_SEQ, D),
            new_mc, jnp.stack(out_mn, 1), jnp.stack(out_mm, 1), new_ds,
            new_k, new_v)
```

```python
import functools

import numpy as np
import jax
import jax.numpy as jnp
from jax import lax
from jax.experimental import pallas as pl
from jax.experimental.pallas import tpu as pltpu

F32 = jnp.float32
BF16 = jnp.bfloat16

D = 1024
BATCH = 32
SEQ = 256
DEPTH = 4
DEC_BATCH = 2
DEC_SEQ = 4096
PAST_LEN = 512
GRID_W = 64
N_EVEN = 2
N_ODD = 2
HA = 4
HB = 4
DH = 128
CHUNK = 64
HC = 16
KVH = 4
HD = 64
WINDOW = 128
QBLOCK = 128
ROPE_THETA = 10000.0
N_GROUPS = 4
EPG = 4
N_EXPERTS = 16
EXPERT_FF = 256
DN_ALPHA = (2 * DEPTH) ** 0.25
LN_EPS = 1e-5

NP_ROWS = BATCH * SEQ
NS_ROWS = DEC_BATCH * DEC_SEQ
T = NP_ROWS + NS_ROWS
N_MOD_ROWS = 8
EVEN_W = 4224
N_GATE_COLS = 4 * HA + 4 * HB
BETA_COL0 = 4 * HA
DECAY_COL0 = 4 * HA + 2 * HB
QKV_W = (HC + 2 * KVH) * HD

P_CHUNKS = SEQ // CHUNK
S_CHUNKS = DEC_SEQ // CHUNK
P_STEPS = BATCH * P_CHUNKS
S_STEPS = DEC_BATCH * S_CHUNKS
P_BLOCKS = NP_ROWS // CHUNK

VMEM_LIMIT = 48 * 1024 * 1024


def _params(n_axes, vmem_limit=VMEM_LIMIT):
    return pltpu.CompilerParams(dimension_semantics=("arbitrary",) * n_axes,
                                vmem_limit_bytes=vmem_limit)


def _mm(a, b):
    return lax.dot_general(a, b, (((1,), (0,)), ((), ())), preferred_element_type=F32)


def _mm_nt(a, b):
    return lax.dot_general(a, b, (((1,), (1,)), ((), ())), preferred_element_type=F32)


def _mm_tn(a, b):
    return lax.dot_general(a, b, (((0,), (0,)), ((), ())), preferred_element_type=F32)


def _bf(x):
    return x.astype(BF16)


def _sigmoid(x):
    return 1.0 / (1.0 + jnp.exp(-x))


def _softplus(x):
    return jnp.maximum(x, 0.0) + jnp.log1p(jnp.exp(-jnp.abs(x)))


def _log_sigmoid(x):
    return jnp.minimum(x, 0.0) - jnp.log1p(jnp.exp(-jnp.abs(x)))


def _mod_row(tile, tm):
    npt = NP_ROWS // tm
    per = DEC_SEQ // tm
    return jnp.where(tile < npt, 0, 1 + (tile - npt) // per)


def _mod_spec(layer, chunk, tm):
    def imap(i, *_):
        return ((layer * N_MOD_ROWS + _mod_row(i, tm)) * 6 + chunk, 0, 0)
    return pl.BlockSpec((1, 1, D), imap)


def _modulation_kernel(c_ref, w_ref, b_ref, o_ref):
    x = c_ref[...]
    s = x * _sigmoid(x)
    o_ref[0] = _mm(_lhs3(s), _rhs3(w_ref[0])) + b_ref[0]


def _modulation(cvecs, w_mod, b_mod):
    out = pl.pallas_call(
        _modulation_kernel,
        out_shape=jax.ShapeDtypeStruct((DEPTH, N_MOD_ROWS, 6 * D), F32),
        grid=(DEPTH, 6),
        in_specs=[pl.BlockSpec((N_MOD_ROWS, D), lambda l, j: (0, 0)),
                  pl.BlockSpec((1, D, D), lambda l, j: (l, 0, j)),
                  pl.BlockSpec((1, 1, D), lambda l, j: (l * 6 + j, 0, 0))],
        out_specs=pl.BlockSpec((1, N_MOD_ROWS, D), lambda l, j: (l, 0, j)),
        compiler_params=_params(2),
        name="modulation",
    )(cvecs, w_mod, b_mod.reshape(DEPTH * 6, 1, D))
    return out.reshape(DEPTH * N_MOD_ROWS * 6, 1, D)


def _x_operands(x, tm):
    if isinstance(x, tuple):
        npt = NP_ROWS // tm
        return ([pl.BlockSpec((tm, D), lambda i, *_: (jnp.minimum(i, npt - 1), 0)),
                 pl.BlockSpec((tm, D), lambda i, *_: (jnp.maximum(i - npt, 0), 0))], list(x))
    return [pl.BlockSpec((tm, D), lambda i, *_: (i, 0))], [x]


def _load_x(x_refs, tm):
    if len(x_refs) == 1:
        return x_refs[0][...]
    return jnp.where(pl.program_id(0) < NP_ROWS // tm, x_refs[0][...], x_refs[1][...])


def _proj_kernel(*refs, tm, n_x):
    x_refs, (sh_ref, sc_ref, w_ref, o_ref) = refs[:n_x], refs[n_x:]
    xm = _load_x(x_refs, tm) * (1.0 + sc_ref[0]) + sh_ref[0]
    o_ref[...] = _mm(_bf(xm), w_ref[...])


def _even_proj(x, mods, layer, w):
    tm = 512
    x_specs, x_arrays = _x_operands(x, tm)
    return pl.pallas_call(
        functools.partial(_proj_kernel, tm=tm, n_x=len(x_arrays)),
        out_shape=jax.ShapeDtypeStruct((T, EVEN_W), F32),
        grid=(T // tm,),
        in_specs=x_specs + [_mod_spec(layer, 0, tm), _mod_spec(layer, 1, tm),
                            pl.BlockSpec((D, EVEN_W), lambda i: (0, 0), pipeline_mode=pl.Buffered(1))],
        out_specs=pl.BlockSpec((tm, EVEN_W), lambda i: (i, 0)),
        compiler_params=_params(1),
        name="even_proj",
    )(*x_arrays, mods, mods, w)


def _qkv_kernel(x_ref, sh_ref, sc_ref, w_ref, cos_ref, sa_ref, sb_ref, kc_in, vc_in, o_ref, kc_ref, vc_ref, *, tm):
    del kc_in, vc_in
    i = pl.program_id(0)
    xm = x_ref[...] * (1.0 + sc_ref[0]) + sh_ref[0]
    acc = _mm(_bf(xm), w_ref[...])

    @pl.when(i < NP_ROWS // tm)
    def _():
        for b in range(tm // SEQ):
            for kv in range(KVH):
                k0 = HC * HD + kv * HD
                v0 = (HC + KVH) * HD + kv * HD
                kc_ref[b, 0, kv] = acc[b * SEQ:(b + 1) * SEQ, k0:k0 + HD]
                vc_ref[b, 0, kv] = acc[b * SEQ:(b + 1) * SEQ, v0:v0 + HD]

    n_q = HC * HD // 128
    n_k = KVH * HD // 128
    is_latent = i >= NP_ROWS // tm

    def scaled(g):
        blk = acc[:, g * 128:(g + 1) * 128]
        return blk * (HD ** -0.5) if g < n_q else blk

    @pl.when(is_latent)
    def _():
        cos, sa, sb = cos_ref[...], sa_ref[...], sb_ref[...]
        for g in range(n_q + n_k):
            blk = scaled(g)
            o_ref[:, g * 128:(g + 1) * 128] = blk * cos + pltpu.roll(blk, 112, 1) * sa + pltpu.roll(blk, 16, 1) * sb

    @pl.when(jnp.logical_not(is_latent))
    def _():
        for g in range(n_q + n_k):
            o_ref[:, g * 128:(g + 1) * 128] = scaled(g)

    o_ref[:, (n_q + n_k) * 128:] = acc[:, (n_q + n_k) * 128:]


def _rope_tables():
    half = HD // 4
    inv = np.float32(ROPE_THETA) ** (-np.arange(half, dtype=np.float32) / np.float32(half))
    pos = np.arange(DEC_SEQ)
    row = (pos // GRID_W).astype(np.float32)[:, None] * inv[None, :]
    col = (pos % GRID_W).astype(np.float32)[:, None] * inv[None, :]
    cos = np.concatenate([np.cos(row), np.cos(row), np.cos(col), np.cos(col)], axis=-1)
    sin = np.concatenate([np.sin(row), np.sin(row), np.sin(col), np.sin(col)], axis=-1)
    first = (np.arange(HD) % 32) < 16
    sa = np.where(first, -sin, 0.0)
    sb = np.where(first, 0.0, sin)
    tile2 = lambda t: jnp.asarray(np.concatenate([t, t], axis=-1), F32)
    return tile2(cos), tile2(sa), tile2(sb)


def _odd_proj(x, mods, layer, w, tables, cache_k, cache_v, o):
    tm = 512
    npt = NP_ROWS // tm
    per = DEC_SEQ // tm
    tab_spec = pl.BlockSpec((tm, 128), lambda i: (jnp.where(i < npt, 0, (i - npt) % per), 0))
    cache_spec = pl.BlockSpec((tm // SEQ, 1, KVH, SEQ, HD), lambda i: (jnp.minimum(i, npt - 1), o, 0, 0, 0))
    cache_shape = jax.ShapeDtypeStruct((BATCH, N_ODD, KVH, SEQ, HD), F32)
    return pl.pallas_call(
        functools.partial(_qkv_kernel, tm=tm),
        out_shape=(jax.ShapeDtypeStruct((T, QKV_W), F32), cache_shape, cache_shape),
        grid=(T // tm,),
        in_specs=[pl.BlockSpec((tm, D), lambda i: (i, 0)),
                  _mod_spec(layer, 0, tm), _mod_spec(layer, 1, tm),
                  pl.BlockSpec((D, QKV_W), lambda i: (0, 0)),
                  tab_spec, tab_spec, tab_spec,
                  pl.BlockSpec(memory_space=pl.ANY), pl.BlockSpec(memory_space=pl.ANY)],
        out_specs=(pl.BlockSpec((tm, QKV_W), lambda i: (i, 0)), cache_spec, cache_spec),
        input_output_aliases={7: 1, 8: 2},
        compiler_params=_params(1),
        name="odd_qkv_proj",
    )(x, mods, mods, w, *tables, cache_k, cache_v)


PREP_ROWS = 256


def _delta_prep_kernel(x_ref, prev_ref, next_ref, w_ref, o_ref):
    i = pl.program_id(0)
    npb = NP_ROWS // PREP_ROWS
    per = DEC_SEQ // PREP_ROWS
    is_latent = i >= npb
    pos = (i - npb) % per
    has_prev = jnp.logical_and(is_latent, pos > 0)
    has_next = jnp.logical_and(is_latent, pos < per - 1)
    x = x_ref[...]
    w = w_ref[...]
    rows = lax.broadcasted_iota(jnp.int32, x.shape, 0)
    prev_row = jnp.where(has_prev, prev_ref[7:8, :], 0.0)
    next_row = jnp.where(has_next, next_ref[0:1, :], 0.0)
    xm1 = jnp.where(rows == 0, prev_row, pltpu.roll(x, 1, 0))
    xp1 = jnp.where(rows == PREP_ROWS - 1, next_row, pltpu.roll(x, PREP_ROWS - 1, 0))
    y = xm1 * w[0:1, :] + x * w[1:2, :] + xp1 * w[2:3, :]
    y = y * _sigmoid(y)
    for h in range(3 * HB):
        yh = y[:, h * DH:(h + 1) * DH]
        if h < 2 * HB:
            inv = lax.rsqrt(jnp.sum(yh * yh, axis=-1, keepdims=True) + 1e-6)
            yh = yh * (inv * (DH ** -0.5) if h < HB else inv)
        o_ref[:, h * DH:(h + 1) * DH] = yh


def _delta_prep(proj, conv_w):
    nblk = T // PREP_ROWS
    sub = PREP_ROWS // 8
    last8 = T // 8 - 1
    return pl.pallas_call(
        _delta_prep_kernel,
        out_shape=jax.ShapeDtypeStruct((T, 3 * 512), F32),
        grid=(nblk,),
        in_specs=[pl.BlockSpec((PREP_ROWS, 3 * 512), lambda i: (i, 0)),
                  pl.BlockSpec((8, 3 * 512), lambda i: (jnp.maximum(i * sub - 1, 0), 0)),
                  pl.BlockSpec((8, 3 * 512), lambda i: (jnp.minimum((i + 1) * sub, last8), 0)),
                  pl.BlockSpec((3, 3 * 512), lambda i: (0, 0))],
        out_specs=pl.BlockSpec((PREP_ROWS, 3 * 512), lambda i: (i, 0)),
        compiler_params=_params(1),
        name="delta_prep",
    )(proj, proj, proj, conv_w)


assert P_STEPS == S_STEPS
SCAN_STEPS = P_STEPS


def _bwd_local(s, nc):
    return (s // nc) * nc + nc - 1 - s % nc


def _scan_blocks():
    return (lambda s: s, lambda s: _bwd_local(s, P_CHUNKS),
            lambda s: P_BLOCKS + s, lambda s: P_BLOCKS + _bwd_local(s, S_CHUNKS))


def _scan_specs(xcol):
    gcol = EVEN_W // 128 - 1
    blocks = _scan_blocks()
    return ([pl.BlockSpec((CHUNK, 3 * 512), lambda s, f=f: (f(s), xcol)) for f in blocks]
            + [pl.BlockSpec((CHUNK, 128), lambda s, f=f: (f(s), gcol)) for f in blocks]
            + [pl.BlockSpec((1, 16, 128), lambda s, f=f: (f(s), 0, 0)) for f in blocks])


def _scan_out_specs():
    local = (lambda s: s, lambda s: _bwd_local(s, P_CHUNKS), lambda s: s, lambda s: _bwd_local(s, S_CHUNKS))
    return [pl.BlockSpec((CHUNK, 512), lambda s, f=f: (f(s), 0)) for f in local]


def _mlstm_kernel(xpf_ref, xpb_ref, xsf_ref, xsb_ref, gpf_ref, gpb_ref, gsf_ref, gsb_ref,
                  rpf_ref, rpb_ref, rsf_ref, rsb_ref,
                  brow_ref, b2_ref, c0_ref, n0_ref, m0_ref, c_all_in,
                  hpf_ref, hpb_ref, hsf_ref, hsb_ref, c_out, n_out, m_out,
                  cp_s, np_s, mp_s, cs_s, ns_s, ms_s):
    del c_all_in
    s = pl.program_id(0)
    jp = s % P_CHUNKS
    js = s % S_CHUNKS

    @pl.when(jp == 0)
    def _():
        cp_s[...] = jnp.zeros_like(cp_s)
        np_s[...] = jnp.zeros_like(np_s)
        mp_s[...] = jnp.zeros_like(mp_s)

    @pl.when(js == 0)
    def _():
        for d in range(2):
            for h in range(HA):
                cs_s[d * HA + h] = c0_ref[0, 0, d, h]
                ns_s[d * HA + h] = jnp.broadcast_to(n0_ref[0, 0, d, h:h + 1, :], (DH, 128)).T
        ms_s[...] = m0_ref[0, 0]

    states = ((cp_s, np_s, mp_s), (cs_s, ns_s, ms_s))

    row = lax.broadcasted_iota(jnp.int32, (CHUNK, 128), 0)
    lane = lax.broadcasted_iota(jnp.int32, (CHUNK, 128), 1)
    left = lane < CHUNK
    lcol = jnp.where(left, lane, lane - CHUNK)
    r64 = lax.broadcasted_iota(jnp.int32, (CHUNK, CHUNK), 0)
    c64 = lax.broadcasted_iota(jnp.int32, (CHUNK, CHUNK), 1)
    r128 = lax.broadcasted_iota(jnp.int32, (128, 128), 0)
    c128 = lax.broadcasted_iota(jnp.int32, (128, 128), 1)
    same_half = (r128 < CHUNK) == (c128 < CHUNK)
    neg = -jnp.inf
    ins = ((xpf_ref, gpf_ref, rpf_ref, hpf_ref), (xpb_ref, gpb_ref, rpb_ref, hpb_ref),
           (xsf_ref, gsf_ref, rsf_ref, hsf_ref), (xsb_ref, gsb_ref, rsb_ref, hsb_ref))

    sel_r = lax.broadcasted_iota(jnp.int32, (128, HA * 128), 0)
    sel_h = lax.broadcasted_iota(jnp.int32, (128, HA * 128), 1) // 128
    ones_lr = lax.broadcasted_iota(jnp.int32, (256, 256), 0)
    ones_lc = lax.broadcasted_iota(jnp.int32, (256, 256), 1)
    half_sum = (((ones_lr % 128) < CHUNK) == (ones_lc < 128)).astype(BF16)
    ones_l = jnp.ones((CHUNK, 128), BF16)
    rows256 = lax.broadcasted_iota(jnp.int32, (CHUNK, 256), 0)

    def running_max(x, d):
        sh = 1
        while sh < CHUNK:
            if d == 0:
                x = jnp.maximum(x, jnp.where(rows256 >= sh, pltpu.roll(x, sh, 0), neg))
            else:
                x = jnp.maximum(x, jnp.where(rows256 < CHUNK - sh, pltpu.roll(x, CHUNK - sh, 0), neg))
            sh *= 2
        return x

    gate = []
    for u in range(4):
        d = u % 2
        gc_ref, gr_ref = ins[u][1], ins[u][2]
        if d == 0:
            incl_p, tri_c, tri_r = lcol <= row, c64 <= r64, jnp.logical_and(same_half, r128 <= c128)
        else:
            incl_p, tri_c, tri_r = lcol >= row, c64 >= r64, jnp.logical_and(same_half, r128 >= c128)
        gc = gc_ref[...] + brow_ref[...]
        gr = gr_ref[0] + b2_ref[...]
        i_rep = _replicate(gc, (sel_r == d * 2 * HA + sel_h).astype(BF16))
        b_rep = _replicate(_prefix_cols(tri_c.astype(BF16), _log_sigmoid(gc)),
                           (sel_r == d * 2 * HA + HA + sel_h).astype(BF16))
        cs_row = _prefix_rows(_log_sigmoid(gr), tri_r.astype(BF16))
        gate.append((incl_p, gr, cs_row, i_rep, b_rep))

    st = []
    for u in range(4):
        d = u % 2
        c_s, n_s, m_s = states[u // 2]
        for hp in range(HA // 2):
            x_ref = ins[u][0]
            incl_p, gr, cs_row, i_rep, b_rep = gate[u]
            end = CHUNK - 1 if d == 0 else 0
            heads = []
            for h in (2 * hp, 2 * hp + 1):
                c = d * HA + h
                heads.append(dict(c=c, h=h, q=x_ref[:, h * DH:(h + 1) * DH],
                                  k=x_ref[:, 512 + h * DH:512 + (h + 1) * DH] * (DH ** -0.5),
                                  v=x_ref[:, 1024 + h * DH:1024 + (h + 1) * DH],
                                  i=i_rep[:, h * 128:(h + 1) * 128], b=b_rep[:, h * 128:(h + 1) * 128],
                                  m=m_s[c:c + 1, :]))
            ha, hb_ = heads
            ri = d * HA + hp
            rf = d * HA + HA // 2 + hp
            i_row = gr[ri:ri + 1, :]
            b_row = cs_row[rf:rf + 1, :]
            run = running_max(jnp.concatenate([ha["i"] - ha["b"], hb_["i"] - hb_["b"]], axis=1), d)
            for idx, hd in enumerate(heads):
                top = jnp.maximum(hd["m"], run[:, idx * 128:(idx + 1) * 128])
                hd["m_t"] = hd["b"] + top
                hd["w_inter"] = jnp.exp(hd["m"] - top)
            b_p = jnp.where(left, ha["b"], hb_["b"])
            m_t_p = jnp.where(left, ha["m_t"], hb_["m_t"])
            dmat = jnp.where(incl_p, b_p - b_row + i_row, neg)
            q_cat = _bf(jnp.concatenate([ha["q"], hb_["q"]], axis=1))
            k_bd = _bf(_block_diag2(ha["k"], hb_["k"]))
            st.append(dict(u=u, d=d, heads=heads, sc=_mm_nt(q_cat, k_bd) * jnp.exp(dmat - m_t_p)))

    for p in st:
        ha, hb_ = p["heads"]
        h_ref = ins[p["u"]][3]
        c_s, n_s, m_s = states[p["u"] // 2]
        end = CHUNK - 1 if p["d"] == 0 else 0
        sc = p["sc"]
        sv = _mm(_bf(sc), _bf(_block_diag2(ha["v"], hb_["v"])))
        dens = _mm(jnp.concatenate(_split2(sc), axis=1), half_sum)
        for idx, hd in enumerate((ha, hb_)):
            c, h, q, k = hd["c"], hd["h"], hd["q"], hd["k"]
            c_mat = c_s[c]
            n_mat = n_s[c]
            q_cn = _mm(_bf(q), _bf(jnp.concatenate([c_mat, n_mat], axis=1)))
            num = sv[:, idx * DH:(idx + 1) * DH] + hd["w_inter"] * q_cn[:, :DH]
            den = dens[:, idx * 128:(idx + 1) * 128] + hd["w_inter"] * q_cn[:, DH:]
            h_ref[:, h * DH:(h + 1) * DH] = num / jnp.maximum(jnp.abs(den), jnp.exp(-hd["m_t"]))
            b_last = hd["b"][end:end + 1, :]
            g_end = b_last - hd["b"] + hd["i"]
            m_new = jnp.maximum(b_last + hd["m"], jnp.max(g_end, axis=0, keepdims=True))
            kwb = _bf(k * jnp.exp(g_end - m_new))
            decay = jnp.exp(b_last + hd["m"] - m_new)
            upd = _mm_tn(kwb, jnp.concatenate([_bf(hd["v"]), ones_l], axis=1))
            c_s[c] = decay * c_mat + upd[:, :DH]
            n_s[c] = decay * n_mat + upd[:, DH:]
            m_s[c:c + 1, :] = m_new

    @pl.when(jp == P_CHUNKS - 1)
    def _():
        for d in range(2):
            for h in range(HA):
                c_out[0, 0, d, h] = cp_s[d * HA + h]
                n_out[0, d, h:h + 1, :] = np_s[d * HA + h].T[0:1, :]
        m_out[0] = mp_s[...]


def _mlstm_scan(proj, gates_t2, bias_row, bias2, c0, n0, m0, c_all, e):
    const2 = lambda s: (0, 0)
    in_specs = (_scan_specs(1)
                + [pl.BlockSpec((1, 128), const2), pl.BlockSpec((16, 128), const2),
                   pl.BlockSpec((1, 1, 2, HA, DH, DH), lambda s: (s // S_CHUNKS, e, 0, 0, 0, 0)),
                   pl.BlockSpec((1, 1, 2, HA, DH), lambda s: (s // S_CHUNKS, e, 0, 0, 0)),
                   pl.BlockSpec((1, 1, 2 * HA, 128), lambda s: (s // S_CHUNKS, e, 0, 0)),
                   pl.BlockSpec(memory_space=pl.ANY)])
    half = jax.ShapeDtypeStruct((NP_ROWS, HA * DH), F32)
    out_shape = (half, half, half, half,
                 jax.ShapeDtypeStruct((BATCH, N_EVEN, 2, HA, DH, DH), F32),
                 jax.ShapeDtypeStruct((BATCH, 2, HA, DH), F32),
                 jax.ShapeDtypeStruct((BATCH, 2 * HA, 128), F32))
    out_specs = _scan_out_specs() + [
        pl.BlockSpec((1, 1, 2, HA, DH, DH), lambda s: (s // P_CHUNKS, e, 0, 0, 0, 0)),
        pl.BlockSpec((1, 2, HA, DH), lambda s: (s // P_CHUNKS, 0, 0, 0)),
        pl.BlockSpec((1, 2 * HA, 128), lambda s: (s // P_CHUNKS, 0, 0))]
    state = [pltpu.VMEM((2 * HA, DH, DH), F32), pltpu.VMEM((2 * HA, DH, 128), F32), pltpu.VMEM((2 * HA, 128), F32)]
    return pl.pallas_call(
        _mlstm_kernel,
        out_shape=out_shape,
        grid=(SCAN_STEPS,),
        in_specs=in_specs,
        out_specs=out_specs,
        scratch_shapes=state + state,
        input_output_aliases={17: 4},
        compiler_params=_params(1),
        name="mlstm_scan",
    )(proj, proj, proj, proj, proj, proj, proj, proj, gates_t2, gates_t2, gates_t2, gates_t2,
      bias_row, bias2, c0, n0, m0, c_all)


def _split2(x):
    hi = _bf(x)
    return hi, _bf(x - hi.astype(F32))


def _split3(x):
    h1 = _bf(x)
    r1 = x - h1.astype(F32)
    h2 = _bf(r1)
    return h1, h2, _bf(r1 - h2.astype(F32))


def _lhs3(x):
    hi, lo = _split2(x)
    return jnp.concatenate([hi, lo, hi], axis=1)


def _rhs3(x):
    hi, lo = _split2(x)
    return jnp.concatenate([hi, hi, lo], axis=0)


def _prefix_cols(tri_bf, x):
    n = x.shape[1]
    r = _mm(tri_bf, jnp.concatenate(_split3(x), axis=1))
    return r[:, :n] + r[:, n:2 * n] + r[:, 2 * n:]


def _replicate(x, sel):
    return _mm(jnp.concatenate(_split3(x), axis=1), jnp.concatenate([sel, sel, sel], axis=0))


def _prefix_rows(x, tri_bf):
    m = x.shape[0]
    r = _mm(jnp.concatenate(_split3(x), axis=0), tri_bf)
    return r[:m] + r[m:2 * m] + r[2 * m:]


LEVELS = tuple(range(6))


def _block_diag2(a, b):
    z = jnp.zeros_like(a)
    return jnp.concatenate([jnp.concatenate([a, z], axis=1), jnp.concatenate([z, b], axis=1)], axis=0)


def _delta_kernel(xpf_ref, xpb_ref, xsf_ref, xsb_ref, gpf_ref, gpb_ref, gsf_ref, gsb_ref,
                  rpf_ref, rpb_ref, rsf_ref, rsb_ref,
                  dtrow_ref, dt2_ref, narow_ref, na2_ref, s0_ref, s_all_in,
                  opf_ref, opb_ref, osf_ref, osb_ref, s_out, sp_s, ss_s):
    del s_all_in
    s = pl.program_id(0)
    jp = s % P_CHUNKS
    js = s % S_CHUNKS

    @pl.when(jp == 0)
    def _():
        sp_s[...] = jnp.zeros_like(sp_s)

    @pl.when(js == 0)
    def _():
        for d in range(2):
            for h in range(HB):
                ss_s[d * HB + h] = s0_ref[0, 0, d, h]

    states = (sp_s, ss_s)

    row = lax.broadcasted_iota(jnp.int32, (CHUNK, 128), 0)
    lane = lax.broadcasted_iota(jnp.int32, (CHUNK, 128), 1)
    left = lane < CHUNK
    lcol = jnp.where(left, lane, lane - CHUNK)
    eye_p = (lcol == row).astype(F32)
    r64 = lax.broadcasted_iota(jnp.int32, (CHUNK, CHUNK), 0)
    c64 = lax.broadcasted_iota(jnp.int32, (CHUNK, CHUNK), 1)
    r128 = lax.broadcasted_iota(jnp.int32, (128, 128), 0)
    c128 = lax.broadcasted_iota(jnp.int32, (128, 128), 1)
    same_half = (r128 < CHUNK) == (c128 < CHUNK)
    ins = ((xpf_ref, gpf_ref, rpf_ref, opf_ref), (xpb_ref, gpb_ref, rpb_ref, opb_ref),
           (xsf_ref, gsf_ref, rsf_ref, osf_ref), (xsb_ref, gsb_ref, rsb_ref, osb_ref))

    gate = []
    for u in range(4):
        d = u % 2
        gc_ref, gr_ref = ins[u][1], ins[u][2]
        if d == 0:
            incl_p, strict_p = lcol <= row, lcol < row
            tri_c, tri_r = c64 <= r64, jnp.logical_and(same_half, r128 <= c128)
        else:
            incl_p, strict_p = lcol >= row, lcol > row
            tri_c, tri_r = c64 >= r64, jnp.logical_and(same_half, r128 >= c128)
        xc = gc_ref[...]
        xr = gr_ref[0]
        la_c = narow_ref[...] * _softplus(xc + dtrow_ref[...])
        la_r = na2_ref[...] * _softplus(xr + dt2_ref[...])
        g_c = _prefix_cols(tri_c.astype(BF16), la_c)
        g_r = _prefix_rows(la_r, tri_r.astype(BF16))
        gate.append((incl_p, strict_p, _sigmoid(xc), g_c, g_r))

    def bd_rhs(hi, lo, mask=None):
        top = left if mask is None else jnp.logical_and(left, mask)
        bot = jnp.logical_not(left) if mask is None else jnp.logical_and(jnp.logical_not(left), mask)
        zero = jnp.zeros_like(hi)
        blocks = [jnp.concatenate([jnp.where(top, x, zero), jnp.where(bot, x, zero)], axis=0) for x in (hi, lo)]
        return jnp.concatenate([blocks[0], blocks[0], blocks[1]], axis=0)

    off = ([], [])
    for lv in LEVELS:
        same = jnp.right_shift(row, lv + 1) == jnp.right_shift(lcol, lv + 1)
        r_hi = jnp.bitwise_and(jnp.right_shift(row, lv), 1) == 1
        c_hi = jnp.bitwise_and(jnp.right_shift(lcol, lv), 1) == 1
        off[0].append(same & r_hi & jnp.logical_not(c_hi))
        off[1].append(same & jnp.logical_not(r_hi) & c_hi)

    pairs = [(u, hp) for u in range(4) for hp in range(HB // 2)]
    st = []
    for u, hp in pairs:
        d = u % 2
        x_ref = ins[u][0]
        incl_p, strict_p, beta_c, g_c, g_r = gate[u]
        end = CHUNK - 1 if d == 0 else 0
        heads = []
        for h in (2 * hp, 2 * hp + 1):
            ib = BETA_COL0 + d * HB + h
            ia = DECAY_COL0 + d * HB + h
            gcol = g_c[:, ia:ia + 1]
            bc = beta_c[:, ib:ib + 1]
            q = x_ref[:, h * DH:(h + 1) * DH]
            k = x_ref[:, 512 + h * DH:512 + (h + 1) * DH]
            v = x_ref[:, 1024 + h * DH:1024 + (h + 1) * DH]
            heads.append(dict(h=h, gcol=gcol, bc=bc, q=q, k=k, v=v, kb=k * bc, eg=jnp.exp(gcol),
                              g_last=gcol[end:end + 1, :]))
        ha, hb_ = heads
        r = (DECAY_COL0 + d * HB) // 2 + hp
        gcol_p = jnp.where(left, ha["gcol"], hb_["gcol"])
        decay = jnp.exp(jnp.where(incl_p, gcol_p - g_r[r:r + 1, :], -jnp.inf))
        k_bd = _bf(_block_diag2(ha["k"], hb_["k"]))
        kb_cat = _bf(jnp.concatenate([ha["kb"], hb_["kb"]], axis=1))
        q_cat = _bf(jnp.concatenate([ha["q"], hb_["q"]], axis=1))
        kq = _mm_nt(jnp.concatenate([kb_cat, q_cat], axis=0), k_bd)
        a_mat = jnp.where(strict_p, kq[:CHUNK] * decay, 0.0)
        qk = kq[CHUNK:] * decay
        a_hi, a_lo = _split2(a_mat)
        st.append(dict(u=u, d=d, heads=heads, t=eye_p - jnp.where(off[d][0], a_mat, 0.0), qk=qk,
                       am=[bd_rhs(a_hi, a_lo, m) for m in off[d][1:]]))

    for li in range(len(LEVELS) - 1):
        for p in st:
            p["t_parts"] = _split2(p["t"])
            t_hi, t_lo = p["t_parts"]
            p["w"] = _mm(jnp.concatenate([t_hi, t_lo, t_hi], axis=1), p["am"][li])
        for p in st:
            p["t"] = p["t"] - _mm(_lhs3(p["w"]), bd_rhs(*p["t_parts"]))

    for p in st:
        ha, hb_ = p["heads"]
        o_ref = ins[p["u"]][3]
        s_s = states[p["u"] // 2]
        rhs_a = jnp.concatenate([ha["v"] * ha["bc"], ha["kb"] * ha["eg"]], axis=1)
        rhs_b = jnp.concatenate([hb_["v"] * hb_["bc"], hb_["kb"] * hb_["eg"]], axis=1)
        (a_hi, a_lo), (b_hi, b_lo) = _split2(rhs_a), _split2(rhs_b)
        bd_hi, bd_lo = _block_diag2(a_hi, b_hi), _block_diag2(a_lo, b_lo)
        sol = _mm(_lhs3(p["t"]), jnp.concatenate([bd_hi, bd_hi, bd_lo], axis=0))
        vn = []
        for idx, hd in enumerate((ha, hb_)):
            c = p["d"] * HB + hd["h"]
            s_mat = s_s[c]
            sbf = _bf(s_mat)
            so = sol[:, idx * 2 * DH:(idx + 1) * 2 * DH]
            both = _mm(_bf(jnp.concatenate([so[:, DH:], hd["q"] * hd["eg"]], axis=0)), sbf)
            v_new = so[:, :DH] - both[:CHUNK]
            vn.append(v_new)
            hd["o1"] = both[CHUNK:]
            s_s[c] = (jnp.exp(hd["g_last"]) * s_mat
                      + _mm_tn(_bf(hd["k"] * jnp.exp(hd["g_last"] - hd["gcol"])), _bf(v_new)))
        o2 = _mm(_bf(p["qk"]), _bf(_block_diag2(vn[0], vn[1])))
        for idx, hd in enumerate((ha, hb_)):
            o_ref[:, hd["h"] * DH:(hd["h"] + 1) * DH] = hd["o1"] + o2[:, idx * DH:(idx + 1) * DH]

    @pl.when(jp == P_CHUNKS - 1)
    def _():
        for d in range(2):
            for h in range(HB):
                s_out[0, 0, d, h] = sp_s[d * HB + h]


def _delta_scan(qkv, proj, gates_t2, dt_row, dt2, na_row, na2, s0, s_all, e):
    const2 = lambda s: (0, 0)
    in_specs = (_scan_specs(0)
                + [pl.BlockSpec((1, 128), const2), pl.BlockSpec((16, 128), const2),
                   pl.BlockSpec((1, 128), const2), pl.BlockSpec((16, 128), const2),
                   pl.BlockSpec((1, 1, 2, HB, DH, DH), lambda s: (s // S_CHUNKS, e, 0, 0, 0, 0)),
                   pl.BlockSpec(memory_space=pl.ANY)])
    half = jax.ShapeDtypeStruct((NP_ROWS, HB * DH), F32)
    out_shape = (half, half, half, half, jax.ShapeDtypeStruct((BATCH, N_EVEN, 2, HB, DH, DH), F32))
    out_specs = _scan_out_specs() + [
        pl.BlockSpec((1, 1, 2, HB, DH, DH), lambda s: (s // P_CHUNKS, e, 0, 0, 0, 0))]
    return pl.pallas_call(
        _delta_kernel,
        out_shape=out_shape,
        grid=(SCAN_STEPS,),
        in_specs=in_specs,
        out_specs=out_specs,
        scratch_shapes=[pltpu.VMEM((2 * HB, DH, DH), F32), pltpu.VMEM((2 * HB, DH, DH), F32)],
        input_output_aliases={17: 4},
        compiler_params=_params(1),
        name="delta_scan",
    )(qkv, qkv, qkv, qkv, proj, proj, proj, proj, gates_t2, gates_t2, gates_t2, gates_t2,
      dt_row, dt2, na_row, na2, s0, s_all)


def _residual_ln(x, gate, y, g, b):
    r = DN_ALPHA * x + gate * y
    mu = jnp.mean(r, axis=-1, keepdims=True)
    var = jnp.mean(jnp.square(r - mu), axis=-1, keepdims=True)
    return (r - mu) * lax.rsqrt(var + LN_EPS) * g + b


def _even_out_kernel(*refs, tm, n_x):
    x_refs = refs[:n_x]
    (hpf_ref, hpb_ref, hsf_ref, hsb_ref, opf_ref, opb_ref, osf_ref, osb_ref,
     oa_ref, zb_ref, mg_ref, dg_ref, w_ref, gate_ref, g_ref, b_ref, o_ref) = refs[n_x:]
    is_p = pl.program_id(0) < NP_ROWS // tm
    hf_ref, hb_ref, of_ref, ob_ref = (
        lambda sl, p=p, q=q: jnp.where(is_p, p[:, sl], q[:, sl])
        for p, q in ((hpf_ref, hsf_ref), (hpb_ref, hsb_ref), (opf_ref, osf_ref), (opb_ref, osb_ref)))
    parts = []
    for h in range(HA):
        sl = slice(h * DH, (h + 1) * DH)
        hh = hf_ref(sl) + hb_ref(sl)
        mu = jnp.mean(hh, axis=-1, keepdims=True)
        var = jnp.mean(jnp.square(hh - mu), axis=-1, keepdims=True)
        parts.append(_sigmoid(oa_ref[:, sl]) * ((hh - mu) * lax.rsqrt(var + LN_EPS) * mg_ref[:, sl]))
    for h in range(HB):
        sl = slice(h * DH, (h + 1) * DH)
        oo = of_ref(sl) + ob_ref(sl)
        z = zb_ref[:, sl]
        nrm = oo * lax.rsqrt(jnp.mean(jnp.square(oo), axis=-1, keepdims=True) + LN_EPS) * dg_ref[:, sl]
        parts.append(nrm * (z * _sigmoid(z)))
    a = jnp.concatenate(parts, axis=1)
    y = _mm(_bf(a), w_ref[...])
    o_ref[...] = _residual_ln(_load_x(x_refs, tm), gate_ref[0], y, g_ref[...], b_ref[...])


def _even_out(h_scan, o_scan, proj, mg, dg, w, x, mods, layer, ln_g, ln_b):
    tm = 512
    npt = NP_ROWS // tm
    row512 = lambda i: (i, 0)
    const2 = lambda i: (0, 0)
    p_spec = pl.BlockSpec((tm, 512), lambda i: (jnp.minimum(i, npt - 1), 0))
    s_spec = pl.BlockSpec((tm, 512), lambda i: (jnp.maximum(i - npt, 0), 0))
    x_specs, x_arrays = _x_operands(x, tm)
    return pl.pallas_call(
        functools.partial(_even_out_kernel, tm=tm, n_x=len(x_arrays)),
        out_shape=jax.ShapeDtypeStruct((T, D), F32),
        grid=(T // tm,),
        in_specs=x_specs + [p_spec, p_spec, s_spec, s_spec, p_spec, p_spec, s_spec, s_spec,
                            pl.BlockSpec((tm, 512), lambda i: (i, 6)),
                            pl.BlockSpec((tm, 512), lambda i: (i, 7)),
                            pl.BlockSpec((1, 512), const2), pl.BlockSpec((1, 512), const2),
                            pl.BlockSpec((D, D), const2),
                            _mod_spec(layer, 2, tm),
                            pl.BlockSpec((1, D), const2), pl.BlockSpec((1, D), const2)],
        out_specs=pl.BlockSpec((tm, D), row512),
        compiler_params=_params(1),
        name="even_out_ln",
    )(*x_arrays, *h_scan, *o_scan, proj, proj, mg, dg, w, mods, ln_g, ln_b)


def _odd_out_kernel(ap_ref, as_ref, w_ref, x_ref, gate_ref, g_ref, b_ref, o_ref, *, tm):
    a = jnp.where(pl.program_id(0) < NP_ROWS // tm, ap_ref[...], as_ref[...])
    y = _mm(_bf(a), w_ref[...])
    o_ref[...] = _residual_ln(x_ref[...], gate_ref[0], y, g_ref[...], b_ref[...])


def _odd_out(a_prompt, a_latent, w, x, mods, layer, ln_g, ln_b):
    tm = 1024
    npt = NP_ROWS // tm
    row = lambda i: (i, 0)
    const2 = lambda i: (0, 0)
    return pl.pallas_call(
        functools.partial(_odd_out_kernel, tm=tm),
        out_shape=jax.ShapeDtypeStruct((T, D), F32),
        grid=(T // tm,),
        in_specs=[pl.BlockSpec((tm, D), lambda i: (jnp.minimum(i, npt - 1), 0)),
                  pl.BlockSpec((tm, D), lambda i: (jnp.maximum(i - npt, 0), 0)),
                  pl.BlockSpec((D, D), const2), pl.BlockSpec((tm, D), row),
                  _mod_spec(layer, 2, tm), pl.BlockSpec((1, D), const2), pl.BlockSpec((1, D), const2)],
        out_specs=pl.BlockSpec((tm, D), row),
        compiler_params=_params(1),
        name="odd_out_ln",
    )(a_prompt, a_latent, w, x, mods, ln_g, ln_b)


def _attn_ctx_kernel(qkv_ref, sink_ref, o_ref):
    ones = jnp.ones((SEQ, HD), BF16)
    for kv in range(KVH):
        k = _bf(qkv_ref[:, HC * HD + kv * HD:HC * HD + (kv + 1) * HD])
        v = _bf(qkv_ref[:, (HC + KVH) * HD + kv * HD:(HC + KVH) * HD + (kv + 1) * HD])
        v1 = jnp.concatenate([v, ones], axis=1)
        heads = range(kv * (HC // KVH), (kv + 1) * (HC // KVH))
        sts = [_mm_nt(k, _bf(qkv_ref[:, h * HD:(h + 1) * HD])) for h in heads]
        ms = [jnp.maximum(jnp.max(st, axis=0, keepdims=True), sink_ref[h]) for st, h in zip(sts, heads)]
        ovs = [_mm_tn(v1, _bf(jnp.exp(st - m))) for st, m in zip(sts, ms)]
        outs = [ov[:HD] / (ov[HD:HD + 1] + jnp.exp(sink_ref[h] - m)) for ov, m, h in zip(ovs, ms, heads)]
        for pr in range(HC // KVH // 2):
            lane0 = (kv * (HC // KVH) + 2 * pr) * HD
            o_ref[:, lane0:lane0 + 2 * HD] = jnp.concatenate(outs[2 * pr:2 * pr + 2], axis=0).T


def _attn_context(qkv, sink):
    return pl.pallas_call(
        _attn_ctx_kernel,
        out_shape=jax.ShapeDtypeStruct((NP_ROWS, HC * HD), F32),
        grid=(BATCH,),
        in_specs=[pl.BlockSpec((SEQ, QKV_W), lambda b: (b, 0)),
                  pl.BlockSpec(memory_space=pltpu.SMEM)],
        out_specs=pl.BlockSpec((SEQ, HC * HD), lambda b: (b, 0)),
        compiler_params=_params(1),
        name="attn_context",
    )(qkv, sink)


def _attn_lat_kernel(q_ref, kp_ref, kc_ref, kn_ref, vp_ref, vc_ref, vn_ref, ck_ref, cv_ref,
                     sink_ref, o_ref, bias_s):
    j = pl.program_id(1)
    nb = DEC_SEQ // QBLOCK
    cc = lax.broadcasted_iota(jnp.int32, (3 * QBLOCK, QBLOCK), 0)
    r = lax.broadcasted_iota(jnp.int32, (3 * QBLOCK, QBLOCK), 1)
    lo = jnp.where(j >= 1, 0, QBLOCK)
    hi = jnp.where(j <= nb - 2, 3 * QBLOCK, 2 * QBLOCK)
    ok = (jnp.abs(QBLOCK + r - cc) <= WINDOW) & (cc >= lo) & (cc < hi)
    bias_s[...] = jnp.where(ok, 0.0, -jnp.inf)
    for kv in range(KVH):
        ks = slice(kv * HD, (kv + 1) * HD)
        k_c = _bf(ck_ref[0, 0, kv])
        k_b = _bf(jnp.concatenate([kp_ref[:, ks], kc_ref[:, ks], kn_ref[:, ks]], axis=0))
        v_c = _bf(cv_ref[0, 0, kv])
        v_b = _bf(jnp.concatenate([vp_ref[:, ks], vc_ref[:, ks], vn_ref[:, ks]], axis=0))
        v1_c = jnp.concatenate([v_c, jnp.ones_like(v_c)], axis=1)
        v1_b = jnp.concatenate([v_b, jnp.ones_like(v_b)], axis=1)
        outs = []
        for h in range(kv * (HC // KVH), (kv + 1) * (HC // KVH)):
            sink = sink_ref[h]
            q = _bf(q_ref[:, h * HD:(h + 1) * HD])
            s_ctx = _mm_nt(k_c, q)
            m_c = jnp.maximum(jnp.max(s_ctx, axis=0, keepdims=True), sink)
            ov_c = _mm_tn(v1_c, _bf(jnp.exp(s_ctx - m_c)))
            s_loc = _mm_nt(k_b, q) + bias_s[...]
            m = jnp.maximum(m_c, jnp.max(s_loc, axis=0, keepdims=True))
            ov = ov_c * jnp.exp(m_c - m) + _mm_tn(v1_b, _bf(jnp.exp(s_loc - m)))
            outs.append(ov[:HD] / (ov[HD:HD + 1] + jnp.exp(sink - m)))
        for pr in range(HC // KVH // 2):
            lane0 = (kv * (HC // KVH) + 2 * pr) * HD
            o_ref[:, lane0:lane0 + 2 * HD] = jnp.concatenate(outs[2 * pr:2 * pr + 2], axis=0).T


def _attn_latent(qkv, cache_k, cache_v, sink, o):
    nb = DEC_SEQ // QBLOCK
    base = NP_ROWS // QBLOCK
    blk = lambda b, j: base + b * nb + j
    prev = lambda b, j: base + b * nb + jnp.maximum(j - 1, 0)
    nxt = lambda b, j: base + b * nb + jnp.minimum(j + 1, nb - 1)
    kcol, vcol = HC * HD // 256, HC * HD // 256 + 1
    cache_spec = pl.BlockSpec((1, 1, KVH, PAST_LEN, HD), lambda b, j: (b, o, 0, 0, 0))
    return pl.pallas_call(
        _attn_lat_kernel,
        out_shape=jax.ShapeDtypeStruct((NS_ROWS, HC * HD), F32),
        grid=(DEC_BATCH, nb),
        in_specs=[pl.BlockSpec((QBLOCK, HC * HD), lambda b, j: (blk(b, j), 0)),
                  pl.BlockSpec((QBLOCK, 256), lambda b, j: (prev(b, j), kcol)),
                  pl.BlockSpec((QBLOCK, 256), lambda b, j: (blk(b, j), kcol)),
                  pl.BlockSpec((QBLOCK, 256), lambda b, j: (nxt(b, j), kcol)),
                  pl.BlockSpec((QBLOCK, 256), lambda b, j: (prev(b, j), vcol)),
                  pl.BlockSpec((QBLOCK, 256), lambda b, j: (blk(b, j), vcol)),
                  pl.BlockSpec((QBLOCK, 256), lambda b, j: (nxt(b, j), vcol)),
                  cache_spec, cache_spec,
                  pl.BlockSpec(memory_space=pltpu.SMEM)],
        out_specs=pl.BlockSpec((QBLOCK, HC * HD), lambda b, j: (b * nb + j, 0)),
        scratch_shapes=[pltpu.VMEM((3 * QBLOCK, QBLOCK), F32)],
        compiler_params=_params(2),
        name="attn_latent",
    )(qkv, qkv, qkv, qkv, qkv, qkv, qkv, cache_k, cache_v, sink)


def _route(lg):
    lane = lax.broadcasted_iota(jnp.int32, lg.shape, 1)
    neg = -jnp.inf
    big = 1 << 20
    is_grp = jnp.logical_and(lane >= N_EXPERTS, lane < N_EXPERTS + N_GROUPS)
    mg = jnp.max(jnp.where(is_grp, lg, neg), axis=1, keepdims=True)
    g_lane = jnp.min(jnp.where(jnp.logical_and(is_grp, lg == mg), lane, big), axis=1, keepdims=True)
    g_w = 1.0 / jnp.sum(jnp.where(is_grp, jnp.exp(lg - mg), 0.0), axis=1, keepdims=True)
    g_idx = g_lane - N_EXPERTS
    in_grp = jnp.logical_and(lane >= g_idx * EPG, lane < (g_idx + 1) * EPG)
    v1 = jnp.max(jnp.where(in_grp, lg, neg), axis=1, keepdims=True)
    i1 = jnp.min(jnp.where(jnp.logical_and(in_grp, lg == v1), lane, big), axis=1, keepdims=True)
    rest = jnp.logical_and(in_grp, lane != i1)
    v2 = jnp.max(jnp.where(rest, lg, neg), axis=1, keepdims=True)
    i2 = jnp.min(jnp.where(jnp.logical_and(rest, lg == v2), lane, big), axis=1, keepdims=True)
    e2 = jnp.exp(v2 - v1)
    p1 = 1.0 / (1.0 + e2)
    p2 = e2 / (1.0 + e2)
    return jnp.where(lane == 0, i1.astype(F32),
                     jnp.where(lane == 1, i2.astype(F32),
                               jnp.where(lane == 2, p1 * g_w, jnp.where(lane == 3, p2 * g_w, 0.0))))


def _moe_dense_kernel(x_ref, sh_ref, sc_ref, wr_ref, br_ref, wg_ref, wu_ref, wd_ref, gate_ref, g_ref, b_ref,
                      *out_and_scratch, tm):
    *o_refs, xm_s, meta_s, acc_s = out_and_scratch
    grp = pl.program_id(1)

    @pl.when(grp == 0)
    def _():
        xm = x_ref[...] * (1.0 + sc_ref[0]) + sh_ref[0]
        meta_s[...] = _route(_mm(_lhs3(xm), _rhs3(wr_ref[...])) + br_ref[...])
        xm_s[...] = _bf(xm)
        acc_s[...] = jnp.zeros_like(acc_s)

    xm = xm_s[...]
    meta = meta_s[...]
    i1 = meta[:, 0:1].astype(jnp.int32)
    i2 = meta[:, 1:2].astype(jnp.int32)
    w1 = meta[:, 2:3]
    w2 = meta[:, 3:4]
    hid = []
    for e in range(EPG):
        eid = grp * EPG + e
        gate = jnp.where(i1 == eid, w1, 0.0) + jnp.where(i2 == eid, w2, 0.0)
        a = _mm(xm, wg_ref[0, e])
        u = _mm(xm, wu_ref[0, e])
        hid.append(_bf((a * _sigmoid(a)) * u * gate))
    acc_s[...] += _mm(jnp.concatenate(hid, axis=1), wd_ref[0])

    def result():
        return _residual_ln(x_ref[...], gate_ref[0], acc_s[...], g_ref[...], b_ref[...])

    last = grp == N_GROUPS - 1
    if len(o_refs) == 1:
        @pl.when(last)
        def _():
            o_refs[0][...] = result()
    else:
        is_p = pl.program_id(0) < NP_ROWS // tm

        @pl.when(jnp.logical_and(last, is_p))
        def _():
            o_refs[0][...] = result()

        @pl.when(jnp.logical_and(last, jnp.logical_not(is_p)))
        def _():
            o_refs[1][...] = result()


def _moe_dense(x, mods, layer, w_r, b_r, wg, wu, wd, ln_g, ln_b, split_out=False):
    tm = 1024
    npt = NP_ROWS // tm
    row = lambda i, g: (i, 0)
    const2 = lambda i, g: (0, 0)
    if split_out:
        half = jax.ShapeDtypeStruct((NP_ROWS, D), F32)
        out_shape = (half, half)
        out_specs = (pl.BlockSpec((tm, D), lambda i, g: (jnp.minimum(i, npt - 1), 0)),
                     pl.BlockSpec((tm, D), lambda i, g: (jnp.maximum(i - npt, 0), 0)))
    else:
        out_shape = jax.ShapeDtypeStruct((T, D), F32)
        out_specs = pl.BlockSpec((tm, D), row)
    return pl.pallas_call(
        functools.partial(_moe_dense_kernel, tm=tm),
        out_shape=out_shape,
        grid=(T // tm, N_GROUPS),
        in_specs=[pl.BlockSpec((tm, D), row), _mod_spec(layer, 3, tm), _mod_spec(layer, 4, tm),
                  pl.BlockSpec((D, 128), const2), pl.BlockSpec((1, 128), const2),
                  pl.BlockSpec((1, EPG, D, EXPERT_FF), lambda i, g: (g, 0, 0, 0)),
                  pl.BlockSpec((1, EPG, D, EXPERT_FF), lambda i, g: (g, 0, 0, 0)),
                  pl.BlockSpec((1, EPG * EXPERT_FF, D), lambda i, g: (g, 0, 0)),
                  _mod_spec(layer, 5, tm), pl.BlockSpec((1, D), const2), pl.BlockSpec((1, D), const2)],
        out_specs=out_specs,
        scratch_shapes=[pltpu.VMEM((tm, D), BF16), pltpu.VMEM((tm, 128), F32), pltpu.VMEM((tm, D), F32)],
        compiler_params=_params(2, VMEM_LIMIT + (2 * tm * D * 4 if split_out else 0)),
        name="moe_dense",
    )(x, mods, mods, w_r, b_r, wg, wu, wd, mods, ln_g, ln_b)


def _permute_even_w(w):
    a_end = 4 * HA * DH
    g_end = a_end + 4 * HA
    c_end = g_end + 3 * HB * DH
    z_end = c_end + HB * DH
    small = jnp.concatenate([w[:, a_end:g_end], w[:, z_end:]], axis=1)
    pad = jnp.zeros((w.shape[0], 128 - small.shape[1]), w.dtype)
    return jnp.concatenate([w[:, g_end:c_end], w[:, :a_end], w[:, c_end:z_end], small, pad], axis=1)


def _lane_row(vals, offset):
    return jnp.zeros((1, 128), F32).at[0, offset:offset + vals.shape[0]].set(vals.astype(F32))


def _pair_rows(vals, offset):
    cols = jnp.zeros((N_GATE_COLS,), F32).at[offset:offset + vals.shape[0]].set(vals.astype(F32))
    return jnp.repeat(cols.reshape(N_GATE_COLS // 2, 2), CHUNK, axis=1)


def kernel(x_prompt, x_sample, c, c_ctx, state_mlstm_c, state_mlstm_n, state_mlstm_m, state_delta, cache_k, cache_v, w_mod, b_mod, ln_g, ln_b, w_in_even, mlstm_gate_b, mlstm_norm_g, delta_conv_w, delta_a_log, delta_dt_bias, delta_norm_g, w_out_even, w_qkv_odd, attn_sink, w_out_odd, w_grp, b_grp, w_erouter, b_erouter, w_gate, w_up, w_down):
    x = (x_prompt.reshape(NP_ROWS, D), x_sample.reshape(NS_ROWS, D))
    cvecs = jnp.concatenate([c_ctx[None, :], c, jnp.zeros((N_MOD_ROWS - 1 - DEC_BATCH, D), F32)], axis=0)
    mods = _modulation(cvecs, w_mod, b_mod)
    tables = _rope_tables()
    m0_all = jnp.broadcast_to(state_mlstm_m.reshape(DEC_BATCH, N_EVEN, 2 * HA, 1), (DEC_BATCH, N_EVEN, 2 * HA, 128))

    out_mn, out_mm = [], []
    new_mc = jnp.zeros((BATCH, N_EVEN, 2, HA, DH, DH), F32)
    new_ds = jnp.zeros((BATCH, N_EVEN, 2, HB, DH, DH), F32)
    new_k = jnp.zeros((BATCH, N_ODD, KVH, SEQ, HD), F32)
    new_v = jnp.zeros((BATCH, N_ODD, KVH, SEQ, HD), F32)
    for l in range(DEPTH):
        if l % 2 == 0:
            e = l // 2
            proj = _even_proj(x, mods, l, _permute_even_w(_bf(w_in_even[e])))
            gates_t2 = (proj[:, EVEN_W - 128:EVEN_W - 128 + N_GATE_COLS]
                        .reshape(T // CHUNK, CHUNK, N_GATE_COLS).transpose(0, 2, 1)
                        .reshape(T // CHUNK, N_GATE_COLS // 2, 2 * CHUNK))
            gb = mlstm_gate_b[e].reshape(-1)
            *h_scan, new_mc, mn, mm = _mlstm_scan(proj, gates_t2, _lane_row(gb, 0), _pair_rows(gb, 0),
                                                  state_mlstm_c, state_mlstm_n, m0_all, new_mc, e)
            qkv = _delta_prep(proj, delta_conv_w[e])
            dtb = delta_dt_bias[e].reshape(-1)
            nea = -jnp.exp(delta_a_log[e].astype(F32)).reshape(-1)
            *o_scan, new_ds = _delta_scan(qkv, proj, gates_t2, _lane_row(dtb, DECAY_COL0),
                                          _pair_rows(dtb, DECAY_COL0), _lane_row(nea, DECAY_COL0),
                                          _pair_rows(nea, DECAY_COL0), state_delta, new_ds, e)
            x = _even_out(h_scan, o_scan, proj, mlstm_norm_g[e][None, :], delta_norm_g[e][None, :],
                          _bf(w_out_even[e]), x, mods, l, ln_g[l, 0][None, :], ln_b[l, 0][None, :])
            out_mn.append(mn)
            out_mm.append(mm[:, :, 0].reshape(BATCH, 2, HA))
        else:
            o = l // 2
            qkv, new_k, new_v = _odd_proj(x, mods, l, _bf(w_qkv_odd[o]), tables, new_k, new_v, o)
            a_p = _attn_context(qkv, attn_sink[o])
            a_s = _attn_latent(qkv, cache_k, cache_v, attn_sink[o], o)
            x = _odd_out(a_p, a_s, _bf(w_out_odd[o]), x, mods, l, ln_g[l, 0][None, :], ln_b[l, 0][None, :])
        w_r = jnp.concatenate([w_erouter[l].transpose(1, 0, 2).reshape(D, N_EXPERTS), w_grp[l],
                               jnp.zeros((D, 128 - N_EXPERTS - N_GROUPS), F32)], axis=1)
        b_r = jnp.concatenate([b_erouter[l].reshape(-1), b_grp[l],
                               jnp.zeros((128 - N_EXPERTS - N_GROUPS,), F32)])[None, :]
        x = _moe_dense(x, mods, l, w_r, b_r,
                       _bf(w_gate[l]).reshape(N_GROUPS, EPG, D, EXPERT_FF),
                       _bf(w_up[l]).reshape(N_GROUPS, EPG, D, EXPERT_FF),
                       _bf(w_down[l]).reshape(N_GROUPS, EPG * EXPERT_FF, D),
                       ln_g[l, 1][None, :], ln_b[l, 1][None, :], split_out=(l == DEPTH - 1))
    return (x[0].reshape(BATCH, SEQ, D), x[1].reshape(DEC_BATCH, DEC_SEQ, D),
            new_mc, jnp.stack(out_mn, 1), jnp.stack(out_mm, 1), new_ds,
            new_k, new_v)
```
